```python
import math
import jax, jax.numpy as jnp
from jax import lax
import numpy as np

D_MODEL = 1024
BATCH = 8
SEQ = 8192
DEPTH = 1

PLE_DIM = 256
N_HEADS = 4
NOPE_DIM = 128
ROPE_DIM = 64
V_DIM = 128
QK_DIM = NOPE_DIM + ROPE_DIM
Q_LORA = 256
KV_LORA = 128
ATTN_WIDTH = N_HEADS * V_DIM
CONV_WIDTH = D_MODEL - ATTN_WIDTH
CONV_K = 3
ROPE_THETA = 10000.0
RMS_EPS = 1e-6
Q_BLOCK = 128
NEG_INF = -1e30
IN_WIDTHS = (Q_LORA, KV_LORA, ROPE_DIM, ATTN_WIDTH,
             CONV_WIDTH, CONV_WIDTH, CONV_WIDTH, CONV_WIDTH)
IN_TOTAL = Q_LORA + KV_LORA + ROPE_DIM + ATTN_WIDTH + 4 * CONV_WIDTH

kernel_name = "hymba_mla_shortconv_ple_block"


def rms_norm(x, g):
    xf = x.astype(jnp.float32)
    y = xf * lax.rsqrt(jnp.mean(xf * xf, axis=-1, keepdims=True) + RMS_EPS)
    return (y * g.astype(jnp.float32)).astype(x.dtype)


def rope_cos_sin(positions):
    inv_freq = 1.0 / (ROPE_THETA ** (jnp.arange(0, ROPE_DIM, 2, dtype=jnp.float32) / ROPE_DIM))
    ang = positions.astype(jnp.float32)[..., None] * inv_freq
    return jnp.cos(ang)[:, :, None, :], jnp.sin(ang)[:, :, None, :]


def apply_rope(t, cos, sin):
    tf = t.astype(jnp.float32)
    t1, t2 = tf[..., : ROPE_DIM // 2], tf[..., ROPE_DIM // 2:]
    out = jnp.concatenate([t1 * cos - t2 * sin, t2 * cos + t1 * sin], axis=-1)
    return out.astype(t.dtype)


def causal_block_attention(q, k, v):
    B, S, H, D = q.shape
    nb = S // Q_BLOCK
    scale = 1.0 / math.sqrt(D)
    q_blocks = q.reshape(B, nb, Q_BLOCK, H, D).transpose(1, 0, 2, 3, 4)
    k_pos = jnp.arange(S)

    def one_block(args):
        qb, bi = args
        s = jnp.einsum('bqhd,bkhd->bhqk', qb, k, preferred_element_type=jnp.float32) * scale
        q_pos = bi * Q_BLOCK + jnp.arange(Q_BLOCK)
        mask = k_pos[None, :] <= q_pos[:, None]
        s = jnp.where(mask[None, None], s, NEG_INF)
        pr = jax.nn.softmax(s, axis=-1).astype(v.dtype)
        return jnp.einsum('bhqk,bkhd->bqhd', pr, v)

    out = lax.map(one_block, (q_blocks, jnp.arange(nb)))
    return out.transpose(1, 0, 2, 3, 4).reshape(B, S, H, v.shape[-1])


def causal_depthwise_conv(u, w):
    C = u.shape[-1]
    return lax.conv_general_dilated(
        u, w[:, None, :].astype(u.dtype), window_strides=(1,),
        padding=[(CONV_K - 1, 0)], dimension_numbers=('NWC', 'WIO', 'NWC'),
        feature_group_count=C)


def hybrid_layer(x, p_i, cos, sin, g_in, w_in, g_cq, w_uq, g_ckv, w_ukv, g_q, g_k,
                 conv_w, g_oa, g_oc, w_o, w_pl, w_plg, g_pl):
    B, S, _ = x.shape
    h = rms_norm(x, g_in)
    proj = h @ w_in
    splits, acc = [], 0
    for wdt in IN_WIDTHS[:-1]:
        acc += wdt
        splits.append(acc)
    c_q, c_kv, k_pe, z_a, cb, cc, cx, z_c = jnp.split(proj, splits, axis=-1)

    q = (rms_norm(c_q, g_cq) @ w_uq).reshape(B, S, N_HEADS, QK_DIM)
    kv = (rms_norm(c_kv, g_ckv) @ w_ukv).reshape(B, S, N_HEADS, NOPE_DIM + V_DIM)
    k_nope, v = kv[..., :NOPE_DIM], kv[..., NOPE_DIM:]
    k = jnp.concatenate(
        [k_nope, jnp.broadcast_to(k_pe[:, :, None, :], (B, S, N_HEADS, ROPE_DIM))], axis=-1)
    q = rms_norm(q, g_q)
    k = rms_norm(k, g_k)
    q = jnp.concatenate([q[..., :NOPE_DIM], apply_rope(q[..., NOPE_DIM:], cos, sin)], axis=-1)
    k = jnp.concatenate([k[..., :NOPE_DIM], apply_rope(k[..., NOPE_DIM:], cos, sin)], axis=-1)
    o_attn = causal_block_attention(q, k, v).reshape(B, S, ATTN_WIDTH)
    y_attn = rms_norm(o_attn * jax.nn.silu(z_a), g_oa)

    u = causal_depthwise_conv(cc * cx, conv_w)
    y_conv = rms_norm(cb * u * jax.nn.silu(z_c), g_oc)

    x = x + jnp.concatenate([y_attn, y_conv], axis=-1) @ w_o

    gate = jax.nn.sigmoid(rms_norm(x, g_pl) @ w_plg)
    return x + gate * (p_i @ w_pl)


def _fwd_setup_inputs(seed: int = 0) -> dict:
    key = jax.random.key(seed)
    ks = jax.random.split(key, 20)
    f32 = jnp.float32

    def nrm(k, shape, fan_in):
        return jax.random.normal(k, shape, f32) * (fan_in ** -0.5)

    def gain(k, n):
        return 1.0 + 0.02 * jax.random.normal(k, (DEPTH, n), f32)

    x = jax.random.normal(ks[0], (BATCH, SEQ, D_MODEL), f32)
    p = jax.random.normal(ks[1], (DEPTH, BATCH, SEQ, PLE_DIM), f32)
    positions = jnp.broadcast_to(jnp.arange(SEQ, dtype=jnp.int32)[None, :], (BATCH, SEQ))
    return {
        "x": x,
        "p": p,
        "positions": positions,
        "g_in": gain(ks[2], D_MODEL),
        "w_in": nrm(ks[3], (DEPTH, D_MODEL, IN_TOTAL), D_MODEL),
        "g_cq": gain(ks[4], Q_LORA),
        "w_uq": nrm(ks[5], (DEPTH, Q_LORA, N_HEADS * QK_DIM), Q_LORA),
        "g_ckv": gain(ks[6], KV_LORA),
        "w_ukv": nrm(ks[7], (DEPTH, KV_LORA, N_HEADS * (NOPE_DIM + V_DIM)), KV_LORA),
        "g_q": gain(ks[8], QK_DIM),
        "g_k": gain(ks[9], QK_DIM),
        "conv_w": nrm(ks[10], (DEPTH, CONV_K, CONV_WIDTH), CONV_K),
        "g_oa": gain(ks[11], ATTN_WIDTH),
        "g_oc": gain(ks[12], CONV_WIDTH),
        "w_o": nrm(ks[13], (DEPTH, D_MODEL, D_MODEL), D_MODEL),
        "w_pl": nrm(ks[14], (DEPTH, PLE_DIM, D_MODEL), PLE_DIM),
        "w_plg": nrm(ks[15], (DEPTH, D_MODEL, D_MODEL), D_MODEL),
        "g_pl": gain(ks[16], D_MODEL),
    }


def _fwd_reference(x, p, positions, g_in, w_in, g_cq, w_uq, g_ckv, w_ukv, g_q, g_k,
              conv_w, g_oa, g_oc, w_o, w_pl, w_plg, g_pl):
    cos, sin = rope_cos_sin(positions)
    h = x
    for i in range(DEPTH):
        h = hybrid_layer(h, p[i], cos, sin, g_in[i], w_in[i], g_cq[i], w_uq[i],
                         g_ckv[i], w_ukv[i], g_q[i], g_k[i], conv_w[i], g_oa[i],
                         g_oc[i], w_o[i], w_pl[i], w_plg[i], g_pl[i])
    return h.astype(x.dtype)


import jax as _jax
import jax.numpy as _jnp

TWIN_FORMAT = 'train_step'
FWD_PARAMS = ['x', 'p', 'positions', 'g_in', 'w_in', 'g_cq', 'w_uq', 'g_ckv', 'w_ukv', 'g_q', 'g_k', 'conv_w', 'g_oa', 'g_oc', 'w_o', 'w_pl', 'w_plg', 'g_pl']
TWIN_WEIGHTS = ['g_in', 'w_in', 'g_cq', 'w_uq', 'g_ckv', 'w_ukv', 'g_q', 'g_k', 'conv_w', 'g_oa', 'g_oc', 'w_o', 'w_pl', 'w_plg', 'g_pl']
TWIN_DIFF_INPUT = 'x'
TWIN_INPUTS = ['x', 'p', 'positions', 'g_in', 'w_in', 'g_cq', 'w_uq', 'g_ckv', 'w_ukv', 'g_q', 'g_k', 'conv_w', 'g_oa', 'g_oc', 'w_o', 'w_pl', 'w_plg', 'g_pl', 'loss_target', 'm_g_in', 'm_w_in', 'm_g_cq', 'm_w_uq', 'm_g_ckv', 'm_w_ukv', 'm_g_q', 'm_g_k', 'm_conv_w', 'm_g_oa', 'm_g_oc', 'm_w_o', 'm_w_pl', 'm_w_plg', 'm_g_pl', 'v_g_in', 'v_w_in', 'v_g_cq', 'v_w_uq', 'v_g_ckv', 'v_w_ukv', 'v_g_q', 'v_g_k', 'v_conv_w', 'v_g_oa', 'v_g_oc', 'v_w_o', 'v_w_pl', 'v_w_plg', 'v_g_pl']
TWIN_OUTPUTS = ['loss', 'grad_x', 'grad_g_in', 'grad_w_in', 'grad_g_cq', 'grad_w_uq', 'grad_g_ckv', 'grad_w_ukv', 'grad_g_q', 'grad_g_k', 'grad_conv_w', 'grad_g_oa', 'grad_g_oc', 'grad_w_o', 'grad_w_pl', 'grad_w_plg', 'grad_g_pl', 'delta_g_in', 'delta_w_in', 'delta_g_cq', 'delta_w_uq', 'delta_g_ckv', 'delta_w_ukv', 'delta_g_q', 'delta_g_k', 'delta_conv_w', 'delta_g_oa', 'delta_g_oc', 'delta_w_o', 'delta_w_pl', 'delta_w_plg', 'delta_g_pl', 'new_m_g_in', 'new_m_w_in', 'new_m_g_cq', 'new_m_w_uq', 'new_m_g_ckv', 'new_m_w_ukv', 'new_m_g_q', 'new_m_g_k', 'new_m_conv_w', 'new_m_g_oa', 'new_m_g_oc', 'new_m_w_o', 'new_m_w_pl', 'new_m_w_plg', 'new_m_g_pl', 'new_v_g_in', 'new_v_w_in', 'new_v_g_cq', 'new_v_w_uq', 'new_v_g_ckv', 'new_v_w_ukv', 'new_v_g_q', 'new_v_g_k', 'new_v_conv_w', 'new_v_g_oa', 'new_v_g_oc', 'new_v_w_o', 'new_v_w_pl', 'new_v_w_plg', 'new_v_g_pl']
TWIN_LEAF_KINDS = {'loss': 'loss', 'grad_x': 'grad_x', 'grad_g_in': 'grad_w', 'grad_w_in': 'grad_w', 'grad_g_cq': 'grad_w', 'grad_w_uq': 'grad_w', 'grad_g_ckv': 'grad_w', 'grad_w_ukv': 'grad_w', 'grad_g_q': 'grad_w', 'grad_g_k': 'grad_w', 'grad_conv_w': 'grad_w', 'grad_g_oa': 'grad_w', 'grad_g_oc': 'grad_w', 'grad_w_o': 'grad_w', 'grad_w_pl': 'grad_w', 'grad_w_plg': 'grad_w', 'grad_g_pl': 'grad_w', 'delta_g_in': 'delta_w', 'delta_w_in': 'delta_w', 'delta_g_cq': 'delta_w', 'delta_w_uq': 'delta_w', 'delta_g_ckv': 'delta_w', 'delta_w_ukv': 'delta_w', 'delta_g_q': 'delta_w', 'delta_g_k': 'delta_w', 'delta_conv_w': 'delta_w', 'delta_g_oa': 'delta_w', 'delta_g_oc': 'delta_w', 'delta_w_o': 'delta_w', 'delta_w_pl': 'delta_w', 'delta_w_plg': 'delta_w', 'delta_g_pl': 'delta_w', 'new_m_g_in': 'new_m', 'new_m_w_in': 'new_m', 'new_m_g_cq': 'new_m', 'new_m_w_uq': 'new_m', 'new_m_g_ckv': 'new_m', 'new_m_w_ukv': 'new_m', 'new_m_g_q': 'new_m', 'new_m_g_k': 'new_m', 'new_m_conv_w': 'new_m', 'new_m_g_oa': 'new_m', 'new_m_g_oc': 'new_m', 'new_m_w_o': 'new_m', 'new_m_w_pl': 'new_m', 'new_m_w_plg': 'new_m', 'new_m_g_pl': 'new_m', 'new_v_g_in': 'new_v', 'new_v_w_in': 'new_v', 'new_v_g_cq': 'new_v', 'new_v_w_uq': 'new_v', 'new_v_g_ckv': 'new_v', 'new_v_w_ukv': 'new_v', 'new_v_g_q': 'new_v', 'new_v_g_k': 'new_v', 'new_v_conv_w': 'new_v', 'new_v_g_oa': 'new_v', 'new_v_g_oc': 'new_v', 'new_v_w_o': 'new_v', 'new_v_w_pl': 'new_v', 'new_v_w_plg': 'new_v', 'new_v_g_pl': 'new_v'}


def _forward(args):
    return _fwd_reference(*[args[k] for k in FWD_PARAMS])


def _output_shape():
    def fwd():
        inp = _fwd_setup_inputs(0)
        return _fwd_reference(*[inp[k] for k in FWD_PARAMS])
    out = _jax.eval_shape(fwd)
    return out.shape, out.dtype

N_MICROBATCH = 1
ADAM_LR = 0.001
ADAM_B1 = 0.9
ADAM_B2 = 0.999
ADAM_EPS = 1e-08
ADAM_WD = 0.01
ADAM_STEP = 10
PER_EXAMPLE_BATCH_AXIS = {'x': 0, 'p': 1, 'positions': 0, 'loss_target': 0}
SHARED_INPUTS = []
_WEIGHT_DTYPES = {'g_in': _jnp.float32, 'w_in': _jnp.float32, 'g_cq': _jnp.float32, 'w_uq': _jnp.float32, 'g_ckv': _jnp.float32, 'w_ukv': _jnp.float32, 'g_q': _jnp.float32, 'g_k': _jnp.float32, 'conv_w': _jnp.float32, 'g_oa': _jnp.float32, 'g_oc': _jnp.float32, 'w_o': _jnp.float32, 'w_pl': _jnp.float32, 'w_plg': _jnp.float32, 'g_pl': _jnp.float32}
MOMENT_SCALE = {'g_in': 9.551318e-01, 'w_in': 5.422196e-01, 'g_cq': 8.557548e-01, 'w_uq': 4.740982e-01, 'g_ckv': 2.165499e+00, 'w_ukv': 6.253520e-01, 'g_q': 8.801523e-01, 'g_k': 8.871050e-01, 'conv_w': 1.274201e+00, 'g_oa': 6.475482e+01, 'g_oc': 7.796006e+01, 'w_o': 1.771638e+00, 'w_pl': 8.827274e-01, 'w_plg': 1.364202e-01, 'g_pl': 1.932724e+00}


def _to_microbatches(a, axis):
    t = _jnp.moveaxis(a, axis, 0)
    t = t.reshape((N_MICROBATCH, t.shape[0] // N_MICROBATCH) + t.shape[1:])
    return _jnp.moveaxis(t, 1, axis + 1)


def setup_inputs(seed: int = 0) -> dict:
    inp = _fwd_setup_inputs(seed)
    key = _jax.random.fold_in(_jax.random.key(seed), 7919)
    shape, _ = _output_shape()
    out = dict(inp)
    out["loss_target"] = _jax.random.normal(_jax.random.fold_in(key, 0), shape, _jnp.float32)
    for i, name in enumerate(TWIN_WEIGHTS):
        w = inp[name].astype(_jnp.float32)
        if MOMENT_SCALE is None:
            s = _jnp.sqrt(_jnp.mean(_jnp.square(w)) + 1e-30)
        else:
            s = MOMENT_SCALE[name]
        km, kv = _jax.random.split(_jax.random.fold_in(key, i + 1))
        out[name] = w
        out["m_" + name] = s * _jax.random.normal(km, w.shape, _jnp.float32)
        out["v_" + name] = (s * s) * _jax.random.uniform(kv, w.shape, _jnp.float32, 0.5, 1.5)
    if N_MICROBATCH > 1:
        for name, axis in PER_EXAMPLE_BATCH_AXIS.items():
            out[name] = _to_microbatches(out[name], axis)
    return {'x': out['x'], 'p': out['p'], 'positions': out['positions'], 'g_in': out['g_in'], 'w_in': out['w_in'], 'g_cq': out['g_cq'], 'w_uq': out['w_uq'], 'g_ckv': out['g_ckv'], 'w_ukv': out['w_ukv'], 'g_q': out['g_q'], 'g_k': out['g_k'], 'conv_w': out['conv_w'], 'g_oa': out['g_oa'], 'g_oc': out['g_oc'], 'w_o': out['w_o'], 'w_pl': out['w_pl'], 'w_plg': out['w_plg'], 'g_pl': out['g_pl'], 'loss_target': out['loss_target'], 'm_g_in': out['m_g_in'], 'm_w_in': out['m_w_in'], 'm_g_cq': out['m_g_cq'], 'm_w_uq': out['m_w_uq'], 'm_g_ckv': out['m_g_ckv'], 'm_w_ukv': out['m_w_ukv'], 'm_g_q': out['m_g_q'], 'm_g_k': out['m_g_k'], 'm_conv_w': out['m_conv_w'], 'm_g_oa': out['m_g_oa'], 'm_g_oc': out['m_g_oc'], 'm_w_o': out['m_w_o'], 'm_w_pl': out['m_w_pl'], 'm_w_plg': out['m_w_plg'], 'm_g_pl': out['m_g_pl'], 'v_g_in': out['v_g_in'], 'v_w_in': out['v_w_in'], 'v_g_cq': out['v_g_cq'], 'v_w_uq': out['v_w_uq'], 'v_g_ckv': out['v_g_ckv'], 'v_w_ukv': out['v_w_ukv'], 'v_g_q': out['v_g_q'], 'v_g_k': out['v_g_k'], 'v_conv_w': out['v_conv_w'], 'v_g_oa': out['v_g_oa'], 'v_g_oc': out['v_g_oc'], 'v_w_o': out['v_w_o'], 'v_w_pl': out['v_w_pl'], 'v_w_plg': out['v_w_plg'], 'v_g_pl': out['v_g_pl']}


def _loss(weights, diff, rest, loss_target):
    with _jax.named_scope("forward"):
        args = {**rest, TWIN_DIFF_INPUT: diff, **{k: w.astype(_WEIGHT_DTYPES[k]) for k, w in weights.items()}}
        y = _forward(args)
    with _jax.named_scope("loss_head"):
        err = _jnp.square(y.astype(_jnp.float32) - loss_target)
        return 0.5 * _jnp.sum(_jnp.mean(err, axis=-1)) if err.ndim else 0.5 * err


def _adamw(w, g, m, v):
    m = ADAM_B1 * m + (1.0 - ADAM_B1) * g
    v = ADAM_B2 * v + (1.0 - ADAM_B2) * _jnp.square(g)
    m_hat = m / (1.0 - ADAM_B1 ** ADAM_STEP)
    v_hat = v / (1.0 - ADAM_B2 ** ADAM_STEP)
    delta = -ADAM_LR * (m_hat / (_jnp.sqrt(v_hat) + ADAM_EPS) + ADAM_WD * w)
    return delta, m, v


def reference(x, p, positions, g_in, w_in, g_cq, w_uq, g_ckv, w_ukv, g_q, g_k, conv_w, g_oa, g_oc, w_o, w_pl, w_plg, g_pl, loss_target, m_g_in, m_w_in, m_g_cq, m_w_uq, m_g_ckv, m_w_ukv, m_g_q, m_g_k, m_conv_w, m_g_oa, m_g_oc, m_w_o, m_w_pl, m_w_plg, m_g_pl, v_g_in, v_w_in, v_g_cq, v_w_uq, v_g_ckv, v_w_ukv, v_g_q, v_g_k, v_conv_w, v_g_oa, v_g_oc, v_w_o, v_w_pl, v_w_plg, v_g_pl):
    given = dict(x=x, p=p, positions=positions, g_in=g_in, w_in=w_in, g_cq=g_cq, w_uq=w_uq, g_ckv=g_ckv, w_ukv=w_ukv, g_q=g_q, g_k=g_k, conv_w=conv_w, g_oa=g_oa, g_oc=g_oc, w_o=w_o, w_pl=w_pl, w_plg=w_plg, g_pl=g_pl, loss_target=loss_target, m_g_in=m_g_in, m_w_in=m_w_in, m_g_cq=m_g_cq, m_w_uq=m_w_uq, m_g_ckv=m_g_ckv, m_w_ukv=m_w_ukv, m_g_q=m_g_q, m_g_k=m_g_k, m_conv_w=m_conv_w, m_g_oa=m_g_oa, m_g_oc=m_g_oc, m_w_o=m_w_o, m_w_pl=m_w_pl, m_w_plg=m_w_plg, m_g_pl=m_g_pl, v_g_in=v_g_in, v_w_in=v_w_in, v_g_cq=v_g_cq, v_w_uq=v_w_uq, v_g_ckv=v_g_ckv, v_w_ukv=v_w_ukv, v_g_q=v_g_q, v_g_k=v_g_k, v_conv_w=v_conv_w, v_g_oa=v_g_oa, v_g_oc=v_g_oc, v_w_o=v_w_o, v_w_pl=v_w_pl, v_w_plg=v_w_plg, v_g_pl=v_g_pl)
    weights = {n: given[n] for n in TWIN_WEIGHTS}
    shared = {n: given[n] for n in SHARED_INPUTS}
    per_example = {n: given[n] for n in ['x', 'p', 'positions']}
    grad_fn = _jax.value_and_grad(_loss, argnums=(0, 1))

    def one_microbatch(ex, loss_target):
        ex = dict(ex)
        diff = ex.pop(TWIN_DIFF_INPUT)
        return grad_fn(weights, diff, {**shared, **ex}, loss_target)

    if N_MICROBATCH == 1:
        loss, (grad_w, grad_x) = one_microbatch(per_example, given["loss_target"])
    else:
        def body(carry, xs):
            loss_sum, grad_sum = carry
            l_k, (gw_k, gx_k) = one_microbatch(xs[0], xs[1])
            with _jax.named_scope("update"):
                return (loss_sum + l_k, _jax.tree.map(_jnp.add, grad_sum, gw_k)), gx_k

        init = (_jnp.zeros((), _jnp.float32), _jax.tree.map(_jnp.zeros_like, weights))
        (loss, grad_w), grad_x = _jax.lax.scan(body, init, (per_example, given["loss_target"]))
    with _jax.named_scope("update"):
        delta_w, new_m, new_v = {}, {}, {}
        for n in TWIN_WEIGHTS:
            delta_w[n], new_m[n], new_v[n] = _adamw(weights[n], grad_w[n], given["m_" + n], given["v_" + n])
    return (loss, grad_x, *[grad_w[n] for n in TWIN_WEIGHTS], *[delta_w[n] for n in TWIN_WEIGHTS],
            *[new_m[n] for n in TWIN_WEIGHTS], *[new_v[n] for n in TWIN_WEIGHTS])
```

```python
import functools
import math

import jax
import jax.numpy as jnp
from jax import lax
from jax.experimental import pallas as pl
from jax.experimental.pallas import tpu as pltpu

f32 = jnp.float32
bf16 = jnp.bfloat16
MM = jnp.bfloat16

D_MODEL = 1024
PLE_DIM = 256
N_HEADS = 4
NOPE_DIM = 128
ROPE_DIM = 64
V_DIM = 128
QK_DIM = NOPE_DIM + ROPE_DIM
Q_LORA = 256
KV_LORA = 128
ATTN_WIDTH = N_HEADS * V_DIM
CONV_WIDTH = D_MODEL - ATTN_WIDTH
CONV_K = 3
ROPE_THETA = 10000.0
RMS_EPS = 1e-6
NEG_INF = -1e30
IN_TOTAL = Q_LORA + KV_LORA + ROPE_DIM + ATTN_WIDTH + 4 * CONV_WIDTH
ADAM_LR = 0.001
ADAM_B1 = 0.9
ADAM_B2 = 0.999
ADAM_EPS = 1e-08
ADAM_WD = 0.01
ADAM_STEP = 10

N_DEV = 8
LANES = 128
HEAD_PAD = 256
QK_PAD = N_HEADS * HEAD_PAD
PROJ_PAD = 3072
TS = 512
TS_M = 256
TS_E = 256
TQ = 512
VMEM_LIMIT = 58 * 1024 * 1024
MESH = pl.DeviceIdType.MESH


def _cparams(sem=None):
    return pltpu.CompilerParams(dimension_semantics=sem, vmem_limit_bytes=VMEM_LIMIT)


def _nt(a, b):
    return lax.dot_general(a, b, (((1,), (1,)), ((), ())), preferred_element_type=f32)


def _tn(a, b):
    return lax.dot_general(a, b, (((0,), (0,)), ((), ())), preferred_element_type=f32)


def _nn(a, b):
    return jnp.dot(a, b, preferred_element_type=f32)


def _rstd(x, n):
    return lax.rsqrt(jnp.sum(x * x, axis=-1, keepdims=True) * (1.0 / n) + RMS_EPS)


def _rms_bwd(dy, xhat, r, g, n):
    u = dy * g
    return r * (u - xhat * (jnp.sum(u * xhat, axis=-1, keepdims=True) * (1.0 / n)))


def _rope(t, c, s1, s2):
    return t * c + pltpu.roll(t, 32, 1) * s1 + pltpu.roll(t, 96, 1) * s2


def _rope_t(d, c, s1, s2):
    return d * c - pltpu.roll(d, 96, 1) * s2 - pltpu.roll(d, 32, 1) * s1


def _const(shape):
    return pl.BlockSpec(shape, lambda *_: (0,) * len(shape), pipeline_mode=pl.Buffered(1))


def _rope_tables(posf, inv_freq):
    S = posf.shape[0]

    def body(pos_ref, f_ref, c_ref, s1_ref, s2_ref):
        ang = pos_ref[...] * f_ref[...]
        lane = lax.broadcasted_iota(jnp.int32, ang.shape, 1)
        cs, sn = jnp.cos(ang), jnp.sin(ang)
        c_ref[...] = jnp.where(lane < ROPE_DIM, cs, 0.0)
        s1_ref[...] = jnp.where((lane >= ROPE_DIM // 2) & (lane < ROPE_DIM), sn, 0.0)
        s2_ref[...] = jnp.where(lane < ROPE_DIM // 2, -sn, 0.0)

    t = min(S, 1024)
    spec = pl.BlockSpec((t, LANES), lambda i: (i, 0))
    return pl.pallas_call(
        body, name="rope_tables", grid=(S // t,),
        out_shape=[jax.ShapeDtypeStruct((S, LANES), f32)] * 3,
        in_specs=[spec, _const((1, LANES))], out_specs=[spec] * 3,
        compiler_params=_cparams(("parallel",)),
    )(posf, inv_freq)


def _fwd_proj(x, tabs, g_in, w_in_p, g_cq, w_uq_p, g_ckv, w_ukv_p, g_q_p, g_k_p):
    S = x.shape[0]

    def body(x_ref, c_ref, s1_ref, s2_ref, g_in_ref, w_in_ref, g_cq_ref, w_uq_ref, g_ckv_ref, w_ukv_ref,
             g_q_ref, g_k_ref, proj_ref, q_ref, k_ref, v_ref):
        xv = x_ref[...]
        h = (xv * _rstd(xv, D_MODEL) * g_in_ref[...]).astype(MM)
        proj_ref[...] = _nn(h, w_in_ref[...])
        c, s1, s2 = c_ref[...], s1_ref[...], s2_ref[...]
        g_q, g_k = g_q_ref[...], g_k_ref[...]

        c_q = proj_ref[:, 0:Q_LORA]
        n_cq = (c_q * _rstd(c_q, Q_LORA) * g_cq_ref[...]).astype(MM)
        q_raw = _nn(n_cq, w_uq_ref[...])
        for hd in range(N_HEADS):
            qh = q_raw[:, hd * HEAD_PAD:(hd + 1) * HEAD_PAD]
            qn = qh * _rstd(qh, QK_DIM) * g_q
            q_ref[:, hd * HEAD_PAD:hd * HEAD_PAD + LANES] = qn[:, :LANES].astype(MM)
            q_ref[:, hd * HEAD_PAD + LANES:(hd + 1) * HEAD_PAD] = _rope(qn[:, LANES:], c, s1, s2).astype(MM)

        c_kv = proj_ref[:, Q_LORA:Q_LORA + KV_LORA]
        n_ckv = (c_kv * _rstd(c_kv, KV_LORA) * g_ckv_ref[...]).astype(MM)
        kv_raw = _nn(n_ckv, w_ukv_ref[...])
        kpe = proj_ref[:, Q_LORA + KV_LORA:Q_LORA + KV_LORA + LANES]
        ss_pe = jnp.sum(kpe * kpe, axis=-1, keepdims=True)
        kr = _rope(kpe * g_k[:, LANES:], c, s1, s2)
        for hd in range(N_HEADS):
            kn = kv_raw[:, hd * LANES:(hd + 1) * LANES]
            rk = lax.rsqrt((jnp.sum(kn * kn, axis=-1, keepdims=True) + ss_pe) * (1.0 / QK_DIM) + RMS_EPS)
            k_ref[:, hd * HEAD_PAD:hd * HEAD_PAD + LANES] = (kn * rk * g_k[:, :LANES]).astype(MM)
            k_ref[:, hd * HEAD_PAD + LANES:(hd + 1) * HEAD_PAD] = (kr * rk).astype(MM)
        v_ref[...] = kv_raw[:, N_HEADS * NOPE_DIM:].astype(MM)

    row = lambda w: pl.BlockSpec((TS, w), lambda i: (i, 0))
    return pl.pallas_call(
        body, name="fwd_proj", grid=(S // TS,),
        out_shape=[jax.ShapeDtypeStruct((S, PROJ_PAD), f32), jax.ShapeDtypeStruct((S, QK_PAD), MM),
                   jax.ShapeDtypeStruct((S, QK_PAD), MM), jax.ShapeDtypeStruct((S, ATTN_WIDTH), MM)],
        in_specs=[row(D_MODEL), row(LANES), row(LANES), row(LANES), _const((1, D_MODEL)), _const((D_MODEL, PROJ_PAD)),
                  _const((1, Q_LORA)), _const((Q_LORA, QK_PAD)), _const((1, KV_LORA)), _const((KV_LORA, 2 * ATTN_WIDTH)),
                  _const((1, HEAD_PAD)), _const((1, HEAD_PAD))],
        out_specs=[row(PROJ_PAD), row(QK_PAD), row(QK_PAD), row(ATTN_WIDTH)],
        compiler_params=_cparams(("parallel",)),
    )(x, *tabs, g_in, w_in_p, g_cq, w_uq_p, g_ckv, w_ukv_p, g_q_p, g_k_p)


def _flash_fwd(q, k, v):
    S = q.shape[0]
    T = min(TQ, S)
    scale = 1.0 / math.sqrt(QK_DIM)

    def body(q_ref, k_ref, v_ref, o_ref, lse_ref, m_sc, l_sc, acc_sc):
        i = pl.program_id(1)
        m_sc[...] = jnp.full(m_sc.shape, NEG_INF, f32)
        l_sc[...] = jnp.zeros(l_sc.shape, f32)
        acc_sc[...] = jnp.zeros(acc_sc.shape, f32)
        qv = q_ref[...]

        def block(j, masked):
            rows = pl.ds(pl.multiple_of(j * T, T), T)
            s = _nt(qv, k_ref[rows, :]) * scale
            if masked:
                r_i = lax.broadcasted_iota(jnp.int32, (T, T), 0)
                c_i = lax.broadcasted_iota(jnp.int32, (T, T), 1)
                s = jnp.where(c_i <= r_i, s, NEG_INF)
            m_prev = m_sc[...]
            m_new = jnp.maximum(m_prev, jnp.max(s, axis=-1, keepdims=True))
            alpha = jnp.exp(m_prev - m_new)
            p = jnp.exp(s - m_new[:, :1])
            l_sc[...] = alpha * l_sc[...] + jnp.sum(p, axis=-1, keepdims=True)
            acc_sc[...] = alpha * acc_sc[...] + _nn(p.astype(MM), v_ref[rows, :])
            m_sc[...] = m_new

        def step(j, carry):
            block(j, False)
            return carry

        lax.fori_loop(0, i, step, 0)
        block(i, True)
        o_ref[...] = acc_sc[...] / l_sc[...]
        lse_ref[...] = m_sc[...] + jnp.log(l_sc[...])

    return pl.pallas_call(
        body, name="flash_fwd", grid=(N_HEADS, S // T),
        out_shape=[jax.ShapeDtypeStruct((S, ATTN_WIDTH), f32), jax.ShapeDtypeStruct((S, ATTN_WIDTH), f32)],
        in_specs=[pl.BlockSpec((T, HEAD_PAD), lambda h, i: (i, h)),
                  pl.BlockSpec((S, HEAD_PAD), lambda h, i: (0, h)),
                  pl.BlockSpec((S, V_DIM), lambda h, i: (0, h))],
        out_specs=[pl.BlockSpec((T, V_DIM), lambda h, i: (i, h)), pl.BlockSpec((T, V_DIM), lambda h, i: (i, h))],
        scratch_shapes=[pltpu.VMEM((T, LANES), f32)] * 3,
        compiler_params=_cparams(("parallel", "parallel")),
    )(q, k, v)


def _shift_down(m, prev8, n):
    T = m.shape[0]
    rows = lax.broadcasted_iota(jnp.int32, m.shape, 0)
    head = jnp.tile(pltpu.roll(prev8, n, 0), (T // 8, 1))
    return jnp.where(rows >= n, pltpu.roll(m, n, 0), head)


def _shift_up(m, next8, n):
    T = m.shape[0]
    rows = lax.broadcasted_iota(jnp.int32, m.shape, 0)
    tail = jnp.tile(pltpu.roll(next8, 8 - n, 0), (T // 8, 1))
    return jnp.where(rows < T - n, pltpu.roll(m, T - n, 0), tail)


def _mid(o, proj, x, p, target, w_o, w_plg, w_pl, conv_w8, g_oa, g_oc, g_pl):
    S = x.shape[0]
    T = min(TS_M, S)
    nt = S // T

    def body(o_ref, za_ref, ccx_ref, cbzc_ref, prev_ref, x_ref, p_ref, t_ref, wo_ref, wplg_ref, wpl_ref, cw_ref,
             goa_ref, goc_ref, gpl_ref,
             dx2_ref, do_ref, delta_ref, dza_ref, dcbzc_ref, du_ref,
             dwo_ref, dwplg_ref, dwpl_ref, dcw_ref, dgoa_ref, dgoc_ref, dgpl_ref, loss_ref):
        i = pl.program_id(0)

        @pl.when(i == 0)
        def _():
            for r in (dwo_ref, dwplg_ref, dwpl_ref, dcw_ref, dgoa_ref, dgoc_ref, dgpl_ref, loss_ref):
                r[...] = jnp.zeros(r.shape, f32)

        o_v = o_ref[...]
        z_a = za_ref[...]
        cc, cx = ccx_ref[:, :CONV_WIDTH], ccx_ref[:, CONV_WIDTH:]
        cb, z_c = cbzc_ref[:, :CONV_WIDTH], cbzc_ref[:, CONV_WIDTH:]
        g_oa, g_oc, g_pl = goa_ref[...], goc_ref[...], gpl_ref[...]
        w0, w1, w2 = cw_ref[0:1, :], cw_ref[1:2, :], cw_ref[2:3, :]

        sa = jax.nn.sigmoid(z_a)
        silu_a = z_a * sa
        ga = o_v * silu_a
        ra = _rstd(ga, ATTN_WIDTH)
        gha = ga * ra
        m0 = cc * cx
        prev = prev_ref[:, :CONV_WIDTH] * prev_ref[:, CONV_WIDTH:] * (i > 0).astype(f32)
        m1 = _shift_down(m0, prev, 1)
        m2 = _shift_down(m0, prev, 2)
        u = w0 * m2 + w1 * m1 + w2 * m0
        sc = jax.nn.sigmoid(z_c)
        silu_c = z_c * sc
        gc = cb * u * silu_c
        rc = _rstd(gc, CONV_WIDTH)
        ghc = gc * rc
        ycat = jnp.concatenate([gha * g_oa, ghc * g_oc], axis=-1).astype(MM)
        x2 = x_ref[...] + _nn(ycat, wo_ref[...])
        r2 = _rstd(x2, D_MODEL)
        xh2 = x2 * r2
        n2 = (xh2 * g_pl).astype(MM)
        gate = jax.nn.sigmoid(_nn(n2, wplg_ref[...]))
        p_b = p_ref[...].astype(MM)
        pw = _nn(p_b, wpl_ref[...])
        err = x2 + gate * pw - t_ref[...]
        loss_ref[...] += jnp.sum(jnp.sum(err * err, axis=-1, keepdims=True), axis=0, keepdims=True) * (0.5 / D_MODEL)
        d_out = err * (1.0 / D_MODEL)
        dwpl_ref[...] += _tn(p_b, (d_out * gate).astype(MM))
        d_glog = (d_out * pw * gate * (1.0 - gate)).astype(MM)
        dwplg_ref[...] += _tn(n2, d_glog)
        d_n2 = _nt(d_glog, wplg_ref[...])
        dgpl_ref[...] += jnp.sum(d_n2 * xh2, axis=0, keepdims=True)
        d_x2 = d_out + _rms_bwd(d_n2, xh2, r2, g_pl, D_MODEL)
        dx2_ref[...] = d_x2
        d_x2b = d_x2.astype(MM)
        dwo_ref[...] += _tn(ycat, d_x2b)
        d_ycat = _nt(d_x2b, wo_ref[...])
        d_ya, d_yc = d_ycat[:, :ATTN_WIDTH], d_ycat[:, ATTN_WIDTH:]
        dgoa_ref[...] += jnp.sum(d_ya * gha, axis=0, keepdims=True)
        dgoc_ref[...] += jnp.sum(d_yc * ghc, axis=0, keepdims=True)
        d_ga = _rms_bwd(d_ya, gha, ra, g_oa, ATTN_WIDTH)
        d_gc = _rms_bwd(d_yc, ghc, rc, g_oc, CONV_WIDTH)
        d_o = d_ga * silu_a
        do_ref[...] = d_o.astype(MM)
        dza_ref[...] = (d_ga * o_v * (sa * (1.0 + z_a * (1.0 - sa)))).astype(MM)
        for hd in range(N_HEADS):
            cols = slice(hd * V_DIM, (hd + 1) * V_DIM)
            dl = jnp.sum(d_o[:, cols] * o_v[:, cols], axis=-1, keepdims=True)
            delta_ref[:, cols] = jnp.broadcast_to(dl, (T, V_DIM))
        d_u = d_gc * cb * silu_c
        du_ref[...] = d_u
        dcbzc_ref[:, :CONV_WIDTH] = (d_gc * u * silu_c).astype(MM)
        dcbzc_ref[:, CONV_WIDTH:] = (d_gc * cb * u * (sc * (1.0 + z_c * (1.0 - sc)))).astype(MM)
        dcw_ref[0:1, :] += jnp.sum(d_u * m2, axis=0, keepdims=True)
        dcw_ref[1:2, :] += jnp.sum(d_u * m1, axis=0, keepdims=True)
        dcw_ref[2:3, :] += jnp.sum(d_u * m0, axis=0, keepdims=True)

    row = lambda w, blk=0: pl.BlockSpec((T, w), lambda i: (i, blk))
    acc = lambda shape: pl.BlockSpec(shape, lambda i: (0, 0))
    tb = T // 8
    sds = jax.ShapeDtypeStruct
    return pl.pallas_call(
        body, name="mid", grid=(nt,),
        out_shape=[sds((S, D_MODEL), f32), sds((S, ATTN_WIDTH), MM), sds((S, ATTN_WIDTH), f32), sds((S, ATTN_WIDTH), MM),
                   sds((S, 2 * CONV_WIDTH), MM), sds((S, CONV_WIDTH), f32),
                   sds((D_MODEL, D_MODEL), f32), sds((D_MODEL, D_MODEL), f32), sds((PLE_DIM, D_MODEL), f32), sds((8, CONV_WIDTH), f32),
                   sds((1, ATTN_WIDTH), f32), sds((1, CONV_WIDTH), f32), sds((1, D_MODEL), f32), sds((1, LANES), f32)],
        in_specs=[row(ATTN_WIDTH), row(ATTN_WIDTH, 1), row(2 * CONV_WIDTH, 1), row(2 * CONV_WIDTH, 2),
                  pl.BlockSpec((8, 2 * CONV_WIDTH), lambda i: (jnp.maximum(i * tb - 1, 0), 1)),
                  row(D_MODEL), row(PLE_DIM), row(D_MODEL),
                  _const((D_MODEL, D_MODEL)), _const((D_MODEL, D_MODEL)), _const((PLE_DIM, D_MODEL)), _const((8, CONV_WIDTH)),
                  _const((1, ATTN_WIDTH)), _const((1, CONV_WIDTH)), _const((1, D_MODEL))],
        out_specs=[row(D_MODEL), row(ATTN_WIDTH), row(ATTN_WIDTH), row(ATTN_WIDTH), row(2 * CONV_WIDTH), row(CONV_WIDTH),
                   acc((D_MODEL, D_MODEL)), acc((D_MODEL, D_MODEL)), acc((PLE_DIM, D_MODEL)), acc((8, CONV_WIDTH)),
                   acc((1, ATTN_WIDTH)), acc((1, CONV_WIDTH)), acc((1, D_MODEL)), acc((1, LANES))],
        compiler_params=_cparams(("arbitrary",)),
    )(o, proj, proj, proj, proj, x, p, target, w_o, w_plg, w_pl, conv_w8, g_oa, g_oc, g_pl)


def _flash_bwd(q, k, v, d_o, lse, delta):
    S = q.shape[0]
    T = min(TQ, S)
    nq = S // T
    scale = 1.0 / math.sqrt(QK_DIM)

    def body(k_ref, v_ref, q_ref, do_ref, lse_ref, delta_ref, dq_ref, dk_ref, dv_ref):
        j = pl.program_id(1)

        @pl.when(j == 0)
        def _():
            dq_ref[...] = jnp.zeros(dq_ref.shape, f32)

        dk_ref[...] = jnp.zeros(dk_ref.shape, f32)
        dv_ref[...] = jnp.zeros(dv_ref.shape, f32)
        kv, vv = k_ref[...], v_ref[...]

        def block(i, masked):
            rows = pl.ds(pl.multiple_of(i * T, T), T)
            qi, doi = q_ref[rows, :], do_ref[rows, :]
            s = _nt(qi, kv) * scale
            if masked:
                r_i = lax.broadcasted_iota(jnp.int32, (T, T), 0)
                c_i = lax.broadcasted_iota(jnp.int32, (T, T), 1)
                s = jnp.where(c_i <= r_i, s, NEG_INF)
            p = jnp.exp(s - lse_ref[rows, :][:, :1])
            dp = _nt(doi, vv)
            ds = (p * (dp - delta_ref[rows, :][:, :1]) * scale).astype(MM)
            dv_ref[...] += _tn(p.astype(MM), doi)
            dk_ref[...] += _tn(ds, qi)
            dq_ref[rows, :] += _nn(ds, kv)

        block(j, True)

        def step(i, carry):
            block(i, False)
            return carry

        lax.fori_loop(j + 1, nq, step, 0)

    whole = lambda w: pl.BlockSpec((S, w), lambda h, j: (0, h))
    return pl.pallas_call(
        body, name="flash_bwd", grid=(N_HEADS, nq),
        out_shape=[jax.ShapeDtypeStruct((S, QK_PAD), f32), jax.ShapeDtypeStruct((S, QK_PAD), f32),
                   jax.ShapeDtypeStruct((S, ATTN_WIDTH), f32)],
        in_specs=[pl.BlockSpec((T, HEAD_PAD), lambda h, j: (j, h)), pl.BlockSpec((T, V_DIM), lambda h, j: (j, h)),
                  whole(HEAD_PAD), whole(V_DIM), whole(V_DIM), whole(V_DIM)],
        out_specs=[whole(HEAD_PAD), pl.BlockSpec((T, HEAD_PAD), lambda h, j: (j, h)),
                   pl.BlockSpec((T, V_DIM), lambda h, j: (j, h))],
        compiler_params=_cparams(("arbitrary", "arbitrary")),
    )(k, v, q, d_o, lse, delta)


def _bwd_proj(x, proj, d_q, d_k, d_v, d_za, d_cbzc, d_u, d_x2, tabs, g_in, w_in_p, g_cq, w_uq_p, g_ckv, w_ukv_p,
              g_q_p, g_k_p, conv_w8):
    S = x.shape[0]
    T = min(TS_E, S)
    nt = S // T

    def body(x_ref, cqkv_ref, ccx_ref, dq_ref, dk_ref, dv_ref, dza_ref, dcbzc_ref, du_ref, dun_ref, dx2_ref,
             c_ref, s1_ref, s2_ref, g_in_ref, w_in_ref, g_cq_ref, w_uq_ref, g_ckv_ref, w_ukv_ref, g_q_ref, g_k_ref, cw_ref,
             gx_ref, dwin_hbm, dwuq_ref, dwukv_ref, dgin_ref, dgcq_ref, dgckv_ref, dgq_ref, dgk_ref,
             dproj_sc, dqraw_sc, dkvraw_sc, dwin_sc, sem):
        i = pl.program_id(0)

        @pl.when(i == 0)
        def _():
            dwin_sc[...] = jnp.zeros(dwin_sc.shape, f32)
            for r in (dwuq_ref, dwukv_ref, dgin_ref, dgcq_ref, dgckv_ref, dgq_ref, dgk_ref):
                r[...] = jnp.zeros(r.shape, f32)

        c, s1, s2 = c_ref[...], s1_ref[...], s2_ref[...]
        g_q, g_k = g_q_ref[...], g_k_ref[...]
        g_cq, g_ckv, g_in = g_cq_ref[...], g_ckv_ref[...], g_in_ref[...]

        c_q = cqkv_ref[:, 0:Q_LORA]
        r_cq = _rstd(c_q, Q_LORA)
        cqh = c_q * r_cq
        n_cq = (cqh * g_cq).astype(MM)
        q_raw = _nn(n_cq, w_uq_ref[...])
        dgq = jnp.zeros((1, HEAD_PAD), f32)
        for hd in range(N_HEADS):
            cols = slice(hd * HEAD_PAD, (hd + 1) * HEAD_PAD)
            qh = q_raw[:, cols]
            rq = _rstd(qh, QK_DIM)
            qhh = qh * rq
            dqh = dq_ref[:, cols]
            d_qn = jnp.concatenate([dqh[:, :LANES], _rope_t(dqh[:, LANES:], c, s1, s2)], axis=-1)
            dgq += jnp.sum(d_qn * qhh, axis=0, keepdims=True)
            dqraw_sc[:, cols] = _rms_bwd(d_qn, qhh, rq, g_q, QK_DIM).astype(MM)
        dgq_ref[...] += dgq
        d_qraw = dqraw_sc[...]
        dwuq_ref[...] += _tn(n_cq, d_qraw)
        d_ncq = _nt(d_qraw, w_uq_ref[...])
        dgcq_ref[...] += jnp.sum(d_ncq * cqh, axis=0, keepdims=True)
        dproj_sc[:, 0:Q_LORA] = _rms_bwd(d_ncq, cqh, r_cq, g_cq, Q_LORA).astype(MM)

        c_kv = cqkv_ref[:, Q_LORA:Q_LORA + KV_LORA]
        r_ckv = _rstd(c_kv, KV_LORA)
        ckvh = c_kv * r_ckv
        n_ckv = (ckvh * g_ckv).astype(MM)
        k_nope = _nn(n_ckv, w_ukv_ref[:, :N_HEADS * NOPE_DIM])
        kpe = cqkv_ref[:, Q_LORA + KV_LORA:Q_LORA + KV_LORA + LANES]
        ss_pe = jnp.sum(kpe * kpe, axis=-1, keepdims=True)
        g_kn, g_kp = g_k[:, :LANES], g_k[:, LANES:]
        d_kpe = jnp.zeros((T, LANES), f32)
        dgk_n = jnp.zeros((1, LANES), f32)
        dgk_p = jnp.zeros((1, LANES), f32)
        for hd in range(N_HEADS):
            kn = k_nope[:, hd * LANES:(hd + 1) * LANES]
            rk = lax.rsqrt((jnp.sum(kn * kn, axis=-1, keepdims=True) + ss_pe) * (1.0 / QK_DIM) + RMS_EPS)
            knh, kph = kn * rk, kpe * rk
            d_kn_n = dk_ref[:, hd * HEAD_PAD:hd * HEAD_PAD + LANES]
            d_kr = _rope_t(dk_ref[:, hd * HEAD_PAD + LANES:(hd + 1) * HEAD_PAD], c, s1, s2)
            dgk_n += jnp.sum(d_kn_n * knh, axis=0, keepdims=True)
            dgk_p += jnp.sum(d_kr * kph, axis=0, keepdims=True)
            u_n, u_p = d_kn_n * g_kn, d_kr * g_kp
            mt = (jnp.sum(u_n * knh, axis=-1, keepdims=True) + jnp.sum(u_p * kph, axis=-1, keepdims=True)) * (1.0 / QK_DIM)
            dkvraw_sc[:, hd * LANES:(hd + 1) * LANES] = (rk * (u_n - knh * mt)).astype(MM)
            d_kpe += rk * (u_p - kph * mt)
        dgk_ref[:, :LANES] += dgk_n
        dgk_ref[:, LANES:] += dgk_p
        dkvraw_sc[:, N_HEADS * NOPE_DIM:] = dv_ref[...].astype(MM)
        d_kvraw = dkvraw_sc[...]
        dwukv_ref[...] += _tn(n_ckv, d_kvraw)
        d_nckv = _nt(d_kvraw, w_ukv_ref[...])
        dgckv_ref[...] += jnp.sum(d_nckv * ckvh, axis=0, keepdims=True)
        dproj_sc[:, Q_LORA:Q_LORA + KV_LORA] = _rms_bwd(d_nckv, ckvh, r_ckv, g_ckv, KV_LORA).astype(MM)
        dproj_sc[:, Q_LORA + KV_LORA:Q_LORA + KV_LORA + LANES] = d_kpe.astype(MM)

        dproj_sc[:, 512:1024] = dza_ref[...]
        d_u = du_ref[...]
        nxt = dun_ref[...] * (i < nt - 1).astype(f32)
        d_m = cw_ref[2:3, :] * d_u + cw_ref[1:2, :] * _shift_up(d_u, nxt, 1) + cw_ref[0:1, :] * _shift_up(d_u, nxt, 2)
        dproj_sc[:, 1024:1536] = (d_m * ccx_ref[:, CONV_WIDTH:]).astype(MM)
        dproj_sc[:, 1536:2048] = (d_m * ccx_ref[:, :CONV_WIDTH]).astype(MM)
        dproj_sc[:, 2048:3072] = dcbzc_ref[...]

        xv = x_ref[...]
        r_in = _rstd(xv, D_MODEL)
        xh = xv * r_in
        hb = (xh * g_in).astype(MM)
        d_proj = dproj_sc[...]
        dwin_sc[...] += _tn(hb, d_proj)
        d_h = _nt(d_proj, w_in_ref[...])
        dgin_ref[...] += jnp.sum(d_h * xh, axis=0, keepdims=True)
        gx_ref[...] = dx2_ref[...] + _rms_bwd(d_h, xh, r_in, g_in, D_MODEL)

        @pl.when(i == nt - 1)
        def _():
            cp = pltpu.make_async_copy(dwin_sc, dwin_hbm, sem)
            cp.start()
            cp.wait()

    row = lambda w, blk=0: pl.BlockSpec((T, w), lambda i: (i, blk))
    acc = lambda shape: pl.BlockSpec(shape, lambda i: (0, 0))
    tb = T // 8
    sds = jax.ShapeDtypeStruct
    return pl.pallas_call(
        body, name="bwd_proj", grid=(nt,),
        out_shape=[sds((S, D_MODEL), f32), sds((D_MODEL, PROJ_PAD), f32), sds((Q_LORA, QK_PAD), f32),
                   sds((KV_LORA, 2 * ATTN_WIDTH), f32), sds((1, D_MODEL), f32), sds((1, Q_LORA), f32), sds((1, KV_LORA), f32),
                   sds((1, HEAD_PAD), f32), sds((1, HEAD_PAD), f32)],
        in_specs=[row(D_MODEL), row(512, 0), row(2 * CONV_WIDTH, 1), row(QK_PAD), row(QK_PAD), row(ATTN_WIDTH),
                  row(ATTN_WIDTH), row(2 * CONV_WIDTH), row(CONV_WIDTH),
                  pl.BlockSpec((8, CONV_WIDTH), lambda i: (jnp.minimum((i + 1) * tb, S // 8 - 1), 0)),
                  row(D_MODEL), row(LANES), row(LANES), row(LANES),
                  _const((1, D_MODEL)), _const((D_MODEL, PROJ_PAD)), _const((1, Q_LORA)), _const((Q_LORA, QK_PAD)),
                  _const((1, KV_LORA)), _const((KV_LORA, 2 * ATTN_WIDTH)), _const((1, HEAD_PAD)), _const((1, HEAD_PAD)),
                  _const((8, CONV_WIDTH))],
        out_specs=[row(D_MODEL), pl.BlockSpec(memory_space=pl.ANY), acc((Q_LORA, QK_PAD)), acc((KV_LORA, 2 * ATTN_WIDTH)),
                   acc((1, D_MODEL)), acc((1, Q_LORA)), acc((1, KV_LORA)), acc((1, HEAD_PAD)), acc((1, HEAD_PAD))],
        scratch_shapes=[pltpu.VMEM((T, PROJ_PAD), MM), pltpu.VMEM((T, QK_PAD), MM), pltpu.VMEM((T, 2 * ATTN_WIDTH), MM),
                        pltpu.VMEM((D_MODEL, PROJ_PAD), f32), pltpu.SemaphoreType.DMA],
        compiler_params=_cparams(("arbitrary",)),
    )(x, proj, proj, d_q, d_k, d_v, d_za, d_cbzc, d_u, d_u, d_x2, *tabs, g_in, w_in_p, g_cq, w_uq_p, g_ckv, w_ukv_p,
      g_q_p, g_k_p, conv_w8)


def _all_gather(block, name, in_vmem):
    m_per, n = block.shape

    def body(x_ref, out_ref, send_sems, recv_sems, local_sem):
        x, y, c = lax.axis_index("x"), lax.axis_index("y"), lax.axis_index("c")
        me, sibling = (x, y, c), (x, y, 1 - c)
        chips = [(1 - x, y), (x, 1 - y), (1 - x, 1 - y)]

        def rows(px, py, pc):
            return out_ref.at[pl.ds((4 * px + 2 * py + pc) * m_per, m_per), :]

        def copy(k, blk, to, src=None):
            return pltpu.make_async_remote_copy(
                src_ref=rows(*blk) if src is None else src, dst_ref=rows(*blk),
                send_sem=send_sems.at[k], recv_sem=recv_sems.at[k], device_id=to, device_id_type=MESH)

        mine = pltpu.make_async_copy(x_ref, rows(*me), local_sem)
        mine.start()
        first = [copy(0, me, sibling, src=x_ref)]
        first += [copy(1 + j, me, (*chip, c), src=x_ref) for j, chip in enumerate(chips)]
        for cp in first:
            cp.start()
        passed = [copy(4 + j, (*chip, c), sibling) for j, chip in enumerate(chips)]
        for j, chip in enumerate(chips):
            copy(1 + j, (*chip, c), me).wait_recv()
            passed[j].start()
        copy(0, sibling, me).wait_recv()
        for j, chip in enumerate(chips):
            copy(4 + j, (*chip, 1 - c), me).wait_recv()
        for cp in first + passed:
            cp.wait_send()
        mine.wait()

    space = pltpu.VMEM if in_vmem else pl.ANY
    return pl.pallas_call(
        body, name=name, out_shape=jax.ShapeDtypeStruct((N_DEV * m_per, n), block.dtype),
        in_specs=[pl.BlockSpec(memory_space=space)], out_specs=pl.BlockSpec(memory_space=space),
        scratch_shapes=[pltpu.SemaphoreType.DMA((7,)), pltpu.SemaphoreType.DMA((7,)), pltpu.SemaphoreType.DMA],
    )(block)


def _sibling_exchange(pack):
    _, nchip, R, n = pack.shape

    def body(p_ref, got_ref, send_sem, recv_sem):
        x, y, c = lax.axis_index("x"), lax.axis_index("y"), lax.axis_index("c")
        cp = pltpu.make_async_remote_copy(src_ref=p_ref.at[1 - c], dst_ref=got_ref, send_sem=send_sem, recv_sem=recv_sem,
                                          device_id=(x, y, 1 - c), device_id_type=MESH)
        cp.start()
        cp.wait()

    return pl.pallas_call(
        body, name="rs_sibling", out_shape=jax.ShapeDtypeStruct((nchip, R, n), pack.dtype),
        in_specs=[pl.BlockSpec(memory_space=pl.ANY)], out_specs=pl.BlockSpec(memory_space=pl.ANY),
        scratch_shapes=[pltpu.SemaphoreType.DMA, pltpu.SemaphoreType.DMA],
    )(pack)


def _chip_exchange(part):
    nchip, R, n = part.shape

    def body(p_ref, got_ref, send_sems, recv_sems, local_sem):
        x, y, c = lax.axis_index("x"), lax.axis_index("y"), lax.axis_index("c")
        my_chip = 2 * x + y
        chips = [(1 - x, y), (x, 1 - y), (1 - x, 1 - y)]
        mine = pltpu.make_async_copy(p_ref.at[my_chip], got_ref.at[my_chip], local_sem)
        mine.start()
        sends = []
        for k, (tx, ty) in enumerate(chips):
            cp = pltpu.make_async_remote_copy(src_ref=p_ref.at[2 * tx + ty], dst_ref=got_ref.at[my_chip],
                                              send_sem=send_sems.at[k], recv_sem=recv_sems.at[k],
                                              device_id=(tx, ty, c), device_id_type=MESH)
            cp.start()
            sends.append(cp)
        for k, (sx, sy) in enumerate(chips):
            pltpu.make_async_remote_copy(src_ref=p_ref.at[my_chip], dst_ref=got_ref.at[2 * sx + sy],
                                         send_sem=send_sems.at[k], recv_sem=recv_sems.at[k],
                                         device_id=(sx, sy, c), device_id_type=MESH).wait_recv()
        for cp in sends:
            cp.wait_send()
        mine.wait()

    return pl.pallas_call(
        body, name="rs_chips", out_shape=jax.ShapeDtypeStruct((nchip, R, n), part.dtype),
        in_specs=[pl.BlockSpec(memory_space=pl.ANY)], out_specs=pl.BlockSpec(memory_space=pl.ANY),
        scratch_shapes=[pltpu.SemaphoreType.DMA((3,)), pltpu.SemaphoreType.DMA((3,)), pltpu.SemaphoreType.DMA],
    )(part)


def _add_blocks(a, b):
    nb, R, n = a.shape
    t = R // 5 if R % 40 == 0 else R

    def body(a_ref, b_ref, o_ref):
        o_ref[...] = a_ref[...] + b_ref[...]

    spec = pl.BlockSpec((1, t, n), lambda i, j: (i, j, 0))
    return pl.pallas_call(body, name="rs_add", grid=(nb, R // t), out_shape=jax.ShapeDtypeStruct(a.shape, a.dtype),
                          in_specs=[spec, spec], out_specs=spec,
                          compiler_params=_cparams(("parallel", "parallel")))(a, b)


def _sum_chips(got):
    nb, R, n = got.shape
    t = R // 5 if R % 40 == 0 else R

    def body(g_ref, o_ref):
        o_ref[...] = ((g_ref[0] + g_ref[1]) + g_ref[2]) + g_ref[3]

    return pl.pallas_call(body, name="rs_sum", grid=(R // t,), out_shape=jax.ShapeDtypeStruct((R, n), got.dtype),
                          in_specs=[pl.BlockSpec((nb, t, n), lambda j: (0, j, 0))], out_specs=pl.BlockSpec((t, n), lambda j: (j, 0)),
                          compiler_params=_cparams(("parallel",)))(got)


def _adamw_all(ws, gs, ms, vs, gain_parts):
    n = len(ws)
    ng = len(GAIN_SLOTS)

    def body(*refs):
        gp_ref = refs[0]
        w_refs, g_refs, m_refs, v_refs = (refs[1 + k * n:1 + (k + 1) * n] for k in range(4))
        outs = refs[1 + 4 * n:]
        gsum_ref, loss_ref = outs[0], outs[1]
        gg_refs = outs[2:2 + ng]
        d_refs, nm_refs, nv_refs = (outs[2 + ng + k * n:2 + ng + (k + 1) * n] for k in range(3))
        tot = gp_ref[0]
        for d in range(1, N_DEV):
            tot = tot + gp_ref[d]
        gsum_ref[...] = tot
        loss_ref[...] = gsum_ref[GAIN_ROWS - 1:GAIN_ROWS, :]
        for k in range(n):
            if k < ng:
                r0, width = GAIN_SLOTS[k]
                nr = width // LANES
                for r in range(nr):
                    gg_refs[k][:, r * LANES:(r + 1) * LANES] = gsum_ref[r0 + r:r0 + r + 1, :]
                g = gg_refs[k][...]
            else:
                g = g_refs[k][...]
            w = w_refs[k][...]
            m = ADAM_B1 * m_refs[k][...] + (1.0 - ADAM_B1) * g
            v = ADAM_B2 * v_refs[k][...] + (1.0 - ADAM_B2) * (g * g)
            m_hat = m / (1.0 - ADAM_B1 ** ADAM_STEP)
            v_hat = v / (1.0 - ADAM_B2 ** ADAM_STEP)
            d_refs[k][...] = -ADAM_LR * (m_hat / (jnp.sqrt(v_hat) + ADAM_EPS) + ADAM_WD * w)
            nm_refs[k][...] = m
            nv_refs[k][...] = v

    sds = jax.ShapeDtypeStruct
    like = [sds(w.shape, f32) for w in ws]
    out_shape = [sds((GAIN_ROWS, LANES), f32), sds((1, LANES), f32)] + like[:ng] + like * 3
    vm = pl.BlockSpec(memory_space=pltpu.VMEM)
    outs = pl.pallas_call(
        body, name="adamw", out_shape=out_shape, in_specs=[vm] * (1 + 4 * n), out_specs=[vm] * len(out_shape),
        compiler_params=pltpu.CompilerParams(vmem_limit_bytes=VMEM_LIMIT),
    )(gain_parts, *ws, *gs, *ms, *vs)
    loss = outs[1]
    gg = outs[2:2 + ng]
    deltas, new_m, new_v = (outs[2 + ng + k * n:2 + ng + (k + 1) * n] for k in range(3))
    return loss, gg, deltas, new_m, new_v


GAIN_ROWS = 32
GAIN_SLOTS = [(0, 1024), (8, 256), (10, 128), (11, 256), (13, 256), (15, 512), (19, 512), (23, 1024)]

W_ROWS = [3008, 192, 128, 16, 1024, 256, 1024]
G_ROWS = [3008, 192, 128, 8, 1024, 256, 1024]


def _pad_rows(a, rows):
    flat = a.reshape(-1)
    return jnp.pad(flat, (0, rows * LANES - flat.shape[0])).reshape(rows, LANES)


def _split_rows(a, rows_list):
    out, r = [], 0
    for n in rows_list:
        out.append(a[..., r:r + n, :])
        r += n
    return out


def kernel(x, p, positions, g_in, w_in, g_cq, w_uq, g_ckv, w_ukv, g_q, g_k, conv_w, g_oa, g_oc, w_o, w_pl, w_plg, g_pl, loss_target, m_g_in, m_w_in, m_g_cq, m_w_uq, m_g_ckv, m_w_ukv, m_g_q, m_g_k, m_conv_w, m_g_oa, m_g_oc, m_w_o, m_w_pl, m_w_plg, m_g_pl, v_g_in, v_w_in, v_g_cq, v_w_uq, v_g_ckv, v_w_ukv, v_g_q, v_g_k, v_conv_w, v_g_oa, v_g_oc, v_w_o, v_w_pl, v_w_plg, v_g_pl):
    S = x.shape[1]
    nd = N_DEV
    xs, ps, tgt = x[0], p[0, 0], loss_target[0]

    conv_bits = lax.bitcast_convert_type(conv_w[0], bf16)
    shards = [w_in[0], w_uq[0], w_ukv[0], conv_bits, w_o[0], w_pl[0], w_plg[0]]
    pack = jnp.concatenate([_pad_rows(s.astype(bf16), r) for s, r in zip(shards, W_ROWS)], axis=0)
    R = pack.shape[0]
    gathered = _all_gather(pack, "ag_weights", in_vmem=False).reshape(nd, R, LANES)
    a_in, a_uq, a_ukv, a_cv, a_o, a_pl, a_plg = _split_rows(gathered, W_ROWS)

    def cols(a, rows, c):
        return a.reshape(nd, rows, c).transpose(1, 0, 2).reshape(rows, nd * c)

    win = cols(a_in, D_MODEL, IN_TOTAL // nd)
    zpad = jnp.zeros((D_MODEL, ROPE_DIM), bf16)
    w_in_p = jnp.concatenate([win[:, 0:448], zpad, win[:, 448:960], win[:, 1472:2496], win[:, 960:1472], win[:, 2496:3008]], axis=1)
    wuq = cols(a_uq, Q_LORA, N_HEADS * QK_DIM // nd).reshape(Q_LORA, N_HEADS, QK_DIM)
    w_uq_p = jnp.pad(wuq, ((0, 0), (0, 0), (0, HEAD_PAD - QK_DIM))).reshape(Q_LORA, QK_PAD)
    wukv = cols(a_ukv, KV_LORA, 2 * ATTN_WIDTH // nd).reshape(KV_LORA, N_HEADS, 2, NOPE_DIM)
    w_ukv_p = wukv.transpose(0, 2, 1, 3).reshape(KV_LORA, 2 * ATTN_WIDTH)
    cv_bits = a_cv.reshape(nd, -1)[:, :CONV_K * (CONV_WIDTH // nd) * 2].reshape(nd, CONV_K, CONV_WIDTH // nd, 2)
    conv_full = lax.bitcast_convert_type(cv_bits, f32).transpose(1, 0, 2).reshape(CONV_K, CONV_WIDTH)
    conv_w8 = jnp.pad(conv_full, ((0, 8 - CONV_K), (0, 0)))
    w_o_f = a_o.reshape(D_MODEL, D_MODEL)
    w_pl_f = cols(a_pl, PLE_DIM, D_MODEL // nd)
    w_plg_f = a_plg.reshape(D_MODEL, D_MODEL)
    w_in_p, w_uq_p, w_ukv_p, w_o_f, w_pl_f, w_plg_f = (w.astype(MM) for w in (w_in_p, w_uq_p, w_ukv_p, w_o_f, w_pl_f, w_plg_f))

    g_q_p = jnp.pad(g_q, ((0, 0), (0, HEAD_PAD - QK_DIM)))
    g_k_p = jnp.pad(g_k, ((0, 0), (0, HEAD_PAD - QK_DIM)))

    inv_freq = 1.0 / (ROPE_THETA ** (jnp.arange(0, ROPE_DIM, 2, dtype=f32) / ROPE_DIM))
    inv_row = jnp.concatenate([inv_freq, inv_freq, jnp.zeros((LANES - ROPE_DIM,), f32)]).reshape(1, LANES)
    posf = jnp.broadcast_to(positions[0].astype(f32)[:, None], (S, LANES))
    tabs = _rope_tables(posf, inv_row)

    proj, q, k, v = _fwd_proj(xs, tabs, g_in, w_in_p, g_cq, w_uq_p, g_ckv, w_ukv_p, g_q_p, g_k_p)
    o, lse = _flash_fwd(q, k, v)
    (d_x2, d_o, delta, d_za, d_cbzc, d_u, dw_o, dw_plg, dw_pl, dcw, dg_oa, dg_oc, dg_pl, loss_part) = _mid(
        o, proj, xs, ps, tgt, w_o_f, w_plg_f, w_pl_f, conv_w8, g_oa, g_oc, g_pl)
    d_q, d_k, d_v = _flash_bwd(q, k, v, d_o, lse, delta)
    (grad_x, dw_in_p, dw_uq_p, dw_ukv_p, dg_in, dg_cq, dg_ckv, dg_q_p, dg_k_p) = _bwd_proj(
        xs, proj, d_q, d_k, d_v, d_za, d_cbzc, d_u, d_x2, tabs, g_in, w_in_p, g_cq, w_uq_p, g_ckv, w_ukv_p, g_q_p, g_k_p, conv_w8)

    dw_in = jnp.concatenate([dw_in_p[:, 0:448], dw_in_p[:, 512:1024], dw_in_p[:, 2048:2560], dw_in_p[:, 1024:2048],
                             dw_in_p[:, 2560:3072]], axis=1)
    dw_uq = dw_uq_p.reshape(Q_LORA, N_HEADS, HEAD_PAD)[:, :, :QK_DIM].reshape(Q_LORA, N_HEADS * QK_DIM)
    dw_ukv = dw_ukv_p.reshape(KV_LORA, 2, N_HEADS, NOPE_DIM).transpose(0, 2, 1, 3).reshape(KV_LORA, 2 * ATTN_WIDTH)
    dconv = dcw[:CONV_K]

    def shard_cols(a):
        rows = a.shape[0]
        return a.reshape(rows, nd, -1).transpose(1, 0, 2).reshape(nd, -1)

    def pack_rows(flat, rows):
        return jnp.pad(flat, ((0, 0), (0, rows * LANES - flat.shape[1]))).reshape(nd, rows, LANES)

    flats = [shard_cols(dw_in), shard_cols(dw_uq), shard_cols(dw_ukv), shard_cols(dconv), dw_o.reshape(nd, -1),
             shard_cols(dw_pl), dw_plg.reshape(nd, -1)]
    gpack = jnp.concatenate([pack_rows(fl, r) for fl, r in zip(flats, G_ROWS)], axis=1)
    Rg = gpack.shape[1]
    gpack = gpack.reshape(4, 2, Rg, LANES).transpose(1, 0, 2, 3)
    my_c = lax.axis_index("c")
    from_sib = _sibling_exchange(gpack)
    mine = lax.dynamic_index_in_dim(gpack, my_c, axis=0, keepdims=False)
    chip_part = _add_blocks(mine, from_sib)
    gshard = _sum_chips(_chip_exchange(chip_part))
    s_in, s_uq, s_ukv, s_cv, s_o, s_pl, s_plg = _split_rows(gshard, G_ROWS)
    grads_w = [s_in.reshape(D_MODEL, -1), s_uq.reshape(Q_LORA, -1), s_ukv.reshape(KV_LORA, -1),
               s_cv.reshape(-1)[:CONV_K * CONV_WIDTH // nd].reshape(CONV_K, -1), s_o.reshape(-1, D_MODEL),
               s_pl.reshape(PLE_DIM, -1), s_plg.reshape(-1, D_MODEL)]

    gains_part = [dg_in, dg_cq, dg_ckv, dg_q_p, dg_k_p, dg_oa, dg_oc, dg_pl]
    gflat = jnp.concatenate([g.reshape(-1) for g in gains_part] + [jnp.zeros(((GAIN_ROWS - 1) * LANES - 3968,), f32),
                                                                    loss_part.reshape(-1)])
    gain_parts = _all_gather(gflat.reshape(GAIN_ROWS, LANES), "ag_gains", in_vmem=True).reshape(nd, GAIN_ROWS, LANES)

    gains = [g_in, g_cq, g_ckv, g_q_p, g_k_p, g_oa, g_oc, g_pl]
    padq = lambda a: jnp.pad(a, ((0, 0), (0, HEAD_PAD - QK_DIM)))
    m_gains = [m_g_in, m_g_cq, m_g_ckv, padq(m_g_q), padq(m_g_k), m_g_oa, m_g_oc, m_g_pl]
    v_gains = [v_g_in, v_g_cq, v_g_ckv, padq(v_g_q), padq(v_g_k), v_g_oa, v_g_oc, v_g_pl]
    ws = gains + [w_in[0], w_uq[0], w_ukv[0], conv_w[0], w_o[0], w_pl[0], w_plg[0]]
    ms = m_gains + [m_w_in[0], m_w_uq[0], m_w_ukv[0], m_conv_w[0], m_w_o[0], m_w_pl[0], m_w_plg[0]]
    vs = v_gains + [v_w_in[0], v_w_uq[0], v_w_ukv[0], v_conv_w[0], v_w_o[0], v_w_pl[0], v_w_plg[0]]
    gs = [jnp.zeros_like(g) for g in gains] + grads_w
    loss_row, gg, deltas, new_m, new_v = _adamw_all(ws, gs, ms, vs, gain_parts)
    loss = loss_row[0, 0]

    def ordered(gl, wl):
        g_in_, g_cq_, g_ckv_, g_q_, g_k_, g_oa_, g_oc_, g_pl_ = gl
        g_q_, g_k_ = g_q_[:, :QK_DIM], g_k_[:, :QK_DIM]
        w_in_, w_uq_, w_ukv_, cw_, w_o_, w_pl_, w_plg_ = [w[None] for w in wl]
        return [g_in_, w_in_, g_cq_, w_uq_, g_ckv_, w_ukv_, g_q_, g_k_, cw_, g_oa_, g_oc_, w_o_, w_pl_, w_plg_, g_pl_]

    ng = len(gains)
    outs = [loss, grad_x[None]]
    outs += ordered(gg, grads_w)
    for lst in (deltas, new_m, new_v):
        outs += ordered(lst[:ng], lst[ng:])
    return tuple(outs)
```

```python
import functools
import math

import jax
import jax.numpy as jnp
from jax import lax
from jax.experimental import pallas as pl
from jax.experimental.pallas import tpu as pltpu

f32 = jnp.float32
bf16 = jnp.bfloat16
MM = jnp.bfloat16

D_MODEL = 1024
PLE_DIM = 256
N_HEADS = 4
NOPE_DIM = 128
ROPE_DIM = 64
V_DIM = 128
QK_DIM = NOPE_DIM + ROPE_DIM
Q_LORA = 256
KV_LORA = 128
ATTN_WIDTH = N_HEADS * V_DIM
CONV_WIDTH = D_MODEL - ATTN_WIDTH
CONV_K = 3
ROPE_THETA = 10000.0
RMS_EPS = 1e-6
NEG_INF = -1e30
IN_TOTAL = Q_LORA + KV_LORA + ROPE_DIM + ATTN_WIDTH + 4 * CONV_WIDTH
ADAM_LR = 0.001
ADAM_B1 = 0.9
ADAM_B2 = 0.999
ADAM_EPS = 1e-08
ADAM_WD = 0.01
ADAM_STEP = 10

N_DEV = 8
LANES = 128
HEAD_PAD = 256
QK_PAD = N_HEADS * HEAD_PAD
PROJ_PAD = 3072
TS = 512
TS_M = 256
TS_E = 256
TQ = 512
VMEM_LIMIT = 58 * 1024 * 1024
MESH = pl.DeviceIdType.MESH
QK_SCALE = 1.0 / math.sqrt(QK_DIM)
SCALE_LOG2E = QK_SCALE * math.log2(math.e)


def _cparams(sem=None):
    return pltpu.CompilerParams(dimension_semantics=sem, vmem_limit_bytes=VMEM_LIMIT)


def _nt(a, b):
    return lax.dot_general(a, b, (((1,), (1,)), ((), ())), preferred_element_type=f32)


def _tn(a, b):
    return lax.dot_general(a, b, (((0,), (0,)), ((), ())), preferred_element_type=f32)


def _nn(a, b):
    return jnp.dot(a, b, preferred_element_type=f32)


def _rstd(x, n):
    return lax.rsqrt(jnp.sum(x * x, axis=-1, keepdims=True) * (1.0 / n) + RMS_EPS)


def _rms_bwd(dy, xhat, r, g, n):
    u = dy * g
    return r * (u - xhat * (jnp.sum(u * xhat, axis=-1, keepdims=True) * (1.0 / n)))


def _rope(t, c, s1, s2):
    return t * c + pltpu.roll(t, 32, 1) * s1 + pltpu.roll(t, 96, 1) * s2


def _rope_t(d, c, s1, s2):
    return d * c - pltpu.roll(d, 96, 1) * s2 - pltpu.roll(d, 32, 1) * s1


def _const(shape):
    return pl.BlockSpec(shape, lambda *_: (0,) * len(shape), pipeline_mode=pl.Buffered(1))


def _rope_tables(posf, inv_freq):
    S = posf.shape[0]

    def body(pos_ref, f_ref, c_ref, s1_ref, s2_ref):
        ang = pos_ref[...] * f_ref[...]
        lane = lax.broadcasted_iota(jnp.int32, ang.shape, 1)
        cs, sn = jnp.cos(ang), jnp.sin(ang)
        c_ref[...] = jnp.where(lane < ROPE_DIM, cs, 0.0)
        s1_ref[...] = jnp.where((lane >= ROPE_DIM // 2) & (lane < ROPE_DIM), sn, 0.0)
        s2_ref[...] = jnp.where(lane < ROPE_DIM // 2, -sn, 0.0)

    t = min(S, TS)
    spec = pl.BlockSpec((t, LANES), lambda i: (i, 0))
    return pl.pallas_call(
        body, name="rope_tables", grid=(S // t,),
        out_shape=[jax.ShapeDtypeStruct((S, LANES), f32)] * 3,
        in_specs=[spec, _const((1, LANES))], out_specs=[spec] * 3,
        compiler_params=_cparams(("parallel",)),
    )(posf, inv_freq)


def _fwd_proj(x, tabs, g_in, w_in_p, g_cq, w_uq_p, g_ckv, w_ukv_p, g_q_p, g_k_p):
    S = x.shape[0]

    def body(x_ref, c_ref, s1_ref, s2_ref, g_in_ref, w_in_ref, g_cq_ref, w_uq_ref, g_ckv_ref, w_ukv_ref,
             g_q_ref, g_k_ref, proj_ref, q_ref, k_ref, v_ref):
        xv = x_ref[...]
        h = (xv * _rstd(xv, D_MODEL) * g_in_ref[...]).astype(MM)
        proj_ref[...] = _nn(h, w_in_ref[...])
        c, s1, s2 = c_ref[...], s1_ref[...], s2_ref[...]
        g_q, g_k = g_q_ref[...], g_k_ref[...]

        c_q = proj_ref[:, 0:Q_LORA]
        n_cq = (c_q * _rstd(c_q, Q_LORA) * g_cq_ref[...]).astype(MM)
        q_raw = _nn(n_cq, w_uq_ref[...])
        for hd in range(N_HEADS):
            qh = q_raw[:, hd * HEAD_PAD:(hd + 1) * HEAD_PAD]
            qn = qh * _rstd(qh, QK_DIM) * g_q
            q_ref[:, hd * HEAD_PAD:hd * HEAD_PAD + LANES] = qn[:, :LANES].astype(MM)
            q_ref[:, hd * HEAD_PAD + LANES:(hd + 1) * HEAD_PAD] = _rope(qn[:, LANES:], c, s1, s2).astype(MM)

        c_kv = proj_ref[:, Q_LORA:Q_LORA + KV_LORA]
        n_ckv = (c_kv * _rstd(c_kv, KV_LORA) * g_ckv_ref[...]).astype(MM)
        kv_raw = _nn(n_ckv, w_ukv_ref[...])
        kpe = proj_ref[:, Q_LORA + KV_LORA:Q_LORA + KV_LORA + LANES]
        ss_pe = jnp.sum(kpe * kpe, axis=-1, keepdims=True)
        kr = _rope(kpe * g_k[:, LANES:], c, s1, s2)
        for hd in range(N_HEADS):
            kn = kv_raw[:, hd * LANES:(hd + 1) * LANES]
            rk = lax.rsqrt((jnp.sum(kn * kn, axis=-1, keepdims=True) + ss_pe) * (1.0 / QK_DIM) + RMS_EPS)
            k_ref[:, hd * HEAD_PAD:hd * HEAD_PAD + LANES] = (kn * rk * g_k[:, :LANES]).astype(MM)
            k_ref[:, hd * HEAD_PAD + LANES:(hd + 1) * HEAD_PAD] = (kr * rk).astype(MM)
        v_ref[...] = kv_raw[:, N_HEADS * NOPE_DIM:].astype(MM)

    row = lambda w: pl.BlockSpec((TS, w), lambda i: (i, 0))
    return pl.pallas_call(
        body, name="fwd_proj", grid=(S // TS,),
        out_shape=[jax.ShapeDtypeStruct((S, PROJ_PAD), f32), jax.ShapeDtypeStruct((S, QK_PAD), MM),
                   jax.ShapeDtypeStruct((S, QK_PAD), MM), jax.ShapeDtypeStruct((S, ATTN_WIDTH), MM)],
        in_specs=[row(D_MODEL), row(LANES), row(LANES), row(LANES), _const((1, D_MODEL)), _const((D_MODEL, PROJ_PAD)),
                  _const((1, Q_LORA)), _const((Q_LORA, QK_PAD)), _const((1, KV_LORA)), _const((KV_LORA, 2 * ATTN_WIDTH)),
                  _const((1, HEAD_PAD)), _const((1, HEAD_PAD))],
        out_specs=[row(PROJ_PAD), row(QK_PAD), row(QK_PAD), row(ATTN_WIDTH)],
        compiler_params=_cparams(("parallel",)),
    )(x, *tabs, g_in, w_in_p, g_cq, w_uq_p, g_ckv, w_ukv_p, g_q_p, g_k_p)


def _flash_fwd(q, k, v):
    S = q.shape[0]
    T = min(TQ, S)

    def body(q_ref, k_ref, v_ref, o_ref, lse_ref, s_a, s_b, m_sc, l_sc, acc_sc):
        n = pl.program_id(1)
        m_sc[...] = jnp.full(m_sc.shape, NEG_INF, f32)
        l_sc[...] = jnp.zeros(l_sc.shape, f32)
        acc_sc[...] = jnp.zeros(acc_sc.shape, f32)
        qv = q_ref[...]

        def rows(j):
            return pl.ds(pl.multiple_of(j * T, T), T)

        def scores(j, dst):
            dst[...] = _nt(k_ref[rows(j), :], qv)

        def update(src, j, masked):
            s = src[...]
            if masked:
                k_i = lax.broadcasted_iota(jnp.int32, (T, T), 0)
                q_i = lax.broadcasted_iota(jnp.int32, (T, T), 1)
                s = jnp.where(k_i <= q_i, s, NEG_INF)
            m_prev = m_sc[...]
            m_new = jnp.maximum(m_prev, jnp.max(s, axis=0, keepdims=True))
            alpha = jnp.exp2((m_prev - m_new) * SCALE_LOG2E)
            p = jnp.exp2((s - m_new[0:1, :]) * SCALE_LOG2E)
            l_sc[...] = alpha * l_sc[...] + jnp.sum(p, axis=0, keepdims=True)
            acc_sc[...] = alpha[0:1, :] * acc_sc[...] + _tn(v_ref[rows(j), :], p.astype(MM))
            m_sc[...] = m_new

        scores(0, s_a)

        def pair(t, carry):
            j = 2 * t
            scores(j + 1, s_b)
            update(s_a, j, False)
            scores(j + 2, s_a)
            update(s_b, j + 1, False)
            return carry

        lax.fori_loop(0, n // 2, pair, 0)

        @pl.when(n % 2 == 0)
        def _():
            update(s_a, n, True)

        @pl.when(n % 2 == 1)
        def _():
            scores(n, s_b)
            update(s_a, n - 1, False)
            update(s_b, n, True)

        o_ref[...] = (acc_sc[...] / l_sc[0:1, :]).T
        lse_ref[...] = m_sc[...] * SCALE_LOG2E + jnp.log2(l_sc[...])

    return pl.pallas_call(
        body, name="flash_fwd", grid=(N_HEADS, S // T),
        out_shape=[jax.ShapeDtypeStruct((S, ATTN_WIDTH), f32), jax.ShapeDtypeStruct((N_HEADS * 8, S), f32)],
        in_specs=[pl.BlockSpec((T, HEAD_PAD), lambda h, i: (i, h)),
                  pl.BlockSpec((S, HEAD_PAD), lambda h, i: (0, h)),
                  pl.BlockSpec((S, V_DIM), lambda h, i: (0, h))],
        out_specs=[pl.BlockSpec((T, V_DIM), lambda h, i: (i, h)), pl.BlockSpec((8, T), lambda h, i: (h, i))],
        scratch_shapes=[pltpu.VMEM((T, T), f32), pltpu.VMEM((T, T), f32), pltpu.VMEM((8, T), f32), pltpu.VMEM((8, T), f32),
                        pltpu.VMEM((V_DIM, T), f32)],
        compiler_params=_cparams(("parallel", "parallel")),
    )(q, k, v)


def _shift_down(m, prev8, n):
    T = m.shape[0]
    rows = lax.broadcasted_iota(jnp.int32, m.shape, 0)
    head = jnp.tile(pltpu.roll(prev8, n, 0), (T // 8, 1))
    return jnp.where(rows >= n, pltpu.roll(m, n, 0), head)


def _shift_up(m, next8, n):
    T = m.shape[0]
    rows = lax.broadcasted_iota(jnp.int32, m.shape, 0)
    tail = jnp.tile(pltpu.roll(next8, 8 - n, 0), (T // 8, 1))
    return jnp.where(rows < T - n, pltpu.roll(m, T - n, 0), tail)


def _mid(o, proj, x, p, target, w_o, w_plg, w_pl, conv_w8, g_oa, g_oc, g_pl):
    S = x.shape[0]
    T = min(TS_M, S)
    nt = S // T

    def body(o_ref, za_ref, ccx_ref, cbzc_ref, prev_ref, x_ref, p_ref, t_ref, wo_ref, wplg_ref, wpl_ref, cw_ref,
             goa_ref, goc_ref, gpl_ref,
             dx2_ref, do_ref, delta_ref, dza_ref, dcbzc_ref, du_ref,
             dwo_ref, dwplg_ref, dwpl_ref, dcw_ref, dgoa_ref, dgoc_ref, dgpl_ref, loss_ref):
        i = pl.program_id(0)

        @pl.when(i == 0)
        def _():
            for r in (dwo_ref, dwplg_ref, dwpl_ref, dcw_ref, dgoa_ref, dgoc_ref, dgpl_ref, loss_ref):
                r[...] = jnp.zeros(r.shape, f32)

        o_v = o_ref[...]
        z_a = za_ref[...]
        cc, cx = ccx_ref[:, :CONV_WIDTH], ccx_ref[:, CONV_WIDTH:]
        cb, z_c = cbzc_ref[:, :CONV_WIDTH], cbzc_ref[:, CONV_WIDTH:]
        g_oa, g_oc, g_pl = goa_ref[...], goc_ref[...], gpl_ref[...]
        w0, w1, w2 = cw_ref[0:1, :], cw_ref[1:2, :], cw_ref[2:3, :]

        sa = jax.nn.sigmoid(z_a)
        silu_a = z_a * sa
        ga = o_v * silu_a
        ra = _rstd(ga, ATTN_WIDTH)
        gha = ga * ra
        m0 = cc * cx
        prev = prev_ref[:, :CONV_WIDTH] * prev_ref[:, CONV_WIDTH:] * (i > 0).astype(f32)
        m1 = _shift_down(m0, prev, 1)
        m2 = _shift_down(m0, prev, 2)
        u = w0 * m2 + w1 * m1 + w2 * m0
        sc = jax.nn.sigmoid(z_c)
        silu_c = z_c * sc
        gc = cb * u * silu_c
        rc = _rstd(gc, CONV_WIDTH)
        ghc = gc * rc
        ycat = jnp.concatenate([gha * g_oa, ghc * g_oc], axis=-1).astype(MM)
        x2 = x_ref[...] + _nn(ycat, wo_ref[...])
        r2 = _rstd(x2, D_MODEL)
        xh2 = x2 * r2
        n2 = (xh2 * g_pl).astype(MM)
        gate = jax.nn.sigmoid(_nn(n2, wplg_ref[...]))
        p_b = p_ref[...].astype(MM)
        pw = _nn(p_b, wpl_ref[...])
        err = x2 + gate * pw - t_ref[...]
        loss_ref[...] += jnp.sum(jnp.sum(err * err, axis=-1, keepdims=True), axis=0, keepdims=True) * (0.5 / D_MODEL)
        d_out = err * (1.0 / D_MODEL)
        dwpl_ref[...] += _tn(p_b, (d_out * gate).astype(MM))
        d_glog = (d_out * pw * gate * (1.0 - gate)).astype(MM)
        dwplg_ref[...] += _tn(n2, d_glog)
        d_n2 = _nt(d_glog, wplg_ref[...])
        dgpl_ref[...] += jnp.sum(d_n2 * xh2, axis=0, keepdims=True)
        d_x2 = d_out + _rms_bwd(d_n2, xh2, r2, g_pl, D_MODEL)
        dx2_ref[...] = d_x2
        d_x2b = d_x2.astype(MM)
        dwo_ref[...] += _tn(ycat, d_x2b)
        d_ycat = _nt(d_x2b, wo_ref[...])
        d_ya, d_yc = d_ycat[:, :ATTN_WIDTH], d_ycat[:, ATTN_WIDTH:]
        dgoa_ref[...] += jnp.sum(d_ya * gha, axis=0, keepdims=True)
        dgoc_ref[...] += jnp.sum(d_yc * ghc, axis=0, keepdims=True)
        d_ga = _rms_bwd(d_ya, gha, ra, g_oa, ATTN_WIDTH)
        d_gc = _rms_bwd(d_yc, ghc, rc, g_oc, CONV_WIDTH)
        d_o = d_ga * silu_a
        do_ref[...] = d_o.astype(MM)
        dza_ref[...] = (d_ga * o_v * (sa * (1.0 + z_a * (1.0 - sa)))).astype(MM)
        for hd in range(N_HEADS):
            cols = slice(hd * V_DIM, (hd + 1) * V_DIM)
            dl = jnp.sum(d_o[:, cols] * o_v[:, cols], axis=-1, keepdims=True)
            delta_ref[hd * 8:(hd + 1) * 8, :] = jnp.broadcast_to(dl, (T, LANES)).T[0:8, :]
        d_u = d_gc * cb * silu_c
        du_ref[...] = d_u
        dcbzc_ref[:, :CONV_WIDTH] = (d_gc * u * silu_c).astype(MM)
        dcbzc_ref[:, CONV_WIDTH:] = (d_gc * cb * u * (sc * (1.0 + z_c * (1.0 - sc)))).astype(MM)
        dcw_ref[0:1, :] += jnp.sum(d_u * m2, axis=0, keepdims=True)
        dcw_ref[1:2, :] += jnp.sum(d_u * m1, axis=0, keepdims=True)
        dcw_ref[2:3, :] += jnp.sum(d_u * m0, axis=0, keepdims=True)

    row = lambda w, blk=0: pl.BlockSpec((T, w), lambda i: (i, blk))
    acc = lambda shape: pl.BlockSpec(shape, lambda i: (0, 0))
    tb = T // 8
    sds = jax.ShapeDtypeStruct
    return pl.pallas_call(
        body, name="mid", grid=(nt,),
        out_shape=[sds((S, D_MODEL), f32), sds((S, ATTN_WIDTH), MM), sds((N_HEADS * 8, S), f32), sds((S, ATTN_WIDTH), MM),
                   sds((S, 2 * CONV_WIDTH), MM), sds((S, CONV_WIDTH), f32),
                   sds((D_MODEL, D_MODEL), f32), sds((D_MODEL, D_MODEL), f32), sds((PLE_DIM, D_MODEL), f32), sds((8, CONV_WIDTH), f32),
                   sds((1, ATTN_WIDTH), f32), sds((1, CONV_WIDTH), f32), sds((1, D_MODEL), f32), sds((1, LANES), f32)],
        in_specs=[row(ATTN_WIDTH), row(ATTN_WIDTH, 1), row(2 * CONV_WIDTH, 1), row(2 * CONV_WIDTH, 2),
                  pl.BlockSpec((8, 2 * CONV_WIDTH), lambda i: (jnp.maximum(i * tb - 1, 0), 1)),
                  row(D_MODEL), row(PLE_DIM), row(D_MODEL),
                  _const((D_MODEL, D_MODEL)), _const((D_MODEL, D_MODEL)), _const((PLE_DIM, D_MODEL)), _const((8, CONV_WIDTH)),
                  _const((1, ATTN_WIDTH)), _const((1, CONV_WIDTH)), _const((1, D_MODEL))],
        out_specs=[row(D_MODEL), row(ATTN_WIDTH), pl.BlockSpec((N_HEADS * 8, T), lambda i: (0, i)), row(ATTN_WIDTH),
                   row(2 * CONV_WIDTH), row(CONV_WIDTH),
                   acc((D_MODEL, D_MODEL)), acc((D_MODEL, D_MODEL)), acc((PLE_DIM, D_MODEL)), acc((8, CONV_WIDTH)),
                   acc((1, ATTN_WIDTH)), acc((1, CONV_WIDTH)), acc((1, D_MODEL)), acc((1, LANES))],
        compiler_params=_cparams(("arbitrary",)),
    )(o, proj, proj, proj, proj, x, p, target, w_o, w_plg, w_pl, conv_w8, g_oa, g_oc, g_pl)


def _flash_bwd(q, k, v, d_o, lse2, delta):
    S = q.shape[0]
    T = min(TQ, S)
    nq = S // T

    def body(k_ref, v_ref, q_ref, do_ref, lse_ref, delta_ref, dq_ref, dk_ref, dv_ref):
        j = pl.program_id(1)

        @pl.when(j == 0)
        def _():
            dq_ref[...] = jnp.zeros(dq_ref.shape, f32)

        dk_ref[...] = jnp.zeros(dk_ref.shape, f32)
        dv_ref[...] = jnp.zeros(dv_ref.shape, f32)
        kv, vv = k_ref[...], v_ref[...]

        def block(i, masked):
            rows = pl.ds(pl.multiple_of(i * T, T), T)
            qi, doi = q_ref[rows, :], do_ref[rows, :]
            s = _nt(kv, qi)
            if masked:
                k_i = lax.broadcasted_iota(jnp.int32, (T, T), 0)
                q_i = lax.broadcasted_iota(jnp.int32, (T, T), 1)
                s = jnp.where(k_i <= q_i, s, NEG_INF)
            p = jnp.exp2(s * SCALE_LOG2E - lse_ref[0:1, rows])
            dp = _nt(vv, doi)
            ds = (p * (dp - delta_ref[0:1, rows])).astype(MM)
            dv_ref[...] += _nn(p.astype(MM), doi)
            dk_ref[...] += _nn(ds, qi)
            dq_ref[rows, :] += _tn(ds, kv)

        block(j, True)

        def step(i, carry):
            block(i, False)
            return carry

        lax.fori_loop(j + 1, nq, step, 0)

    whole = lambda w: pl.BlockSpec((S, w), lambda h, j: (0, h))
    stat = pl.BlockSpec((8, S), lambda h, j: (h, 0))
    return pl.pallas_call(
        body, name="flash_bwd", grid=(N_HEADS, nq),
        out_shape=[jax.ShapeDtypeStruct((S, QK_PAD), f32), jax.ShapeDtypeStruct((S, QK_PAD), f32),
                   jax.ShapeDtypeStruct((S, ATTN_WIDTH), f32)],
        in_specs=[pl.BlockSpec((T, HEAD_PAD), lambda h, j: (j, h)), pl.BlockSpec((T, V_DIM), lambda h, j: (j, h)),
                  whole(HEAD_PAD), whole(V_DIM), stat, stat],
        out_specs=[whole(HEAD_PAD), pl.BlockSpec((T, HEAD_PAD), lambda h, j: (j, h)),
                   pl.BlockSpec((T, V_DIM), lambda h, j: (j, h))],
        compiler_params=_cparams(("arbitrary", "arbitrary")),
    )(k, v, q, d_o, lse2, delta)


def _bwd_proj(x, proj, d_q, d_k, d_v, d_za, d_cbzc, d_u, d_x2, tabs, g_in, w_in_p, g_cq, w_uq_p, g_ckv, w_ukv_p,
              g_q_p, g_k_p, conv_w8):
    S = x.shape[0]
    T = min(TS_E, S)
    nt = S // T

    def body(x_ref, cqkv_ref, ccx_ref, dq_ref, dk_ref, dv_ref, dza_ref, dcbzc_ref, du_ref, dun_ref, dx2_ref,
             c_ref, s1_ref, s2_ref, g_in_ref, w_in_ref, g_cq_ref, w_uq_ref, g_ckv_ref, w_ukv_ref, g_q_ref, g_k_ref, cw_ref,
             gx_ref, dwin_hbm, dwuq_ref, dwukv_ref, dgin_ref, dgcq_ref, dgckv_ref, dgq_ref, dgk_ref,
             dproj_sc, dqraw_sc, dkvraw_sc, dwin_sc, sem):
        i = pl.program_id(0)

        @pl.when(i == 0)
        def _():
            dwin_sc[...] = jnp.zeros(dwin_sc.shape, f32)
            for r in (dwuq_ref, dwukv_ref, dgin_ref, dgcq_ref, dgckv_ref, dgq_ref, dgk_ref):
                r[...] = jnp.zeros(r.shape, f32)

        c, s1, s2 = c_ref[...], s1_ref[...], s2_ref[...]
        g_q, g_k = g_q_ref[...], g_k_ref[...]
        g_cq, g_ckv, g_in = g_cq_ref[...], g_ckv_ref[...], g_in_ref[...]

        c_q = cqkv_ref[:, 0:Q_LORA]
        r_cq = _rstd(c_q, Q_LORA)
        cqh = c_q * r_cq
        n_cq = (cqh * g_cq).astype(MM)
        q_raw = _nn(n_cq, w_uq_ref[...])
        dgq = jnp.zeros((1, HEAD_PAD), f32)
        for hd in range(N_HEADS):
            cols = slice(hd * HEAD_PAD, (hd + 1) * HEAD_PAD)
            qh = q_raw[:, cols]
            rq = _rstd(qh, QK_DIM)
            qhh = qh * rq
            dqh = dq_ref[:, cols] * QK_SCALE
            d_qn = jnp.concatenate([dqh[:, :LANES], _rope_t(dqh[:, LANES:], c, s1, s2)], axis=-1)
            dgq += jnp.sum(d_qn * qhh, axis=0, keepdims=True)
            dqraw_sc[:, cols] = _rms_bwd(d_qn, qhh, rq, g_q, QK_DIM).astype(MM)
        dgq_ref[...] += dgq
        d_qraw = dqraw_sc[...]
        dwuq_ref[...] += _tn(n_cq, d_qraw)
        d_ncq = _nt(d_qraw, w_uq_ref[...])
        dgcq_ref[...] += jnp.sum(d_ncq * cqh, axis=0, keepdims=True)
        dproj_sc[:, 0:Q_LORA] = _rms_bwd(d_ncq, cqh, r_cq, g_cq, Q_LORA).astype(MM)

        c_kv = cqkv_ref[:, Q_LORA:Q_LORA + KV_LORA]
        r_ckv = _rstd(c_kv, KV_LORA)
        ckvh = c_kv * r_ckv
        n_ckv = (ckvh * g_ckv).astype(MM)
        k_nope = _nn(n_ckv, w_ukv_ref[:, :N_HEADS * NOPE_DIM])
        kpe = cqkv_ref[:, Q_LORA + KV_LORA:Q_LORA + KV_LORA + LANES]
        ss_pe = jnp.sum(kpe * kpe, axis=-1, keepdims=True)
        g_kn, g_kp = g_k[:, :LANES], g_k[:, LANES:]
        d_kpe = jnp.zeros((T, LANES), f32)
        dgk_n = jnp.zeros((1, LANES), f32)
        dgk_p = jnp.zeros((1, LANES), f32)
        for hd in range(N_HEADS):
            kn = k_nope[:, hd * LANES:(hd + 1) * LANES]
            rk = lax.rsqrt((jnp.sum(kn * kn, axis=-1, keepdims=True) + ss_pe) * (1.0 / QK_DIM) + RMS_EPS)
            knh, kph = kn * rk, kpe * rk
            d_kn_n = dk_ref[:, hd * HEAD_PAD:hd * HEAD_PAD + LANES] * QK_SCALE
            d_kr = _rope_t(dk_ref[:, hd * HEAD_PAD + LANES:(hd + 1) * HEAD_PAD] * QK_SCALE, c, s1, s2)
            dgk_n += jnp.sum(d_kn_n * knh, axis=0, keepdims=True)
            dgk_p += jnp.sum(d_kr * kph, axis=0, keepdims=True)
            u_n, u_p = d_kn_n * g_kn, d_kr * g_kp
            mt = (jnp.sum(u_n * knh, axis=-1, keepdims=True) + jnp.sum(u_p * kph, axis=-1, keepdims=True)) * (1.0 / QK_DIM)
            dkvraw_sc[:, hd * LANES:(hd + 1) * LANES] = (rk * (u_n - knh * mt)).astype(MM)
            d_kpe += rk * (u_p - kph * mt)
        dgk_ref[:, :LANES] += dgk_n
        dgk_ref[:, LANES:] += dgk_p
        dkvraw_sc[:, N_HEADS * NOPE_DIM:] = dv_ref[...].astype(MM)
        d_kvraw = dkvraw_sc[...]
        dwukv_ref[...] += _tn(n_ckv, d_kvraw)
        d_nckv = _nt(d_kvraw, w_ukv_ref[...])
        dgckv_ref[...] += jnp.sum(d_nckv * ckvh, axis=0, keepdims=True)
        dproj_sc[:, Q_LORA:Q_LORA + KV_LORA] = _rms_bwd(d_nckv, ckvh, r_ckv, g_ckv, KV_LORA).astype(MM)
        dproj_sc[:, Q_LORA + KV_LORA:Q_LORA + KV_LORA + LANES] = d_kpe.astype(MM)

        dproj_sc[:, 512:1024] = dza_ref[...]
        d_u = du_ref[...]
        nxt = dun_ref[...] * (i < nt - 1).astype(f32)
        d_m = cw_ref[2:3, :] * d_u + cw_ref[1:2, :] * _shift_up(d_u, nxt, 1) + cw_ref[0:1, :] * _shift_up(d_u, nxt, 2)
        dproj_sc[:, 1024:1536] = (d_m * ccx_ref[:, CONV_WIDTH:]).astype(MM)
        dproj_sc[:, 1536:2048] = (d_m * ccx_ref[:, :CONV_WIDTH]).astype(MM)
        dproj_sc[:, 2048:3072] = dcbzc_ref[...]

        xv = x_ref[...]
        r_in = _rstd(xv, D_MODEL)
        xh = xv * r_in
        hb = (xh * g_in).astype(MM)
        d_proj = dproj_sc[...]
        dwin_sc[...] += _tn(hb, d_proj)
        d_h = _nt(d_proj, w_in_ref[...])
        dgin_ref[...] += jnp.sum(d_h * xh, axis=0, keepdims=True)
        gx_ref[...] = dx2_ref[...] + _rms_bwd(d_h, xh, r_in, g_in, D_MODEL)

        @pl.when(i == nt - 1)
        def _():
            cp = pltpu.make_async_copy(dwin_sc, dwin_hbm, sem)
            cp.start()
            cp.wait()

    row = lambda w, blk=0: pl.BlockSpec((T, w), lambda i: (i, blk))
    acc = lambda shape: pl.BlockSpec(shape, lambda i: (0, 0))
    tb = T // 8
    sds = jax.ShapeDtypeStruct
    return pl.pallas_call(
        body, name="bwd_proj", grid=(nt,),
        out_shape=[sds((S, D_MODEL), f32), sds((D_MODEL, PROJ_PAD), f32), sds((Q_LORA, QK_PAD), f32),
                   sds((KV_LORA, 2 * ATTN_WIDTH), f32), sds((1, D_MODEL), f32), sds((1, Q_LORA), f32), sds((1, KV_LORA), f32),
                   sds((1, HEAD_PAD), f32), sds((1, HEAD_PAD), f32)],
        in_specs=[row(D_MODEL), row(512, 0), row(2 * CONV_WIDTH, 1), row(QK_PAD), row(QK_PAD), row(ATTN_WIDTH),
                  row(ATTN_WIDTH), row(2 * CONV_WIDTH), row(CONV_WIDTH),
                  pl.BlockSpec((8, CONV_WIDTH), lambda i: (jnp.minimum((i + 1) * tb, S // 8 - 1), 0)),
                  row(D_MODEL), row(LANES), row(LANES), row(LANES),
                  _const((1, D_MODEL)), _const((D_MODEL, PROJ_PAD)), _const((1, Q_LORA)), _const((Q_LORA, QK_PAD)),
                  _const((1, KV_LORA)), _const((KV_LORA, 2 * ATTN_WIDTH)), _const((1, HEAD_PAD)), _const((1, HEAD_PAD)),
                  _const((8, CONV_WIDTH))],
        out_specs=[row(D_MODEL), pl.BlockSpec(memory_space=pl.ANY), acc((Q_LORA, QK_PAD)), acc((KV_LORA, 2 * ATTN_WIDTH)),
                   acc((1, D_MODEL)), acc((1, Q_LORA)), acc((1, KV_LORA)), acc((1, HEAD_PAD)), acc((1, HEAD_PAD))],
        scratch_shapes=[pltpu.VMEM((T, PROJ_PAD), MM), pltpu.VMEM((T, QK_PAD), MM), pltpu.VMEM((T, 2 * ATTN_WIDTH), MM),
                        pltpu.VMEM((D_MODEL, PROJ_PAD), f32), pltpu.SemaphoreType.DMA],
        compiler_params=_cparams(("arbitrary",)),
    )(x, proj, proj, d_q, d_k, d_v, d_za, d_cbzc, d_u, d_u, d_x2, *tabs, g_in, w_in_p, g_cq, w_uq_p, g_ckv, w_ukv_p,
      g_q_p, g_k_p, conv_w8)


def _all_gather(block, name, in_vmem):
    m_per, n = block.shape

    def body(x_ref, out_ref, send_sems, recv_sems, local_sem):
        x, y, c = lax.axis_index("x"), lax.axis_index("y"), lax.axis_index("c")
        me, sibling = (x, y, c), (x, y, 1 - c)
        chips = [(1 - x, y), (x, 1 - y), (1 - x, 1 - y)]

        def rows(px, py, pc):
            return out_ref.at[pl.ds((4 * px + 2 * py + pc) * m_per, m_per), :]

        def copy(k, blk, to, src=None):
            return pltpu.make_async_remote_copy(
                src_ref=rows(*blk) if src is None else src, dst_ref=rows(*blk),
                send_sem=send_sems.at[k], recv_sem=recv_sems.at[k], device_id=to, device_id_type=MESH)

        mine = pltpu.make_async_copy(x_ref, rows(*me), local_sem)
        mine.start()
        first = [copy(0, me, sibling, src=x_ref)]
        first += [copy(1 + j, me, (*chip, c), src=x_ref) for j, chip in enumerate(chips)]
        for cp in first:
            cp.start()
        passed = [copy(4 + j, (*chip, c), sibling) for j, chip in enumerate(chips)]
        for j, chip in enumerate(chips):
            copy(1 + j, (*chip, c), me).wait_recv()
            passed[j].start()
        copy(0, sibling, me).wait_recv()
        for j, chip in enumerate(chips):
            copy(4 + j, (*chip, 1 - c), me).wait_recv()
        for cp in first + passed:
            cp.wait_send()
        mine.wait()

    space = pltpu.VMEM if in_vmem else pl.ANY
    return pl.pallas_call(
        body, name=name, out_shape=jax.ShapeDtypeStruct((N_DEV * m_per, n), block.dtype),
        in_specs=[pl.BlockSpec(memory_space=space)], out_specs=pl.BlockSpec(memory_space=space),
        scratch_shapes=[pltpu.SemaphoreType.DMA((7,)), pltpu.SemaphoreType.DMA((7,)), pltpu.SemaphoreType.DMA],
    )(block)


def _sibling_exchange(pack):
    _, nchip, R, n = pack.shape

    def body(p_ref, got_ref, send_sem, recv_sem):
        x, y, c = lax.axis_index("x"), lax.axis_index("y"), lax.axis_index("c")
        cp = pltpu.make_async_remote_copy(src_ref=p_ref.at[1 - c], dst_ref=got_ref, send_sem=send_sem, recv_sem=recv_sem,
                                          device_id=(x, y, 1 - c), device_id_type=MESH)
        cp.start()
        cp.wait()

    return pl.pallas_call(
        body, name="rs_sibling", out_shape=jax.ShapeDtypeStruct((nchip, R, n), pack.dtype),
        in_specs=[pl.BlockSpec(memory_space=pl.ANY)], out_specs=pl.BlockSpec(memory_space=pl.ANY),
        scratch_shapes=[pltpu.SemaphoreType.DMA, pltpu.SemaphoreType.DMA],
    )(pack)


def _chip_exchange(part):
    nchip, R, n = part.shape

    def body(p_ref, got_ref, send_sems, recv_sems, local_sem):
        x, y, c = lax.axis_index("x"), lax.axis_index("y"), lax.axis_index("c")
        my_chip = 2 * x + y
        chips = [(1 - x, y), (x, 1 - y), (1 - x, 1 - y)]
        mine = pltpu.make_async_copy(p_ref.at[my_chip], got_ref.at[my_chip], local_sem)
        mine.start()
        sends = []
        for k, (tx, ty) in enumerate(chips):
            cp = pltpu.make_async_remote_copy(src_ref=p_ref.at[2 * tx + ty], dst_ref=got_ref.at[my_chip],
                                              send_sem=send_sems.at[k], recv_sem=recv_sems.at[k],
                                              device_id=(tx, ty, c), device_id_type=MESH)
            cp.start()
            sends.append(cp)
        for k, (sx, sy) in enumerate(chips):
            pltpu.make_async_remote_copy(src_ref=p_ref.at[my_chip], dst_ref=got_ref.at[2 * sx + sy],
                                         send_sem=send_sems.at[k], recv_sem=recv_sems.at[k],
                                         device_id=(sx, sy, c), device_id_type=MESH).wait_recv()
        for cp in sends:
            cp.wait_send()
        mine.wait()

    return pl.pallas_call(
        body, name="rs_chips", out_shape=jax.ShapeDtypeStruct((nchip, R, n), part.dtype),
        in_specs=[pl.BlockSpec(memory_space=pl.ANY)], out_specs=pl.BlockSpec(memory_space=pl.ANY),
        scratch_shapes=[pltpu.SemaphoreType.DMA((3,)), pltpu.SemaphoreType.DMA((3,)), pltpu.SemaphoreType.DMA],
    )(part)


def _add_blocks(a, b):
    nb, R, n = a.shape
    t = R // 5 if R % 40 == 0 else R

    def body(a_ref, b_ref, o_ref):
        o_ref[...] = a_ref[...] + b_ref[...]

    spec = pl.BlockSpec((1, t, n), lambda i, j: (i, j, 0))
    return pl.pallas_call(body, name="rs_add", grid=(nb, R // t), out_shape=jax.ShapeDtypeStruct(a.shape, a.dtype),
                          in_specs=[spec, spec], out_specs=spec,
                          compiler_params=_cparams(("parallel", "parallel")))(a, b)


def _sum_chips(got):
    nb, R, n = got.shape
    t = R // 5 if R % 40 == 0 else R

    def body(g_ref, o_ref):
        o_ref[...] = ((g_ref[0] + g_ref[1]) + g_ref[2]) + g_ref[3]

    return pl.pallas_call(body, name="rs_sum", grid=(R // t,), out_shape=jax.ShapeDtypeStruct((R, n), got.dtype),
                          in_specs=[pl.BlockSpec((nb, t, n), lambda j: (0, j, 0))], out_specs=pl.BlockSpec((t, n), lambda j: (j, 0)),
                          compiler_params=_cparams(("parallel",)))(got)


def _adamw_all(ws, gs, ms, vs, gain_parts):
    n = len(ws)
    ng = len(GAIN_SLOTS)

    def body(*refs):
        gp_ref = refs[0]
        w_refs, g_refs, m_refs, v_refs = (refs[1 + k * n:1 + (k + 1) * n] for k in range(4))
        outs = refs[1 + 4 * n:]
        gsum_ref, loss_ref = outs[0], outs[1]
        gg_refs = outs[2:2 + ng]
        d_refs, nm_refs, nv_refs = (outs[2 + ng + k * n:2 + ng + (k + 1) * n] for k in range(3))
        tot = gp_ref[0]
        for d in range(1, N_DEV):
            tot = tot + gp_ref[d]
        gsum_ref[...] = tot
        loss_ref[...] = gsum_ref[GAIN_ROWS - 1:GAIN_ROWS, :]
        for k in range(n):
            if k < ng:
                r0, width = GAIN_SLOTS[k]
                nr = width // LANES
                for r in range(nr):
                    gg_refs[k][:, r * LANES:(r + 1) * LANES] = gsum_ref[r0 + r:r0 + r + 1, :]
                g = gg_refs[k][...]
            else:
                g = g_refs[k][...]
            w = w_refs[k][...]
            m = ADAM_B1 * m_refs[k][...] + (1.0 - ADAM_B1) * g
            v = ADAM_B2 * v_refs[k][...] + (1.0 - ADAM_B2) * (g * g)
            m_hat = m / (1.0 - ADAM_B1 ** ADAM_STEP)
            v_hat = v / (1.0 - ADAM_B2 ** ADAM_STEP)
            d_refs[k][...] = -ADAM_LR * (m_hat / (jnp.sqrt(v_hat) + ADAM_EPS) + ADAM_WD * w)
            nm_refs[k][...] = m
            nv_refs[k][...] = v

    sds = jax.ShapeDtypeStruct
    like = [sds(w.shape, f32) for w in ws]
    out_shape = [sds((GAIN_ROWS, LANES), f32), sds((1, LANES), f32)] + like[:ng] + like * 3
    vm = pl.BlockSpec(memory_space=pltpu.VMEM)
    outs = pl.pallas_call(
        body, name="adamw", out_shape=out_shape, in_specs=[vm] * (1 + 4 * n), out_specs=[vm] * len(out_shape),
        compiler_params=pltpu.CompilerParams(vmem_limit_bytes=VMEM_LIMIT),
    )(gain_parts, *ws, *gs, *ms, *vs)
    loss = outs[1]
    gg = outs[2:2 + ng]
    deltas, new_m, new_v = (outs[2 + ng + k * n:2 + ng + (k + 1) * n] for k in range(3))
    return loss, gg, deltas, new_m, new_v


GAIN_ROWS = 32
GAIN_SLOTS = [(0, 1024), (8, 256), (10, 128), (11, 256), (13, 256), (15, 512), (19, 512), (23, 1024)]

W_ROWS = [3008, 192, 128, 16, 1024, 256, 1024]
G_ROWS = [3008, 192, 128, 8, 1024, 256, 1024]


def _pad_rows(a, rows):
    flat = a.reshape(-1)
    return jnp.pad(flat, (0, rows * LANES - flat.shape[0])).reshape(rows, LANES)


def _split_rows(a, rows_list):
    out, r = [], 0
    for n in rows_list:
        out.append(a[..., r:r + n, :])
        r += n
    return out


def kernel(x, p, positions, g_in, w_in, g_cq, w_uq, g_ckv, w_ukv, g_q, g_k, conv_w, g_oa, g_oc, w_o, w_pl, w_plg, g_pl, loss_target, m_g_in, m_w_in, m_g_cq, m_w_uq, m_g_ckv, m_w_ukv, m_g_q, m_g_k, m_conv_w, m_g_oa, m_g_oc, m_w_o, m_w_pl, m_w_plg, m_g_pl, v_g_in, v_w_in, v_g_cq, v_w_uq, v_g_ckv, v_w_ukv, v_g_q, v_g_k, v_conv_w, v_g_oa, v_g_oc, v_w_o, v_w_pl, v_w_plg, v_g_pl):
    S = x.shape[1]
    nd = N_DEV
    xs, ps, tgt = x[0], p[0, 0], loss_target[0]

    conv_bits = lax.bitcast_convert_type(conv_w[0], bf16)
    shards = [w_in[0], w_uq[0], w_ukv[0], conv_bits, w_o[0], w_pl[0], w_plg[0]]
    pack = jnp.concatenate([_pad_rows(s.astype(bf16), r) for s, r in zip(shards, W_ROWS)], axis=0)
    R = pack.shape[0]
    gathered = _all_gather(pack, "ag_weights", in_vmem=False).reshape(nd, R, LANES)
    a_in, a_uq, a_ukv, a_cv, a_o, a_pl, a_plg = _split_rows(gathered, W_ROWS)

    def cols(a, rows, c):
        return a.reshape(nd, rows, c).transpose(1, 0, 2).reshape(rows, nd * c)

    win = cols(a_in, D_MODEL, IN_TOTAL // nd)
    zpad = jnp.zeros((D_MODEL, ROPE_DIM), bf16)
    w_in_p = jnp.concatenate([win[:, 0:448], zpad, win[:, 448:960], win[:, 1472:2496], win[:, 960:1472], win[:, 2496:3008]], axis=1)
    wuq = cols(a_uq, Q_LORA, N_HEADS * QK_DIM // nd).reshape(Q_LORA, N_HEADS, QK_DIM)
    w_uq_p = jnp.pad(wuq, ((0, 0), (0, 0), (0, HEAD_PAD - QK_DIM))).reshape(Q_LORA, QK_PAD)
    wukv = cols(a_ukv, KV_LORA, 2 * ATTN_WIDTH // nd).reshape(KV_LORA, N_HEADS, 2, NOPE_DIM)
    w_ukv_p = wukv.transpose(0, 2, 1, 3).reshape(KV_LORA, 2 * ATTN_WIDTH)
    cv_bits = a_cv.reshape(nd, -1)[:, :CONV_K * (CONV_WIDTH // nd) * 2].reshape(nd, CONV_K, CONV_WIDTH // nd, 2)
    conv_full = lax.bitcast_convert_type(cv_bits, f32).transpose(1, 0, 2).reshape(CONV_K, CONV_WIDTH)
    conv_w8 = jnp.pad(conv_full, ((0, 8 - CONV_K), (0, 0)))
    w_o_f = a_o.reshape(D_MODEL, D_MODEL)
    w_pl_f = cols(a_pl, PLE_DIM, D_MODEL // nd)
    w_plg_f = a_plg.reshape(D_MODEL, D_MODEL)
    w_in_p, w_uq_p, w_ukv_p, w_o_f, w_pl_f, w_plg_f = (w.astype(MM) for w in (w_in_p, w_uq_p, w_ukv_p, w_o_f, w_pl_f, w_plg_f))

    g_q_p = jnp.pad(g_q, ((0, 0), (0, HEAD_PAD - QK_DIM)))
    g_k_p = jnp.pad(g_k, ((0, 0), (0, HEAD_PAD - QK_DIM)))

    inv_freq = 1.0 / (ROPE_THETA ** (jnp.arange(0, ROPE_DIM, 2, dtype=f32) / ROPE_DIM))
    inv_row = jnp.concatenate([inv_freq, inv_freq, jnp.zeros((LANES - ROPE_DIM,), f32)]).reshape(1, LANES)
    posf = jnp.broadcast_to(positions[0].astype(f32)[:, None], (S, LANES))
    tabs = _rope_tables(posf, inv_row)

    proj, q, k, v = _fwd_proj(xs, tabs, g_in, w_in_p, g_cq, w_uq_p, g_ckv, w_ukv_p, g_q_p, g_k_p)
    o, lse = _flash_fwd(q, k, v)
    (d_x2, d_o, delta, d_za, d_cbzc, d_u, dw_o, dw_plg, dw_pl, dcw, dg_oa, dg_oc, dg_pl, loss_part) = _mid(
        o, proj, xs, ps, tgt, w_o_f, w_plg_f, w_pl_f, conv_w8, g_oa, g_oc, g_pl)
    d_q, d_k, d_v = _flash_bwd(q, k, v, d_o, lse, delta)
    (grad_x, dw_in_p, dw_uq_p, dw_ukv_p, dg_in, dg_cq, dg_ckv, dg_q_p, dg_k_p) = _bwd_proj(
        xs, proj, d_q, d_k, d_v, d_za, d_cbzc, d_u, d_x2, tabs, g_in, w_in_p, g_cq, w_uq_p, g_ckv, w_ukv_p, g_q_p, g_k_p, conv_w8)

    dw_in = jnp.concatenate([dw_in_p[:, 0:448], dw_in_p[:, 512:1024], dw_in_p[:, 2048:2560], dw_in_p[:, 1024:2048],
                             dw_in_p[:, 2560:3072]], axis=1)
    dw_uq = dw_uq_p.reshape(Q_LORA, N_HEADS, HEAD_PAD)[:, :, :QK_DIM].reshape(Q_LORA, N_HEADS * QK_DIM)
    dw_ukv = dw_ukv_p.reshape(KV_LORA, 2, N_HEADS, NOPE_DIM).transpose(0, 2, 1, 3).reshape(KV_LORA, 2 * ATTN_WIDTH)
    dconv = dcw[:CONV_K]

    def shard_cols(a):
        rows = a.shape[0]
        return a.reshape(rows, nd, -1).transpose(1, 0, 2).reshape(nd, -1)

    def pack_rows(flat, rows):
        return jnp.pad(flat, ((0, 0), (0, rows * LANES - flat.shape[1]))).reshape(nd, rows, LANES)

    flats = [shard_cols(dw_in), shard_cols(dw_uq), shard_cols(dw_ukv), shard_cols(dconv), dw_o.reshape(nd, -1),
             shard_cols(dw_pl), dw_plg.reshape(nd, -1)]
    gpack = jnp.concatenate([pack_rows(fl, r) for fl, r in zip(flats, G_ROWS)], axis=1)
    Rg = gpack.shape[1]
    gpack = gpack.reshape(4, 2, Rg, LANES).transpose(1, 0, 2, 3)
    my_c = lax.axis_index("c")
    from_sib = _sibling_exchange(gpack)
    mine = lax.dynamic_index_in_dim(gpack, my_c, axis=0, keepdims=False)
    chip_part = _add_blocks(mine, from_sib)
    gshard = _sum_chips(_chip_exchange(chip_part))
    s_in, s_uq, s_ukv, s_cv, s_o, s_pl, s_plg = _split_rows(gshard, G_ROWS)
    grads_w = [s_in.reshape(D_MODEL, -1), s_uq.reshape(Q_LORA, -1), s_ukv.reshape(KV_LORA, -1),
               s_cv.reshape(-1)[:CONV_K * CONV_WIDTH // nd].reshape(CONV_K, -1), s_o.reshape(-1, D_MODEL),
               s_pl.reshape(PLE_DIM, -1), s_plg.reshape(-1, D_MODEL)]

    gains_part = [dg_in, dg_cq, dg_ckv, dg_q_p, dg_k_p, dg_oa, dg_oc, dg_pl]
    gflat = jnp.concatenate([g.reshape(-1) for g in gains_part] + [jnp.zeros(((GAIN_ROWS - 1) * LANES - 3968,), f32),
                                                                    loss_part.reshape(-1)])
    gain_parts = _all_gather(gflat.reshape(GAIN_ROWS, LANES), "ag_gains", in_vmem=True).reshape(nd, GAIN_ROWS, LANES)

    gains = [g_in, g_cq, g_ckv, g_q_p, g_k_p, g_oa, g_oc, g_pl]
    padq = lambda a: jnp.pad(a, ((0, 0), (0, HEAD_PAD - QK_DIM)))
    m_gains = [m_g_in, m_g_cq, m_g_ckv, padq(m_g_q), padq(m_g_k), m_g_oa, m_g_oc, m_g_pl]
    v_gains = [v_g_in, v_g_cq, v_g_ckv, padq(v_g_q), padq(v_g_k), v_g_oa, v_g_oc, v_g_pl]
    ws = gains + [w_in[0], w_uq[0], w_ukv[0], conv_w[0], w_o[0], w_pl[0], w_plg[0]]
    ms = m_gains + [m_w_in[0], m_w_uq[0], m_w_ukv[0], m_conv_w[0], m_w_o[0], m_w_pl[0], m_w_plg[0]]
    vs = v_gains + [v_w_in[0], v_w_uq[0], v_w_ukv[0], v_conv_w[0], v_w_o[0], v_w_pl[0], v_w_plg[0]]
    gs = [jnp.zeros_like(g) for g in gains] + grads_w
    loss_row, gg, deltas, new_m, new_v = _adamw_all(ws, gs, ms, vs, gain_parts)
    loss = loss_row[0, 0]

    def ordered(gl, wl):
        g_in_, g_cq_, g_ckv_, g_q_, g_k_, g_oa_, g_oc_, g_pl_ = gl
        g_q_, g_k_ = g_q_[:, :QK_DIM], g_k_[:, :QK_DIM]
        w_in_, w_uq_, w_ukv_, cw_, w_o_, w_pl_, w_plg_ = [w[None] for w in wl]
        return [g_in_, w_in_, g_cq_, w_uq_, g_ckv_, w_ukv_, g_q_, g_k_, cw_, g_oa_, g_oc_, w_o_, w_pl_, w_plg_, g_pl_]

    ng = len(gains)
    outs = [loss, grad_x[None]]
    outs += ordered(gg, grads_w)
    for lst in (deltas, new_m, new_v):
        outs += ordered(lst[:ng], lst[ng:])
    return tuple(outs)
```

```python
import functools
import math

import jax
import jax.numpy as jnp
from jax import lax
from jax.experimental import pallas as pl
from jax.experimental.pallas import tpu as pltpu

f32 = jnp.float32
bf16 = jnp.bfloat16
MM = jnp.bfloat16
RS_DTYPE = jnp.bfloat16

D_MODEL = 1024
PLE_DIM = 256
N_HEADS = 4
NOPE_DIM = 128
ROPE_DIM = 64
V_DIM = 128
QK_DIM = NOPE_DIM + ROPE_DIM
Q_LORA = 256
KV_LORA = 128
ATTN_WIDTH = N_HEADS * V_DIM
CONV_WIDTH = D_MODEL - ATTN_WIDTH
CONV_K = 3
ROPE_THETA = 10000.0
RMS_EPS = 1e-6
NEG_INF = -1e30
IN_TOTAL = Q_LORA + KV_LORA + ROPE_DIM + ATTN_WIDTH + 4 * CONV_WIDTH
ADAM_LR = 0.001
ADAM_B1 = 0.9
ADAM_B2 = 0.999
ADAM_EPS = 1e-08
ADAM_WD = 0.01
ADAM_STEP = 10

N_DEV = 8
LANES = 128
HEAD_PAD = 256
QK_PAD = N_HEADS * HEAD_PAD
PROJ_PAD = 3072
TS = 512
TS_M = 256
TS_E = 256
TQ = 512
VMEM_LIMIT = 58 * 1024 * 1024
MESH = pl.DeviceIdType.MESH
QK_SCALE = 1.0 / math.sqrt(QK_DIM)
SCALE_LOG2E = QK_SCALE * math.log2(math.e)


def _cparams(sem=None):
    return pltpu.CompilerParams(dimension_semantics=sem, vmem_limit_bytes=VMEM_LIMIT)


def _nt(a, b):
    return lax.dot_general(a, b, (((1,), (1,)), ((), ())), preferred_element_type=f32)


def _tn(a, b):
    return lax.dot_general(a, b, (((0,), (0,)), ((), ())), preferred_element_type=f32)


def _nn(a, b):
    return jnp.dot(a, b, preferred_element_type=f32)


def _rstd(x, n):
    return lax.rsqrt(jnp.sum(x * x, axis=-1, keepdims=True) * (1.0 / n) + RMS_EPS)


def _rms_bwd(dy, xhat, r, g, n):
    u = dy * g
    return r * (u - xhat * (jnp.sum(u * xhat, axis=-1, keepdims=True) * (1.0 / n)))


def _rope(t, c, s1, s2):
    return t * c + pltpu.roll(t, 32, 1) * s1 + pltpu.roll(t, 96, 1) * s2


def _rope_t(d, c, s1, s2):
    return d * c - pltpu.roll(d, 96, 1) * s2 - pltpu.roll(d, 32, 1) * s1


def _const(shape):
    return pl.BlockSpec(shape, lambda *_: (0,) * len(shape), pipeline_mode=pl.Buffered(1))


def _rope_tables(posf, inv_freq):
    S = posf.shape[0]

    def body(pos_ref, f_ref, c_ref, s1_ref, s2_ref):
        ang = pos_ref[...] * f_ref[...]
        lane = lax.broadcasted_iota(jnp.int32, ang.shape, 1)
        cs, sn = jnp.cos(ang), jnp.sin(ang)
        c_ref[...] = jnp.where(lane < ROPE_DIM, cs, 0.0)
        s1_ref[...] = jnp.where((lane >= ROPE_DIM // 2) & (lane < ROPE_DIM), sn, 0.0)
        s2_ref[...] = jnp.where(lane < ROPE_DIM // 2, -sn, 0.0)

    t = min(S, TS)
    spec = pl.BlockSpec((t, LANES), lambda i: (i, 0))
    return pl.pallas_call(
        body, name="rope_tables", grid=(S // t,),
        out_shape=[jax.ShapeDtypeStruct((S, LANES), f32)] * 3,
        in_specs=[spec, _const((1, LANES))], out_specs=[spec] * 3,
        compiler_params=_cparams(("parallel",)),
    )(posf, inv_freq)


def _fwd_proj(x, tabs, g_in, w_in_p, g_cq, w_uq_p, g_ckv, w_ukv_p, g_q_p, g_k_p):
    S = x.shape[0]

    def body(x_ref, c_ref, s1_ref, s2_ref, g_in_ref, w_in_ref, g_cq_ref, w_uq_ref, g_ckv_ref, w_ukv_ref,
             g_q_ref, g_k_ref, proj_ref, q_ref, k_ref, v_ref):
        xv = x_ref[...]
        h = (xv * _rstd(xv, D_MODEL) * g_in_ref[...]).astype(MM)
        proj_ref[...] = _nn(h, w_in_ref[...])
        c, s1, s2 = c_ref[...], s1_ref[...], s2_ref[...]
        g_q, g_k = g_q_ref[...], g_k_ref[...]

        c_q = proj_ref[:, 0:Q_LORA]
        n_cq = (c_q * _rstd(c_q, Q_LORA) * g_cq_ref[...]).astype(MM)
        q_raw = _nn(n_cq, w_uq_ref[...])
        for hd in range(N_HEADS):
            qh = q_raw[:, hd * HEAD_PAD:(hd + 1) * HEAD_PAD]
            qn = qh * _rstd(qh, QK_DIM) * g_q
            q_ref[:, hd * HEAD_PAD:hd * HEAD_PAD + LANES] = qn[:, :LANES].astype(MM)
            q_ref[:, hd * HEAD_PAD + LANES:(hd + 1) * HEAD_PAD] = _rope(qn[:, LANES:], c, s1, s2).astype(MM)

        c_kv = proj_ref[:, Q_LORA:Q_LORA + KV_LORA]
        n_ckv = (c_kv * _rstd(c_kv, KV_LORA) * g_ckv_ref[...]).astype(MM)
        kv_raw = _nn(n_ckv, w_ukv_ref[...])
        kpe = proj_ref[:, Q_LORA + KV_LORA:Q_LORA + KV_LORA + LANES]
        ss_pe = jnp.sum(kpe * kpe, axis=-1, keepdims=True)
        kr = _rope(kpe * g_k[:, LANES:], c, s1, s2)
        for hd in range(N_HEADS):
            kn = kv_raw[:, hd * LANES:(hd + 1) * LANES]
            rk = lax.rsqrt((jnp.sum(kn * kn, axis=-1, keepdims=True) + ss_pe) * (1.0 / QK_DIM) + RMS_EPS)
            k_ref[:, hd * HEAD_PAD:hd * HEAD_PAD + LANES] = (kn * rk * g_k[:, :LANES]).astype(MM)
            k_ref[:, hd * HEAD_PAD + LANES:(hd + 1) * HEAD_PAD] = (kr * rk).astype(MM)
        v_ref[...] = kv_raw[:, N_HEADS * NOPE_DIM:].astype(MM)

    row = lambda w: pl.BlockSpec((TS, w), lambda i: (i, 0))
    return pl.pallas_call(
        body, name="fwd_proj", grid=(S // TS,),
        out_shape=[jax.ShapeDtypeStruct((S, PROJ_PAD), f32), jax.ShapeDtypeStruct((S, QK_PAD), MM),
                   jax.ShapeDtypeStruct((S, QK_PAD), MM), jax.ShapeDtypeStruct((S, ATTN_WIDTH), MM)],
        in_specs=[row(D_MODEL), row(LANES), row(LANES), row(LANES), _const((1, D_MODEL)), _const((D_MODEL, PROJ_PAD)),
                  _const((1, Q_LORA)), _const((Q_LORA, QK_PAD)), _const((1, KV_LORA)), _const((KV_LORA, 2 * ATTN_WIDTH)),
                  _const((1, HEAD_PAD)), _const((1, HEAD_PAD))],
        out_specs=[row(PROJ_PAD), row(QK_PAD), row(QK_PAD), row(ATTN_WIDTH)],
        compiler_params=_cparams(("parallel",)),
    )(x, *tabs, g_in, w_in_p, g_cq, w_uq_p, g_ckv, w_ukv_p, g_q_p, g_k_p)


def _flash_fwd(q, k, v):
    S = q.shape[0]
    T = min(TQ, S)

    def body(q_ref, k_ref, v_ref, o_ref, lse_ref, s_a, s_b, m_sc, l_sc, acc_sc):
        n = pl.program_id(1)
        m_sc[...] = jnp.full(m_sc.shape, NEG_INF, f32)
        l_sc[...] = jnp.zeros(l_sc.shape, f32)
        acc_sc[...] = jnp.zeros(acc_sc.shape, f32)
        qv = q_ref[...]

        def rows(j):
            return pl.ds(pl.multiple_of(j * T, T), T)

        def scores(j, dst):
            dst[...] = _nt(k_ref[rows(j), :], qv)

        def update(src, j, masked):
            s = src[...]
            if masked:
                k_i = lax.broadcasted_iota(jnp.int32, (T, T), 0)
                q_i = lax.broadcasted_iota(jnp.int32, (T, T), 1)
                s = jnp.where(k_i <= q_i, s, NEG_INF)
            m_prev = m_sc[...]
            m_new = jnp.maximum(m_prev, jnp.max(s, axis=0, keepdims=True))
            alpha = jnp.exp2((m_prev - m_new) * SCALE_LOG2E)
            p = jnp.exp2((s - m_new[0:1, :]) * SCALE_LOG2E)
            l_sc[...] = alpha * l_sc[...] + jnp.sum(p, axis=0, keepdims=True)
            acc_sc[...] = alpha[0:1, :] * acc_sc[...] + _tn(v_ref[rows(j), :], p.astype(MM))
            m_sc[...] = m_new

        scores(0, s_a)

        def pair(t, carry):
            j = 2 * t
            scores(j + 1, s_b)
            update(s_a, j, False)
            scores(j + 2, s_a)
            update(s_b, j + 1, False)
            return carry

        lax.fori_loop(0, n // 2, pair, 0)

        @pl.when(n % 2 == 0)
        def _():
            update(s_a, n, True)

        @pl.when(n % 2 == 1)
        def _():
            scores(n, s_b)
            update(s_a, n - 1, False)
            update(s_b, n, True)

        o_ref[...] = (acc_sc[...] / l_sc[0:1, :]).T
        lse_ref[...] = m_sc[...] * SCALE_LOG2E + jnp.log2(l_sc[...])

    return pl.pallas_call(
        body, name="flash_fwd", grid=(N_HEADS, S // T),
        out_shape=[jax.ShapeDtypeStruct((S, ATTN_WIDTH), f32), jax.ShapeDtypeStruct((N_HEADS * 8, S), f32)],
        in_specs=[pl.BlockSpec((T, HEAD_PAD), lambda h, i: (i, h)),
                  pl.BlockSpec((S, HEAD_PAD), lambda h, i: (0, h)),
                  pl.BlockSpec((S, V_DIM), lambda h, i: (0, h))],
        out_specs=[pl.BlockSpec((T, V_DIM), lambda h, i: (i, h)), pl.BlockSpec((8, T), lambda h, i: (h, i))],
        scratch_shapes=[pltpu.VMEM((T, T), f32), pltpu.VMEM((T, T), f32), pltpu.VMEM((8, T), f32), pltpu.VMEM((8, T), f32),
                        pltpu.VMEM((V_DIM, T), f32)],
        compiler_params=_cparams(("parallel", "parallel")),
    )(q, k, v)


def _shift_down(m, prev8, n):
    T = m.shape[0]
    rows = lax.broadcasted_iota(jnp.int32, m.shape, 0)
    head = jnp.tile(pltpu.roll(prev8, n, 0), (T // 8, 1))
    return jnp.where(rows >= n, pltpu.roll(m, n, 0), head)


def _shift_up(m, next8, n):
    T = m.shape[0]
    rows = lax.broadcasted_iota(jnp.int32, m.shape, 0)
    tail = jnp.tile(pltpu.roll(next8, 8 - n, 0), (T // 8, 1))
    return jnp.where(rows < T - n, pltpu.roll(m, T - n, 0), tail)


def _mid(o, proj, x, p, target, w_o, w_plg, w_pl, conv_w8, g_oa, g_oc, g_pl):
    S = x.shape[0]
    T = min(TS_M, S)
    nt = S // T

    def body(o_ref, za_ref, ccx_ref, cbzc_ref, prev_ref, x_ref, p_ref, t_ref, wo_ref, wplg_ref, wpl_ref, cw_ref,
             goa_ref, goc_ref, gpl_ref,
             dx2_ref, do_ref, delta_ref, dza_ref, dcbzc_ref, du_ref,
             dwo_ref, dwplg_ref, dwpl_ref, dcw_ref, dgoa_ref, dgoc_ref, dgpl_ref, loss_ref):
        i = pl.program_id(0)

        @pl.when(i == 0)
        def _():
            for r in (dwo_ref, dwplg_ref, dwpl_ref, dcw_ref, dgoa_ref, dgoc_ref, dgpl_ref, loss_ref):
                r[...] = jnp.zeros(r.shape, f32)

        o_v = o_ref[...]
        z_a = za_ref[...]
        cc, cx = ccx_ref[:, :CONV_WIDTH], ccx_ref[:, CONV_WIDTH:]
        cb, z_c = cbzc_ref[:, :CONV_WIDTH], cbzc_ref[:, CONV_WIDTH:]
        g_oa, g_oc, g_pl = goa_ref[...], goc_ref[...], gpl_ref[...]
        w0, w1, w2 = cw_ref[0:1, :], cw_ref[1:2, :], cw_ref[2:3, :]

        sa = jax.nn.sigmoid(z_a)
        silu_a = z_a * sa
        ga = o_v * silu_a
        ra = _rstd(ga, ATTN_WIDTH)
        gha = ga * ra
        m0 = cc * cx
        prev = prev_ref[:, :CONV_WIDTH] * prev_ref[:, CONV_WIDTH:] * (i > 0).astype(f32)
        m1 = _shift_down(m0, prev, 1)
        m2 = _shift_down(m0, prev, 2)
        u = w0 * m2 + w1 * m1 + w2 * m0
        sc = jax.nn.sigmoid(z_c)
        silu_c = z_c * sc
        gc = cb * u * silu_c
        rc = _rstd(gc, CONV_WIDTH)
        ghc = gc * rc
        ycat = jnp.concatenate([gha * g_oa, ghc * g_oc], axis=-1).astype(MM)
        x2 = x_ref[...] + _nn(ycat, wo_ref[...])
        r2 = _rstd(x2, D_MODEL)
        xh2 = x2 * r2
        n2 = (xh2 * g_pl).astype(MM)
        gate = jax.nn.sigmoid(_nn(n2, wplg_ref[...]))
        p_b = p_ref[...].astype(MM)
        pw = _nn(p_b, wpl_ref[...])
        err = x2 + gate * pw - t_ref[...]
        loss_ref[...] += jnp.sum(jnp.sum(err * err, axis=-1, keepdims=True), axis=0, keepdims=True) * (0.5 / D_MODEL)
        d_out = err * (1.0 / D_MODEL)
        dwpl_ref[...] += _tn(p_b, (d_out * gate).astype(MM))
        d_glog = (d_out * pw * gate * (1.0 - gate)).astype(MM)
        dwplg_ref[...] += _tn(n2, d_glog)
        d_n2 = _nt(d_glog, wplg_ref[...])
        dgpl_ref[...] += jnp.sum(d_n2 * xh2, axis=0, keepdims=True)
        d_x2 = d_out + _rms_bwd(d_n2, xh2, r2, g_pl, D_MODEL)
        dx2_ref[...] = d_x2
        d_x2b = d_x2.astype(MM)
        dwo_ref[...] += _tn(ycat, d_x2b)
        d_ycat = _nt(d_x2b, wo_ref[...])
        d_ya, d_yc = d_ycat[:, :ATTN_WIDTH], d_ycat[:, ATTN_WIDTH:]
        dgoa_ref[...] += jnp.sum(d_ya * gha, axis=0, keepdims=True)
        dgoc_ref[...] += jnp.sum(d_yc * ghc, axis=0, keepdims=True)
        d_ga = _rms_bwd(d_ya, gha, ra, g_oa, ATTN_WIDTH)
        d_gc = _rms_bwd(d_yc, ghc, rc, g_oc, CONV_WIDTH)
        d_o = d_ga * silu_a
        do_ref[...] = d_o.astype(MM)
        dza_ref[...] = (d_ga * o_v * (sa * (1.0 + z_a * (1.0 - sa)))).astype(MM)
        for hd in range(N_HEADS):
            cols = slice(hd * V_DIM, (hd + 1) * V_DIM)
            dl = jnp.sum(d_o[:, cols] * o_v[:, cols], axis=-1, keepdims=True)
            delta_ref[hd * 8:(hd + 1) * 8, :] = jnp.broadcast_to(dl, (T, LANES)).T[0:8, :]
        d_u = d_gc * cb * silu_c
        du_ref[...] = d_u
        dcbzc_ref[:, :CONV_WIDTH] = (d_gc * u * silu_c).astype(MM)
        dcbzc_ref[:, CONV_WIDTH:] = (d_gc * cb * u * (sc * (1.0 + z_c * (1.0 - sc)))).astype(MM)
        dcw_ref[0:1, :] += jnp.sum(d_u * m2, axis=0, keepdims=True)
        dcw_ref[1:2, :] += jnp.sum(d_u * m1, axis=0, keepdims=True)
        dcw_ref[2:3, :] += jnp.sum(d_u * m0, axis=0, keepdims=True)

    row = lambda w, blk=0: pl.BlockSpec((T, w), lambda i: (i, blk))
    acc = lambda shape: pl.BlockSpec(shape, lambda i: (0, 0))
    tb = T // 8
    sds = jax.ShapeDtypeStruct
    return pl.pallas_call(
        body, name="mid", grid=(nt,),
        out_shape=[sds((S, D_MODEL), f32), sds((S, ATTN_WIDTH), MM), sds((N_HEADS * 8, S), f32), sds((S, ATTN_WIDTH), MM),
                   sds((S, 2 * CONV_WIDTH), MM), sds((S, CONV_WIDTH), f32),
                   sds((D_MODEL, D_MODEL), f32), sds((D_MODEL, D_MODEL), f32), sds((PLE_DIM, D_MODEL), f32), sds((8, CONV_WIDTH), f32),
                   sds((1, ATTN_WIDTH), f32), sds((1, CONV_WIDTH), f32), sds((1, D_MODEL), f32), sds((1, LANES), f32)],
        in_specs=[row(ATTN_WIDTH), row(ATTN_WIDTH, 1), row(2 * CONV_WIDTH, 1), row(2 * CONV_WIDTH, 2),
                  pl.BlockSpec((8, 2 * CONV_WIDTH), lambda i: (jnp.maximum(i * tb - 1, 0), 1)),
                  row(D_MODEL), row(PLE_DIM), row(D_MODEL),
                  _const((D_MODEL, D_MODEL)), _const((D_MODEL, D_MODEL)), _const((PLE_DIM, D_MODEL)), _const((8, CONV_WIDTH)),
                  _const((1, ATTN_WIDTH)), _const((1, CONV_WIDTH)), _const((1, D_MODEL))],
        out_specs=[row(D_MODEL), row(ATTN_WIDTH), pl.BlockSpec((N_HEADS * 8, T), lambda i: (0, i)), row(ATTN_WIDTH),
                   row(2 * CONV_WIDTH), row(CONV_WIDTH),
                   acc((D_MODEL, D_MODEL)), acc((D_MODEL, D_MODEL)), acc((PLE_DIM, D_MODEL)), acc((8, CONV_WIDTH)),
                   acc((1, ATTN_WIDTH)), acc((1, CONV_WIDTH)), acc((1, D_MODEL)), acc((1, LANES))],
        compiler_params=_cparams(("arbitrary",)),
    )(o, proj, proj, proj, proj, x, p, target, w_o, w_plg, w_pl, conv_w8, g_oa, g_oc, g_pl)


def _flash_bwd(q, k, v, d_o, lse2, delta):
    S = q.shape[0]
    T = min(TQ, S)
    nq = S // T

    def body(k_ref, v_ref, q_ref, do_ref, lse_ref, delta_ref, dq_ref, dk_ref, dv_ref, s_a, dp_a, s_b, dp_b):
        j = pl.program_id(1)

        @pl.when(j == 0)
        def _():
            dq_ref[...] = jnp.zeros(dq_ref.shape, f32)

        dk_ref[...] = jnp.zeros(dk_ref.shape, f32)
        dv_ref[...] = jnp.zeros(dv_ref.shape, f32)
        kv, vv = k_ref[...], v_ref[...]

        def rows_of(i):
            return pl.ds(pl.multiple_of(i * T, T), T)

        def scores(i, s_dst, dp_dst):
            rows = rows_of(i)
            s_dst[...] = _nt(kv, q_ref[rows, :])
            dp_dst[...] = _nt(vv, do_ref[rows, :])

        def grads(i, s_src, dp_src, masked):
            rows = rows_of(i)
            s = s_src[...]
            if masked:
                k_i = lax.broadcasted_iota(jnp.int32, (T, T), 0)
                q_i = lax.broadcasted_iota(jnp.int32, (T, T), 1)
                s = jnp.where(k_i <= q_i, s, NEG_INF)
            p = jnp.exp2(s * SCALE_LOG2E - lse_ref[0:1, rows])
            ds = (p * (dp_src[...] - delta_ref[0:1, rows])).astype(MM)
            dv_ref[...] += _nn(p.astype(MM), do_ref[rows, :])
            dk_ref[...] += _nn(ds, q_ref[rows, :])
            dq_ref[rows, :] += _tn(ds, kv)

        last = nq - 1
        scores(j, s_a, dp_a)
        scores(jnp.minimum(j + 1, last), s_b, dp_b)
        grads(j, s_a, dp_a, True)
        rest = last - j

        def pair(t, carry):
            u = j + 1 + 2 * t
            scores(u + 1, s_a, dp_a)
            grads(u, s_b, dp_b, False)
            scores(jnp.minimum(u + 2, last), s_b, dp_b)
            grads(u + 1, s_a, dp_a, False)
            return carry

        lax.fori_loop(0, rest // 2, pair, 0)

        @pl.when(rest % 2 == 1)
        def _():
            grads(last, s_b, dp_b, False)

    whole = lambda w: pl.BlockSpec((S, w), lambda h, j: (0, h))
    stat = pl.BlockSpec((8, S), lambda h, j: (h, 0))
    return pl.pallas_call(
        body, name="flash_bwd", grid=(N_HEADS, nq),
        out_shape=[jax.ShapeDtypeStruct((S, QK_PAD), f32), jax.ShapeDtypeStruct((S, QK_PAD), f32),
                   jax.ShapeDtypeStruct((S, ATTN_WIDTH), f32)],
        in_specs=[pl.BlockSpec((T, HEAD_PAD), lambda h, j: (j, h)), pl.BlockSpec((T, V_DIM), lambda h, j: (j, h)),
                  whole(HEAD_PAD), whole(V_DIM), stat, stat],
        out_specs=[whole(HEAD_PAD), pl.BlockSpec((T, HEAD_PAD), lambda h, j: (j, h)),
                   pl.BlockSpec((T, V_DIM), lambda h, j: (j, h))],
        scratch_shapes=[pltpu.VMEM((T, T), f32)] * 4,
        compiler_params=_cparams(("arbitrary", "arbitrary")),
    )(k, v, q, d_o, lse2, delta)


def _bwd_proj(x, proj, d_q, d_k, d_v, d_za, d_cbzc, d_u, d_x2, tabs, g_in, w_in_p, g_cq, w_uq_p, g_ckv, w_ukv_p,
              g_q_p, g_k_p, conv_w8):
    S = x.shape[0]
    T = min(TS_E, S)
    nt = S // T

    def body(x_ref, cqkv_ref, ccx_ref, dq_ref, dk_ref, dv_ref, dza_ref, dcbzc_ref, du_ref, dun_ref, dx2_ref,
             c_ref, s1_ref, s2_ref, g_in_ref, w_in_ref, g_cq_ref, w_uq_ref, g_ckv_ref, w_ukv_ref, g_q_ref, g_k_ref, cw_ref,
             gx_ref, dwin_hbm, dwuq_ref, dwukv_ref, dgin_ref, dgcq_ref, dgckv_ref, dgq_ref, dgk_ref,
             dproj_sc, dqraw_sc, dkvraw_sc, dwin_sc, sem):
        i = pl.program_id(0)

        @pl.when(i == 0)
        def _():
            dwin_sc[...] = jnp.zeros(dwin_sc.shape, f32)
            for r in (dwuq_ref, dwukv_ref, dgin_ref, dgcq_ref, dgckv_ref, dgq_ref, dgk_ref):
                r[...] = jnp.zeros(r.shape, f32)

        c, s1, s2 = c_ref[...], s1_ref[...], s2_ref[...]
        g_q, g_k = g_q_ref[...], g_k_ref[...]
        g_cq, g_ckv, g_in = g_cq_ref[...], g_ckv_ref[...], g_in_ref[...]

        c_q = cqkv_ref[:, 0:Q_LORA]
        r_cq = _rstd(c_q, Q_LORA)
        cqh = c_q * r_cq
        n_cq = (cqh * g_cq).astype(MM)
        q_raw = _nn(n_cq, w_uq_ref[...])
        dgq = jnp.zeros((1, HEAD_PAD), f32)
        for hd in range(N_HEADS):
            cols = slice(hd * HEAD_PAD, (hd + 1) * HEAD_PAD)
            qh = q_raw[:, cols]
            rq = _rstd(qh, QK_DIM)
            qhh = qh * rq
            dqh = dq_ref[:, cols] * QK_SCALE
            d_qn = jnp.concatenate([dqh[:, :LANES], _rope_t(dqh[:, LANES:], c, s1, s2)], axis=-1)
            dgq += jnp.sum(d_qn * qhh, axis=0, keepdims=True)
            dqraw_sc[:, cols] = _rms_bwd(d_qn, qhh, rq, g_q, QK_DIM).astype(MM)
        dgq_ref[...] += dgq
        d_qraw = dqraw_sc[...]
        dwuq_ref[...] += _tn(n_cq, d_qraw)
        d_ncq = _nt(d_qraw, w_uq_ref[...])
        dgcq_ref[...] += jnp.sum(d_ncq * cqh, axis=0, keepdims=True)
        dproj_sc[:, 0:Q_LORA] = _rms_bwd(d_ncq, cqh, r_cq, g_cq, Q_LORA).astype(MM)

        c_kv = cqkv_ref[:, Q_LORA:Q_LORA + KV_LORA]
        r_ckv = _rstd(c_kv, KV_LORA)
        ckvh = c_kv * r_ckv
        n_ckv = (ckvh * g_ckv).astype(MM)
        k_nope = _nn(n_ckv, w_ukv_ref[:, :N_HEADS * NOPE_DIM])
        kpe = cqkv_ref[:, Q_LORA + KV_LORA:Q_LORA + KV_LORA + LANES]
        ss_pe = jnp.sum(kpe * kpe, axis=-1, keepdims=True)
        g_kn, g_kp = g_k[:, :LANES], g_k[:, LANES:]
        d_kpe = jnp.zeros((T, LANES), f32)
        dgk_n = jnp.zeros((1, LANES), f32)
        dgk_p = jnp.zeros((1, LANES), f32)
        for hd in range(N_HEADS):
            kn = k_nope[:, hd * LANES:(hd + 1) * LANES]
            rk = lax.rsqrt((jnp.sum(kn * kn, axis=-1, keepdims=True) + ss_pe) * (1.0 / QK_DIM) + RMS_EPS)
            knh, kph = kn * rk, kpe * rk
            d_kn_n = dk_ref[:, hd * HEAD_PAD:hd * HEAD_PAD + LANES] * QK_SCALE
            d_kr = _rope_t(dk_ref[:, hd * HEAD_PAD + LANES:(hd + 1) * HEAD_PAD] * QK_SCALE, c, s1, s2)
            dgk_n += jnp.sum(d_kn_n * knh, axis=0, keepdims=True)
            dgk_p += jnp.sum(d_kr * kph, axis=0, keepdims=True)
            u_n, u_p = d_kn_n * g_kn, d_kr * g_kp
            mt = (jnp.sum(u_n * knh, axis=-1, keepdims=True) + jnp.sum(u_p * kph, axis=-1, keepdims=True)) * (1.0 / QK_DIM)
            dkvraw_sc[:, hd * LANES:(hd + 1) * LANES] = (rk * (u_n - knh * mt)).astype(MM)
            d_kpe += rk * (u_p - kph * mt)
        dgk_ref[:, :LANES] += dgk_n
        dgk_ref[:, LANES:] += dgk_p
        dkvraw_sc[:, N_HEADS * NOPE_DIM:] = dv_ref[...].astype(MM)
        d_kvraw = dkvraw_sc[...]
        dwukv_ref[...] += _tn(n_ckv, d_kvraw)
        d_nckv = _nt(d_kvraw, w_ukv_ref[...])
        dgckv_ref[...] += jnp.sum(d_nckv * ckvh, axis=0, keepdims=True)
        dproj_sc[:, Q_LORA:Q_LORA + KV_LORA] = _rms_bwd(d_nckv, ckvh, r_ckv, g_ckv, KV_LORA).astype(MM)
        dproj_sc[:, Q_LORA + KV_LORA:Q_LORA + KV_LORA + LANES] = d_kpe.astype(MM)

        dproj_sc[:, 512:1024] = dza_ref[...]
        d_u = du_ref[...]
        nxt = dun_ref[...] * (i < nt - 1).astype(f32)
        d_m = cw_ref[2:3, :] * d_u + cw_ref[1:2, :] * _shift_up(d_u, nxt, 1) + cw_ref[0:1, :] * _shift_up(d_u, nxt, 2)
        dproj_sc[:, 1024:1536] = (d_m * ccx_ref[:, CONV_WIDTH:]).astype(MM)
        dproj_sc[:, 1536:2048] = (d_m * ccx_ref[:, :CONV_WIDTH]).astype(MM)
        dproj_sc[:, 2048:3072] = dcbzc_ref[...]

        xv = x_ref[...]
        r_in = _rstd(xv, D_MODEL)
        xh = xv * r_in
        hb = (xh * g_in).astype(MM)
        d_proj = dproj_sc[...]
        dwin_sc[...] += _tn(hb, d_proj)
        d_h = _nt(d_proj, w_in_ref[...])
        dgin_ref[...] += jnp.sum(d_h * xh, axis=0, keepdims=True)
        gx_ref[...] = dx2_ref[...] + _rms_bwd(d_h, xh, r_in, g_in, D_MODEL)

        @pl.when(i == nt - 1)
        def _():
            cp = pltpu.make_async_copy(dwin_sc, dwin_hbm, sem)
            cp.start()
            cp.wait()

    row = lambda w, blk=0: pl.BlockSpec((T, w), lambda i: (i, blk))
    acc = lambda shape: pl.BlockSpec(shape, lambda i: (0, 0))
    tb = T // 8
    sds = jax.ShapeDtypeStruct
    return pl.pallas_call(
        body, name="bwd_proj", grid=(nt,),
        out_shape=[sds((S, D_MODEL), f32), sds((D_MODEL, PROJ_PAD), f32), sds((Q_LORA, QK_PAD), f32),
                   sds((KV_LORA, 2 * ATTN_WIDTH), f32), sds((1, D_MODEL), f32), sds((1, Q_LORA), f32), sds((1, KV_LORA), f32),
                   sds((1, HEAD_PAD), f32), sds((1, HEAD_PAD), f32)],
        in_specs=[row(D_MODEL), row(512, 0), row(2 * CONV_WIDTH, 1), row(QK_PAD), row(QK_PAD), row(ATTN_WIDTH),
                  row(ATTN_WIDTH), row(2 * CONV_WIDTH), row(CONV_WIDTH),
                  pl.BlockSpec((8, CONV_WIDTH), lambda i: (jnp.minimum((i + 1) * tb, S // 8 - 1), 0)),
                  row(D_MODEL), row(LANES), row(LANES), row(LANES),
                  _const((1, D_MODEL)), _const((D_MODEL, PROJ_PAD)), _const((1, Q_LORA)), _const((Q_LORA, QK_PAD)),
                  _const((1, KV_LORA)), _const((KV_LORA, 2 * ATTN_WIDTH)), _const((1, HEAD_PAD)), _const((1, HEAD_PAD)),
                  _const((8, CONV_WIDTH))],
        out_specs=[row(D_MODEL), pl.BlockSpec(memory_space=pl.ANY), acc((Q_LORA, QK_PAD)), acc((KV_LORA, 2 * ATTN_WIDTH)),
                   acc((1, D_MODEL)), acc((1, Q_LORA)), acc((1, KV_LORA)), acc((1, HEAD_PAD)), acc((1, HEAD_PAD))],
        scratch_shapes=[pltpu.VMEM((T, PROJ_PAD), MM), pltpu.VMEM((T, QK_PAD), MM), pltpu.VMEM((T, 2 * ATTN_WIDTH), MM),
                        pltpu.VMEM((D_MODEL, PROJ_PAD), f32), pltpu.SemaphoreType.DMA],
        compiler_params=_cparams(("arbitrary",)),
    )(x, proj, proj, d_q, d_k, d_v, d_za, d_cbzc, d_u, d_u, d_x2, *tabs, g_in, w_in_p, g_cq, w_uq_p, g_ckv, w_ukv_p,
      g_q_p, g_k_p, conv_w8)


def _all_gather(block, name, in_vmem):
    m_per, n = block.shape

    def body(x_ref, out_ref, send_sems, recv_sems, local_sem):
        x, y, c = lax.axis_index("x"), lax.axis_index("y"), lax.axis_index("c")
        me, sibling = (x, y, c), (x, y, 1 - c)
        chips = [(1 - x, y), (x, 1 - y), (1 - x, 1 - y)]

        def rows(px, py, pc):
            return out_ref.at[pl.ds((4 * px + 2 * py + pc) * m_per, m_per), :]

        def copy(k, blk, to, src=None):
            return pltpu.make_async_remote_copy(
                src_ref=rows(*blk) if src is None else src, dst_ref=rows(*blk),
                send_sem=send_sems.at[k], recv_sem=recv_sems.at[k], device_id=to, device_id_type=MESH)

        mine = pltpu.make_async_copy(x_ref, rows(*me), local_sem)
        mine.start()
        first = [copy(0, me, sibling, src=x_ref)]
        first += [copy(1 + j, me, (*chip, c), src=x_ref) for j, chip in enumerate(chips)]
        for cp in first:
            cp.start()
        passed = [copy(4 + j, (*chip, c), sibling) for j, chip in enumerate(chips)]
        for j, chip in enumerate(chips):
            copy(1 + j, (*chip, c), me).wait_recv()
            passed[j].start()
        copy(0, sibling, me).wait_recv()
        for j, chip in enumerate(chips):
            copy(4 + j, (*chip, 1 - c), me).wait_recv()
        for cp in first + passed:
            cp.wait_send()
        mine.wait()

    space = pltpu.VMEM if in_vmem else pl.ANY
    return pl.pallas_call(
        body, name=name, out_shape=jax.ShapeDtypeStruct((N_DEV * m_per, n), block.dtype),
        in_specs=[pl.BlockSpec(memory_space=space)], out_specs=pl.BlockSpec(memory_space=space),
        scratch_shapes=[pltpu.SemaphoreType.DMA((7,)), pltpu.SemaphoreType.DMA((7,)), pltpu.SemaphoreType.DMA],
    )(block)


def _sibling_exchange(pack):
    _, nchip, R, n = pack.shape

    def body(p_ref, got_ref, send_sem, recv_sem):
        x, y, c = lax.axis_index("x"), lax.axis_index("y"), lax.axis_index("c")
        cp = pltpu.make_async_remote_copy(src_ref=p_ref.at[1 - c], dst_ref=got_ref, send_sem=send_sem, recv_sem=recv_sem,
                                          device_id=(x, y, 1 - c), device_id_type=MESH)
        cp.start()
        cp.wait()

    return pl.pallas_call(
        body, name="rs_sibling", out_shape=jax.ShapeDtypeStruct((nchip, R, n), pack.dtype),
        in_specs=[pl.BlockSpec(memory_space=pl.ANY)], out_specs=pl.BlockSpec(memory_space=pl.ANY),
        scratch_shapes=[pltpu.SemaphoreType.DMA, pltpu.SemaphoreType.DMA],
    )(pack)


def _chip_exchange(part):
    nchip, R, n = part.shape

    def body(p_ref, got_ref, send_sems, recv_sems, local_sem):
        x, y, c = lax.axis_index("x"), lax.axis_index("y"), lax.axis_index("c")
        my_chip = 2 * x + y
        chips = [(1 - x, y), (x, 1 - y), (1 - x, 1 - y)]
        mine = pltpu.make_async_copy(p_ref.at[my_chip], got_ref.at[my_chip], local_sem)
        mine.start()
        sends = []
        for k, (tx, ty) in enumerate(chips):
            cp = pltpu.make_async_remote_copy(src_ref=p_ref.at[2 * tx + ty], dst_ref=got_ref.at[my_chip],
                                              send_sem=send_sems.at[k], recv_sem=recv_sems.at[k],
                                              device_id=(tx, ty, c), device_id_type=MESH)
            cp.start()
            sends.append(cp)
        for k, (sx, sy) in enumerate(chips):
            pltpu.make_async_remote_copy(src_ref=p_ref.at[my_chip], dst_ref=got_ref.at[2 * sx + sy],
                                         send_sem=send_sems.at[k], recv_sem=recv_sems.at[k],
                                         device_id=(sx, sy, c), device_id_type=MESH).wait_recv()
        for cp in sends:
            cp.wait_send()
        mine.wait()

    return pl.pallas_call(
        body, name="rs_chips", out_shape=jax.ShapeDtypeStruct((nchip, R, n), part.dtype),
        in_specs=[pl.BlockSpec(memory_space=pl.ANY)], out_specs=pl.BlockSpec(memory_space=pl.ANY),
        scratch_shapes=[pltpu.SemaphoreType.DMA((3,)), pltpu.SemaphoreType.DMA((3,)), pltpu.SemaphoreType.DMA],
    )(part)


def _add_blocks(a, b):
    nb, R, n = a.shape
    t = R // 5 if R % 80 == 0 else R

    def body(a_ref, b_ref, o_ref):
        o_ref[...] = (a_ref[...] + b_ref[...]).astype(o_ref.dtype)

    spec = pl.BlockSpec((1, t, n), lambda i, j: (i, j, 0))
    return pl.pallas_call(body, name="rs_add", grid=(nb, R // t), out_shape=jax.ShapeDtypeStruct(a.shape, RS_DTYPE),
                          in_specs=[spec, spec], out_specs=spec,
                          compiler_params=_cparams(("parallel", "parallel")))(a, b)


def _sum_chips(got):
    nb, R, n = got.shape
    t = R // 5 if R % 80 == 0 else R

    def body(g_ref, o_ref):
        g = g_ref[...].astype(f32)
        o_ref[...] = ((g[0] + g[1]) + g[2]) + g[3]

    return pl.pallas_call(body, name="rs_sum", grid=(R // t,), out_shape=jax.ShapeDtypeStruct((R, n), f32),
                          in_specs=[pl.BlockSpec((nb, t, n), lambda j: (0, j, 0))], out_specs=pl.BlockSpec((t, n), lambda j: (j, 0)),
                          compiler_params=_cparams(("parallel",)))(got)


def _adamw_all(ws, gs, ms, vs, gain_parts):
    n = len(ws)
    ng = len(GAIN_SLOTS)

    def body(*refs):
        gp_ref = refs[0]
        w_refs, g_refs, m_refs, v_refs = (refs[1 + k * n:1 + (k + 1) * n] for k in range(4))
        outs = refs[1 + 4 * n:]
        gsum_ref, loss_ref = outs[0], outs[1]
        gg_refs = outs[2:2 + ng]
        d_refs, nm_refs, nv_refs = (outs[2 + ng + k * n:2 + ng + (k + 1) * n] for k in range(3))
        tot = gp_ref[0]
        for d in range(1, N_DEV):
            tot = tot + gp_ref[d]
        gsum_ref[...] = tot
        loss_ref[...] = gsum_ref[GAIN_ROWS - 1:GAIN_ROWS, :]
        for k in range(n):
            if k < ng:
                r0, width = GAIN_SLOTS[k]
                nr = width // LANES
                for r in range(nr):
                    gg_refs[k][:, r * LANES:(r + 1) * LANES] = gsum_ref[r0 + r:r0 + r + 1, :]
                g = gg_refs[k][...]
            else:
                g = g_refs[k][...]
            w = w_refs[k][...]
            m = ADAM_B1 * m_refs[k][...] + (1.0 - ADAM_B1) * g
            v = ADAM_B2 * v_refs[k][...] + (1.0 - ADAM_B2) * (g * g)
            m_hat = m / (1.0 - ADAM_B1 ** ADAM_STEP)
            v_hat = v / (1.0 - ADAM_B2 ** ADAM_STEP)
            d_refs[k][...] = -ADAM_LR * (m_hat / (jnp.sqrt(v_hat) + ADAM_EPS) + ADAM_WD * w)
            nm_refs[k][...] = m
            nv_refs[k][...] = v

    sds = jax.ShapeDtypeStruct
    like = [sds(w.shape, f32) for w in ws]
    out_shape = [sds((GAIN_ROWS, LANES), f32), sds((1, LANES), f32)] + like[:ng] + like * 3
    vm = pl.BlockSpec(memory_space=pltpu.VMEM)
    outs = pl.pallas_call(
        body, name="adamw", out_shape=out_shape, in_specs=[vm] * (1 + 4 * n), out_specs=[vm] * len(out_shape),
        compiler_params=pltpu.CompilerParams(vmem_limit_bytes=VMEM_LIMIT),
    )(gain_parts, *ws, *gs, *ms, *vs)
    loss = outs[1]
    gg = outs[2:2 + ng]
    deltas, new_m, new_v = (outs[2 + ng + k * n:2 + ng + (k + 1) * n] for k in range(3))
    return loss, gg, deltas, new_m, new_v


GAIN_ROWS = 32
GAIN_SLOTS = [(0, 1024), (8, 256), (10, 128), (11, 256), (13, 256), (15, 512), (19, 512), (23, 1024)]

W_ROWS = [3008, 192, 128, 16, 1024, 256, 1024]
G_ROWS = [3008, 192, 128, 48, 1024, 256, 1024]


def _pad_rows(a, rows):
    flat = a.reshape(-1)
    return jnp.pad(flat, (0, rows * LANES - flat.shape[0])).reshape(rows, LANES)


def _split_rows(a, rows_list):
    out, r = [], 0
    for n in rows_list:
        out.append(a[..., r:r + n, :])
        r += n
    return out


def kernel(x, p, positions, g_in, w_in, g_cq, w_uq, g_ckv, w_ukv, g_q, g_k, conv_w, g_oa, g_oc, w_o, w_pl, w_plg, g_pl, loss_target, m_g_in, m_w_in, m_g_cq, m_w_uq, m_g_ckv, m_w_ukv, m_g_q, m_g_k, m_conv_w, m_g_oa, m_g_oc, m_w_o, m_w_pl, m_w_plg, m_g_pl, v_g_in, v_w_in, v_g_cq, v_w_uq, v_g_ckv, v_w_ukv, v_g_q, v_g_k, v_conv_w, v_g_oa, v_g_oc, v_w_o, v_w_pl, v_w_plg, v_g_pl):
    S = x.shape[1]
    nd = N_DEV
    xs, ps, tgt = x[0], p[0, 0], loss_target[0]

    conv_bits = lax.bitcast_convert_type(conv_w[0], bf16)
    shards = [w_in[0], w_uq[0], w_ukv[0], conv_bits, w_o[0], w_pl[0], w_plg[0]]
    pack = jnp.concatenate([_pad_rows(s.astype(bf16), r) for s, r in zip(shards, W_ROWS)], axis=0)
    R = pack.shape[0]
    gathered = _all_gather(pack, "ag_weights", in_vmem=False).reshape(nd, R, LANES)
    a_in, a_uq, a_ukv, a_cv, a_o, a_pl, a_plg = _split_rows(gathered, W_ROWS)

    def cols(a, rows, c):
        return a.reshape(nd, rows, c).transpose(1, 0, 2).reshape(rows, nd * c)

    win = cols(a_in, D_MODEL, IN_TOTAL // nd)
    zpad = jnp.zeros((D_MODEL, ROPE_DIM), bf16)
    w_in_p = jnp.concatenate([win[:, 0:448], zpad, win[:, 448:960], win[:, 1472:2496], win[:, 960:1472], win[:, 2496:3008]], axis=1)
    wuq = cols(a_uq, Q_LORA, N_HEADS * QK_DIM // nd).reshape(Q_LORA, N_HEADS, QK_DIM)
    w_uq_p = jnp.pad(wuq, ((0, 0), (0, 0), (0, HEAD_PAD - QK_DIM))).reshape(Q_LORA, QK_PAD)
    wukv = cols(a_ukv, KV_LORA, 2 * ATTN_WIDTH // nd).reshape(KV_LORA, N_HEADS, 2, NOPE_DIM)
    w_ukv_p = wukv.transpose(0, 2, 1, 3).reshape(KV_LORA, 2 * ATTN_WIDTH)
    cv_bits = a_cv.reshape(nd, -1)[:, :CONV_K * (CONV_WIDTH // nd) * 2].reshape(nd, CONV_K, CONV_WIDTH // nd, 2)
    conv_full = lax.bitcast_convert_type(cv_bits, f32).transpose(1, 0, 2).reshape(CONV_K, CONV_WIDTH)
    conv_w8 = jnp.pad(conv_full, ((0, 8 - CONV_K), (0, 0)))
    w_o_f = a_o.reshape(D_MODEL, D_MODEL)
    w_pl_f = cols(a_pl, PLE_DIM, D_MODEL // nd)
    w_plg_f = a_plg.reshape(D_MODEL, D_MODEL)
    w_in_p, w_uq_p, w_ukv_p, w_o_f, w_pl_f, w_plg_f = (w.astype(MM) for w in (w_in_p, w_uq_p, w_ukv_p, w_o_f, w_pl_f, w_plg_f))

    g_q_p = jnp.pad(g_q, ((0, 0), (0, HEAD_PAD - QK_DIM)))
    g_k_p = jnp.pad(g_k, ((0, 0), (0, HEAD_PAD - QK_DIM)))

    inv_freq = 1.0 / (ROPE_THETA ** (jnp.arange(0, ROPE_DIM, 2, dtype=f32) / ROPE_DIM))
    inv_row = jnp.concatenate([inv_freq, inv_freq, jnp.zeros((LANES - ROPE_DIM,), f32)]).reshape(1, LANES)
    posf = jnp.broadcast_to(positions[0].astype(f32)[:, None], (S, LANES))
    tabs = _rope_tables(posf, inv_row)

    proj, q, k, v = _fwd_proj(xs, tabs, g_in, w_in_p, g_cq, w_uq_p, g_ckv, w_ukv_p, g_q_p, g_k_p)
    o, lse = _flash_fwd(q, k, v)
    (d_x2, d_o, delta, d_za, d_cbzc, d_u, dw_o, dw_plg, dw_pl, dcw, dg_oa, dg_oc, dg_pl, loss_part) = _mid(
        o, proj, xs, ps, tgt, w_o_f, w_plg_f, w_pl_f, conv_w8, g_oa, g_oc, g_pl)
    d_q, d_k, d_v = _flash_bwd(q, k, v, d_o, lse, delta)
    (grad_x, dw_in_p, dw_uq_p, dw_ukv_p, dg_in, dg_cq, dg_ckv, dg_q_p, dg_k_p) = _bwd_proj(
        xs, proj, d_q, d_k, d_v, d_za, d_cbzc, d_u, d_x2, tabs, g_in, w_in_p, g_cq, w_uq_p, g_ckv, w_ukv_p, g_q_p, g_k_p, conv_w8)

    dw_in = jnp.concatenate([dw_in_p[:, 0:448], dw_in_p[:, 512:1024], dw_in_p[:, 2048:2560], dw_in_p[:, 1024:2048],
                             dw_in_p[:, 2560:3072]], axis=1)
    dw_uq = dw_uq_p.reshape(Q_LORA, N_HEADS, HEAD_PAD)[:, :, :QK_DIM].reshape(Q_LORA, N_HEADS * QK_DIM)
    dw_ukv = dw_ukv_p.reshape(KV_LORA, 2, N_HEADS, NOPE_DIM).transpose(0, 2, 1, 3).reshape(KV_LORA, 2 * ATTN_WIDTH)
    dconv = dcw[:CONV_K]

    def shard_cols(a):
        rows = a.shape[0]
        return a.reshape(rows, nd, -1).transpose(1, 0, 2).reshape(nd, -1)

    def pack_rows(flat, rows):
        return jnp.pad(flat, ((0, 0), (0, rows * LANES - flat.shape[1]))).reshape(nd, rows, LANES)

    flats = [shard_cols(dw_in), shard_cols(dw_uq), shard_cols(dw_ukv), shard_cols(dconv), dw_o.reshape(nd, -1),
             shard_cols(dw_pl), dw_plg.reshape(nd, -1)]
    gpack = jnp.concatenate([pack_rows(fl, r) for fl, r in zip(flats, G_ROWS)], axis=1)
    Rg = gpack.shape[1]
    gpack = gpack.reshape(4, 2, Rg, LANES).transpose(1, 0, 2, 3)
    my_c = lax.axis_index("c")
    from_sib = _sibling_exchange(gpack)
    mine = lax.dynamic_index_in_dim(gpack, my_c, axis=0, keepdims=False)
    chip_part = _add_blocks(mine, from_sib)
    gshard = _sum_chips(_chip_exchange(chip_part))
    s_in, s_uq, s_ukv, s_cv, s_o, s_pl, s_plg = _split_rows(gshard, G_ROWS)
    grads_w = [s_in.reshape(D_MODEL, -1), s_uq.reshape(Q_LORA, -1), s_ukv.reshape(KV_LORA, -1),
               s_cv.reshape(-1)[:CONV_K * CONV_WIDTH // nd].reshape(CONV_K, -1), s_o.reshape(-1, D_MODEL),
               s_pl.reshape(PLE_DIM, -1), s_plg.reshape(-1, D_MODEL)]

    gains_part = [dg_in, dg_cq, dg_ckv, dg_q_p, dg_k_p, dg_oa, dg_oc, dg_pl]
    gflat = jnp.concatenate([g.reshape(-1) for g in gains_part] + [jnp.zeros(((GAIN_ROWS - 1) * LANES - 3968,), f32),
                                                                    loss_part.reshape(-1)])
    gain_parts = _all_gather(gflat.reshape(GAIN_ROWS, LANES), "ag_gains", in_vmem=True).reshape(nd, GAIN_ROWS, LANES)

    gains = [g_in, g_cq, g_ckv, g_q_p, g_k_p, g_oa, g_oc, g_pl]
    padq = lambda a: jnp.pad(a, ((0, 0), (0, HEAD_PAD - QK_DIM)))
    m_gains = [m_g_in, m_g_cq, m_g_ckv, padq(m_g_q), padq(m_g_k), m_g_oa, m_g_oc, m_g_pl]
    v_gains = [v_g_in, v_g_cq, v_g_ckv, padq(v_g_q), padq(v_g_k), v_g_oa, v_g_oc, v_g_pl]
    ws = gains + [w_in[0], w_uq[0], w_ukv[0], conv_w[0], w_o[0], w_pl[0], w_plg[0]]
    ms = m_gains + [m_w_in[0], m_w_uq[0], m_w_ukv[0], m_conv_w[0], m_w_o[0], m_w_pl[0], m_w_plg[0]]
    vs = v_gains + [v_w_in[0], v_w_uq[0], v_w_ukv[0], v_conv_w[0], v_w_o[0], v_w_pl[0], v_w_plg[0]]
    gs = [jnp.zeros_like(g) for g in gains] + grads_w
    loss_row, gg, deltas, new_m, new_v = _adamw_all(ws, gs, ms, vs, gain_parts)
    loss = loss_row[0, 0]

    def ordered(gl, wl):
        g_in_, g_cq_, g_ckv_, g_q_, g_k_, g_oa_, g_oc_, g_pl_ = gl
        g_q_, g_k_ = g_q_[:, :QK_DIM], g_k_[:, :QK_DIM]
        w_in_, w_uq_, w_ukv_, cw_, w_o_, w_pl_, w_plg_ = [w[None] for w in wl]
        return [g_in_, w_in_, g_cq_, w_uq_, g_ckv_, w_ukv_, g_q_, g_k_, cw_, g_oa_, g_oc_, w_o_, w_pl_, w_plg_, g_pl_]

    ng = len(gains)
    outs = [loss, grad_x[None]]
    outs += ordered(gg, grads_w)
    for lst in (deltas, new_m, new_v):
        outs += ordered(lst[:ng], lst[ng:])
    return tuple(outs)
```

```python
import functools
import math

import jax
import jax.numpy as jnp
from jax import lax
from jax.experimental import pallas as pl
from jax.experimental.pallas import tpu as pltpu

f32 = jnp.float32
bf16 = jnp.bfloat16
MM = jnp.bfloat16
RS_DTYPE = jnp.bfloat16

D_MODEL = 1024
PLE_DIM = 256
N_HEADS = 4
NOPE_DIM = 128
ROPE_DIM = 64
V_DIM = 128
QK_DIM = NOPE_DIM + ROPE_DIM
Q_LORA = 256
KV_LORA = 128
ATTN_WIDTH = N_HEADS * V_DIM
CONV_WIDTH = D_MODEL - ATTN_WIDTH
CONV_K = 3
ROPE_THETA = 10000.0
RMS_EPS = 1e-6
NEG_INF = -1e30
IN_TOTAL = Q_LORA + KV_LORA + ROPE_DIM + ATTN_WIDTH + 4 * CONV_WIDTH
ADAM_LR = 0.001
ADAM_B1 = 0.9
ADAM_B2 = 0.999
ADAM_EPS = 1e-08
ADAM_WD = 0.01
ADAM_STEP = 10

N_DEV = 8
LANES = 128
HEAD_PAD = 256
QK_PAD = N_HEADS * HEAD_PAD
PROJ_PAD = 3072
TS = 512
TS_M = 256
TS_E = 256
TQ = 512
VMEM_LIMIT = 58 * 1024 * 1024
MESH = pl.DeviceIdType.MESH
QK_SCALE = 1.0 / math.sqrt(QK_DIM)
SCALE_LOG2E = QK_SCALE * math.log2(math.e)


def _cparams(sem=None):
    return pltpu.CompilerParams(dimension_semantics=sem, vmem_limit_bytes=VMEM_LIMIT)


def _nt(a, b):
    return lax.dot_general(a, b, (((1,), (1,)), ((), ())), preferred_element_type=f32)


def _tn(a, b):
    return lax.dot_general(a, b, (((0,), (0,)), ((), ())), preferred_element_type=f32)


def _nn(a, b):
    return jnp.dot(a, b, preferred_element_type=f32)


def _rstd(x, n):
    return lax.rsqrt(jnp.sum(x * x, axis=-1, keepdims=True) * (1.0 / n) + RMS_EPS)


def _rms_bwd(dy, xhat, r, g, n):
    u = dy * g
    return r * (u - xhat * (jnp.sum(u * xhat, axis=-1, keepdims=True) * (1.0 / n)))


def _rope(t, c, s1, s2):
    return t * c + pltpu.roll(t, 32, 1) * s1 + pltpu.roll(t, 96, 1) * s2


def _rope_t(d, c, s1, s2):
    return d * c - pltpu.roll(d, 96, 1) * s2 - pltpu.roll(d, 32, 1) * s1


def _const(shape):
    return pl.BlockSpec(shape, lambda *_: (0,) * len(shape), pipeline_mode=pl.Buffered(1))


def _rope_tables(posf, inv_freq):
    S = posf.shape[0]

    def body(pos_ref, f_ref, c_ref, s1_ref, s2_ref):
        ang = pos_ref[...] * f_ref[...]
        lane = lax.broadcasted_iota(jnp.int32, ang.shape, 1)
        cs, sn = jnp.cos(ang), jnp.sin(ang)
        c_ref[...] = jnp.where(lane < ROPE_DIM, cs, 0.0)
        s1_ref[...] = jnp.where((lane >= ROPE_DIM // 2) & (lane < ROPE_DIM), sn, 0.0)
        s2_ref[...] = jnp.where(lane < ROPE_DIM // 2, -sn, 0.0)

    t = min(S, TS)
    spec = pl.BlockSpec((t, LANES), lambda i: (i, 0))
    return pl.pallas_call(
        body, name="rope_tables", grid=(S // t,),
        out_shape=[jax.ShapeDtypeStruct((S, LANES), f32)] * 3,
        in_specs=[spec, _const((1, LANES))], out_specs=[spec] * 3,
        compiler_params=_cparams(("parallel",)),
    )(posf, inv_freq)


def _fwd_proj(x, tabs, g_in, w_in_p, g_cq, w_uq_p, g_ckv, w_ukv_p, g_q_p, g_k_p):
    S = x.shape[0]

    def body(x_ref, c_ref, s1_ref, s2_ref, g_in_ref, w_in_ref, g_cq_ref, w_uq_ref, g_ckv_ref, w_ukv_ref,
             g_q_ref, g_k_ref, proj_ref, q_ref, k_ref, v_ref):
        xv = x_ref[...]
        h = (xv * _rstd(xv, D_MODEL) * g_in_ref[...]).astype(MM)
        proj_ref[...] = _nn(h, w_in_ref[...])
        c, s1, s2 = c_ref[...], s1_ref[...], s2_ref[...]
        g_q, g_k = g_q_ref[...], g_k_ref[...]

        c_q = proj_ref[:, 0:Q_LORA]
        n_cq = (c_q * _rstd(c_q, Q_LORA) * g_cq_ref[...]).astype(MM)
        q_raw = _nn(n_cq, w_uq_ref[...])
        for hd in range(N_HEADS):
            qh = q_raw[:, hd * HEAD_PAD:(hd + 1) * HEAD_PAD]
            qn = qh * _rstd(qh, QK_DIM) * g_q
            q_ref[:, hd * HEAD_PAD:hd * HEAD_PAD + LANES] = qn[:, :LANES].astype(MM)
            q_ref[:, hd * HEAD_PAD + LANES:(hd + 1) * HEAD_PAD] = _rope(qn[:, LANES:], c, s1, s2).astype(MM)

        c_kv = proj_ref[:, Q_LORA:Q_LORA + KV_LORA]
        n_ckv = (c_kv * _rstd(c_kv, KV_LORA) * g_ckv_ref[...]).astype(MM)
        kv_raw = _nn(n_ckv, w_ukv_ref[...])
        kpe = proj_ref[:, Q_LORA + KV_LORA:Q_LORA + KV_LORA + LANES]
        ss_pe = jnp.sum(kpe * kpe, axis=-1, keepdims=True)
        kr = _rope(kpe * g_k[:, LANES:], c, s1, s2)
        for hd in range(N_HEADS):
            kn = kv_raw[:, hd * LANES:(hd + 1) * LANES]
            rk = lax.rsqrt((jnp.sum(kn * kn, axis=-1, keepdims=True) + ss_pe) * (1.0 / QK_DIM) + RMS_EPS)
            k_ref[:, hd * HEAD_PAD:hd * HEAD_PAD + LANES] = (kn * rk * g_k[:, :LANES]).astype(MM)
            k_ref[:, hd * HEAD_PAD + LANES:(hd + 1) * HEAD_PAD] = (kr * rk).astype(MM)
        v_ref[...] = kv_raw[:, N_HEADS * NOPE_DIM:].astype(MM)

    row = lambda w: pl.BlockSpec((TS, w), lambda i: (i, 0))
    return pl.pallas_call(
        body, name="fwd_proj", grid=(S // TS,),
        out_shape=[jax.ShapeDtypeStruct((S, PROJ_PAD), f32), jax.ShapeDtypeStruct((S, QK_PAD), MM),
                   jax.ShapeDtypeStruct((S, QK_PAD), MM), jax.ShapeDtypeStruct((S, ATTN_WIDTH), MM)],
        in_specs=[row(D_MODEL), row(LANES), row(LANES), row(LANES), _const((1, D_MODEL)), _const((D_MODEL, PROJ_PAD)),
                  _const((1, Q_LORA)), _const((Q_LORA, QK_PAD)), _const((1, KV_LORA)), _const((KV_LORA, 2 * ATTN_WIDTH)),
                  _const((1, HEAD_PAD)), _const((1, HEAD_PAD))],
        out_specs=[row(PROJ_PAD), row(QK_PAD), row(QK_PAD), row(ATTN_WIDTH)],
        compiler_params=_cparams(("parallel",)),
    )(x, *tabs, g_in, w_in_p, g_cq, w_uq_p, g_ckv, w_ukv_p, g_q_p, g_k_p)


def _flash_fwd(q, k, v):
    S = q.shape[0]
    T = min(TQ, S)

    def body(q_ref, k_ref, v_ref, o_ref, lse_ref, s_a, s_b, m_sc, l_sc, acc_sc):
        n = pl.program_id(1)
        m_sc[...] = jnp.full(m_sc.shape, NEG_INF, f32)
        l_sc[...] = jnp.zeros(l_sc.shape, f32)
        acc_sc[...] = jnp.zeros(acc_sc.shape, f32)
        qv = q_ref[...]

        def rows(j):
            return pl.ds(pl.multiple_of(j * T, T), T)

        def scores(j, dst):
            dst[...] = _nt(k_ref[rows(j), :], qv)

        def update(src, j, masked):
            s = src[...]
            if masked:
                k_i = lax.broadcasted_iota(jnp.int32, (T, T), 0)
                q_i = lax.broadcasted_iota(jnp.int32, (T, T), 1)
                s = jnp.where(k_i <= q_i, s, NEG_INF)
            m_prev = m_sc[...]
            m_new = jnp.maximum(m_prev, jnp.max(s, axis=0, keepdims=True))
            alpha = jnp.exp2((m_prev - m_new) * SCALE_LOG2E)
            p = jnp.exp2((s - m_new[0:1, :]) * SCALE_LOG2E)
            l_sc[...] = alpha * l_sc[...] + jnp.sum(p, axis=0, keepdims=True)
            acc_sc[...] = alpha[0:1, :] * acc_sc[...] + _tn(v_ref[rows(j), :], p.astype(MM))
            m_sc[...] = m_new

        scores(0, s_a)

        def pair(t, carry):
            j = 2 * t
            scores(j + 1, s_b)
            update(s_a, j, False)
            scores(j + 2, s_a)
            update(s_b, j + 1, False)
            return carry

        lax.fori_loop(0, n // 2, pair, 0)

        @pl.when(n % 2 == 0)
        def _():
            update(s_a, n, True)

        @pl.when(n % 2 == 1)
        def _():
            scores(n, s_b)
            update(s_a, n - 1, False)
            update(s_b, n, True)

        o_ref[...] = (acc_sc[...] / l_sc[0:1, :]).T
        lse_ref[...] = m_sc[...] * SCALE_LOG2E + jnp.log2(l_sc[...])

    return pl.pallas_call(
        body, name="flash_fwd", grid=(N_HEADS, S // T),
        out_shape=[jax.ShapeDtypeStruct((S, ATTN_WIDTH), f32), jax.ShapeDtypeStruct((N_HEADS * 8, S), f32)],
        in_specs=[pl.BlockSpec((T, HEAD_PAD), lambda h, i: (i, h)),
                  pl.BlockSpec((S, HEAD_PAD), lambda h, i: (0, h)),
                  pl.BlockSpec((S, V_DIM), lambda h, i: (0, h))],
        out_specs=[pl.BlockSpec((T, V_DIM), lambda h, i: (i, h)), pl.BlockSpec((8, T), lambda h, i: (h, i))],
        scratch_shapes=[pltpu.VMEM((T, T), f32), pltpu.VMEM((T, T), f32), pltpu.VMEM((8, T), f32), pltpu.VMEM((8, T), f32),
                        pltpu.VMEM((V_DIM, T), f32)],
        compiler_params=_cparams(("parallel", "parallel")),
    )(q, k, v)


def _shift_down(m, prev8, n):
    T = m.shape[0]
    rows = lax.broadcasted_iota(jnp.int32, m.shape, 0)
    head = jnp.tile(pltpu.roll(prev8, n, 0), (T // 8, 1))
    return jnp.where(rows >= n, pltpu.roll(m, n, 0), head)


def _shift_up(m, next8, n):
    T = m.shape[0]
    rows = lax.broadcasted_iota(jnp.int32, m.shape, 0)
    tail = jnp.tile(pltpu.roll(next8, 8 - n, 0), (T // 8, 1))
    return jnp.where(rows < T - n, pltpu.roll(m, T - n, 0), tail)


def _mid(o, proj, x, p, target, w_o, w_plg, w_pl, conv_w8, g_oa, g_oc, g_pl):
    S = x.shape[0]
    T = min(TS_M, S)
    nt = S // T

    def body(o_ref, za_ref, ccx_ref, cbzc_ref, prev_ref, x_ref, p_ref, t_ref, wo_ref, wplg_ref, wpl_ref, cw_ref,
             goa_ref, goc_ref, gpl_ref,
             dx2_ref, do_ref, delta_ref, dza_ref, dcbzc_ref, du_ref,
             dwo_ref, dwplg_ref, dwpl_ref, dcw_ref, dgoa_ref, dgoc_ref, dgpl_ref, loss_ref):
        i = pl.program_id(0)

        @pl.when(i == 0)
        def _():
            for r in (dwo_ref, dwplg_ref, dwpl_ref, dcw_ref, dgoa_ref, dgoc_ref, dgpl_ref, loss_ref):
                r[...] = jnp.zeros(r.shape, f32)

        o_v = o_ref[...]
        z_a = za_ref[...]
        cc, cx = ccx_ref[:, :CONV_WIDTH], ccx_ref[:, CONV_WIDTH:]
        cb, z_c = cbzc_ref[:, :CONV_WIDTH], cbzc_ref[:, CONV_WIDTH:]
        g_oa, g_oc, g_pl = goa_ref[...], goc_ref[...], gpl_ref[...]
        w0, w1, w2 = cw_ref[0:1, :], cw_ref[1:2, :], cw_ref[2:3, :]

        sa = jax.nn.sigmoid(z_a)
        silu_a = z_a * sa
        ga = o_v * silu_a
        ra = _rstd(ga, ATTN_WIDTH)
        gha = ga * ra
        m0 = cc * cx
        prev = prev_ref[:, :CONV_WIDTH] * prev_ref[:, CONV_WIDTH:] * (i > 0).astype(f32)
        m1 = _shift_down(m0, prev, 1)
        m2 = _shift_down(m0, prev, 2)
        u = w0 * m2 + w1 * m1 + w2 * m0
        sc = jax.nn.sigmoid(z_c)
        silu_c = z_c * sc
        gc = cb * u * silu_c
        rc = _rstd(gc, CONV_WIDTH)
        ghc = gc * rc
        ycat = jnp.concatenate([gha * g_oa, ghc * g_oc], axis=-1).astype(MM)
        x2 = x_ref[...] + _nn(ycat, wo_ref[...])
        r2 = _rstd(x2, D_MODEL)
        xh2 = x2 * r2
        n2 = (xh2 * g_pl).astype(MM)
        gate = jax.nn.sigmoid(_nn(n2, wplg_ref[...]))
        p_b = p_ref[...].astype(MM)
        pw = _nn(p_b, wpl_ref[...])
        err = x2 + gate * pw - t_ref[...]
        loss_ref[...] += jnp.sum(jnp.sum(err * err, axis=-1, keepdims=True), axis=0, keepdims=True) * (0.5 / D_MODEL)
        d_out = err * (1.0 / D_MODEL)
        dwpl_ref[...] += _tn(p_b, (d_out * gate).astype(MM))
        d_glog = (d_out * pw * gate * (1.0 - gate)).astype(MM)
        dwplg_ref[...] += _tn(n2, d_glog)
        d_n2 = _nt(d_glog, wplg_ref[...])
        dgpl_ref[...] += jnp.sum(d_n2 * xh2, axis=0, keepdims=True)
        d_x2 = d_out + _rms_bwd(d_n2, xh2, r2, g_pl, D_MODEL)
        dx2_ref[...] = d_x2
        d_x2b = d_x2.astype(MM)
        dwo_ref[...] += _tn(ycat, d_x2b)
        d_ycat = _nt(d_x2b, wo_ref[...])
        d_ya, d_yc = d_ycat[:, :ATTN_WIDTH], d_ycat[:, ATTN_WIDTH:]
        dgoa_ref[...] += jnp.sum(d_ya * gha, axis=0, keepdims=True)
        dgoc_ref[...] += jnp.sum(d_yc * ghc, axis=0, keepdims=True)
        d_ga = _rms_bwd(d_ya, gha, ra, g_oa, ATTN_WIDTH)
        d_gc = _rms_bwd(d_yc, ghc, rc, g_oc, CONV_WIDTH)
        d_o = d_ga * silu_a
        do_ref[...] = d_o.astype(MM)
        dza_ref[...] = (d_ga * o_v * (sa * (1.0 + z_a * (1.0 - sa)))).astype(MM)
        for hd in range(N_HEADS):
            cols = slice(hd * V_DIM, (hd + 1) * V_DIM)
            dl = jnp.sum(d_o[:, cols] * o_v[:, cols], axis=-1, keepdims=True)
            delta_ref[hd * 8:(hd + 1) * 8, :] = jnp.broadcast_to(dl, (T, LANES)).T[0:8, :]
        d_u = d_gc * cb * silu_c
        du_ref[...] = d_u
        dcbzc_ref[:, :CONV_WIDTH] = (d_gc * u * silu_c).astype(MM)
        dcbzc_ref[:, CONV_WIDTH:] = (d_gc * cb * u * (sc * (1.0 + z_c * (1.0 - sc)))).astype(MM)
        dcw_ref[0:1, :] += jnp.sum(d_u * m2, axis=0, keepdims=True)
        dcw_ref[1:2, :] += jnp.sum(d_u * m1, axis=0, keepdims=True)
        dcw_ref[2:3, :] += jnp.sum(d_u * m0, axis=0, keepdims=True)

    row = lambda w, blk=0: pl.BlockSpec((T, w), lambda i: (i, blk))
    acc = lambda shape: pl.BlockSpec(shape, lambda i: (0, 0))
    tb = T // 8
    sds = jax.ShapeDtypeStruct
    return pl.pallas_call(
        body, name="mid", grid=(nt,),
        out_shape=[sds((S, D_MODEL), f32), sds((S, ATTN_WIDTH), MM), sds((N_HEADS * 8, S), f32), sds((S, ATTN_WIDTH), MM),
                   sds((S, 2 * CONV_WIDTH), MM), sds((S, CONV_WIDTH), f32),
                   sds((D_MODEL, D_MODEL), f32), sds((D_MODEL, D_MODEL), f32), sds((PLE_DIM, D_MODEL), f32), sds((8, CONV_WIDTH), f32),
                   sds((1, ATTN_WIDTH), f32), sds((1, CONV_WIDTH), f32), sds((1, D_MODEL), f32), sds((1, LANES), f32)],
        in_specs=[row(ATTN_WIDTH), row(ATTN_WIDTH, 1), row(2 * CONV_WIDTH, 1), row(2 * CONV_WIDTH, 2),
                  pl.BlockSpec((8, 2 * CONV_WIDTH), lambda i: (jnp.maximum(i * tb - 1, 0), 1)),
                  row(D_MODEL), row(PLE_DIM), row(D_MODEL),
                  _const((D_MODEL, D_MODEL)), _const((D_MODEL, D_MODEL)), _const((PLE_DIM, D_MODEL)), _const((8, CONV_WIDTH)),
                  _const((1, ATTN_WIDTH)), _const((1, CONV_WIDTH)), _const((1, D_MODEL))],
        out_specs=[row(D_MODEL), row(ATTN_WIDTH), pl.BlockSpec((N_HEADS * 8, T), lambda i: (0, i)), row(ATTN_WIDTH),
                   row(2 * CONV_WIDTH), row(CONV_WIDTH),
                   acc((D_MODEL, D_MODEL)), acc((D_MODEL, D_MODEL)), acc((PLE_DIM, D_MODEL)), acc((8, CONV_WIDTH)),
                   acc((1, ATTN_WIDTH)), acc((1, CONV_WIDTH)), acc((1, D_MODEL)), acc((1, LANES))],
        compiler_params=_cparams(("arbitrary",)),
    )(o, proj, proj, proj, proj, x, p, target, w_o, w_plg, w_pl, conv_w8, g_oa, g_oc, g_pl)


def _flash_bwd(q, k, v, d_o, lse2, delta):
    S = q.shape[0]
    T = min(TQ, S)
    nq = S // T

    def body(k_ref, v_ref, q_ref, do_ref, lse_ref, delta_ref, dq_ref, dk_ref, dv_ref, s_a, dp_a, s_b, dp_b):
        j = pl.program_id(1)

        @pl.when(j == 0)
        def _():
            dq_ref[...] = jnp.zeros(dq_ref.shape, f32)

        dk_ref[...] = jnp.zeros(dk_ref.shape, f32)
        dv_ref[...] = jnp.zeros(dv_ref.shape, f32)
        kv, vv = k_ref[...], v_ref[...]

        def rows_of(i):
            return pl.ds(pl.multiple_of(i * T, T), T)

        def scores(i, s_dst, dp_dst):
            rows = rows_of(i)
            s_dst[...] = _nt(kv, q_ref[rows, :])
            dp_dst[...] = _nt(vv, do_ref[rows, :])

        def grads(i, s_src, dp_src, masked):
            rows = rows_of(i)
            s = s_src[...]
            if masked:
                k_i = lax.broadcasted_iota(jnp.int32, (T, T), 0)
                q_i = lax.broadcasted_iota(jnp.int32, (T, T), 1)
                s = jnp.where(k_i <= q_i, s, NEG_INF)
            p = jnp.exp2(s * SCALE_LOG2E - lse_ref[0:1, rows])
            ds = (p * (dp_src[...] - delta_ref[0:1, rows])).astype(MM)
            dv_ref[...] += _nn(p.astype(MM), do_ref[rows, :])
            dk_ref[...] += _nn(ds, q_ref[rows, :])
            dq_ref[rows, :] += _tn(ds, kv)

        last = nq - 1
        scores(j, s_a, dp_a)
        scores(jnp.minimum(j + 1, last), s_b, dp_b)
        grads(j, s_a, dp_a, True)
        rest = last - j

        def pair(t, carry):
            u = j + 1 + 2 * t
            scores(u + 1, s_a, dp_a)
            grads(u, s_b, dp_b, False)
            scores(jnp.minimum(u + 2, last), s_b, dp_b)
            grads(u + 1, s_a, dp_a, False)
            return carry

        lax.fori_loop(0, rest // 2, pair, 0)

        @pl.when(rest % 2 == 1)
        def _():
            grads(last, s_b, dp_b, False)

    whole = lambda w: pl.BlockSpec((S, w), lambda h, j: (0, h))
    stat = pl.BlockSpec((8, S), lambda h, j: (h, 0))
    return pl.pallas_call(
        body, name="flash_bwd", grid=(N_HEADS, nq),
        out_shape=[jax.ShapeDtypeStruct((S, QK_PAD), f32), jax.ShapeDtypeStruct((S, QK_PAD), f32),
                   jax.ShapeDtypeStruct((S, ATTN_WIDTH), f32)],
        in_specs=[pl.BlockSpec((T, HEAD_PAD), lambda h, j: (j, h)), pl.BlockSpec((T, V_DIM), lambda h, j: (j, h)),
                  whole(HEAD_PAD), whole(V_DIM), stat, stat],
        out_specs=[whole(HEAD_PAD), pl.BlockSpec((T, HEAD_PAD), lambda h, j: (j, h)),
                   pl.BlockSpec((T, V_DIM), lambda h, j: (j, h))],
        scratch_shapes=[pltpu.VMEM((T, T), f32)] * 4,
        compiler_params=_cparams(("arbitrary", "arbitrary")),
    )(k, v, q, d_o, lse2, delta)


def _bwd_proj(x, proj, d_q, d_k, d_v, d_za, d_cbzc, d_u, d_x2, tabs, g_in, w_in_p, g_cq, w_uq_p, g_ckv, w_ukv_p,
              g_q_p, g_k_p, conv_w8):
    S = x.shape[0]
    T = min(TS_E, S)
    nt = S // T

    def body(x_ref, cqkv_ref, ccx_ref, dq_ref, dk_ref, dv_ref, dza_ref, dcbzc_ref, du_ref, dun_ref, dx2_ref,
             c_ref, s1_ref, s2_ref, g_in_ref, w_in_ref, g_cq_ref, w_uq_ref, g_ckv_ref, w_ukv_ref, g_q_ref, g_k_ref, cw_ref,
             gx_ref, dwin_hbm, dwuq_ref, dwukv_ref, dgin_ref, dgcq_ref, dgckv_ref, dgq_ref, dgk_ref,
             dproj_a, dproj_b, dqraw_sc, dkvraw_sc, dwin_sc, sem):
        i = pl.program_id(0)

        @pl.when(i == 0)
        def _():
            dwin_sc[...] = jnp.zeros(dwin_sc.shape, f32)
            dproj_b[...] = jnp.zeros(dproj_b.shape, MM)
            for r in (dwuq_ref, dwukv_ref, dgin_ref, dgcq_ref, dgckv_ref, dgq_ref, dgk_ref):
                r[...] = jnp.zeros(r.shape, f32)

        def stage1(dproj_sc):
            live = (i < nt).astype(f32)
            c, s1, s2 = c_ref[...], s1_ref[...], s2_ref[...]
            g_q, g_k = g_q_ref[...], g_k_ref[...]
            g_cq, g_ckv = g_cq_ref[...], g_ckv_ref[...]

            c_q = cqkv_ref[:, 0:Q_LORA]
            r_cq = _rstd(c_q, Q_LORA)
            cqh = c_q * r_cq
            n_cq = (cqh * g_cq).astype(MM)
            q_raw = _nn(n_cq, w_uq_ref[...])
            dgq = jnp.zeros((1, HEAD_PAD), f32)
            for hd in range(N_HEADS):
                cols = slice(hd * HEAD_PAD, (hd + 1) * HEAD_PAD)
                qh = q_raw[:, cols]
                rq = _rstd(qh, QK_DIM)
                qhh = qh * rq
                dqh = dq_ref[:, cols] * QK_SCALE
                d_qn = jnp.concatenate([dqh[:, :LANES], _rope_t(dqh[:, LANES:], c, s1, s2)], axis=-1)
                dgq += jnp.sum(d_qn * qhh, axis=0, keepdims=True)
                dqraw_sc[:, cols] = _rms_bwd(d_qn, qhh, rq, g_q, QK_DIM).astype(MM)
            dgq_ref[...] += dgq * live
            d_qraw = dqraw_sc[...]
            dwuq_ref[...] += _tn((cqh * (g_cq * live)).astype(MM), d_qraw)
            d_ncq = _nt(d_qraw, w_uq_ref[...])
            dgcq_ref[...] += jnp.sum(d_ncq * cqh, axis=0, keepdims=True) * live
            dproj_sc[:, 0:Q_LORA] = _rms_bwd(d_ncq, cqh, r_cq, g_cq, Q_LORA).astype(MM)

            c_kv = cqkv_ref[:, Q_LORA:Q_LORA + KV_LORA]
            r_ckv = _rstd(c_kv, KV_LORA)
            ckvh = c_kv * r_ckv
            n_ckv = (ckvh * g_ckv).astype(MM)
            k_nope = _nn(n_ckv, w_ukv_ref[:, :N_HEADS * NOPE_DIM])
            kpe = cqkv_ref[:, Q_LORA + KV_LORA:Q_LORA + KV_LORA + LANES]
            ss_pe = jnp.sum(kpe * kpe, axis=-1, keepdims=True)
            g_kn, g_kp = g_k[:, :LANES], g_k[:, LANES:]
            d_kpe = jnp.zeros((T, LANES), f32)
            dgk_n = jnp.zeros((1, LANES), f32)
            dgk_p = jnp.zeros((1, LANES), f32)
            for hd in range(N_HEADS):
                kn = k_nope[:, hd * LANES:(hd + 1) * LANES]
                rk = lax.rsqrt((jnp.sum(kn * kn, axis=-1, keepdims=True) + ss_pe) * (1.0 / QK_DIM) + RMS_EPS)
                knh, kph = kn * rk, kpe * rk
                d_kn_n = dk_ref[:, hd * HEAD_PAD:hd * HEAD_PAD + LANES] * QK_SCALE
                d_kr = _rope_t(dk_ref[:, hd * HEAD_PAD + LANES:(hd + 1) * HEAD_PAD] * QK_SCALE, c, s1, s2)
                dgk_n += jnp.sum(d_kn_n * knh, axis=0, keepdims=True)
                dgk_p += jnp.sum(d_kr * kph, axis=0, keepdims=True)
                u_n, u_p = d_kn_n * g_kn, d_kr * g_kp
                mt = (jnp.sum(u_n * knh, axis=-1, keepdims=True) + jnp.sum(u_p * kph, axis=-1, keepdims=True)) * (1.0 / QK_DIM)
                dkvraw_sc[:, hd * LANES:(hd + 1) * LANES] = (rk * (u_n - knh * mt)).astype(MM)
                d_kpe += rk * (u_p - kph * mt)
            dgk_ref[:, :LANES] += dgk_n * live
            dgk_ref[:, LANES:] += dgk_p * live
            dkvraw_sc[:, N_HEADS * NOPE_DIM:] = dv_ref[...].astype(MM)
            d_kvraw = dkvraw_sc[...]
            dwukv_ref[...] += _tn((ckvh * (g_ckv * live)).astype(MM), d_kvraw)
            d_nckv = _nt(d_kvraw, w_ukv_ref[...])
            dgckv_ref[...] += jnp.sum(d_nckv * ckvh, axis=0, keepdims=True) * live
            dproj_sc[:, Q_LORA:Q_LORA + KV_LORA] = _rms_bwd(d_nckv, ckvh, r_ckv, g_ckv, KV_LORA).astype(MM)
            dproj_sc[:, Q_LORA + KV_LORA:Q_LORA + KV_LORA + LANES] = d_kpe.astype(MM)

            dproj_sc[:, 512:1024] = dza_ref[...]
            d_u = du_ref[...]
            nxt = dun_ref[...] * (i < nt - 1).astype(f32)
            d_m = cw_ref[2:3, :] * d_u + cw_ref[1:2, :] * _shift_up(d_u, nxt, 1) + cw_ref[0:1, :] * _shift_up(d_u, nxt, 2)
            dproj_sc[:, 1024:1536] = (d_m * ccx_ref[:, CONV_WIDTH:]).astype(MM)
            dproj_sc[:, 1536:2048] = (d_m * ccx_ref[:, :CONV_WIDTH]).astype(MM)
            dproj_sc[:, 2048:3072] = dcbzc_ref[...]

        def stage2(dproj_sc):
            g_in = g_in_ref[...]
            xv = x_ref[...]
            r_in = _rstd(xv, D_MODEL)
            xh = xv * r_in
            hb = (xh * g_in).astype(MM)
            d_proj = dproj_sc[...]
            dwin_sc[...] += _tn(hb, d_proj)
            d_h = _nt(d_proj, w_in_ref[...])
            dgin_ref[...] += jnp.sum(d_h * xh, axis=0, keepdims=True)
            gx_ref[...] = dx2_ref[...] + _rms_bwd(d_h, xh, r_in, g_in, D_MODEL)

        @pl.when(i % 2 == 0)
        def _():
            stage2(dproj_b)
            stage1(dproj_a)

        @pl.when(i % 2 == 1)
        def _():
            stage2(dproj_a)
            stage1(dproj_b)

        @pl.when(i == nt)
        def _():
            cp = pltpu.make_async_copy(dwin_sc, dwin_hbm, sem)
            cp.start()
            cp.wait()

    cur = lambda i: jnp.minimum(i, nt - 1)
    prv = lambda i: jnp.maximum(i - 1, 0)
    row = lambda w, blk=0: pl.BlockSpec((T, w), lambda i: (cur(i), blk))
    row2 = lambda w: pl.BlockSpec((T, w), lambda i: (prv(i), 0))
    acc = lambda shape: pl.BlockSpec(shape, lambda i: (0, 0))
    tb = T // 8
    sds = jax.ShapeDtypeStruct
    return pl.pallas_call(
        body, name="bwd_proj", grid=(nt + 1,),
        out_shape=[sds((S, D_MODEL), f32), sds((D_MODEL, PROJ_PAD), f32), sds((Q_LORA, QK_PAD), f32),
                   sds((KV_LORA, 2 * ATTN_WIDTH), f32), sds((1, D_MODEL), f32), sds((1, Q_LORA), f32), sds((1, KV_LORA), f32),
                   sds((1, HEAD_PAD), f32), sds((1, HEAD_PAD), f32)],
        in_specs=[row2(D_MODEL), row(512, 0), row(2 * CONV_WIDTH, 1), row(QK_PAD), row(QK_PAD), row(ATTN_WIDTH),
                  row(ATTN_WIDTH), row(2 * CONV_WIDTH), row(CONV_WIDTH),
                  pl.BlockSpec((8, CONV_WIDTH), lambda i: (jnp.minimum((cur(i) + 1) * tb, S // 8 - 1), 0)),
                  row2(D_MODEL), row(LANES), row(LANES), row(LANES),
                  _const((1, D_MODEL)), _const((D_MODEL, PROJ_PAD)), _const((1, Q_LORA)), _const((Q_LORA, QK_PAD)),
                  _const((1, KV_LORA)), _const((KV_LORA, 2 * ATTN_WIDTH)), _const((1, HEAD_PAD)), _const((1, HEAD_PAD)),
                  _const((8, CONV_WIDTH))],
        out_specs=[row2(D_MODEL), pl.BlockSpec(memory_space=pl.ANY), acc((Q_LORA, QK_PAD)), acc((KV_LORA, 2 * ATTN_WIDTH)),
                   acc((1, D_MODEL)), acc((1, Q_LORA)), acc((1, KV_LORA)), acc((1, HEAD_PAD)), acc((1, HEAD_PAD))],
        scratch_shapes=[pltpu.VMEM((T, PROJ_PAD), MM), pltpu.VMEM((T, PROJ_PAD), MM), pltpu.VMEM((T, QK_PAD), MM),
                        pltpu.VMEM((T, 2 * ATTN_WIDTH), MM), pltpu.VMEM((D_MODEL, PROJ_PAD), f32), pltpu.SemaphoreType.DMA],
        compiler_params=_cparams(("arbitrary",)),
    )(x, proj, proj, d_q, d_k, d_v, d_za, d_cbzc, d_u, d_u, d_x2, *tabs, g_in, w_in_p, g_cq, w_uq_p, g_ckv, w_ukv_p,
      g_q_p, g_k_p, conv_w8)


def _all_gather(block, name, in_vmem):
    m_per, n = block.shape

    def body(x_ref, out_ref, send_sems, recv_sems, local_sem):
        x, y, c = lax.axis_index("x"), lax.axis_index("y"), lax.axis_index("c")
        me, sibling = (x, y, c), (x, y, 1 - c)
        chips = [(1 - x, y), (x, 1 - y), (1 - x, 1 - y)]

        def rows(px, py, pc):
            return out_ref.at[pl.ds((4 * px + 2 * py + pc) * m_per, m_per), :]

        def copy(k, blk, to, src=None):
            return pltpu.make_async_remote_copy(
                src_ref=rows(*blk) if src is None else src, dst_ref=rows(*blk),
                send_sem=send_sems.at[k], recv_sem=recv_sems.at[k], device_id=to, device_id_type=MESH)

        mine = pltpu.make_async_copy(x_ref, rows(*me), local_sem)
        mine.start()
        first = [copy(0, me, sibling, src=x_ref)]
        first += [copy(1 + j, me, (*chip, c), src=x_ref) for j, chip in enumerate(chips)]
        for cp in first:
            cp.start()
        passed = [copy(4 + j, (*chip, c), sibling) for j, chip in enumerate(chips)]
        for j, chip in enumerate(chips):
            copy(1 + j, (*chip, c), me).wait_recv()
            passed[j].start()
        copy(0, sibling, me).wait_recv()
        for j, chip in enumerate(chips):
            copy(4 + j, (*chip, 1 - c), me).wait_recv()
        for cp in first + passed:
            cp.wait_send()
        mine.wait()

    space = pltpu.VMEM if in_vmem else pl.ANY
    return pl.pallas_call(
        body, name=name, out_shape=jax.ShapeDtypeStruct((N_DEV * m_per, n), block.dtype),
        in_specs=[pl.BlockSpec(memory_space=space)], out_specs=pl.BlockSpec(memory_space=space),
        scratch_shapes=[pltpu.SemaphoreType.DMA((7,)), pltpu.SemaphoreType.DMA((7,)), pltpu.SemaphoreType.DMA],
    )(block)


def _sibling_exchange(pack):
    _, nchip, R, n = pack.shape

    def body(p_ref, got_ref, send_sem, recv_sem):
        x, y, c = lax.axis_index("x"), lax.axis_index("y"), lax.axis_index("c")
        cp = pltpu.make_async_remote_copy(src_ref=p_ref.at[1 - c], dst_ref=got_ref, send_sem=send_sem, recv_sem=recv_sem,
                                          device_id=(x, y, 1 - c), device_id_type=MESH)
        cp.start()
        cp.wait()

    return pl.pallas_call(
        body, name="rs_sibling", out_shape=jax.ShapeDtypeStruct((nchip, R, n), pack.dtype),
        in_specs=[pl.BlockSpec(memory_space=pl.ANY)], out_specs=pl.BlockSpec(memory_space=pl.ANY),
        scratch_shapes=[pltpu.SemaphoreType.DMA, pltpu.SemaphoreType.DMA],
    )(pack)


def _chip_exchange(part):
    nchip, R, n = part.shape

    def body(p_ref, got_ref, send_sems, recv_sems, local_sem):
        x, y, c = lax.axis_index("x"), lax.axis_index("y"), lax.axis_index("c")
        my_chip = 2 * x + y
        chips = [(1 - x, y), (x, 1 - y), (1 - x, 1 - y)]
        mine = pltpu.make_async_copy(p_ref.at[my_chip], got_ref.at[my_chip], local_sem)
        mine.start()
        sends = []
        for k, (tx, ty) in enumerate(chips):
            cp = pltpu.make_async_remote_copy(src_ref=p_ref.at[2 * tx + ty], dst_ref=got_ref.at[my_chip],
                                              send_sem=send_sems.at[k], recv_sem=recv_sems.at[k],
                                              device_id=(tx, ty, c), device_id_type=MESH)
            cp.start()
            sends.append(cp)
        for k, (sx, sy) in enumerate(chips):
            pltpu.make_async_remote_copy(src_ref=p_ref.at[my_chip], dst_ref=got_ref.at[2 * sx + sy],
                                         send_sem=send_sems.at[k], recv_sem=recv_sems.at[k],
                                         device_id=(sx, sy, c), device_id_type=MESH).wait_recv()
        for cp in sends:
            cp.wait_send()
        mine.wait()

    return pl.pallas_call(
        body, name="rs_chips", out_shape=jax.ShapeDtypeStruct((nchip, R, n), part.dtype),
        in_specs=[pl.BlockSpec(memory_space=pl.ANY)], out_specs=pl.BlockSpec(memory_space=pl.ANY),
        scratch_shapes=[pltpu.SemaphoreType.DMA((3,)), pltpu.SemaphoreType.DMA((3,)), pltpu.SemaphoreType.DMA],
    )(part)


def _add_blocks(a, b):
    nb, R, n = a.shape
    t = R // 5 if R % 80 == 0 else R

    def body(a_ref, b_ref, o_ref):
        o_ref[...] = (a_ref[...] + b_ref[...]).astype(o_ref.dtype)

    spec = pl.BlockSpec((1, t, n), lambda i, j: (i, j, 0))
    return pl.pallas_call(body, name="rs_add", grid=(nb, R // t), out_shape=jax.ShapeDtypeStruct(a.shape, RS_DTYPE),
                          in_specs=[spec, spec], out_specs=spec,
                          compiler_params=_cparams(("parallel", "parallel")))(a, b)


def _sum_chips(got):
    nb, R, n = got.shape
    t = R // 5 if R % 80 == 0 else R

    def body(g_ref, o_ref):
        g = g_ref[...].astype(f32)
        o_ref[...] = ((g[0] + g[1]) + g[2]) + g[3]

    return pl.pallas_call(body, name="rs_sum", grid=(R // t,), out_shape=jax.ShapeDtypeStruct((R, n), f32),
                          in_specs=[pl.BlockSpec((nb, t, n), lambda j: (0, j, 0))], out_specs=pl.BlockSpec((t, n), lambda j: (j, 0)),
                          compiler_params=_cparams(("parallel",)))(got)


def _adamw_all(ws, gs, ms, vs, gain_parts):
    n = len(ws)
    ng = len(GAIN_SLOTS)

    def body(*refs):
        gp_ref = refs[0]
        w_refs, g_refs, m_refs, v_refs = (refs[1 + k * n:1 + (k + 1) * n] for k in range(4))
        outs = refs[1 + 4 * n:]
        gsum_ref, loss_ref = outs[0], outs[1]
        gg_refs = outs[2:2 + ng]
        d_refs, nm_refs, nv_refs = (outs[2 + ng + k * n:2 + ng + (k + 1) * n] for k in range(3))
        tot = gp_ref[0]
        for d in range(1, N_DEV):
            tot = tot + gp_ref[d]
        gsum_ref[...] = tot
        loss_ref[...] = gsum_ref[GAIN_ROWS - 1:GAIN_ROWS, :]
        for k in range(n):
            if k < ng:
                r0, width = GAIN_SLOTS[k]
                nr = width // LANES
                for r in range(nr):
                    gg_refs[k][:, r * LANES:(r + 1) * LANES] = gsum_ref[r0 + r:r0 + r + 1, :]
                g = gg_refs[k][...]
            else:
                g = g_refs[k][...]
            w = w_refs[k][...]
            m = ADAM_B1 * m_refs[k][...] + (1.0 - ADAM_B1) * g
            v = ADAM_B2 * v_refs[k][...] + (1.0 - ADAM_B2) * (g * g)
            m_hat = m / (1.0 - ADAM_B1 ** ADAM_STEP)
            v_hat = v / (1.0 - ADAM_B2 ** ADAM_STEP)
            d_refs[k][...] = -ADAM_LR * (m_hat / (jnp.sqrt(v_hat) + ADAM_EPS) + ADAM_WD * w)
            nm_refs[k][...] = m
            nv_refs[k][...] = v

    sds = jax.ShapeDtypeStruct
    like = [sds(w.shape, f32) for w in ws]
    out_shape = [sds((GAIN_ROWS, LANES), f32), sds((1, LANES), f32)] + like[:ng] + like * 3
    vm = pl.BlockSpec(memory_space=pltpu.VMEM)
    outs = pl.pallas_call(
        body, name="adamw", out_shape=out_shape, in_specs=[vm] * (1 + 4 * n), out_specs=[vm] * len(out_shape),
        compiler_params=pltpu.CompilerParams(vmem_limit_bytes=VMEM_LIMIT),
    )(gain_parts, *ws, *gs, *ms, *vs)
    loss = outs[1]
    gg = outs[2:2 + ng]
    deltas, new_m, new_v = (outs[2 + ng + k * n:2 + ng + (k + 1) * n] for k in range(3))
    return loss, gg, deltas, new_m, new_v


GAIN_ROWS = 32
GAIN_SLOTS = [(0, 1024), (8, 256), (10, 128), (11, 256), (13, 256), (15, 512), (19, 512), (23, 1024)]

W_ROWS = [3008, 192, 128, 16, 1024, 256, 1024]
G_ROWS = [3008, 192, 128, 48, 1024, 256, 1024]


def _pad_rows(a, rows):
    flat = a.reshape(-1)
    return jnp.pad(flat, (0, rows * LANES - flat.shape[0])).reshape(rows, LANES)


def _split_rows(a, rows_list):
    out, r = [], 0
    for n in rows_list:
        out.append(a[..., r:r + n, :])
        r += n
    return out


def kernel(x, p, positions, g_in, w_in, g_cq, w_uq, g_ckv, w_ukv, g_q, g_k, conv_w, g_oa, g_oc, w_o, w_pl, w_plg, g_pl, loss_target, m_g_in, m_w_in, m_g_cq, m_w_uq, m_g_ckv, m_w_ukv, m_g_q, m_g_k, m_conv_w, m_g_oa, m_g_oc, m_w_o, m_w_pl, m_w_plg, m_g_pl, v_g_in, v_w_in, v_g_cq, v_w_uq, v_g_ckv, v_w_ukv, v_g_q, v_g_k, v_conv_w, v_g_oa, v_g_oc, v_w_o, v_w_pl, v_w_plg, v_g_pl):
    S = x.shape[1]
    nd = N_DEV
    xs, ps, tgt = x[0], p[0, 0], loss_target[0]

    conv_bits = lax.bitcast_convert_type(conv_w[0], bf16)
    shards = [w_in[0], w_uq[0], w_ukv[0], conv_bits, w_o[0], w_pl[0], w_plg[0]]
    pack = jnp.concatenate([_pad_rows(s.astype(bf16), r) for s, r in zip(shards, W_ROWS)], axis=0)
    R = pack.shape[0]
    gathered = _all_gather(pack, "ag_weights", in_vmem=False).reshape(nd, R, LANES)
    a_in, a_uq, a_ukv, a_cv, a_o, a_pl, a_plg = _split_rows(gathered, W_ROWS)

    def cols(a, rows, c):
        return a.reshape(nd, rows, c).transpose(1, 0, 2).reshape(rows, nd * c)

    win = cols(a_in, D_MODEL, IN_TOTAL // nd)
    zpad = jnp.zeros((D_MODEL, ROPE_DIM), bf16)
    w_in_p = jnp.concatenate([win[:, 0:448], zpad, win[:, 448:960], win[:, 1472:2496], win[:, 960:1472], win[:, 2496:3008]], axis=1)
    wuq = cols(a_uq, Q_LORA, N_HEADS * QK_DIM // nd).reshape(Q_LORA, N_HEADS, QK_DIM)
    w_uq_p = jnp.pad(wuq, ((0, 0), (0, 0), (0, HEAD_PAD - QK_DIM))).reshape(Q_LORA, QK_PAD)
    wukv = cols(a_ukv, KV_LORA, 2 * ATTN_WIDTH // nd).reshape(KV_LORA, N_HEADS, 2, NOPE_DIM)
    w_ukv_p = wukv.transpose(0, 2, 1, 3).reshape(KV_LORA, 2 * ATTN_WIDTH)
    cv_bits = a_cv.reshape(nd, -1)[:, :CONV_K * (CONV_WIDTH // nd) * 2].reshape(nd, CONV_K, CONV_WIDTH // nd, 2)
    conv_full = lax.bitcast_convert_type(cv_bits, f32).transpose(1, 0, 2).reshape(CONV_K, CONV_WIDTH)
    conv_w8 = jnp.pad(conv_full, ((0, 8 - CONV_K), (0, 0)))
    w_o_f = a_o.reshape(D_MODEL, D_MODEL)
    w_pl_f = cols(a_pl, PLE_DIM, D_MODEL // nd)
    w_plg_f = a_plg.reshape(D_MODEL, D_MODEL)
    w_in_p, w_uq_p, w_ukv_p, w_o_f, w_pl_f, w_plg_f = (w.astype(MM) for w in (w_in_p, w_uq_p, w_ukv_p, w_o_f, w_pl_f, w_plg_f))

    g_q_p = jnp.pad(g_q, ((0, 0), (0, HEAD_PAD - QK_DIM)))
    g_k_p = jnp.pad(g_k, ((0, 0), (0, HEAD_PAD - QK_DIM)))

    inv_freq = 1.0 / (ROPE_THETA ** (jnp.arange(0, ROPE_DIM, 2, dtype=f32) / ROPE_DIM))
    inv_row = jnp.concatenate([inv_freq, inv_freq, jnp.zeros((LANES - ROPE_DIM,), f32)]).reshape(1, LANES)
    posf = jnp.broadcast_to(positions[0].astype(f32)[:, None], (S, LANES))
    tabs = _rope_tables(posf, inv_row)

    proj, q, k, v = _fwd_proj(xs, tabs, g_in, w_in_p, g_cq, w_uq_p, g_ckv, w_ukv_p, g_q_p, g_k_p)
    o, lse = _flash_fwd(q, k, v)
    (d_x2, d_o, delta, d_za, d_cbzc, d_u, dw_o, dw_plg, dw_pl, dcw, dg_oa, dg_oc, dg_pl, loss_part) = _mid(
        o, proj, xs, ps, tgt, w_o_f, w_plg_f, w_pl_f, conv_w8, g_oa, g_oc, g_pl)
    d_q, d_k, d_v = _flash_bwd(q, k, v, d_o, lse, delta)
    (grad_x, dw_in_p, dw_uq_p, dw_ukv_p, dg_in, dg_cq, dg_ckv, dg_q_p, dg_k_p) = _bwd_proj(
        xs, proj, d_q, d_k, d_v, d_za, d_cbzc, d_u, d_x2, tabs, g_in, w_in_p, g_cq, w_uq_p, g_ckv, w_ukv_p, g_q_p, g_k_p, conv_w8)

    dw_in = jnp.concatenate([dw_in_p[:, 0:448], dw_in_p[:, 512:1024], dw_in_p[:, 2048:2560], dw_in_p[:, 1024:2048],
                             dw_in_p[:, 2560:3072]], axis=1)
    dw_uq = dw_uq_p.reshape(Q_LORA, N_HEADS, HEAD_PAD)[:, :, :QK_DIM].reshape(Q_LORA, N_HEADS * QK_DIM)
    dw_ukv = dw_ukv_p.reshape(KV_LORA, 2, N_HEADS, NOPE_DIM).transpose(0, 2, 1, 3).reshape(KV_LORA, 2 * ATTN_WIDTH)
    dconv = dcw[:CONV_K]

    def shard_cols(a):
        rows = a.shape[0]
        return a.reshape(rows, nd, -1).transpose(1, 0, 2).reshape(nd, -1)

    def pack_rows(flat, rows):
        return jnp.pad(flat, ((0, 0), (0, rows * LANES - flat.shape[1]))).reshape(nd, rows, LANES)

    flats = [shard_cols(dw_in), shard_cols(dw_uq), shard_cols(dw_ukv), shard_cols(dconv), dw_o.reshape(nd, -1),
             shard_cols(dw_pl), dw_plg.reshape(nd, -1)]
    gpack = jnp.concatenate([pack_rows(fl, r) for fl, r in zip(flats, G_ROWS)], axis=1)
    Rg = gpack.shape[1]
    gpack = gpack.reshape(4, 2, Rg, LANES).transpose(1, 0, 2, 3)
    my_c = lax.axis_index("c")
    from_sib = _sibling_exchange(gpack)
    mine = lax.dynamic_index_in_dim(gpack, my_c, axis=0, keepdims=False)
    chip_part = _add_blocks(mine, from_sib)
    gshard = _sum_chips(_chip_exchange(chip_part))
    s_in, s_uq, s_ukv, s_cv, s_o, s_pl, s_plg = _split_rows(gshard, G_ROWS)
    grads_w = [s_in.reshape(D_MODEL, -1), s_uq.reshape(Q_LORA, -1), s_ukv.reshape(KV_LORA, -1),
               s_cv.reshape(-1)[:CONV_K * CONV_WIDTH // nd].reshape(CONV_K, -1), s_o.reshape(-1, D_MODEL),
               s_pl.reshape(PLE_DIM, -1), s_plg.reshape(-1, D_MODEL)]

    gains_part = [dg_in, dg_cq, dg_ckv, dg_q_p, dg_k_p, dg_oa, dg_oc, dg_pl]
    gflat = jnp.concatenate([g.reshape(-1) for g in gains_part] + [jnp.zeros(((GAIN_ROWS - 1) * LANES - 3968,), f32),
                                                                    loss_part.reshape(-1)])
    gain_parts = _all_gather(gflat.reshape(GAIN_ROWS, LANES), "ag_gains", in_vmem=True).reshape(nd, GAIN_ROWS, LANES)

    gains = [g_in, g_cq, g_ckv, g_q_p, g_k_p, g_oa, g_oc, g_pl]
    padq = lambda a: jnp.pad(a, ((0, 0), (0, HEAD_PAD - QK_DIM)))
    m_gains = [m_g_in, m_g_cq, m_g_ckv, padq(m_g_q), padq(m_g_k), m_g_oa, m_g_oc, m_g_pl]
    v_gains = [v_g_in, v_g_cq, v_g_ckv, padq(v_g_q), padq(v_g_k), v_g_oa, v_g_oc, v_g_pl]
    ws = gains + [w_in[0], w_uq[0], w_ukv[0], conv_w[0], w_o[0], w_pl[0], w_plg[0]]
    ms = m_gains + [m_w_in[0], m_w_uq[0], m_w_ukv[0], m_conv_w[0], m_w_o[0], m_w_pl[0], m_w_plg[0]]
    vs = v_gains + [v_w_in[0], v_w_uq[0], v_w_ukv[0], v_conv_w[0], v_w_o[0], v_w_pl[0], v_w_plg[0]]
    gs = [jnp.zeros_like(g) for g in gains] + grads_w
    loss_row, gg, deltas, new_m, new_v = _adamw_all(ws, gs, ms, vs, gain_parts)
    loss = loss_row[0, 0]

    def ordered(gl, wl):
        g_in_, g_cq_, g_ckv_, g_q_, g_k_, g_oa_, g_oc_, g_pl_ = gl
        g_q_, g_k_ = g_q_[:, :QK_DIM], g_k_[:, :QK_DIM]
        w_in_, w_uq_, w_ukv_, cw_, w_o_, w_pl_, w_plg_ = [w[None] for w in wl]
        return [g_in_, w_in_, g_cq_, w_uq_, g_ckv_, w_ukv_, g_q_, g_k_, cw_, g_oa_, g_oc_, w_o_, w_pl_, w_plg_, g_pl_]

    ng = len(gains)
    outs = [loss, grad_x[None]]
    outs += ordered(gg, grads_w)
    for lst in (deltas, new_m, new_v):
        outs += ordered(lst[:ng], lst[ng:])
    return tuple(outs)
```

```python
import functools
import math

import jax
import jax.numpy as jnp
from jax import lax
from jax.experimental import pallas as pl
from jax.experimental.pallas import tpu as pltpu

f32 = jnp.float32
bf16 = jnp.bfloat16
MM = jnp.bfloat16
RS_DTYPE = jnp.bfloat16

D_MODEL = 1024
PLE_DIM = 256
N_HEADS = 4
NOPE_DIM = 128
ROPE_DIM = 64
V_DIM = 128
QK_DIM = NOPE_DIM + ROPE_DIM
Q_LORA = 256
KV_LORA = 128
ATTN_WIDTH = N_HEADS * V_DIM
CONV_WIDTH = D_MODEL - ATTN_WIDTH
CONV_K = 3
ROPE_THETA = 10000.0
RMS_EPS = 1e-6
NEG_INF = -1e30
IN_TOTAL = Q_LORA + KV_LORA + ROPE_DIM + ATTN_WIDTH + 4 * CONV_WIDTH
ADAM_LR = 0.001
ADAM_B1 = 0.9
ADAM_B2 = 0.999
ADAM_EPS = 1e-08
ADAM_WD = 0.01
ADAM_STEP = 10

N_DEV = 8
LANES = 128
HEAD_PAD = 256
QK_PAD = N_HEADS * HEAD_PAD
PROJ_PAD = 3072
TS = 512
TS_M = 256
TS_E = 256
TQ = 512
VMEM_LIMIT = 58 * 1024 * 1024
MESH = pl.DeviceIdType.MESH
QK_SCALE = 1.0 / math.sqrt(QK_DIM)
SCALE_LOG2E = QK_SCALE * math.log2(math.e)


def _cparams(sem=None):
    return pltpu.CompilerParams(dimension_semantics=sem, vmem_limit_bytes=VMEM_LIMIT)


def _nt(a, b):
    return lax.dot_general(a, b, (((1,), (1,)), ((), ())), preferred_element_type=f32)


def _tn(a, b):
    return lax.dot_general(a, b, (((0,), (0,)), ((), ())), preferred_element_type=f32)


def _nn(a, b):
    return jnp.dot(a, b, preferred_element_type=f32)


def _rstd(x, n):
    return lax.rsqrt(jnp.sum(x * x, axis=-1, keepdims=True) * (1.0 / n) + RMS_EPS)


def _rms_bwd(dy, xhat, r, g, n):
    u = dy * g
    return r * (u - xhat * (jnp.sum(u * xhat, axis=-1, keepdims=True) * (1.0 / n)))


def _rope(t, c, s1, s2):
    return t * c + pltpu.roll(t, 32, 1) * s1 + pltpu.roll(t, 96, 1) * s2


def _rope_t(d, c, s1, s2):
    return d * c - pltpu.roll(d, 96, 1) * s2 - pltpu.roll(d, 32, 1) * s1


def _const(shape):
    return pl.BlockSpec(shape, lambda *_: (0,) * len(shape), pipeline_mode=pl.Buffered(1))


def _rope_tables(posf, inv_freq):
    S = posf.shape[0]

    def body(pos_ref, f_ref, c_ref, s1_ref, s2_ref):
        ang = pos_ref[...] * f_ref[...]
        lane = lax.broadcasted_iota(jnp.int32, ang.shape, 1)
        cs, sn = jnp.cos(ang), jnp.sin(ang)
        c_ref[...] = jnp.where(lane < ROPE_DIM, cs, 0.0)
        s1_ref[...] = jnp.where((lane >= ROPE_DIM // 2) & (lane < ROPE_DIM), sn, 0.0)
        s2_ref[...] = jnp.where(lane < ROPE_DIM // 2, -sn, 0.0)

    t = min(S, TS)
    spec = pl.BlockSpec((t, LANES), lambda i: (i, 0))
    return pl.pallas_call(
        body, name="rope_tables", grid=(S // t,),
        out_shape=[jax.ShapeDtypeStruct((S, LANES), f32)] * 3,
        in_specs=[spec, _const((1, LANES))], out_specs=[spec] * 3,
        compiler_params=_cparams(("parallel",)),
    )(posf, inv_freq)


def _fwd_proj(x, tabs, g_in, w_in_p, g_cq, w_uq_p, g_ckv, w_ukv_p, g_q_p, g_k_p):
    S = x.shape[0]

    def body(x_ref, c_ref, s1_ref, s2_ref, g_in_ref, w_in_ref, g_cq_ref, w_uq_ref, g_ckv_ref, w_ukv_ref,
             g_q_ref, g_k_ref, proj_ref, q_ref, k_ref, v_ref):
        xv = x_ref[...]
        h = (xv * _rstd(xv, D_MODEL) * g_in_ref[...]).astype(MM)
        proj_ref[...] = _nn(h, w_in_ref[...])
        c, s1, s2 = c_ref[...], s1_ref[...], s2_ref[...]
        g_q, g_k = g_q_ref[...], g_k_ref[...]

        c_q = proj_ref[:, 0:Q_LORA]
        n_cq = (c_q * _rstd(c_q, Q_LORA) * g_cq_ref[...]).astype(MM)
        q_raw = _nn(n_cq, w_uq_ref[...])
        for hd in range(N_HEADS):
            qh = q_raw[:, hd * HEAD_PAD:(hd + 1) * HEAD_PAD]
            qn = qh * _rstd(qh, QK_DIM) * g_q
            q_ref[:, hd * HEAD_PAD:hd * HEAD_PAD + LANES] = qn[:, :LANES].astype(MM)
            q_ref[:, hd * HEAD_PAD + LANES:(hd + 1) * HEAD_PAD] = _rope(qn[:, LANES:], c, s1, s2).astype(MM)

        c_kv = proj_ref[:, Q_LORA:Q_LORA + KV_LORA]
        n_ckv = (c_kv * _rstd(c_kv, KV_LORA) * g_ckv_ref[...]).astype(MM)
        kv_raw = _nn(n_ckv, w_ukv_ref[...])
        kpe = proj_ref[:, Q_LORA + KV_LORA:Q_LORA + KV_LORA + LANES]
        ss_pe = jnp.sum(kpe * kpe, axis=-1, keepdims=True)
        kr = _rope(kpe * g_k[:, LANES:], c, s1, s2)
        for hd in range(N_HEADS):
            kn = kv_raw[:, hd * LANES:(hd + 1) * LANES]
            rk = lax.rsqrt((jnp.sum(kn * kn, axis=-1, keepdims=True) + ss_pe) * (1.0 / QK_DIM) + RMS_EPS)
            k_ref[:, hd * HEAD_PAD:hd * HEAD_PAD + LANES] = (kn * rk * g_k[:, :LANES]).astype(MM)
            k_ref[:, hd * HEAD_PAD + LANES:(hd + 1) * HEAD_PAD] = (kr * rk).astype(MM)
        v_ref[...] = kv_raw[:, N_HEADS * NOPE_DIM:].astype(MM)

    row = lambda w: pl.BlockSpec((TS, w), lambda i: (i, 0))
    return pl.pallas_call(
        body, name="fwd_proj", grid=(S // TS,),
        out_shape=[jax.ShapeDtypeStruct((S, PROJ_PAD), f32), jax.ShapeDtypeStruct((S, QK_PAD), MM),
                   jax.ShapeDtypeStruct((S, QK_PAD), MM), jax.ShapeDtypeStruct((S, ATTN_WIDTH), MM)],
        in_specs=[row(D_MODEL), row(LANES), row(LANES), row(LANES), _const((1, D_MODEL)), _const((D_MODEL, PROJ_PAD)),
                  _const((1, Q_LORA)), _const((Q_LORA, QK_PAD)), _const((1, KV_LORA)), _const((KV_LORA, 2 * ATTN_WIDTH)),
                  _const((1, HEAD_PAD)), _const((1, HEAD_PAD))],
        out_specs=[row(PROJ_PAD), row(QK_PAD), row(QK_PAD), row(ATTN_WIDTH)],
        compiler_params=_cparams(("parallel",)),
    )(x, *tabs, g_in, w_in_p, g_cq, w_uq_p, g_ckv, w_ukv_p, g_q_p, g_k_p)


def _exchange(src_of, dst_ref, send_sems, recv_sems, local_sem):
    x, y, c = lax.axis_index("x"), lax.axis_index("y"), lax.axis_index("c")
    me = 4 * x + 2 * y + c
    remote = []
    for k in range(1, N_DEV):
        px = 1 - x if (k >> 2) & 1 else x
        py = 1 - y if (k >> 1) & 1 else y
        pc = 1 - c if k & 1 else c
        remote.append(pltpu.make_async_remote_copy(
            src_ref=src_of(4 * px + 2 * py + pc), dst_ref=dst_ref.at[me], send_sem=send_sems.at[k - 1],
            recv_sem=recv_sems.at[k - 1], device_id=(px, py, pc), device_id_type=MESH))
    return remote, pltpu.make_async_copy(src_of(me), dst_ref.at[me], local_sem)


def _exchange_start(copies):
    remote, local = copies
    local.start()
    for cp in remote:
        cp.start()


def _exchange_wait(copies):
    remote, local = copies
    for cp in remote:
        cp.wait_recv()
    for cp in remote:
        cp.wait_send()
    local.wait()


EXCHANGE_SEMS = [pltpu.SemaphoreType.DMA((N_DEV - 1,)), pltpu.SemaphoreType.DMA((N_DEV - 1,)), pltpu.SemaphoreType.DMA]


def _flash_fwd(q, k, v, block):
    S = q.shape[0]
    T = min(TQ, S)
    nq = S // T

    def body(q_ref, k_ref, v_ref, blk_ref, o_ref, lse_ref, all_ref, s_a, s_b, m_sc, l_sc, acc_sc, send_sems, recv_sems, local_sem):
        n = pl.program_id(1)
        gather = _exchange(lambda d: blk_ref, all_ref, send_sems, recv_sems, local_sem)

        @pl.when((pl.program_id(0) == 0) & (n == 0))
        def _():
            _exchange_start(gather)

        m_sc[...] = jnp.full(m_sc.shape, NEG_INF, f32)
        l_sc[...] = jnp.zeros(l_sc.shape, f32)
        acc_sc[...] = jnp.zeros(acc_sc.shape, f32)
        qv = q_ref[...]

        def rows(j):
            return pl.ds(pl.multiple_of(j * T, T), T)

        def scores(j, dst):
            dst[...] = _nt(k_ref[rows(j), :], qv)

        def update(src, j, masked):
            s = src[...]
            if masked:
                k_i = lax.broadcasted_iota(jnp.int32, (T, T), 0)
                q_i = lax.broadcasted_iota(jnp.int32, (T, T), 1)
                s = jnp.where(k_i <= q_i, s, NEG_INF)
            m_prev = m_sc[...]
            m_new = jnp.maximum(m_prev, jnp.max(s, axis=0, keepdims=True))
            alpha = jnp.exp2((m_prev - m_new) * SCALE_LOG2E)
            p = jnp.exp2((s - m_new[0:1, :]) * SCALE_LOG2E)
            l_sc[...] = alpha * l_sc[...] + jnp.sum(p, axis=0, keepdims=True)
            acc_sc[...] = alpha[0:1, :] * acc_sc[...] + _tn(v_ref[rows(j), :], p.astype(MM))
            m_sc[...] = m_new

        scores(0, s_a)

        def pair(t, carry):
            j = 2 * t
            scores(j + 1, s_b)
            update(s_a, j, False)
            scores(j + 2, s_a)
            update(s_b, j + 1, False)
            return carry

        lax.fori_loop(0, n // 2, pair, 0)

        @pl.when(n % 2 == 0)
        def _():
            update(s_a, n, True)

        @pl.when(n % 2 == 1)
        def _():
            scores(n, s_b)
            update(s_a, n - 1, False)
            update(s_b, n, True)

        o_ref[...] = (acc_sc[...] / l_sc[0:1, :]).T
        lse_ref[...] = m_sc[...] * SCALE_LOG2E + jnp.log2(l_sc[...])

        @pl.when((pl.program_id(0) == N_HEADS - 1) & (n == nq - 1))
        def _():
            _exchange_wait(gather)

    return pl.pallas_call(
        body, name="flash_fwd", grid=(N_HEADS, nq),
        out_shape=[jax.ShapeDtypeStruct((S, ATTN_WIDTH), f32), jax.ShapeDtypeStruct((N_HEADS * 8, S), f32),
                   jax.ShapeDtypeStruct((N_DEV,) + block.shape, block.dtype)],
        in_specs=[pl.BlockSpec((T, HEAD_PAD), lambda h, i: (i, h)),
                  pl.BlockSpec((S, HEAD_PAD), lambda h, i: (0, h)),
                  pl.BlockSpec((S, V_DIM), lambda h, i: (0, h)),
                  pl.BlockSpec(memory_space=pl.ANY)],
        out_specs=[pl.BlockSpec((T, V_DIM), lambda h, i: (i, h)), pl.BlockSpec((8, T), lambda h, i: (h, i)),
                   pl.BlockSpec(memory_space=pl.ANY)],
        scratch_shapes=[pltpu.VMEM((T, T), f32), pltpu.VMEM((T, T), f32), pltpu.VMEM((8, T), f32), pltpu.VMEM((8, T), f32),
                        pltpu.VMEM((V_DIM, T), f32)] + EXCHANGE_SEMS,
        compiler_params=_cparams(("arbitrary", "arbitrary")),
    )(q, k, v, block)


def _shift_down(m, prev8, n):
    T = m.shape[0]
    rows = lax.broadcasted_iota(jnp.int32, m.shape, 0)
    head = jnp.tile(pltpu.roll(prev8, n, 0), (T // 8, 1))
    return jnp.where(rows >= n, pltpu.roll(m, n, 0), head)


def _shift_up(m, next8, n):
    T = m.shape[0]
    rows = lax.broadcasted_iota(jnp.int32, m.shape, 0)
    tail = jnp.tile(pltpu.roll(next8, 8 - n, 0), (T // 8, 1))
    return jnp.where(rows < T - n, pltpu.roll(m, T - n, 0), tail)


def _mid(o, proj, x, p, target, w_o, w_plg, w_pl, conv_w8, g_oa, g_oc, g_pl):
    S = x.shape[0]
    T = min(TS_M, S)
    nt = S // T

    def body(o_ref, za_ref, ccx_ref, cbzc_ref, prev_ref, x_ref, p_ref, t_ref, wo_ref, wplg_ref, wpl_ref, cw_ref,
             goa_ref, goc_ref, gpl_ref,
             dx2_ref, do_ref, delta_ref, dza_ref, dcbzc_ref, du_ref,
             dwo_ref, dwplg_ref, dwpl_ref, dcw_ref, dgoa_ref, dgoc_ref, dgpl_ref, loss_ref):
        i = pl.program_id(0)

        @pl.when(i == 0)
        def _():
            for r in (dwo_ref, dwplg_ref, dwpl_ref, dcw_ref, dgoa_ref, dgoc_ref, dgpl_ref, loss_ref):
                r[...] = jnp.zeros(r.shape, f32)

        o_v = o_ref[...]
        z_a = za_ref[...]
        cc, cx = ccx_ref[:, :CONV_WIDTH], ccx_ref[:, CONV_WIDTH:]
        cb, z_c = cbzc_ref[:, :CONV_WIDTH], cbzc_ref[:, CONV_WIDTH:]
        g_oa, g_oc, g_pl = goa_ref[...], goc_ref[...], gpl_ref[...]
        w0, w1, w2 = cw_ref[0:1, :], cw_ref[1:2, :], cw_ref[2:3, :]

        sa = jax.nn.sigmoid(z_a)
        silu_a = z_a * sa
        ga = o_v * silu_a
        ra = _rstd(ga, ATTN_WIDTH)
        gha = ga * ra
        m0 = cc * cx
        prev = prev_ref[:, :CONV_WIDTH] * prev_ref[:, CONV_WIDTH:] * (i > 0).astype(f32)
        m1 = _shift_down(m0, prev, 1)
        m2 = _shift_down(m0, prev, 2)
        u = w0 * m2 + w1 * m1 + w2 * m0
        sc = jax.nn.sigmoid(z_c)
        silu_c = z_c * sc
        gc = cb * u * silu_c
        rc = _rstd(gc, CONV_WIDTH)
        ghc = gc * rc
        ycat = jnp.concatenate([gha * g_oa, ghc * g_oc], axis=-1).astype(MM)
        x2 = x_ref[...] + _nn(ycat, wo_ref[...])
        r2 = _rstd(x2, D_MODEL)
        xh2 = x2 * r2
        n2 = (xh2 * g_pl).astype(MM)
        gate = jax.nn.sigmoid(_nn(n2, wplg_ref[...]))
        p_b = p_ref[...].astype(MM)
        pw = _nn(p_b, wpl_ref[...])
        err = x2 + gate * pw - t_ref[...]
        loss_ref[...] += jnp.sum(jnp.sum(err * err, axis=-1, keepdims=True), axis=0, keepdims=True) * (0.5 / D_MODEL)
        d_out = err * (1.0 / D_MODEL)
        dwpl_ref[...] += _tn(p_b, (d_out * gate).astype(MM))
        d_glog = (d_out * pw * gate * (1.0 - gate)).astype(MM)
        dwplg_ref[...] += _tn(n2, d_glog)
        d_n2 = _nt(d_glog, wplg_ref[...])
        dgpl_ref[...] += jnp.sum(d_n2 * xh2, axis=0, keepdims=True)
        d_x2 = d_out + _rms_bwd(d_n2, xh2, r2, g_pl, D_MODEL)
        dx2_ref[...] = d_x2
        d_x2b = d_x2.astype(MM)
        dwo_ref[...] += _tn(ycat, d_x2b)
        d_ycat = _nt(d_x2b, wo_ref[...])
        d_ya, d_yc = d_ycat[:, :ATTN_WIDTH], d_ycat[:, ATTN_WIDTH:]
        dgoa_ref[...] += jnp.sum(d_ya * gha, axis=0, keepdims=True)
        dgoc_ref[...] += jnp.sum(d_yc * ghc, axis=0, keepdims=True)
        d_ga = _rms_bwd(d_ya, gha, ra, g_oa, ATTN_WIDTH)
        d_gc = _rms_bwd(d_yc, ghc, rc, g_oc, CONV_WIDTH)
        d_o = d_ga * silu_a
        do_ref[...] = d_o.astype(MM)
        dza_ref[...] = (d_ga * o_v * (sa * (1.0 + z_a * (1.0 - sa)))).astype(MM)
        for hd in range(N_HEADS):
            cols = slice(hd * V_DIM, (hd + 1) * V_DIM)
            dl = jnp.sum(d_o[:, cols] * o_v[:, cols], axis=-1, keepdims=True)
            delta_ref[hd * 8:(hd + 1) * 8, :] = jnp.broadcast_to(dl, (T, LANES)).T[0:8, :]
        d_u = d_gc * cb * silu_c
        du_ref[...] = d_u
        dcbzc_ref[:, :CONV_WIDTH] = (d_gc * u * silu_c).astype(MM)
        dcbzc_ref[:, CONV_WIDTH:] = (d_gc * cb * u * (sc * (1.0 + z_c * (1.0 - sc)))).astype(MM)
        dcw_ref[0:1, :] += jnp.sum(d_u * m2, axis=0, keepdims=True)
        dcw_ref[1:2, :] += jnp.sum(d_u * m1, axis=0, keepdims=True)
        dcw_ref[2:3, :] += jnp.sum(d_u * m0, axis=0, keepdims=True)

    row = lambda w, blk=0: pl.BlockSpec((T, w), lambda i: (i, blk))
    acc = lambda shape: pl.BlockSpec(shape, lambda i: (0, 0))
    tb = T // 8
    sds = jax.ShapeDtypeStruct
    return pl.pallas_call(
        body, name="mid", grid=(nt,),
        out_shape=[sds((S, D_MODEL), f32), sds((S, ATTN_WIDTH), MM), sds((N_HEADS * 8, S), f32), sds((S, ATTN_WIDTH), MM),
                   sds((S, 2 * CONV_WIDTH), MM), sds((S, CONV_WIDTH), f32),
                   sds((D_MODEL, D_MODEL), f32), sds((D_MODEL, D_MODEL), f32), sds((PLE_DIM, D_MODEL), f32), sds((8, CONV_WIDTH), f32),
                   sds((1, ATTN_WIDTH), f32), sds((1, CONV_WIDTH), f32), sds((1, D_MODEL), f32), sds((1, LANES), f32)],
        in_specs=[row(ATTN_WIDTH), row(ATTN_WIDTH, 1), row(2 * CONV_WIDTH, 1), row(2 * CONV_WIDTH, 2),
                  pl.BlockSpec((8, 2 * CONV_WIDTH), lambda i: (jnp.maximum(i * tb - 1, 0), 1)),
                  row(D_MODEL), row(PLE_DIM), row(D_MODEL),
                  _const((D_MODEL, D_MODEL)), _const((D_MODEL, D_MODEL)), _const((PLE_DIM, D_MODEL)), _const((8, CONV_WIDTH)),
                  _const((1, ATTN_WIDTH)), _const((1, CONV_WIDTH)), _const((1, D_MODEL))],
        out_specs=[row(D_MODEL), row(ATTN_WIDTH), pl.BlockSpec((N_HEADS * 8, T), lambda i: (0, i)), row(ATTN_WIDTH),
                   row(2 * CONV_WIDTH), row(CONV_WIDTH),
                   acc((D_MODEL, D_MODEL)), acc((D_MODEL, D_MODEL)), acc((PLE_DIM, D_MODEL)), acc((8, CONV_WIDTH)),
                   acc((1, ATTN_WIDTH)), acc((1, CONV_WIDTH)), acc((1, D_MODEL)), acc((1, LANES))],
        compiler_params=_cparams(("arbitrary",)),
    )(o, proj, proj, proj, proj, x, p, target, w_o, w_plg, w_pl, conv_w8, g_oa, g_oc, g_pl)


def _flash_bwd(q, k, v, d_o, lse2, delta, by_target):
    S = q.shape[0]
    T = min(TQ, S)
    nq = S // T

    def body(k_ref, v_ref, q_ref, do_ref, lse_ref, delta_ref, out_ref, dq_ref, dk_ref, dv_ref, in_ref, s_a, dp_a, s_b, dp_b,
             send_sems, recv_sems, local_sem):
        j = pl.program_id(1)
        scatter = _exchange(lambda d: out_ref.at[d], in_ref, send_sems, recv_sems, local_sem)

        @pl.when((pl.program_id(0) == 0) & (j == 0))
        def _():
            _exchange_start(scatter)

        @pl.when(j == 0)
        def _():
            dq_ref[...] = jnp.zeros(dq_ref.shape, f32)

        dk_ref[...] = jnp.zeros(dk_ref.shape, f32)
        dv_ref[...] = jnp.zeros(dv_ref.shape, f32)
        kv, vv = k_ref[...], v_ref[...]

        def rows_of(i):
            return pl.ds(pl.multiple_of(i * T, T), T)

        def scores(i, s_dst, dp_dst):
            rows = rows_of(i)
            s_dst[...] = _nt(kv, q_ref[rows, :])
            dp_dst[...] = _nt(vv, do_ref[rows, :])

        def grads(i, s_src, dp_src, masked):
            rows = rows_of(i)
            s = s_src[...]
            if masked:
                k_i = lax.broadcasted_iota(jnp.int32, (T, T), 0)
                q_i = lax.broadcasted_iota(jnp.int32, (T, T), 1)
                s = jnp.where(k_i <= q_i, s, NEG_INF)
            p = jnp.exp2(s * SCALE_LOG2E - lse_ref[0:1, rows])
            ds = (p * (dp_src[...] - delta_ref[0:1, rows])).astype(MM)
            dv_ref[...] += _nn(p.astype(MM), do_ref[rows, :])
            dk_ref[...] += _nn(ds, q_ref[rows, :])
            dq_ref[rows, :] += _tn(ds, kv)

        last = nq - 1
        scores(j, s_a, dp_a)
        scores(jnp.minimum(j + 1, last), s_b, dp_b)
        grads(j, s_a, dp_a, True)
        rest = last - j

        def pair(t, carry):
            u = j + 1 + 2 * t
            scores(u + 1, s_a, dp_a)
            grads(u, s_b, dp_b, False)
            scores(jnp.minimum(u + 2, last), s_b, dp_b)
            grads(u + 1, s_a, dp_a, False)
            return carry

        lax.fori_loop(0, rest // 2, pair, 0)

        @pl.when(rest % 2 == 1)
        def _():
            grads(last, s_b, dp_b, False)

        @pl.when((pl.program_id(0) == N_HEADS - 1) & (j == nq - 1))
        def _():
            _exchange_wait(scatter)

    whole = lambda w: pl.BlockSpec((S, w), lambda h, j: (0, h))
    stat = pl.BlockSpec((8, S), lambda h, j: (h, 0))
    return pl.pallas_call(
        body, name="flash_bwd", grid=(N_HEADS, nq),
        out_shape=[jax.ShapeDtypeStruct((S, QK_PAD), f32), jax.ShapeDtypeStruct((S, QK_PAD), f32),
                   jax.ShapeDtypeStruct((S, ATTN_WIDTH), f32), jax.ShapeDtypeStruct(by_target.shape, by_target.dtype)],
        in_specs=[pl.BlockSpec((T, HEAD_PAD), lambda h, j: (j, h)), pl.BlockSpec((T, V_DIM), lambda h, j: (j, h)),
                  whole(HEAD_PAD), whole(V_DIM), stat, stat, pl.BlockSpec(memory_space=pl.ANY)],
        out_specs=[whole(HEAD_PAD), pl.BlockSpec((T, HEAD_PAD), lambda h, j: (j, h)),
                   pl.BlockSpec((T, V_DIM), lambda h, j: (j, h)), pl.BlockSpec(memory_space=pl.ANY)],
        scratch_shapes=[pltpu.VMEM((T, T), f32)] * 4 + EXCHANGE_SEMS,
        compiler_params=_cparams(("arbitrary", "arbitrary")),
    )(k, v, q, d_o, lse2, delta, by_target)


def _bwd_proj(x, proj, d_q, d_k, d_v, d_za, d_cbzc, d_u, d_x2, tabs, g_in, w_in_p, g_cq, w_uq_p, g_ckv, w_ukv_p,
              g_q_p, g_k_p, conv_w8):
    S = x.shape[0]
    T = min(TS_E, S)
    nt = S // T

    def body(x_ref, cqkv_ref, ccx_ref, dq_ref, dk_ref, dv_ref, dza_ref, dcbzc_ref, du_ref, dun_ref, dx2_ref,
             c_ref, s1_ref, s2_ref, g_in_ref, w_in_ref, g_cq_ref, w_uq_ref, g_ckv_ref, w_ukv_ref, g_q_ref, g_k_ref, cw_ref,
             gx_ref, dwin_hbm, dwuq_ref, dwukv_ref, dgin_ref, dgcq_ref, dgckv_ref, dgq_ref, dgk_ref,
             dproj_a, dproj_b, dqraw_sc, dkvraw_sc, dwin_sc, sem):
        i = pl.program_id(0)

        @pl.when(i == 0)
        def _():
            dwin_sc[...] = jnp.zeros(dwin_sc.shape, f32)
            dproj_b[...] = jnp.zeros(dproj_b.shape, MM)
            for r in (dwuq_ref, dwukv_ref, dgin_ref, dgcq_ref, dgckv_ref, dgq_ref, dgk_ref):
                r[...] = jnp.zeros(r.shape, f32)

        def stage1(dproj_sc):
            live = (i < nt).astype(f32)
            c, s1, s2 = c_ref[...], s1_ref[...], s2_ref[...]
            g_q, g_k = g_q_ref[...], g_k_ref[...]
            g_cq, g_ckv = g_cq_ref[...], g_ckv_ref[...]

            c_q = cqkv_ref[:, 0:Q_LORA]
            r_cq = _rstd(c_q, Q_LORA)
            cqh = c_q * r_cq
            n_cq = (cqh * g_cq).astype(MM)
            q_raw = _nn(n_cq, w_uq_ref[...])
            dgq = jnp.zeros((1, HEAD_PAD), f32)
            for hd in range(N_HEADS):
                cols = slice(hd * HEAD_PAD, (hd + 1) * HEAD_PAD)
                qh = q_raw[:, cols]
                rq = _rstd(qh, QK_DIM)
                qhh = qh * rq
                dqh = dq_ref[:, cols] * QK_SCALE
                d_qn = jnp.concatenate([dqh[:, :LANES], _rope_t(dqh[:, LANES:], c, s1, s2)], axis=-1)
                dgq += jnp.sum(d_qn * qhh, axis=0, keepdims=True)
                dqraw_sc[:, cols] = _rms_bwd(d_qn, qhh, rq, g_q, QK_DIM).astype(MM)
            dgq_ref[...] += dgq * live
            d_qraw = dqraw_sc[...]
            dwuq_ref[...] += _tn((cqh * (g_cq * live)).astype(MM), d_qraw)
            d_ncq = _nt(d_qraw, w_uq_ref[...])
            dgcq_ref[...] += jnp.sum(d_ncq * cqh, axis=0, keepdims=True) * live
            dproj_sc[:, 0:Q_LORA] = _rms_bwd(d_ncq, cqh, r_cq, g_cq, Q_LORA).astype(MM)

            c_kv = cqkv_ref[:, Q_LORA:Q_LORA + KV_LORA]
            r_ckv = _rstd(c_kv, KV_LORA)
            ckvh = c_kv * r_ckv
            n_ckv = (ckvh * g_ckv).astype(MM)
            k_nope = _nn(n_ckv, w_ukv_ref[:, :N_HEADS * NOPE_DIM])
            kpe = cqkv_ref[:, Q_LORA + KV_LORA:Q_LORA + KV_LORA + LANES]
            ss_pe = jnp.sum(kpe * kpe, axis=-1, keepdims=True)
            g_kn, g_kp = g_k[:, :LANES], g_k[:, LANES:]
            d_kpe = jnp.zeros((T, LANES), f32)
            dgk_n = jnp.zeros((1, LANES), f32)
            dgk_p = jnp.zeros((1, LANES), f32)
            for hd in range(N_HEADS):
                kn = k_nope[:, hd * LANES:(hd + 1) * LANES]
                rk = lax.rsqrt((jnp.sum(kn * kn, axis=-1, keepdims=True) + ss_pe) * (1.0 / QK_DIM) + RMS_EPS)
                knh, kph = kn * rk, kpe * rk
                d_kn_n = dk_ref[:, hd * HEAD_PAD:hd * HEAD_PAD + LANES] * QK_SCALE
                d_kr = _rope_t(dk_ref[:, hd * HEAD_PAD + LANES:(hd + 1) * HEAD_PAD] * QK_SCALE, c, s1, s2)
                dgk_n += jnp.sum(d_kn_n * knh, axis=0, keepdims=True)
                dgk_p += jnp.sum(d_kr * kph, axis=0, keepdims=True)
                u_n, u_p = d_kn_n * g_kn, d_kr * g_kp
                mt = (jnp.sum(u_n * knh, axis=-1, keepdims=True) + jnp.sum(u_p * kph, axis=-1, keepdims=True)) * (1.0 / QK_DIM)
                dkvraw_sc[:, hd * LANES:(hd + 1) * LANES] = (rk * (u_n - knh * mt)).astype(MM)
                d_kpe += rk * (u_p - kph * mt)
            dgk_ref[:, :LANES] += dgk_n * live
            dgk_ref[:, LANES:] += dgk_p * live
            dkvraw_sc[:, N_HEADS * NOPE_DIM:] = dv_ref[...].astype(MM)
            d_kvraw = dkvraw_sc[...]
            dwukv_ref[...] += _tn((ckvh * (g_ckv * live)).astype(MM), d_kvraw)
            d_nckv = _nt(d_kvraw, w_ukv_ref[...])
            dgckv_ref[...] += jnp.sum(d_nckv * ckvh, axis=0, keepdims=True) * live
            dproj_sc[:, Q_LORA:Q_LORA + KV_LORA] = _rms_bwd(d_nckv, ckvh, r_ckv, g_ckv, KV_LORA).astype(MM)
            dproj_sc[:, Q_LORA + KV_LORA:Q_LORA + KV_LORA + LANES] = d_kpe.astype(MM)

            dproj_sc[:, 512:1024] = dza_ref[...]
            d_u = du_ref[...]
            nxt = dun_ref[...] * (i < nt - 1).astype(f32)
            d_m = cw_ref[2:3, :] * d_u + cw_ref[1:2, :] * _shift_up(d_u, nxt, 1) + cw_ref[0:1, :] * _shift_up(d_u, nxt, 2)
            dproj_sc[:, 1024:1536] = (d_m * ccx_ref[:, CONV_WIDTH:]).astype(MM)
            dproj_sc[:, 1536:2048] = (d_m * ccx_ref[:, :CONV_WIDTH]).astype(MM)
            dproj_sc[:, 2048:3072] = dcbzc_ref[...]

        def stage2(dproj_sc):
            g_in = g_in_ref[...]
            xv = x_ref[...]
            r_in = _rstd(xv, D_MODEL)
            xh = xv * r_in
            hb = (xh * g_in).astype(MM)
            d_proj = dproj_sc[...]
            dwin_sc[...] += _tn(hb, d_proj)
            d_h = _nt(d_proj, w_in_ref[...])
            dgin_ref[...] += jnp.sum(d_h * xh, axis=0, keepdims=True)
            gx_ref[...] = dx2_ref[...] + _rms_bwd(d_h, xh, r_in, g_in, D_MODEL)

        @pl.when(i % 2 == 0)
        def _():
            stage2(dproj_b)
            stage1(dproj_a)

        @pl.when(i % 2 == 1)
        def _():
            stage2(dproj_a)
            stage1(dproj_b)

        @pl.when(i == nt)
        def _():
            cp = pltpu.make_async_copy(dwin_sc, dwin_hbm, sem)
            cp.start()
            cp.wait()

    cur = lambda i: jnp.minimum(i, nt - 1)
    prv = lambda i: jnp.maximum(i - 1, 0)
    row = lambda w, blk=0: pl.BlockSpec((T, w), lambda i: (cur(i), blk))
    row2 = lambda w: pl.BlockSpec((T, w), lambda i: (prv(i), 0))
    acc = lambda shape: pl.BlockSpec(shape, lambda i: (0, 0))
    tb = T // 8
    sds = jax.ShapeDtypeStruct
    return pl.pallas_call(
        body, name="bwd_proj", grid=(nt + 1,),
        out_shape=[sds((S, D_MODEL), f32), sds((D_MODEL, PROJ_PAD), f32), sds((Q_LORA, QK_PAD), f32),
                   sds((KV_LORA, 2 * ATTN_WIDTH), f32), sds((1, D_MODEL), f32), sds((1, Q_LORA), f32), sds((1, KV_LORA), f32),
                   sds((1, HEAD_PAD), f32), sds((1, HEAD_PAD), f32)],
        in_specs=[row2(D_MODEL), row(512, 0), row(2 * CONV_WIDTH, 1), row(QK_PAD), row(QK_PAD), row(ATTN_WIDTH),
                  row(ATTN_WIDTH), row(2 * CONV_WIDTH), row(CONV_WIDTH),
                  pl.BlockSpec((8, CONV_WIDTH), lambda i: (jnp.minimum((cur(i) + 1) * tb, S // 8 - 1), 0)),
                  row2(D_MODEL), row(LANES), row(LANES), row(LANES),
                  _const((1, D_MODEL)), _const((D_MODEL, PROJ_PAD)), _const((1, Q_LORA)), _const((Q_LORA, QK_PAD)),
                  _const((1, KV_LORA)), _const((KV_LORA, 2 * ATTN_WIDTH)), _const((1, HEAD_PAD)), _const((1, HEAD_PAD)),
                  _const((8, CONV_WIDTH))],
        out_specs=[row2(D_MODEL), pl.BlockSpec(memory_space=pl.ANY), acc((Q_LORA, QK_PAD)), acc((KV_LORA, 2 * ATTN_WIDTH)),
                   acc((1, D_MODEL)), acc((1, Q_LORA)), acc((1, KV_LORA)), acc((1, HEAD_PAD)), acc((1, HEAD_PAD))],
        scratch_shapes=[pltpu.VMEM((T, PROJ_PAD), MM), pltpu.VMEM((T, PROJ_PAD), MM), pltpu.VMEM((T, QK_PAD), MM),
                        pltpu.VMEM((T, 2 * ATTN_WIDTH), MM), pltpu.VMEM((D_MODEL, PROJ_PAD), f32), pltpu.SemaphoreType.DMA],
        compiler_params=_cparams(("arbitrary",)),
    )(x, proj, proj, d_q, d_k, d_v, d_za, d_cbzc, d_u, d_u, d_x2, *tabs, g_in, w_in_p, g_cq, w_uq_p, g_ckv, w_ukv_p,
      g_q_p, g_k_p, conv_w8)


def _all_gather(block, name, in_vmem):
    m_per, n = block.shape

    def body(x_ref, out_ref, send_sems, recv_sems, local_sem):
        x, y, c = lax.axis_index("x"), lax.axis_index("y"), lax.axis_index("c")
        me, sibling = (x, y, c), (x, y, 1 - c)
        chips = [(1 - x, y), (x, 1 - y), (1 - x, 1 - y)]

        def rows(px, py, pc):
            return out_ref.at[pl.ds((4 * px + 2 * py + pc) * m_per, m_per), :]

        def copy(k, blk, to, src=None):
            return pltpu.make_async_remote_copy(
                src_ref=rows(*blk) if src is None else src, dst_ref=rows(*blk),
                send_sem=send_sems.at[k], recv_sem=recv_sems.at[k], device_id=to, device_id_type=MESH)

        mine = pltpu.make_async_copy(x_ref, rows(*me), local_sem)
        mine.start()
        first = [copy(0, me, sibling, src=x_ref)]
        first += [copy(1 + j, me, (*chip, c), src=x_ref) for j, chip in enumerate(chips)]
        for cp in first:
            cp.start()
        passed = [copy(4 + j, (*chip, c), sibling) for j, chip in enumerate(chips)]
        for j, chip in enumerate(chips):
            copy(1 + j, (*chip, c), me).wait_recv()
            passed[j].start()
        copy(0, sibling, me).wait_recv()
        for j, chip in enumerate(chips):
            copy(4 + j, (*chip, 1 - c), me).wait_recv()
        for cp in first + passed:
            cp.wait_send()
        mine.wait()

    space = pltpu.VMEM if in_vmem else pl.ANY
    return pl.pallas_call(
        body, name=name, out_shape=jax.ShapeDtypeStruct((N_DEV * m_per, n), block.dtype),
        in_specs=[pl.BlockSpec(memory_space=space)], out_specs=pl.BlockSpec(memory_space=space),
        scratch_shapes=[pltpu.SemaphoreType.DMA((7,)), pltpu.SemaphoreType.DMA((7,)), pltpu.SemaphoreType.DMA],
    )(block)


def _sibling_exchange(pack):
    _, nchip, R, n = pack.shape

    def body(p_ref, got_ref, send_sem, recv_sem):
        x, y, c = lax.axis_index("x"), lax.axis_index("y"), lax.axis_index("c")
        cp = pltpu.make_async_remote_copy(src_ref=p_ref.at[1 - c], dst_ref=got_ref, send_sem=send_sem, recv_sem=recv_sem,
                                          device_id=(x, y, 1 - c), device_id_type=MESH)
        cp.start()
        cp.wait()

    return pl.pallas_call(
        body, name="rs_sibling", out_shape=jax.ShapeDtypeStruct((nchip, R, n), pack.dtype),
        in_specs=[pl.BlockSpec(memory_space=pl.ANY)], out_specs=pl.BlockSpec(memory_space=pl.ANY),
        scratch_shapes=[pltpu.SemaphoreType.DMA, pltpu.SemaphoreType.DMA],
    )(pack)


def _chip_exchange(part):
    nchip, R, n = part.shape

    def body(p_ref, got_ref, send_sems, recv_sems, local_sem):
        x, y, c = lax.axis_index("x"), lax.axis_index("y"), lax.axis_index("c")
        my_chip = 2 * x + y
        chips = [(1 - x, y), (x, 1 - y), (1 - x, 1 - y)]
        mine = pltpu.make_async_copy(p_ref.at[my_chip], got_ref.at[my_chip], local_sem)
        mine.start()
        sends = []
        for k, (tx, ty) in enumerate(chips):
            cp = pltpu.make_async_remote_copy(src_ref=p_ref.at[2 * tx + ty], dst_ref=got_ref.at[my_chip],
                                              send_sem=send_sems.at[k], recv_sem=recv_sems.at[k],
                                              device_id=(tx, ty, c), device_id_type=MESH)
            cp.start()
            sends.append(cp)
        for k, (sx, sy) in enumerate(chips):
            pltpu.make_async_remote_copy(src_ref=p_ref.at[my_chip], dst_ref=got_ref.at[2 * sx + sy],
                                         send_sem=send_sems.at[k], recv_sem=recv_sems.at[k],
                                         device_id=(sx, sy, c), device_id_type=MESH).wait_recv()
        for cp in sends:
            cp.wait_send()
        mine.wait()

    return pl.pallas_call(
        body, name="rs_chips", out_shape=jax.ShapeDtypeStruct((nchip, R, n), part.dtype),
        in_specs=[pl.BlockSpec(memory_space=pl.ANY)], out_specs=pl.BlockSpec(memory_space=pl.ANY),
        scratch_shapes=[pltpu.SemaphoreType.DMA((3,)), pltpu.SemaphoreType.DMA((3,)), pltpu.SemaphoreType.DMA],
    )(part)


def _add_blocks(a, b):
    nb, R, n = a.shape
    t = R // 5 if R % 80 == 0 else R

    def body(a_ref, b_ref, o_ref):
        o_ref[...] = (a_ref[...] + b_ref[...]).astype(o_ref.dtype)

    spec = pl.BlockSpec((1, t, n), lambda i, j: (i, j, 0))
    return pl.pallas_call(body, name="rs_add", grid=(nb, R // t), out_shape=jax.ShapeDtypeStruct(a.shape, RS_DTYPE),
                          in_specs=[spec, spec], out_specs=spec,
                          compiler_params=_cparams(("parallel", "parallel")))(a, b)


def _sum_chips(got):
    nb, R, n = got.shape
    t = R // 5 if R % 80 == 0 else R

    def body(g_ref, o_ref):
        g = g_ref[...].astype(f32)
        o_ref[...] = ((g[0] + g[1]) + g[2]) + g[3]

    return pl.pallas_call(body, name="rs_sum", grid=(R // t,), out_shape=jax.ShapeDtypeStruct((R, n), f32),
                          in_specs=[pl.BlockSpec((nb, t, n), lambda j: (0, j, 0))], out_specs=pl.BlockSpec((t, n), lambda j: (j, 0)),
                          compiler_params=_cparams(("parallel",)))(got)


def _sum_slots(got):
    nb, R, n = got.shape
    t = R // 5 if R % 80 == 0 else R

    def body(g_ref, o_ref):
        tot = g_ref[0].astype(f32)
        for d in range(1, nb):
            tot = tot + g_ref[d].astype(f32)
        o_ref[...] = tot

    return pl.pallas_call(body, name="rs_sum8", grid=(R // t,), out_shape=jax.ShapeDtypeStruct((R, n), f32),
                          in_specs=[pl.BlockSpec((nb, t, n), lambda j: (0, j, 0))], out_specs=pl.BlockSpec((t, n), lambda j: (j, 0)),
                          compiler_params=_cparams(("parallel",)))(got)


def _adamw_all(ws, gs, ms, vs, gain_parts):
    n = len(ws)
    ng = len(GAIN_SLOTS)

    def body(*refs):
        gp_ref = refs[0]
        w_refs, g_refs, m_refs, v_refs = (refs[1 + k * n:1 + (k + 1) * n] for k in range(4))
        outs = refs[1 + 4 * n:]
        gsum_ref, loss_ref = outs[0], outs[1]
        gg_refs = outs[2:2 + ng]
        d_refs, nm_refs, nv_refs = (outs[2 + ng + k * n:2 + ng + (k + 1) * n] for k in range(3))
        tot = gp_ref[0]
        for d in range(1, N_DEV):
            tot = tot + gp_ref[d]
        gsum_ref[...] = tot
        loss_ref[...] = gsum_ref[GAIN_ROWS - 1:GAIN_ROWS, :]
        for k in range(n):
            if k < ng:
                r0, width = GAIN_SLOTS[k]
                nr = width // LANES
                for r in range(nr):
                    gg_refs[k][:, r * LANES:(r + 1) * LANES] = gsum_ref[r0 + r:r0 + r + 1, :]
                g = gg_refs[k][...]
            else:
                g = g_refs[k][...]
            w = w_refs[k][...]
            m = ADAM_B1 * m_refs[k][...] + (1.0 - ADAM_B1) * g
            v = ADAM_B2 * v_refs[k][...] + (1.0 - ADAM_B2) * (g * g)
            m_hat = m / (1.0 - ADAM_B1 ** ADAM_STEP)
            v_hat = v / (1.0 - ADAM_B2 ** ADAM_STEP)
            d_refs[k][...] = -ADAM_LR * (m_hat / (jnp.sqrt(v_hat) + ADAM_EPS) + ADAM_WD * w)
            nm_refs[k][...] = m
            nv_refs[k][...] = v

    sds = jax.ShapeDtypeStruct
    like = [sds(w.shape, f32) for w in ws]
    out_shape = [sds((GAIN_ROWS, LANES), f32), sds((1, LANES), f32)] + like[:ng] + like * 3
    vm = pl.BlockSpec(memory_space=pltpu.VMEM)
    outs = pl.pallas_call(
        body, name="adamw", out_shape=out_shape, in_specs=[vm] * (1 + 4 * n), out_specs=[vm] * len(out_shape),
        compiler_params=pltpu.CompilerParams(vmem_limit_bytes=VMEM_LIMIT),
    )(gain_parts, *ws, *gs, *ms, *vs)
    loss = outs[1]
    gg = outs[2:2 + ng]
    deltas, new_m, new_v = (outs[2 + ng + k * n:2 + ng + (k + 1) * n] for k in range(3))
    return loss, gg, deltas, new_m, new_v


GAIN_ROWS = 32
GAIN_SLOTS = [(0, 1024), (8, 256), (10, 128), (11, 256), (13, 256), (15, 512), (19, 512), (23, 1024)]

W1_ROWS = [3008, 192, 128]
W2_ROWS = [16, 1024, 256, 1024]
GA_ROWS = [16, 1024, 256, 1024]
GB_ROWS = [3008, 192, 128, 32]


def _pad_rows(a, rows):
    flat = a.reshape(-1)
    return jnp.pad(flat, (0, rows * LANES - flat.shape[0])).reshape(rows, LANES)


def _split_rows(a, rows_list):
    out, r = [], 0
    for n in rows_list:
        out.append(a[..., r:r + n, :])
        r += n
    return out


def kernel(x, p, positions, g_in, w_in, g_cq, w_uq, g_ckv, w_ukv, g_q, g_k, conv_w, g_oa, g_oc, w_o, w_pl, w_plg, g_pl, loss_target, m_g_in, m_w_in, m_g_cq, m_w_uq, m_g_ckv, m_w_ukv, m_g_q, m_g_k, m_conv_w, m_g_oa, m_g_oc, m_w_o, m_w_pl, m_w_plg, m_g_pl, v_g_in, v_w_in, v_g_cq, v_w_uq, v_g_ckv, v_w_ukv, v_g_q, v_g_k, v_conv_w, v_g_oa, v_g_oc, v_w_o, v_w_pl, v_w_plg, v_g_pl):
    S = x.shape[1]
    nd = N_DEV
    xs, ps, tgt = x[0], p[0, 0], loss_target[0]

    def pack_shards(shards, rows):
        return jnp.concatenate([_pad_rows(s.astype(bf16), r) for s, r in zip(shards, rows)], axis=0)

    gathered = _all_gather(pack_shards([w_in[0], w_uq[0], w_ukv[0]], W1_ROWS), "ag_weights", in_vmem=False)
    a_in, a_uq, a_ukv = _split_rows(gathered.reshape(nd, -1, LANES), W1_ROWS)
    conv_bits = lax.bitcast_convert_type(conv_w[0], bf16)
    pack2 = pack_shards([conv_bits, w_o[0], w_pl[0], w_plg[0]], W2_ROWS)

    def cols(a, rows, c):
        return a.reshape(nd, rows, c).transpose(1, 0, 2).reshape(rows, nd * c)

    win = cols(a_in, D_MODEL, IN_TOTAL // nd)
    zpad = jnp.zeros((D_MODEL, ROPE_DIM), bf16)
    w_in_p = jnp.concatenate([win[:, 0:448], zpad, win[:, 448:960], win[:, 1472:2496], win[:, 960:1472], win[:, 2496:3008]], axis=1)
    wuq = cols(a_uq, Q_LORA, N_HEADS * QK_DIM // nd).reshape(Q_LORA, N_HEADS, QK_DIM)
    w_uq_p = jnp.pad(wuq, ((0, 0), (0, 0), (0, HEAD_PAD - QK_DIM))).reshape(Q_LORA, QK_PAD)
    wukv = cols(a_ukv, KV_LORA, 2 * ATTN_WIDTH // nd).reshape(KV_LORA, N_HEADS, 2, NOPE_DIM)
    w_ukv_p = wukv.transpose(0, 2, 1, 3).reshape(KV_LORA, 2 * ATTN_WIDTH)
    w_in_p, w_uq_p, w_ukv_p = (w.astype(MM) for w in (w_in_p, w_uq_p, w_ukv_p))

    g_q_p = jnp.pad(g_q, ((0, 0), (0, HEAD_PAD - QK_DIM)))
    g_k_p = jnp.pad(g_k, ((0, 0), (0, HEAD_PAD - QK_DIM)))

    inv_freq = 1.0 / (ROPE_THETA ** (jnp.arange(0, ROPE_DIM, 2, dtype=f32) / ROPE_DIM))
    inv_row = jnp.concatenate([inv_freq, inv_freq, jnp.zeros((LANES - ROPE_DIM,), f32)]).reshape(1, LANES)
    posf = jnp.broadcast_to(positions[0].astype(f32)[:, None], (S, LANES))
    tabs = _rope_tables(posf, inv_row)

    proj, q, k, v = _fwd_proj(xs, tabs, g_in, w_in_p, g_cq, w_uq_p, g_ckv, w_ukv_p, g_q_p, g_k_p)
    o, lse, gathered2 = _flash_fwd(q, k, v, pack2)
    a_cv, a_o, a_pl, a_plg = _split_rows(gathered2, W2_ROWS)
    cv_bits = a_cv.reshape(nd, -1)[:, :CONV_K * (CONV_WIDTH // nd) * 2].reshape(nd, CONV_K, CONV_WIDTH // nd, 2)
    conv_full = lax.bitcast_convert_type(cv_bits, f32).transpose(1, 0, 2).reshape(CONV_K, CONV_WIDTH)
    conv_w8 = jnp.pad(conv_full, ((0, 8 - CONV_K), (0, 0)))
    w_o_f = a_o.reshape(D_MODEL, D_MODEL).astype(MM)
    w_pl_f = cols(a_pl, PLE_DIM, D_MODEL // nd).astype(MM)
    w_plg_f = a_plg.reshape(D_MODEL, D_MODEL).astype(MM)
    (d_x2, d_o, delta, d_za, d_cbzc, d_u, dw_o, dw_plg, dw_pl, dcw, dg_oa, dg_oc, dg_pl, loss_part) = _mid(
        o, proj, xs, ps, tgt, w_o_f, w_plg_f, w_pl_f, conv_w8, g_oa, g_oc, g_pl)

    def shard_cols(a):
        rows = a.shape[0]
        return a.reshape(rows, nd, -1).transpose(1, 0, 2).reshape(nd, -1)

    def pack_grads(flats, rows):
        return jnp.concatenate([jnp.pad(fl, ((0, 0), (0, r * LANES - fl.shape[1]))).reshape(nd, r, LANES)
                                for fl, r in zip(flats, rows)], axis=1)

    gpack_a = pack_grads([shard_cols(dcw[:CONV_K]), dw_o.reshape(nd, -1), shard_cols(dw_pl), dw_plg.reshape(nd, -1)],
                         GA_ROWS).astype(RS_DTYPE)
    d_q, d_k, d_v, got_a = _flash_bwd(q, k, v, d_o, lse, delta, gpack_a)
    s_cv, s_o, s_pl, s_plg = _split_rows(_sum_slots(got_a), GA_ROWS)
    (grad_x, dw_in_p, dw_uq_p, dw_ukv_p, dg_in, dg_cq, dg_ckv, dg_q_p, dg_k_p) = _bwd_proj(
        xs, proj, d_q, d_k, d_v, d_za, d_cbzc, d_u, d_x2, tabs, g_in, w_in_p, g_cq, w_uq_p, g_ckv, w_ukv_p, g_q_p, g_k_p, conv_w8)

    dw_in = jnp.concatenate([dw_in_p[:, 0:448], dw_in_p[:, 512:1024], dw_in_p[:, 2048:2560], dw_in_p[:, 1024:2048],
                             dw_in_p[:, 2560:3072]], axis=1)
    dw_uq = dw_uq_p.reshape(Q_LORA, N_HEADS, HEAD_PAD)[:, :, :QK_DIM].reshape(Q_LORA, N_HEADS * QK_DIM)
    dw_ukv = dw_ukv_p.reshape(KV_LORA, 2, N_HEADS, NOPE_DIM).transpose(0, 2, 1, 3).reshape(KV_LORA, 2 * ATTN_WIDTH)
    gpack = pack_grads([shard_cols(dw_in), shard_cols(dw_uq), shard_cols(dw_ukv), jnp.zeros((nd, 0), f32)], GB_ROWS)
    Rg = gpack.shape[1]
    gpack = gpack.reshape(4, 2, Rg, LANES).transpose(1, 0, 2, 3)
    my_c = lax.axis_index("c")
    from_sib = _sibling_exchange(gpack)
    mine = lax.dynamic_index_in_dim(gpack, my_c, axis=0, keepdims=False)
    chip_part = _add_blocks(mine, from_sib)
    gshard = _sum_chips(_chip_exchange(chip_part))
    s_in, s_uq, s_ukv, _ = _split_rows(gshard, GB_ROWS)
    grads_w = [s_in.reshape(D_MODEL, -1), s_uq.reshape(Q_LORA, -1), s_ukv.reshape(KV_LORA, -1),
               s_cv.reshape(-1)[:CONV_K * CONV_WIDTH // nd].reshape(CONV_K, -1), s_o.reshape(-1, D_MODEL),
               s_pl.reshape(PLE_DIM, -1), s_plg.reshape(-1, D_MODEL)]

    gains_part = [dg_in, dg_cq, dg_ckv, dg_q_p, dg_k_p, dg_oa, dg_oc, dg_pl]
    gflat = jnp.concatenate([g.reshape(-1) for g in gains_part] + [jnp.zeros(((GAIN_ROWS - 1) * LANES - 3968,), f32),
                                                                    loss_part.reshape(-1)])
    gain_parts = _all_gather(gflat.reshape(GAIN_ROWS, LANES), "ag_gains", in_vmem=True).reshape(nd, GAIN_ROWS, LANES)

    gains = [g_in, g_cq, g_ckv, g_q_p, g_k_p, g_oa, g_oc, g_pl]
    padq = lambda a: jnp.pad(a, ((0, 0), (0, HEAD_PAD - QK_DIM)))
    m_gains = [m_g_in, m_g_cq, m_g_ckv, padq(m_g_q), padq(m_g_k), m_g_oa, m_g_oc, m_g_pl]
    v_gains = [v_g_in, v_g_cq, v_g_ckv, padq(v_g_q), padq(v_g_k), v_g_oa, v_g_oc, v_g_pl]
    ws = gains + [w_in[0], w_uq[0], w_ukv[0], conv_w[0], w_o[0], w_pl[0], w_plg[0]]
    ms = m_gains + [m_w_in[0], m_w_uq[0], m_w_ukv[0], m_conv_w[0], m_w_o[0], m_w_pl[0], m_w_plg[0]]
    vs = v_gains + [v_w_in[0], v_w_uq[0], v_w_ukv[0], v_conv_w[0], v_w_o[0], v_w_pl[0], v_w_plg[0]]
    gs = [jnp.zeros_like(g) for g in gains] + grads_w
    loss_row, gg, deltas, new_m, new_v = _adamw_all(ws, gs, ms, vs, gain_parts)
    loss = loss_row[0, 0]

    def ordered(gl, wl):
        g_in_, g_cq_, g_ckv_, g_q_, g_k_, g_oa_, g_oc_, g_pl_ = gl
        g_q_, g_k_ = g_q_[:, :QK_DIM], g_k_[:, :QK_DIM]
        w_in_, w_uq_, w_ukv_, cw_, w_o_, w_pl_, w_plg_ = [w[None] for w in wl]
        return [g_in_, w_in_, g_cq_, w_uq_, g_ckv_, w_ukv_, g_q_, g_k_, cw_, g_oa_, g_oc_, w_o_, w_pl_, w_plg_, g_pl_]

    ng = len(gains)
    outs = [loss, grad_x[None]]
    outs += ordered(gg, grads_w)
    for lst in (deltas, new_m, new_v):
        outs += ordered(lst[:ng], lst[ng:])
    return tuple(outs)
```

```python
import functools
import math

import jax
import jax.numpy as jnp
from jax import lax
from jax.experimental import pallas as pl
from jax.experimental.pallas import tpu as pltpu

f32 = jnp.float32
bf16 = jnp.bfloat16
MM = jnp.bfloat16
RS_DTYPE = jnp.bfloat16

D_MODEL = 1024
PLE_DIM = 256
N_HEADS = 4
NOPE_DIM = 128
ROPE_DIM = 64
V_DIM = 128
QK_DIM = NOPE_DIM + ROPE_DIM
Q_LORA = 256
KV_LORA = 128
ATTN_WIDTH = N_HEADS * V_DIM
CONV_WIDTH = D_MODEL - ATTN_WIDTH
CONV_K = 3
ROPE_THETA = 10000.0
RMS_EPS = 1e-6
NEG_INF = -1e30
IN_TOTAL = Q_LORA + KV_LORA + ROPE_DIM + ATTN_WIDTH + 4 * CONV_WIDTH
ADAM_LR = 0.001
ADAM_B1 = 0.9
ADAM_B2 = 0.999
ADAM_EPS = 1e-08
ADAM_WD = 0.01
ADAM_STEP = 10

N_DEV = 8
LANES = 128
HEAD_PAD = 256
QK_PAD = N_HEADS * HEAD_PAD
PROJ_PAD = 3072
TS = 512
TS_M = 256
TS_E = 256
TQ = 512
VMEM_LIMIT = 58 * 1024 * 1024
MESH = pl.DeviceIdType.MESH
QK_SCALE = 1.0 / math.sqrt(QK_DIM)
SCALE_LOG2E = QK_SCALE * math.log2(math.e)


def _cparams(sem=None):
    return pltpu.CompilerParams(dimension_semantics=sem, vmem_limit_bytes=VMEM_LIMIT)


def _nt(a, b):
    return lax.dot_general(a, b, (((1,), (1,)), ((), ())), preferred_element_type=f32)


def _tn(a, b):
    return lax.dot_general(a, b, (((0,), (0,)), ((), ())), preferred_element_type=f32)


def _nn(a, b):
    return jnp.dot(a, b, preferred_element_type=f32)


def _rstd(x, n):
    return lax.rsqrt(jnp.sum(x * x, axis=-1, keepdims=True) * (1.0 / n) + RMS_EPS)


def _rms_bwd(dy, xhat, r, g, n):
    u = dy * g
    return r * (u - xhat * (jnp.sum(u * xhat, axis=-1, keepdims=True) * (1.0 / n)))


def _rope(t, c, s1, s2):
    return t * c + pltpu.roll(t, 32, 1) * s1 + pltpu.roll(t, 96, 1) * s2


def _rope_t(d, c, s1, s2):
    return d * c - pltpu.roll(d, 96, 1) * s2 - pltpu.roll(d, 32, 1) * s1


def _const(shape):
    return pl.BlockSpec(shape, lambda *_: (0,) * len(shape), pipeline_mode=pl.Buffered(1))


def _rope_tables(posf, inv_freq):
    S = posf.shape[0]

    def body(pos_ref, f_ref, c_ref, s1_ref, s2_ref):
        ang = pos_ref[...] * f_ref[...]
        lane = lax.broadcasted_iota(jnp.int32, ang.shape, 1)
        cs, sn = jnp.cos(ang), jnp.sin(ang)
        c_ref[...] = jnp.where(lane < ROPE_DIM, cs, 0.0)
        s1_ref[...] = jnp.where((lane >= ROPE_DIM // 2) & (lane < ROPE_DIM), sn, 0.0)
        s2_ref[...] = jnp.where(lane < ROPE_DIM // 2, -sn, 0.0)

    t = min(S, TS)
    spec = pl.BlockSpec((t, LANES), lambda i: (i, 0))
    return pl.pallas_call(
        body, name="rope_tables", grid=(S // t,),
        out_shape=[jax.ShapeDtypeStruct((S, LANES), f32)] * 3,
        in_specs=[spec, _const((1, LANES))], out_specs=[spec] * 3,
        compiler_params=_cparams(("parallel",)),
    )(posf, inv_freq)


def _fwd_proj(x, tabs, g_in, w_in_p, g_cq, w_uq_p, g_ckv, w_ukv_p, g_q_p, g_k_p):
    S = x.shape[0]

    def body(x_ref, c_ref, s1_ref, s2_ref, g_in_ref, w_in_ref, g_cq_ref, w_uq_ref, g_ckv_ref, w_ukv_ref,
             g_q_ref, g_k_ref, proj_ref, q_ref, k_ref, v_ref):
        xv = x_ref[...]
        h = (xv * _rstd(xv, D_MODEL) * g_in_ref[...]).astype(MM)
        proj_ref[...] = _nn(h, w_in_ref[...])
        c, s1, s2 = c_ref[...], s1_ref[...], s2_ref[...]
        g_q, g_k = g_q_ref[...], g_k_ref[...]

        c_q = proj_ref[:, 0:Q_LORA]
        n_cq = (c_q * _rstd(c_q, Q_LORA) * g_cq_ref[...]).astype(MM)
        q_raw = _nn(n_cq, w_uq_ref[...])
        for hd in range(N_HEADS):
            qh = q_raw[:, hd * HEAD_PAD:(hd + 1) * HEAD_PAD]
            qn = qh * _rstd(qh, QK_DIM) * g_q
            q_ref[:, hd * HEAD_PAD:hd * HEAD_PAD + LANES] = qn[:, :LANES].astype(MM)
            q_ref[:, hd * HEAD_PAD + LANES:(hd + 1) * HEAD_PAD] = _rope(qn[:, LANES:], c, s1, s2).astype(MM)

        c_kv = proj_ref[:, Q_LORA:Q_LORA + KV_LORA]
        n_ckv = (c_kv * _rstd(c_kv, KV_LORA) * g_ckv_ref[...]).astype(MM)
        kv_raw = _nn(n_ckv, w_ukv_ref[...])
        kpe = proj_ref[:, Q_LORA + KV_LORA:Q_LORA + KV_LORA + LANES]
        ss_pe = jnp.sum(kpe * kpe, axis=-1, keepdims=True)
        kr = _rope(kpe * g_k[:, LANES:], c, s1, s2)
        for hd in range(N_HEADS):
            kn = kv_raw[:, hd * LANES:(hd + 1) * LANES]
            rk = lax.rsqrt((jnp.sum(kn * kn, axis=-1, keepdims=True) + ss_pe) * (1.0 / QK_DIM) + RMS_EPS)
            k_ref[:, hd * HEAD_PAD:hd * HEAD_PAD + LANES] = (kn * rk * g_k[:, :LANES]).astype(MM)
            k_ref[:, hd * HEAD_PAD + LANES:(hd + 1) * HEAD_PAD] = (kr * rk).astype(MM)
        v_ref[...] = kv_raw[:, N_HEADS * NOPE_DIM:].astype(MM)

    row = lambda w: pl.BlockSpec((TS, w), lambda i: (i, 0))
    return pl.pallas_call(
        body, name="fwd_proj", grid=(S // TS,),
        out_shape=[jax.ShapeDtypeStruct((S, PROJ_PAD), f32), jax.ShapeDtypeStruct((S, QK_PAD), MM),
                   jax.ShapeDtypeStruct((S, QK_PAD), MM), jax.ShapeDtypeStruct((S, ATTN_WIDTH), MM)],
        in_specs=[row(D_MODEL), row(LANES), row(LANES), row(LANES), _const((1, D_MODEL)), _const((D_MODEL, PROJ_PAD)),
                  _const((1, Q_LORA)), _const((Q_LORA, QK_PAD)), _const((1, KV_LORA)), _const((KV_LORA, 2 * ATTN_WIDTH)),
                  _const((1, HEAD_PAD)), _const((1, HEAD_PAD))],
        out_specs=[row(PROJ_PAD), row(QK_PAD), row(QK_PAD), row(ATTN_WIDTH)],
        compiler_params=_cparams(("parallel",)),
    )(x, *tabs, g_in, w_in_p, g_cq, w_uq_p, g_ckv, w_ukv_p, g_q_p, g_k_p)


def _exchange(src_of, dst_ref, send_sems, recv_sems, local_sem):
    x, y, c = lax.axis_index("x"), lax.axis_index("y"), lax.axis_index("c")
    me = 4 * x + 2 * y + c
    remote = []
    for k in range(1, N_DEV):
        px = 1 - x if (k >> 2) & 1 else x
        py = 1 - y if (k >> 1) & 1 else y
        pc = 1 - c if k & 1 else c
        remote.append(pltpu.make_async_remote_copy(
            src_ref=src_of(4 * px + 2 * py + pc), dst_ref=dst_ref.at[me], send_sem=send_sems.at[k - 1],
            recv_sem=recv_sems.at[k - 1], device_id=(px, py, pc), device_id_type=MESH))
    return remote, pltpu.make_async_copy(src_of(me), dst_ref.at[me], local_sem)


def _exchange_start(copies):
    remote, local = copies
    local.start()
    for cp in remote:
        cp.start()


def _exchange_wait(copies):
    remote, local = copies
    for cp in remote:
        cp.wait_recv()
    for cp in remote:
        cp.wait_send()
    local.wait()


EXCHANGE_SEMS = [pltpu.SemaphoreType.DMA((N_DEV - 1,)), pltpu.SemaphoreType.DMA((N_DEV - 1,)), pltpu.SemaphoreType.DMA]


def _flash_fwd(q, k, v, block):
    S = q.shape[0]
    T = min(TQ, S)
    nq = S // T

    def body(q_ref, k_ref, v_ref, blk_ref, o_ref, lse_ref, all_ref, s_a, s_b, m_sc, l_sc, acc_sc, send_sems, recv_sems, local_sem):
        n = pl.program_id(1)
        gather = _exchange(lambda d: blk_ref, all_ref, send_sems, recv_sems, local_sem)

        @pl.when((pl.program_id(0) == 0) & (n == 0))
        def _():
            _exchange_start(gather)

        m_sc[...] = jnp.full(m_sc.shape, NEG_INF, f32)
        l_sc[...] = jnp.zeros(l_sc.shape, f32)
        acc_sc[...] = jnp.zeros(acc_sc.shape, f32)
        qv = q_ref[...]

        def rows(j):
            return pl.ds(pl.multiple_of(j * T, T), T)

        def scores(j, dst):
            dst[...] = _nt(k_ref[rows(j), :], qv)

        def update(src, j, masked):
            s = src[...]
            if masked:
                k_i = lax.broadcasted_iota(jnp.int32, (T, T), 0)
                q_i = lax.broadcasted_iota(jnp.int32, (T, T), 1)
                s = jnp.where(k_i <= q_i, s, NEG_INF)
            m_prev = m_sc[...]
            m_new = jnp.maximum(m_prev, jnp.max(s, axis=0, keepdims=True))
            alpha = jnp.exp2((m_prev - m_new) * SCALE_LOG2E)
            p = jnp.exp2((s - m_new[0:1, :]) * SCALE_LOG2E)
            l_sc[...] = alpha * l_sc[...] + jnp.sum(p, axis=0, keepdims=True)
            acc_sc[...] = alpha[0:1, :] * acc_sc[...] + _tn(v_ref[rows(j), :], p.astype(MM))
            m_sc[...] = m_new

        scores(0, s_a)

        def pair(t, carry):
            j = 2 * t
            scores(j + 1, s_b)
            update(s_a, j, False)
            scores(j + 2, s_a)
            update(s_b, j + 1, False)
            return carry

        lax.fori_loop(0, n // 2, pair, 0)

        @pl.when(n % 2 == 0)
        def _():
            update(s_a, n, True)

        @pl.when(n % 2 == 1)
        def _():
            scores(n, s_b)
            update(s_a, n - 1, False)
            update(s_b, n, True)

        o_ref[...] = (acc_sc[...] / l_sc[0:1, :]).T
        lse_ref[...] = m_sc[...] * SCALE_LOG2E + jnp.log2(l_sc[...])

        @pl.when((pl.program_id(0) == N_HEADS - 1) & (n == nq - 1))
        def _():
            _exchange_wait(gather)

    return pl.pallas_call(
        body, name="flash_fwd", grid=(N_HEADS, nq),
        out_shape=[jax.ShapeDtypeStruct((S, ATTN_WIDTH), f32), jax.ShapeDtypeStruct((N_HEADS * 8, S), f32),
                   jax.ShapeDtypeStruct((N_DEV,) + block.shape, block.dtype)],
        in_specs=[pl.BlockSpec((T, HEAD_PAD), lambda h, i: (i, h)),
                  pl.BlockSpec((S, HEAD_PAD), lambda h, i: (0, h)),
                  pl.BlockSpec((S, V_DIM), lambda h, i: (0, h)),
                  pl.BlockSpec(memory_space=pl.ANY)],
        out_specs=[pl.BlockSpec((T, V_DIM), lambda h, i: (i, h)), pl.BlockSpec((8, T), lambda h, i: (h, i)),
                   pl.BlockSpec(memory_space=pl.ANY)],
        scratch_shapes=[pltpu.VMEM((T, T), f32), pltpu.VMEM((T, T), f32), pltpu.VMEM((8, T), f32), pltpu.VMEM((8, T), f32),
                        pltpu.VMEM((V_DIM, T), f32)] + EXCHANGE_SEMS,
        compiler_params=_cparams(("arbitrary", "arbitrary")),
    )(q, k, v, block)


def _shift_down(m, prev8, n):
    T = m.shape[0]
    rows = lax.broadcasted_iota(jnp.int32, m.shape, 0)
    head = jnp.tile(pltpu.roll(prev8, n, 0), (T // 8, 1))
    return jnp.where(rows >= n, pltpu.roll(m, n, 0), head)


def _shift_up(m, next8, n):
    T = m.shape[0]
    rows = lax.broadcasted_iota(jnp.int32, m.shape, 0)
    tail = jnp.tile(pltpu.roll(next8, 8 - n, 0), (T // 8, 1))
    return jnp.where(rows < T - n, pltpu.roll(m, T - n, 0), tail)


def _mid(o, proj, x, p, target, w_o, w_plg, w_pl, conv_w8, g_oa, g_oc, g_pl):
    S = x.shape[0]
    T = min(TS_M, S)
    nt = S // T

    def body(o_ref, za_ref, ccx_ref, cbzc_ref, prev_ref, x_ref, p_ref, t_ref, wo_ref, wplg_ref, wpl_ref, cw_ref,
             goa_ref, goc_ref, gpl_ref,
             dx2_ref, do_ref, delta_ref, dza_ref, dcbzc_ref, du_ref,
             dwo_ref, dwplg_ref, dwpl_ref, dcw_ref, dgoa_ref, dgoc_ref, dgpl_ref, loss_ref):
        i = pl.program_id(0)

        @pl.when(i == 0)
        def _():
            for r in (dwo_ref, dwplg_ref, dwpl_ref, dcw_ref, dgoa_ref, dgoc_ref, dgpl_ref, loss_ref):
                r[...] = jnp.zeros(r.shape, f32)

        o_v = o_ref[...]
        z_a = za_ref[...]
        cc, cx = ccx_ref[:, :CONV_WIDTH], ccx_ref[:, CONV_WIDTH:]
        cb, z_c = cbzc_ref[:, :CONV_WIDTH], cbzc_ref[:, CONV_WIDTH:]
        g_oa, g_oc, g_pl = goa_ref[...], goc_ref[...], gpl_ref[...]
        w0, w1, w2 = cw_ref[0:1, :], cw_ref[1:2, :], cw_ref[2:3, :]

        sa = jax.nn.sigmoid(z_a)
        silu_a = z_a * sa
        ga = o_v * silu_a
        ra = _rstd(ga, ATTN_WIDTH)
        gha = ga * ra
        m0 = cc * cx
        prev = prev_ref[:, :CONV_WIDTH] * prev_ref[:, CONV_WIDTH:] * (i > 0).astype(f32)
        m1 = _shift_down(m0, prev, 1)
        m2 = _shift_down(m0, prev, 2)
        u = w0 * m2 + w1 * m1 + w2 * m0
        sc = jax.nn.sigmoid(z_c)
        silu_c = z_c * sc
        gc = cb * u * silu_c
        rc = _rstd(gc, CONV_WIDTH)
        ghc = gc * rc
        ycat = jnp.concatenate([gha * g_oa, ghc * g_oc], axis=-1).astype(MM)
        x2 = x_ref[...] + _nn(ycat, wo_ref[...])
        r2 = _rstd(x2, D_MODEL)
        xh2 = x2 * r2
        n2 = (xh2 * g_pl).astype(MM)
        gate = jax.nn.sigmoid(_nn(n2, wplg_ref[...]))
        p_b = p_ref[...].astype(MM)
        pw = _nn(p_b, wpl_ref[...])
        err = x2 + gate * pw - t_ref[...]
        loss_ref[...] += jnp.sum(jnp.sum(err * err, axis=-1, keepdims=True), axis=0, keepdims=True) * (0.5 / D_MODEL)
        d_out = err * (1.0 / D_MODEL)
        dwpl_ref[...] += _tn(p_b, (d_out * gate).astype(MM))
        d_glog = (d_out * pw * gate * (1.0 - gate)).astype(MM)
        dwplg_ref[...] += _tn(n2, d_glog)
        d_n2 = _nt(d_glog, wplg_ref[...])
        dgpl_ref[...] += jnp.sum(d_n2 * xh2, axis=0, keepdims=True)
        d_x2 = d_out + _rms_bwd(d_n2, xh2, r2, g_pl, D_MODEL)
        dx2_ref[...] = d_x2
        d_x2b = d_x2.astype(MM)
        dwo_ref[...] += _tn(ycat, d_x2b)
        d_ycat = _nt(d_x2b, wo_ref[...])
        d_ya, d_yc = d_ycat[:, :ATTN_WIDTH], d_ycat[:, ATTN_WIDTH:]
        dgoa_ref[...] += jnp.sum(d_ya * gha, axis=0, keepdims=True)
        dgoc_ref[...] += jnp.sum(d_yc * ghc, axis=0, keepdims=True)
        d_ga = _rms_bwd(d_ya, gha, ra, g_oa, ATTN_WIDTH)
        d_gc = _rms_bwd(d_yc, ghc, rc, g_oc, CONV_WIDTH)
        d_o = d_ga * silu_a
        do_ref[...] = d_o.astype(MM)
        dza_ref[...] = (d_ga * o_v * (sa * (1.0 + z_a * (1.0 - sa)))).astype(MM)
        for hd in range(N_HEADS):
            cols = slice(hd * V_DIM, (hd + 1) * V_DIM)
            dl = jnp.sum(d_o[:, cols] * o_v[:, cols], axis=-1, keepdims=True)
            delta_ref[hd * 8:(hd + 1) * 8, :] = jnp.broadcast_to(dl, (T, LANES)).T[0:8, :]
        d_u = d_gc * cb * silu_c
        du_ref[...] = d_u
        dcbzc_ref[:, :CONV_WIDTH] = (d_gc * u * silu_c).astype(MM)
        dcbzc_ref[:, CONV_WIDTH:] = (d_gc * cb * u * (sc * (1.0 + z_c * (1.0 - sc)))).astype(MM)
        dcw_ref[0:1, :] += jnp.sum(d_u * m2, axis=0, keepdims=True)
        dcw_ref[1:2, :] += jnp.sum(d_u * m1, axis=0, keepdims=True)
        dcw_ref[2:3, :] += jnp.sum(d_u * m0, axis=0, keepdims=True)

    row = lambda w, blk=0: pl.BlockSpec((T, w), lambda i: (i, blk))
    acc = lambda shape: pl.BlockSpec(shape, lambda i: (0, 0))
    tb = T // 8
    sds = jax.ShapeDtypeStruct
    return pl.pallas_call(
        body, name="mid", grid=(nt,),
        out_shape=[sds((S, D_MODEL), f32), sds((S, ATTN_WIDTH), MM), sds((N_HEADS * 8, S), f32), sds((S, ATTN_WIDTH), MM),
                   sds((S, 2 * CONV_WIDTH), MM), sds((S, CONV_WIDTH), f32),
                   sds((D_MODEL, D_MODEL), f32), sds((D_MODEL, D_MODEL), f32), sds((PLE_DIM, D_MODEL), f32), sds((8, CONV_WIDTH), f32),
                   sds((1, ATTN_WIDTH), f32), sds((1, CONV_WIDTH), f32), sds((1, D_MODEL), f32), sds((1, LANES), f32)],
        in_specs=[row(ATTN_WIDTH), row(ATTN_WIDTH, 1), row(2 * CONV_WIDTH, 1), row(2 * CONV_WIDTH, 2),
                  pl.BlockSpec((8, 2 * CONV_WIDTH), lambda i: (jnp.maximum(i * tb - 1, 0), 1)),
                  row(D_MODEL), row(PLE_DIM), row(D_MODEL),
                  _const((D_MODEL, D_MODEL)), _const((D_MODEL, D_MODEL)), _const((PLE_DIM, D_MODEL)), _const((8, CONV_WIDTH)),
                  _const((1, ATTN_WIDTH)), _const((1, CONV_WIDTH)), _const((1, D_MODEL))],
        out_specs=[row(D_MODEL), row(ATTN_WIDTH), pl.BlockSpec((N_HEADS * 8, T), lambda i: (0, i)), row(ATTN_WIDTH),
                   row(2 * CONV_WIDTH), row(CONV_WIDTH),
                   acc((D_MODEL, D_MODEL)), acc((D_MODEL, D_MODEL)), acc((PLE_DIM, D_MODEL)), acc((8, CONV_WIDTH)),
                   acc((1, ATTN_WIDTH)), acc((1, CONV_WIDTH)), acc((1, D_MODEL)), acc((1, LANES))],
        compiler_params=_cparams(("arbitrary",)),
    )(o, proj, proj, proj, proj, x, p, target, w_o, w_plg, w_pl, conv_w8, g_oa, g_oc, g_pl)


def _flash_bwd(q, k, v, d_o, lse2, delta, by_target):
    S = q.shape[0]
    T = min(TQ, S)
    nq = S // T

    def body(k_ref, v_ref, q_ref, do_ref, lse_ref, delta_ref, out_ref, dq_ref, dk_ref, dv_ref, in_ref, s_a, dp_a, s_b, dp_b,
             send_sems, recv_sems, local_sem):
        j = pl.program_id(1)
        scatter = _exchange(lambda d: out_ref.at[d], in_ref, send_sems, recv_sems, local_sem)

        @pl.when((pl.program_id(0) == 0) & (j == 0))
        def _():
            _exchange_start(scatter)

        @pl.when(j == 0)
        def _():
            dq_ref[...] = jnp.zeros(dq_ref.shape, f32)

        dk_ref[...] = jnp.zeros(dk_ref.shape, f32)
        dv_ref[...] = jnp.zeros(dv_ref.shape, f32)
        kv, vv = k_ref[...], v_ref[...]

        def rows_of(i):
            return pl.ds(pl.multiple_of(i * T, T), T)

        def scores(i, s_dst, dp_dst):
            rows = rows_of(i)
            s_dst[...] = _nt(kv, q_ref[rows, :])
            dp_dst[...] = _nt(vv, do_ref[rows, :])

        def grads(i, s_src, dp_src, masked):
            rows = rows_of(i)
            s = s_src[...]
            if masked:
                k_i = lax.broadcasted_iota(jnp.int32, (T, T), 0)
                q_i = lax.broadcasted_iota(jnp.int32, (T, T), 1)
                s = jnp.where(k_i <= q_i, s, NEG_INF)
            p = jnp.exp2(s * SCALE_LOG2E - lse_ref[0:1, rows])
            ds = (p * (dp_src[...] - delta_ref[0:1, rows])).astype(MM)
            dv_ref[...] += _nn(p.astype(MM), do_ref[rows, :])
            dk_ref[...] += _nn(ds, q_ref[rows, :])
            dq_ref[rows, :] += _tn(ds, kv)

        last = nq - 1
        scores(j, s_a, dp_a)
        scores(jnp.minimum(j + 1, last), s_b, dp_b)
        grads(j, s_a, dp_a, True)
        rest = last - j

        def pair(t, carry):
            u = j + 1 + 2 * t
            scores(u + 1, s_a, dp_a)
            grads(u, s_b, dp_b, False)
            scores(jnp.minimum(u + 2, last), s_b, dp_b)
            grads(u + 1, s_a, dp_a, False)
            return carry

        lax.fori_loop(0, rest // 2, pair, 0)

        @pl.when(rest % 2 == 1)
        def _():
            grads(last, s_b, dp_b, False)

        @pl.when((pl.program_id(0) == N_HEADS - 1) & (j == nq - 1))
        def _():
            _exchange_wait(scatter)

    whole = lambda w: pl.BlockSpec((S, w), lambda h, j: (0, h))
    stat = pl.BlockSpec((8, S), lambda h, j: (h, 0))
    return pl.pallas_call(
        body, name="flash_bwd", grid=(N_HEADS, nq),
        out_shape=[jax.ShapeDtypeStruct((S, QK_PAD), f32), jax.ShapeDtypeStruct((S, QK_PAD), f32),
                   jax.ShapeDtypeStruct((S, ATTN_WIDTH), f32), jax.ShapeDtypeStruct(by_target.shape, by_target.dtype)],
        in_specs=[pl.BlockSpec((T, HEAD_PAD), lambda h, j: (j, h)), pl.BlockSpec((T, V_DIM), lambda h, j: (j, h)),
                  whole(HEAD_PAD), whole(V_DIM), stat, stat, pl.BlockSpec(memory_space=pl.ANY)],
        out_specs=[whole(HEAD_PAD), pl.BlockSpec((T, HEAD_PAD), lambda h, j: (j, h)),
                   pl.BlockSpec((T, V_DIM), lambda h, j: (j, h)), pl.BlockSpec(memory_space=pl.ANY)],
        scratch_shapes=[pltpu.VMEM((T, T), f32)] * 4 + EXCHANGE_SEMS,
        compiler_params=_cparams(("arbitrary", "arbitrary")),
    )(k, v, q, d_o, lse2, delta, by_target)


def _bwd_proj(x, proj, d_q, d_k, d_v, d_za, d_cbzc, d_u, d_x2, tabs, g_in, w_in_p, g_cq, w_uq_p, g_ckv, w_ukv_p,
              g_q_p, g_k_p, conv_w8):
    S = x.shape[0]
    T = min(TS_E, S)
    nt = S // T

    def body(x_ref, cqkv_ref, ccx_ref, dq_ref, dk_ref, dv_ref, dza_ref, dcbzc_ref, du_ref, dun_ref, dx2_ref,
             c_ref, s1_ref, s2_ref, g_in_ref, w_in_ref, g_cq_ref, w_uq_ref, g_ckv_ref, w_ukv_ref, g_q_ref, g_k_ref, cw_ref,
             gx_ref, dwin_hbm, dwuq_ref, dwukv_ref, dgin_ref, dgcq_ref, dgckv_ref, dgq_ref, dgk_ref,
             dproj_a, dproj_b, dqraw_sc, dkvraw_sc, dwin_sc, sem):
        i = pl.program_id(0)

        @pl.when(i == 0)
        def _():
            dwin_sc[...] = jnp.zeros(dwin_sc.shape, f32)
            dproj_b[...] = jnp.zeros(dproj_b.shape, MM)
            for r in (dwuq_ref, dwukv_ref, dgin_ref, dgcq_ref, dgckv_ref, dgq_ref, dgk_ref):
                r[...] = jnp.zeros(r.shape, f32)

        def stage1(dproj_sc):
            live = (i < nt).astype(f32)
            c, s1, s2 = c_ref[...], s1_ref[...], s2_ref[...]
            g_q, g_k = g_q_ref[...], g_k_ref[...]
            g_cq, g_ckv = g_cq_ref[...], g_ckv_ref[...]

            c_q = cqkv_ref[:, 0:Q_LORA]
            r_cq = _rstd(c_q, Q_LORA)
            cqh = c_q * r_cq
            n_cq = (cqh * g_cq).astype(MM)
            q_raw = _nn(n_cq, w_uq_ref[...])
            dgq = jnp.zeros((1, HEAD_PAD), f32)
            for hd in range(N_HEADS):
                cols = slice(hd * HEAD_PAD, (hd + 1) * HEAD_PAD)
                qh = q_raw[:, cols]
                rq = _rstd(qh, QK_DIM)
                qhh = qh * rq
                dqh = dq_ref[:, cols] * QK_SCALE
                d_qn = jnp.concatenate([dqh[:, :LANES], _rope_t(dqh[:, LANES:], c, s1, s2)], axis=-1)
                dgq += jnp.sum(d_qn * qhh, axis=0, keepdims=True)
                dqraw_sc[:, cols] = _rms_bwd(d_qn, qhh, rq, g_q, QK_DIM).astype(MM)
            dgq_ref[...] += dgq * live
            d_qraw = dqraw_sc[...]
            dwuq_ref[...] += _tn((cqh * (g_cq * live)).astype(MM), d_qraw)
            d_ncq = _nt(d_qraw, w_uq_ref[...])
            dgcq_ref[...] += jnp.sum(d_ncq * cqh, axis=0, keepdims=True) * live
            dproj_sc[:, 0:Q_LORA] = _rms_bwd(d_ncq, cqh, r_cq, g_cq, Q_LORA).astype(MM)

            c_kv = cqkv_ref[:, Q_LORA:Q_LORA + KV_LORA]
            r_ckv = _rstd(c_kv, KV_LORA)
            ckvh = c_kv * r_ckv
            n_ckv = (ckvh * g_ckv).astype(MM)
            k_nope = _nn(n_ckv, w_ukv_ref[:, :N_HEADS * NOPE_DIM])
            kpe = cqkv_ref[:, Q_LORA + KV_LORA:Q_LORA + KV_LORA + LANES]
            ss_pe = jnp.sum(kpe * kpe, axis=-1, keepdims=True)
            g_kn, g_kp = g_k[:, :LANES], g_k[:, LANES:]
            d_kpe = jnp.zeros((T, LANES), f32)
            dgk_n = jnp.zeros((1, LANES), f32)
            dgk_p = jnp.zeros((1, LANES), f32)
            for hd in range(N_HEADS):
                kn = k_nope[:, hd * LANES:(hd + 1) * LANES]
                rk = lax.rsqrt((jnp.sum(kn * kn, axis=-1, keepdims=True) + ss_pe) * (1.0 / QK_DIM) + RMS_EPS)
                knh, kph = kn * rk, kpe * rk
                d_kn_n = dk_ref[:, hd * HEAD_PAD:hd * HEAD_PAD + LANES] * QK_SCALE
                d_kr = _rope_t(dk_ref[:, hd * HEAD_PAD + LANES:(hd + 1) * HEAD_PAD] * QK_SCALE, c, s1, s2)
                dgk_n += jnp.sum(d_kn_n * knh, axis=0, keepdims=True)
                dgk_p += jnp.sum(d_kr * kph, axis=0, keepdims=True)
                u_n, u_p = d_kn_n * g_kn, d_kr * g_kp
                mt = (jnp.sum(u_n * knh, axis=-1, keepdims=True) + jnp.sum(u_p * kph, axis=-1, keepdims=True)) * (1.0 / QK_DIM)
                dkvraw_sc[:, hd * LANES:(hd + 1) * LANES] = (rk * (u_n - knh * mt)).astype(MM)
                d_kpe += rk * (u_p - kph * mt)
            dgk_ref[:, :LANES] += dgk_n * live
            dgk_ref[:, LANES:] += dgk_p * live
            dkvraw_sc[:, N_HEADS * NOPE_DIM:] = dv_ref[...].astype(MM)
            d_kvraw = dkvraw_sc[...]
            dwukv_ref[...] += _tn((ckvh * (g_ckv * live)).astype(MM), d_kvraw)
            d_nckv = _nt(d_kvraw, w_ukv_ref[...])
            dgckv_ref[...] += jnp.sum(d_nckv * ckvh, axis=0, keepdims=True) * live
            dproj_sc[:, Q_LORA:Q_LORA + KV_LORA] = _rms_bwd(d_nckv, ckvh, r_ckv, g_ckv, KV_LORA).astype(MM)
            dproj_sc[:, Q_LORA + KV_LORA:Q_LORA + KV_LORA + LANES] = d_kpe.astype(MM)

            dproj_sc[:, 512:1024] = dza_ref[...]
            d_u = du_ref[...]
            nxt = dun_ref[...] * (i < nt - 1).astype(f32)
            d_m = cw_ref[2:3, :] * d_u + cw_ref[1:2, :] * _shift_up(d_u, nxt, 1) + cw_ref[0:1, :] * _shift_up(d_u, nxt, 2)
            dproj_sc[:, 1024:1536] = (d_m * ccx_ref[:, CONV_WIDTH:]).astype(MM)
            dproj_sc[:, 1536:2048] = (d_m * ccx_ref[:, :CONV_WIDTH]).astype(MM)
            dproj_sc[:, 2048:3072] = dcbzc_ref[...]

        def stage2(dproj_sc):
            g_in = g_in_ref[...]
            xv = x_ref[...]
            r_in = _rstd(xv, D_MODEL)
            xh = xv * r_in
            hb = (xh * g_in).astype(MM)
            d_proj = dproj_sc[...]
            dwin_sc[...] += _tn(hb, d_proj)
            d_h = _nt(d_proj, w_in_ref[...])
            dgin_ref[...] += jnp.sum(d_h * xh, axis=0, keepdims=True)
            gx_ref[...] = dx2_ref[...] + _rms_bwd(d_h, xh, r_in, g_in, D_MODEL)

        @pl.when(i % 2 == 0)
        def _():
            stage2(dproj_b)
            stage1(dproj_a)

        @pl.when(i % 2 == 1)
        def _():
            stage2(dproj_a)
            stage1(dproj_b)

        @pl.when(i == nt)
        def _():
            cp = pltpu.make_async_copy(dwin_sc, dwin_hbm, sem)
            cp.start()
            cp.wait()

    cur = lambda i: jnp.minimum(i, nt - 1)
    prv = lambda i: jnp.maximum(i - 1, 0)
    row = lambda w, blk=0: pl.BlockSpec((T, w), lambda i: (cur(i), blk))
    row2 = lambda w: pl.BlockSpec((T, w), lambda i: (prv(i), 0))
    acc = lambda shape: pl.BlockSpec(shape, lambda i: (0, 0))
    tb = T // 8
    sds = jax.ShapeDtypeStruct
    return pl.pallas_call(
        body, name="bwd_proj", grid=(nt + 1,),
        out_shape=[sds((S, D_MODEL), f32), sds((D_MODEL, PROJ_PAD), f32), sds((Q_LORA, QK_PAD), f32),
                   sds((KV_LORA, 2 * ATTN_WIDTH), f32), sds((1, D_MODEL), f32), sds((1, Q_LORA), f32), sds((1, KV_LORA), f32),
                   sds((1, HEAD_PAD), f32), sds((1, HEAD_PAD), f32)],
        in_specs=[row2(D_MODEL), row(512, 0), row(2 * CONV_WIDTH, 1), row(QK_PAD), row(QK_PAD), row(ATTN_WIDTH),
                  row(ATTN_WIDTH), row(2 * CONV_WIDTH), row(CONV_WIDTH),
                  pl.BlockSpec((8, CONV_WIDTH), lambda i: (jnp.minimum((cur(i) + 1) * tb, S // 8 - 1), 0)),
                  row2(D_MODEL), row(LANES), row(LANES), row(LANES),
                  _const((1, D_MODEL)), _const((D_MODEL, PROJ_PAD)), _const((1, Q_LORA)), _const((Q_LORA, QK_PAD)),
                  _const((1, KV_LORA)), _const((KV_LORA, 2 * ATTN_WIDTH)), _const((1, HEAD_PAD)), _const((1, HEAD_PAD)),
                  _const((8, CONV_WIDTH))],
        out_specs=[row2(D_MODEL), pl.BlockSpec(memory_space=pl.ANY), acc((Q_LORA, QK_PAD)), acc((KV_LORA, 2 * ATTN_WIDTH)),
                   acc((1, D_MODEL)), acc((1, Q_LORA)), acc((1, KV_LORA)), acc((1, HEAD_PAD)), acc((1, HEAD_PAD))],
        scratch_shapes=[pltpu.VMEM((T, PROJ_PAD), MM), pltpu.VMEM((T, PROJ_PAD), MM), pltpu.VMEM((T, QK_PAD), MM),
                        pltpu.VMEM((T, 2 * ATTN_WIDTH), MM), pltpu.VMEM((D_MODEL, PROJ_PAD), f32), pltpu.SemaphoreType.DMA],
        compiler_params=_cparams(("arbitrary",)),
    )(x, proj, proj, d_q, d_k, d_v, d_za, d_cbzc, d_u, d_u, d_x2, *tabs, g_in, w_in_p, g_cq, w_uq_p, g_ckv, w_ukv_p,
      g_q_p, g_k_p, conv_w8)


def _all_gather(block, name, in_vmem):
    m_per, n = block.shape

    def body(x_ref, out_ref, send_sems, recv_sems, local_sem):
        x, y, c = lax.axis_index("x"), lax.axis_index("y"), lax.axis_index("c")
        me, sibling = (x, y, c), (x, y, 1 - c)
        chips = [(1 - x, y), (x, 1 - y), (1 - x, 1 - y)]

        def rows(px, py, pc):
            return out_ref.at[pl.ds((4 * px + 2 * py + pc) * m_per, m_per), :]

        def copy(k, blk, to, src=None):
            return pltpu.make_async_remote_copy(
                src_ref=rows(*blk) if src is None else src, dst_ref=rows(*blk),
                send_sem=send_sems.at[k], recv_sem=recv_sems.at[k], device_id=to, device_id_type=MESH)

        mine = pltpu.make_async_copy(x_ref, rows(*me), local_sem)
        mine.start()
        first = [copy(0, me, sibling, src=x_ref)]
        first += [copy(1 + j, me, (*chip, c), src=x_ref) for j, chip in enumerate(chips)]
        for cp in first:
            cp.start()
        passed = [copy(4 + j, (*chip, c), sibling) for j, chip in enumerate(chips)]
        for j, chip in enumerate(chips):
            copy(1 + j, (*chip, c), me).wait_recv()
            passed[j].start()
        copy(0, sibling, me).wait_recv()
        for j, chip in enumerate(chips):
            copy(4 + j, (*chip, 1 - c), me).wait_recv()
        for cp in first + passed:
            cp.wait_send()
        mine.wait()

    space = pltpu.VMEM if in_vmem else pl.ANY
    return pl.pallas_call(
        body, name=name, out_shape=jax.ShapeDtypeStruct((N_DEV * m_per, n), block.dtype),
        in_specs=[pl.BlockSpec(memory_space=space)], out_specs=pl.BlockSpec(memory_space=space),
        scratch_shapes=[pltpu.SemaphoreType.DMA((7,)), pltpu.SemaphoreType.DMA((7,)), pltpu.SemaphoreType.DMA],
    )(block)


def _sibling_exchange(pack):
    nchip, _, R, n = pack.shape

    def body(p_ref, got_ref, send_sems, recv_sems):
        x, y, c = lax.axis_index("x"), lax.axis_index("y"), lax.axis_index("c")
        copies = [pltpu.make_async_remote_copy(src_ref=p_ref.at[t, 1 - c], dst_ref=got_ref.at[t], send_sem=send_sems.at[t],
                                               recv_sem=recv_sems.at[t], device_id=(x, y, 1 - c), device_id_type=MESH)
                  for t in range(nchip)]
        for cp in copies:
            cp.start()
        for cp in copies:
            cp.wait()

    return pl.pallas_call(
        body, name="rs_sibling", out_shape=jax.ShapeDtypeStruct((nchip, R, n), pack.dtype),
        in_specs=[pl.BlockSpec(memory_space=pl.ANY)], out_specs=pl.BlockSpec(memory_space=pl.ANY),
        scratch_shapes=[pltpu.SemaphoreType.DMA((nchip,)), pltpu.SemaphoreType.DMA((nchip,))],
    )(pack)


def _chip_exchange(part):
    nchip, R, n = part.shape

    def body(p_ref, got_ref, send_sems, recv_sems, local_sem):
        x, y, c = lax.axis_index("x"), lax.axis_index("y"), lax.axis_index("c")
        my_chip = 2 * x + y
        chips = [(1 - x, y), (x, 1 - y), (1 - x, 1 - y)]
        mine = pltpu.make_async_copy(p_ref.at[my_chip], got_ref.at[my_chip], local_sem)
        mine.start()
        sends = []
        for k, (tx, ty) in enumerate(chips):
            cp = pltpu.make_async_remote_copy(src_ref=p_ref.at[2 * tx + ty], dst_ref=got_ref.at[my_chip],
                                              send_sem=send_sems.at[k], recv_sem=recv_sems.at[k],
                                              device_id=(tx, ty, c), device_id_type=MESH)
            cp.start()
            sends.append(cp)
        for k, (sx, sy) in enumerate(chips):
            pltpu.make_async_remote_copy(src_ref=p_ref.at[my_chip], dst_ref=got_ref.at[2 * sx + sy],
                                         send_sem=send_sems.at[k], recv_sem=recv_sems.at[k],
                                         device_id=(sx, sy, c), device_id_type=MESH).wait_recv()
        for cp in sends:
            cp.wait_send()
        mine.wait()

    return pl.pallas_call(
        body, name="rs_chips", out_shape=jax.ShapeDtypeStruct((nchip, R, n), part.dtype),
        in_specs=[pl.BlockSpec(memory_space=pl.ANY)], out_specs=pl.BlockSpec(memory_space=pl.ANY),
        scratch_shapes=[pltpu.SemaphoreType.DMA((3,)), pltpu.SemaphoreType.DMA((3,)), pltpu.SemaphoreType.DMA],
    )(part)


def _add_blocks(pack, from_sib, my_c):
    nb, _, R, n = pack.shape

    def body(c_ref, a_ref, b_ref, o_ref):
        o_ref[...] = (a_ref[0] + b_ref[...]).astype(o_ref.dtype)

    return pl.pallas_call(
        body, name="rs_add", out_shape=jax.ShapeDtypeStruct((nb, R, n), RS_DTYPE),
        grid_spec=pltpu.PrefetchScalarGridSpec(
            num_scalar_prefetch=1, grid=(nb,),
            in_specs=[pl.BlockSpec((1, 1, R, n), lambda i, c: (i, c[0], 0, 0)), pl.BlockSpec((1, R, n), lambda i, c: (i, 0, 0))],
            out_specs=pl.BlockSpec((1, R, n), lambda i, c: (i, 0, 0))),
        compiler_params=_cparams(("parallel",)))(my_c, pack, from_sib)


def _sum_slabs(got, name):
    nb, R, n = got.shape

    def body(g_ref, o_ref):
        tot = g_ref[0].astype(f32)
        for d in range(1, nb):
            tot = tot + g_ref[d].astype(f32)
        o_ref[...] = tot

    return pl.pallas_call(body, name=name, grid=(n // LANES,), out_shape=jax.ShapeDtypeStruct((R, n), f32),
                          in_specs=[pl.BlockSpec((nb, R, LANES), lambda j: (0, 0, j))], out_specs=pl.BlockSpec((R, LANES), lambda j: (0, j)),
                          compiler_params=_cparams(("parallel",)))(got)


def _adamw_all(ws, gs, ms, vs, gain_parts):
    n = len(ws)
    ng = len(GAIN_SLOTS)

    def body(*refs):
        gp_ref = refs[0]
        w_refs = refs[1:1 + n]
        g_refs = refs[1 + n:1 + 2 * n - ng]
        m_refs = refs[1 + 2 * n - ng:1 + 3 * n - ng]
        v_refs = refs[1 + 3 * n - ng:1 + 4 * n - ng]
        outs = refs[1 + 4 * n - ng:]
        gsum_ref, loss_ref = outs[0], outs[1]
        gg_refs = outs[2:2 + ng]
        d_refs, nm_refs, nv_refs = (outs[2 + ng + k * n:2 + ng + (k + 1) * n] for k in range(3))
        tot = gp_ref[0]
        for d in range(1, N_DEV):
            tot = tot + gp_ref[d]
        gsum_ref[...] = tot
        loss_ref[...] = gsum_ref[GAIN_ROWS - 1:GAIN_ROWS, :]
        for k in range(n):
            if k < ng:
                r0, width = GAIN_SLOTS[k]
                for r in range(width // LANES):
                    gg_refs[k][:, r * LANES:(r + 1) * LANES] = gsum_ref[r0 + r:r0 + r + 1, :]
                g = gg_refs[k][...]
            else:
                g = g_refs[k - ng][...]
            w = w_refs[k][...]
            m = ADAM_B1 * m_refs[k][...] + (1.0 - ADAM_B1) * g
            v = ADAM_B2 * v_refs[k][...] + (1.0 - ADAM_B2) * (g * g)
            m_hat = m / (1.0 - ADAM_B1 ** ADAM_STEP)
            v_hat = v / (1.0 - ADAM_B2 ** ADAM_STEP)
            d_refs[k][...] = -ADAM_LR * (m_hat / (jnp.sqrt(v_hat) + ADAM_EPS) + ADAM_WD * w)
            nm_refs[k][...] = m
            nv_refs[k][...] = v

    sds = jax.ShapeDtypeStruct
    like = [sds(w.shape, f32) for w in ws]
    out_shape = [sds((GAIN_ROWS, LANES), f32), sds((1, LANES), f32)] + like[:ng] + like * 3
    full = lambda a: pl.BlockSpec(a.shape, lambda i: (0,) * len(a.shape))
    operands = [gain_parts, *ws, *gs, *ms, *vs]
    outs = pl.pallas_call(
        body, name="adamw", grid=(1,), out_shape=out_shape, in_specs=[full(a) for a in operands],
        out_specs=[full(a) for a in out_shape], compiler_params=_cparams(("arbitrary",)),
    )(*operands)
    loss = outs[1]
    gg = outs[2:2 + ng]
    deltas, new_m, new_v = (outs[2 + ng + k * n:2 + ng + (k + 1) * n] for k in range(3))
    return loss, gg, deltas, new_m, new_v


GAIN_ROWS = 32
GAIN_SLOTS = [(0, 1024), (8, 256), (10, 128), (11, 256), (13, 256), (15, 512), (19, 512), (23, 1024)]

SH_IN = IN_TOTAL // N_DEV
SLAB1_COLS = 384
UQ_ROWS = Q_LORA * (N_HEADS * QK_DIM // N_DEV) // SLAB1_COLS
UKV_ROWS = 48
SLAB1_ROWS = D_MODEL + UQ_ROWS + UKV_ROWS
SH_ROWS = D_MODEL // N_DEV
PL_ROWS = PLE_DIM * (D_MODEL // N_DEV) // D_MODEL
CONV_ROWS = 16
SLAB2_ROWS = 2 * SH_ROWS + PL_ROWS + CONV_ROWS
IN_SEGMENTS = [(0, 448, 0), (448, 960, 512), (1472, 2496, 1024), (960, 1472, 2048), (2496, 3008, 2560)]


def _flat_rows(a, rows, cols):
    lead = a.shape[:-2]
    flat = a.reshape(lead + (-1,))
    pad = [(0, 0)] * len(lead) + [(0, rows * cols - flat.shape[-1])]
    return jnp.pad(flat, pad).reshape(lead + (rows, cols))


def _slab1(in_sh, uq_sh, ukv_sh):
    pad = [(0, 0)] * (in_sh.ndim - 1) + [(0, SLAB1_COLS - SH_IN)]
    return jnp.concatenate([jnp.pad(in_sh, pad), _flat_rows(uq_sh, UQ_ROWS, SLAB1_COLS), _flat_rows(ukv_sh, UKV_ROWS, SLAB1_COLS)],
                           axis=-2)


def _slab2(o_sh, plg_sh, pl_sh, conv_sh):
    return jnp.concatenate([o_sh, plg_sh, _flat_rows(pl_sh, PL_ROWS, D_MODEL), _flat_rows(conv_sh, CONV_ROWS, D_MODEL)], axis=-2)


def _by_device(a, width):
    return a.reshape(a.shape[0], N_DEV, width).transpose(1, 0, 2)


def _from_devices(a):
    return a.transpose(1, 0, 2).reshape(a.shape[1], -1)


def kernel(x, p, positions, g_in, w_in, g_cq, w_uq, g_ckv, w_ukv, g_q, g_k, conv_w, g_oa, g_oc, w_o, w_pl, w_plg, g_pl, loss_target, m_g_in, m_w_in, m_g_cq, m_w_uq, m_g_ckv, m_w_ukv, m_g_q, m_g_k, m_conv_w, m_g_oa, m_g_oc, m_w_o, m_w_pl, m_w_plg, m_g_pl, v_g_in, v_w_in, v_g_cq, v_w_uq, v_g_ckv, v_w_ukv, v_g_q, v_g_k, v_conv_w, v_g_oa, v_g_oc, v_w_o, v_w_pl, v_w_plg, v_g_pl):
    S = x.shape[1]
    nd = N_DEV
    xs, ps, tgt = x[0], p[0, 0], loss_target[0]

    slab1 = _slab1(w_in[0], w_uq[0], w_ukv[0]).astype(bf16)
    g1 = _all_gather(slab1, "ag_weights", in_vmem=False).reshape(nd, SLAB1_ROWS, SLAB1_COLS)
    conv_bits = lax.bitcast_convert_type(conv_w[0], bf16).reshape(CONV_K, -1)
    slab2 = _slab2(w_o[0].astype(bf16), w_plg[0].astype(bf16), w_pl[0].astype(bf16), conv_bits)

    pieces, pos = [], 0
    for a, b, start in sorted(IN_SEGMENTS, key=lambda s: s[2]):
        if start > pos:
            pieces.append(jnp.zeros((D_MODEL, start - pos), bf16))
        pieces += [g1[d, :D_MODEL, max(a - SH_IN * d, 0):min(b - SH_IN * d, SH_IN)] for d in range(a // SH_IN, (b - 1) // SH_IN + 1)]
        pos = start + b - a
    w_in_p = jnp.concatenate(pieces, axis=1)
    wuq = _from_devices(g1[:, D_MODEL:D_MODEL + UQ_ROWS].reshape(nd, Q_LORA, -1)).reshape(Q_LORA, N_HEADS, QK_DIM)
    w_uq_p = jnp.pad(wuq, ((0, 0), (0, 0), (0, HEAD_PAD - QK_DIM))).reshape(Q_LORA, QK_PAD)
    ukv_flat = g1[:, D_MODEL + UQ_ROWS:].reshape(nd, -1)[:, :KV_LORA * V_DIM]
    wukv = _from_devices(ukv_flat.reshape(nd, KV_LORA, V_DIM)).reshape(KV_LORA, N_HEADS, 2, NOPE_DIM)
    w_ukv_p = wukv.transpose(0, 2, 1, 3).reshape(KV_LORA, 2 * ATTN_WIDTH)
    w_in_p, w_uq_p, w_ukv_p = (w.astype(MM) for w in (w_in_p, w_uq_p, w_ukv_p))

    g_q_p = jnp.pad(g_q, ((0, 0), (0, HEAD_PAD - QK_DIM)))
    g_k_p = jnp.pad(g_k, ((0, 0), (0, HEAD_PAD - QK_DIM)))

    inv_freq = 1.0 / (ROPE_THETA ** (jnp.arange(0, ROPE_DIM, 2, dtype=f32) / ROPE_DIM))
    inv_row = jnp.concatenate([inv_freq, inv_freq, jnp.zeros((LANES - ROPE_DIM,), f32)]).reshape(1, LANES)
    posf = jnp.broadcast_to(positions[0].astype(f32)[:, None], (S, LANES))
    tabs = _rope_tables(posf, inv_row)

    proj, q, k, v = _fwd_proj(xs, tabs, g_in, w_in_p, g_cq, w_uq_p, g_ckv, w_ukv_p, g_q_p, g_k_p)
    o, lse, g2 = _flash_fwd(q, k, v, slab2)
    w_o_f = g2[:, 0:SH_ROWS].reshape(D_MODEL, D_MODEL).astype(MM)
    w_plg_f = g2[:, SH_ROWS:2 * SH_ROWS].reshape(D_MODEL, D_MODEL).astype(MM)
    w_pl_f = _from_devices(g2[:, 2 * SH_ROWS:2 * SH_ROWS + PL_ROWS].reshape(nd, PLE_DIM, -1)).astype(MM)
    cv_bits = g2[:, 2 * SH_ROWS + PL_ROWS:].reshape(nd, -1)[:, :CONV_K * (CONV_WIDTH // nd) * 2]
    conv_full = _from_devices(lax.bitcast_convert_type(cv_bits.reshape(nd, CONV_K, CONV_WIDTH // nd, 2), f32))
    conv_w8 = jnp.pad(conv_full, ((0, 8 - CONV_K), (0, 0)))
    (d_x2, d_o, delta, d_za, d_cbzc, d_u, dw_o, dw_plg, dw_pl, dcw, dg_oa, dg_oc, dg_pl, loss_part) = _mid(
        o, proj, xs, ps, tgt, w_o_f, w_plg_f, w_pl_f, conv_w8, g_oa, g_oc, g_pl)

    slabs_a = _slab2(dw_o.reshape(nd, SH_ROWS, D_MODEL), dw_plg.reshape(nd, SH_ROWS, D_MODEL), _by_device(dw_pl, D_MODEL // nd),
                     _by_device(dcw[:CONV_K], CONV_WIDTH // nd)).astype(RS_DTYPE)
    d_q, d_k, d_v, got_a = _flash_bwd(q, k, v, d_o, lse, delta, slabs_a)
    shard_a = _sum_slabs(got_a, "rs_sum8")
    (grad_x, dw_in_p, dw_uq_p, dw_ukv_p, dg_in, dg_cq, dg_ckv, dg_q_p, dg_k_p) = _bwd_proj(
        xs, proj, d_q, d_k, d_v, d_za, d_cbzc, d_u, d_x2, tabs, g_in, w_in_p, g_cq, w_uq_p, g_ckv, w_ukv_p, g_q_p, g_k_p, conv_w8)

    def dw_in_cols(a, b):
        return [dw_in_p[:, start + max(a, s) - s:start + min(b, e) - s] for s, e, start in sorted(IN_SEGMENTS) if max(a, s) < min(b, e)]

    in_shards = jnp.stack([jnp.concatenate(dw_in_cols(SH_IN * d, SH_IN * (d + 1)), axis=1) for d in range(nd)])
    dw_uq = dw_uq_p.reshape(Q_LORA, N_HEADS, HEAD_PAD)[:, :, :QK_DIM].reshape(Q_LORA, N_HEADS * QK_DIM)
    dw_ukv = dw_ukv_p.reshape(KV_LORA, 2, N_HEADS, NOPE_DIM).transpose(0, 2, 1, 3).reshape(KV_LORA, 2 * ATTN_WIDTH)
    slabs_b = _slab1(in_shards, _by_device(dw_uq, N_HEADS * QK_DIM // nd), _by_device(dw_ukv, 2 * ATTN_WIDTH // nd))
    slabs_b = slabs_b.reshape(4, 2, SLAB1_ROWS, SLAB1_COLS)
    my_c = lax.axis_index("c").astype(jnp.int32).reshape(1)
    chip_part = _add_blocks(slabs_b, _sibling_exchange(slabs_b), my_c)
    shard_b = _sum_slabs(_chip_exchange(chip_part), "rs_sum")
    grads_w = [shard_b[:D_MODEL, :SH_IN], shard_b[D_MODEL:D_MODEL + UQ_ROWS].reshape(Q_LORA, -1),
               shard_b[D_MODEL + UQ_ROWS:].reshape(-1)[:KV_LORA * V_DIM].reshape(KV_LORA, V_DIM),
               shard_a[2 * SH_ROWS + PL_ROWS:].reshape(-1)[:CONV_K * CONV_WIDTH // nd].reshape(CONV_K, -1),
               shard_a[0:SH_ROWS], shard_a[2 * SH_ROWS:2 * SH_ROWS + PL_ROWS].reshape(PLE_DIM, -1), shard_a[SH_ROWS:2 * SH_ROWS]]

    gains_part = [dg_in, dg_cq, dg_ckv, dg_q_p, dg_k_p, dg_oa, dg_oc, dg_pl]
    gflat = jnp.concatenate([g.reshape(-1) for g in gains_part] + [jnp.zeros(((GAIN_ROWS - 1) * LANES - 3968,), f32),
                                                                    loss_part.reshape(-1)])
    gain_parts = _all_gather(gflat.reshape(GAIN_ROWS, LANES), "ag_gains", in_vmem=True).reshape(nd, GAIN_ROWS, LANES)

    gains = [g_in, g_cq, g_ckv, g_q_p, g_k_p, g_oa, g_oc, g_pl]
    padq = lambda a: jnp.pad(a, ((0, 0), (0, HEAD_PAD - QK_DIM)))
    m_gains = [m_g_in, m_g_cq, m_g_ckv, padq(m_g_q), padq(m_g_k), m_g_oa, m_g_oc, m_g_pl]
    v_gains = [v_g_in, v_g_cq, v_g_ckv, padq(v_g_q), padq(v_g_k), v_g_oa, v_g_oc, v_g_pl]
    ws = gains + [w_in[0], w_uq[0], w_ukv[0], conv_w[0], w_o[0], w_pl[0], w_plg[0]]
    ms = m_gains + [m_w_in[0], m_w_uq[0], m_w_ukv[0], m_conv_w[0], m_w_o[0], m_w_pl[0], m_w_plg[0]]
    vs = v_gains + [v_w_in[0], v_w_uq[0], v_w_ukv[0], v_conv_w[0], v_w_o[0], v_w_pl[0], v_w_plg[0]]
    loss_row, gg, deltas, new_m, new_v = _adamw_all(ws, grads_w, ms, vs, gain_parts)
    loss = loss_row[0, 0]

    def ordered(gl, wl):
        g_in_, g_cq_, g_ckv_, g_q_, g_k_, g_oa_, g_oc_, g_pl_ = gl
        g_q_, g_k_ = g_q_[:, :QK_DIM], g_k_[:, :QK_DIM]
        w_in_, w_uq_, w_ukv_, cw_, w_o_, w_pl_, w_plg_ = [w[None] for w in wl]
        return [g_in_, w_in_, g_cq_, w_uq_, g_ckv_, w_ukv_, g_q_, g_k_, cw_, g_oa_, g_oc_, w_o_, w_pl_, w_plg_, g_pl_]

    ng = len(gains)
    outs = [loss, grad_x[None]]
    outs += ordered(gg, grads_w)
    for lst in (deltas, new_m, new_v):
        outs += ordered(lst[:ng], lst[ng:])
    return tuple(outs)
```

```python
import functools
import math

import jax
import jax.numpy as jnp
from jax import lax
from jax.experimental import pallas as pl
from jax.experimental.pallas import tpu as pltpu

f32 = jnp.float32
bf16 = jnp.bfloat16
MM = jnp.bfloat16
RS_DTYPE = jnp.bfloat16

D_MODEL = 1024
PLE_DIM = 256
N_HEADS = 4
NOPE_DIM = 128
ROPE_DIM = 64
V_DIM = 128
QK_DIM = NOPE_DIM + ROPE_DIM
Q_LORA = 256
KV_LORA = 128
ATTN_WIDTH = N_HEADS * V_DIM
CONV_WIDTH = D_MODEL - ATTN_WIDTH
CONV_K = 3
ROPE_THETA = 10000.0
RMS_EPS = 1e-6
NEG_INF = -1e30
IN_TOTAL = Q_LORA + KV_LORA + ROPE_DIM + ATTN_WIDTH + 4 * CONV_WIDTH
ADAM_LR = 0.001
ADAM_B1 = 0.9
ADAM_B2 = 0.999
ADAM_EPS = 1e-08
ADAM_WD = 0.01
ADAM_STEP = 10

N_DEV = 8
LANES = 128
HEAD_PAD = 256
QK_PAD = N_HEADS * HEAD_PAD
PROJ_PAD = 3072
TS = 512
TS_M = 256
TS_E = 256
TQ = 512
VMEM_LIMIT = 58 * 1024 * 1024
MESH = pl.DeviceIdType.MESH
QK_SCALE = 1.0 / math.sqrt(QK_DIM)
SCALE_LOG2E = QK_SCALE * math.log2(math.e)


def _cparams(sem=None):
    return pltpu.CompilerParams(dimension_semantics=sem, vmem_limit_bytes=VMEM_LIMIT)


def _nt(a, b):
    return lax.dot_general(a, b, (((1,), (1,)), ((), ())), preferred_element_type=f32)


def _tn(a, b):
    return lax.dot_general(a, b, (((0,), (0,)), ((), ())), preferred_element_type=f32)


def _nn(a, b):
    return jnp.dot(a, b, preferred_element_type=f32)


def _rstd(x, n):
    return lax.rsqrt(jnp.sum(x * x, axis=-1, keepdims=True) * (1.0 / n) + RMS_EPS)


def _rms_bwd(dy, xhat, r, g, n):
    u = dy * g
    return r * (u - xhat * (jnp.sum(u * xhat, axis=-1, keepdims=True) * (1.0 / n)))


def _rope(t, c, s1, s2):
    return t * c + pltpu.roll(t, 32, 1) * s1 + pltpu.roll(t, 96, 1) * s2


def _rope_t(d, c, s1, s2):
    return d * c - pltpu.roll(d, 96, 1) * s2 - pltpu.roll(d, 32, 1) * s1


def _const(shape):
    return pl.BlockSpec(shape, lambda *_: (0,) * len(shape), pipeline_mode=pl.Buffered(1))


def _rope_tables(posf, inv_freq):
    S = posf.shape[0]

    def body(pos_ref, f_ref, c_ref, s1_ref, s2_ref):
        ang = pos_ref[...] * f_ref[...]
        lane = lax.broadcasted_iota(jnp.int32, ang.shape, 1)
        cs, sn = jnp.cos(ang), jnp.sin(ang)
        c_ref[...] = jnp.where(lane < ROPE_DIM, cs, 0.0)
        s1_ref[...] = jnp.where((lane >= ROPE_DIM // 2) & (lane < ROPE_DIM), sn, 0.0)
        s2_ref[...] = jnp.where(lane < ROPE_DIM // 2, -sn, 0.0)

    t = min(S, TS)
    spec = pl.BlockSpec((t, LANES), lambda i: (i, 0))
    return pl.pallas_call(
        body, name="rope_tables", grid=(S // t,),
        out_shape=[jax.ShapeDtypeStruct((S, LANES), f32)] * 3,
        in_specs=[spec, _const((1, LANES))], out_specs=[spec] * 3,
        compiler_params=_cparams(("parallel",)),
    )(posf, inv_freq)


def _fwd_proj(x, tabs, g_in, w_in_p, g_cq, w_uq_p, g_ckv, w_ukv_p, g_q_p, g_k_p):
    S = x.shape[0]
    T = min(TS, S)
    nt = S // T
    LAT = Q_LORA + KV_LORA + LANES

    def body(x_ref, c_ref, s1_ref, s2_ref, g_in_ref, w_in_ref, g_cq_ref, w_uq_ref, g_ckv_ref, w_ukv_ref,
             g_q_ref, g_k_ref, proj_ref, q_ref, k_ref, v_ref, lat_a, lat_b):
        i = pl.program_id(0)

        @pl.when(i == 0)
        def _():
            lat_b[...] = jnp.zeros(lat_b.shape, f32)

        def stage1(lat):
            xv = x_ref[...]
            h = (xv * _rstd(xv, D_MODEL) * g_in_ref[...]).astype(MM)
            proj_ref[...] = _nn(h, w_in_ref[...])
            lat[...] = proj_ref[:, 0:LAT]

        def stage2(lat):
            c, s1, s2 = c_ref[...], s1_ref[...], s2_ref[...]
            g_q, g_k = g_q_ref[...], g_k_ref[...]
            c_q = lat[:, 0:Q_LORA]
            n_cq = (c_q * _rstd(c_q, Q_LORA) * g_cq_ref[...]).astype(MM)
            q_raw = _nn(n_cq, w_uq_ref[...])
            for hd in range(N_HEADS):
                qh = q_raw[:, hd * HEAD_PAD:(hd + 1) * HEAD_PAD]
                qn = qh * _rstd(qh, QK_DIM) * g_q
                q_ref[:, hd * HEAD_PAD:hd * HEAD_PAD + LANES] = qn[:, :LANES].astype(MM)
                q_ref[:, hd * HEAD_PAD + LANES:(hd + 1) * HEAD_PAD] = _rope(qn[:, LANES:], c, s1, s2).astype(MM)

            c_kv = lat[:, Q_LORA:Q_LORA + KV_LORA]
            n_ckv = (c_kv * _rstd(c_kv, KV_LORA) * g_ckv_ref[...]).astype(MM)
            kv_raw = _nn(n_ckv, w_ukv_ref[...])
            kpe = lat[:, Q_LORA + KV_LORA:LAT]
            ss_pe = jnp.sum(kpe * kpe, axis=-1, keepdims=True)
            kr = _rope(kpe * g_k[:, LANES:], c, s1, s2)
            for hd in range(N_HEADS):
                kn = kv_raw[:, hd * LANES:(hd + 1) * LANES]
                rk = lax.rsqrt((jnp.sum(kn * kn, axis=-1, keepdims=True) + ss_pe) * (1.0 / QK_DIM) + RMS_EPS)
                k_ref[:, hd * HEAD_PAD:hd * HEAD_PAD + LANES] = (kn * rk * g_k[:, :LANES]).astype(MM)
                k_ref[:, hd * HEAD_PAD + LANES:(hd + 1) * HEAD_PAD] = (kr * rk).astype(MM)
            v_ref[...] = kv_raw[:, N_HEADS * NOPE_DIM:].astype(MM)

        @pl.when(i % 2 == 0)
        def _():
            stage2(lat_b)
            stage1(lat_a)

        @pl.when(i % 2 == 1)
        def _():
            stage2(lat_a)
            stage1(lat_b)

    cur = lambda w: pl.BlockSpec((T, w), lambda i: (jnp.minimum(i, nt - 1), 0))
    prv = lambda w: pl.BlockSpec((T, w), lambda i: (jnp.maximum(i - 1, 0), 0))
    return pl.pallas_call(
        body, name="fwd_proj", grid=(nt + 1,),
        out_shape=[jax.ShapeDtypeStruct((S, PROJ_PAD), f32), jax.ShapeDtypeStruct((S, QK_PAD), MM),
                   jax.ShapeDtypeStruct((S, QK_PAD), MM), jax.ShapeDtypeStruct((S, ATTN_WIDTH), MM)],
        in_specs=[cur(D_MODEL), prv(LANES), prv(LANES), prv(LANES), _const((1, D_MODEL)), _const((D_MODEL, PROJ_PAD)),
                  _const((1, Q_LORA)), _const((Q_LORA, QK_PAD)), _const((1, KV_LORA)), _const((KV_LORA, 2 * ATTN_WIDTH)),
                  _const((1, HEAD_PAD)), _const((1, HEAD_PAD))],
        out_specs=[cur(PROJ_PAD), prv(QK_PAD), prv(QK_PAD), prv(ATTN_WIDTH)],
        scratch_shapes=[pltpu.VMEM((T, LAT), f32), pltpu.VMEM((T, LAT), f32)],
        compiler_params=_cparams(("arbitrary",)),
    )(x, *tabs, g_in, w_in_p, g_cq, w_uq_p, g_ckv, w_ukv_p, g_q_p, g_k_p)


def _exchange(src_of, dst_ref, send_sems, recv_sems, local_sem):
    x, y, c = lax.axis_index("x"), lax.axis_index("y"), lax.axis_index("c")
    me = 4 * x + 2 * y + c
    remote = []
    for k in range(1, N_DEV):
        px = 1 - x if (k >> 2) & 1 else x
        py = 1 - y if (k >> 1) & 1 else y
        pc = 1 - c if k & 1 else c
        remote.append(pltpu.make_async_remote_copy(
            src_ref=src_of(4 * px + 2 * py + pc), dst_ref=dst_ref.at[me], send_sem=send_sems.at[k - 1],
            recv_sem=recv_sems.at[k - 1], device_id=(px, py, pc), device_id_type=MESH))
    return remote, pltpu.make_async_copy(src_of(me), dst_ref.at[me], local_sem)


def _exchange_start(copies):
    remote, local = copies
    local.start()
    for cp in remote:
        cp.start()


def _exchange_wait(copies):
    remote, local = copies
    for cp in remote:
        cp.wait_recv()
    for cp in remote:
        cp.wait_send()
    local.wait()


EXCHANGE_SEMS = [pltpu.SemaphoreType.DMA((N_DEV - 1,)), pltpu.SemaphoreType.DMA((N_DEV - 1,)), pltpu.SemaphoreType.DMA]


def _flash_fwd(q, k, v, block):
    S = q.shape[0]
    T = min(TQ, S)
    nq = S // T

    def body(q_ref, k_ref, v_ref, blk_ref, o_ref, lse_ref, all_ref, s_a, s_b, m_sc, l_sc, acc_sc, send_sems, recv_sems, local_sem):
        n = pl.program_id(1)
        gather = _exchange(lambda d: blk_ref, all_ref, send_sems, recv_sems, local_sem)

        @pl.when((pl.program_id(0) == 0) & (n == 0))
        def _():
            _exchange_start(gather)

        m_sc[...] = jnp.full(m_sc.shape, NEG_INF, f32)
        l_sc[...] = jnp.zeros(l_sc.shape, f32)
        acc_sc[...] = jnp.zeros(acc_sc.shape, f32)
        qv = q_ref[...]

        def rows(j):
            return pl.ds(pl.multiple_of(j * T, T), T)

        def scores(j, dst):
            dst[...] = _nt(k_ref[rows(j), :], qv)

        def update(src, j, masked):
            s = src[...]
            if masked:
                k_i = lax.broadcasted_iota(jnp.int32, (T, T), 0)
                q_i = lax.broadcasted_iota(jnp.int32, (T, T), 1)
                s = jnp.where(k_i <= q_i, s, NEG_INF)
            m_prev = m_sc[...]
            m_new = jnp.maximum(m_prev, jnp.max(s, axis=0, keepdims=True))
            alpha = jnp.exp2((m_prev - m_new) * SCALE_LOG2E)
            p = jnp.exp2((s - m_new[0:1, :]) * SCALE_LOG2E)
            l_sc[...] = alpha * l_sc[...] + jnp.sum(p, axis=0, keepdims=True)
            acc_sc[...] = alpha[0:1, :] * acc_sc[...] + _tn(v_ref[rows(j), :], p.astype(MM))
            m_sc[...] = m_new

        scores(0, s_a)

        def pair(t, carry):
            j = 2 * t
            scores(j + 1, s_b)
            update(s_a, j, False)
            scores(j + 2, s_a)
            update(s_b, j + 1, False)
            return carry

        lax.fori_loop(0, n // 2, pair, 0)

        @pl.when(n % 2 == 0)
        def _():
            update(s_a, n, True)

        @pl.when(n % 2 == 1)
        def _():
            scores(n, s_b)
            update(s_a, n - 1, False)
            update(s_b, n, True)

        o_ref[...] = (acc_sc[...] / l_sc[0:1, :]).T
        lse_ref[...] = m_sc[...] * SCALE_LOG2E + jnp.log2(l_sc[...])

        @pl.when((pl.program_id(0) == N_HEADS - 1) & (n == nq - 1))
        def _():
            _exchange_wait(gather)

    return pl.pallas_call(
        body, name="flash_fwd", grid=(N_HEADS, nq),
        out_shape=[jax.ShapeDtypeStruct((S, ATTN_WIDTH), f32), jax.ShapeDtypeStruct((N_HEADS * 8, S), f32),
                   jax.ShapeDtypeStruct((N_DEV,) + block.shape, block.dtype)],
        in_specs=[pl.BlockSpec((T, HEAD_PAD), lambda h, i: (i, h)),
                  pl.BlockSpec((S, HEAD_PAD), lambda h, i: (0, h)),
                  pl.BlockSpec((S, V_DIM), lambda h, i: (0, h)),
                  pl.BlockSpec(memory_space=pl.ANY)],
        out_specs=[pl.BlockSpec((T, V_DIM), lambda h, i: (i, h)), pl.BlockSpec((8, T), lambda h, i: (h, i)),
                   pl.BlockSpec(memory_space=pl.ANY)],
        scratch_shapes=[pltpu.VMEM((T, T), f32), pltpu.VMEM((T, T), f32), pltpu.VMEM((8, T), f32), pltpu.VMEM((8, T), f32),
                        pltpu.VMEM((V_DIM, T), f32)] + EXCHANGE_SEMS,
        compiler_params=_cparams(("arbitrary", "arbitrary")),
    )(q, k, v, block)


def _shift_down(m, prev8, n):
    T = m.shape[0]
    rows = lax.broadcasted_iota(jnp.int32, m.shape, 0)
    head = jnp.tile(pltpu.roll(prev8, n, 0), (T // 8, 1))
    return jnp.where(rows >= n, pltpu.roll(m, n, 0), head)


def _shift_up(m, next8, n):
    T = m.shape[0]
    rows = lax.broadcasted_iota(jnp.int32, m.shape, 0)
    tail = jnp.tile(pltpu.roll(next8, 8 - n, 0), (T // 8, 1))
    return jnp.where(rows < T - n, pltpu.roll(m, T - n, 0), tail)


def _mid(o, proj, x, p, target, w_o, w_plg, w_pl, conv_w8, g_oa, g_oc, g_pl):
    S = x.shape[0]
    T = min(TS_M, S)
    nt = S // T

    def body(o_ref, za_ref, ccx_ref, cbzc_ref, prev_ref, o2_ref, za2_ref, ccx2_ref, cbzc2_ref, prev2_ref,
             x_ref, p_ref, t_ref, wo_ref, wplg_ref, wpl_ref, cw_ref, goa_ref, goc_ref, gpl_ref,
             dx2_ref, do_ref, delta_ref, dza_ref, dcbzc_ref, du_ref,
             dwo_ref, dwplg_ref, dwpl_ref, dcw_ref, dgoa_ref, dgoc_ref, dgpl_ref, loss_ref,
             ycat_a, dx2b_a, ycat_b, dx2b_b):
        i = pl.program_id(0)

        @pl.when(i == 0)
        def _():
            ycat_b[...] = jnp.zeros(ycat_b.shape, MM)
            dx2b_b[...] = jnp.zeros(dx2b_b.shape, MM)
            for r in (dwo_ref, dwplg_ref, dwpl_ref, dcw_ref, dgoa_ref, dgoc_ref, dgpl_ref, loss_ref):
                r[...] = jnp.zeros(r.shape, f32)

        def branches(o_r, za_r, ccx_r, cbzc_r, prev_r, first):
            o_v, z_a = o_r[...], za_r[...]
            cc, cx = ccx_r[:, :CONV_WIDTH], ccx_r[:, CONV_WIDTH:]
            cb, z_c = cbzc_r[:, :CONV_WIDTH], cbzc_r[:, CONV_WIDTH:]
            sa = jax.nn.sigmoid(z_a)
            silu_a = z_a * sa
            ga = o_v * silu_a
            ra = _rstd(ga, ATTN_WIDTH)
            gha = ga * ra
            m0 = cc * cx
            prev = prev_r[:, :CONV_WIDTH] * prev_r[:, CONV_WIDTH:] * (1.0 - first.astype(f32))
            m1 = _shift_down(m0, prev, 1)
            m2 = _shift_down(m0, prev, 2)
            u = cw_ref[0:1, :] * m2 + cw_ref[1:2, :] * m1 + cw_ref[2:3, :] * m0
            sc = jax.nn.sigmoid(z_c)
            silu_c = z_c * sc
            gc = cb * u * silu_c
            rc = _rstd(gc, CONV_WIDTH)
            ghc = gc * rc
            return o_v, z_a, cb, z_c, sa, silu_a, ra, gha, m0, m1, m2, u, sc, silu_c, rc, ghc

        def stage1(ycat_sc, dx2b_sc):
            live = (i < nt).astype(f32)
            g_pl = gpl_ref[...]
            (_, _, _, _, _, _, _, gha, _, _, _, _, _, _, _, ghc) = branches(o_ref, za_ref, ccx_ref, cbzc_ref, prev_ref, i == 0)
            ycat = jnp.concatenate([gha * goa_ref[...], ghc * goc_ref[...]], axis=-1).astype(MM)
            ycat_sc[...] = ycat
            x2 = x_ref[...] + _nn(ycat, wo_ref[...])
            r2 = _rstd(x2, D_MODEL)
            xh2 = x2 * r2
            n2 = (xh2 * g_pl).astype(MM)
            gate = jax.nn.sigmoid(_nn(n2, wplg_ref[...]))
            p_b = p_ref[...].astype(MM)
            pw = _nn(p_b, wpl_ref[...])
            err = x2 + gate * pw - t_ref[...]
            loss_ref[...] += jnp.sum(jnp.sum(err * err, axis=-1, keepdims=True), axis=0, keepdims=True) * (0.5 / D_MODEL * live)
            d_out = err * (1.0 / D_MODEL)
            dwpl_ref[...] += _tn((p_ref[...] * live).astype(MM), (d_out * gate).astype(MM))
            d_glog = (d_out * pw * gate * (1.0 - gate)).astype(MM)
            dwplg_ref[...] += _tn((xh2 * (g_pl * live)).astype(MM), d_glog)
            d_n2 = _nt(d_glog, wplg_ref[...])
            dgpl_ref[...] += jnp.sum(d_n2 * xh2, axis=0, keepdims=True) * live
            d_x2 = d_out + _rms_bwd(d_n2, xh2, r2, g_pl, D_MODEL)
            dx2_ref[...] = d_x2
            dx2b_sc[...] = d_x2.astype(MM)

        def stage2(ycat_sc, dx2b_sc):
            g_oa, g_oc = goa_ref[...], goc_ref[...]
            (o_v, z_a, cb, z_c, sa, silu_a, ra, gha, m0, m1, m2, u, sc, silu_c, rc, ghc) = branches(
                o2_ref, za2_ref, ccx2_ref, cbzc2_ref, prev2_ref, i <= 1)
            d_x2b = dx2b_sc[...]
            dwo_ref[...] += _tn(ycat_sc[...], d_x2b)
            d_ycat = _nt(d_x2b, wo_ref[...])
            d_ya, d_yc = d_ycat[:, :ATTN_WIDTH], d_ycat[:, ATTN_WIDTH:]
            dgoa_ref[...] += jnp.sum(d_ya * gha, axis=0, keepdims=True)
            dgoc_ref[...] += jnp.sum(d_yc * ghc, axis=0, keepdims=True)
            d_ga = _rms_bwd(d_ya, gha, ra, g_oa, ATTN_WIDTH)
            d_gc = _rms_bwd(d_yc, ghc, rc, g_oc, CONV_WIDTH)
            d_o = d_ga * silu_a
            do_ref[...] = d_o.astype(MM)
            dza_ref[...] = (d_ga * o_v * (sa * (1.0 + z_a * (1.0 - sa)))).astype(MM)
            for hd in range(N_HEADS):
                cols = slice(hd * V_DIM, (hd + 1) * V_DIM)
                dl = jnp.sum(d_o[:, cols] * o_v[:, cols], axis=-1, keepdims=True)
                delta_ref[hd * 8:(hd + 1) * 8, :] = jnp.broadcast_to(dl, (T, LANES)).T[0:8, :]
            d_u = d_gc * cb * silu_c
            du_ref[...] = d_u
            dcbzc_ref[:, :CONV_WIDTH] = (d_gc * u * silu_c).astype(MM)
            dcbzc_ref[:, CONV_WIDTH:] = (d_gc * cb * u * (sc * (1.0 + z_c * (1.0 - sc)))).astype(MM)
            dcw_ref[0:1, :] += jnp.sum(d_u * m2, axis=0, keepdims=True)
            dcw_ref[1:2, :] += jnp.sum(d_u * m1, axis=0, keepdims=True)
            dcw_ref[2:3, :] += jnp.sum(d_u * m0, axis=0, keepdims=True)

        @pl.when(i % 2 == 0)
        def _():
            stage2(ycat_b, dx2b_b)
            stage1(ycat_a, dx2b_a)

        @pl.when(i % 2 == 1)
        def _():
            stage2(ycat_a, dx2b_a)
            stage1(ycat_b, dx2b_b)

    cur = lambda i: jnp.minimum(i, nt - 1)
    prv = lambda i: jnp.maximum(i - 1, 0)
    tb = T // 8

    def tile_specs(which):
        row = lambda w, blk: pl.BlockSpec((T, w), lambda i: (which(i), blk))
        return [row(ATTN_WIDTH, 0), row(ATTN_WIDTH, 1), row(2 * CONV_WIDTH, 1), row(2 * CONV_WIDTH, 2),
                pl.BlockSpec((8, 2 * CONV_WIDTH), lambda i: (jnp.maximum(which(i) * tb - 1, 0), 1))]

    row1 = lambda w: pl.BlockSpec((T, w), lambda i: (cur(i), 0))
    row2 = lambda w: pl.BlockSpec((T, w), lambda i: (prv(i), 0))
    acc = lambda shape: pl.BlockSpec(shape, lambda i: (0, 0))
    sds = jax.ShapeDtypeStruct
    return pl.pallas_call(
        body, name="mid", grid=(nt + 1,),
        out_shape=[sds((S, D_MODEL), f32), sds((S, ATTN_WIDTH), MM), sds((N_HEADS * 8, S), f32), sds((S, ATTN_WIDTH), MM),
                   sds((S, 2 * CONV_WIDTH), MM), sds((S, CONV_WIDTH), f32),
                   sds((D_MODEL, D_MODEL), f32), sds((D_MODEL, D_MODEL), f32), sds((PLE_DIM, D_MODEL), f32), sds((8, CONV_WIDTH), f32),
                   sds((1, ATTN_WIDTH), f32), sds((1, CONV_WIDTH), f32), sds((1, D_MODEL), f32), sds((1, LANES), f32)],
        in_specs=tile_specs(cur) + tile_specs(prv) + [
            row1(D_MODEL), row1(PLE_DIM), row1(D_MODEL),
            _const((D_MODEL, D_MODEL)), _const((D_MODEL, D_MODEL)), _const((PLE_DIM, D_MODEL)), _const((8, CONV_WIDTH)),
            _const((1, ATTN_WIDTH)), _const((1, CONV_WIDTH)), _const((1, D_MODEL))],
        out_specs=[row1(D_MODEL), row2(ATTN_WIDTH), pl.BlockSpec((N_HEADS * 8, T), lambda i: (0, prv(i))), row2(ATTN_WIDTH),
                   row2(2 * CONV_WIDTH), row2(CONV_WIDTH),
                   acc((D_MODEL, D_MODEL)), acc((D_MODEL, D_MODEL)), acc((PLE_DIM, D_MODEL)), acc((8, CONV_WIDTH)),
                   acc((1, ATTN_WIDTH)), acc((1, CONV_WIDTH)), acc((1, D_MODEL)), acc((1, LANES))],
        scratch_shapes=[pltpu.VMEM((T, D_MODEL), MM)] * 4,
        compiler_params=_cparams(("arbitrary",)),
    )(o, proj, proj, proj, proj, o, proj, proj, proj, proj, x, p, target, w_o, w_plg, w_pl, conv_w8, g_oa, g_oc, g_pl)


def _flash_bwd(q, k, v, d_o, lse2, delta, by_target):
    S = q.shape[0]
    T = min(TQ, S)
    nq = S // T

    def body(k_ref, v_ref, q_ref, do_ref, lse_ref, delta_ref, out_ref, dq_ref, dk_ref, dv_ref, in_ref, s_a, dp_a, s_b, dp_b,
             send_sems, recv_sems, local_sem):
        j = pl.program_id(1)
        scatter = _exchange(lambda d: out_ref.at[d], in_ref, send_sems, recv_sems, local_sem)

        @pl.when((pl.program_id(0) == 0) & (j == 0))
        def _():
            _exchange_start(scatter)

        @pl.when(j == 0)
        def _():
            dq_ref[...] = jnp.zeros(dq_ref.shape, f32)

        dk_ref[...] = jnp.zeros(dk_ref.shape, f32)
        dv_ref[...] = jnp.zeros(dv_ref.shape, f32)
        kv, vv = k_ref[...], v_ref[...]

        def rows_of(i):
            return pl.ds(pl.multiple_of(i * T, T), T)

        def scores(i, s_dst, dp_dst):
            rows = rows_of(i)
            s_dst[...] = _nt(kv, q_ref[rows, :])
            dp_dst[...] = _nt(vv, do_ref[rows, :])

        def grads(i, s_src, dp_src, masked):
            rows = rows_of(i)
            s = s_src[...]
            if masked:
                k_i = lax.broadcasted_iota(jnp.int32, (T, T), 0)
                q_i = lax.broadcasted_iota(jnp.int32, (T, T), 1)
                s = jnp.where(k_i <= q_i, s, NEG_INF)
            p = jnp.exp2(s * SCALE_LOG2E - lse_ref[0:1, rows])
            ds = (p * (dp_src[...] - delta_ref[0:1, rows])).astype(MM)
            dv_ref[...] += _nn(p.astype(MM), do_ref[rows, :])
            dk_ref[...] += _nn(ds, q_ref[rows, :])
            dq_ref[rows, :] += _tn(ds, kv)

        last = nq - 1
        scores(j, s_a, dp_a)
        scores(jnp.minimum(j + 1, last), s_b, dp_b)
        grads(j, s_a, dp_a, True)
        rest = last - j

        def pair(t, carry):
            u = j + 1 + 2 * t
            scores(u + 1, s_a, dp_a)
            grads(u, s_b, dp_b, False)
            scores(jnp.minimum(u + 2, last), s_b, dp_b)
            grads(u + 1, s_a, dp_a, False)
            return carry

        lax.fori_loop(0, rest // 2, pair, 0)

        @pl.when(rest % 2 == 1)
        def _():
            grads(last, s_b, dp_b, False)

        @pl.when((pl.program_id(0) == N_HEADS - 1) & (j == nq - 1))
        def _():
            _exchange_wait(scatter)

    whole = lambda w: pl.BlockSpec((S, w), lambda h, j: (0, h))
    stat = pl.BlockSpec((8, S), lambda h, j: (h, 0))
    return pl.pallas_call(
        body, name="flash_bwd", grid=(N_HEADS, nq),
        out_shape=[jax.ShapeDtypeStruct((S, QK_PAD), f32), jax.ShapeDtypeStruct((S, QK_PAD), f32),
                   jax.ShapeDtypeStruct((S, ATTN_WIDTH), f32), jax.ShapeDtypeStruct(by_target.shape, by_target.dtype)],
        in_specs=[pl.BlockSpec((T, HEAD_PAD), lambda h, j: (j, h)), pl.BlockSpec((T, V_DIM), lambda h, j: (j, h)),
                  whole(HEAD_PAD), whole(V_DIM), stat, stat, pl.BlockSpec(memory_space=pl.ANY)],
        out_specs=[whole(HEAD_PAD), pl.BlockSpec((T, HEAD_PAD), lambda h, j: (j, h)),
                   pl.BlockSpec((T, V_DIM), lambda h, j: (j, h)), pl.BlockSpec(memory_space=pl.ANY)],
        scratch_shapes=[pltpu.VMEM((T, T), f32)] * 4 + EXCHANGE_SEMS,
        compiler_params=_cparams(("arbitrary", "arbitrary")),
    )(k, v, q, d_o, lse2, delta, by_target)


def _bwd_proj(x, proj, d_q, d_k, d_v, d_za, d_cbzc, d_u, d_x2, tabs, g_in, w_in_p, g_cq, w_uq_p, g_ckv, w_ukv_p,
              g_q_p, g_k_p, conv_w8):
    S = x.shape[0]
    T = min(TS_E, S)
    nt = S // T

    def body(x_ref, cqkv_ref, ccx_ref, dq_ref, dk_ref, dv_ref, dza_ref, dcbzc_ref, du_ref, dun_ref, dx2_ref,
             c_ref, s1_ref, s2_ref, g_in_ref, w_in_ref, g_cq_ref, w_uq_ref, g_ckv_ref, w_ukv_ref, g_q_ref, g_k_ref, cw_ref,
             gx_ref, dwin_hbm, dwuq_ref, dwukv_ref, dgin_ref, dgcq_ref, dgckv_ref, dgq_ref, dgk_ref,
             dproj_a, dproj_b, dqraw_sc, dkvraw_sc, dwin_sc, sem):
        i = pl.program_id(0)

        @pl.when(i == 0)
        def _():
            dwin_sc[...] = jnp.zeros(dwin_sc.shape, f32)
            dproj_b[...] = jnp.zeros(dproj_b.shape, MM)
            for r in (dwuq_ref, dwukv_ref, dgin_ref, dgcq_ref, dgckv_ref, dgq_ref, dgk_ref):
                r[...] = jnp.zeros(r.shape, f32)

        def stage1(dproj_sc):
            live = (i < nt).astype(f32)
            c, s1, s2 = c_ref[...], s1_ref[...], s2_ref[...]
            g_q, g_k = g_q_ref[...], g_k_ref[...]
            g_cq, g_ckv = g_cq_ref[...], g_ckv_ref[...]

            c_q = cqkv_ref[:, 0:Q_LORA]
            r_cq = _rstd(c_q, Q_LORA)
            cqh = c_q * r_cq
            n_cq = (cqh * g_cq).astype(MM)
            q_raw = _nn(n_cq, w_uq_ref[...])
            dgq = jnp.zeros((1, HEAD_PAD), f32)
            for hd in range(N_HEADS):
                cols = slice(hd * HEAD_PAD, (hd + 1) * HEAD_PAD)
                qh = q_raw[:, cols]
                rq = _rstd(qh, QK_DIM)
                qhh = qh * rq
                dqh = dq_ref[:, cols] * QK_SCALE
                d_qn = jnp.concatenate([dqh[:, :LANES], _rope_t(dqh[:, LANES:], c, s1, s2)], axis=-1)
                dgq += jnp.sum(d_qn * qhh, axis=0, keepdims=True)
                dqraw_sc[:, cols] = _rms_bwd(d_qn, qhh, rq, g_q, QK_DIM).astype(MM)
            dgq_ref[...] += dgq * live
            d_qraw = dqraw_sc[...]
            dwuq_ref[...] += _tn((cqh * (g_cq * live)).astype(MM), d_qraw)
            d_ncq = _nt(d_qraw, w_uq_ref[...])
            dgcq_ref[...] += jnp.sum(d_ncq * cqh, axis=0, keepdims=True) * live
            dproj_sc[:, 0:Q_LORA] = _rms_bwd(d_ncq, cqh, r_cq, g_cq, Q_LORA).astype(MM)

            c_kv = cqkv_ref[:, Q_LORA:Q_LORA + KV_LORA]
            r_ckv = _rstd(c_kv, KV_LORA)
            ckvh = c_kv * r_ckv
            n_ckv = (ckvh * g_ckv).astype(MM)
            k_nope = _nn(n_ckv, w_ukv_ref[:, :N_HEADS * NOPE_DIM])
            kpe = cqkv_ref[:, Q_LORA + KV_LORA:Q_LORA + KV_LORA + LANES]
            ss_pe = jnp.sum(kpe * kpe, axis=-1, keepdims=True)
            g_kn, g_kp = g_k[:, :LANES], g_k[:, LANES:]
            d_kpe = jnp.zeros((T, LANES), f32)
            dgk_n = jnp.zeros((1, LANES), f32)
            dgk_p = jnp.zeros((1, LANES), f32)
            for hd in range(N_HEADS):
                kn = k_nope[:, hd * LANES:(hd + 1) * LANES]
                rk = lax.rsqrt((jnp.sum(kn * kn, axis=-1, keepdims=True) + ss_pe) * (1.0 / QK_DIM) + RMS_EPS)
                knh, kph = kn * rk, kpe * rk
                d_kn_n = dk_ref[:, hd * HEAD_PAD:hd * HEAD_PAD + LANES] * QK_SCALE
                d_kr = _rope_t(dk_ref[:, hd * HEAD_PAD + LANES:(hd + 1) * HEAD_PAD] * QK_SCALE, c, s1, s2)
                dgk_n += jnp.sum(d_kn_n * knh, axis=0, keepdims=True)
                dgk_p += jnp.sum(d_kr * kph, axis=0, keepdims=True)
                u_n, u_p = d_kn_n * g_kn, d_kr * g_kp
                mt = (jnp.sum(u_n * knh, axis=-1, keepdims=True) + jnp.sum(u_p * kph, axis=-1, keepdims=True)) * (1.0 / QK_DIM)
                dkvraw_sc[:, hd * LANES:(hd + 1) * LANES] = (rk * (u_n - knh * mt)).astype(MM)
                d_kpe += rk * (u_p - kph * mt)
            dgk_ref[:, :LANES] += dgk_n * live
            dgk_ref[:, LANES:] += dgk_p * live
            dkvraw_sc[:, N_HEADS * NOPE_DIM:] = dv_ref[...].astype(MM)
            d_kvraw = dkvraw_sc[...]
            dwukv_ref[...] += _tn((ckvh * (g_ckv * live)).astype(MM), d_kvraw)
            d_nckv = _nt(d_kvraw, w_ukv_ref[...])
            dgckv_ref[...] += jnp.sum(d_nckv * ckvh, axis=0, keepdims=True) * live
            dproj_sc[:, Q_LORA:Q_LORA + KV_LORA] = _rms_bwd(d_nckv, ckvh, r_ckv, g_ckv, KV_LORA).astype(MM)
            dproj_sc[:, Q_LORA + KV_LORA:Q_LORA + KV_LORA + LANES] = d_kpe.astype(MM)

            dproj_sc[:, 512:1024] = dza_ref[...]
            d_u = du_ref[...]
            nxt = dun_ref[...] * (i < nt - 1).astype(f32)
            d_m = cw_ref[2:3, :] * d_u + cw_ref[1:2, :] * _shift_up(d_u, nxt, 1) + cw_ref[0:1, :] * _shift_up(d_u, nxt, 2)
            dproj_sc[:, 1024:1536] = (d_m * ccx_ref[:, CONV_WIDTH:]).astype(MM)
            dproj_sc[:, 1536:2048] = (d_m * ccx_ref[:, :CONV_WIDTH]).astype(MM)
            dproj_sc[:, 2048:3072] = dcbzc_ref[...]

        def stage2(dproj_sc):
            g_in = g_in_ref[...]
            xv = x_ref[...]
            r_in = _rstd(xv, D_MODEL)
            xh = xv * r_in
            hb = (xh * g_in).astype(MM)
            d_proj = dproj_sc[...]
            dwin_sc[...] += _tn(hb, d_proj)
            d_h = _nt(d_proj, w_in_ref[...])
            dgin_ref[...] += jnp.sum(d_h * xh, axis=0, keepdims=True)
            gx_ref[...] = dx2_ref[...] + _rms_bwd(d_h, xh, r_in, g_in, D_MODEL)

        @pl.when(i % 2 == 0)
        def _():
            stage2(dproj_b)
            stage1(dproj_a)

        @pl.when(i % 2 == 1)
        def _():
            stage2(dproj_a)
            stage1(dproj_b)

        @pl.when(i == nt)
        def _():
            cp = pltpu.make_async_copy(dwin_sc, dwin_hbm, sem)
            cp.start()
            cp.wait()

    cur = lambda i: jnp.minimum(i, nt - 1)
    prv = lambda i: jnp.maximum(i - 1, 0)
    row = lambda w, blk=0: pl.BlockSpec((T, w), lambda i: (cur(i), blk))
    row2 = lambda w: pl.BlockSpec((T, w), lambda i: (prv(i), 0))
    acc = lambda shape: pl.BlockSpec(shape, lambda i: (0, 0))
    tb = T // 8
    sds = jax.ShapeDtypeStruct
    return pl.pallas_call(
        body, name="bwd_proj", grid=(nt + 1,),
        out_shape=[sds((S, D_MODEL), f32), sds((D_MODEL, PROJ_PAD), f32), sds((Q_LORA, QK_PAD), f32),
                   sds((KV_LORA, 2 * ATTN_WIDTH), f32), sds((1, D_MODEL), f32), sds((1, Q_LORA), f32), sds((1, KV_LORA), f32),
                   sds((1, HEAD_PAD), f32), sds((1, HEAD_PAD), f32)],
        in_specs=[row2(D_MODEL), row(512, 0), row(2 * CONV_WIDTH, 1), row(QK_PAD), row(QK_PAD), row(ATTN_WIDTH),
                  row(ATTN_WIDTH), row(2 * CONV_WIDTH), row(CONV_WIDTH),
                  pl.BlockSpec((8, CONV_WIDTH), lambda i: (jnp.minimum((cur(i) + 1) * tb, S // 8 - 1), 0)),
                  row2(D_MODEL), row(LANES), row(LANES), row(LANES),
                  _const((1, D_MODEL)), _const((D_MODEL, PROJ_PAD)), _const((1, Q_LORA)), _const((Q_LORA, QK_PAD)),
                  _const((1, KV_LORA)), _const((KV_LORA, 2 * ATTN_WIDTH)), _const((1, HEAD_PAD)), _const((1, HEAD_PAD)),
                  _const((8, CONV_WIDTH))],
        out_specs=[row2(D_MODEL), pl.BlockSpec(memory_space=pl.ANY), acc((Q_LORA, QK_PAD)), acc((KV_LORA, 2 * ATTN_WIDTH)),
                   acc((1, D_MODEL)), acc((1, Q_LORA)), acc((1, KV_LORA)), acc((1, HEAD_PAD)), acc((1, HEAD_PAD))],
        scratch_shapes=[pltpu.VMEM((T, PROJ_PAD), MM), pltpu.VMEM((T, PROJ_PAD), MM), pltpu.VMEM((T, QK_PAD), MM),
                        pltpu.VMEM((T, 2 * ATTN_WIDTH), MM), pltpu.VMEM((D_MODEL, PROJ_PAD), f32), pltpu.SemaphoreType.DMA],
        compiler_params=_cparams(("arbitrary",)),
    )(x, proj, proj, d_q, d_k, d_v, d_za, d_cbzc, d_u, d_u, d_x2, *tabs, g_in, w_in_p, g_cq, w_uq_p, g_ckv, w_ukv_p,
      g_q_p, g_k_p, conv_w8)


def _all_gather(block, name, in_vmem):
    m_per, n = block.shape

    def body(x_ref, out_ref, send_sems, recv_sems, local_sem):
        x, y, c = lax.axis_index("x"), lax.axis_index("y"), lax.axis_index("c")
        me, sibling = (x, y, c), (x, y, 1 - c)
        chips = [(1 - x, y), (x, 1 - y), (1 - x, 1 - y)]

        def rows(px, py, pc):
            return out_ref.at[pl.ds((4 * px + 2 * py + pc) * m_per, m_per), :]

        def copy(k, blk, to, src=None):
            return pltpu.make_async_remote_copy(
                src_ref=rows(*blk) if src is None else src, dst_ref=rows(*blk),
                send_sem=send_sems.at[k], recv_sem=recv_sems.at[k], device_id=to, device_id_type=MESH)

        mine = pltpu.make_async_copy(x_ref, rows(*me), local_sem)
        mine.start()
        first = [copy(0, me, sibling, src=x_ref)]
        first += [copy(1 + j, me, (*chip, c), src=x_ref) for j, chip in enumerate(chips)]
        for cp in first:
            cp.start()
        passed = [copy(4 + j, (*chip, c), sibling) for j, chip in enumerate(chips)]
        for j, chip in enumerate(chips):
            copy(1 + j, (*chip, c), me).wait_recv()
            passed[j].start()
        copy(0, sibling, me).wait_recv()
        for j, chip in enumerate(chips):
            copy(4 + j, (*chip, 1 - c), me).wait_recv()
        for cp in first + passed:
            cp.wait_send()
        mine.wait()

    space = pltpu.VMEM if in_vmem else pl.ANY
    return pl.pallas_call(
        body, name=name, out_shape=jax.ShapeDtypeStruct((N_DEV * m_per, n), block.dtype),
        in_specs=[pl.BlockSpec(memory_space=space)], out_specs=pl.BlockSpec(memory_space=space),
        scratch_shapes=[pltpu.SemaphoreType.DMA((7,)), pltpu.SemaphoreType.DMA((7,)), pltpu.SemaphoreType.DMA],
    )(block)


def _sibling_exchange(pack):
    nchip, _, R, n = pack.shape

    def body(p_ref, got_ref, send_sems, recv_sems):
        x, y, c = lax.axis_index("x"), lax.axis_index("y"), lax.axis_index("c")
        copies = [pltpu.make_async_remote_copy(src_ref=p_ref.at[t, 1 - c], dst_ref=got_ref.at[t], send_sem=send_sems.at[t],
                                               recv_sem=recv_sems.at[t], device_id=(x, y, 1 - c), device_id_type=MESH)
                  for t in range(nchip)]
        for cp in copies:
            cp.start()
        for cp in copies:
            cp.wait()

    return pl.pallas_call(
        body, name="rs_sibling", out_shape=jax.ShapeDtypeStruct((nchip, R, n), pack.dtype),
        in_specs=[pl.BlockSpec(memory_space=pl.ANY)], out_specs=pl.BlockSpec(memory_space=pl.ANY),
        scratch_shapes=[pltpu.SemaphoreType.DMA((nchip,)), pltpu.SemaphoreType.DMA((nchip,))],
    )(pack)


def _chip_exchange(part):
    nchip, R, n = part.shape

    def body(p_ref, got_ref, send_sems, recv_sems, local_sem):
        x, y, c = lax.axis_index("x"), lax.axis_index("y"), lax.axis_index("c")
        my_chip = 2 * x + y
        chips = [(1 - x, y), (x, 1 - y), (1 - x, 1 - y)]
        mine = pltpu.make_async_copy(p_ref.at[my_chip], got_ref.at[my_chip], local_sem)
        mine.start()
        sends = []
        for k, (tx, ty) in enumerate(chips):
            cp = pltpu.make_async_remote_copy(src_ref=p_ref.at[2 * tx + ty], dst_ref=got_ref.at[my_chip],
                                              send_sem=send_sems.at[k], recv_sem=recv_sems.at[k],
                                              device_id=(tx, ty, c), device_id_type=MESH)
            cp.start()
            sends.append(cp)
        for k, (sx, sy) in enumerate(chips):
            pltpu.make_async_remote_copy(src_ref=p_ref.at[my_chip], dst_ref=got_ref.at[2 * sx + sy],
                                         send_sem=send_sems.at[k], recv_sem=recv_sems.at[k],
                                         device_id=(sx, sy, c), device_id_type=MESH).wait_recv()
        for cp in sends:
            cp.wait_send()
        mine.wait()

    return pl.pallas_call(
        body, name="rs_chips", out_shape=jax.ShapeDtypeStruct((nchip, R, n), part.dtype),
        in_specs=[pl.BlockSpec(memory_space=pl.ANY)], out_specs=pl.BlockSpec(memory_space=pl.ANY),
        scratch_shapes=[pltpu.SemaphoreType.DMA((3,)), pltpu.SemaphoreType.DMA((3,)), pltpu.SemaphoreType.DMA],
    )(part)


def _add_blocks(pack, from_sib, my_c):
    nb, _, R, n = pack.shape

    def body(c_ref, a_ref, b_ref, o_ref):
        o_ref[...] = (a_ref[0] + b_ref[...]).astype(o_ref.dtype)

    return pl.pallas_call(
        body, name="rs_add", out_shape=jax.ShapeDtypeStruct((nb, R, n), RS_DTYPE),
        grid_spec=pltpu.PrefetchScalarGridSpec(
            num_scalar_prefetch=1, grid=(nb,),
            in_specs=[pl.BlockSpec((1, 1, R, n), lambda i, c: (i, c[0], 0, 0)), pl.BlockSpec((1, R, n), lambda i, c: (i, 0, 0))],
            out_specs=pl.BlockSpec((1, R, n), lambda i, c: (i, 0, 0))),
        compiler_params=_cparams(("parallel",)))(my_c, pack, from_sib)


def _sum_slabs(got, name):
    nb, R, n = got.shape

    def body(g_ref, o_ref):
        tot = g_ref[0].astype(f32)
        for d in range(1, nb):
            tot = tot + g_ref[d].astype(f32)
        o_ref[...] = tot

    return pl.pallas_call(body, name=name, grid=(n // LANES,), out_shape=jax.ShapeDtypeStruct((R, n), f32),
                          in_specs=[pl.BlockSpec((nb, R, LANES), lambda j: (0, 0, j))], out_specs=pl.BlockSpec((R, LANES), lambda j: (0, j)),
                          compiler_params=_cparams(("parallel",)))(got)


def _adamw_all(ws, gs, ms, vs, gain_parts):
    n = len(ws)
    ng = len(GAIN_SLOTS)

    def body(*refs):
        gp_ref = refs[0]
        w_refs = refs[1:1 + n]
        g_refs = refs[1 + n:1 + 2 * n - ng]
        m_refs = refs[1 + 2 * n - ng:1 + 3 * n - ng]
        v_refs = refs[1 + 3 * n - ng:1 + 4 * n - ng]
        outs = refs[1 + 4 * n - ng:]
        gsum_ref, loss_ref = outs[0], outs[1]
        gg_refs = outs[2:2 + ng]
        d_refs, nm_refs, nv_refs = (outs[2 + ng + k * n:2 + ng + (k + 1) * n] for k in range(3))
        tot = gp_ref[0]
        for d in range(1, N_DEV):
            tot = tot + gp_ref[d]
        gsum_ref[...] = tot
        loss_ref[...] = gsum_ref[GAIN_ROWS - 1:GAIN_ROWS, :]
        for k in range(n):
            if k < ng:
                r0, width = GAIN_SLOTS[k]
                for r in range(width // LANES):
                    gg_refs[k][:, r * LANES:(r + 1) * LANES] = gsum_ref[r0 + r:r0 + r + 1, :]
                g = gg_refs[k][...]
            else:
                g = g_refs[k - ng][...]
            w = w_refs[k][...]
            m = ADAM_B1 * m_refs[k][...] + (1.0 - ADAM_B1) * g
            v = ADAM_B2 * v_refs[k][...] + (1.0 - ADAM_B2) * (g * g)
            m_hat = m / (1.0 - ADAM_B1 ** ADAM_STEP)
            v_hat = v / (1.0 - ADAM_B2 ** ADAM_STEP)
            d_refs[k][...] = -ADAM_LR * (m_hat / (jnp.sqrt(v_hat) + ADAM_EPS) + ADAM_WD * w)
            nm_refs[k][...] = m
            nv_refs[k][...] = v

    sds = jax.ShapeDtypeStruct
    like = [sds(w.shape, f32) for w in ws]
    out_shape = [sds((GAIN_ROWS, LANES), f32), sds((1, LANES), f32)] + like[:ng] + like * 3
    full = lambda a: pl.BlockSpec(a.shape, lambda i: (0,) * len(a.shape))
    operands = [gain_parts, *ws, *gs, *ms, *vs]
    outs = pl.pallas_call(
        body, name="adamw", grid=(1,), out_shape=out_shape, in_specs=[full(a) for a in operands],
        out_specs=[full(a) for a in out_shape], compiler_params=_cparams(("arbitrary",)),
    )(*operands)
    loss = outs[1]
    gg = outs[2:2 + ng]
    deltas, new_m, new_v = (outs[2 + ng + k * n:2 + ng + (k + 1) * n] for k in range(3))
    return loss, gg, deltas, new_m, new_v


GAIN_ROWS = 32
GAIN_SLOTS = [(0, 1024), (8, 256), (10, 128), (11, 256), (13, 256), (15, 512), (19, 512), (23, 1024)]

SH_IN = IN_TOTAL // N_DEV
SLAB1_COLS = 384
UQ_ROWS = Q_LORA * (N_HEADS * QK_DIM // N_DEV) // SLAB1_COLS
UKV_ROWS = 48
SLAB1_ROWS = D_MODEL + UQ_ROWS + UKV_ROWS
SH_ROWS = D_MODEL // N_DEV
PL_ROWS = PLE_DIM * (D_MODEL // N_DEV) // D_MODEL
CONV_ROWS = 16
SLAB2_ROWS = 2 * SH_ROWS + PL_ROWS + CONV_ROWS
IN_SEGMENTS = [(0, 448, 0), (448, 960, 512), (1472, 2496, 1024), (960, 1472, 2048), (2496, 3008, 2560)]


def _flat_rows(a, rows, cols):
    lead = a.shape[:-2]
    flat = a.reshape(lead + (-1,))
    pad = [(0, 0)] * len(lead) + [(0, rows * cols - flat.shape[-1])]
    return jnp.pad(flat, pad).reshape(lead + (rows, cols))


def _slab1(in_sh, uq_sh, ukv_sh):
    pad = [(0, 0)] * (in_sh.ndim - 1) + [(0, SLAB1_COLS - SH_IN)]
    return jnp.concatenate([jnp.pad(in_sh, pad), _flat_rows(uq_sh, UQ_ROWS, SLAB1_COLS), _flat_rows(ukv_sh, UKV_ROWS, SLAB1_COLS)],
                           axis=-2)


def _slab2(o_sh, plg_sh, pl_sh, conv_sh):
    return jnp.concatenate([o_sh, plg_sh, _flat_rows(pl_sh, PL_ROWS, D_MODEL), _flat_rows(conv_sh, CONV_ROWS, D_MODEL)], axis=-2)


def _by_device(a, width):
    return a.reshape(a.shape[0], N_DEV, width).transpose(1, 0, 2)


def _from_devices(a):
    return a.transpose(1, 0, 2).reshape(a.shape[1], -1)


def kernel(x, p, positions, g_in, w_in, g_cq, w_uq, g_ckv, w_ukv, g_q, g_k, conv_w, g_oa, g_oc, w_o, w_pl, w_plg, g_pl, loss_target, m_g_in, m_w_in, m_g_cq, m_w_uq, m_g_ckv, m_w_ukv, m_g_q, m_g_k, m_conv_w, m_g_oa, m_g_oc, m_w_o, m_w_pl, m_w_plg, m_g_pl, v_g_in, v_w_in, v_g_cq, v_w_uq, v_g_ckv, v_w_ukv, v_g_q, v_g_k, v_conv_w, v_g_oa, v_g_oc, v_w_o, v_w_pl, v_w_plg, v_g_pl):
    S = x.shape[1]
    nd = N_DEV
    xs, ps, tgt = x[0], p[0, 0], loss_target[0]

    slab1 = _slab1(w_in[0], w_uq[0], w_ukv[0]).astype(bf16)
    g1 = _all_gather(slab1, "ag_weights", in_vmem=False).reshape(nd, SLAB1_ROWS, SLAB1_COLS)
    conv_bits = lax.bitcast_convert_type(conv_w[0], bf16).reshape(CONV_K, -1)
    slab2 = _slab2(w_o[0].astype(bf16), w_plg[0].astype(bf16), w_pl[0].astype(bf16), conv_bits)

    pieces, pos = [], 0
    for a, b, start in sorted(IN_SEGMENTS, key=lambda s: s[2]):
        if start > pos:
            pieces.append(jnp.zeros((D_MODEL, start - pos), bf16))
        pieces += [g1[d, :D_MODEL, max(a - SH_IN * d, 0):min(b - SH_IN * d, SH_IN)] for d in range(a // SH_IN, (b - 1) // SH_IN + 1)]
        pos = start + b - a
    w_in_p = jnp.concatenate(pieces, axis=1)
    wuq = _from_devices(g1[:, D_MODEL:D_MODEL + UQ_ROWS].reshape(nd, Q_LORA, -1)).reshape(Q_LORA, N_HEADS, QK_DIM)
    w_uq_p = jnp.pad(wuq, ((0, 0), (0, 0), (0, HEAD_PAD - QK_DIM))).reshape(Q_LORA, QK_PAD)
    ukv_flat = g1[:, D_MODEL + UQ_ROWS:].reshape(nd, -1)[:, :KV_LORA * V_DIM]
    wukv = _from_devices(ukv_flat.reshape(nd, KV_LORA, V_DIM)).reshape(KV_LORA, N_HEADS, 2, NOPE_DIM)
    w_ukv_p = wukv.transpose(0, 2, 1, 3).reshape(KV_LORA, 2 * ATTN_WIDTH)
    w_in_p, w_uq_p, w_ukv_p = (w.astype(MM) for w in (w_in_p, w_uq_p, w_ukv_p))

    g_q_p = jnp.pad(g_q, ((0, 0), (0, HEAD_PAD - QK_DIM)))
    g_k_p = jnp.pad(g_k, ((0, 0), (0, HEAD_PAD - QK_DIM)))

    inv_freq = 1.0 / (ROPE_THETA ** (jnp.arange(0, ROPE_DIM, 2, dtype=f32) / ROPE_DIM))
    inv_row = jnp.concatenate([inv_freq, inv_freq, jnp.zeros((LANES - ROPE_DIM,), f32)]).reshape(1, LANES)
    posf = jnp.broadcast_to(positions[0].astype(f32)[:, None], (S, LANES))
    tabs = _rope_tables(posf, inv_row)

    proj, q, k, v = _fwd_proj(xs, tabs, g_in, w_in_p, g_cq, w_uq_p, g_ckv, w_ukv_p, g_q_p, g_k_p)
    o, lse, g2 = _flash_fwd(q, k, v, slab2)
    w_o_f = g2[:, 0:SH_ROWS].reshape(D_MODEL, D_MODEL).astype(MM)
    w_plg_f = g2[:, SH_ROWS:2 * SH_ROWS].reshape(D_MODEL, D_MODEL).astype(MM)
    w_pl_f = _from_devices(g2[:, 2 * SH_ROWS:2 * SH_ROWS + PL_ROWS].reshape(nd, PLE_DIM, -1)).astype(MM)
    cv_bits = g2[:, 2 * SH_ROWS + PL_ROWS:].reshape(nd, -1)[:, :CONV_K * (CONV_WIDTH // nd) * 2]
    conv_full = _from_devices(lax.bitcast_convert_type(cv_bits.reshape(nd, CONV_K, CONV_WIDTH // nd, 2), f32))
    conv_w8 = jnp.pad(conv_full, ((0, 8 - CONV_K), (0, 0)))
    (d_x2, d_o, delta, d_za, d_cbzc, d_u, dw_o, dw_plg, dw_pl, dcw, dg_oa, dg_oc, dg_pl, loss_part) = _mid(
        o, proj, xs, ps, tgt, w_o_f, w_plg_f, w_pl_f, conv_w8, g_oa, g_oc, g_pl)

    slabs_a = _slab2(dw_o.reshape(nd, SH_ROWS, D_MODEL), dw_plg.reshape(nd, SH_ROWS, D_MODEL), _by_device(dw_pl, D_MODEL // nd),
                     _by_device(dcw[:CONV_K], CONV_WIDTH // nd)).astype(RS_DTYPE)
    d_q, d_k, d_v, got_a = _flash_bwd(q, k, v, d_o, lse, delta, slabs_a)
    shard_a = _sum_slabs(got_a, "rs_sum8")
    (grad_x, dw_in_p, dw_uq_p, dw_ukv_p, dg_in, dg_cq, dg_ckv, dg_q_p, dg_k_p) = _bwd_proj(
        xs, proj, d_q, d_k, d_v, d_za, d_cbzc, d_u, d_x2, tabs, g_in, w_in_p, g_cq, w_uq_p, g_ckv, w_ukv_p, g_q_p, g_k_p, conv_w8)

    def dw_in_cols(a, b):
        return [dw_in_p[:, start + max(a, s) - s:start + min(b, e) - s] for s, e, start in sorted(IN_SEGMENTS) if max(a, s) < min(b, e)]

    in_shards = jnp.stack([jnp.concatenate(dw_in_cols(SH_IN * d, SH_IN * (d + 1)), axis=1) for d in range(nd)])
    dw_uq = dw_uq_p.reshape(Q_LORA, N_HEADS, HEAD_PAD)[:, :, :QK_DIM].reshape(Q_LORA, N_HEADS * QK_DIM)
    dw_ukv = dw_ukv_p.reshape(KV_LORA, 2, N_HEADS, NOPE_DIM).transpose(0, 2, 1, 3).reshape(KV_LORA, 2 * ATTN_WIDTH)
    slabs_b = _slab1(in_shards, _by_device(dw_uq, N_HEADS * QK_DIM // nd), _by_device(dw_ukv, 2 * ATTN_WIDTH // nd))
    slabs_b = slabs_b.reshape(4, 2, SLAB1_ROWS, SLAB1_COLS)
    my_c = lax.axis_index("c").astype(jnp.int32).reshape(1)
    chip_part = _add_blocks(slabs_b, _sibling_exchange(slabs_b), my_c)
    shard_b = _sum_slabs(_chip_exchange(chip_part), "rs_sum")
    grads_w = [shard_b[:D_MODEL, :SH_IN], shard_b[D_MODEL:D_MODEL + UQ_ROWS].reshape(Q_LORA, -1),
               shard_b[D_MODEL + UQ_ROWS:].reshape(-1)[:KV_LORA * V_DIM].reshape(KV_LORA, V_DIM),
               shard_a[2 * SH_ROWS + PL_ROWS:].reshape(-1)[:CONV_K * CONV_WIDTH // nd].reshape(CONV_K, -1),
               shard_a[0:SH_ROWS], shard_a[2 * SH_ROWS:2 * SH_ROWS + PL_ROWS].reshape(PLE_DIM, -1), shard_a[SH_ROWS:2 * SH_ROWS]]

    gains_part = [dg_in, dg_cq, dg_ckv, dg_q_p, dg_k_p, dg_oa, dg_oc, dg_pl]
    gflat = jnp.concatenate([g.reshape(-1) for g in gains_part] + [jnp.zeros(((GAIN_ROWS - 1) * LANES - 3968,), f32),
                                                                    loss_part.reshape(-1)])
    gain_parts = _all_gather(gflat.reshape(GAIN_ROWS, LANES), "ag_gains", in_vmem=True).reshape(nd, GAIN_ROWS, LANES)

    gains = [g_in, g_cq, g_ckv, g_q_p, g_k_p, g_oa, g_oc, g_pl]
    padq = lambda a: jnp.pad(a, ((0, 0), (0, HEAD_PAD - QK_DIM)))
    m_gains = [m_g_in, m_g_cq, m_g_ckv, padq(m_g_q), padq(m_g_k), m_g_oa, m_g_oc, m_g_pl]
    v_gains = [v_g_in, v_g_cq, v_g_ckv, padq(v_g_q), padq(v_g_k), v_g_oa, v_g_oc, v_g_pl]
    ws = gains + [w_in[0], w_uq[0], w_ukv[0], conv_w[0], w_o[0], w_pl[0], w_plg[0]]
    ms = m_gains + [m_w_in[0], m_w_uq[0], m_w_ukv[0], m_conv_w[0], m_w_o[0], m_w_pl[0], m_w_plg[0]]
    vs = v_gains + [v_w_in[0], v_w_uq[0], v_w_ukv[0], v_conv_w[0], v_w_o[0], v_w_pl[0], v_w_plg[0]]
    loss_row, gg, deltas, new_m, new_v = _adamw_all(ws, grads_w, ms, vs, gain_parts)
    loss = loss_row[0, 0]

    def ordered(gl, wl):
        g_in_, g_cq_, g_ckv_, g_q_, g_k_, g_oa_, g_oc_, g_pl_ = gl
        g_q_, g_k_ = g_q_[:, :QK_DIM], g_k_[:, :QK_DIM]
        w_in_, w_uq_, w_ukv_, cw_, w_o_, w_pl_, w_plg_ = [w[None] for w in wl]
        return [g_in_, w_in_, g_cq_, w_uq_, g_ckv_, w_ukv_, g_q_, g_k_, cw_, g_oa_, g_oc_, w_o_, w_pl_, w_plg_, g_pl_]

    ng = len(gains)
    outs = [loss, grad_x[None]]
    outs += ordered(gg, grads_w)
    for lst in (deltas, new_m, new_v):
        outs += ordered(lst[:ng], lst[ng:])
    return tuple(outs)
```

```python
import functools
import math

import jax
import jax.numpy as jnp
from jax import lax
from jax.experimental import pallas as pl
from jax.experimental.pallas import tpu as pltpu

f32 = jnp.float32
bf16 = jnp.bfloat16
MM = jnp.bfloat16
RS_DTYPE = jnp.bfloat16

D_MODEL = 1024
PLE_DIM = 256
N_HEADS = 4
NOPE_DIM = 128
ROPE_DIM = 64
V_DIM = 128
QK_DIM = NOPE_DIM + ROPE_DIM
Q_LORA = 256
KV_LORA = 128
ATTN_WIDTH = N_HEADS * V_DIM
CONV_WIDTH = D_MODEL - ATTN_WIDTH
CONV_K = 3
ROPE_THETA = 10000.0
RMS_EPS = 1e-6
NEG_INF = -1e30
IN_TOTAL = Q_LORA + KV_LORA + ROPE_DIM + ATTN_WIDTH + 4 * CONV_WIDTH
ADAM_LR = 0.001
ADAM_B1 = 0.9
ADAM_B2 = 0.999
ADAM_EPS = 1e-08
ADAM_WD = 0.01
ADAM_STEP = 10

N_DEV = 8
LANES = 128
HEAD_PAD = 256
QK_PAD = N_HEADS * HEAD_PAD
PROJ_PAD = 3072
TS = 512
TS_M = 256
TS_E = 256
TQ = 512
VMEM_LIMIT = 58 * 1024 * 1024
MESH = pl.DeviceIdType.MESH
QK_SCALE = 1.0 / math.sqrt(QK_DIM)
SCALE_LOG2E = QK_SCALE * math.log2(math.e)


def _cparams(sem=None):
    return pltpu.CompilerParams(dimension_semantics=sem, vmem_limit_bytes=VMEM_LIMIT)


def _nt(a, b):
    return lax.dot_general(a, b, (((1,), (1,)), ((), ())), preferred_element_type=f32)


def _tn(a, b):
    return lax.dot_general(a, b, (((0,), (0,)), ((), ())), preferred_element_type=f32)


def _nn(a, b):
    return jnp.dot(a, b, preferred_element_type=f32)


def _rstd(x, n):
    return lax.rsqrt(jnp.sum(x * x, axis=-1, keepdims=True) * (1.0 / n) + RMS_EPS)


def _rms_bwd(dy, xhat, r, g, n):
    u = dy * g
    return r * (u - xhat * (jnp.sum(u * xhat, axis=-1, keepdims=True) * (1.0 / n)))


def _rope(t, c, s1, s2):
    return t * c + pltpu.roll(t, 32, 1) * s1 + pltpu.roll(t, 96, 1) * s2


def _rope_t(d, c, s1, s2):
    return d * c - pltpu.roll(d, 96, 1) * s2 - pltpu.roll(d, 32, 1) * s1


def _const(shape):
    return pl.BlockSpec(shape, lambda *_: (0,) * len(shape), pipeline_mode=pl.Buffered(1))


def _rope_tables(posf, inv_freq):
    S = posf.shape[0]

    def body(pos_ref, f_ref, c_ref, s1_ref, s2_ref):
        ang = pos_ref[...] * f_ref[...]
        lane = lax.broadcasted_iota(jnp.int32, ang.shape, 1)
        cs, sn = jnp.cos(ang), jnp.sin(ang)
        c_ref[...] = jnp.where(lane < ROPE_DIM, cs, 0.0)
        s1_ref[...] = jnp.where((lane >= ROPE_DIM // 2) & (lane < ROPE_DIM), sn, 0.0)
        s2_ref[...] = jnp.where(lane < ROPE_DIM // 2, -sn, 0.0)

    t = min(S, TS)
    spec = pl.BlockSpec((t, LANES), lambda i: (i, 0))
    return pl.pallas_call(
        body, name="rope_tables", grid=(S // t,),
        out_shape=[jax.ShapeDtypeStruct((S, LANES), f32)] * 3,
        in_specs=[spec, _const((1, LANES))], out_specs=[spec] * 3,
        compiler_params=_cparams(("parallel",)),
    )(posf, inv_freq)


def _fwd_proj(x, tabs, g_in, w_in_p, g_cq, w_uq_p, g_ckv, w_ukv_p, g_q_p, g_k_p):
    S = x.shape[0]
    T = min(TS, S)
    nt = S // T
    LAT = Q_LORA + KV_LORA + LANES

    def body(x_ref, c_ref, s1_ref, s2_ref, g_in_ref, w_in_ref, g_cq_ref, w_uq_ref, g_ckv_ref, w_ukv_ref,
             g_q_ref, g_k_ref, proj_ref, q_ref, k_ref, v_ref, lat_a, lat_b):
        i = pl.program_id(0)

        @pl.when(i == 0)
        def _():
            lat_b[...] = jnp.zeros(lat_b.shape, f32)

        def stage1(lat):
            xv = x_ref[...]
            h = (xv * _rstd(xv, D_MODEL) * g_in_ref[...]).astype(MM)
            proj_ref[...] = _nn(h, w_in_ref[...])
            lat[...] = proj_ref[:, 0:LAT]

        def stage2(lat):
            c, s1, s2 = c_ref[...], s1_ref[...], s2_ref[...]
            g_q, g_k = g_q_ref[...], g_k_ref[...]
            c_q = lat[:, 0:Q_LORA]
            n_cq = (c_q * _rstd(c_q, Q_LORA) * g_cq_ref[...]).astype(MM)
            q_raw = _nn(n_cq, w_uq_ref[...])
            for hd in range(N_HEADS):
                qh = q_raw[:, hd * HEAD_PAD:(hd + 1) * HEAD_PAD]
                qn = qh * _rstd(qh, QK_DIM) * g_q
                q_ref[:, hd * HEAD_PAD:hd * HEAD_PAD + LANES] = qn[:, :LANES].astype(MM)
                q_ref[:, hd * HEAD_PAD + LANES:(hd + 1) * HEAD_PAD] = _rope(qn[:, LANES:], c, s1, s2).astype(MM)

            c_kv = lat[:, Q_LORA:Q_LORA + KV_LORA]
            n_ckv = (c_kv * _rstd(c_kv, KV_LORA) * g_ckv_ref[...]).astype(MM)
            kv_raw = _nn(n_ckv, w_ukv_ref[...])
            kpe = lat[:, Q_LORA + KV_LORA:LAT]
            ss_pe = jnp.sum(kpe * kpe, axis=-1, keepdims=True)
            kr = _rope(kpe * g_k[:, LANES:], c, s1, s2)
            for hd in range(N_HEADS):
                kn = kv_raw[:, hd * LANES:(hd + 1) * LANES]
                rk = lax.rsqrt((jnp.sum(kn * kn, axis=-1, keepdims=True) + ss_pe) * (1.0 / QK_DIM) + RMS_EPS)
                k_ref[:, hd * HEAD_PAD:hd * HEAD_PAD + LANES] = (kn * rk * g_k[:, :LANES]).astype(MM)
                k_ref[:, hd * HEAD_PAD + LANES:(hd + 1) * HEAD_PAD] = (kr * rk).astype(MM)
            v_ref[...] = kv_raw[:, N_HEADS * NOPE_DIM:].astype(MM)

        @pl.when(i % 2 == 0)
        def _():
            stage2(lat_b)
            stage1(lat_a)

        @pl.when(i % 2 == 1)
        def _():
            stage2(lat_a)
            stage1(lat_b)

    cur = lambda w: pl.BlockSpec((T, w), lambda i: (jnp.minimum(i, nt - 1), 0))
    prv = lambda w: pl.BlockSpec((T, w), lambda i: (jnp.maximum(i - 1, 0), 0))
    return pl.pallas_call(
        body, name="fwd_proj", grid=(nt + 1,),
        out_shape=[jax.ShapeDtypeStruct((S, PROJ_PAD), f32), jax.ShapeDtypeStruct((S, QK_PAD), MM),
                   jax.ShapeDtypeStruct((S, QK_PAD), MM), jax.ShapeDtypeStruct((S, ATTN_WIDTH), MM)],
        in_specs=[cur(D_MODEL), prv(LANES), prv(LANES), prv(LANES), _const((1, D_MODEL)), _const((D_MODEL, PROJ_PAD)),
                  _const((1, Q_LORA)), _const((Q_LORA, QK_PAD)), _const((1, KV_LORA)), _const((KV_LORA, 2 * ATTN_WIDTH)),
                  _const((1, HEAD_PAD)), _const((1, HEAD_PAD))],
        out_specs=[cur(PROJ_PAD), prv(QK_PAD), prv(QK_PAD), prv(ATTN_WIDTH)],
        scratch_shapes=[pltpu.VMEM((T, LAT), f32), pltpu.VMEM((T, LAT), f32)],
        compiler_params=_cparams(("arbitrary",)),
    )(x, *tabs, g_in, w_in_p, g_cq, w_uq_p, g_ckv, w_ukv_p, g_q_p, g_k_p)


def _exchange(src_of, dst_ref, send_sems, recv_sems, local_sem):
    x, y, c = lax.axis_index("x"), lax.axis_index("y"), lax.axis_index("c")
    me = 4 * x + 2 * y + c
    remote = []
    for k in range(1, N_DEV):
        px = 1 - x if (k >> 2) & 1 else x
        py = 1 - y if (k >> 1) & 1 else y
        pc = 1 - c if k & 1 else c
        remote.append(pltpu.make_async_remote_copy(
            src_ref=src_of(4 * px + 2 * py + pc), dst_ref=dst_ref.at[me], send_sem=send_sems.at[k - 1],
            recv_sem=recv_sems.at[k - 1], device_id=(px, py, pc), device_id_type=MESH))
    return remote, pltpu.make_async_copy(src_of(me), dst_ref.at[me], local_sem)


def _exchange_start(copies):
    remote, local = copies
    local.start()
    for cp in remote:
        cp.start()


def _exchange_wait(copies):
    remote, local = copies
    for cp in remote:
        cp.wait_recv()
    for cp in remote:
        cp.wait_send()
    local.wait()


EXCHANGE_SEMS = [pltpu.SemaphoreType.DMA((N_DEV - 1,)), pltpu.SemaphoreType.DMA((N_DEV - 1,)), pltpu.SemaphoreType.DMA]


def _flash_fwd(q, k, v, block):
    S = q.shape[0]
    T = min(TQ, S)
    nq = S // T

    def body(q_ref, k_ref, v_ref, blk_ref, o_ref, lse_ref, all_ref, s_a, s_b, m_sc, l_sc, acc_sc, send_sems, recv_sems, local_sem):
        n = pl.program_id(1)
        gather = _exchange(lambda d: blk_ref, all_ref, send_sems, recv_sems, local_sem)

        @pl.when((pl.program_id(0) == 0) & (n == 0))
        def _():
            _exchange_start(gather)

        m_sc[...] = jnp.full(m_sc.shape, NEG_INF, f32)
        l_sc[...] = jnp.zeros(l_sc.shape, f32)
        acc_sc[...] = jnp.zeros(acc_sc.shape, f32)
        qv = q_ref[...]

        def rows(j):
            return pl.ds(pl.multiple_of(j * T, T), T)

        def scores(j, dst):
            dst[...] = _nt(k_ref[rows(j), :], qv)

        def update(src, j, masked):
            s = src[...]
            if masked:
                k_i = lax.broadcasted_iota(jnp.int32, (T, T), 0)
                q_i = lax.broadcasted_iota(jnp.int32, (T, T), 1)
                s = jnp.where(k_i <= q_i, s, NEG_INF)
            m_prev = m_sc[...]
            m_new = jnp.maximum(m_prev, jnp.max(s, axis=0, keepdims=True))
            alpha = jnp.exp2((m_prev - m_new) * SCALE_LOG2E)
            p = jnp.exp2((s - m_new[0:1, :]) * SCALE_LOG2E)
            l_sc[...] = alpha * l_sc[...] + jnp.sum(p, axis=0, keepdims=True)
            acc_sc[...] = alpha[0:1, :] * acc_sc[...] + _tn(v_ref[rows(j), :], p.astype(MM))
            m_sc[...] = m_new

        scores(0, s_a)

        def pair(t, carry):
            j = 2 * t
            scores(j + 1, s_b)
            update(s_a, j, False)
            scores(j + 2, s_a)
            update(s_b, j + 1, False)
            return carry

        lax.fori_loop(0, n // 2, pair, 0)

        @pl.when(n % 2 == 0)
        def _():
            update(s_a, n, True)

        @pl.when(n % 2 == 1)
        def _():
            scores(n, s_b)
            update(s_a, n - 1, False)
            update(s_b, n, True)

        o_ref[...] = (acc_sc[...] / l_sc[0:1, :]).T
        lse_ref[...] = m_sc[...] * SCALE_LOG2E + jnp.log2(l_sc[...])

        @pl.when((pl.program_id(0) == N_HEADS - 1) & (n == nq - 1))
        def _():
            _exchange_wait(gather)

    return pl.pallas_call(
        body, name="flash_fwd", grid=(N_HEADS, nq),
        out_shape=[jax.ShapeDtypeStruct((S, ATTN_WIDTH), f32), jax.ShapeDtypeStruct((N_HEADS * 8, S), f32),
                   jax.ShapeDtypeStruct((N_DEV,) + block.shape, block.dtype)],
        in_specs=[pl.BlockSpec((T, HEAD_PAD), lambda h, i: (i, h)),
                  pl.BlockSpec((S, HEAD_PAD), lambda h, i: (0, h)),
                  pl.BlockSpec((S, V_DIM), lambda h, i: (0, h)),
                  pl.BlockSpec(memory_space=pl.ANY)],
        out_specs=[pl.BlockSpec((T, V_DIM), lambda h, i: (i, h)), pl.BlockSpec((8, T), lambda h, i: (h, i)),
                   pl.BlockSpec(memory_space=pl.ANY)],
        scratch_shapes=[pltpu.VMEM((T, T), f32), pltpu.VMEM((T, T), f32), pltpu.VMEM((8, T), f32), pltpu.VMEM((8, T), f32),
                        pltpu.VMEM((V_DIM, T), f32)] + EXCHANGE_SEMS,
        compiler_params=_cparams(("arbitrary", "arbitrary")),
    )(q, k, v, block)


def _shift_down(m, prev8, n):
    T = m.shape[0]
    rows = lax.broadcasted_iota(jnp.int32, m.shape, 0)
    head = jnp.tile(pltpu.roll(prev8, n, 0), (T // 8, 1))
    return jnp.where(rows >= n, pltpu.roll(m, n, 0), head)


def _shift_up(m, next8, n):
    T = m.shape[0]
    rows = lax.broadcasted_iota(jnp.int32, m.shape, 0)
    tail = jnp.tile(pltpu.roll(next8, 8 - n, 0), (T // 8, 1))
    return jnp.where(rows < T - n, pltpu.roll(m, T - n, 0), tail)


def _mid(o, proj, x, p, target, w_o, w_plg, w_pl, conv_w8, g_oa, g_oc, g_pl):
    S = x.shape[0]
    T = min(TS_M, S)
    nt = S // T

    def body(o_ref, za_ref, ccx_ref, cbzc_ref, prev_ref, x_ref, p_ref, t_ref, wo_ref, wplg_ref, wpl_ref, cw_ref,
             goa_ref, goc_ref, gpl_ref,
             dx2_ref, do_ref, delta_ref, dza_ref, dcbzc_ref, du_ref,
             dwo_ref, dwplg_ref, dwpl_ref, dcw_ref, dgoa_ref, dgoc_ref, dgpl_ref, loss_ref):
        i = pl.program_id(0)

        @pl.when(i == 0)
        def _():
            for r in (dwo_ref, dwplg_ref, dwpl_ref, dcw_ref, dgoa_ref, dgoc_ref, dgpl_ref, loss_ref):
                r[...] = jnp.zeros(r.shape, f32)

        o_v = o_ref[...]
        z_a = za_ref[...]
        cc, cx = ccx_ref[:, :CONV_WIDTH], ccx_ref[:, CONV_WIDTH:]
        cb, z_c = cbzc_ref[:, :CONV_WIDTH], cbzc_ref[:, CONV_WIDTH:]
        g_oa, g_oc, g_pl = goa_ref[...], goc_ref[...], gpl_ref[...]
        w0, w1, w2 = cw_ref[0:1, :], cw_ref[1:2, :], cw_ref[2:3, :]

        sa = jax.nn.sigmoid(z_a)
        silu_a = z_a * sa
        ga = o_v * silu_a
        ra = _rstd(ga, ATTN_WIDTH)
        gha = ga * ra
        m0 = cc * cx
        prev = prev_ref[:, :CONV_WIDTH] * prev_ref[:, CONV_WIDTH:] * (i > 0).astype(f32)
        m1 = _shift_down(m0, prev, 1)
        m2 = _shift_down(m0, prev, 2)
        u = w0 * m2 + w1 * m1 + w2 * m0
        sc = jax.nn.sigmoid(z_c)
        silu_c = z_c * sc
        gc = cb * u * silu_c
        rc = _rstd(gc, CONV_WIDTH)
        ghc = gc * rc
        ycat = jnp.concatenate([gha * g_oa, ghc * g_oc], axis=-1).astype(MM)
        x2 = x_ref[...] + _nn(ycat, wo_ref[...])
        r2 = _rstd(x2, D_MODEL)
        xh2 = x2 * r2
        n2 = (xh2 * g_pl).astype(MM)
        gate = jax.nn.sigmoid(_nn(n2, wplg_ref[...]))
        p_b = p_ref[...].astype(MM)
        pw = _nn(p_b, wpl_ref[...])
        err = x2 + gate * pw - t_ref[...]
        loss_ref[...] += jnp.sum(jnp.sum(err * err, axis=-1, keepdims=True), axis=0, keepdims=True) * (0.5 / D_MODEL)
        d_out = err * (1.0 / D_MODEL)
        dwpl_ref[...] += _tn(p_b, (d_out * gate).astype(MM))
        d_glog = (d_out * pw * gate * (1.0 - gate)).astype(MM)
        dwplg_ref[...] += _tn(n2, d_glog)
        d_n2 = _nt(d_glog, wplg_ref[...])
        dgpl_ref[...] += jnp.sum(d_n2 * xh2, axis=0, keepdims=True)
        d_x2 = d_out + _rms_bwd(d_n2, xh2, r2, g_pl, D_MODEL)
        dx2_ref[...] = d_x2
        d_x2b = d_x2.astype(MM)
        dwo_ref[...] += _tn(ycat, d_x2b)
        d_ycat = _nt(d_x2b, wo_ref[...])
        d_ya, d_yc = d_ycat[:, :ATTN_WIDTH], d_ycat[:, ATTN_WIDTH:]
        dgoa_ref[...] += jnp.sum(d_ya * gha, axis=0, keepdims=True)
        dgoc_ref[...] += jnp.sum(d_yc * ghc, axis=0, keepdims=True)
        d_ga = _rms_bwd(d_ya, gha, ra, g_oa, ATTN_WIDTH)
        d_gc = _rms_bwd(d_yc, ghc, rc, g_oc, CONV_WIDTH)
        d_o = d_ga * silu_a
        do_ref[...] = d_o.astype(MM)
        dza_ref[...] = (d_ga * o_v * (sa * (1.0 + z_a * (1.0 - sa)))).astype(MM)
        for hd in range(N_HEADS):
            cols = slice(hd * V_DIM, (hd + 1) * V_DIM)
            dl = jnp.sum(d_o[:, cols] * o_v[:, cols], axis=-1, keepdims=True)
            delta_ref[hd * 8:(hd + 1) * 8, :] = jnp.broadcast_to(dl, (T, LANES)).T[0:8, :]
        d_u = d_gc * cb * silu_c
        du_ref[...] = d_u
        dcbzc_ref[:, :CONV_WIDTH] = (d_gc * u * silu_c).astype(MM)
        dcbzc_ref[:, CONV_WIDTH:] = (d_gc * cb * u * (sc * (1.0 + z_c * (1.0 - sc)))).astype(MM)
        dcw_ref[0:1, :] += jnp.sum(d_u * m2, axis=0, keepdims=True)
        dcw_ref[1:2, :] += jnp.sum(d_u * m1, axis=0, keepdims=True)
        dcw_ref[2:3, :] += jnp.sum(d_u * m0, axis=0, keepdims=True)

    row = lambda w, blk=0: pl.BlockSpec((T, w), lambda i: (i, blk))
    acc = lambda shape: pl.BlockSpec(shape, lambda i: (0, 0))
    tb = T // 8
    sds = jax.ShapeDtypeStruct
    return pl.pallas_call(
        body, name="mid", grid=(nt,),
        out_shape=[sds((S, D_MODEL), f32), sds((S, ATTN_WIDTH), MM), sds((N_HEADS * 8, S), f32), sds((S, ATTN_WIDTH), MM),
                   sds((S, 2 * CONV_WIDTH), MM), sds((S, CONV_WIDTH), f32),
                   sds((D_MODEL, D_MODEL), f32), sds((D_MODEL, D_MODEL), f32), sds((PLE_DIM, D_MODEL), f32), sds((8, CONV_WIDTH), f32),
                   sds((1, ATTN_WIDTH), f32), sds((1, CONV_WIDTH), f32), sds((1, D_MODEL), f32), sds((1, LANES), f32)],
        in_specs=[row(ATTN_WIDTH), row(ATTN_WIDTH, 1), row(2 * CONV_WIDTH, 1), row(2 * CONV_WIDTH, 2),
                  pl.BlockSpec((8, 2 * CONV_WIDTH), lambda i: (jnp.maximum(i * tb - 1, 0), 1)),
                  row(D_MODEL), row(PLE_DIM), row(D_MODEL),
                  _const((D_MODEL, D_MODEL)), _const((D_MODEL, D_MODEL)), _const((PLE_DIM, D_MODEL)), _const((8, CONV_WIDTH)),
                  _const((1, ATTN_WIDTH)), _const((1, CONV_WIDTH)), _const((1, D_MODEL))],
        out_specs=[row(D_MODEL), row(ATTN_WIDTH), pl.BlockSpec((N_HEADS * 8, T), lambda i: (0, i)), row(ATTN_WIDTH),
                   row(2 * CONV_WIDTH), row(CONV_WIDTH),
                   acc((D_MODEL, D_MODEL)), acc((D_MODEL, D_MODEL)), acc((PLE_DIM, D_MODEL)), acc((8, CONV_WIDTH)),
                   acc((1, ATTN_WIDTH)), acc((1, CONV_WIDTH)), acc((1, D_MODEL)), acc((1, LANES))],
        compiler_params=_cparams(("arbitrary",)),
    )(o, proj, proj, proj, proj, x, p, target, w_o, w_plg, w_pl, conv_w8, g_oa, g_oc, g_pl)


def _flash_bwd(q, k, v, d_o, lse2, delta, by_target):
    S = q.shape[0]
    T = min(TQ, S)
    nq = S // T

    def body(k_ref, v_ref, q_ref, do_ref, lse_ref, delta_ref, out_ref, dq_ref, dk_ref, dv_ref, in_ref, s_a, dp_a, s_b, dp_b,
             send_sems, recv_sems, local_sem):
        j = pl.program_id(1)
        scatter = _exchange(lambda d: out_ref.at[d], in_ref, send_sems, recv_sems, local_sem)

        @pl.when((pl.program_id(0) == 0) & (j == 0))
        def _():
            _exchange_start(scatter)

        @pl.when(j == 0)
        def _():
            dq_ref[...] = jnp.zeros(dq_ref.shape, f32)

        dk_ref[...] = jnp.zeros(dk_ref.shape, f32)
        dv_ref[...] = jnp.zeros(dv_ref.shape, f32)
        kv, vv = k_ref[...], v_ref[...]

        def rows_of(i):
            return pl.ds(pl.multiple_of(i * T, T), T)

        def scores(i, s_dst, dp_dst):
            rows = rows_of(i)
            s_dst[...] = _nt(kv, q_ref[rows, :])
            dp_dst[...] = _nt(vv, do_ref[rows, :])

        def grads(i, s_src, dp_src, masked):
            rows = rows_of(i)
            s = s_src[...]
            if masked:
                k_i = lax.broadcasted_iota(jnp.int32, (T, T), 0)
                q_i = lax.broadcasted_iota(jnp.int32, (T, T), 1)
                s = jnp.where(k_i <= q_i, s, NEG_INF)
            p = jnp.exp2(s * SCALE_LOG2E - lse_ref[0:1, rows])
            ds = (p * (dp_src[...] - delta_ref[0:1, rows])).astype(MM)
            dv_ref[...] += _nn(p.astype(MM), do_ref[rows, :])
            dk_ref[...] += _nn(ds, q_ref[rows, :])
            dq_ref[rows, :] += _tn(ds, kv)

        last = nq - 1
        scores(j, s_a, dp_a)
        scores(jnp.minimum(j + 1, last), s_b, dp_b)
        grads(j, s_a, dp_a, True)
        rest = last - j

        def pair(t, carry):
            u = j + 1 + 2 * t
            scores(u + 1, s_a, dp_a)
            grads(u, s_b, dp_b, False)
            scores(jnp.minimum(u + 2, last), s_b, dp_b)
            grads(u + 1, s_a, dp_a, False)
            return carry

        lax.fori_loop(0, rest // 2, pair, 0)

        @pl.when(rest % 2 == 1)
        def _():
            grads(last, s_b, dp_b, False)

        @pl.when((pl.program_id(0) == N_HEADS - 1) & (j == nq - 1))
        def _():
            _exchange_wait(scatter)

    whole = lambda w: pl.BlockSpec((S, w), lambda h, j: (0, h))
    stat = pl.BlockSpec((8, S), lambda h, j: (h, 0))
    return pl.pallas_call(
        body, name="flash_bwd", grid=(N_HEADS, nq),
        out_shape=[jax.ShapeDtypeStruct((S, QK_PAD), f32), jax.ShapeDtypeStruct((S, QK_PAD), f32),
                   jax.ShapeDtypeStruct((S, ATTN_WIDTH), f32), jax.ShapeDtypeStruct(by_target.shape, by_target.dtype)],
        in_specs=[pl.BlockSpec((T, HEAD_PAD), lambda h, j: (j, h)), pl.BlockSpec((T, V_DIM), lambda h, j: (j, h)),
                  whole(HEAD_PAD), whole(V_DIM), stat, stat, pl.BlockSpec(memory_space=pl.ANY)],
        out_specs=[whole(HEAD_PAD), pl.BlockSpec((T, HEAD_PAD), lambda h, j: (j, h)),
                   pl.BlockSpec((T, V_DIM), lambda h, j: (j, h)), pl.BlockSpec(memory_space=pl.ANY)],
        scratch_shapes=[pltpu.VMEM((T, T), f32)] * 4 + EXCHANGE_SEMS,
        compiler_params=_cparams(("arbitrary", "arbitrary")),
    )(k, v, q, d_o, lse2, delta, by_target)


def _bwd_proj(x, proj, d_q, d_k, d_v, d_za, d_cbzc, d_u, d_x2, tabs, g_in, w_in_p, g_cq, w_uq_p, g_ckv, w_ukv_p,
              g_q_p, g_k_p, conv_w8):
    S = x.shape[0]
    T = min(TS_E, S)
    nt = S // T

    def body(x_ref, cqkv_ref, ccx_ref, dq_ref, dk_ref, dv_ref, dza_ref, dcbzc_ref, du_ref, dun_ref, dx2_ref,
             c_ref, s1_ref, s2_ref, g_in_ref, w_in_ref, g_cq_ref, w_uq_ref, g_ckv_ref, w_ukv_ref, g_q_ref, g_k_ref, cw_ref,
             gx_ref, dwin_hbm, dwuq_ref, dwukv_ref, dgin_ref, dgcq_ref, dgckv_ref, dgq_ref, dgk_ref,
             dproj_a, dproj_b, dqraw_sc, dkvraw_sc, dwin_sc, sem):
        i = pl.program_id(0)

        @pl.when(i == 0)
        def _():
            dwin_sc[...] = jnp.zeros(dwin_sc.shape, f32)
            dproj_b[...] = jnp.zeros(dproj_b.shape, MM)
            for r in (dwuq_ref, dwukv_ref, dgin_ref, dgcq_ref, dgckv_ref, dgq_ref, dgk_ref):
                r[...] = jnp.zeros(r.shape, f32)

        def stage1(dproj_sc):
            live = (i < nt).astype(f32)
            c, s1, s2 = c_ref[...], s1_ref[...], s2_ref[...]
            g_q, g_k = g_q_ref[...], g_k_ref[...]
            g_cq, g_ckv = g_cq_ref[...], g_ckv_ref[...]

            c_q = cqkv_ref[:, 0:Q_LORA]
            r_cq = _rstd(c_q, Q_LORA)
            cqh = c_q * r_cq
            n_cq = (cqh * g_cq).astype(MM)
            q_raw = _nn(n_cq, w_uq_ref[...])
            dgq = jnp.zeros((1, HEAD_PAD), f32)
            for hd in range(N_HEADS):
                cols = slice(hd * HEAD_PAD, (hd + 1) * HEAD_PAD)
                qh = q_raw[:, cols]
                rq = _rstd(qh, QK_DIM)
                qhh = qh * rq
                dqh = dq_ref[:, cols] * QK_SCALE
                d_qn = jnp.concatenate([dqh[:, :LANES], _rope_t(dqh[:, LANES:], c, s1, s2)], axis=-1)
                dgq += jnp.sum(d_qn * qhh, axis=0, keepdims=True)
                dqraw_sc[:, cols] = _rms_bwd(d_qn, qhh, rq, g_q, QK_DIM).astype(MM)
            dgq_ref[...] += dgq * live
            d_qraw = dqraw_sc[...]
            dwuq_ref[...] += _tn((cqh * (g_cq * live)).astype(MM), d_qraw)
            d_ncq = _nt(d_qraw, w_uq_ref[...])
            dgcq_ref[...] += jnp.sum(d_ncq * cqh, axis=0, keepdims=True) * live
            dproj_sc[:, 0:Q_LORA] = _rms_bwd(d_ncq, cqh, r_cq, g_cq, Q_LORA).astype(MM)

            c_kv = cqkv_ref[:, Q_LORA:Q_LORA + KV_LORA]
            r_ckv = _rstd(c_kv, KV_LORA)
            ckvh = c_kv * r_ckv
            n_ckv = (ckvh * g_ckv).astype(MM)
            k_nope = _nn(n_ckv, w_ukv_ref[:, :N_HEADS * NOPE_DIM])
            kpe = cqkv_ref[:, Q_LORA + KV_LORA:Q_LORA + KV_LORA + LANES]
            ss_pe = jnp.sum(kpe * kpe, axis=-1, keepdims=True)
            g_kn, g_kp = g_k[:, :LANES], g_k[:, LANES:]
            d_kpe = jnp.zeros((T, LANES), f32)
            dgk_n = jnp.zeros((1, LANES), f32)
            dgk_p = jnp.zeros((1, LANES), f32)
            for hd in range(N_HEADS):
                kn = k_nope[:, hd * LANES:(hd + 1) * LANES]
                rk = lax.rsqrt((jnp.sum(kn * kn, axis=-1, keepdims=True) + ss_pe) * (1.0 / QK_DIM) + RMS_EPS)
                knh, kph = kn * rk, kpe * rk
                d_kn_n = dk_ref[:, hd * HEAD_PAD:hd * HEAD_PAD + LANES] * QK_SCALE
                d_kr = _rope_t(dk_ref[:, hd * HEAD_PAD + LANES:(hd + 1) * HEAD_PAD] * QK_SCALE, c, s1, s2)
                dgk_n += jnp.sum(d_kn_n * knh, axis=0, keepdims=True)
                dgk_p += jnp.sum(d_kr * kph, axis=0, keepdims=True)
                u_n, u_p = d_kn_n * g_kn, d_kr * g_kp
                mt = (jnp.sum(u_n * knh, axis=-1, keepdims=True) + jnp.sum(u_p * kph, axis=-1, keepdims=True)) * (1.0 / QK_DIM)
                dkvraw_sc[:, hd * LANES:(hd + 1) * LANES] = (rk * (u_n - knh * mt)).astype(MM)
                d_kpe += rk * (u_p - kph * mt)
            dgk_ref[:, :LANES] += dgk_n * live
            dgk_ref[:, LANES:] += dgk_p * live
            dkvraw_sc[:, N_HEADS * NOPE_DIM:] = dv_ref[...].astype(MM)
            d_kvraw = dkvraw_sc[...]
            dwukv_ref[...] += _tn((ckvh * (g_ckv * live)).astype(MM), d_kvraw)
            d_nckv = _nt(d_kvraw, w_ukv_ref[...])
            dgckv_ref[...] += jnp.sum(d_nckv * ckvh, axis=0, keepdims=True) * live
            dproj_sc[:, Q_LORA:Q_LORA + KV_LORA] = _rms_bwd(d_nckv, ckvh, r_ckv, g_ckv, KV_LORA).astype(MM)
            dproj_sc[:, Q_LORA + KV_LORA:Q_LORA + KV_LORA + LANES] = d_kpe.astype(MM)

            dproj_sc[:, 512:1024] = dza_ref[...]
            d_u = du_ref[...]
            nxt = dun_ref[...] * (i < nt - 1).astype(f32)
            d_m = cw_ref[2:3, :] * d_u + cw_ref[1:2, :] * _shift_up(d_u, nxt, 1) + cw_ref[0:1, :] * _shift_up(d_u, nxt, 2)
            dproj_sc[:, 1024:1536] = (d_m * ccx_ref[:, CONV_WIDTH:]).astype(MM)
            dproj_sc[:, 1536:2048] = (d_m * ccx_ref[:, :CONV_WIDTH]).astype(MM)
            dproj_sc[:, 2048:3072] = dcbzc_ref[...]

        def stage2(dproj_sc):
            g_in = g_in_ref[...]
            xv = x_ref[...]
            r_in = _rstd(xv, D_MODEL)
            xh = xv * r_in
            hb = (xh * g_in).astype(MM)
            d_proj = dproj_sc[...]
            dwin_sc[...] += _tn(hb, d_proj)
            d_h = _nt(d_proj, w_in_ref[...])
            dgin_ref[...] += jnp.sum(d_h * xh, axis=0, keepdims=True)
            gx_ref[...] = dx2_ref[...] + _rms_bwd(d_h, xh, r_in, g_in, D_MODEL)

        @pl.when(i % 2 == 0)
        def _():
            stage2(dproj_b)
            stage1(dproj_a)

        @pl.when(i % 2 == 1)
        def _():
            stage2(dproj_a)
            stage1(dproj_b)

        @pl.when(i == nt)
        def _():
            cp = pltpu.make_async_copy(dwin_sc, dwin_hbm, sem)
            cp.start()
            cp.wait()

    cur = lambda i: jnp.minimum(i, nt - 1)
    prv = lambda i: jnp.maximum(i - 1, 0)
    row = lambda w, blk=0: pl.BlockSpec((T, w), lambda i: (cur(i), blk))
    row2 = lambda w: pl.BlockSpec((T, w), lambda i: (prv(i), 0))
    acc = lambda shape: pl.BlockSpec(shape, lambda i: (0, 0))
    tb = T // 8
    sds = jax.ShapeDtypeStruct
    return pl.pallas_call(
        body, name="bwd_proj", grid=(nt + 1,),
        out_shape=[sds((S, D_MODEL), f32), sds((D_MODEL, PROJ_PAD), f32), sds((Q_LORA, QK_PAD), f32),
                   sds((KV_LORA, 2 * ATTN_WIDTH), f32), sds((1, D_MODEL), f32), sds((1, Q_LORA), f32), sds((1, KV_LORA), f32),
                   sds((1, HEAD_PAD), f32), sds((1, HEAD_PAD), f32)],
        in_specs=[row2(D_MODEL), row(512, 0), row(2 * CONV_WIDTH, 1), row(QK_PAD), row(QK_PAD), row(ATTN_WIDTH),
                  row(ATTN_WIDTH), row(2 * CONV_WIDTH), row(CONV_WIDTH),
                  pl.BlockSpec((8, CONV_WIDTH), lambda i: (jnp.minimum((cur(i) + 1) * tb, S // 8 - 1), 0)),
                  row2(D_MODEL), row(LANES), row(LANES), row(LANES),
                  _const((1, D_MODEL)), _const((D_MODEL, PROJ_PAD)), _const((1, Q_LORA)), _const((Q_LORA, QK_PAD)),
                  _const((1, KV_LORA)), _const((KV_LORA, 2 * ATTN_WIDTH)), _const((1, HEAD_PAD)), _const((1, HEAD_PAD)),
                  _const((8, CONV_WIDTH))],
        out_specs=[row2(D_MODEL), pl.BlockSpec(memory_space=pl.ANY), acc((Q_LORA, QK_PAD)), acc((KV_LORA, 2 * ATTN_WIDTH)),
                   acc((1, D_MODEL)), acc((1, Q_LORA)), acc((1, KV_LORA)), acc((1, HEAD_PAD)), acc((1, HEAD_PAD))],
        scratch_shapes=[pltpu.VMEM((T, PROJ_PAD), MM), pltpu.VMEM((T, PROJ_PAD), MM), pltpu.VMEM((T, QK_PAD), MM),
                        pltpu.VMEM((T, 2 * ATTN_WIDTH), MM), pltpu.VMEM((D_MODEL, PROJ_PAD), f32), pltpu.SemaphoreType.DMA],
        compiler_params=_cparams(("arbitrary",)),
    )(x, proj, proj, d_q, d_k, d_v, d_za, d_cbzc, d_u, d_u, d_x2, *tabs, g_in, w_in_p, g_cq, w_uq_p, g_ckv, w_ukv_p,
      g_q_p, g_k_p, conv_w8)


def _all_gather(block, name, in_vmem):
    m_per, n = block.shape

    def body(x_ref, out_ref, send_sems, recv_sems, local_sem):
        x, y, c = lax.axis_index("x"), lax.axis_index("y"), lax.axis_index("c")
        me, sibling = (x, y, c), (x, y, 1 - c)
        chips = [(1 - x, y), (x, 1 - y), (1 - x, 1 - y)]

        def rows(px, py, pc):
            return out_ref.at[pl.ds((4 * px + 2 * py + pc) * m_per, m_per), :]

        def copy(k, blk, to, src=None):
            return pltpu.make_async_remote_copy(
                src_ref=rows(*blk) if src is None else src, dst_ref=rows(*blk),
                send_sem=send_sems.at[k], recv_sem=recv_sems.at[k], device_id=to, device_id_type=MESH)

        mine = pltpu.make_async_copy(x_ref, rows(*me), local_sem)
        mine.start()
        first = [copy(0, me, sibling, src=x_ref)]
        first += [copy(1 + j, me, (*chip, c), src=x_ref) for j, chip in enumerate(chips)]
        for cp in first:
            cp.start()
        passed = [copy(4 + j, (*chip, c), sibling) for j, chip in enumerate(chips)]
        for j, chip in enumerate(chips):
            copy(1 + j, (*chip, c), me).wait_recv()
            passed[j].start()
        copy(0, sibling, me).wait_recv()
        for j, chip in enumerate(chips):
            copy(4 + j, (*chip, 1 - c), me).wait_recv()
        for cp in first + passed:
            cp.wait_send()
        mine.wait()

    space = pltpu.VMEM if in_vmem else pl.ANY
    return pl.pallas_call(
        body, name=name, out_shape=jax.ShapeDtypeStruct((N_DEV * m_per, n), block.dtype),
        in_specs=[pl.BlockSpec(memory_space=space)], out_specs=pl.BlockSpec(memory_space=space),
        scratch_shapes=[pltpu.SemaphoreType.DMA((7,)), pltpu.SemaphoreType.DMA((7,)), pltpu.SemaphoreType.DMA],
    )(block)


def _sibling_exchange(pack):
    nchip, _, R, n = pack.shape

    def body(p_ref, got_ref, send_sems, recv_sems):
        x, y, c = lax.axis_index("x"), lax.axis_index("y"), lax.axis_index("c")
        copies = [pltpu.make_async_remote_copy(src_ref=p_ref.at[t, 1 - c], dst_ref=got_ref.at[t], send_sem=send_sems.at[t],
                                               recv_sem=recv_sems.at[t], device_id=(x, y, 1 - c), device_id_type=MESH)
                  for t in range(nchip)]
        for cp in copies:
            cp.start()
        for cp in copies:
            cp.wait()

    return pl.pallas_call(
        body, name="rs_sibling", out_shape=jax.ShapeDtypeStruct((nchip, R, n), pack.dtype),
        in_specs=[pl.BlockSpec(memory_space=pl.ANY)], out_specs=pl.BlockSpec(memory_space=pl.ANY),
        scratch_shapes=[pltpu.SemaphoreType.DMA((nchip,)), pltpu.SemaphoreType.DMA((nchip,))],
    )(pack)


def _chip_exchange(part):
    nchip, R, n = part.shape

    def body(p_ref, got_ref, send_sems, recv_sems, local_sem):
        x, y, c = lax.axis_index("x"), lax.axis_index("y"), lax.axis_index("c")
        my_chip = 2 * x + y
        chips = [(1 - x, y), (x, 1 - y), (1 - x, 1 - y)]
        mine = pltpu.make_async_copy(p_ref.at[my_chip], got_ref.at[my_chip], local_sem)
        mine.start()
        sends = []
        for k, (tx, ty) in enumerate(chips):
            cp = pltpu.make_async_remote_copy(src_ref=p_ref.at[2 * tx + ty], dst_ref=got_ref.at[my_chip],
                                              send_sem=send_sems.at[k], recv_sem=recv_sems.at[k],
                                              device_id=(tx, ty, c), device_id_type=MESH)
            cp.start()
            sends.append(cp)
        for k, (sx, sy) in enumerate(chips):
            pltpu.make_async_remote_copy(src_ref=p_ref.at[my_chip], dst_ref=got_ref.at[2 * sx + sy],
                                         send_sem=send_sems.at[k], recv_sem=recv_sems.at[k],
                                         device_id=(sx, sy, c), device_id_type=MESH).wait_recv()
        for cp in sends:
            cp.wait_send()
        mine.wait()

    return pl.pallas_call(
        body, name="rs_chips", out_shape=jax.ShapeDtypeStruct((nchip, R, n), part.dtype),
        in_specs=[pl.BlockSpec(memory_space=pl.ANY)], out_specs=pl.BlockSpec(memory_space=pl.ANY),
        scratch_shapes=[pltpu.SemaphoreType.DMA((3,)), pltpu.SemaphoreType.DMA((3,)), pltpu.SemaphoreType.DMA],
    )(part)


def _add_blocks(pack, from_sib, my_c):
    nb, _, R, n = pack.shape

    def body(c_ref, a_ref, b_ref, o_ref):
        o_ref[...] = (a_ref[0] + b_ref[...]).astype(o_ref.dtype)

    return pl.pallas_call(
        body, name="rs_add", out_shape=jax.ShapeDtypeStruct((nb, R, n), RS_DTYPE),
        grid_spec=pltpu.PrefetchScalarGridSpec(
            num_scalar_prefetch=1, grid=(nb,),
            in_specs=[pl.BlockSpec((1, 1, R, n), lambda i, c: (i, c[0], 0, 0)), pl.BlockSpec((1, R, n), lambda i, c: (i, 0, 0))],
            out_specs=pl.BlockSpec((1, R, n), lambda i, c: (i, 0, 0))),
        compiler_params=_cparams(("parallel",)))(my_c, pack, from_sib)


def _sum_slabs(got, name):
    nb, R, n = got.shape

    def body(g_ref, o_ref):
        tot = g_ref[0].astype(f32)
        for d in range(1, nb):
            tot = tot + g_ref[d].astype(f32)
        o_ref[...] = tot

    return pl.pallas_call(body, name=name, grid=(n // LANES,), out_shape=jax.ShapeDtypeStruct((R, n), f32),
                          in_specs=[pl.BlockSpec((nb, R, LANES), lambda j: (0, 0, j))], out_specs=pl.BlockSpec((R, LANES), lambda j: (0, j)),
                          compiler_params=_cparams(("parallel",)))(got)


def _adamw_all(ws, gs, ms, vs, gain_parts):
    n = len(ws)
    ng = len(GAIN_SLOTS)

    def body(*refs):
        gp_ref = refs[0]
        w_refs = refs[1:1 + n]
        g_refs = refs[1 + n:1 + 2 * n - ng]
        m_refs = refs[1 + 2 * n - ng:1 + 3 * n - ng]
        v_refs = refs[1 + 3 * n - ng:1 + 4 * n - ng]
        outs = refs[1 + 4 * n - ng:]
        gsum_ref, loss_ref = outs[0], outs[1]
        gg_refs = outs[2:2 + ng]
        d_refs, nm_refs, nv_refs = (outs[2 + ng + k * n:2 + ng + (k + 1) * n] for k in range(3))
        tot = gp_ref[0]
        for d in range(1, N_DEV):
            tot = tot + gp_ref[d]
        gsum_ref[...] = tot
        loss_ref[...] = gsum_ref[GAIN_ROWS - 1:GAIN_ROWS, :]
        for k in range(n):
            if k < ng:
                r0, width = GAIN_SLOTS[k]
                for r in range(width // LANES):
                    gg_refs[k][:, r * LANES:(r + 1) * LANES] = gsum_ref[r0 + r:r0 + r + 1, :]
                g = gg_refs[k][...]
            else:
                g = g_refs[k - ng][...]
            w = w_refs[k][...]
            m = ADAM_B1 * m_refs[k][...] + (1.0 - ADAM_B1) * g
            v = ADAM_B2 * v_refs[k][...] + (1.0 - ADAM_B2) * (g * g)
            m_hat = m / (1.0 - ADAM_B1 ** ADAM_STEP)
            v_hat = v / (1.0 - ADAM_B2 ** ADAM_STEP)
            d_refs[k][...] = -ADAM_LR * (m_hat / (jnp.sqrt(v_hat) + ADAM_EPS) + ADAM_WD * w)
            nm_refs[k][...] = m
            nv_refs[k][...] = v

    sds = jax.ShapeDtypeStruct
    like = [sds(w.shape, f32) for w in ws]
    out_shape = [sds((GAIN_ROWS, LANES), f32), sds((1, LANES), f32)] + like[:ng] + like * 3
    full = lambda a: pl.BlockSpec(a.shape, lambda i: (0,) * len(a.shape))
    operands = [gain_parts, *ws, *gs, *ms, *vs]
    outs = pl.pallas_call(
        body, name="adamw", grid=(1,), out_shape=out_shape, in_specs=[full(a) for a in operands],
        out_specs=[full(a) for a in out_shape], compiler_params=_cparams(("arbitrary",)),
    )(*operands)
    loss = outs[1]
    gg = outs[2:2 + ng]
    deltas, new_m, new_v = (outs[2 + ng + k * n:2 + ng + (k + 1) * n] for k in range(3))
    return loss, gg, deltas, new_m, new_v


GAIN_ROWS = 32
GAIN_SLOTS = [(0, 1024), (8, 256), (10, 128), (11, 256), (13, 256), (15, 512), (19, 512), (23, 1024)]

SH_IN = IN_TOTAL // N_DEV
SLAB1_COLS = 384
UQ_ROWS = Q_LORA * (N_HEADS * QK_DIM // N_DEV) // SLAB1_COLS
UKV_ROWS = 48
SLAB1_ROWS = D_MODEL + UQ_ROWS + UKV_ROWS
SH_ROWS = D_MODEL // N_DEV
PL_ROWS = PLE_DIM * (D_MODEL // N_DEV) // D_MODEL
CONV_ROWS = 16
SLAB2_ROWS = 2 * SH_ROWS + PL_ROWS + CONV_ROWS
IN_SEGMENTS = [(0, 448, 0), (448, 960, 512), (1472, 2496, 1024), (960, 1472, 2048), (2496, 3008, 2560)]


def _flat_rows(a, rows, cols):
    lead = a.shape[:-2]
    flat = a.reshape(lead + (-1,))
    pad = [(0, 0)] * len(lead) + [(0, rows * cols - flat.shape[-1])]
    return jnp.pad(flat, pad).reshape(lead + (rows, cols))


def _slab1(in_sh, uq_sh, ukv_sh):
    pad = [(0, 0)] * (in_sh.ndim - 1) + [(0, SLAB1_COLS - in_sh.shape[-1])]
    return jnp.concatenate([jnp.pad(in_sh, pad), _flat_rows(uq_sh, UQ_ROWS, SLAB1_COLS), _flat_rows(ukv_sh, UKV_ROWS, SLAB1_COLS)],
                           axis=-2)


def _slab2(o_sh, plg_sh, pl_sh, conv_sh):
    return jnp.concatenate([o_sh, plg_sh, _flat_rows(pl_sh, PL_ROWS, D_MODEL), _flat_rows(conv_sh, CONV_ROWS, D_MODEL)], axis=-2)


def _by_device(a, width):
    return a.reshape(a.shape[0], N_DEV, width).transpose(1, 0, 2)


def _from_devices(a):
    return a.transpose(1, 0, 2).reshape(a.shape[1], -1)


def kernel(x, p, positions, g_in, w_in, g_cq, w_uq, g_ckv, w_ukv, g_q, g_k, conv_w, g_oa, g_oc, w_o, w_pl, w_plg, g_pl, loss_target, m_g_in, m_w_in, m_g_cq, m_w_uq, m_g_ckv, m_w_ukv, m_g_q, m_g_k, m_conv_w, m_g_oa, m_g_oc, m_w_o, m_w_pl, m_w_plg, m_g_pl, v_g_in, v_w_in, v_g_cq, v_w_uq, v_g_ckv, v_w_ukv, v_g_q, v_g_k, v_conv_w, v_g_oa, v_g_oc, v_w_o, v_w_pl, v_w_plg, v_g_pl):
    S = x.shape[1]
    nd = N_DEV
    xs, ps, tgt = x[0], p[0, 0], loss_target[0]

    slab1 = _slab1(w_in[0], w_uq[0], w_ukv[0]).astype(bf16)
    g1 = _all_gather(slab1, "ag_weights", in_vmem=False).reshape(nd, SLAB1_ROWS, SLAB1_COLS)
    conv_bits = lax.bitcast_convert_type(conv_w[0], bf16).reshape(CONV_K, -1)
    slab2 = _slab2(w_o[0].astype(bf16), w_plg[0].astype(bf16), w_pl[0].astype(bf16), conv_bits)

    pieces, pos = [], 0
    for a, b, start in sorted(IN_SEGMENTS, key=lambda s: s[2]):
        if start > pos:
            pieces.append(jnp.zeros((D_MODEL, start - pos), bf16))
        pieces += [g1[d, :D_MODEL, max(a - SH_IN * d, 0):min(b - SH_IN * d, SH_IN)] for d in range(a // SH_IN, (b - 1) // SH_IN + 1)]
        pos = start + b - a
    w_in_p = jnp.concatenate(pieces, axis=1)
    wuq = _from_devices(g1[:, D_MODEL:D_MODEL + UQ_ROWS].reshape(nd, Q_LORA, -1)).reshape(Q_LORA, N_HEADS, QK_DIM)
    w_uq_p = jnp.pad(wuq, ((0, 0), (0, 0), (0, HEAD_PAD - QK_DIM))).reshape(Q_LORA, QK_PAD)
    ukv_flat = g1[:, D_MODEL + UQ_ROWS:].reshape(nd, -1)[:, :KV_LORA * V_DIM]
    wukv = _from_devices(ukv_flat.reshape(nd, KV_LORA, V_DIM)).reshape(KV_LORA, N_HEADS, 2, NOPE_DIM)
    w_ukv_p = wukv.transpose(0, 2, 1, 3).reshape(KV_LORA, 2 * ATTN_WIDTH)
    w_in_p, w_uq_p, w_ukv_p = (w.astype(MM) for w in (w_in_p, w_uq_p, w_ukv_p))

    g_q_p = jnp.pad(g_q, ((0, 0), (0, HEAD_PAD - QK_DIM)))
    g_k_p = jnp.pad(g_k, ((0, 0), (0, HEAD_PAD - QK_DIM)))

    inv_freq = 1.0 / (ROPE_THETA ** (jnp.arange(0, ROPE_DIM, 2, dtype=f32) / ROPE_DIM))
    inv_row = jnp.concatenate([inv_freq, inv_freq, jnp.zeros((LANES - ROPE_DIM,), f32)]).reshape(1, LANES)
    posf = jnp.broadcast_to(positions[0].astype(f32)[:, None], (S, LANES))
    tabs = _rope_tables(posf, inv_row)

    proj, q, k, v = _fwd_proj(xs, tabs, g_in, w_in_p, g_cq, w_uq_p, g_ckv, w_ukv_p, g_q_p, g_k_p)
    o, lse, g2 = _flash_fwd(q, k, v, slab2)
    w_o_f = g2[:, 0:SH_ROWS].reshape(D_MODEL, D_MODEL).astype(MM)
    w_plg_f = g2[:, SH_ROWS:2 * SH_ROWS].reshape(D_MODEL, D_MODEL).astype(MM)
    w_pl_f = _from_devices(g2[:, 2 * SH_ROWS:2 * SH_ROWS + PL_ROWS].reshape(nd, PLE_DIM, -1)).astype(MM)
    cv_bits = g2[:, 2 * SH_ROWS + PL_ROWS:].reshape(nd, -1)[:, :CONV_K * (CONV_WIDTH // nd) * 2]
    conv_full = _from_devices(lax.bitcast_convert_type(cv_bits.reshape(nd, CONV_K, CONV_WIDTH // nd, 2), f32))
    conv_w8 = jnp.pad(conv_full, ((0, 8 - CONV_K), (0, 0)))
    (d_x2, d_o, delta, d_za, d_cbzc, d_u, dw_o, dw_plg, dw_pl, dcw, dg_oa, dg_oc, dg_pl, loss_part) = _mid(
        o, proj, xs, ps, tgt, w_o_f, w_plg_f, w_pl_f, conv_w8, g_oa, g_oc, g_pl)

    slabs_a = _slab2(dw_o.reshape(nd, SH_ROWS, D_MODEL), dw_plg.reshape(nd, SH_ROWS, D_MODEL), _by_device(dw_pl, D_MODEL // nd),
                     _by_device(dcw[:CONV_K], CONV_WIDTH // nd)).astype(RS_DTYPE)
    d_q, d_k, d_v, got_a = _flash_bwd(q, k, v, d_o, lse, delta, slabs_a)
    shard_a = _sum_slabs(got_a, "rs_sum8")
    (grad_x, dw_in_p, dw_uq_p, dw_ukv_p, dg_in, dg_cq, dg_ckv, dg_q_p, dg_k_p) = _bwd_proj(
        xs, proj, d_q, d_k, d_v, d_za, d_cbzc, d_u, d_x2, tabs, g_in, w_in_p, g_cq, w_uq_p, g_ckv, w_ukv_p, g_q_p, g_k_p, conv_w8)

    def dw_in_cols(a, b):
        return [dw_in_p[:, start + max(a, s) - s:start + min(b, e) - s] for s, e, start in sorted(IN_SEGMENTS) if max(a, s) < min(b, e)]

    no_cols = jnp.zeros((D_MODEL, SLAB1_COLS - SH_IN), f32)
    in_shards = jnp.stack([jnp.concatenate(dw_in_cols(SH_IN * d, SH_IN * (d + 1)) + [no_cols], axis=1) for d in range(nd)])
    dw_uq = dw_uq_p.reshape(Q_LORA, N_HEADS, HEAD_PAD)[:, :, :QK_DIM].reshape(Q_LORA, N_HEADS * QK_DIM)
    dw_ukv = dw_ukv_p.reshape(KV_LORA, 2, N_HEADS, NOPE_DIM).transpose(0, 2, 1, 3).reshape(KV_LORA, 2 * ATTN_WIDTH)
    slabs_b = _slab1(in_shards, _by_device(dw_uq, N_HEADS * QK_DIM // nd), _by_device(dw_ukv, 2 * ATTN_WIDTH // nd))
    slabs_b = slabs_b.reshape(4, 2, SLAB1_ROWS, SLAB1_COLS)
    my_c = lax.axis_index("c").astype(jnp.int32).reshape(1)
    chip_part = _add_blocks(slabs_b, _sibling_exchange(slabs_b), my_c)
    shard_b = _sum_slabs(_chip_exchange(chip_part), "rs_sum")
    grads_w = [shard_b[:D_MODEL, :SH_IN], shard_b[D_MODEL:D_MODEL + UQ_ROWS].reshape(Q_LORA, -1),
               shard_b[D_MODEL + UQ_ROWS:].reshape(-1)[:KV_LORA * V_DIM].reshape(KV_LORA, V_DIM),
               shard_a[2 * SH_ROWS + PL_ROWS:].reshape(-1)[:CONV_K * CONV_WIDTH // nd].reshape(CONV_K, -1),
               shard_a[0:SH_ROWS], shard_a[2 * SH_ROWS:2 * SH_ROWS + PL_ROWS].reshape(PLE_DIM, -1), shard_a[SH_ROWS:2 * SH_ROWS]]

    gains_part = [dg_in, dg_cq, dg_ckv, dg_q_p, dg_k_p, dg_oa, dg_oc, dg_pl]
    gflat = jnp.concatenate([g.reshape(-1) for g in gains_part] + [jnp.zeros(((GAIN_ROWS - 1) * LANES - 3968,), f32),
                                                                    loss_part.reshape(-1)])
    gain_parts = _all_gather(gflat.reshape(GAIN_ROWS, LANES), "ag_gains", in_vmem=True).reshape(nd, GAIN_ROWS, LANES)

    gains = [g_in, g_cq, g_ckv, g_q_p, g_k_p, g_oa, g_oc, g_pl]
    padq = lambda a: jnp.pad(a, ((0, 0), (0, HEAD_PAD - QK_DIM)))
    m_gains = [m_g_in, m_g_cq, m_g_ckv, padq(m_g_q), padq(m_g_k), m_g_oa, m_g_oc, m_g_pl]
    v_gains = [v_g_in, v_g_cq, v_g_ckv, padq(v_g_q), padq(v_g_k), v_g_oa, v_g_oc, v_g_pl]
    ws = gains + [w_in[0], w_uq[0], w_ukv[0], conv_w[0], w_o[0], w_pl[0], w_plg[0]]
    ms = m_gains + [m_w_in[0], m_w_uq[0], m_w_ukv[0], m_conv_w[0], m_w_o[0], m_w_pl[0], m_w_plg[0]]
    vs = v_gains + [v_w_in[0], v_w_uq[0], v_w_ukv[0], v_conv_w[0], v_w_o[0], v_w_pl[0], v_w_plg[0]]
    loss_row, gg, deltas, new_m, new_v = _adamw_all(ws, grads_w, ms, vs, gain_parts)
    loss = loss_row[0, 0]

    def ordered(gl, wl):
        g_in_, g_cq_, g_ckv_, g_q_, g_k_, g_oa_, g_oc_, g_pl_ = gl
        g_q_, g_k_ = g_q_[:, :QK_DIM], g_k_[:, :QK_DIM]
        w_in_, w_uq_, w_ukv_, cw_, w_o_, w_pl_, w_plg_ = [w[None] for w in wl]
        return [g_in_, w_in_, g_cq_, w_uq_, g_ckv_, w_ukv_, g_q_, g_k_, cw_, g_oa_, g_oc_, w_o_, w_pl_, w_plg_, g_pl_]

    ng = len(gains)
    outs = [loss, grad_x[None]]
    outs += ordered(gg, grads_w)
    for lst in (deltas, new_m, new_v):
        outs += ordered(lst[:ng], lst[ng:])
    return tuple(outs)
```

```python
import functools
import math

import jax
import jax.numpy as jnp
from jax import lax
from jax.experimental import pallas as pl
from jax.experimental.pallas import tpu as pltpu

f32 = jnp.float32
bf16 = jnp.bfloat16
MM = jnp.bfloat16
RS_DTYPE = jnp.bfloat16

D_MODEL = 1024
PLE_DIM = 256
N_HEADS = 4
NOPE_DIM = 128
ROPE_DIM = 64
V_DIM = 128
QK_DIM = NOPE_DIM + ROPE_DIM
Q_LORA = 256
KV_LORA = 128
ATTN_WIDTH = N_HEADS * V_DIM
CONV_WIDTH = D_MODEL - ATTN_WIDTH
CONV_K = 3
ROPE_THETA = 10000.0
RMS_EPS = 1e-6
NEG_INF = -1e30
IN_TOTAL = Q_LORA + KV_LORA + ROPE_DIM + ATTN_WIDTH + 4 * CONV_WIDTH
ADAM_LR = 0.001
ADAM_B1 = 0.9
ADAM_B2 = 0.999
ADAM_EPS = 1e-08
ADAM_WD = 0.01
ADAM_STEP = 10

N_DEV = 8
LANES = 128
HEAD_PAD = 256
QK_PAD = N_HEADS * HEAD_PAD
PROJ_PAD = 3072
TS = 512
TS_M = 256
TS_E = 256
TQ = 512
VMEM_LIMIT = 58 * 1024 * 1024
MESH = pl.DeviceIdType.MESH
QK_SCALE = 1.0 / math.sqrt(QK_DIM)
SCALE_LOG2E = QK_SCALE * math.log2(math.e)


def _cparams(sem=None):
    return pltpu.CompilerParams(dimension_semantics=sem, vmem_limit_bytes=VMEM_LIMIT)


def _nt(a, b):
    return lax.dot_general(a, b, (((1,), (1,)), ((), ())), preferred_element_type=f32)


def _tn(a, b):
    return lax.dot_general(a, b, (((0,), (0,)), ((), ())), preferred_element_type=f32)


def _nn(a, b):
    return jnp.dot(a, b, preferred_element_type=f32)


def _rstd(x, n):
    return lax.rsqrt(jnp.sum(x * x, axis=-1, keepdims=True) * (1.0 / n) + RMS_EPS)


def _rms_bwd(dy, xhat, r, g, n):
    u = dy * g
    return r * (u - xhat * (jnp.sum(u * xhat, axis=-1, keepdims=True) * (1.0 / n)))


def _rope(t, c, s1, s2):
    return t * c + pltpu.roll(t, 32, 1) * s1 + pltpu.roll(t, 96, 1) * s2


def _rope_t(d, c, s1, s2):
    return d * c - pltpu.roll(d, 96, 1) * s2 - pltpu.roll(d, 32, 1) * s1


def _const(shape):
    return pl.BlockSpec(shape, lambda *_: (0,) * len(shape), pipeline_mode=pl.Buffered(1))


def _two_level_gather(x_ref, out_ref, send_sems, recv_sems, local_sem):
    m_per = x_ref.shape[0]
    x, y, c = lax.axis_index("x"), lax.axis_index("y"), lax.axis_index("c")
    me, sibling = (x, y, c), (x, y, 1 - c)
    chips = [(1 - x, y), (x, 1 - y), (1 - x, 1 - y)]

    def rows(px, py, pc):
        return out_ref.at[pl.ds((4 * px + 2 * py + pc) * m_per, m_per), :]

    def copy(k, blk, to, src=None):
        return pltpu.make_async_remote_copy(
            src_ref=rows(*blk) if src is None else src, dst_ref=rows(*blk),
            send_sem=send_sems.at[k], recv_sem=recv_sems.at[k], device_id=to, device_id_type=MESH)

    mine = pltpu.make_async_copy(x_ref, rows(*me), local_sem)
    first = [copy(0, me, sibling, src=x_ref)] + [copy(1 + j, me, (*chip, c), src=x_ref) for j, chip in enumerate(chips)]
    passed = [copy(4 + j, (*chip, c), sibling) for j, chip in enumerate(chips)]

    def start():
        mine.start()
        for cp in first:
            cp.start()

    def finish():
        for j, chip in enumerate(chips):
            copy(1 + j, (*chip, c), me).wait_recv()
            passed[j].start()
        copy(0, sibling, me).wait_recv()
        for j, chip in enumerate(chips):
            copy(4 + j, (*chip, 1 - c), me).wait_recv()
        for cp in first + passed:
            cp.wait_send()
        mine.wait()

    return start, finish


GATHER_SEMS = [pltpu.SemaphoreType.DMA((7,)), pltpu.SemaphoreType.DMA((7,)), pltpu.SemaphoreType.DMA]


def _rope_tables(posf, inv_freq, block):
    S = posf.shape[0]
    t = min(S, TS)
    nt = S // t

    def body(pos_ref, f_ref, blk_ref, c_ref, s1_ref, s2_ref, all_ref, send_sems, recv_sems, local_sem):
        i = pl.program_id(0)
        start, finish = _two_level_gather(blk_ref, all_ref, send_sems, recv_sems, local_sem)

        @pl.when(i == 0)
        def _():
            start()

        ang = pos_ref[...] * f_ref[...]
        lane = lax.broadcasted_iota(jnp.int32, ang.shape, 1)
        cs, sn = jnp.cos(ang), jnp.sin(ang)
        c_ref[...] = jnp.where(lane < ROPE_DIM, cs, 0.0)
        s1_ref[...] = jnp.where((lane >= ROPE_DIM // 2) & (lane < ROPE_DIM), sn, 0.0)
        s2_ref[...] = jnp.where(lane < ROPE_DIM // 2, -sn, 0.0)

        @pl.when(i == nt - 1)
        def _():
            finish()

    spec = pl.BlockSpec((t, LANES), lambda i: (i, 0))
    hbm = pl.BlockSpec(memory_space=pl.ANY)
    outs = pl.pallas_call(
        body, name="rope_tables", grid=(nt,),
        out_shape=[jax.ShapeDtypeStruct((S, LANES), f32)] * 3 + [jax.ShapeDtypeStruct((N_DEV * block.shape[0], block.shape[1]), block.dtype)],
        in_specs=[spec, _const((1, LANES)), hbm], out_specs=[spec] * 3 + [hbm],
        scratch_shapes=list(GATHER_SEMS),
        compiler_params=_cparams(("arbitrary",)),
    )(posf, inv_freq, block)
    return outs[:3], outs[3]


def _fwd_proj(x, tabs, g_in, w_in_p, g_cq, w_uq_p, g_ckv, w_ukv_p, g_q_p, g_k_p):
    S = x.shape[0]
    T = min(TS, S)
    nt = S // T
    LAT = Q_LORA + KV_LORA + LANES

    def body(x_ref, c_ref, s1_ref, s2_ref, g_in_ref, w_in_ref, g_cq_ref, w_uq_ref, g_ckv_ref, w_ukv_ref,
             g_q_ref, g_k_ref, proj_ref, q_ref, k_ref, v_ref, lat_a, lat_b):
        i = pl.program_id(0)

        @pl.when(i == 0)
        def _():
            lat_b[...] = jnp.zeros(lat_b.shape, f32)

        def stage1(lat):
            xv = x_ref[...]
            h = (xv * _rstd(xv, D_MODEL) * g_in_ref[...]).astype(MM)
            proj_ref[...] = _nn(h, w_in_ref[...])
            lat[...] = proj_ref[:, 0:LAT]

        def stage2(lat):
            c, s1, s2 = c_ref[...], s1_ref[...], s2_ref[...]
            g_q, g_k = g_q_ref[...], g_k_ref[...]
            c_q = lat[:, 0:Q_LORA]
            n_cq = (c_q * _rstd(c_q, Q_LORA) * g_cq_ref[...]).astype(MM)
            q_raw = _nn(n_cq, w_uq_ref[...])
            for hd in range(N_HEADS):
                qh = q_raw[:, hd * HEAD_PAD:(hd + 1) * HEAD_PAD]
                qn = qh * _rstd(qh, QK_DIM) * g_q
                q_ref[:, hd * HEAD_PAD:hd * HEAD_PAD + LANES] = qn[:, :LANES].astype(MM)
                q_ref[:, hd * HEAD_PAD + LANES:(hd + 1) * HEAD_PAD] = _rope(qn[:, LANES:], c, s1, s2).astype(MM)

            c_kv = lat[:, Q_LORA:Q_LORA + KV_LORA]
            n_ckv = (c_kv * _rstd(c_kv, KV_LORA) * g_ckv_ref[...]).astype(MM)
            kv_raw = _nn(n_ckv, w_ukv_ref[...])
            kpe = lat[:, Q_LORA + KV_LORA:LAT]
            ss_pe = jnp.sum(kpe * kpe, axis=-1, keepdims=True)
            kr = _rope(kpe * g_k[:, LANES:], c, s1, s2)
            for hd in range(N_HEADS):
                kn = kv_raw[:, hd * LANES:(hd + 1) * LANES]
                rk = lax.rsqrt((jnp.sum(kn * kn, axis=-1, keepdims=True) + ss_pe) * (1.0 / QK_DIM) + RMS_EPS)
                k_ref[:, hd * HEAD_PAD:hd * HEAD_PAD + LANES] = (kn * rk * g_k[:, :LANES]).astype(MM)
                k_ref[:, hd * HEAD_PAD + LANES:(hd + 1) * HEAD_PAD] = (kr * rk).astype(MM)
            v_ref[...] = kv_raw[:, N_HEADS * NOPE_DIM:].astype(MM)

        @pl.when(i % 2 == 0)
        def _():
            stage2(lat_b)
            stage1(lat_a)

        @pl.when(i % 2 == 1)
        def _():
            stage2(lat_a)
            stage1(lat_b)

    cur = lambda w: pl.BlockSpec((T, w), lambda i: (jnp.minimum(i, nt - 1), 0))
    prv = lambda w: pl.BlockSpec((T, w), lambda i: (jnp.maximum(i - 1, 0), 0))
    return pl.pallas_call(
        body, name="fwd_proj", grid=(nt + 1,),
        out_shape=[jax.ShapeDtypeStruct((S, PROJ_PAD), f32), jax.ShapeDtypeStruct((S, QK_PAD), MM),
                   jax.ShapeDtypeStruct((S, QK_PAD), MM), jax.ShapeDtypeStruct((S, ATTN_WIDTH), MM)],
        in_specs=[cur(D_MODEL), prv(LANES), prv(LANES), prv(LANES), _const((1, D_MODEL)), _const((D_MODEL, PROJ_PAD)),
                  _const((1, Q_LORA)), _const((Q_LORA, QK_PAD)), _const((1, KV_LORA)), _const((KV_LORA, 2 * ATTN_WIDTH)),
                  _const((1, HEAD_PAD)), _const((1, HEAD_PAD))],
        out_specs=[cur(PROJ_PAD), prv(QK_PAD), prv(QK_PAD), prv(ATTN_WIDTH)],
        scratch_shapes=[pltpu.VMEM((T, LAT), f32), pltpu.VMEM((T, LAT), f32)],
        compiler_params=_cparams(("arbitrary",)),
    )(x, *tabs, g_in, w_in_p, g_cq, w_uq_p, g_ckv, w_ukv_p, g_q_p, g_k_p)


def _exchange(src_of, dst_ref, send_sems, recv_sems, local_sem):
    x, y, c = lax.axis_index("x"), lax.axis_index("y"), lax.axis_index("c")
    me = 4 * x + 2 * y + c
    remote = []
    for k in range(1, N_DEV):
        px = 1 - x if (k >> 2) & 1 else x
        py = 1 - y if (k >> 1) & 1 else y
        pc = 1 - c if k & 1 else c
        remote.append(pltpu.make_async_remote_copy(
            src_ref=src_of(4 * px + 2 * py + pc), dst_ref=dst_ref.at[me], send_sem=send_sems.at[k - 1],
            recv_sem=recv_sems.at[k - 1], device_id=(px, py, pc), device_id_type=MESH))
    return remote, pltpu.make_async_copy(src_of(me), dst_ref.at[me], local_sem)


def _exchange_start(copies):
    remote, local = copies
    local.start()
    for cp in remote:
        cp.start()


def _exchange_wait(copies):
    remote, local = copies
    for cp in remote:
        cp.wait_recv()
    for cp in remote:
        cp.wait_send()
    local.wait()


EXCHANGE_SEMS = [pltpu.SemaphoreType.DMA((N_DEV - 1,)), pltpu.SemaphoreType.DMA((N_DEV - 1,)), pltpu.SemaphoreType.DMA]


def _flash_fwd(q, k, v, block):
    S = q.shape[0]
    T = min(TQ, S)
    nq = S // T

    def body(q_ref, k_ref, v_ref, blk_ref, o_ref, lse_ref, all_ref, s_a, s_b, m_sc, l_sc, acc_sc, send_sems, recv_sems, local_sem):
        n = pl.program_id(1)
        gather = _exchange(lambda d: blk_ref, all_ref, send_sems, recv_sems, local_sem)

        @pl.when((pl.program_id(0) == 0) & (n == 0))
        def _():
            _exchange_start(gather)

        m_sc[...] = jnp.full(m_sc.shape, NEG_INF, f32)
        l_sc[...] = jnp.zeros(l_sc.shape, f32)
        acc_sc[...] = jnp.zeros(acc_sc.shape, f32)
        qv = q_ref[...]

        def rows(j):
            return pl.ds(pl.multiple_of(j * T, T), T)

        def scores(j, dst):
            dst[...] = _nt(k_ref[rows(j), :], qv)

        def update(src, j, masked):
            s = src[...]
            if masked:
                k_i = lax.broadcasted_iota(jnp.int32, (T, T), 0)
                q_i = lax.broadcasted_iota(jnp.int32, (T, T), 1)
                s = jnp.where(k_i <= q_i, s, NEG_INF)
            m_prev = m_sc[...]
            m_new = jnp.maximum(m_prev, jnp.max(s, axis=0, keepdims=True))
            alpha = jnp.exp2((m_prev - m_new) * SCALE_LOG2E)
            p = jnp.exp2((s - m_new[0:1, :]) * SCALE_LOG2E)
            l_sc[...] = alpha * l_sc[...] + jnp.sum(p, axis=0, keepdims=True)
            acc_sc[...] = alpha[0:1, :] * acc_sc[...] + _tn(v_ref[rows(j), :], p.astype(MM))
            m_sc[...] = m_new

        scores(0, s_a)

        def pair(t, carry):
            j = 2 * t
            scores(j + 1, s_b)
            update(s_a, j, False)
            scores(j + 2, s_a)
            update(s_b, j + 1, False)
            return carry

        lax.fori_loop(0, n // 2, pair, 0)

        @pl.when(n % 2 == 0)
        def _():
            update(s_a, n, True)

        @pl.when(n % 2 == 1)
        def _():
            scores(n, s_b)
            update(s_a, n - 1, False)
            update(s_b, n, True)

        o_ref[...] = (acc_sc[...] / l_sc[0:1, :]).T
        lse_ref[...] = m_sc[...] * SCALE_LOG2E + jnp.log2(l_sc[...])

        @pl.when((pl.program_id(0) == N_HEADS - 1) & (n == nq - 1))
        def _():
            _exchange_wait(gather)

    return pl.pallas_call(
        body, name="flash_fwd", grid=(N_HEADS, nq),
        out_shape=[jax.ShapeDtypeStruct((S, ATTN_WIDTH), f32), jax.ShapeDtypeStruct((N_HEADS * 8, S), f32),
                   jax.ShapeDtypeStruct((N_DEV,) + block.shape, block.dtype)],
        in_specs=[pl.BlockSpec((T, HEAD_PAD), lambda h, i: (i, h)),
                  pl.BlockSpec((S, HEAD_PAD), lambda h, i: (0, h)),
                  pl.BlockSpec((S, V_DIM), lambda h, i: (0, h)),
                  pl.BlockSpec(memory_space=pl.ANY)],
        out_specs=[pl.BlockSpec((T, V_DIM), lambda h, i: (i, h)), pl.BlockSpec((8, T), lambda h, i: (h, i)),
                   pl.BlockSpec(memory_space=pl.ANY)],
        scratch_shapes=[pltpu.VMEM((T, T), f32), pltpu.VMEM((T, T), f32), pltpu.VMEM((8, T), f32), pltpu.VMEM((8, T), f32),
                        pltpu.VMEM((V_DIM, T), f32)] + EXCHANGE_SEMS,
        compiler_params=_cparams(("arbitrary", "arbitrary")),
    )(q, k, v, block)


def _shift_down(m, prev8, n):
    T = m.shape[0]
    rows = lax.broadcasted_iota(jnp.int32, m.shape, 0)
    head = jnp.tile(pltpu.roll(prev8, n, 0), (T // 8, 1))
    return jnp.where(rows >= n, pltpu.roll(m, n, 0), head)


def _shift_up(m, next8, n):
    T = m.shape[0]
    rows = lax.broadcasted_iota(jnp.int32, m.shape, 0)
    tail = jnp.tile(pltpu.roll(next8, 8 - n, 0), (T // 8, 1))
    return jnp.where(rows < T - n, pltpu.roll(m, T - n, 0), tail)


def _mid(o, proj, x, p, target, w_o, w_plg, w_pl, conv_w8, g_oa, g_oc, g_pl):
    S = x.shape[0]
    T = min(TS_M, S)
    nt = S // T

    def body(o_ref, za_ref, ccx_ref, cbzc_ref, prev_ref, x_ref, p_ref, t_ref, wo_ref, wplg_ref, wpl_ref, cw_ref,
             goa_ref, goc_ref, gpl_ref,
             dx2_ref, do_ref, delta_ref, dza_ref, dcbzc_ref, du_ref,
             dwo_ref, dwplg_ref, dwpl_ref, dcw_ref, dgoa_ref, dgoc_ref, dgpl_ref, loss_ref):
        i = pl.program_id(0)

        @pl.when(i == 0)
        def _():
            for r in (dwo_ref, dwplg_ref, dwpl_ref, dcw_ref, dgoa_ref, dgoc_ref, dgpl_ref, loss_ref):
                r[...] = jnp.zeros(r.shape, f32)

        o_v = o_ref[...]
        z_a = za_ref[...]
        cc, cx = ccx_ref[:, :CONV_WIDTH], ccx_ref[:, CONV_WIDTH:]
        cb, z_c = cbzc_ref[:, :CONV_WIDTH], cbzc_ref[:, CONV_WIDTH:]
        g_oa, g_oc, g_pl = goa_ref[...], goc_ref[...], gpl_ref[...]
        w0, w1, w2 = cw_ref[0:1, :], cw_ref[1:2, :], cw_ref[2:3, :]

        sa = jax.nn.sigmoid(z_a)
        silu_a = z_a * sa
        ga = o_v * silu_a
        ra = _rstd(ga, ATTN_WIDTH)
        gha = ga * ra
        m0 = cc * cx
        prev = prev_ref[:, :CONV_WIDTH] * prev_ref[:, CONV_WIDTH:] * (i > 0).astype(f32)
        m1 = _shift_down(m0, prev, 1)
        m2 = _shift_down(m0, prev, 2)
        u = w0 * m2 + w1 * m1 + w2 * m0
        sc = jax.nn.sigmoid(z_c)
        silu_c = z_c * sc
        gc = cb * u * silu_c
        rc = _rstd(gc, CONV_WIDTH)
        ghc = gc * rc
        ycat = jnp.concatenate([gha * g_oa, ghc * g_oc], axis=-1).astype(MM)
        x2 = x_ref[...] + _nn(ycat, wo_ref[...])
        r2 = _rstd(x2, D_MODEL)
        xh2 = x2 * r2
        n2 = (xh2 * g_pl).astype(MM)
        gate = jax.nn.sigmoid(_nn(n2, wplg_ref[...]))
        p_b = p_ref[...].astype(MM)
        pw = _nn(p_b, wpl_ref[...])
        err = x2 + gate * pw - t_ref[...]
        loss_ref[...] += jnp.sum(jnp.sum(err * err, axis=-1, keepdims=True), axis=0, keepdims=True) * (0.5 / D_MODEL)
        d_out = err * (1.0 / D_MODEL)
        dwpl_ref[...] += _tn(p_b, (d_out * gate).astype(MM))
        d_glog = (d_out * pw * gate * (1.0 - gate)).astype(MM)
        dwplg_ref[...] += _tn(n2, d_glog)
        d_n2 = _nt(d_glog, wplg_ref[...])
        dgpl_ref[...] += jnp.sum(d_n2 * xh2, axis=0, keepdims=True)
        d_x2 = d_out + _rms_bwd(d_n2, xh2, r2, g_pl, D_MODEL)
        dx2_ref[...] = d_x2
        d_x2b = d_x2.astype(MM)
        dwo_ref[...] += _tn(ycat, d_x2b)
        d_ycat = _nt(d_x2b, wo_ref[...])
        d_ya, d_yc = d_ycat[:, :ATTN_WIDTH], d_ycat[:, ATTN_WIDTH:]
        dgoa_ref[...] += jnp.sum(d_ya * gha, axis=0, keepdims=True)
        dgoc_ref[...] += jnp.sum(d_yc * ghc, axis=0, keepdims=True)
        d_ga = _rms_bwd(d_ya, gha, ra, g_oa, ATTN_WIDTH)
        d_gc = _rms_bwd(d_yc, ghc, rc, g_oc, CONV_WIDTH)
        d_o = d_ga * silu_a
        do_ref[...] = d_o.astype(MM)
        dza_ref[...] = (d_ga * o_v * (sa * (1.0 + z_a * (1.0 - sa)))).astype(MM)
        for hd in range(N_HEADS):
            cols = slice(hd * V_DIM, (hd + 1) * V_DIM)
            dl = jnp.sum(d_o[:, cols] * o_v[:, cols], axis=-1, keepdims=True)
            delta_ref[hd * 8:(hd + 1) * 8, :] = jnp.broadcast_to(dl, (T, LANES)).T[0:8, :]
        d_u = d_gc * cb * silu_c
        du_ref[...] = d_u
        dcbzc_ref[:, :CONV_WIDTH] = (d_gc * u * silu_c).astype(MM)
        dcbzc_ref[:, CONV_WIDTH:] = (d_gc * cb * u * (sc * (1.0 + z_c * (1.0 - sc)))).astype(MM)
        dcw_ref[0:1, :] += jnp.sum(d_u * m2, axis=0, keepdims=True)
        dcw_ref[1:2, :] += jnp.sum(d_u * m1, axis=0, keepdims=True)
        dcw_ref[2:3, :] += jnp.sum(d_u * m0, axis=0, keepdims=True)

    row = lambda w, blk=0: pl.BlockSpec((T, w), lambda i: (i, blk))
    acc = lambda shape: pl.BlockSpec(shape, lambda i: (0, 0))
    tb = T // 8
    sds = jax.ShapeDtypeStruct
    return pl.pallas_call(
        body, name="mid", grid=(nt,),
        out_shape=[sds((S, D_MODEL), f32), sds((S, ATTN_WIDTH), MM), sds((N_HEADS * 8, S), f32), sds((S, ATTN_WIDTH), MM),
                   sds((S, 2 * CONV_WIDTH), MM), sds((S, CONV_WIDTH), f32),
                   sds((D_MODEL, D_MODEL), f32), sds((D_MODEL, D_MODEL), f32), sds((PLE_DIM, D_MODEL), f32), sds((8, CONV_WIDTH), f32),
                   sds((1, ATTN_WIDTH), f32), sds((1, CONV_WIDTH), f32), sds((1, D_MODEL), f32), sds((1, LANES), f32)],
        in_specs=[row(ATTN_WIDTH), row(ATTN_WIDTH, 1), row(2 * CONV_WIDTH, 1), row(2 * CONV_WIDTH, 2),
                  pl.BlockSpec((8, 2 * CONV_WIDTH), lambda i: (jnp.maximum(i * tb - 1, 0), 1)),
                  row(D_MODEL), row(PLE_DIM), row(D_MODEL),
                  _const((D_MODEL, D_MODEL)), _const((D_MODEL, D_MODEL)), _const((PLE_DIM, D_MODEL)), _const((8, CONV_WIDTH)),
                  _const((1, ATTN_WIDTH)), _const((1, CONV_WIDTH)), _const((1, D_MODEL))],
        out_specs=[row(D_MODEL), row(ATTN_WIDTH), pl.BlockSpec((N_HEADS * 8, T), lambda i: (0, i)), row(ATTN_WIDTH),
                   row(2 * CONV_WIDTH), row(CONV_WIDTH),
                   acc((D_MODEL, D_MODEL)), acc((D_MODEL, D_MODEL)), acc((PLE_DIM, D_MODEL)), acc((8, CONV_WIDTH)),
                   acc((1, ATTN_WIDTH)), acc((1, CONV_WIDTH)), acc((1, D_MODEL)), acc((1, LANES))],
        compiler_params=_cparams(("arbitrary",)),
    )(o, proj, proj, proj, proj, x, p, target, w_o, w_plg, w_pl, conv_w8, g_oa, g_oc, g_pl)


def _flash_bwd(q, k, v, d_o, lse2, delta, by_target):
    S = q.shape[0]
    T = min(TQ, S)
    nq = S // T

    def body(k_ref, v_ref, q_ref, do_ref, lse_ref, delta_ref, out_ref, dq_ref, dk_ref, dv_ref, in_ref, s_a, dp_a, s_b, dp_b,
             send_sems, recv_sems, local_sem):
        j = pl.program_id(1)
        scatter = _exchange(lambda d: out_ref.at[d], in_ref, send_sems, recv_sems, local_sem)

        @pl.when((pl.program_id(0) == 0) & (j == 0))
        def _():
            _exchange_start(scatter)

        @pl.when(j == 0)
        def _():
            dq_ref[...] = jnp.zeros(dq_ref.shape, f32)

        dk_ref[...] = jnp.zeros(dk_ref.shape, f32)
        dv_ref[...] = jnp.zeros(dv_ref.shape, f32)
        kv, vv = k_ref[...], v_ref[...]

        def rows_of(i):
            return pl.ds(pl.multiple_of(i * T, T), T)

        def scores(i, s_dst, dp_dst):
            rows = rows_of(i)
            s_dst[...] = _nt(kv, q_ref[rows, :])
            dp_dst[...] = _nt(vv, do_ref[rows, :])

        def grads(i, s_src, dp_src, masked):
            rows = rows_of(i)
            s = s_src[...]
            if masked:
                k_i = lax.broadcasted_iota(jnp.int32, (T, T), 0)
                q_i = lax.broadcasted_iota(jnp.int32, (T, T), 1)
                s = jnp.where(k_i <= q_i, s, NEG_INF)
            p = jnp.exp2(s * SCALE_LOG2E - lse_ref[0:1, rows])
            ds = (p * (dp_src[...] - delta_ref[0:1, rows])).astype(MM)
            dv_ref[...] += _nn(p.astype(MM), do_ref[rows, :])
            dk_ref[...] += _nn(ds, q_ref[rows, :])
            dq_ref[rows, :] += _tn(ds, kv)

        last = nq - 1
        scores(j, s_a, dp_a)
        scores(jnp.minimum(j + 1, last), s_b, dp_b)
        grads(j, s_a, dp_a, True)
        rest = last - j

        def pair(t, carry):
            u = j + 1 + 2 * t
            scores(u + 1, s_a, dp_a)
            grads(u, s_b, dp_b, False)
            scores(jnp.minimum(u + 2, last), s_b, dp_b)
            grads(u + 1, s_a, dp_a, False)
            return carry

        lax.fori_loop(0, rest // 2, pair, 0)

        @pl.when(rest % 2 == 1)
        def _():
            grads(last, s_b, dp_b, False)

        @pl.when((pl.program_id(0) == N_HEADS - 1) & (j == nq - 1))
        def _():
            _exchange_wait(scatter)

    whole = lambda w: pl.BlockSpec((S, w), lambda h, j: (0, h))
    stat = pl.BlockSpec((8, S), lambda h, j: (h, 0))
    return pl.pallas_call(
        body, name="flash_bwd", grid=(N_HEADS, nq),
        out_shape=[jax.ShapeDtypeStruct((S, QK_PAD), f32), jax.ShapeDtypeStruct((S, QK_PAD), f32),
                   jax.ShapeDtypeStruct((S, ATTN_WIDTH), f32), jax.ShapeDtypeStruct(by_target.shape, by_target.dtype)],
        in_specs=[pl.BlockSpec((T, HEAD_PAD), lambda h, j: (j, h)), pl.BlockSpec((T, V_DIM), lambda h, j: (j, h)),
                  whole(HEAD_PAD), whole(V_DIM), stat, stat, pl.BlockSpec(memory_space=pl.ANY)],
        out_specs=[whole(HEAD_PAD), pl.BlockSpec((T, HEAD_PAD), lambda h, j: (j, h)),
                   pl.BlockSpec((T, V_DIM), lambda h, j: (j, h)), pl.BlockSpec(memory_space=pl.ANY)],
        scratch_shapes=[pltpu.VMEM((T, T), f32)] * 4 + EXCHANGE_SEMS,
        compiler_params=_cparams(("arbitrary", "arbitrary")),
    )(k, v, q, d_o, lse2, delta, by_target)


def _bwd_proj(x, proj, d_q, d_k, d_v, d_za, d_cbzc, d_u, d_x2, tabs, g_in, w_in_p, g_cq, w_uq_p, g_ckv, w_ukv_p,
              g_q_p, g_k_p, conv_w8):
    S = x.shape[0]
    T = min(TS_E, S)
    nt = S // T

    def body(x_ref, cqkv_ref, ccx_ref, dq_ref, dk_ref, dv_ref, dza_ref, dcbzc_ref, du_ref, dun_ref, dx2_ref,
             c_ref, s1_ref, s2_ref, g_in_ref, w_in_ref, g_cq_ref, w_uq_ref, g_ckv_ref, w_ukv_ref, g_q_ref, g_k_ref, cw_ref,
             gx_ref, dwin_hbm, dwuq_ref, dwukv_ref, dgin_ref, dgcq_ref, dgckv_ref, dgq_ref, dgk_ref,
             dproj_a, dproj_b, dqraw_sc, dkvraw_sc, dwin_sc, sem):
        i = pl.program_id(0)

        @pl.when(i == 0)
        def _():
            dwin_sc[...] = jnp.zeros(dwin_sc.shape, f32)
            dproj_b[...] = jnp.zeros(dproj_b.shape, MM)
            for r in (dwuq_ref, dwukv_ref, dgin_ref, dgcq_ref, dgckv_ref, dgq_ref, dgk_ref):
                r[...] = jnp.zeros(r.shape, f32)

        def stage1(dproj_sc):
            live = (i < nt).astype(f32)
            c, s1, s2 = c_ref[...], s1_ref[...], s2_ref[...]
            g_q, g_k = g_q_ref[...], g_k_ref[...]
            g_cq, g_ckv = g_cq_ref[...], g_ckv_ref[...]

            c_q = cqkv_ref[:, 0:Q_LORA]
            r_cq = _rstd(c_q, Q_LORA)
            cqh = c_q * r_cq
            n_cq = (cqh * g_cq).astype(MM)
            q_raw = _nn(n_cq, w_uq_ref[...])
            dgq = jnp.zeros((1, HEAD_PAD), f32)
            for hd in range(N_HEADS):
                cols = slice(hd * HEAD_PAD, (hd + 1) * HEAD_PAD)
                qh = q_raw[:, cols]
                rq = _rstd(qh, QK_DIM)
                qhh = qh * rq
                dqh = dq_ref[:, cols] * QK_SCALE
                d_qn = jnp.concatenate([dqh[:, :LANES], _rope_t(dqh[:, LANES:], c, s1, s2)], axis=-1)
                dgq += jnp.sum(d_qn * qhh, axis=0, keepdims=True)
                dqraw_sc[:, cols] = _rms_bwd(d_qn, qhh, rq, g_q, QK_DIM).astype(MM)
            dgq_ref[...] += dgq * live
            d_qraw = dqraw_sc[...]
            dwuq_ref[...] += _tn((cqh * (g_cq * live)).astype(MM), d_qraw)
            d_ncq = _nt(d_qraw, w_uq_ref[...])
            dgcq_ref[...] += jnp.sum(d_ncq * cqh, axis=0, keepdims=True) * live
            dproj_sc[:, 0:Q_LORA] = _rms_bwd(d_ncq, cqh, r_cq, g_cq, Q_LORA).astype(MM)

            c_kv = cqkv_ref[:, Q_LORA:Q_LORA + KV_LORA]
            r_ckv = _rstd(c_kv, KV_LORA)
            ckvh = c_kv * r_ckv
            n_ckv = (ckvh * g_ckv).astype(MM)
            k_nope = _nn(n_ckv, w_ukv_ref[:, :N_HEADS * NOPE_DIM])
            kpe = cqkv_ref[:, Q_LORA + KV_LORA:Q_LORA + KV_LORA + LANES]
            ss_pe = jnp.sum(kpe * kpe, axis=-1, keepdims=True)
            g_kn, g_kp = g_k[:, :LANES], g_k[:, LANES:]
            d_kpe = jnp.zeros((T, LANES), f32)
            dgk_n = jnp.zeros((1, LANES), f32)
            dgk_p = jnp.zeros((1, LANES), f32)
            for hd in range(N_HEADS):
                kn = k_nope[:, hd * LANES:(hd + 1) * LANES]
                rk = lax.rsqrt((jnp.sum(kn * kn, axis=-1, keepdims=True) + ss_pe) * (1.0 / QK_DIM) + RMS_EPS)
                knh, kph = kn * rk, kpe * rk
                d_kn_n = dk_ref[:, hd * HEAD_PAD:hd * HEAD_PAD + LANES] * QK_SCALE
                d_kr = _rope_t(dk_ref[:, hd * HEAD_PAD + LANES:(hd + 1) * HEAD_PAD] * QK_SCALE, c, s1, s2)
                dgk_n += jnp.sum(d_kn_n * knh, axis=0, keepdims=True)
                dgk_p += jnp.sum(d_kr * kph, axis=0, keepdims=True)
                u_n, u_p = d_kn_n * g_kn, d_kr * g_kp
                mt = (jnp.sum(u_n * knh, axis=-1, keepdims=True) + jnp.sum(u_p * kph, axis=-1, keepdims=True)) * (1.0 / QK_DIM)
                dkvraw_sc[:, hd * LANES:(hd + 1) * LANES] = (rk * (u_n - knh * mt)).astype(MM)
                d_kpe += rk * (u_p - kph * mt)
            dgk_ref[:, :LANES] += dgk_n * live
            dgk_ref[:, LANES:] += dgk_p * live
            dkvraw_sc[:, N_HEADS * NOPE_DIM:] = dv_ref[...].astype(MM)
            d_kvraw = dkvraw_sc[...]
            dwukv_ref[...] += _tn((ckvh * (g_ckv * live)).astype(MM), d_kvraw)
            d_nckv = _nt(d_kvraw, w_ukv_ref[...])
            dgckv_ref[...] += jnp.sum(d_nckv * ckvh, axis=0, keepdims=True) * live
            dproj_sc[:, Q_LORA:Q_LORA + KV_LORA] = _rms_bwd(d_nckv, ckvh, r_ckv, g_ckv, KV_LORA).astype(MM)
            dproj_sc[:, Q_LORA + KV_LORA:Q_LORA + KV_LORA + LANES] = d_kpe.astype(MM)

            dproj_sc[:, 512:1024] = dza_ref[...]
            d_u = du_ref[...]
            nxt = dun_ref[...] * (i < nt - 1).astype(f32)
            d_m = cw_ref[2:3, :] * d_u + cw_ref[1:2, :] * _shift_up(d_u, nxt, 1) + cw_ref[0:1, :] * _shift_up(d_u, nxt, 2)
            dproj_sc[:, 1024:1536] = (d_m * ccx_ref[:, CONV_WIDTH:]).astype(MM)
            dproj_sc[:, 1536:2048] = (d_m * ccx_ref[:, :CONV_WIDTH]).astype(MM)
            dproj_sc[:, 2048:3072] = dcbzc_ref[...]

        def stage2(dproj_sc):
            g_in = g_in_ref[...]
            xv = x_ref[...]
            r_in = _rstd(xv, D_MODEL)
            xh = xv * r_in
            hb = (xh * g_in).astype(MM)
            d_proj = dproj_sc[...]
            dwin_sc[...] += _tn(hb, d_proj)
            d_h = _nt(d_proj, w_in_ref[...])
            dgin_ref[...] += jnp.sum(d_h * xh, axis=0, keepdims=True)
            gx_ref[...] = dx2_ref[...] + _rms_bwd(d_h, xh, r_in, g_in, D_MODEL)

        @pl.when(i % 2 == 0)
        def _():
            stage2(dproj_b)
            stage1(dproj_a)

        @pl.when(i % 2 == 1)
        def _():
            stage2(dproj_a)
            stage1(dproj_b)

        @pl.when(i == nt)
        def _():
            cp = pltpu.make_async_copy(dwin_sc, dwin_hbm, sem)
            cp.start()
            cp.wait()

    cur = lambda i: jnp.minimum(i, nt - 1)
    prv = lambda i: jnp.maximum(i - 1, 0)
    row = lambda w, blk=0: pl.BlockSpec((T, w), lambda i: (cur(i), blk))
    row2 = lambda w: pl.BlockSpec((T, w), lambda i: (prv(i), 0))
    acc = lambda shape: pl.BlockSpec(shape, lambda i: (0, 0))
    tb = T // 8
    sds = jax.ShapeDtypeStruct
    return pl.pallas_call(
        body, name="bwd_proj", grid=(nt + 1,),
        out_shape=[sds((S, D_MODEL), f32), sds((D_MODEL, PROJ_PAD), f32), sds((Q_LORA, QK_PAD), f32),
                   sds((KV_LORA, 2 * ATTN_WIDTH), f32), sds((1, D_MODEL), f32), sds((1, Q_LORA), f32), sds((1, KV_LORA), f32),
                   sds((1, HEAD_PAD), f32), sds((1, HEAD_PAD), f32)],
        in_specs=[row2(D_MODEL), row(512, 0), row(2 * CONV_WIDTH, 1), row(QK_PAD), row(QK_PAD), row(ATTN_WIDTH),
                  row(ATTN_WIDTH), row(2 * CONV_WIDTH), row(CONV_WIDTH),
                  pl.BlockSpec((8, CONV_WIDTH), lambda i: (jnp.minimum((cur(i) + 1) * tb, S // 8 - 1), 0)),
                  row2(D_MODEL), row(LANES), row(LANES), row(LANES),
                  _const((1, D_MODEL)), _const((D_MODEL, PROJ_PAD)), _const((1, Q_LORA)), _const((Q_LORA, QK_PAD)),
                  _const((1, KV_LORA)), _const((KV_LORA, 2 * ATTN_WIDTH)), _const((1, HEAD_PAD)), _const((1, HEAD_PAD)),
                  _const((8, CONV_WIDTH))],
        out_specs=[row2(D_MODEL), pl.BlockSpec(memory_space=pl.ANY), acc((Q_LORA, QK_PAD)), acc((KV_LORA, 2 * ATTN_WIDTH)),
                   acc((1, D_MODEL)), acc((1, Q_LORA)), acc((1, KV_LORA)), acc((1, HEAD_PAD)), acc((1, HEAD_PAD))],
        scratch_shapes=[pltpu.VMEM((T, PROJ_PAD), MM), pltpu.VMEM((T, PROJ_PAD), MM), pltpu.VMEM((T, QK_PAD), MM),
                        pltpu.VMEM((T, 2 * ATTN_WIDTH), MM), pltpu.VMEM((D_MODEL, PROJ_PAD), f32), pltpu.SemaphoreType.DMA],
        compiler_params=_cparams(("arbitrary",)),
    )(x, proj, proj, d_q, d_k, d_v, d_za, d_cbzc, d_u, d_u, d_x2, *tabs, g_in, w_in_p, g_cq, w_uq_p, g_ckv, w_ukv_p,
      g_q_p, g_k_p, conv_w8)


def _all_gather(block, name, in_vmem):
    m_per, n = block.shape

    def body(x_ref, out_ref, send_sems, recv_sems, local_sem):
        start, finish = _two_level_gather(x_ref, out_ref, send_sems, recv_sems, local_sem)
        start()
        finish()

    space = pltpu.VMEM if in_vmem else pl.ANY
    return pl.pallas_call(
        body, name=name, out_shape=jax.ShapeDtypeStruct((N_DEV * m_per, n), block.dtype),
        in_specs=[pl.BlockSpec(memory_space=space)], out_specs=pl.BlockSpec(memory_space=space),
        scratch_shapes=list(GATHER_SEMS),
    )(block)


def _sibling_exchange(pack):
    nchip, _, R, n = pack.shape

    def body(p_ref, got_ref, send_sems, recv_sems):
        x, y, c = lax.axis_index("x"), lax.axis_index("y"), lax.axis_index("c")
        copies = [pltpu.make_async_remote_copy(src_ref=p_ref.at[t, 1 - c], dst_ref=got_ref.at[t], send_sem=send_sems.at[t],
                                               recv_sem=recv_sems.at[t], device_id=(x, y, 1 - c), device_id_type=MESH)
                  for t in range(nchip)]
        for cp in copies:
            cp.start()
        for cp in copies:
            cp.wait()

    return pl.pallas_call(
        body, name="rs_sibling", out_shape=jax.ShapeDtypeStruct((nchip, R, n), pack.dtype),
        in_specs=[pl.BlockSpec(memory_space=pl.ANY)], out_specs=pl.BlockSpec(memory_space=pl.ANY),
        scratch_shapes=[pltpu.SemaphoreType.DMA((nchip,)), pltpu.SemaphoreType.DMA((nchip,))],
    )(pack)


def _chip_exchange(part):
    nchip, R, n = part.shape

    def body(p_ref, got_ref, send_sems, recv_sems, local_sem):
        x, y, c = lax.axis_index("x"), lax.axis_index("y"), lax.axis_index("c")
        my_chip = 2 * x + y
        chips = [(1 - x, y), (x, 1 - y), (1 - x, 1 - y)]
        mine = pltpu.make_async_copy(p_ref.at[my_chip], got_ref.at[my_chip], local_sem)
        mine.start()
        sends = []
        for k, (tx, ty) in enumerate(chips):
            cp = pltpu.make_async_remote_copy(src_ref=p_ref.at[2 * tx + ty], dst_ref=got_ref.at[my_chip],
                                              send_sem=send_sems.at[k], recv_sem=recv_sems.at[k],
                                              device_id=(tx, ty, c), device_id_type=MESH)
            cp.start()
            sends.append(cp)
        for k, (sx, sy) in enumerate(chips):
            pltpu.make_async_remote_copy(src_ref=p_ref.at[my_chip], dst_ref=got_ref.at[2 * sx + sy],
                                         send_sem=send_sems.at[k], recv_sem=recv_sems.at[k],
                                         device_id=(sx, sy, c), device_id_type=MESH).wait_recv()
        for cp in sends:
            cp.wait_send()
        mine.wait()

    return pl.pallas_call(
        body, name="rs_chips", out_shape=jax.ShapeDtypeStruct((nchip, R, n), part.dtype),
        in_specs=[pl.BlockSpec(memory_space=pl.ANY)], out_specs=pl.BlockSpec(memory_space=pl.ANY),
        scratch_shapes=[pltpu.SemaphoreType.DMA((3,)), pltpu.SemaphoreType.DMA((3,)), pltpu.SemaphoreType.DMA],
    )(part)


def _add_blocks(pack, from_sib, my_c):
    nb, _, R, n = pack.shape

    def body(c_ref, a_ref, b_ref, o_ref):
        o_ref[...] = (a_ref[0] + b_ref[...]).astype(o_ref.dtype)

    return pl.pallas_call(
        body, name="rs_add", out_shape=jax.ShapeDtypeStruct((nb, R, n), RS_DTYPE),
        grid_spec=pltpu.PrefetchScalarGridSpec(
            num_scalar_prefetch=1, grid=(nb,),
            in_specs=[pl.BlockSpec((1, 1, R, n), lambda i, c: (i, c[0], 0, 0)), pl.BlockSpec((1, R, n), lambda i, c: (i, 0, 0))],
            out_specs=pl.BlockSpec((1, R, n), lambda i, c: (i, 0, 0))),
        compiler_params=_cparams(("parallel",)))(my_c, pack, from_sib)


def _sum_slabs(got, name):
    nb, R, n = got.shape

    def body(g_ref, o_ref):
        tot = g_ref[0].astype(f32)
        for d in range(1, nb):
            tot = tot + g_ref[d].astype(f32)
        o_ref[...] = tot

    return pl.pallas_call(body, name=name, grid=(n // LANES,), out_shape=jax.ShapeDtypeStruct((R, n), f32),
                          in_specs=[pl.BlockSpec((nb, R, LANES), lambda j: (0, 0, j))], out_specs=pl.BlockSpec((R, LANES), lambda j: (0, j)),
                          compiler_params=_cparams(("parallel",)))(got)


def _adamw_all(ws, gs, ms, vs, gain_parts):
    n = len(ws)
    ng = len(GAIN_SLOTS)

    def body(*refs):
        gp_ref = refs[0]
        w_refs = refs[1:1 + n]
        g_refs = refs[1 + n:1 + 2 * n - ng]
        m_refs = refs[1 + 2 * n - ng:1 + 3 * n - ng]
        v_refs = refs[1 + 3 * n - ng:1 + 4 * n - ng]
        outs = refs[1 + 4 * n - ng:]
        gsum_ref, loss_ref = outs[0], outs[1]
        gg_refs = outs[2:2 + ng]
        d_refs, nm_refs, nv_refs = (outs[2 + ng + k * n:2 + ng + (k + 1) * n] for k in range(3))
        tot = gp_ref[0]
        for d in range(1, N_DEV):
            tot = tot + gp_ref[d]
        gsum_ref[...] = tot
        loss_ref[...] = gsum_ref[GAIN_ROWS - 1:GAIN_ROWS, :]
        for k in range(n):
            if k < ng:
                r0, width = GAIN_SLOTS[k]
                for r in range(width // LANES):
                    gg_refs[k][:, r * LANES:(r + 1) * LANES] = gsum_ref[r0 + r:r0 + r + 1, :]
                g = gg_refs[k][...]
            else:
                g = g_refs[k - ng][...]
            w = w_refs[k][...]
            m = ADAM_B1 * m_refs[k][...] + (1.0 - ADAM_B1) * g
            v = ADAM_B2 * v_refs[k][...] + (1.0 - ADAM_B2) * (g * g)
            m_hat = m / (1.0 - ADAM_B1 ** ADAM_STEP)
            v_hat = v / (1.0 - ADAM_B2 ** ADAM_STEP)
            d_refs[k][...] = -ADAM_LR * (m_hat / (jnp.sqrt(v_hat) + ADAM_EPS) + ADAM_WD * w)
            nm_refs[k][...] = m
            nv_refs[k][...] = v

    sds = jax.ShapeDtypeStruct
    like = [sds(w.shape, f32) for w in ws]
    out_shape = [sds((GAIN_ROWS, LANES), f32), sds((1, LANES), f32)] + like[:ng] + like * 3
    full = lambda a: pl.BlockSpec(a.shape, lambda i: (0,) * len(a.shape))
    operands = [gain_parts, *ws, *gs, *ms, *vs]
    outs = pl.pallas_call(
        body, name="adamw", grid=(1,), out_shape=out_shape, in_specs=[full(a) for a in operands],
        out_specs=[full(a) for a in out_shape], compiler_params=_cparams(("arbitrary",)),
    )(*operands)
    loss = outs[1]
    gg = outs[2:2 + ng]
    deltas, new_m, new_v = (outs[2 + ng + k * n:2 + ng + (k + 1) * n] for k in range(3))
    return loss, gg, deltas, new_m, new_v


GAIN_ROWS = 32
GAIN_SLOTS = [(0, 1024), (8, 256), (10, 128), (11, 256), (13, 256), (15, 512), (19, 512), (23, 1024)]

SH_IN = IN_TOTAL // N_DEV
SLAB1_COLS = 384
UQ_ROWS = Q_LORA * (N_HEADS * QK_DIM // N_DEV) // SLAB1_COLS
UKV_ROWS = 48
SLAB1_ROWS = D_MODEL + UQ_ROWS + UKV_ROWS
SH_ROWS = D_MODEL // N_DEV
PL_ROWS = PLE_DIM * (D_MODEL // N_DEV) // D_MODEL
CONV_ROWS = 16
SLAB2_ROWS = 2 * SH_ROWS + PL_ROWS + CONV_ROWS
IN_SEGMENTS = [(0, 448, 0), (448, 960, 512), (1472, 2496, 1024), (960, 1472, 2048), (2496, 3008, 2560)]


def _flat_rows(a, rows, cols):
    lead = a.shape[:-2]
    flat = a.reshape(lead + (-1,))
    pad = [(0, 0)] * len(lead) + [(0, rows * cols - flat.shape[-1])]
    return jnp.pad(flat, pad).reshape(lead + (rows, cols))


def _slab1(in_sh, uq_sh, ukv_sh):
    pad = [(0, 0)] * (in_sh.ndim - 1) + [(0, SLAB1_COLS - in_sh.shape[-1])]
    return jnp.concatenate([jnp.pad(in_sh, pad), _flat_rows(uq_sh, UQ_ROWS, SLAB1_COLS), _flat_rows(ukv_sh, UKV_ROWS, SLAB1_COLS)],
                           axis=-2)


def _slab2(o_sh, plg_sh, pl_sh, conv_sh):
    return jnp.concatenate([o_sh, plg_sh, _flat_rows(pl_sh, PL_ROWS, D_MODEL), _flat_rows(conv_sh, CONV_ROWS, D_MODEL)], axis=-2)


def _in_plan():
    src = [None] * PROJ_PAD
    for a, b, start in IN_SEGMENTS:
        for o in range(a, b):
            src[start + o - a] = (o // SH_IN, o % SH_IN)
    plan = []
    for t in range(PROJ_PAD // LANES):
        runs, j = [], 0
        key_of = lambda jj: None if src[t * LANES + jj] is None else (
            src[t * LANES + jj][0], src[t * LANES + jj][1] // LANES, src[t * LANES + jj][1] % LANES - jj)
        while j < LANES:
            key = key_of(j)
            lo = j
            while j < LANES and key_of(j) == key:
                j += 1
            if key is not None:
                runs.append(key + (lo, j))
        plan.append(runs)
    return plan


def _assemble_w_in(g1):
    plan = _in_plan()
    rows = 256

    def body(g_ref, out_ref):
        src_lane = lax.broadcasted_iota(jnp.int32, (LANES, LANES), 0)
        out_lane = lax.broadcasted_iota(jnp.int32, (LANES, LANES), 1)
        for t, runs in enumerate(plan):
            acc = jnp.zeros((rows, LANES), f32)
            for d, tile, shift, lo, hi in runs:
                pick = ((src_lane == out_lane + shift) & (out_lane >= lo) & (out_lane < hi)).astype(bf16)
                acc = acc + _nn(g_ref[d, :, tile * LANES:(tile + 1) * LANES], pick)
            out_ref[:, t * LANES:(t + 1) * LANES] = acc.astype(out_ref.dtype)

    return pl.pallas_call(
        body, name="w_in_columns", grid=(D_MODEL // rows,), out_shape=jax.ShapeDtypeStruct((D_MODEL, PROJ_PAD), MM),
        in_specs=[pl.BlockSpec((N_DEV, rows, SLAB1_COLS), lambda i: (0, i, 0))],
        out_specs=pl.BlockSpec((rows, PROJ_PAD), lambda i: (i, 0)),
        compiler_params=_cparams(("parallel",)))(g1)


def _by_device(a, width):
    return a.reshape(a.shape[0], N_DEV, width).transpose(1, 0, 2)


def _from_devices(a):
    return a.transpose(1, 0, 2).reshape(a.shape[1], -1)


def kernel(x, p, positions, g_in, w_in, g_cq, w_uq, g_ckv, w_ukv, g_q, g_k, conv_w, g_oa, g_oc, w_o, w_pl, w_plg, g_pl, loss_target, m_g_in, m_w_in, m_g_cq, m_w_uq, m_g_ckv, m_w_ukv, m_g_q, m_g_k, m_conv_w, m_g_oa, m_g_oc, m_w_o, m_w_pl, m_w_plg, m_g_pl, v_g_in, v_w_in, v_g_cq, v_w_uq, v_g_ckv, v_w_ukv, v_g_q, v_g_k, v_conv_w, v_g_oa, v_g_oc, v_w_o, v_w_pl, v_w_plg, v_g_pl):
    S = x.shape[1]
    nd = N_DEV
    xs, ps, tgt = x[0], p[0, 0], loss_target[0]

    slab1 = _slab1(w_in[0], w_uq[0], w_ukv[0]).astype(bf16)
    inv_freq = 1.0 / (ROPE_THETA ** (jnp.arange(0, ROPE_DIM, 2, dtype=f32) / ROPE_DIM))
    inv_row = jnp.concatenate([inv_freq, inv_freq, jnp.zeros((LANES - ROPE_DIM,), f32)]).reshape(1, LANES)
    posf = jnp.broadcast_to(positions[0].astype(f32)[:, None], (S, LANES))
    tabs, g1 = _rope_tables(posf, inv_row, slab1)
    g1 = g1.reshape(nd, SLAB1_ROWS, SLAB1_COLS)
    conv_bits = lax.bitcast_convert_type(conv_w[0], bf16).reshape(CONV_K, -1)
    slab2 = _slab2(w_o[0].astype(bf16), w_plg[0].astype(bf16), w_pl[0].astype(bf16), conv_bits)

    w_in_p = _assemble_w_in(g1)
    wuq = _from_devices(g1[:, D_MODEL:D_MODEL + UQ_ROWS].reshape(nd, Q_LORA, -1)).reshape(Q_LORA, N_HEADS, QK_DIM)
    w_uq_p = jnp.pad(wuq, ((0, 0), (0, 0), (0, HEAD_PAD - QK_DIM))).reshape(Q_LORA, QK_PAD)
    ukv_flat = g1[:, D_MODEL + UQ_ROWS:].reshape(nd, -1)[:, :KV_LORA * V_DIM]
    wukv = _from_devices(ukv_flat.reshape(nd, KV_LORA, V_DIM)).reshape(KV_LORA, N_HEADS, 2, NOPE_DIM)
    w_ukv_p = wukv.transpose(0, 2, 1, 3).reshape(KV_LORA, 2 * ATTN_WIDTH)
    w_in_p, w_uq_p, w_ukv_p = (w.astype(MM) for w in (w_in_p, w_uq_p, w_ukv_p))

    g_q_p = jnp.pad(g_q, ((0, 0), (0, HEAD_PAD - QK_DIM)))
    g_k_p = jnp.pad(g_k, ((0, 0), (0, HEAD_PAD - QK_DIM)))

    proj, q, k, v = _fwd_proj(xs, tabs, g_in, w_in_p, g_cq, w_uq_p, g_ckv, w_ukv_p, g_q_p, g_k_p)
    o, lse, g2 = _flash_fwd(q, k, v, slab2)
    w_o_f = g2[:, 0:SH_ROWS].reshape(D_MODEL, D_MODEL).astype(MM)
    w_plg_f = g2[:, SH_ROWS:2 * SH_ROWS].reshape(D_MODEL, D_MODEL).astype(MM)
    w_pl_f = _from_devices(g2[:, 2 * SH_ROWS:2 * SH_ROWS + PL_ROWS].reshape(nd, PLE_DIM, -1)).astype(MM)
    cv_bits = g2[:, 2 * SH_ROWS + PL_ROWS:].reshape(nd, -1)[:, :CONV_K * (CONV_WIDTH // nd) * 2]
    conv_full = _from_devices(lax.bitcast_convert_type(cv_bits.reshape(nd, CONV_K, CONV_WIDTH // nd, 2), f32))
    conv_w8 = jnp.pad(conv_full, ((0, 8 - CONV_K), (0, 0)))
    (d_x2, d_o, delta, d_za, d_cbzc, d_u, dw_o, dw_plg, dw_pl, dcw, dg_oa, dg_oc, dg_pl, loss_part) = _mid(
        o, proj, xs, ps, tgt, w_o_f, w_plg_f, w_pl_f, conv_w8, g_oa, g_oc, g_pl)

    slabs_a = _slab2(dw_o.reshape(nd, SH_ROWS, D_MODEL), dw_plg.reshape(nd, SH_ROWS, D_MODEL), _by_device(dw_pl, D_MODEL // nd),
                     _by_device(dcw[:CONV_K], CONV_WIDTH // nd)).astype(RS_DTYPE)
    d_q, d_k, d_v, got_a = _flash_bwd(q, k, v, d_o, lse, delta, slabs_a)
    shard_a = _sum_slabs(got_a, "rs_sum8")
    (grad_x, dw_in_p, dw_uq_p, dw_ukv_p, dg_in, dg_cq, dg_ckv, dg_q_p, dg_k_p) = _bwd_proj(
        xs, proj, d_q, d_k, d_v, d_za, d_cbzc, d_u, d_x2, tabs, g_in, w_in_p, g_cq, w_uq_p, g_ckv, w_ukv_p, g_q_p, g_k_p, conv_w8)

    def dw_in_cols(a, b):
        return [dw_in_p[:, start + max(a, s) - s:start + min(b, e) - s] for s, e, start in sorted(IN_SEGMENTS) if max(a, s) < min(b, e)]

    no_cols = jnp.zeros((D_MODEL, SLAB1_COLS - SH_IN), f32)
    in_shards = jnp.stack([jnp.concatenate(dw_in_cols(SH_IN * d, SH_IN * (d + 1)) + [no_cols], axis=1) for d in range(nd)])
    dw_uq = dw_uq_p.reshape(Q_LORA, N_HEADS, HEAD_PAD)[:, :, :QK_DIM].reshape(Q_LORA, N_HEADS * QK_DIM)
    dw_ukv = dw_ukv_p.reshape(KV_LORA, 2, N_HEADS, NOPE_DIM).transpose(0, 2, 1, 3).reshape(KV_LORA, 2 * ATTN_WIDTH)
    slabs_b = _slab1(in_shards, _by_device(dw_uq, N_HEADS * QK_DIM // nd), _by_device(dw_ukv, 2 * ATTN_WIDTH // nd))
    slabs_b = slabs_b.reshape(4, 2, SLAB1_ROWS, SLAB1_COLS)
    my_c = lax.axis_index("c").astype(jnp.int32).reshape(1)
    chip_part = _add_blocks(slabs_b, _sibling_exchange(slabs_b), my_c)
    shard_b = _sum_slabs(_chip_exchange(chip_part), "rs_sum")
    grads_w = [shard_b[:D_MODEL, :SH_IN], shard_b[D_MODEL:D_MODEL + UQ_ROWS].reshape(Q_LORA, -1),
               shard_b[D_MODEL + UQ_ROWS:].reshape(-1)[:KV_LORA * V_DIM].reshape(KV_LORA, V_DIM),
               shard_a[2 * SH_ROWS + PL_ROWS:].reshape(-1)[:CONV_K * CONV_WIDTH // nd].reshape(CONV_K, -1),
               shard_a[0:SH_ROWS], shard_a[2 * SH_ROWS:2 * SH_ROWS + PL_ROWS].reshape(PLE_DIM, -1), shard_a[SH_ROWS:2 * SH_ROWS]]

    gains_part = [dg_in, dg_cq, dg_ckv, dg_q_p, dg_k_p, dg_oa, dg_oc, dg_pl]
    gflat = jnp.concatenate([g.reshape(-1) for g in gains_part] + [jnp.zeros(((GAIN_ROWS - 1) * LANES - 3968,), f32),
                                                                    loss_part.reshape(-1)])
    gain_parts = _all_gather(gflat.reshape(GAIN_ROWS, LANES), "ag_gains", in_vmem=True).reshape(nd, GAIN_ROWS, LANES)

    gains = [g_in, g_cq, g_ckv, g_q_p, g_k_p, g_oa, g_oc, g_pl]
    padq = lambda a: jnp.pad(a, ((0, 0), (0, HEAD_PAD - QK_DIM)))
    m_gains = [m_g_in, m_g_cq, m_g_ckv, padq(m_g_q), padq(m_g_k), m_g_oa, m_g_oc, m_g_pl]
    v_gains = [v_g_in, v_g_cq, v_g_ckv, padq(v_g_q), padq(v_g_k), v_g_oa, v_g_oc, v_g_pl]
    ws = gains + [w_in, w_uq, w_ukv, conv_w, w_o, w_pl, w_plg]
    ms = m_gains + [m_w_in, m_w_uq, m_w_ukv, m_conv_w, m_w_o, m_w_pl, m_w_plg]
    vs = v_gains + [v_w_in, v_w_uq, v_w_ukv, v_conv_w, v_w_o, v_w_pl, v_w_plg]
    grads_w = [g[None] for g in grads_w]
    loss_row, gg, deltas, new_m, new_v = _adamw_all(ws, grads_w, ms, vs, gain_parts)
    loss = loss_row[0, 0]

    def ordered(gl, wl):
        g_in_, g_cq_, g_ckv_, g_q_, g_k_, g_oa_, g_oc_, g_pl_ = gl
        g_q_, g_k_ = g_q_[:, :QK_DIM], g_k_[:, :QK_DIM]
        w_in_, w_uq_, w_ukv_, cw_, w_o_, w_pl_, w_plg_ = wl
        return [g_in_, w_in_, g_cq_, w_uq_, g_ckv_, w_ukv_, g_q_, g_k_, cw_, g_oa_, g_oc_, w_o_, w_pl_, w_plg_, g_pl_]

    ng = len(gains)
    outs = [loss, grad_x[None]]
    outs += ordered(gg, grads_w)
    for lst in (deltas, new_m, new_v):
        outs += ordered(lst[:ng], lst[ng:])
    return tuple(outs)
```

```python
import functools
import math

import jax
import jax.numpy as jnp
from jax import lax
from jax.experimental import pallas as pl
from jax.experimental.pallas import tpu as pltpu

f32 = jnp.float32
bf16 = jnp.bfloat16
MM = jnp.bfloat16
RS_DTYPE = jnp.bfloat16

D_MODEL = 1024
PLE_DIM = 256
N_HEADS = 4
NOPE_DIM = 128
ROPE_DIM = 64
V_DIM = 128
QK_DIM = NOPE_DIM + ROPE_DIM
Q_LORA = 256
KV_LORA = 128
ATTN_WIDTH = N_HEADS * V_DIM
CONV_WIDTH = D_MODEL - ATTN_WIDTH
CONV_K = 3
ROPE_THETA = 10000.0
RMS_EPS = 1e-6
NEG_INF = -1e30
IN_TOTAL = Q_LORA + KV_LORA + ROPE_DIM + ATTN_WIDTH + 4 * CONV_WIDTH
ADAM_LR = 0.001
ADAM_B1 = 0.9
ADAM_B2 = 0.999
ADAM_EPS = 1e-08
ADAM_WD = 0.01
ADAM_STEP = 10

N_DEV = 8
LANES = 128
HEAD_PAD = 256
QK_PAD = N_HEADS * HEAD_PAD
PROJ_PAD = 3072
TS = 512
TS_M = 256
TS_E = 256
TQ = 512
VMEM_LIMIT = 58 * 1024 * 1024
MESH = pl.DeviceIdType.MESH
QK_SCALE = 1.0 / math.sqrt(QK_DIM)
SCALE_LOG2E = QK_SCALE * math.log2(math.e)


def _cparams(sem=None):
    return pltpu.CompilerParams(dimension_semantics=sem, vmem_limit_bytes=VMEM_LIMIT)


def _nt(a, b):
    return lax.dot_general(a, b, (((1,), (1,)), ((), ())), preferred_element_type=f32)


def _tn(a, b):
    return lax.dot_general(a, b, (((0,), (0,)), ((), ())), preferred_element_type=f32)


def _nn(a, b):
    return jnp.dot(a, b, preferred_element_type=f32)


def _rstd(x, n):
    return lax.rsqrt(jnp.sum(x * x, axis=-1, keepdims=True) * (1.0 / n) + RMS_EPS)


def _rms_bwd(dy, xhat, r, g, n):
    u = dy * g
    return r * (u - xhat * (jnp.sum(u * xhat, axis=-1, keepdims=True) * (1.0 / n)))


def _rope(t, c, s1, s2):
    return t * c + pltpu.roll(t, 32, 1) * s1 + pltpu.roll(t, 96, 1) * s2


def _rope_t(d, c, s1, s2):
    return d * c - pltpu.roll(d, 96, 1) * s2 - pltpu.roll(d, 32, 1) * s1


def _const(shape):
    return pl.BlockSpec(shape, lambda *_: (0,) * len(shape), pipeline_mode=pl.Buffered(1))


def _two_level_gather(x_ref, out_ref, send_sems, recv_sems, local_sem):
    m_per = x_ref.shape[0]
    x, y, c = lax.axis_index("x"), lax.axis_index("y"), lax.axis_index("c")
    me, sibling = (x, y, c), (x, y, 1 - c)
    chips = [(1 - x, y), (x, 1 - y), (1 - x, 1 - y)]

    def rows(px, py, pc):
        return out_ref.at[pl.ds((4 * px + 2 * py + pc) * m_per, m_per), :]

    def copy(k, blk, to, src=None):
        return pltpu.make_async_remote_copy(
            src_ref=rows(*blk) if src is None else src, dst_ref=rows(*blk),
            send_sem=send_sems.at[k], recv_sem=recv_sems.at[k], device_id=to, device_id_type=MESH)

    mine = pltpu.make_async_copy(x_ref, rows(*me), local_sem)
    first = [copy(0, me, sibling, src=x_ref)] + [copy(1 + j, me, (*chip, c), src=x_ref) for j, chip in enumerate(chips)]
    passed = [copy(4 + j, (*chip, c), sibling) for j, chip in enumerate(chips)]

    def start():
        mine.start()
        for cp in first:
            cp.start()

    def finish():
        for j, chip in enumerate(chips):
            copy(1 + j, (*chip, c), me).wait_recv()
            passed[j].start()
        copy(0, sibling, me).wait_recv()
        for j, chip in enumerate(chips):
            copy(4 + j, (*chip, 1 - c), me).wait_recv()
        for cp in first + passed:
            cp.wait_send()
        mine.wait()

    return start, finish


GATHER_SEMS = [pltpu.SemaphoreType.DMA((7,)), pltpu.SemaphoreType.DMA((7,)), pltpu.SemaphoreType.DMA]


def _rope_tables(posf, inv_freq, block):
    S = posf.shape[0]
    t = min(S, TS)
    nt = S // t

    def body(pos_ref, f_ref, blk_ref, c_ref, s1_ref, s2_ref, all_ref, send_sems, recv_sems, local_sem):
        i = pl.program_id(0)
        start, finish = _two_level_gather(blk_ref, all_ref, send_sems, recv_sems, local_sem)

        @pl.when(i == 0)
        def _():
            start()

        ang = pos_ref[...] * f_ref[...]
        lane = lax.broadcasted_iota(jnp.int32, ang.shape, 1)
        cs, sn = jnp.cos(ang), jnp.sin(ang)
        c_ref[...] = jnp.where(lane < ROPE_DIM, cs, 0.0)
        s1_ref[...] = jnp.where((lane >= ROPE_DIM // 2) & (lane < ROPE_DIM), sn, 0.0)
        s2_ref[...] = jnp.where(lane < ROPE_DIM // 2, -sn, 0.0)

        @pl.when(i == nt - 1)
        def _():
            finish()

    spec = pl.BlockSpec((t, LANES), lambda i: (i, 0))
    hbm = pl.BlockSpec(memory_space=pl.ANY)
    outs = pl.pallas_call(
        body, name="rope_tables", grid=(nt,),
        out_shape=[jax.ShapeDtypeStruct((S, LANES), f32)] * 3 + [jax.ShapeDtypeStruct((N_DEV * block.shape[0], block.shape[1]), block.dtype)],
        in_specs=[spec, _const((1, LANES)), hbm], out_specs=[spec] * 3 + [hbm],
        scratch_shapes=list(GATHER_SEMS),
        compiler_params=_cparams(("arbitrary",)),
    )(posf, inv_freq, block)
    return outs[:3], outs[3]


def _fwd_proj(x, tabs, g_in, w_in_p, g_cq, w_uq_p, g_ckv, w_ukv_p, g_q_p, g_k_p):
    S = x.shape[0]
    T = min(TS, S)
    nt = S // T
    LAT = Q_LORA + KV_LORA + LANES

    def body(x_ref, c_ref, s1_ref, s2_ref, g_in_ref, w_in_ref, g_cq_ref, w_uq_ref, g_ckv_ref, w_ukv_ref,
             g_q_ref, g_k_ref, proj_ref, q_ref, k_ref, v_ref, lat_a, lat_b):
        i = pl.program_id(0)

        @pl.when(i == 0)
        def _():
            lat_b[...] = jnp.zeros(lat_b.shape, f32)

        def stage1(lat):
            xv = x_ref[...]
            h = (xv * _rstd(xv, D_MODEL) * g_in_ref[...]).astype(MM)
            proj_ref[...] = _nn(h, w_in_ref[...])
            lat[...] = proj_ref[:, 0:LAT]

        def stage2(lat):
            c, s1, s2 = c_ref[...], s1_ref[...], s2_ref[...]
            g_q, g_k = g_q_ref[...], g_k_ref[...]
            c_q = lat[:, 0:Q_LORA]
            n_cq = (c_q * _rstd(c_q, Q_LORA) * g_cq_ref[...]).astype(MM)
            q_raw = _nn(n_cq, w_uq_ref[...])
            for hd in range(N_HEADS):
                qh = q_raw[:, hd * HEAD_PAD:(hd + 1) * HEAD_PAD]
                qn = qh * _rstd(qh, QK_DIM) * g_q
                q_ref[:, hd * HEAD_PAD:hd * HEAD_PAD + LANES] = qn[:, :LANES].astype(MM)
                q_ref[:, hd * HEAD_PAD + LANES:(hd + 1) * HEAD_PAD] = _rope(qn[:, LANES:], c, s1, s2).astype(MM)

            c_kv = lat[:, Q_LORA:Q_LORA + KV_LORA]
            n_ckv = (c_kv * _rstd(c_kv, KV_LORA) * g_ckv_ref[...]).astype(MM)
            kv_raw = _nn(n_ckv, w_ukv_ref[...])
            kpe = lat[:, Q_LORA + KV_LORA:LAT]
            ss_pe = jnp.sum(kpe * kpe, axis=-1, keepdims=True)
            kr = _rope(kpe * g_k[:, LANES:], c, s1, s2)
            for hd in range(N_HEADS):
                kn = kv_raw[:, hd * LANES:(hd + 1) * LANES]
                rk = lax.rsqrt((jnp.sum(kn * kn, axis=-1, keepdims=True) + ss_pe) * (1.0 / QK_DIM) + RMS_EPS)
                k_ref[:, hd * HEAD_PAD:hd * HEAD_PAD + LANES] = (kn * rk * g_k[:, :LANES]).astype(MM)
                k_ref[:, hd * HEAD_PAD + LANES:(hd + 1) * HEAD_PAD] = (kr * rk).astype(MM)
            v_ref[...] = kv_raw[:, N_HEADS * NOPE_DIM:].astype(MM)

        @pl.when(i % 2 == 0)
        def _():
            stage2(lat_b)
            stage1(lat_a)

        @pl.when(i % 2 == 1)
        def _():
            stage2(lat_a)
            stage1(lat_b)

    cur = lambda w: pl.BlockSpec((T, w), lambda i: (jnp.minimum(i, nt - 1), 0))
    prv = lambda w: pl.BlockSpec((T, w), lambda i: (jnp.maximum(i - 1, 0), 0))
    return pl.pallas_call(
        body, name="fwd_proj", grid=(nt + 1,),
        out_shape=[jax.ShapeDtypeStruct((S, PROJ_PAD), f32), jax.ShapeDtypeStruct((S, QK_PAD), MM),
                   jax.ShapeDtypeStruct((S, QK_PAD), MM), jax.ShapeDtypeStruct((S, ATTN_WIDTH), MM)],
        in_specs=[cur(D_MODEL), prv(LANES), prv(LANES), prv(LANES), _const((1, D_MODEL)), _const((D_MODEL, PROJ_PAD)),
                  _const((1, Q_LORA)), _const((Q_LORA, QK_PAD)), _const((1, KV_LORA)), _const((KV_LORA, 2 * ATTN_WIDTH)),
                  _const((1, HEAD_PAD)), _const((1, HEAD_PAD))],
        out_specs=[cur(PROJ_PAD), prv(QK_PAD), prv(QK_PAD), prv(ATTN_WIDTH)],
        scratch_shapes=[pltpu.VMEM((T, LAT), f32), pltpu.VMEM((T, LAT), f32)],
        compiler_params=_cparams(("arbitrary",)),
    )(x, *tabs, g_in, w_in_p, g_cq, w_uq_p, g_ckv, w_ukv_p, g_q_p, g_k_p)


def _exchange(src_of, dst_ref, send_sems, recv_sems, local_sem):
    x, y, c = lax.axis_index("x"), lax.axis_index("y"), lax.axis_index("c")
    me = 4 * x + 2 * y + c
    remote = []
    for k in range(1, N_DEV):
        px = 1 - x if (k >> 2) & 1 else x
        py = 1 - y if (k >> 1) & 1 else y
        pc = 1 - c if k & 1 else c
        remote.append(pltpu.make_async_remote_copy(
            src_ref=src_of(4 * px + 2 * py + pc), dst_ref=dst_ref.at[me], send_sem=send_sems.at[k - 1],
            recv_sem=recv_sems.at[k - 1], device_id=(px, py, pc), device_id_type=MESH))
    return remote, pltpu.make_async_copy(src_of(me), dst_ref.at[me], local_sem)


def _exchange_start(copies):
    remote, local = copies
    local.start()
    for cp in remote:
        cp.start()


def _exchange_wait(copies):
    remote, local = copies
    for cp in remote:
        cp.wait_recv()
    for cp in remote:
        cp.wait_send()
    local.wait()


EXCHANGE_SEMS = [pltpu.SemaphoreType.DMA((N_DEV - 1,)), pltpu.SemaphoreType.DMA((N_DEV - 1,)), pltpu.SemaphoreType.DMA]


def _flash_fwd(q, k, v, block):
    S = q.shape[0]
    T = min(TQ, S)
    nq = S // T

    def body(q_ref, k_ref, v_ref, blk_ref, o_ref, lse_ref, all_ref, s_a, s_b, m_sc, l_sc, acc_sc, send_sems, recv_sems, local_sem):
        n = pl.program_id(1)
        gather = _exchange(lambda d: blk_ref, all_ref, send_sems, recv_sems, local_sem)

        @pl.when((pl.program_id(0) == 0) & (n == 0))
        def _():
            _exchange_start(gather)

        m_sc[...] = jnp.full(m_sc.shape, NEG_INF, f32)
        l_sc[...] = jnp.zeros(l_sc.shape, f32)
        acc_sc[...] = jnp.zeros(acc_sc.shape, f32)
        qv = q_ref[...]

        def rows(j):
            return pl.ds(pl.multiple_of(j * T, T), T)

        def scores(j, dst):
            dst[...] = _nt(k_ref[rows(j), :], qv)

        def update(src, j, masked):
            s = src[...]
            if masked:
                k_i = lax.broadcasted_iota(jnp.int32, (T, T), 0)
                q_i = lax.broadcasted_iota(jnp.int32, (T, T), 1)
                s = jnp.where(k_i <= q_i, s, NEG_INF)
            m_prev = m_sc[...]
            m_new = jnp.maximum(m_prev, jnp.max(s, axis=0, keepdims=True))
            alpha = jnp.exp2((m_prev - m_new) * SCALE_LOG2E)
            p = jnp.exp2((s - m_new[0:1, :]) * SCALE_LOG2E)
            l_sc[...] = alpha * l_sc[...] + jnp.sum(p, axis=0, keepdims=True)
            acc_sc[...] = alpha[0:1, :] * acc_sc[...] + _tn(v_ref[rows(j), :], p.astype(MM))
            m_sc[...] = m_new

        scores(0, s_a)

        def pair(t, carry):
            j = 2 * t
            scores(j + 1, s_b)
            update(s_a, j, False)
            scores(j + 2, s_a)
            update(s_b, j + 1, False)
            return carry

        lax.fori_loop(0, n // 2, pair, 0)

        @pl.when(n % 2 == 0)
        def _():
            update(s_a, n, True)

        @pl.when(n % 2 == 1)
        def _():
            scores(n, s_b)
            update(s_a, n - 1, False)
            update(s_b, n, True)

        o_ref[...] = (acc_sc[...] / l_sc[0:1, :]).T
        lse_ref[...] = m_sc[...] * SCALE_LOG2E + jnp.log2(l_sc[...])

        @pl.when((pl.program_id(0) == N_HEADS - 1) & (n == nq - 1))
        def _():
            _exchange_wait(gather)

    return pl.pallas_call(
        body, name="flash_fwd", grid=(N_HEADS, nq),
        out_shape=[jax.ShapeDtypeStruct((S, ATTN_WIDTH), f32), jax.ShapeDtypeStruct((N_HEADS * 8, S), f32),
                   jax.ShapeDtypeStruct((N_DEV,) + block.shape, block.dtype)],
        in_specs=[pl.BlockSpec((T, HEAD_PAD), lambda h, i: (i, h)),
                  pl.BlockSpec((S, HEAD_PAD), lambda h, i: (0, h)),
                  pl.BlockSpec((S, V_DIM), lambda h, i: (0, h)),
                  pl.BlockSpec(memory_space=pl.ANY)],
        out_specs=[pl.BlockSpec((T, V_DIM), lambda h, i: (i, h)), pl.BlockSpec((8, T), lambda h, i: (h, i)),
                   pl.BlockSpec(memory_space=pl.ANY)],
        scratch_shapes=[pltpu.VMEM((T, T), f32), pltpu.VMEM((T, T), f32), pltpu.VMEM((8, T), f32), pltpu.VMEM((8, T), f32),
                        pltpu.VMEM((V_DIM, T), f32)] + EXCHANGE_SEMS,
        compiler_params=_cparams(("arbitrary", "arbitrary")),
    )(q, k, v, block)


def _shift_down(m, prev8, n):
    T = m.shape[0]
    rows = lax.broadcasted_iota(jnp.int32, m.shape, 0)
    head = jnp.tile(pltpu.roll(prev8, n, 0), (T // 8, 1))
    return jnp.where(rows >= n, pltpu.roll(m, n, 0), head)


def _shift_up(m, next8, n):
    T = m.shape[0]
    rows = lax.broadcasted_iota(jnp.int32, m.shape, 0)
    tail = jnp.tile(pltpu.roll(next8, 8 - n, 0), (T // 8, 1))
    return jnp.where(rows < T - n, pltpu.roll(m, T - n, 0), tail)


def _mid(o, proj, x, p, target, w_o, w_plg, w_pl, conv_w8, g_oa, g_oc, g_pl):
    S = x.shape[0]
    T = min(TS_M, S)
    nt = S // T

    def body(o_ref, za_ref, ccx_ref, cbzc_ref, prev_ref, x_ref, p_ref, t_ref, wo_ref, wplg_ref, wpl_ref, cw_ref,
             goa_ref, goc_ref, gpl_ref,
             dx2_ref, do_ref, delta_ref, dza_ref, dcbzc_ref, du_ref,
             dwo_ref, dwplg_ref, dwpl_ref, dcw_ref, dgoa_ref, dgoc_ref, dgpl_ref, loss_ref):
        i = pl.program_id(0)

        @pl.when(i == 0)
        def _():
            for r in (dwo_ref, dwplg_ref, dwpl_ref, dcw_ref, dgoa_ref, dgoc_ref, dgpl_ref, loss_ref):
                r[...] = jnp.zeros(r.shape, f32)

        o_v = o_ref[...]
        z_a = za_ref[...]
        cc, cx = ccx_ref[:, :CONV_WIDTH], ccx_ref[:, CONV_WIDTH:]
        cb, z_c = cbzc_ref[:, :CONV_WIDTH], cbzc_ref[:, CONV_WIDTH:]
        g_oa, g_oc, g_pl = goa_ref[...], goc_ref[...], gpl_ref[...]
        w0, w1, w2 = cw_ref[0:1, :], cw_ref[1:2, :], cw_ref[2:3, :]

        sa = jax.nn.sigmoid(z_a)
        silu_a = z_a * sa
        ga = o_v * silu_a
        ra = _rstd(ga, ATTN_WIDTH)
        gha = ga * ra
        m0 = cc * cx
        prev = prev_ref[:, :CONV_WIDTH] * prev_ref[:, CONV_WIDTH:] * (i > 0).astype(f32)
        m1 = _shift_down(m0, prev, 1)
        m2 = _shift_down(m0, prev, 2)
        u = w0 * m2 + w1 * m1 + w2 * m0
        sc = jax.nn.sigmoid(z_c)
        silu_c = z_c * sc
        gc = cb * u * silu_c
        rc = _rstd(gc, CONV_WIDTH)
        ghc = gc * rc
        ycat = jnp.concatenate([gha * g_oa, ghc * g_oc], axis=-1).astype(MM)
        x2 = x_ref[...] + _nn(ycat, wo_ref[...])
        r2 = _rstd(x2, D_MODEL)
        xh2 = x2 * r2
        n2 = (xh2 * g_pl).astype(MM)
        gate = jax.nn.sigmoid(_nn(n2, wplg_ref[...]))
        p_b = p_ref[...].astype(MM)
        pw = _nn(p_b, wpl_ref[...])
        err = x2 + gate * pw - t_ref[...]
        loss_ref[...] += jnp.sum(jnp.sum(err * err, axis=-1, keepdims=True), axis=0, keepdims=True) * (0.5 / D_MODEL)
        d_out = err * (1.0 / D_MODEL)
        dwpl_ref[...] += _tn(p_b, (d_out * gate).astype(MM))
        d_glog = (d_out * pw * gate * (1.0 - gate)).astype(MM)
        dwplg_ref[...] += _tn(n2, d_glog)
        d_n2 = _nt(d_glog, wplg_ref[...])
        dgpl_ref[...] += jnp.sum(d_n2 * xh2, axis=0, keepdims=True)
        d_x2 = d_out + _rms_bwd(d_n2, xh2, r2, g_pl, D_MODEL)
        dx2_ref[...] = d_x2
        d_x2b = d_x2.astype(MM)
        dwo_ref[...] += _tn(ycat, d_x2b)
        d_ycat = _nt(d_x2b, wo_ref[...])
        d_ya, d_yc = d_ycat[:, :ATTN_WIDTH], d_ycat[:, ATTN_WIDTH:]
        dgoa_ref[...] += jnp.sum(d_ya * gha, axis=0, keepdims=True)
        dgoc_ref[...] += jnp.sum(d_yc * ghc, axis=0, keepdims=True)
        d_ga = _rms_bwd(d_ya, gha, ra, g_oa, ATTN_WIDTH)
        d_gc = _rms_bwd(d_yc, ghc, rc, g_oc, CONV_WIDTH)
        d_o = d_ga * silu_a
        do_ref[...] = d_o.astype(MM)
        dza_ref[...] = (d_ga * o_v * (sa * (1.0 + z_a * (1.0 - sa)))).astype(MM)
        for hd in range(N_HEADS):
            cols = slice(hd * V_DIM, (hd + 1) * V_DIM)
            dl = jnp.sum(d_o[:, cols] * o_v[:, cols], axis=-1, keepdims=True)
            delta_ref[hd * 8:(hd + 1) * 8, :] = jnp.broadcast_to(dl, (T, LANES)).T[0:8, :]
        d_u = d_gc * cb * silu_c
        du_ref[...] = d_u
        dcbzc_ref[:, :CONV_WIDTH] = (d_gc * u * silu_c).astype(MM)
        dcbzc_ref[:, CONV_WIDTH:] = (d_gc * cb * u * (sc * (1.0 + z_c * (1.0 - sc)))).astype(MM)
        dcw_ref[0:1, :] += jnp.sum(d_u * m2, axis=0, keepdims=True)
        dcw_ref[1:2, :] += jnp.sum(d_u * m1, axis=0, keepdims=True)
        dcw_ref[2:3, :] += jnp.sum(d_u * m0, axis=0, keepdims=True)

    row = lambda w, blk=0: pl.BlockSpec((T, w), lambda i: (i, blk))
    acc = lambda shape: pl.BlockSpec(shape, lambda i: (0, 0))
    tb = T // 8
    sds = jax.ShapeDtypeStruct
    return pl.pallas_call(
        body, name="mid", grid=(nt,),
        out_shape=[sds((S, D_MODEL), f32), sds((S, ATTN_WIDTH), MM), sds((N_HEADS * 8, S), f32), sds((S, ATTN_WIDTH), MM),
                   sds((S, 2 * CONV_WIDTH), MM), sds((S, CONV_WIDTH), f32),
                   sds((D_MODEL, D_MODEL), f32), sds((D_MODEL, D_MODEL), f32), sds((PLE_DIM, D_MODEL), f32), sds((8, CONV_WIDTH), f32),
                   sds((1, ATTN_WIDTH), f32), sds((1, CONV_WIDTH), f32), sds((1, D_MODEL), f32), sds((1, LANES), f32)],
        in_specs=[row(ATTN_WIDTH), row(ATTN_WIDTH, 1), row(2 * CONV_WIDTH, 1), row(2 * CONV_WIDTH, 2),
                  pl.BlockSpec((8, 2 * CONV_WIDTH), lambda i: (jnp.maximum(i * tb - 1, 0), 1)),
                  row(D_MODEL), row(PLE_DIM), row(D_MODEL),
                  _const((D_MODEL, D_MODEL)), _const((D_MODEL, D_MODEL)), _const((PLE_DIM, D_MODEL)), _const((8, CONV_WIDTH)),
                  _const((1, ATTN_WIDTH)), _const((1, CONV_WIDTH)), _const((1, D_MODEL))],
        out_specs=[row(D_MODEL), row(ATTN_WIDTH), pl.BlockSpec((N_HEADS * 8, T), lambda i: (0, i)), row(ATTN_WIDTH),
                   row(2 * CONV_WIDTH), row(CONV_WIDTH),
                   acc((D_MODEL, D_MODEL)), acc((D_MODEL, D_MODEL)), acc((PLE_DIM, D_MODEL)), acc((8, CONV_WIDTH)),
                   acc((1, ATTN_WIDTH)), acc((1, CONV_WIDTH)), acc((1, D_MODEL)), acc((1, LANES))],
        compiler_params=_cparams(("arbitrary",)),
    )(o, proj, proj, proj, proj, x, p, target, w_o, w_plg, w_pl, conv_w8, g_oa, g_oc, g_pl)


def _flash_bwd(q, k, v, d_o, lse2, delta, by_target):
    S = q.shape[0]
    T = min(TQ, S)
    nq = S // T

    def body(k_ref, v_ref, q_ref, do_ref, lse_ref, delta_ref, out_ref, dq_ref, dk_ref, dv_ref, in_ref, s_a, dp_a, s_b, dp_b,
             send_sems, recv_sems, local_sem):
        j = pl.program_id(1)
        scatter = _exchange(lambda d: out_ref.at[d], in_ref, send_sems, recv_sems, local_sem)

        @pl.when((pl.program_id(0) == 0) & (j == 0))
        def _():
            _exchange_start(scatter)

        @pl.when(j == 0)
        def _():
            dq_ref[...] = jnp.zeros(dq_ref.shape, f32)

        dk_ref[...] = jnp.zeros(dk_ref.shape, f32)
        dv_ref[...] = jnp.zeros(dv_ref.shape, f32)
        kv, vv = k_ref[...], v_ref[...]

        def rows_of(i):
            return pl.ds(pl.multiple_of(i * T, T), T)

        def scores(i, s_dst, dp_dst):
            rows = rows_of(i)
            s_dst[...] = _nt(kv, q_ref[rows, :])
            dp_dst[...] = _nt(vv, do_ref[rows, :])

        def grads(i, s_src, dp_src, masked):
            rows = rows_of(i)
            s = s_src[...]
            if masked:
                k_i = lax.broadcasted_iota(jnp.int32, (T, T), 0)
                q_i = lax.broadcasted_iota(jnp.int32, (T, T), 1)
                s = jnp.where(k_i <= q_i, s, NEG_INF)
            p = jnp.exp2(s * SCALE_LOG2E - lse_ref[0:1, rows])
            ds = (p * (dp_src[...] - delta_ref[0:1, rows])).astype(MM)
            dv_ref[...] += _nn(p.astype(MM), do_ref[rows, :])
            dk_ref[...] += _nn(ds, q_ref[rows, :])
            dq_ref[rows, :] += _tn(ds, kv)

        last = nq - 1
        scores(j, s_a, dp_a)
        scores(jnp.minimum(j + 1, last), s_b, dp_b)
        grads(j, s_a, dp_a, True)
        rest = last - j

        def pair(t, carry):
            u = j + 1 + 2 * t
            scores(u + 1, s_a, dp_a)
            grads(u, s_b, dp_b, False)
            scores(jnp.minimum(u + 2, last), s_b, dp_b)
            grads(u + 1, s_a, dp_a, False)
            return carry

        lax.fori_loop(0, rest // 2, pair, 0)

        @pl.when(rest % 2 == 1)
        def _():
            grads(last, s_b, dp_b, False)

        @pl.when((pl.program_id(0) == N_HEADS - 1) & (j == nq - 1))
        def _():
            _exchange_wait(scatter)

    whole = lambda w: pl.BlockSpec((S, w), lambda h, j: (0, h))
    stat = pl.BlockSpec((8, S), lambda h, j: (h, 0))
    return pl.pallas_call(
        body, name="flash_bwd", grid=(N_HEADS, nq),
        out_shape=[jax.ShapeDtypeStruct((S, QK_PAD), f32), jax.ShapeDtypeStruct((S, QK_PAD), f32),
                   jax.ShapeDtypeStruct((S, ATTN_WIDTH), f32), jax.ShapeDtypeStruct(by_target.shape, by_target.dtype)],
        in_specs=[pl.BlockSpec((T, HEAD_PAD), lambda h, j: (j, h)), pl.BlockSpec((T, V_DIM), lambda h, j: (j, h)),
                  whole(HEAD_PAD), whole(V_DIM), stat, stat, pl.BlockSpec(memory_space=pl.ANY)],
        out_specs=[whole(HEAD_PAD), pl.BlockSpec((T, HEAD_PAD), lambda h, j: (j, h)),
                   pl.BlockSpec((T, V_DIM), lambda h, j: (j, h)), pl.BlockSpec(memory_space=pl.ANY)],
        scratch_shapes=[pltpu.VMEM((T, T), f32)] * 4 + EXCHANGE_SEMS,
        compiler_params=_cparams(("arbitrary", "arbitrary")),
    )(k, v, q, d_o, lse2, delta, by_target)


def _bwd_proj(x, proj, d_q, d_k, d_v, d_za, d_cbzc, d_u, d_x2, tabs, g_in, w_in_p, g_cq, w_uq_p, g_ckv, w_ukv_p,
              g_q_p, g_k_p, conv_w8):
    S = x.shape[0]
    T = min(TS_E, S)
    nt = S // T
    PIECE = 512

    def body(x_ref, cqkv_ref, ccx_ref, dq_ref, dk_ref, dv_ref, dza_ref, dcbzc_ref, du_ref, dun_ref, dx2_ref,
             c_ref, s1_ref, s2_ref, g_in_ref, w_in_ref, g_cq_ref, w_uq_ref, g_ckv_ref, w_ukv_ref, g_q_ref, g_k_ref, cw_ref,
             gx_ref, dwin_hbm, dwuq_ref, dwukv_ref, dgin_ref, dgcq_ref, dgckv_ref, dgq_ref, dgk_ref,
             dproj_a, dproj_b, dqraw_sc, dkvraw_sc, dwin_sc, sem):
        i = pl.program_id(0)

        @pl.when(i == 0)
        def _():
            dwin_sc[...] = jnp.zeros(dwin_sc.shape, f32)
            dproj_b[...] = jnp.zeros(dproj_b.shape, MM)
            for r in (dwuq_ref, dwukv_ref, dgin_ref, dgcq_ref, dgckv_ref, dgq_ref, dgk_ref):
                r[...] = jnp.zeros(r.shape, f32)

        def stage1(dproj_sc):
            live = (i < nt).astype(f32)
            c, s1, s2 = c_ref[...], s1_ref[...], s2_ref[...]
            g_q, g_k = g_q_ref[...], g_k_ref[...]
            g_cq, g_ckv = g_cq_ref[...], g_ckv_ref[...]

            c_q = cqkv_ref[:, 0:Q_LORA]
            r_cq = _rstd(c_q, Q_LORA)
            cqh = c_q * r_cq
            n_cq = (cqh * g_cq).astype(MM)
            q_raw = _nn(n_cq, w_uq_ref[...])
            dgq = jnp.zeros((1, HEAD_PAD), f32)
            for hd in range(N_HEADS):
                cols = slice(hd * HEAD_PAD, (hd + 1) * HEAD_PAD)
                qh = q_raw[:, cols]
                rq = _rstd(qh, QK_DIM)
                qhh = qh * rq
                dqh = dq_ref[:, cols] * QK_SCALE
                d_qn = jnp.concatenate([dqh[:, :LANES], _rope_t(dqh[:, LANES:], c, s1, s2)], axis=-1)
                dgq += jnp.sum(d_qn * qhh, axis=0, keepdims=True)
                dqraw_sc[:, cols] = _rms_bwd(d_qn, qhh, rq, g_q, QK_DIM).astype(MM)
                yield
            dgq_ref[...] += dgq * live
            d_qraw = dqraw_sc[...]
            dwuq_ref[...] += _tn((cqh * (g_cq * live)).astype(MM), d_qraw)
            d_ncq = _nt(d_qraw, w_uq_ref[...])
            dgcq_ref[...] += jnp.sum(d_ncq * cqh, axis=0, keepdims=True) * live
            dproj_sc[:, 0:Q_LORA] = _rms_bwd(d_ncq, cqh, r_cq, g_cq, Q_LORA).astype(MM)
            yield

            c_kv = cqkv_ref[:, Q_LORA:Q_LORA + KV_LORA]
            r_ckv = _rstd(c_kv, KV_LORA)
            ckvh = c_kv * r_ckv
            n_ckv = (ckvh * g_ckv).astype(MM)
            k_nope = _nn(n_ckv, w_ukv_ref[:, :N_HEADS * NOPE_DIM])
            kpe = cqkv_ref[:, Q_LORA + KV_LORA:Q_LORA + KV_LORA + LANES]
            ss_pe = jnp.sum(kpe * kpe, axis=-1, keepdims=True)
            g_kn, g_kp = g_k[:, :LANES], g_k[:, LANES:]
            d_kpe = jnp.zeros((T, LANES), f32)
            dgk_n = jnp.zeros((1, LANES), f32)
            dgk_p = jnp.zeros((1, LANES), f32)
            for hd in range(N_HEADS):
                kn = k_nope[:, hd * LANES:(hd + 1) * LANES]
                rk = lax.rsqrt((jnp.sum(kn * kn, axis=-1, keepdims=True) + ss_pe) * (1.0 / QK_DIM) + RMS_EPS)
                knh, kph = kn * rk, kpe * rk
                d_kn_n = dk_ref[:, hd * HEAD_PAD:hd * HEAD_PAD + LANES] * QK_SCALE
                d_kr = _rope_t(dk_ref[:, hd * HEAD_PAD + LANES:(hd + 1) * HEAD_PAD] * QK_SCALE, c, s1, s2)
                dgk_n += jnp.sum(d_kn_n * knh, axis=0, keepdims=True)
                dgk_p += jnp.sum(d_kr * kph, axis=0, keepdims=True)
                u_n, u_p = d_kn_n * g_kn, d_kr * g_kp
                mt = (jnp.sum(u_n * knh, axis=-1, keepdims=True) + jnp.sum(u_p * kph, axis=-1, keepdims=True)) * (1.0 / QK_DIM)
                dkvraw_sc[:, hd * LANES:(hd + 1) * LANES] = (rk * (u_n - knh * mt)).astype(MM)
                d_kpe += rk * (u_p - kph * mt)
                yield
            dgk_ref[:, :LANES] += dgk_n * live
            dgk_ref[:, LANES:] += dgk_p * live
            dkvraw_sc[:, N_HEADS * NOPE_DIM:] = dv_ref[...].astype(MM)
            d_kvraw = dkvraw_sc[...]
            dwukv_ref[...] += _tn((ckvh * (g_ckv * live)).astype(MM), d_kvraw)
            d_nckv = _nt(d_kvraw, w_ukv_ref[...])
            dgckv_ref[...] += jnp.sum(d_nckv * ckvh, axis=0, keepdims=True) * live
            dproj_sc[:, Q_LORA:Q_LORA + KV_LORA] = _rms_bwd(d_nckv, ckvh, r_ckv, g_ckv, KV_LORA).astype(MM)
            dproj_sc[:, Q_LORA + KV_LORA:Q_LORA + KV_LORA + LANES] = d_kpe.astype(MM)
            yield

            dproj_sc[:, 512:1024] = dza_ref[...]
            d_u = du_ref[...]
            nxt = dun_ref[...] * (i < nt - 1).astype(f32)
            d_m = cw_ref[2:3, :] * d_u + cw_ref[1:2, :] * _shift_up(d_u, nxt, 1) + cw_ref[0:1, :] * _shift_up(d_u, nxt, 2)
            dproj_sc[:, 1024:1536] = (d_m * ccx_ref[:, CONV_WIDTH:]).astype(MM)
            dproj_sc[:, 1536:2048] = (d_m * ccx_ref[:, :CONV_WIDTH]).astype(MM)
            dproj_sc[:, 2048:3072] = dcbzc_ref[...]

        def stage2(dproj_sc):
            g_in = g_in_ref[...]
            xv = x_ref[...]
            r_in = _rstd(xv, D_MODEL)
            xh = xv * r_in
            hb = (xh * g_in).astype(MM)
            yield
            d_h = jnp.zeros((T, D_MODEL), f32)
            for cb in range(PROJ_PAD // PIECE):
                cols = slice(cb * PIECE, (cb + 1) * PIECE)
                d_piece = dproj_sc[:, cols]
                dwin_sc[:, cols] += _tn(hb, d_piece)
                d_h = d_h + _nt(d_piece, w_in_ref[:, cols])
                yield
            dgin_ref[...] += jnp.sum(d_h * xh, axis=0, keepdims=True)
            gx_ref[...] = dx2_ref[...] + _rms_bwd(d_h, xh, r_in, g_in, D_MODEL)

        def interleaved(*stages):
            stages = list(stages)
            while stages:
                for st in list(stages):
                    if next(st, "done") == "done":
                        stages.remove(st)

        @pl.when(i % 2 == 0)
        def _():
            interleaved(stage2(dproj_b), stage1(dproj_a))

        @pl.when(i % 2 == 1)
        def _():
            interleaved(stage2(dproj_a), stage1(dproj_b))

        @pl.when(i == nt)
        def _():
            cp = pltpu.make_async_copy(dwin_sc, dwin_hbm, sem)
            cp.start()
            cp.wait()

    cur = lambda i: jnp.minimum(i, nt - 1)
    prv = lambda i: jnp.maximum(i - 1, 0)
    row = lambda w, blk=0: pl.BlockSpec((T, w), lambda i: (cur(i), blk))
    row2 = lambda w: pl.BlockSpec((T, w), lambda i: (prv(i), 0))
    acc = lambda shape: pl.BlockSpec(shape, lambda i: (0, 0))
    tb = T // 8
    sds = jax.ShapeDtypeStruct
    return pl.pallas_call(
        body, name="bwd_proj", grid=(nt + 1,),
        out_shape=[sds((S, D_MODEL), f32), sds((D_MODEL, PROJ_PAD), f32), sds((Q_LORA, QK_PAD), f32),
                   sds((KV_LORA, 2 * ATTN_WIDTH), f32), sds((1, D_MODEL), f32), sds((1, Q_LORA), f32), sds((1, KV_LORA), f32),
                   sds((1, HEAD_PAD), f32), sds((1, HEAD_PAD), f32)],
        in_specs=[row2(D_MODEL), row(512, 0), row(2 * CONV_WIDTH, 1), row(QK_PAD), row(QK_PAD), row(ATTN_WIDTH),
                  row(ATTN_WIDTH), row(2 * CONV_WIDTH), row(CONV_WIDTH),
                  pl.BlockSpec((8, CONV_WIDTH), lambda i: (jnp.minimum((cur(i) + 1) * tb, S // 8 - 1), 0)),
                  row2(D_MODEL), row(LANES), row(LANES), row(LANES),
                  _const((1, D_MODEL)), _const((D_MODEL, PROJ_PAD)), _const((1, Q_LORA)), _const((Q_LORA, QK_PAD)),
                  _const((1, KV_LORA)), _const((KV_LORA, 2 * ATTN_WIDTH)), _const((1, HEAD_PAD)), _const((1, HEAD_PAD)),
                  _const((8, CONV_WIDTH))],
        out_specs=[row2(D_MODEL), pl.BlockSpec(memory_space=pl.ANY), acc((Q_LORA, QK_PAD)), acc((KV_LORA, 2 * ATTN_WIDTH)),
                   acc((1, D_MODEL)), acc((1, Q_LORA)), acc((1, KV_LORA)), acc((1, HEAD_PAD)), acc((1, HEAD_PAD))],
        scratch_shapes=[pltpu.VMEM((T, PROJ_PAD), MM), pltpu.VMEM((T, PROJ_PAD), MM), pltpu.VMEM((T, QK_PAD), MM),
                        pltpu.VMEM((T, 2 * ATTN_WIDTH), MM), pltpu.VMEM((D_MODEL, PROJ_PAD), f32), pltpu.SemaphoreType.DMA],
        compiler_params=_cparams(("arbitrary",)),
    )(x, proj, proj, d_q, d_k, d_v, d_za, d_cbzc, d_u, d_u, d_x2, *tabs, g_in, w_in_p, g_cq, w_uq_p, g_ckv, w_ukv_p,
      g_q_p, g_k_p, conv_w8)


def _all_gather(block, name, in_vmem):
    m_per, n = block.shape

    def body(x_ref, out_ref, send_sems, recv_sems, local_sem):
        start, finish = _two_level_gather(x_ref, out_ref, send_sems, recv_sems, local_sem)
        start()
        finish()

    space = pltpu.VMEM if in_vmem else pl.ANY
    return pl.pallas_call(
        body, name=name, out_shape=jax.ShapeDtypeStruct((N_DEV * m_per, n), block.dtype),
        in_specs=[pl.BlockSpec(memory_space=space)], out_specs=pl.BlockSpec(memory_space=space),
        scratch_shapes=list(GATHER_SEMS),
    )(block)


def _sibling_exchange(pack):
    nchip, _, R, n = pack.shape

    def body(p_ref, got_ref, send_sems, recv_sems):
        x, y, c = lax.axis_index("x"), lax.axis_index("y"), lax.axis_index("c")
        copies = [pltpu.make_async_remote_copy(src_ref=p_ref.at[t, 1 - c], dst_ref=got_ref.at[t], send_sem=send_sems.at[t],
                                               recv_sem=recv_sems.at[t], device_id=(x, y, 1 - c), device_id_type=MESH)
                  for t in range(nchip)]
        for cp in copies:
            cp.start()
        for cp in copies:
            cp.wait()

    return pl.pallas_call(
        body, name="rs_sibling", out_shape=jax.ShapeDtypeStruct((nchip, R, n), pack.dtype),
        in_specs=[pl.BlockSpec(memory_space=pl.ANY)], out_specs=pl.BlockSpec(memory_space=pl.ANY),
        scratch_shapes=[pltpu.SemaphoreType.DMA((nchip,)), pltpu.SemaphoreType.DMA((nchip,))],
    )(pack)


def _chip_exchange(part):
    nchip, R, n = part.shape

    def body(p_ref, got_ref, send_sems, recv_sems, local_sem):
        x, y, c = lax.axis_index("x"), lax.axis_index("y"), lax.axis_index("c")
        my_chip = 2 * x + y
        chips = [(1 - x, y), (x, 1 - y), (1 - x, 1 - y)]
        mine = pltpu.make_async_copy(p_ref.at[my_chip], got_ref.at[my_chip], local_sem)
        mine.start()
        sends = []
        for k, (tx, ty) in enumerate(chips):
            cp = pltpu.make_async_remote_copy(src_ref=p_ref.at[2 * tx + ty], dst_ref=got_ref.at[my_chip],
                                              send_sem=send_sems.at[k], recv_sem=recv_sems.at[k],
                                              device_id=(tx, ty, c), device_id_type=MESH)
            cp.start()
            sends.append(cp)
        for k, (sx, sy) in enumerate(chips):
            pltpu.make_async_remote_copy(src_ref=p_ref.at[my_chip], dst_ref=got_ref.at[2 * sx + sy],
                                         send_sem=send_sems.at[k], recv_sem=recv_sems.at[k],
                                         device_id=(sx, sy, c), device_id_type=MESH).wait_recv()
        for cp in sends:
            cp.wait_send()
        mine.wait()

    return pl.pallas_call(
        body, name="rs_chips", out_shape=jax.ShapeDtypeStruct((nchip, R, n), part.dtype),
        in_specs=[pl.BlockSpec(memory_space=pl.ANY)], out_specs=pl.BlockSpec(memory_space=pl.ANY),
        scratch_shapes=[pltpu.SemaphoreType.DMA((3,)), pltpu.SemaphoreType.DMA((3,)), pltpu.SemaphoreType.DMA],
    )(part)


def _add_blocks(pack, from_sib, my_c):
    nb, _, R, n = pack.shape

    def body(c_ref, a_ref, b_ref, o_ref):
        o_ref[...] = (a_ref[0] + b_ref[...]).astype(o_ref.dtype)

    return pl.pallas_call(
        body, name="rs_add", out_shape=jax.ShapeDtypeStruct((nb, R, n), RS_DTYPE),
        grid_spec=pltpu.PrefetchScalarGridSpec(
            num_scalar_prefetch=1, grid=(nb,),
            in_specs=[pl.BlockSpec((1, 1, R, n), lambda i, c: (i, c[0], 0, 0)), pl.BlockSpec((1, R, n), lambda i, c: (i, 0, 0))],
            out_specs=pl.BlockSpec((1, R, n), lambda i, c: (i, 0, 0))),
        compiler_params=_cparams(("parallel",)))(my_c, pack, from_sib)


def _sum_slabs(got, name):
    nb, R, n = got.shape

    def body(g_ref, o_ref):
        tot = g_ref[0].astype(f32)
        for d in range(1, nb):
            tot = tot + g_ref[d].astype(f32)
        o_ref[...] = tot

    return pl.pallas_call(body, name=name, grid=(n // LANES,), out_shape=jax.ShapeDtypeStruct((R, n), f32),
                          in_specs=[pl.BlockSpec((nb, R, LANES), lambda j: (0, 0, j))], out_specs=pl.BlockSpec((R, LANES), lambda j: (0, j)),
                          compiler_params=_cparams(("parallel",)))(got)


def _adamw_all(ws, gs, ms, vs, gain_parts):
    n = len(ws)
    ng = len(GAIN_SLOTS)

    def body(*refs):
        gp_ref = refs[0]
        w_refs = refs[1:1 + n]
        g_refs = refs[1 + n:1 + 2 * n - ng]
        m_refs = refs[1 + 2 * n - ng:1 + 3 * n - ng]
        v_refs = refs[1 + 3 * n - ng:1 + 4 * n - ng]
        outs = refs[1 + 4 * n - ng:]
        gsum_ref, loss_ref = outs[0], outs[1]
        gg_refs = outs[2:2 + ng]
        d_refs, nm_refs, nv_refs = (outs[2 + ng + k * n:2 + ng + (k + 1) * n] for k in range(3))
        tot = gp_ref[0]
        for d in range(1, N_DEV):
            tot = tot + gp_ref[d]
        gsum_ref[...] = tot
        loss_ref[...] = gsum_ref[GAIN_ROWS - 1:GAIN_ROWS, :]
        for k in range(n):
            if k < ng:
                r0, width = GAIN_SLOTS[k]
                for r in range(width // LANES):
                    gg_refs[k][:, r * LANES:(r + 1) * LANES] = gsum_ref[r0 + r:r0 + r + 1, :]
                g = gg_refs[k][...]
            else:
                g = g_refs[k - ng][...]
            w = w_refs[k][...]
            m = ADAM_B1 * m_refs[k][...] + (1.0 - ADAM_B1) * g
            v = ADAM_B2 * v_refs[k][...] + (1.0 - ADAM_B2) * (g * g)
            m_hat = m / (1.0 - ADAM_B1 ** ADAM_STEP)
            v_hat = v / (1.0 - ADAM_B2 ** ADAM_STEP)
            d_refs[k][...] = -ADAM_LR * (m_hat / (jnp.sqrt(v_hat) + ADAM_EPS) + ADAM_WD * w)
            nm_refs[k][...] = m
            nv_refs[k][...] = v

    sds = jax.ShapeDtypeStruct
    like = [sds(w.shape, f32) for w in ws]
    out_shape = [sds((GAIN_ROWS, LANES), f32), sds((1, LANES), f32)] + like[:ng] + like * 3
    full = lambda a: pl.BlockSpec(a.shape, lambda i: (0,) * len(a.shape))
    operands = [gain_parts, *ws, *gs, *ms, *vs]
    outs = pl.pallas_call(
        body, name="adamw", grid=(1,), out_shape=out_shape, in_specs=[full(a) for a in operands],
        out_specs=[full(a) for a in out_shape], compiler_params=_cparams(("arbitrary",)),
    )(*operands)
    loss = outs[1]
    gg = outs[2:2 + ng]
    deltas, new_m, new_v = (outs[2 + ng + k * n:2 + ng + (k + 1) * n] for k in range(3))
    return loss, gg, deltas, new_m, new_v


GAIN_ROWS = 32
GAIN_SLOTS = [(0, 1024), (8, 256), (10, 128), (11, 256), (13, 256), (15, 512), (19, 512), (23, 1024)]

SH_IN = IN_TOTAL // N_DEV
SLAB1_COLS = 384
UQ_ROWS = Q_LORA * (N_HEADS * QK_DIM // N_DEV) // SLAB1_COLS
UKV_ROWS = 48
SLAB1_ROWS = D_MODEL + UQ_ROWS + UKV_ROWS
SH_ROWS = D_MODEL // N_DEV
PL_ROWS = PLE_DIM * (D_MODEL // N_DEV) // D_MODEL
CONV_ROWS = 16
SLAB2_ROWS = 2 * SH_ROWS + PL_ROWS + CONV_ROWS
IN_SEGMENTS = [(0, 448, 0), (448, 960, 512), (1472, 2496, 1024), (960, 1472, 2048), (2496, 3008, 2560)]


def _flat_rows(a, rows, cols):
    lead = a.shape[:-2]
    flat = a.reshape(lead + (-1,))
    pad = [(0, 0)] * len(lead) + [(0, rows * cols - flat.shape[-1])]
    return jnp.pad(flat, pad).reshape(lead + (rows, cols))


def _slab1(in_sh, uq_sh, ukv_sh):
    pad = [(0, 0)] * (in_sh.ndim - 1) + [(0, SLAB1_COLS - in_sh.shape[-1])]
    return jnp.concatenate([jnp.pad(in_sh, pad), _flat_rows(uq_sh, UQ_ROWS, SLAB1_COLS), _flat_rows(ukv_sh, UKV_ROWS, SLAB1_COLS)],
                           axis=-2)


def _slab2(o_sh, plg_sh, pl_sh, conv_sh):
    return jnp.concatenate([o_sh, plg_sh, _flat_rows(pl_sh, PL_ROWS, D_MODEL), _flat_rows(conv_sh, CONV_ROWS, D_MODEL)], axis=-2)


def _in_plan():
    src = [None] * PROJ_PAD
    for a, b, start in IN_SEGMENTS:
        for o in range(a, b):
            src[start + o - a] = (o // SH_IN, o % SH_IN)
    plan = []
    for t in range(PROJ_PAD // LANES):
        runs, j = [], 0
        key_of = lambda jj: None if src[t * LANES + jj] is None else (
            src[t * LANES + jj][0], src[t * LANES + jj][1] // LANES, src[t * LANES + jj][1] % LANES - jj)
        while j < LANES:
            key = key_of(j)
            lo = j
            while j < LANES and key_of(j) == key:
                j += 1
            if key is not None:
                runs.append(key + (lo, j))
        plan.append(runs)
    return plan


def _assemble_w_in(g1):
    plan = _in_plan()
    rows = 256

    def body(g_ref, out_ref):
        src_lane = lax.broadcasted_iota(jnp.int32, (LANES, LANES), 0)
        out_lane = lax.broadcasted_iota(jnp.int32, (LANES, LANES), 1)
        for t, runs in enumerate(plan):
            acc = jnp.zeros((rows, LANES), f32)
            for d, tile, shift, lo, hi in runs:
                pick = ((src_lane == out_lane + shift) & (out_lane >= lo) & (out_lane < hi)).astype(bf16)
                acc = acc + _nn(g_ref[d, :, tile * LANES:(tile + 1) * LANES], pick)
            out_ref[:, t * LANES:(t + 1) * LANES] = acc.astype(out_ref.dtype)

    return pl.pallas_call(
        body, name="w_in_columns", grid=(D_MODEL // rows,), out_shape=jax.ShapeDtypeStruct((D_MODEL, PROJ_PAD), MM),
        in_specs=[pl.BlockSpec((N_DEV, rows, SLAB1_COLS), lambda i: (0, i, 0))],
        out_specs=pl.BlockSpec((rows, PROJ_PAD), lambda i: (i, 0)),
        compiler_params=_cparams(("parallel",)))(g1)


def _by_device(a, width):
    return a.reshape(a.shape[0], N_DEV, width).transpose(1, 0, 2)


def _from_devices(a):
    return a.transpose(1, 0, 2).reshape(a.shape[1], -1)


def kernel(x, p, positions, g_in, w_in, g_cq, w_uq, g_ckv, w_ukv, g_q, g_k, conv_w, g_oa, g_oc, w_o, w_pl, w_plg, g_pl, loss_target, m_g_in, m_w_in, m_g_cq, m_w_uq, m_g_ckv, m_w_ukv, m_g_q, m_g_k, m_conv_w, m_g_oa, m_g_oc, m_w_o, m_w_pl, m_w_plg, m_g_pl, v_g_in, v_w_in, v_g_cq, v_w_uq, v_g_ckv, v_w_ukv, v_g_q, v_g_k, v_conv_w, v_g_oa, v_g_oc, v_w_o, v_w_pl, v_w_plg, v_g_pl):
    S = x.shape[1]
    nd = N_DEV
    xs, ps, tgt = x[0], p[0, 0], loss_target[0]

    slab1 = _slab1(w_in[0], w_uq[0], w_ukv[0]).astype(bf16)
    inv_freq = 1.0 / (ROPE_THETA ** (jnp.arange(0, ROPE_DIM, 2, dtype=f32) / ROPE_DIM))
    inv_row = jnp.concatenate([inv_freq, inv_freq, jnp.zeros((LANES - ROPE_DIM,), f32)]).reshape(1, LANES)
    posf = jnp.broadcast_to(positions[0].astype(f32)[:, None], (S, LANES))
    tabs, g1 = _rope_tables(posf, inv_row, slab1)
    g1 = g1.reshape(nd, SLAB1_ROWS, SLAB1_COLS)
    conv_bits = lax.bitcast_convert_type(conv_w[0], bf16).reshape(CONV_K, -1)
    slab2 = _slab2(w_o[0].astype(bf16), w_plg[0].astype(bf16), w_pl[0].astype(bf16), conv_bits)

    w_in_p = _assemble_w_in(g1)
    wuq = _from_devices(g1[:, D_MODEL:D_MODEL + UQ_ROWS].reshape(nd, Q_LORA, -1)).reshape(Q_LORA, N_HEADS, QK_DIM)
    w_uq_p = jnp.pad(wuq, ((0, 0), (0, 0), (0, HEAD_PAD - QK_DIM))).reshape(Q_LORA, QK_PAD)
    ukv_flat = g1[:, D_MODEL + UQ_ROWS:].reshape(nd, -1)[:, :KV_LORA * V_DIM]
    wukv = _from_devices(ukv_flat.reshape(nd, KV_LORA, V_DIM)).reshape(KV_LORA, N_HEADS, 2, NOPE_DIM)
    w_ukv_p = wukv.transpose(0, 2, 1, 3).reshape(KV_LORA, 2 * ATTN_WIDTH)
    w_in_p, w_uq_p, w_ukv_p = (w.astype(MM) for w in (w_in_p, w_uq_p, w_ukv_p))

    g_q_p = jnp.pad(g_q, ((0, 0), (0, HEAD_PAD - QK_DIM)))
    g_k_p = jnp.pad(g_k, ((0, 0), (0, HEAD_PAD - QK_DIM)))

    proj, q, k, v = _fwd_proj(xs, tabs, g_in, w_in_p, g_cq, w_uq_p, g_ckv, w_ukv_p, g_q_p, g_k_p)
    o, lse, g2 = _flash_fwd(q, k, v, slab2)
    w_o_f = g2[:, 0:SH_ROWS].reshape(D_MODEL, D_MODEL).astype(MM)
    w_plg_f = g2[:, SH_ROWS:2 * SH_ROWS].reshape(D_MODEL, D_MODEL).astype(MM)
    w_pl_f = _from_devices(g2[:, 2 * SH_ROWS:2 * SH_ROWS + PL_ROWS].reshape(nd, PLE_DIM, -1)).astype(MM)
    cv_bits = g2[:, 2 * SH_ROWS + PL_ROWS:].reshape(nd, -1)[:, :CONV_K * (CONV_WIDTH // nd) * 2]
    conv_full = _from_devices(lax.bitcast_convert_type(cv_bits.reshape(nd, CONV_K, CONV_WIDTH // nd, 2), f32))
    conv_w8 = jnp.pad(conv_full, ((0, 8 - CONV_K), (0, 0)))
    (d_x2, d_o, delta, d_za, d_cbzc, d_u, dw_o, dw_plg, dw_pl, dcw, dg_oa, dg_oc, dg_pl, loss_part) = _mid(
        o, proj, xs, ps, tgt, w_o_f, w_plg_f, w_pl_f, conv_w8, g_oa, g_oc, g_pl)

    slabs_a = _slab2(dw_o.reshape(nd, SH_ROWS, D_MODEL), dw_plg.reshape(nd, SH_ROWS, D_MODEL), _by_device(dw_pl, D_MODEL // nd),
                     _by_device(dcw[:CONV_K], CONV_WIDTH // nd)).astype(RS_DTYPE)
    d_q, d_k, d_v, got_a = _flash_bwd(q, k, v, d_o, lse, delta, slabs_a)
    shard_a = _sum_slabs(got_a, "rs_sum8")
    (grad_x, dw_in_p, dw_uq_p, dw_ukv_p, dg_in, dg_cq, dg_ckv, dg_q_p, dg_k_p) = _bwd_proj(
        xs, proj, d_q, d_k, d_v, d_za, d_cbzc, d_u, d_x2, tabs, g_in, w_in_p, g_cq, w_uq_p, g_ckv, w_ukv_p, g_q_p, g_k_p, conv_w8)

    def dw_in_cols(a, b):
        return [dw_in_p[:, start + max(a, s) - s:start + min(b, e) - s] for s, e, start in sorted(IN_SEGMENTS) if max(a, s) < min(b, e)]

    no_cols = jnp.zeros((D_MODEL, SLAB1_COLS - SH_IN), f32)
    in_shards = jnp.stack([jnp.concatenate(dw_in_cols(SH_IN * d, SH_IN * (d + 1)) + [no_cols], axis=1) for d in range(nd)])
    dw_uq = dw_uq_p.reshape(Q_LORA, N_HEADS, HEAD_PAD)[:, :, :QK_DIM].reshape(Q_LORA, N_HEADS * QK_DIM)
    dw_ukv = dw_ukv_p.reshape(KV_LORA, 2, N_HEADS, NOPE_DIM).transpose(0, 2, 1, 3).reshape(KV_LORA, 2 * ATTN_WIDTH)
    slabs_b = _slab1(in_shards, _by_device(dw_uq, N_HEADS * QK_DIM // nd), _by_device(dw_ukv, 2 * ATTN_WIDTH // nd))
    slabs_b = slabs_b.reshape(4, 2, SLAB1_ROWS, SLAB1_COLS)
    my_c = lax.axis_index("c").astype(jnp.int32).reshape(1)
    chip_part = _add_blocks(slabs_b, _sibling_exchange(slabs_b), my_c)
    shard_b = _sum_slabs(_chip_exchange(chip_part), "rs_sum")
    grads_w = [shard_b[:D_MODEL, :SH_IN], shard_b[D_MODEL:D_MODEL + UQ_ROWS].reshape(Q_LORA, -1),
               shard_b[D_MODEL + UQ_ROWS:].reshape(-1)[:KV_LORA * V_DIM].reshape(KV_LORA, V_DIM),
               shard_a[2 * SH_ROWS + PL_ROWS:].reshape(-1)[:CONV_K * CONV_WIDTH // nd].reshape(CONV_K, -1),
               shard_a[0:SH_ROWS], shard_a[2 * SH_ROWS:2 * SH_ROWS + PL_ROWS].reshape(PLE_DIM, -1), shard_a[SH_ROWS:2 * SH_ROWS]]

    gains_part = [dg_in, dg_cq, dg_ckv, dg_q_p, dg_k_p, dg_oa, dg_oc, dg_pl]
    gflat = jnp.concatenate([g.reshape(-1) for g in gains_part] + [jnp.zeros(((GAIN_ROWS - 1) * LANES - 3968,), f32),
                                                                    loss_part.reshape(-1)])
    gain_parts = _all_gather(gflat.reshape(GAIN_ROWS, LANES), "ag_gains", in_vmem=True).reshape(nd, GAIN_ROWS, LANES)

    gains = [g_in, g_cq, g_ckv, g_q_p, g_k_p, g_oa, g_oc, g_pl]
    padq = lambda a: jnp.pad(a, ((0, 0), (0, HEAD_PAD - QK_DIM)))
    m_gains = [m_g_in, m_g_cq, m_g_ckv, padq(m_g_q), padq(m_g_k), m_g_oa, m_g_oc, m_g_pl]
    v_gains = [v_g_in, v_g_cq, v_g_ckv, padq(v_g_q), padq(v_g_k), v_g_oa, v_g_oc, v_g_pl]
    tr = lambda a: jnp.swapaxes(a, 1, 2)
    turned = lambda l: [tr(l[0]), tr(l[1])] + list(l[2:])
    ws = gains + turned([w_in, w_uq, w_ukv, conv_w, w_o, w_pl, w_plg])
    ms = m_gains + turned([m_w_in, m_w_uq, m_w_ukv, m_conv_w, m_w_o, m_w_pl, m_w_plg])
    vs = v_gains + turned([v_w_in, v_w_uq, v_w_ukv, v_conv_w, v_w_o, v_w_pl, v_w_plg])
    loss_row, gg, deltas, new_m, new_v = _adamw_all(ws, turned([g[None] for g in grads_w]), ms, vs, gain_parts)
    grads_w = [g[None] for g in grads_w]
    ng = len(gains)
    deltas, new_m, new_v = (list(l[:ng]) + turned(l[ng:]) for l in (deltas, new_m, new_v))
    loss = loss_row[0, 0]

    def ordered(gl, wl):
        g_in_, g_cq_, g_ckv_, g_q_, g_k_, g_oa_, g_oc_, g_pl_ = gl
        g_q_, g_k_ = g_q_[:, :QK_DIM], g_k_[:, :QK_DIM]
        w_in_, w_uq_, w_ukv_, cw_, w_o_, w_pl_, w_plg_ = wl
        return [g_in_, w_in_, g_cq_, w_uq_, g_ckv_, w_ukv_, g_q_, g_k_, cw_, g_oa_, g_oc_, w_o_, w_pl_, w_plg_, g_pl_]

    ng = len(gains)
    outs = [loss, grad_x[None]]
    outs += ordered(gg, grads_w)
    for lst in (deltas, new_m, new_v):
        outs += ordered(lst[:ng], lst[ng:])
    return tuple(outs)
```

```python
import functools
import math

import jax
import jax.numpy as jnp
from jax import lax
from jax.experimental import pallas as pl
from jax.experimental.pallas import tpu as pltpu

f32 = jnp.float32
bf16 = jnp.bfloat16
MM = jnp.bfloat16
RS_DTYPE = jnp.bfloat16

D_MODEL = 1024
PLE_DIM = 256
N_HEADS = 4
NOPE_DIM = 128
ROPE_DIM = 64
V_DIM = 128
QK_DIM = NOPE_DIM + ROPE_DIM
Q_LORA = 256
KV_LORA = 128
ATTN_WIDTH = N_HEADS * V_DIM
CONV_WIDTH = D_MODEL - ATTN_WIDTH
CONV_K = 3
ROPE_THETA = 10000.0
RMS_EPS = 1e-6
NEG_INF = -1e30
IN_TOTAL = Q_LORA + KV_LORA + ROPE_DIM + ATTN_WIDTH + 4 * CONV_WIDTH
ADAM_LR = 0.001
ADAM_B1 = 0.9
ADAM_B2 = 0.999
ADAM_EPS = 1e-08
ADAM_WD = 0.01
ADAM_STEP = 10

N_DEV = 8
LANES = 128
HEAD_PAD = 256
QK_PAD = N_HEADS * HEAD_PAD
PROJ_PAD = 3072
TS = 512
TS_M = 256
TS_E = 256
TQ = 512
VMEM_LIMIT = 58 * 1024 * 1024
MESH = pl.DeviceIdType.MESH
QK_SCALE = 1.0 / math.sqrt(QK_DIM)
SCALE_LOG2E = QK_SCALE * math.log2(math.e)


def _cparams(sem=None):
    return pltpu.CompilerParams(dimension_semantics=sem, vmem_limit_bytes=VMEM_LIMIT)


def _nt(a, b):
    return lax.dot_general(a, b, (((1,), (1,)), ((), ())), preferred_element_type=f32)


def _tn(a, b):
    return lax.dot_general(a, b, (((0,), (0,)), ((), ())), preferred_element_type=f32)


def _nn(a, b):
    return jnp.dot(a, b, preferred_element_type=f32)


def _rstd(x, n):
    return lax.rsqrt(jnp.sum(x * x, axis=-1, keepdims=True) * (1.0 / n) + RMS_EPS)


def _rms_bwd(dy, xhat, r, g, n):
    u = dy * g
    return r * (u - xhat * (jnp.sum(u * xhat, axis=-1, keepdims=True) * (1.0 / n)))


def _rope(t, c, s1, s2):
    return t * c + pltpu.roll(t, 32, 1) * s1 + pltpu.roll(t, 96, 1) * s2


def _rope_t(d, c, s1, s2):
    return d * c - pltpu.roll(d, 96, 1) * s2 - pltpu.roll(d, 32, 1) * s1


def _alternate(*stages):
    stages = list(stages)
    while stages:
        for st in list(stages):
            if next(st, "done") == "done":
                stages.remove(st)


def _const(shape):
    return pl.BlockSpec(shape, lambda *_: (0,) * len(shape), pipeline_mode=pl.Buffered(1))


def _two_level_gather(x_ref, out_ref, send_sems, recv_sems, local_sem):
    m_per = x_ref.shape[0]
    x, y, c = lax.axis_index("x"), lax.axis_index("y"), lax.axis_index("c")
    me, sibling = (x, y, c), (x, y, 1 - c)
    chips = [(1 - x, y), (x, 1 - y), (1 - x, 1 - y)]

    def rows(px, py, pc):
        return out_ref.at[pl.ds((4 * px + 2 * py + pc) * m_per, m_per), :]

    def copy(k, blk, to, src=None):
        return pltpu.make_async_remote_copy(
            src_ref=rows(*blk) if src is None else src, dst_ref=rows(*blk),
            send_sem=send_sems.at[k], recv_sem=recv_sems.at[k], device_id=to, device_id_type=MESH)

    mine = pltpu.make_async_copy(x_ref, rows(*me), local_sem)
    first = [copy(0, me, sibling, src=x_ref)] + [copy(1 + j, me, (*chip, c), src=x_ref) for j, chip in enumerate(chips)]
    passed = [copy(4 + j, (*chip, c), sibling) for j, chip in enumerate(chips)]

    def start():
        mine.start()
        for cp in first:
            cp.start()

    def finish():
        for j, chip in enumerate(chips):
            copy(1 + j, (*chip, c), me).wait_recv()
            passed[j].start()
        copy(0, sibling, me).wait_recv()
        for j, chip in enumerate(chips):
            copy(4 + j, (*chip, 1 - c), me).wait_recv()
        for cp in first + passed:
            cp.wait_send()
        mine.wait()

    return start, finish


GATHER_SEMS = [pltpu.SemaphoreType.DMA((7,)), pltpu.SemaphoreType.DMA((7,)), pltpu.SemaphoreType.DMA]


def _rope_tables(posf, inv_freq, block):
    S = posf.shape[0]
    t = min(S, TS)
    nt = S // t

    def body(pos_ref, f_ref, blk_ref, c_ref, s1_ref, s2_ref, all_ref, send_sems, recv_sems, local_sem):
        i = pl.program_id(0)
        start, finish = _two_level_gather(blk_ref, all_ref, send_sems, recv_sems, local_sem)

        @pl.when(i == 0)
        def _():
            start()

        ang = pos_ref[...] * f_ref[...]
        lane = lax.broadcasted_iota(jnp.int32, ang.shape, 1)
        cs, sn = jnp.cos(ang), jnp.sin(ang)
        c_ref[...] = jnp.where(lane < ROPE_DIM, cs, 0.0)
        s1_ref[...] = jnp.where((lane >= ROPE_DIM // 2) & (lane < ROPE_DIM), sn, 0.0)
        s2_ref[...] = jnp.where(lane < ROPE_DIM // 2, -sn, 0.0)

        @pl.when(i == nt - 1)
        def _():
            finish()

    spec = pl.BlockSpec((t, LANES), lambda i: (i, 0))
    hbm = pl.BlockSpec(memory_space=pl.ANY)
    outs = pl.pallas_call(
        body, name="rope_tables", grid=(nt,),
        out_shape=[jax.ShapeDtypeStruct((S, LANES), f32)] * 3 + [jax.ShapeDtypeStruct((N_DEV * block.shape[0], block.shape[1]), block.dtype)],
        in_specs=[spec, _const((1, LANES)), hbm], out_specs=[spec] * 3 + [hbm],
        scratch_shapes=list(GATHER_SEMS),
        compiler_params=_cparams(("arbitrary",)),
    )(posf, inv_freq, block)
    return outs[:3], outs[3]


def _fwd_proj(x, tabs, g_in, w_in_p, g_cq, w_uq_p, g_ckv, w_ukv_p, g_q_p, g_k_p):
    S = x.shape[0]
    T = min(TS, S)
    nt = S // T
    LAT = Q_LORA + KV_LORA + LANES
    PIECE = 512

    def body(x_ref, c_ref, s1_ref, s2_ref, g_in_ref, w_in_ref, g_cq_ref, w_uq_ref, g_ckv_ref, w_ukv_ref,
             g_q_ref, g_k_ref, proj_ref, q_ref, k_ref, v_ref, lat_a, lat_b):
        i = pl.program_id(0)

        @pl.when(i == 0)
        def _():
            lat_b[...] = jnp.zeros(lat_b.shape, f32)

        def stage1(lat):
            xv = x_ref[...]
            h = (xv * _rstd(xv, D_MODEL) * g_in_ref[...]).astype(MM)
            yield
            for cb in range(PROJ_PAD // PIECE):
                cols = slice(cb * PIECE, (cb + 1) * PIECE)
                proj_ref[:, cols] = _nn(h, w_in_ref[:, cols])
                yield
            lat[...] = proj_ref[:, 0:LAT]

        def stage2(lat):
            c, s1, s2 = c_ref[...], s1_ref[...], s2_ref[...]
            g_q, g_k = g_q_ref[...], g_k_ref[...]
            c_q = lat[:, 0:Q_LORA]
            n_cq = (c_q * _rstd(c_q, Q_LORA) * g_cq_ref[...]).astype(MM)
            q_raw = _nn(n_cq, w_uq_ref[...])
            for hd in range(N_HEADS):
                qh = q_raw[:, hd * HEAD_PAD:(hd + 1) * HEAD_PAD]
                qn = qh * _rstd(qh, QK_DIM) * g_q
                q_ref[:, hd * HEAD_PAD:hd * HEAD_PAD + LANES] = qn[:, :LANES].astype(MM)
                q_ref[:, hd * HEAD_PAD + LANES:(hd + 1) * HEAD_PAD] = _rope(qn[:, LANES:], c, s1, s2).astype(MM)
                yield

            c_kv = lat[:, Q_LORA:Q_LORA + KV_LORA]
            n_ckv = (c_kv * _rstd(c_kv, KV_LORA) * g_ckv_ref[...]).astype(MM)
            kv_raw = _nn(n_ckv, w_ukv_ref[...])
            kpe = lat[:, Q_LORA + KV_LORA:LAT]
            ss_pe = jnp.sum(kpe * kpe, axis=-1, keepdims=True)
            kr = _rope(kpe * g_k[:, LANES:], c, s1, s2)
            for hd in range(N_HEADS):
                kn = kv_raw[:, hd * LANES:(hd + 1) * LANES]
                rk = lax.rsqrt((jnp.sum(kn * kn, axis=-1, keepdims=True) + ss_pe) * (1.0 / QK_DIM) + RMS_EPS)
                k_ref[:, hd * HEAD_PAD:hd * HEAD_PAD + LANES] = (kn * rk * g_k[:, :LANES]).astype(MM)
                k_ref[:, hd * HEAD_PAD + LANES:(hd + 1) * HEAD_PAD] = (kr * rk).astype(MM)
                yield
            v_ref[...] = kv_raw[:, N_HEADS * NOPE_DIM:].astype(MM)

        @pl.when(i % 2 == 0)
        def _():
            _alternate(stage1(lat_a), stage2(lat_b))

        @pl.when(i % 2 == 1)
        def _():
            _alternate(stage1(lat_b), stage2(lat_a))

    cur = lambda w: pl.BlockSpec((T, w), lambda i: (jnp.minimum(i, nt - 1), 0))
    prv = lambda w: pl.BlockSpec((T, w), lambda i: (jnp.maximum(i - 1, 0), 0))
    return pl.pallas_call(
        body, name="fwd_proj", grid=(nt + 1,),
        out_shape=[jax.ShapeDtypeStruct((S, PROJ_PAD), f32), jax.ShapeDtypeStruct((S, QK_PAD), MM),
                   jax.ShapeDtypeStruct((S, QK_PAD), MM), jax.ShapeDtypeStruct((S, ATTN_WIDTH), MM)],
        in_specs=[cur(D_MODEL), prv(LANES), prv(LANES), prv(LANES), _const((1, D_MODEL)), _const((D_MODEL, PROJ_PAD)),
                  _const((1, Q_LORA)), _const((Q_LORA, QK_PAD)), _const((1, KV_LORA)), _const((KV_LORA, 2 * ATTN_WIDTH)),
                  _const((1, HEAD_PAD)), _const((1, HEAD_PAD))],
        out_specs=[cur(PROJ_PAD), prv(QK_PAD), prv(QK_PAD), prv(ATTN_WIDTH)],
        scratch_shapes=[pltpu.VMEM((T, LAT), f32), pltpu.VMEM((T, LAT), f32)],
        compiler_params=_cparams(("arbitrary",)),
    )(x, *tabs, g_in, w_in_p, g_cq, w_uq_p, g_ckv, w_ukv_p, g_q_p, g_k_p)


def _exchange(src_of, dst_ref, send_sems, recv_sems, local_sem):
    x, y, c = lax.axis_index("x"), lax.axis_index("y"), lax.axis_index("c")
    me = 4 * x + 2 * y + c
    remote = []
    for k in range(1, N_DEV):
        px = 1 - x if (k >> 2) & 1 else x
        py = 1 - y if (k >> 1) & 1 else y
        pc = 1 - c if k & 1 else c
        remote.append(pltpu.make_async_remote_copy(
            src_ref=src_of(4 * px + 2 * py + pc), dst_ref=dst_ref.at[me], send_sem=send_sems.at[k - 1],
            recv_sem=recv_sems.at[k - 1], device_id=(px, py, pc), device_id_type=MESH))
    return remote, pltpu.make_async_copy(src_of(me), dst_ref.at[me], local_sem)


def _exchange_start(copies):
    remote, local = copies
    local.start()
    for cp in remote:
        cp.start()


def _exchange_wait(copies):
    remote, local = copies
    for cp in remote:
        cp.wait_recv()
    for cp in remote:
        cp.wait_send()
    local.wait()


EXCHANGE_SEMS = [pltpu.SemaphoreType.DMA((N_DEV - 1,)), pltpu.SemaphoreType.DMA((N_DEV - 1,)), pltpu.SemaphoreType.DMA]


def _flash_fwd(q, k, v, block):
    S = q.shape[0]
    T = min(TQ, S)
    nq = S // T

    def body(q_ref, k_ref, v_ref, blk_ref, o_ref, lse_ref, all_ref, s_a, s_b, m_sc, l_sc, acc_sc, send_sems, recv_sems, local_sem):
        n = pl.program_id(1)
        gather = _exchange(lambda d: blk_ref, all_ref, send_sems, recv_sems, local_sem)

        @pl.when((pl.program_id(0) == 0) & (n == 0))
        def _():
            _exchange_start(gather)

        m_sc[...] = jnp.full(m_sc.shape, NEG_INF, f32)
        l_sc[...] = jnp.zeros(l_sc.shape, f32)
        acc_sc[...] = jnp.zeros(acc_sc.shape, f32)
        qv = q_ref[...]

        def rows(j):
            return pl.ds(pl.multiple_of(j * T, T), T)

        def scores(j, dst):
            dst[...] = _nt(k_ref[rows(j), :], qv)

        def update(src, j, masked):
            s = src[...]
            if masked:
                k_i = lax.broadcasted_iota(jnp.int32, (T, T), 0)
                q_i = lax.broadcasted_iota(jnp.int32, (T, T), 1)
                s = jnp.where(k_i <= q_i, s, NEG_INF)
            m_prev = m_sc[...]
            m_new = jnp.maximum(m_prev, jnp.max(s, axis=0, keepdims=True))
            alpha = jnp.exp2((m_prev - m_new) * SCALE_LOG2E)
            p = jnp.exp2((s - m_new[0:1, :]) * SCALE_LOG2E)
            l_sc[...] = alpha * l_sc[...] + jnp.sum(p, axis=0, keepdims=True)
            acc_sc[...] = alpha[0:1, :] * acc_sc[...] + _tn(v_ref[rows(j), :], p.astype(MM))
            m_sc[...] = m_new

        scores(0, s_a)

        def pair(t, carry):
            j = 2 * t
            scores(j + 1, s_b)
            update(s_a, j, False)
            scores(j + 2, s_a)
            update(s_b, j + 1, False)
            return carry

        lax.fori_loop(0, n // 2, pair, 0)

        @pl.when(n % 2 == 0)
        def _():
            update(s_a, n, True)

        @pl.when(n % 2 == 1)
        def _():
            scores(n, s_b)
            update(s_a, n - 1, False)
            update(s_b, n, True)

        o_ref[...] = (acc_sc[...] / l_sc[0:1, :]).T
        lse_ref[...] = m_sc[...] * SCALE_LOG2E + jnp.log2(l_sc[...])

        @pl.when((pl.program_id(0) == N_HEADS - 1) & (n == nq - 1))
        def _():
            _exchange_wait(gather)

    return pl.pallas_call(
        body, name="flash_fwd", grid=(N_HEADS, nq),
        out_shape=[jax.ShapeDtypeStruct((S, ATTN_WIDTH), f32), jax.ShapeDtypeStruct((N_HEADS * 8, S), f32),
                   jax.ShapeDtypeStruct((N_DEV,) + block.shape, block.dtype)],
        in_specs=[pl.BlockSpec((T, HEAD_PAD), lambda h, i: (i, h)),
                  pl.BlockSpec((S, HEAD_PAD), lambda h, i: (0, h)),
                  pl.BlockSpec((S, V_DIM), lambda h, i: (0, h)),
                  pl.BlockSpec(memory_space=pl.ANY)],
        out_specs=[pl.BlockSpec((T, V_DIM), lambda h, i: (i, h)), pl.BlockSpec((8, T), lambda h, i: (h, i)),
                   pl.BlockSpec(memory_space=pl.ANY)],
        scratch_shapes=[pltpu.VMEM((T, T), f32), pltpu.VMEM((T, T), f32), pltpu.VMEM((8, T), f32), pltpu.VMEM((8, T), f32),
                        pltpu.VMEM((V_DIM, T), f32)] + EXCHANGE_SEMS,
        compiler_params=_cparams(("arbitrary", "arbitrary")),
    )(q, k, v, block)


def _shift_down(m, prev8, n):
    T = m.shape[0]
    rows = lax.broadcasted_iota(jnp.int32, m.shape, 0)
    head = jnp.tile(pltpu.roll(prev8, n, 0), (T // 8, 1))
    return jnp.where(rows >= n, pltpu.roll(m, n, 0), head)


def _shift_up(m, next8, n):
    T = m.shape[0]
    rows = lax.broadcasted_iota(jnp.int32, m.shape, 0)
    tail = jnp.tile(pltpu.roll(next8, 8 - n, 0), (T // 8, 1))
    return jnp.where(rows < T - n, pltpu.roll(m, T - n, 0), tail)


def _mid(o, proj, x, p, target, w_o, w_plg, w_pl, conv_w8, g_oa, g_oc, g_pl):
    S = x.shape[0]
    T = min(TS_M, S)
    nt = S // T

    def body(o_ref, za_ref, ccx_ref, cbzc_ref, prev_ref, x_ref, p_ref, t_ref, wo_ref, wplg_ref, wpl_ref, cw_ref,
             goa_ref, goc_ref, gpl_ref,
             dx2_ref, do_ref, delta_ref, dza_ref, dcbzc_ref, du_ref,
             dwo_ref, dwplg_ref, dwpl_ref, dcw_ref, dgoa_ref, dgoc_ref, dgpl_ref, loss_ref):
        i = pl.program_id(0)

        @pl.when(i == 0)
        def _():
            for r in (dwo_ref, dwplg_ref, dwpl_ref, dcw_ref, dgoa_ref, dgoc_ref, dgpl_ref, loss_ref):
                r[...] = jnp.zeros(r.shape, f32)

        o_v = o_ref[...]
        z_a = za_ref[...]
        cc, cx = ccx_ref[:, :CONV_WIDTH], ccx_ref[:, CONV_WIDTH:]
        cb, z_c = cbzc_ref[:, :CONV_WIDTH], cbzc_ref[:, CONV_WIDTH:]
        g_oa, g_oc, g_pl = goa_ref[...], goc_ref[...], gpl_ref[...]
        w0, w1, w2 = cw_ref[0:1, :], cw_ref[1:2, :], cw_ref[2:3, :]
        p_b = p_ref[...].astype(MM)
        pw = _nn(p_b, wpl_ref[...])

        sa = jax.nn.sigmoid(z_a)
        silu_a = z_a * sa
        ga = o_v * silu_a
        ra = _rstd(ga, ATTN_WIDTH)
        gha = ga * ra
        ya = (gha * g_oa).astype(MM)
        x2 = x_ref[...] + _nn(ya, wo_ref[:ATTN_WIDTH, :])
        m0 = cc * cx
        prev = prev_ref[:, :CONV_WIDTH] * prev_ref[:, CONV_WIDTH:] * (i > 0).astype(f32)
        m1 = _shift_down(m0, prev, 1)
        m2 = _shift_down(m0, prev, 2)
        u = w0 * m2 + w1 * m1 + w2 * m0
        sc = jax.nn.sigmoid(z_c)
        silu_c = z_c * sc
        gc = cb * u * silu_c
        rc = _rstd(gc, CONV_WIDTH)
        ghc = gc * rc
        yc = (ghc * g_oc).astype(MM)
        ycat = jnp.concatenate([ya, yc], axis=-1)
        x2 = x2 + _nn(yc, wo_ref[ATTN_WIDTH:, :])
        r2 = _rstd(x2, D_MODEL)
        xh2 = x2 * r2
        n2 = (xh2 * g_pl).astype(MM)
        gate = jax.nn.sigmoid(_nn(n2, wplg_ref[...]))
        err = x2 + gate * pw - t_ref[...]
        loss_ref[...] += jnp.sum(jnp.sum(err * err, axis=-1, keepdims=True), axis=0, keepdims=True) * (0.5 / D_MODEL)
        d_out = err * (1.0 / D_MODEL)
        d_glog = (d_out * pw * gate * (1.0 - gate)).astype(MM)
        d_n2 = _nt(d_glog, wplg_ref[...])
        dwpl_ref[...] += _tn(p_b, (d_out * gate).astype(MM))
        dwplg_ref[...] += _tn(n2, d_glog)
        dgpl_ref[...] += jnp.sum(d_n2 * xh2, axis=0, keepdims=True)
        d_x2 = d_out + _rms_bwd(d_n2, xh2, r2, g_pl, D_MODEL)
        dx2_ref[...] = d_x2
        d_x2b = d_x2.astype(MM)
        d_ycat = _nt(d_x2b, wo_ref[...])
        dwo_ref[...] += _tn(ycat, d_x2b)
        d_ya, d_yc = d_ycat[:, :ATTN_WIDTH], d_ycat[:, ATTN_WIDTH:]
        dgoa_ref[...] += jnp.sum(d_ya * gha, axis=0, keepdims=True)
        dgoc_ref[...] += jnp.sum(d_yc * ghc, axis=0, keepdims=True)
        d_ga = _rms_bwd(d_ya, gha, ra, g_oa, ATTN_WIDTH)
        d_gc = _rms_bwd(d_yc, ghc, rc, g_oc, CONV_WIDTH)
        d_o = d_ga * silu_a
        do_ref[...] = d_o.astype(MM)
        dza_ref[...] = (d_ga * o_v * (sa * (1.0 + z_a * (1.0 - sa)))).astype(MM)
        for hd in range(N_HEADS):
            cols = slice(hd * V_DIM, (hd + 1) * V_DIM)
            dl = jnp.sum(d_o[:, cols] * o_v[:, cols], axis=-1, keepdims=True)
            delta_ref[hd * 8:(hd + 1) * 8, :] = jnp.broadcast_to(dl, (T, LANES)).T[0:8, :]
        d_u = d_gc * cb * silu_c
        du_ref[...] = d_u
        dcbzc_ref[:, :CONV_WIDTH] = (d_gc * u * silu_c).astype(MM)
        dcbzc_ref[:, CONV_WIDTH:] = (d_gc * cb * u * (sc * (1.0 + z_c * (1.0 - sc)))).astype(MM)
        dcw_ref[0:1, :] += jnp.sum(d_u * m2, axis=0, keepdims=True)
        dcw_ref[1:2, :] += jnp.sum(d_u * m1, axis=0, keepdims=True)
        dcw_ref[2:3, :] += jnp.sum(d_u * m0, axis=0, keepdims=True)

    row = lambda w, blk=0: pl.BlockSpec((T, w), lambda i: (i, blk))
    acc = lambda shape: pl.BlockSpec(shape, lambda i: (0, 0))
    tb = T // 8
    sds = jax.ShapeDtypeStruct
    return pl.pallas_call(
        body, name="mid", grid=(nt,),
        out_shape=[sds((S, D_MODEL), f32), sds((S, ATTN_WIDTH), MM), sds((N_HEADS * 8, S), f32), sds((S, ATTN_WIDTH), MM),
                   sds((S, 2 * CONV_WIDTH), MM), sds((S, CONV_WIDTH), f32),
                   sds((D_MODEL, D_MODEL), f32), sds((D_MODEL, D_MODEL), f32), sds((PLE_DIM, D_MODEL), f32), sds((8, CONV_WIDTH), f32),
                   sds((1, ATTN_WIDTH), f32), sds((1, CONV_WIDTH), f32), sds((1, D_MODEL), f32), sds((1, LANES), f32)],
        in_specs=[row(ATTN_WIDTH), row(ATTN_WIDTH, 1), row(2 * CONV_WIDTH, 1), row(2 * CONV_WIDTH, 2),
                  pl.BlockSpec((8, 2 * CONV_WIDTH), lambda i: (jnp.maximum(i * tb - 1, 0), 1)),
                  row(D_MODEL), row(PLE_DIM), row(D_MODEL),
                  _const((D_MODEL, D_MODEL)), _const((D_MODEL, D_MODEL)), _const((PLE_DIM, D_MODEL)), _const((8, CONV_WIDTH)),
                  _const((1, ATTN_WIDTH)), _const((1, CONV_WIDTH)), _const((1, D_MODEL))],
        out_specs=[row(D_MODEL), row(ATTN_WIDTH), pl.BlockSpec((N_HEADS * 8, T), lambda i: (0, i)), row(ATTN_WIDTH),
                   row(2 * CONV_WIDTH), row(CONV_WIDTH),
                   acc((D_MODEL, D_MODEL)), acc((D_MODEL, D_MODEL)), acc((PLE_DIM, D_MODEL)), acc((8, CONV_WIDTH)),
                   acc((1, ATTN_WIDTH)), acc((1, CONV_WIDTH)), acc((1, D_MODEL)), acc((1, LANES))],
        compiler_params=_cparams(("arbitrary",)),
    )(o, proj, proj, proj, proj, x, p, target, w_o, w_plg, w_pl, conv_w8, g_oa, g_oc, g_pl)


def _flash_bwd(q, k, v, d_o, lse2, delta, by_target):
    S = q.shape[0]
    T = min(TQ, S)
    nq = S // T

    def body(k_ref, v_ref, q_ref, do_ref, lse_ref, delta_ref, out_ref, dq_ref, dk_ref, dv_ref, in_ref, s_a, dp_a, s_b, dp_b,
             send_sems, recv_sems, local_sem):
        j = pl.program_id(1)
        scatter = _exchange(lambda d: out_ref.at[d], in_ref, send_sems, recv_sems, local_sem)

        @pl.when((pl.program_id(0) == 0) & (j == 0))
        def _():
            _exchange_start(scatter)

        @pl.when(j == 0)
        def _():
            dq_ref[...] = jnp.zeros(dq_ref.shape, f32)

        dk_ref[...] = jnp.zeros(dk_ref.shape, f32)
        dv_ref[...] = jnp.zeros(dv_ref.shape, f32)
        kv, vv = k_ref[...], v_ref[...]

        def rows_of(i):
            return pl.ds(pl.multiple_of(i * T, T), T)

        def scores(i, s_dst, dp_dst):
            rows = rows_of(i)
            s_dst[...] = _nt(kv, q_ref[rows, :])
            dp_dst[...] = _nt(vv, do_ref[rows, :])

        def grads(i, s_src, dp_src, masked):
            rows = rows_of(i)
            s = s_src[...]
            if masked:
                k_i = lax.broadcasted_iota(jnp.int32, (T, T), 0)
                q_i = lax.broadcasted_iota(jnp.int32, (T, T), 1)
                s = jnp.where(k_i <= q_i, s, NEG_INF)
            p = jnp.exp2(s * SCALE_LOG2E - lse_ref[0:1, rows])
            ds = (p * (dp_src[...] - delta_ref[0:1, rows])).astype(MM)
            dv_ref[...] += _nn(p.astype(MM), do_ref[rows, :])
            dk_ref[...] += _nn(ds, q_ref[rows, :])
            dq_ref[rows, :] += _tn(ds, kv)

        last = nq - 1
        scores(j, s_a, dp_a)
        scores(jnp.minimum(j + 1, last), s_b, dp_b)
        grads(j, s_a, dp_a, True)
        rest = last - j

        def pair(t, carry):
            u = j + 1 + 2 * t
            scores(u + 1, s_a, dp_a)
            grads(u, s_b, dp_b, False)
            scores(jnp.minimum(u + 2, last), s_b, dp_b)
            grads(u + 1, s_a, dp_a, False)
            return carry

        lax.fori_loop(0, rest // 2, pair, 0)

        @pl.when(rest % 2 == 1)
        def _():
            grads(last, s_b, dp_b, False)

        @pl.when((pl.program_id(0) == N_HEADS - 1) & (j == nq - 1))
        def _():
            _exchange_wait(scatter)

    whole = lambda w: pl.BlockSpec((S, w), lambda h, j: (0, h))
    stat = pl.BlockSpec((8, S), lambda h, j: (h, 0))
    return pl.pallas_call(
        body, name="flash_bwd", grid=(N_HEADS, nq),
        out_shape=[jax.ShapeDtypeStruct((S, QK_PAD), f32), jax.ShapeDtypeStruct((S, QK_PAD), f32),
                   jax.ShapeDtypeStruct((S, ATTN_WIDTH), f32), jax.ShapeDtypeStruct(by_target.shape, by_target.dtype)],
        in_specs=[pl.BlockSpec((T, HEAD_PAD), lambda h, j: (j, h)), pl.BlockSpec((T, V_DIM), lambda h, j: (j, h)),
                  whole(HEAD_PAD), whole(V_DIM), stat, stat, pl.BlockSpec(memory_space=pl.ANY)],
        out_specs=[whole(HEAD_PAD), pl.BlockSpec((T, HEAD_PAD), lambda h, j: (j, h)),
                   pl.BlockSpec((T, V_DIM), lambda h, j: (j, h)), pl.BlockSpec(memory_space=pl.ANY)],
        scratch_shapes=[pltpu.VMEM((T, T), f32)] * 4 + EXCHANGE_SEMS,
        compiler_params=_cparams(("arbitrary", "arbitrary")),
    )(k, v, q, d_o, lse2, delta, by_target)


def _bwd_proj(x, proj, d_q, d_k, d_v, d_za, d_cbzc, d_u, d_x2, tabs, g_in, w_in_p, g_cq, w_uq_p, g_ckv, w_ukv_p,
              g_q_p, g_k_p, conv_w8):
    S = x.shape[0]
    T = min(TS_E, S)
    nt = S // T
    PIECE = 512

    def body(x_ref, cqkv_ref, ccx_ref, dq_ref, dk_ref, dv_ref, dza_ref, dcbzc_ref, du_ref, dun_ref, dx2_ref,
             c_ref, s1_ref, s2_ref, g_in_ref, w_in_ref, g_cq_ref, w_uq_ref, g_ckv_ref, w_ukv_ref, g_q_ref, g_k_ref, cw_ref,
             gx_ref, dwin_hbm, dwuq_ref, dwukv_ref, dgin_ref, dgcq_ref, dgckv_ref, dgq_ref, dgk_ref,
             dproj_a, dproj_b, dqraw_sc, dkvraw_sc, dwin_sc, sem):
        i = pl.program_id(0)

        @pl.when(i == 0)
        def _():
            dwin_sc[...] = jnp.zeros(dwin_sc.shape, f32)
            dproj_b[...] = jnp.zeros(dproj_b.shape, MM)
            for r in (dwuq_ref, dwukv_ref, dgin_ref, dgcq_ref, dgckv_ref, dgq_ref, dgk_ref):
                r[...] = jnp.zeros(r.shape, f32)

        def stage1(dproj_sc):
            live = (i < nt).astype(f32)
            c, s1, s2 = c_ref[...], s1_ref[...], s2_ref[...]
            g_q, g_k = g_q_ref[...], g_k_ref[...]
            g_cq, g_ckv = g_cq_ref[...], g_ckv_ref[...]

            c_q = cqkv_ref[:, 0:Q_LORA]
            r_cq = _rstd(c_q, Q_LORA)
            cqh = c_q * r_cq
            n_cq = (cqh * g_cq).astype(MM)
            q_raw = _nn(n_cq, w_uq_ref[...])
            dgq = jnp.zeros((1, HEAD_PAD), f32)
            for hd in range(N_HEADS):
                cols = slice(hd * HEAD_PAD, (hd + 1) * HEAD_PAD)
                qh = q_raw[:, cols]
                rq = _rstd(qh, QK_DIM)
                qhh = qh * rq
                dqh = dq_ref[:, cols] * QK_SCALE
                d_qn = jnp.concatenate([dqh[:, :LANES], _rope_t(dqh[:, LANES:], c, s1, s2)], axis=-1)
                dgq += jnp.sum(d_qn * qhh, axis=0, keepdims=True)
                dqraw_sc[:, cols] = _rms_bwd(d_qn, qhh, rq, g_q, QK_DIM).astype(MM)
                yield
            dgq_ref[...] += dgq * live
            d_qraw = dqraw_sc[...]
            dwuq_ref[...] += _tn((cqh * (g_cq * live)).astype(MM), d_qraw)
            d_ncq = _nt(d_qraw, w_uq_ref[...])
            dgcq_ref[...] += jnp.sum(d_ncq * cqh, axis=0, keepdims=True) * live
            dproj_sc[:, 0:Q_LORA] = _rms_bwd(d_ncq, cqh, r_cq, g_cq, Q_LORA).astype(MM)
            yield

            c_kv = cqkv_ref[:, Q_LORA:Q_LORA + KV_LORA]
            r_ckv = _rstd(c_kv, KV_LORA)
            ckvh = c_kv * r_ckv
            n_ckv = (ckvh * g_ckv).astype(MM)
            k_nope = _nn(n_ckv, w_ukv_ref[:, :N_HEADS * NOPE_DIM])
            kpe = cqkv_ref[:, Q_LORA + KV_LORA:Q_LORA + KV_LORA + LANES]
            ss_pe = jnp.sum(kpe * kpe, axis=-1, keepdims=True)
            g_kn, g_kp = g_k[:, :LANES], g_k[:, LANES:]
            d_kpe = jnp.zeros((T, LANES), f32)
            dgk_n = jnp.zeros((1, LANES), f32)
            dgk_p = jnp.zeros((1, LANES), f32)
            for hd in range(N_HEADS):
                kn = k_nope[:, hd * LANES:(hd + 1) * LANES]
                rk = lax.rsqrt((jnp.sum(kn * kn, axis=-1, keepdims=True) + ss_pe) * (1.0 / QK_DIM) + RMS_EPS)
                knh, kph = kn * rk, kpe * rk
                d_kn_n = dk_ref[:, hd * HEAD_PAD:hd * HEAD_PAD + LANES] * QK_SCALE
                d_kr = _rope_t(dk_ref[:, hd * HEAD_PAD + LANES:(hd + 1) * HEAD_PAD] * QK_SCALE, c, s1, s2)
                dgk_n += jnp.sum(d_kn_n * knh, axis=0, keepdims=True)
                dgk_p += jnp.sum(d_kr * kph, axis=0, keepdims=True)
                u_n, u_p = d_kn_n * g_kn, d_kr * g_kp
                mt = (jnp.sum(u_n * knh, axis=-1, keepdims=True) + jnp.sum(u_p * kph, axis=-1, keepdims=True)) * (1.0 / QK_DIM)
                dkvraw_sc[:, hd * LANES:(hd + 1) * LANES] = (rk * (u_n - knh * mt)).astype(MM)
                d_kpe += rk * (u_p - kph * mt)
                yield
            dgk_ref[:, :LANES] += dgk_n * live
            dgk_ref[:, LANES:] += dgk_p * live
            dkvraw_sc[:, N_HEADS * NOPE_DIM:] = dv_ref[...].astype(MM)
            d_kvraw = dkvraw_sc[...]
            dwukv_ref[...] += _tn((ckvh * (g_ckv * live)).astype(MM), d_kvraw)
            d_nckv = _nt(d_kvraw, w_ukv_ref[...])
            dgckv_ref[...] += jnp.sum(d_nckv * ckvh, axis=0, keepdims=True) * live
            dproj_sc[:, Q_LORA:Q_LORA + KV_LORA] = _rms_bwd(d_nckv, ckvh, r_ckv, g_ckv, KV_LORA).astype(MM)
            dproj_sc[:, Q_LORA + KV_LORA:Q_LORA + KV_LORA + LANES] = d_kpe.astype(MM)
            yield

            dproj_sc[:, 512:1024] = dza_ref[...]
            d_u = du_ref[...]
            nxt = dun_ref[...] * (i < nt - 1).astype(f32)
            d_m = cw_ref[2:3, :] * d_u + cw_ref[1:2, :] * _shift_up(d_u, nxt, 1) + cw_ref[0:1, :] * _shift_up(d_u, nxt, 2)
            dproj_sc[:, 1024:1536] = (d_m * ccx_ref[:, CONV_WIDTH:]).astype(MM)
            dproj_sc[:, 1536:2048] = (d_m * ccx_ref[:, :CONV_WIDTH]).astype(MM)
            dproj_sc[:, 2048:3072] = dcbzc_ref[...]

        def stage2(dproj_sc):
            g_in = g_in_ref[...]
            xv = x_ref[...]
            r_in = _rstd(xv, D_MODEL)
            xh = xv * r_in
            hb = (xh * g_in).astype(MM)
            yield
            d_h = jnp.zeros((T, D_MODEL), f32)
            for cb in range(PROJ_PAD // PIECE):
                cols = slice(cb * PIECE, (cb + 1) * PIECE)
                d_piece = dproj_sc[:, cols]
                dwin_sc[:, cols] += _tn(hb, d_piece)
                d_h = d_h + _nt(d_piece, w_in_ref[:, cols])
                yield
            dgin_ref[...] += jnp.sum(d_h * xh, axis=0, keepdims=True)
            gx_ref[...] = dx2_ref[...] + _rms_bwd(d_h, xh, r_in, g_in, D_MODEL)

        @pl.when(i % 2 == 0)
        def _():
            _alternate(stage2(dproj_b), stage1(dproj_a))

        @pl.when(i % 2 == 1)
        def _():
            _alternate(stage2(dproj_a), stage1(dproj_b))

        @pl.when(i == nt)
        def _():
            cp = pltpu.make_async_copy(dwin_sc, dwin_hbm, sem)
            cp.start()
            cp.wait()

    cur = lambda i: jnp.minimum(i, nt - 1)
    prv = lambda i: jnp.maximum(i - 1, 0)
    row = lambda w, blk=0: pl.BlockSpec((T, w), lambda i: (cur(i), blk))
    row2 = lambda w: pl.BlockSpec((T, w), lambda i: (prv(i), 0))
    acc = lambda shape: pl.BlockSpec(shape, lambda i: (0, 0))
    tb = T // 8
    sds = jax.ShapeDtypeStruct
    return pl.pallas_call(
        body, name="bwd_proj", grid=(nt + 1,),
        out_shape=[sds((S, D_MODEL), f32), sds((D_MODEL, PROJ_PAD), f32), sds((Q_LORA, QK_PAD), f32),
                   sds((KV_LORA, 2 * ATTN_WIDTH), f32), sds((1, D_MODEL), f32), sds((1, Q_LORA), f32), sds((1, KV_LORA), f32),
                   sds((1, HEAD_PAD), f32), sds((1, HEAD_PAD), f32)],
        in_specs=[row2(D_MODEL), row(512, 0), row(2 * CONV_WIDTH, 1), row(QK_PAD), row(QK_PAD), row(ATTN_WIDTH),
                  row(ATTN_WIDTH), row(2 * CONV_WIDTH), row(CONV_WIDTH),
                  pl.BlockSpec((8, CONV_WIDTH), lambda i: (jnp.minimum((cur(i) + 1) * tb, S // 8 - 1), 0)),
                  row2(D_MODEL), row(LANES), row(LANES), row(LANES),
                  _const((1, D_MODEL)), _const((D_MODEL, PROJ_PAD)), _const((1, Q_LORA)), _const((Q_LORA, QK_PAD)),
                  _const((1, KV_LORA)), _const((KV_LORA, 2 * ATTN_WIDTH)), _const((1, HEAD_PAD)), _const((1, HEAD_PAD)),
                  _const((8, CONV_WIDTH))],
        out_specs=[row2(D_MODEL), pl.BlockSpec(memory_space=pl.ANY), acc((Q_LORA, QK_PAD)), acc((KV_LORA, 2 * ATTN_WIDTH)),
                   acc((1, D_MODEL)), acc((1, Q_LORA)), acc((1, KV_LORA)), acc((1, HEAD_PAD)), acc((1, HEAD_PAD))],
        scratch_shapes=[pltpu.VMEM((T, PROJ_PAD), MM), pltpu.VMEM((T, PROJ_PAD), MM), pltpu.VMEM((T, QK_PAD), MM),
                        pltpu.VMEM((T, 2 * ATTN_WIDTH), MM), pltpu.VMEM((D_MODEL, PROJ_PAD), f32), pltpu.SemaphoreType.DMA],
        compiler_params=_cparams(("arbitrary",)),
    )(x, proj, proj, d_q, d_k, d_v, d_za, d_cbzc, d_u, d_u, d_x2, *tabs, g_in, w_in_p, g_cq, w_uq_p, g_ckv, w_ukv_p,
      g_q_p, g_k_p, conv_w8)


def _all_gather(block, name, in_vmem):
    m_per, n = block.shape

    def body(x_ref, out_ref, send_sems, recv_sems, local_sem):
        start, finish = _two_level_gather(x_ref, out_ref, send_sems, recv_sems, local_sem)
        start()
        finish()

    space = pltpu.VMEM if in_vmem else pl.ANY
    return pl.pallas_call(
        body, name=name, out_shape=jax.ShapeDtypeStruct((N_DEV * m_per, n), block.dtype),
        in_specs=[pl.BlockSpec(memory_space=space)], out_specs=pl.BlockSpec(memory_space=space),
        scratch_shapes=list(GATHER_SEMS),
    )(block)


def _sibling_exchange(pack):
    nchip, _, R, n = pack.shape

    def body(p_ref, got_ref, send_sems, recv_sems):
        x, y, c = lax.axis_index("x"), lax.axis_index("y"), lax.axis_index("c")
        copies = [pltpu.make_async_remote_copy(src_ref=p_ref.at[t, 1 - c], dst_ref=got_ref.at[t], send_sem=send_sems.at[t],
                                               recv_sem=recv_sems.at[t], device_id=(x, y, 1 - c), device_id_type=MESH)
                  for t in range(nchip)]
        for cp in copies:
            cp.start()
        for cp in copies:
            cp.wait()

    return pl.pallas_call(
        body, name="rs_sibling", out_shape=jax.ShapeDtypeStruct((nchip, R, n), pack.dtype),
        in_specs=[pl.BlockSpec(memory_space=pl.ANY)], out_specs=pl.BlockSpec(memory_space=pl.ANY),
        scratch_shapes=[pltpu.SemaphoreType.DMA((nchip,)), pltpu.SemaphoreType.DMA((nchip,))],
    )(pack)


def _chip_exchange(part):
    nchip, R, n = part.shape

    def body(p_ref, got_ref, send_sems, recv_sems, local_sem):
        x, y, c = lax.axis_index("x"), lax.axis_index("y"), lax.axis_index("c")
        my_chip = 2 * x + y
        chips = [(1 - x, y), (x, 1 - y), (1 - x, 1 - y)]
        mine = pltpu.make_async_copy(p_ref.at[my_chip], got_ref.at[my_chip], local_sem)
        mine.start()
        sends = []
        for k, (tx, ty) in enumerate(chips):
            cp = pltpu.make_async_remote_copy(src_ref=p_ref.at[2 * tx + ty], dst_ref=got_ref.at[my_chip],
                                              send_sem=send_sems.at[k], recv_sem=recv_sems.at[k],
                                              device_id=(tx, ty, c), device_id_type=MESH)
            cp.start()
            sends.append(cp)
        for k, (sx, sy) in enumerate(chips):
            pltpu.make_async_remote_copy(src_ref=p_ref.at[my_chip], dst_ref=got_ref.at[2 * sx + sy],
                                         send_sem=send_sems.at[k], recv_sem=recv_sems.at[k],
                                         device_id=(sx, sy, c), device_id_type=MESH).wait_recv()
        for cp in sends:
            cp.wait_send()
        mine.wait()

    return pl.pallas_call(
        body, name="rs_chips", out_shape=jax.ShapeDtypeStruct((nchip, R, n), part.dtype),
        in_specs=[pl.BlockSpec(memory_space=pl.ANY)], out_specs=pl.BlockSpec(memory_space=pl.ANY),
        scratch_shapes=[pltpu.SemaphoreType.DMA((3,)), pltpu.SemaphoreType.DMA((3,)), pltpu.SemaphoreType.DMA],
    )(part)


def _add_blocks(pack, from_sib, my_c):
    nb, _, R, n = pack.shape

    def body(c_ref, a_ref, b_ref, o_ref):
        o_ref[...] = (a_ref[0] + b_ref[...]).astype(o_ref.dtype)

    return pl.pallas_call(
        body, name="rs_add", out_shape=jax.ShapeDtypeStruct((nb, R, n), RS_DTYPE),
        grid_spec=pltpu.PrefetchScalarGridSpec(
            num_scalar_prefetch=1, grid=(nb,),
            in_specs=[pl.BlockSpec((1, 1, R, n), lambda i, c: (i, c[0], 0, 0)), pl.BlockSpec((1, R, n), lambda i, c: (i, 0, 0))],
            out_specs=pl.BlockSpec((1, R, n), lambda i, c: (i, 0, 0))),
        compiler_params=_cparams(("parallel",)))(my_c, pack, from_sib)


def _sum_slabs(got, name):
    nb, R, n = got.shape

    def body(g_ref, o_ref):
        tot = g_ref[0].astype(f32)
        for d in range(1, nb):
            tot = tot + g_ref[d].astype(f32)
        o_ref[...] = tot

    return pl.pallas_call(body, name=name, grid=(n // LANES,), out_shape=jax.ShapeDtypeStruct((R, n), f32),
                          in_specs=[pl.BlockSpec((nb, R, LANES), lambda j: (0, 0, j))], out_specs=pl.BlockSpec((R, LANES), lambda j: (0, j)),
                          compiler_params=_cparams(("parallel",)))(got)


def _adamw_all(ws, gs, ms, vs, gain_parts):
    n = len(ws)
    ng = len(GAIN_SLOTS)

    def body(*refs):
        gp_ref = refs[0]
        w_refs = refs[1:1 + n]
        g_refs = refs[1 + n:1 + 2 * n - ng]
        m_refs = refs[1 + 2 * n - ng:1 + 3 * n - ng]
        v_refs = refs[1 + 3 * n - ng:1 + 4 * n - ng]
        outs = refs[1 + 4 * n - ng:]
        gsum_ref, loss_ref = outs[0], outs[1]
        gg_refs = outs[2:2 + ng]
        d_refs, nm_refs, nv_refs = (outs[2 + ng + k * n:2 + ng + (k + 1) * n] for k in range(3))
        tot = gp_ref[0]
        for d in range(1, N_DEV):
            tot = tot + gp_ref[d]
        gsum_ref[...] = tot
        loss_ref[...] = gsum_ref[GAIN_ROWS - 1:GAIN_ROWS, :]
        for k in range(n):
            if k < ng:
                r0, width = GAIN_SLOTS[k]
                for r in range(width // LANES):
                    gg_refs[k][:, r * LANES:(r + 1) * LANES] = gsum_ref[r0 + r:r0 + r + 1, :]
                g = gg_refs[k][...]
            else:
                g = g_refs[k - ng][...]
            w = w_refs[k][...]
            m = ADAM_B1 * m_refs[k][...] + (1.0 - ADAM_B1) * g
            v = ADAM_B2 * v_refs[k][...] + (1.0 - ADAM_B2) * (g * g)
            m_hat = m / (1.0 - ADAM_B1 ** ADAM_STEP)
            v_hat = v / (1.0 - ADAM_B2 ** ADAM_STEP)
            d_refs[k][...] = -ADAM_LR * (m_hat / (jnp.sqrt(v_hat) + ADAM_EPS) + ADAM_WD * w)
            nm_refs[k][...] = m
            nv_refs[k][...] = v

    sds = jax.ShapeDtypeStruct
    like = [sds(w.shape, f32) for w in ws]
    out_shape = [sds((GAIN_ROWS, LANES), f32), sds((1, LANES), f32)] + like[:ng] + like * 3
    full = lambda a: pl.BlockSpec(a.shape, lambda i: (0,) * len(a.shape))
    operands = [gain_parts, *ws, *gs, *ms, *vs]
    outs = pl.pallas_call(
        body, name="adamw", grid=(1,), out_shape=out_shape, in_specs=[full(a) for a in operands],
        out_specs=[full(a) for a in out_shape], compiler_params=_cparams(("arbitrary",)),
    )(*operands)
    loss = outs[1]
    gg = outs[2:2 + ng]
    deltas, new_m, new_v = (outs[2 + ng + k * n:2 + ng + (k + 1) * n] for k in range(3))
    return loss, gg, deltas, new_m, new_v


GAIN_ROWS = 32
GAIN_SLOTS = [(0, 1024), (8, 256), (10, 128), (11, 256), (13, 256), (15, 512), (19, 512), (23, 1024)]

SH_IN = IN_TOTAL // N_DEV
SLAB1_COLS = 384
UQ_ROWS = Q_LORA * (N_HEADS * QK_DIM // N_DEV) // SLAB1_COLS
UKV_ROWS = 48
SLAB1_ROWS = D_MODEL + UQ_ROWS + UKV_ROWS
SH_ROWS = D_MODEL // N_DEV
PL_ROWS = PLE_DIM * (D_MODEL // N_DEV) // D_MODEL
CONV_ROWS = 16
SLAB2_ROWS = 2 * SH_ROWS + PL_ROWS + CONV_ROWS
IN_SEGMENTS = [(0, 448, 0), (448, 960, 512), (1472, 2496, 1024), (960, 1472, 2048), (2496, 3008, 2560)]


def _flat_rows(a, rows, cols):
    lead = a.shape[:-2]
    flat = a.reshape(lead + (-1,))
    pad = [(0, 0)] * len(lead) + [(0, rows * cols - flat.shape[-1])]
    return jnp.pad(flat, pad).reshape(lead + (rows, cols))


def _slab1(in_sh, uq_sh, ukv_sh):
    pad = [(0, 0)] * (in_sh.ndim - 1) + [(0, SLAB1_COLS - in_sh.shape[-1])]
    return jnp.concatenate([jnp.pad(in_sh, pad), _flat_rows(uq_sh, UQ_ROWS, SLAB1_COLS), _flat_rows(ukv_sh, UKV_ROWS, SLAB1_COLS)],
                           axis=-2)


def _slab2(o_sh, plg_sh, pl_sh, conv_sh):
    return jnp.concatenate([o_sh, plg_sh, _flat_rows(pl_sh, PL_ROWS, D_MODEL), _flat_rows(conv_sh, CONV_ROWS, D_MODEL)], axis=-2)


def _in_plan():
    src = [None] * PROJ_PAD
    for a, b, start in IN_SEGMENTS:
        for o in range(a, b):
            src[start + o - a] = (o // SH_IN, o % SH_IN)
    plan = []
    for t in range(PROJ_PAD // LANES):
        runs, j = [], 0
        key_of = lambda jj: None if src[t * LANES + jj] is None else (
            src[t * LANES + jj][0], src[t * LANES + jj][1] // LANES, src[t * LANES + jj][1] % LANES - jj)
        while j < LANES:
            key = key_of(j)
            lo = j
            while j < LANES and key_of(j) == key:
                j += 1
            if key is not None:
                runs.append(key + (lo, j))
        plan.append(runs)
    return plan


def _assemble_w_in(g1):
    plan = _in_plan()
    rows = 256

    def body(g_ref, out_ref):
        src_lane = lax.broadcasted_iota(jnp.int32, (LANES, LANES), 0)
        out_lane = lax.broadcasted_iota(jnp.int32, (LANES, LANES), 1)
        for t, runs in enumerate(plan):
            acc = jnp.zeros((rows, LANES), f32)
            for d, tile, shift, lo, hi in runs:
                pick = ((src_lane == out_lane + shift) & (out_lane >= lo) & (out_lane < hi)).astype(bf16)
                acc = acc + _nn(g_ref[d, :, tile * LANES:(tile + 1) * LANES], pick)
            out_ref[:, t * LANES:(t + 1) * LANES] = acc.astype(out_ref.dtype)

    return pl.pallas_call(
        body, name="w_in_columns", grid=(D_MODEL // rows,), out_shape=jax.ShapeDtypeStruct((D_MODEL, PROJ_PAD), MM),
        in_specs=[pl.BlockSpec((N_DEV, rows, SLAB1_COLS), lambda i: (0, i, 0))],
        out_specs=pl.BlockSpec((rows, PROJ_PAD), lambda i: (i, 0)),
        compiler_params=_cparams(("parallel",)))(g1)


def _by_device(a, width):
    return a.reshape(a.shape[0], N_DEV, width).transpose(1, 0, 2)


def _from_devices(a):
    return a.transpose(1, 0, 2).reshape(a.shape[1], -1)


def kernel(x, p, positions, g_in, w_in, g_cq, w_uq, g_ckv, w_ukv, g_q, g_k, conv_w, g_oa, g_oc, w_o, w_pl, w_plg, g_pl, loss_target, m_g_in, m_w_in, m_g_cq, m_w_uq, m_g_ckv, m_w_ukv, m_g_q, m_g_k, m_conv_w, m_g_oa, m_g_oc, m_w_o, m_w_pl, m_w_plg, m_g_pl, v_g_in, v_w_in, v_g_cq, v_w_uq, v_g_ckv, v_w_ukv, v_g_q, v_g_k, v_conv_w, v_g_oa, v_g_oc, v_w_o, v_w_pl, v_w_plg, v_g_pl):
    S = x.shape[1]
    nd = N_DEV
    xs, ps, tgt = x[0], p[0, 0], loss_target[0]

    slab1 = _slab1(w_in[0], w_uq[0], w_ukv[0]).astype(bf16)
    inv_freq = 1.0 / (ROPE_THETA ** (jnp.arange(0, ROPE_DIM, 2, dtype=f32) / ROPE_DIM))
    inv_row = jnp.concatenate([inv_freq, inv_freq, jnp.zeros((LANES - ROPE_DIM,), f32)]).reshape(1, LANES)
    posf = jnp.broadcast_to(positions[0].astype(f32)[:, None], (S, LANES))
    tabs, g1 = _rope_tables(posf, inv_row, slab1)
    g1 = g1.reshape(nd, SLAB1_ROWS, SLAB1_COLS)
    conv_bits = lax.bitcast_convert_type(conv_w[0], bf16).reshape(CONV_K, -1)
    slab2 = _slab2(w_o[0].astype(bf16), w_plg[0].astype(bf16), w_pl[0].astype(bf16), conv_bits)

    w_in_p = _assemble_w_in(g1)
    wuq = _from_devices(g1[:, D_MODEL:D_MODEL + UQ_ROWS].reshape(nd, Q_LORA, -1)).reshape(Q_LORA, N_HEADS, QK_DIM)
    w_uq_p = jnp.pad(wuq, ((0, 0), (0, 0), (0, HEAD_PAD - QK_DIM))).reshape(Q_LORA, QK_PAD)
    ukv_flat = g1[:, D_MODEL + UQ_ROWS:].reshape(nd, -1)[:, :KV_LORA * V_DIM]
    wukv = _from_devices(ukv_flat.reshape(nd, KV_LORA, V_DIM)).reshape(KV_LORA, N_HEADS, 2, NOPE_DIM)
    w_ukv_p = wukv.transpose(0, 2, 1, 3).reshape(KV_LORA, 2 * ATTN_WIDTH)
    w_in_p, w_uq_p, w_ukv_p = (w.astype(MM) for w in (w_in_p, w_uq_p, w_ukv_p))

    g_q_p = jnp.pad(g_q, ((0, 0), (0, HEAD_PAD - QK_DIM)))
    g_k_p = jnp.pad(g_k, ((0, 0), (0, HEAD_PAD - QK_DIM)))

    proj, q, k, v = _fwd_proj(xs, tabs, g_in, w_in_p, g_cq, w_uq_p, g_ckv, w_ukv_p, g_q_p, g_k_p)
    o, lse, g2 = _flash_fwd(q, k, v, slab2)
    w_o_f = g2[:, 0:SH_ROWS].reshape(D_MODEL, D_MODEL).astype(MM)
    w_plg_f = g2[:, SH_ROWS:2 * SH_ROWS].reshape(D_MODEL, D_MODEL).astype(MM)
    w_pl_f = _from_devices(g2[:, 2 * SH_ROWS:2 * SH_ROWS + PL_ROWS].reshape(nd, PLE_DIM, -1)).astype(MM)
    cv_bits = g2[:, 2 * SH_ROWS + PL_ROWS:].reshape(nd, -1)[:, :CONV_K * (CONV_WIDTH // nd) * 2]
    conv_full = _from_devices(lax.bitcast_convert_type(cv_bits.reshape(nd, CONV_K, CONV_WIDTH // nd, 2), f32))
    conv_w8 = jnp.pad(conv_full, ((0, 8 - CONV_K), (0, 0)))
    (d_x2, d_o, delta, d_za, d_cbzc, d_u, dw_o, dw_plg, dw_pl, dcw, dg_oa, dg_oc, dg_pl, loss_part) = _mid(
        o, proj, xs, ps, tgt, w_o_f, w_plg_f, w_pl_f, conv_w8, g_oa, g_oc, g_pl)

    slabs_a = _slab2(dw_o.reshape(nd, SH_ROWS, D_MODEL), dw_plg.reshape(nd, SH_ROWS, D_MODEL), _by_device(dw_pl, D_MODEL // nd),
                     _by_device(dcw[:CONV_K], CONV_WIDTH // nd)).astype(RS_DTYPE)
    d_q, d_k, d_v, got_a = _flash_bwd(q, k, v, d_o, lse, delta, slabs_a)
    shard_a = _sum_slabs(got_a, "rs_sum8")
    (grad_x, dw_in_p, dw_uq_p, dw_ukv_p, dg_in, dg_cq, dg_ckv, dg_q_p, dg_k_p) = _bwd_proj(
        xs, proj, d_q, d_k, d_v, d_za, d_cbzc, d_u, d_x2, tabs, g_in, w_in_p, g_cq, w_uq_p, g_ckv, w_ukv_p, g_q_p, g_k_p, conv_w8)

    def dw_in_cols(a, b):
        return [dw_in_p[:, start + max(a, s) - s:start + min(b, e) - s] for s, e, start in sorted(IN_SEGMENTS) if max(a, s) < min(b, e)]

    no_cols = jnp.zeros((D_MODEL, SLAB1_COLS - SH_IN), f32)
    in_shards = jnp.stack([jnp.concatenate(dw_in_cols(SH_IN * d, SH_IN * (d + 1)) + [no_cols], axis=1) for d in range(nd)])
    dw_uq = dw_uq_p.reshape(Q_LORA, N_HEADS, HEAD_PAD)[:, :, :QK_DIM].reshape(Q_LORA, N_HEADS * QK_DIM)
    dw_ukv = dw_ukv_p.reshape(KV_LORA, 2, N_HEADS, NOPE_DIM).transpose(0, 2, 1, 3).reshape(KV_LORA, 2 * ATTN_WIDTH)
    slabs_b = _slab1(in_shards, _by_device(dw_uq, N_HEADS * QK_DIM // nd), _by_device(dw_ukv, 2 * ATTN_WIDTH // nd))
    slabs_b = slabs_b.reshape(4, 2, SLAB1_ROWS, SLAB1_COLS)
    my_c = lax.axis_index("c").astype(jnp.int32).reshape(1)
    chip_part = _add_blocks(slabs_b, _sibling_exchange(slabs_b), my_c)
    shard_b = _sum_slabs(_chip_exchange(chip_part), "rs_sum")
    grads_w = [shard_b[:D_MODEL, :SH_IN], shard_b[D_MODEL:D_MODEL + UQ_ROWS].reshape(Q_LORA, -1),
               shard_b[D_MODEL + UQ_ROWS:].reshape(-1)[:KV_LORA * V_DIM].reshape(KV_LORA, V_DIM),
               shard_a[2 * SH_ROWS + PL_ROWS:].reshape(-1)[:CONV_K * CONV_WIDTH // nd].reshape(CONV_K, -1),
               shard_a[0:SH_ROWS], shard_a[2 * SH_ROWS:2 * SH_ROWS + PL_ROWS].reshape(PLE_DIM, -1), shard_a[SH_ROWS:2 * SH_ROWS]]

    gains_part = [dg_in, dg_cq, dg_ckv, dg_q_p, dg_k_p, dg_oa, dg_oc, dg_pl]
    gflat = jnp.concatenate([g.reshape(-1) for g in gains_part] + [jnp.zeros(((GAIN_ROWS - 1) * LANES - 3968,), f32),
                                                                    loss_part.reshape(-1)])
    gain_parts = _all_gather(gflat.reshape(GAIN_ROWS, LANES), "ag_gains", in_vmem=True).reshape(nd, GAIN_ROWS, LANES)

    gains = [g_in, g_cq, g_ckv, g_q_p, g_k_p, g_oa, g_oc, g_pl]
    padq = lambda a: jnp.pad(a, ((0, 0), (0, HEAD_PAD - QK_DIM)))
    m_gains = [m_g_in, m_g_cq, m_g_ckv, padq(m_g_q), padq(m_g_k), m_g_oa, m_g_oc, m_g_pl]
    v_gains = [v_g_in, v_g_cq, v_g_ckv, padq(v_g_q), padq(v_g_k), v_g_oa, v_g_oc, v_g_pl]
    tr = lambda a: jnp.swapaxes(a, 1, 2)
    turned = lambda l: [tr(l[0]), tr(l[1])] + list(l[2:])
    ws = gains + turned([w_in, w_uq, w_ukv, conv_w, w_o, w_pl, w_plg])
    ms = m_gains + turned([m_w_in, m_w_uq, m_w_ukv, m_conv_w, m_w_o, m_w_pl, m_w_plg])
    vs = v_gains + turned([v_w_in, v_w_uq, v_w_ukv, v_conv_w, v_w_o, v_w_pl, v_w_plg])
    loss_row, gg, deltas, new_m, new_v = _adamw_all(ws, turned([g[None] for g in grads_w]), ms, vs, gain_parts)
    grads_w = [g[None] for g in grads_w]
    ng = len(gains)
    deltas, new_m, new_v = (list(l[:ng]) + turned(l[ng:]) for l in (deltas, new_m, new_v))
    loss = loss_row[0, 0]

    def ordered(gl, wl):
        g_in_, g_cq_, g_ckv_, g_q_, g_k_, g_oa_, g_oc_, g_pl_ = gl
        g_q_, g_k_ = g_q_[:, :QK_DIM], g_k_[:, :QK_DIM]
        w_in_, w_uq_, w_ukv_, cw_, w_o_, w_pl_, w_plg_ = wl
        return [g_in_, w_in_, g_cq_, w_uq_, g_ckv_, w_ukv_, g_q_, g_k_, cw_, g_oa_, g_oc_, w_o_, w_pl_, w_plg_, g_pl_]

    ng = len(gains)
    outs = [loss, grad_x[None]]
    outs += ordered(gg, grads_w)
    for lst in (deltas, new_m, new_v):
        outs += ordered(lst[:ng], lst[ng:])
    return tuple(outs)
```

```python
import functools
import math

import jax
import jax.numpy as jnp
from jax import lax
from jax.experimental import pallas as pl
from jax.experimental.pallas import tpu as pltpu

f32 = jnp.float32
bf16 = jnp.bfloat16
MM = jnp.bfloat16
RS_DTYPE = jnp.bfloat16

D_MODEL = 1024
PLE_DIM = 256
N_HEADS = 4
NOPE_DIM = 128
ROPE_DIM = 64
V_DIM = 128
QK_DIM = NOPE_DIM + ROPE_DIM
Q_LORA = 256
KV_LORA = 128
ATTN_WIDTH = N_HEADS * V_DIM
CONV_WIDTH = D_MODEL - ATTN_WIDTH
CONV_K = 3
ROPE_THETA = 10000.0
RMS_EPS = 1e-6
NEG_INF = -1e30
IN_TOTAL = Q_LORA + KV_LORA + ROPE_DIM + ATTN_WIDTH + 4 * CONV_WIDTH
ADAM_LR = 0.001
ADAM_B1 = 0.9
ADAM_B2 = 0.999
ADAM_EPS = 1e-08
ADAM_WD = 0.01
ADAM_STEP = 10

N_DEV = 8
LANES = 128
HEAD_PAD = 256
QK_PAD = N_HEADS * HEAD_PAD
PROJ_PAD = 3072
TS = 512
TS_M = 256
TS_E = 256
TQ = 512
VMEM_LIMIT = 58 * 1024 * 1024
MESH = pl.DeviceIdType.MESH
QK_SCALE = 1.0 / math.sqrt(QK_DIM)
SCALE_LOG2E = QK_SCALE * math.log2(math.e)
DK_SCALE = QK_SCALE / SCALE_LOG2E


def _cparams(sem=None):
    return pltpu.CompilerParams(dimension_semantics=sem, vmem_limit_bytes=VMEM_LIMIT)


def _nt(a, b):
    return lax.dot_general(a, b, (((1,), (1,)), ((), ())), preferred_element_type=f32)


def _tn(a, b):
    return lax.dot_general(a, b, (((0,), (0,)), ((), ())), preferred_element_type=f32)


def _nn(a, b):
    return jnp.dot(a, b, preferred_element_type=f32)


def _rstd(x, n):
    return lax.rsqrt(jnp.sum(x * x, axis=-1, keepdims=True) * (1.0 / n) + RMS_EPS)


def _rms_bwd(dy, xhat, r, g, n):
    u = dy * g
    return r * (u - xhat * (jnp.sum(u * xhat, axis=-1, keepdims=True) * (1.0 / n)))


def _rope(t, c, s1, s2):
    return t * c + pltpu.roll(t, 32, 1) * s1 + pltpu.roll(t, 96, 1) * s2


def _rope_t(d, c, s1, s2):
    return d * c - pltpu.roll(d, 96, 1) * s2 - pltpu.roll(d, 32, 1) * s1


def _alternate(*stages):
    stages = list(stages)
    while stages:
        for st in list(stages):
            if next(st, "done") == "done":
                stages.remove(st)


def _const(shape):
    return pl.BlockSpec(shape, lambda *_: (0,) * len(shape), pipeline_mode=pl.Buffered(1))


def _two_level_gather(x_ref, out_ref, send_sems, recv_sems, local_sem):
    m_per = x_ref.shape[0]
    x, y, c = lax.axis_index("x"), lax.axis_index("y"), lax.axis_index("c")
    me, sibling = (x, y, c), (x, y, 1 - c)
    chips = [(1 - x, y), (x, 1 - y), (1 - x, 1 - y)]

    def rows(px, py, pc):
        return out_ref.at[pl.ds((4 * px + 2 * py + pc) * m_per, m_per), :]

    def copy(k, blk, to, src=None):
        return pltpu.make_async_remote_copy(
            src_ref=rows(*blk) if src is None else src, dst_ref=rows(*blk),
            send_sem=send_sems.at[k], recv_sem=recv_sems.at[k], device_id=to, device_id_type=MESH)

    mine = pltpu.make_async_copy(x_ref, rows(*me), local_sem)
    first = [copy(0, me, sibling, src=x_ref)] + [copy(1 + j, me, (*chip, c), src=x_ref) for j, chip in enumerate(chips)]
    passed = [copy(4 + j, (*chip, c), sibling) for j, chip in enumerate(chips)]

    def start():
        mine.start()
        for cp in first:
            cp.start()

    def finish():
        for j, chip in enumerate(chips):
            copy(1 + j, (*chip, c), me).wait_recv()
            passed[j].start()
        copy(0, sibling, me).wait_recv()
        for j, chip in enumerate(chips):
            copy(4 + j, (*chip, 1 - c), me).wait_recv()
        for cp in first + passed:
            cp.wait_send()
        mine.wait()

    return start, finish


GATHER_SEMS = [pltpu.SemaphoreType.DMA((7,)), pltpu.SemaphoreType.DMA((7,)), pltpu.SemaphoreType.DMA]


def _rope_tables(posf, inv_freq, block):
    S = posf.shape[0]
    t = min(S, TS)
    nt = S // t

    def body(pos_ref, f_ref, blk_ref, c_ref, s1_ref, s2_ref, all_ref, send_sems, recv_sems, local_sem):
        i = pl.program_id(0)
        start, finish = _two_level_gather(blk_ref, all_ref, send_sems, recv_sems, local_sem)

        @pl.when(i == 0)
        def _():
            start()

        ang = pos_ref[...] * f_ref[...]
        lane = lax.broadcasted_iota(jnp.int32, ang.shape, 1)
        cs, sn = jnp.cos(ang), jnp.sin(ang)
        c_ref[...] = jnp.where(lane < ROPE_DIM, cs, 0.0)
        s1_ref[...] = jnp.where((lane >= ROPE_DIM // 2) & (lane < ROPE_DIM), sn, 0.0)
        s2_ref[...] = jnp.where(lane < ROPE_DIM // 2, -sn, 0.0)

        @pl.when(i == nt - 1)
        def _():
            finish()

    spec = pl.BlockSpec((t, LANES), lambda i: (i, 0))
    hbm = pl.BlockSpec(memory_space=pl.ANY)
    outs = pl.pallas_call(
        body, name="rope_tables", grid=(nt,),
        out_shape=[jax.ShapeDtypeStruct((S, LANES), f32)] * 3 + [jax.ShapeDtypeStruct((N_DEV * block.shape[0], block.shape[1]), block.dtype)],
        in_specs=[spec, _const((1, LANES)), hbm], out_specs=[spec] * 3 + [hbm],
        scratch_shapes=list(GATHER_SEMS),
        compiler_params=_cparams(("arbitrary",)),
    )(posf, inv_freq, block)
    return outs[:3], outs[3]


def _fwd_proj(x, tabs, g_in, w_in_p, g_cq, w_uq_p, g_ckv, w_ukv_p, g_q_p, g_k_p):
    S = x.shape[0]
    T = min(TS, S)
    nt = S // T
    LAT = Q_LORA + KV_LORA + LANES
    PIECE = 512

    def body(x_ref, c_ref, s1_ref, s2_ref, g_in_ref, w_in_ref, g_cq_ref, w_uq_ref, g_ckv_ref, w_ukv_ref,
             g_q_ref, g_k_ref, proj_ref, q_ref, k_ref, v_ref, lat_a, lat_b):
        i = pl.program_id(0)

        @pl.when(i == 0)
        def _():
            lat_b[...] = jnp.zeros(lat_b.shape, f32)

        def stage1(lat):
            xv = x_ref[...]
            h = (xv * _rstd(xv, D_MODEL) * g_in_ref[...]).astype(MM)
            yield
            for cb in range(PROJ_PAD // PIECE):
                cols = slice(cb * PIECE, (cb + 1) * PIECE)
                proj_ref[:, cols] = _nn(h, w_in_ref[:, cols])
                yield
            lat[...] = proj_ref[:, 0:LAT]

        def stage2(lat):
            c, s1, s2 = c_ref[...], s1_ref[...], s2_ref[...]
            g_q, g_k = g_q_ref[...], g_k_ref[...]
            c_q = lat[:, 0:Q_LORA]
            n_cq = (c_q * _rstd(c_q, Q_LORA) * g_cq_ref[...]).astype(MM)
            q_raw = _nn(n_cq, w_uq_ref[...])
            for hd in range(N_HEADS):
                qh = q_raw[:, hd * HEAD_PAD:(hd + 1) * HEAD_PAD]
                qn = qh * _rstd(qh, QK_DIM) * g_q
                q_ref[:, hd * HEAD_PAD:hd * HEAD_PAD + LANES] = qn[:, :LANES].astype(MM)
                q_ref[:, hd * HEAD_PAD + LANES:(hd + 1) * HEAD_PAD] = _rope(qn[:, LANES:], c, s1, s2).astype(MM)
                yield

            c_kv = lat[:, Q_LORA:Q_LORA + KV_LORA]
            n_ckv = (c_kv * _rstd(c_kv, KV_LORA) * g_ckv_ref[...]).astype(MM)
            kv_raw = _nn(n_ckv, w_ukv_ref[...])
            kpe = lat[:, Q_LORA + KV_LORA:LAT]
            ss_pe = jnp.sum(kpe * kpe, axis=-1, keepdims=True)
            kr = _rope(kpe * g_k[:, LANES:], c, s1, s2)
            for hd in range(N_HEADS):
                kn = kv_raw[:, hd * LANES:(hd + 1) * LANES]
                rk = lax.rsqrt((jnp.sum(kn * kn, axis=-1, keepdims=True) + ss_pe) * (1.0 / QK_DIM) + RMS_EPS)
                k_ref[:, hd * HEAD_PAD:hd * HEAD_PAD + LANES] = (kn * rk * g_k[:, :LANES]).astype(MM)
                k_ref[:, hd * HEAD_PAD + LANES:(hd + 1) * HEAD_PAD] = (kr * rk).astype(MM)
                yield
            v_ref[...] = kv_raw[:, N_HEADS * NOPE_DIM:].astype(MM)

        @pl.when(i % 2 == 0)
        def _():
            _alternate(stage1(lat_a), stage2(lat_b))

        @pl.when(i % 2 == 1)
        def _():
            _alternate(stage1(lat_b), stage2(lat_a))

    cur = lambda w: pl.BlockSpec((T, w), lambda i: (jnp.minimum(i, nt - 1), 0))
    prv = lambda w: pl.BlockSpec((T, w), lambda i: (jnp.maximum(i - 1, 0), 0))
    return pl.pallas_call(
        body, name="fwd_proj", grid=(nt + 1,),
        out_shape=[jax.ShapeDtypeStruct((S, PROJ_PAD), f32), jax.ShapeDtypeStruct((S, QK_PAD), MM),
                   jax.ShapeDtypeStruct((S, QK_PAD), MM), jax.ShapeDtypeStruct((S, ATTN_WIDTH), MM)],
        in_specs=[cur(D_MODEL), prv(LANES), prv(LANES), prv(LANES), _const((1, D_MODEL)), _const((D_MODEL, PROJ_PAD)),
                  _const((1, Q_LORA)), _const((Q_LORA, QK_PAD)), _const((1, KV_LORA)), _const((KV_LORA, 2 * ATTN_WIDTH)),
                  _const((1, HEAD_PAD)), _const((1, HEAD_PAD))],
        out_specs=[cur(PROJ_PAD), prv(QK_PAD), prv(QK_PAD), prv(ATTN_WIDTH)],
        scratch_shapes=[pltpu.VMEM((T, LAT), f32), pltpu.VMEM((T, LAT), f32)],
        compiler_params=_cparams(("arbitrary",)),
    )(x, *tabs, g_in, w_in_p, g_cq, w_uq_p, g_ckv, w_ukv_p, g_q_p, g_k_p)


def _exchange(src_of, dst_ref, send_sems, recv_sems, local_sem):
    x, y, c = lax.axis_index("x"), lax.axis_index("y"), lax.axis_index("c")
    me = 4 * x + 2 * y + c
    remote = []
    for k in range(1, N_DEV):
        px = 1 - x if (k >> 2) & 1 else x
        py = 1 - y if (k >> 1) & 1 else y
        pc = 1 - c if k & 1 else c
        remote.append(pltpu.make_async_remote_copy(
            src_ref=src_of(4 * px + 2 * py + pc), dst_ref=dst_ref.at[me], send_sem=send_sems.at[k - 1],
            recv_sem=recv_sems.at[k - 1], device_id=(px, py, pc), device_id_type=MESH))
    return remote, pltpu.make_async_copy(src_of(me), dst_ref.at[me], local_sem)


def _exchange_start(copies):
    remote, local = copies
    local.start()
    for cp in remote:
        cp.start()


def _exchange_wait(copies):
    remote, local = copies
    for cp in remote:
        cp.wait_recv()
    for cp in remote:
        cp.wait_send()
    local.wait()


EXCHANGE_SEMS = [pltpu.SemaphoreType.DMA((N_DEV - 1,)), pltpu.SemaphoreType.DMA((N_DEV - 1,)), pltpu.SemaphoreType.DMA]


def _flash_fwd(q, k, v, block):
    S = q.shape[0]
    T = min(TQ, S)
    nq = S // T

    def body(q_ref, k_ref, v_ref, blk_ref, o_ref, lse_ref, all_ref, s_a, s_b, m_sc, l_sc, acc_sc, send_sems, recv_sems, local_sem):
        n = pl.program_id(1)
        gather = _exchange(lambda d: blk_ref, all_ref, send_sems, recv_sems, local_sem)

        @pl.when((pl.program_id(0) == 0) & (n == 0))
        def _():
            _exchange_start(gather)

        m_sc[...] = jnp.full(m_sc.shape, NEG_INF, f32)
        l_sc[...] = jnp.zeros(l_sc.shape, f32)
        acc_sc[...] = jnp.zeros(acc_sc.shape, f32)
        qv = q_ref[...]

        def rows(j):
            return pl.ds(pl.multiple_of(j * T, T), T)

        def scores(j, dst):
            dst[...] = _nt(k_ref[rows(j), :], qv)

        def update(src, j, masked):
            s = src[...]
            if masked:
                k_i = lax.broadcasted_iota(jnp.int32, (T, T), 0)
                q_i = lax.broadcasted_iota(jnp.int32, (T, T), 1)
                s = jnp.where(k_i <= q_i, s, NEG_INF)
            m_prev = m_sc[...]
            m_new = jnp.maximum(m_prev, jnp.max(s, axis=0, keepdims=True))
            alpha = jnp.exp2(m_prev - m_new)
            p = jnp.exp2(s - m_new[0:1, :])
            l_sc[...] = alpha * l_sc[...] + jnp.sum(p, axis=0, keepdims=True)
            acc_sc[...] = alpha[0:1, :] * acc_sc[...] + _tn(v_ref[rows(j), :], p.astype(MM))
            m_sc[...] = m_new

        scores(0, s_a)

        def pair(t, carry):
            j = 2 * t
            scores(j + 1, s_b)
            update(s_a, j, False)
            scores(j + 2, s_a)
            update(s_b, j + 1, False)
            return carry

        lax.fori_loop(0, n // 2, pair, 0)

        @pl.when(n % 2 == 0)
        def _():
            update(s_a, n, True)

        @pl.when(n % 2 == 1)
        def _():
            scores(n, s_b)
            update(s_a, n - 1, False)
            update(s_b, n, True)

        o_ref[...] = (acc_sc[...] / l_sc[0:1, :]).T
        lse_ref[...] = m_sc[...] + jnp.log2(l_sc[...])

        @pl.when((pl.program_id(0) == N_HEADS - 1) & (n == nq - 1))
        def _():
            _exchange_wait(gather)

    return pl.pallas_call(
        body, name="flash_fwd", grid=(N_HEADS, nq),
        out_shape=[jax.ShapeDtypeStruct((S, ATTN_WIDTH), f32), jax.ShapeDtypeStruct((N_HEADS * 8, S), f32),
                   jax.ShapeDtypeStruct((N_DEV,) + block.shape, block.dtype)],
        in_specs=[pl.BlockSpec((T, HEAD_PAD), lambda h, i: (i, h)),
                  pl.BlockSpec((S, HEAD_PAD), lambda h, i: (0, h)),
                  pl.BlockSpec((S, V_DIM), lambda h, i: (0, h)),
                  pl.BlockSpec(memory_space=pl.ANY)],
        out_specs=[pl.BlockSpec((T, V_DIM), lambda h, i: (i, h)), pl.BlockSpec((8, T), lambda h, i: (h, i)),
                   pl.BlockSpec(memory_space=pl.ANY)],
        scratch_shapes=[pltpu.VMEM((T, T), f32), pltpu.VMEM((T, T), f32), pltpu.VMEM((8, T), f32), pltpu.VMEM((8, T), f32),
                        pltpu.VMEM((V_DIM, T), f32)] + EXCHANGE_SEMS,
        compiler_params=_cparams(("arbitrary", "arbitrary")),
    )(q, k, v, block)


def _shift_down(m, prev8, n):
    T = m.shape[0]
    rows = lax.broadcasted_iota(jnp.int32, m.shape, 0)
    head = jnp.tile(pltpu.roll(prev8, n, 0), (T // 8, 1))
    return jnp.where(rows >= n, pltpu.roll(m, n, 0), head)


def _shift_up(m, next8, n):
    T = m.shape[0]
    rows = lax.broadcasted_iota(jnp.int32, m.shape, 0)
    tail = jnp.tile(pltpu.roll(next8, 8 - n, 0), (T // 8, 1))
    return jnp.where(rows < T - n, pltpu.roll(m, T - n, 0), tail)


def _mid(o, proj, x, p, target, w_o, w_plg, w_pl, conv_w8, g_oa, g_oc, g_pl):
    S = x.shape[0]
    T = min(TS_M, S)
    nt = S // T

    def body(o_ref, za_ref, ccx_ref, cbzc_ref, prev_ref, x_ref, p_ref, t_ref, wo_ref, wplg_ref, wpl_ref, cw_ref,
             goa_ref, goc_ref, gpl_ref,
             dx2_ref, do_ref, delta_ref, dza_ref, dcbzc_ref, du_ref,
             dwo_ref, dwplg_ref, dwpl_ref, dcw_ref, dgoa_ref, dgoc_ref, dgpl_ref, loss_ref):
        i = pl.program_id(0)

        @pl.when(i == 0)
        def _():
            for r in (dwo_ref, dwplg_ref, dwpl_ref, dcw_ref, dgoa_ref, dgoc_ref, dgpl_ref, loss_ref):
                r[...] = jnp.zeros(r.shape, f32)

        o_v = o_ref[...]
        z_a = za_ref[...]
        cc, cx = ccx_ref[:, :CONV_WIDTH], ccx_ref[:, CONV_WIDTH:]
        cb, z_c = cbzc_ref[:, :CONV_WIDTH], cbzc_ref[:, CONV_WIDTH:]
        g_oa, g_oc, g_pl = goa_ref[...], goc_ref[...], gpl_ref[...]
        w0, w1, w2 = cw_ref[0:1, :], cw_ref[1:2, :], cw_ref[2:3, :]
        p_b = p_ref[...].astype(MM)
        pw = _nn(p_b, wpl_ref[...])

        sa = jax.nn.sigmoid(z_a)
        silu_a = z_a * sa
        ga = o_v * silu_a
        ra = _rstd(ga, ATTN_WIDTH)
        gha = ga * ra
        ya = (gha * g_oa).astype(MM)
        x2 = x_ref[...] + _nn(ya, wo_ref[:ATTN_WIDTH, :])
        m0 = cc * cx
        prev = prev_ref[:, :CONV_WIDTH] * prev_ref[:, CONV_WIDTH:] * (i > 0).astype(f32)
        m1 = _shift_down(m0, prev, 1)
        m2 = _shift_down(m0, prev, 2)
        u = w0 * m2 + w1 * m1 + w2 * m0
        sc = jax.nn.sigmoid(z_c)
        silu_c = z_c * sc
        gc = cb * u * silu_c
        rc = _rstd(gc, CONV_WIDTH)
        ghc = gc * rc
        yc = (ghc * g_oc).astype(MM)
        ycat = jnp.concatenate([ya, yc], axis=-1)
        x2 = x2 + _nn(yc, wo_ref[ATTN_WIDTH:, :])
        r2 = _rstd(x2, D_MODEL)
        xh2 = x2 * r2
        n2 = (xh2 * g_pl).astype(MM)
        gate = jax.nn.sigmoid(_nn(n2, wplg_ref[...]))
        err = x2 + gate * pw - t_ref[...]
        loss_ref[...] += jnp.sum(jnp.sum(err * err, axis=-1, keepdims=True), axis=0, keepdims=True) * (0.5 / D_MODEL)
        d_out = err * (1.0 / D_MODEL)
        d_glog = (d_out * pw * gate * (1.0 - gate)).astype(MM)
        d_n2 = _nt(d_glog, wplg_ref[...])
        dwpl_ref[...] += _tn(p_b, (d_out * gate).astype(MM))
        dwplg_ref[...] += _tn(n2, d_glog)
        dgpl_ref[...] += jnp.sum(d_n2 * xh2, axis=0, keepdims=True)
        d_x2 = d_out + _rms_bwd(d_n2, xh2, r2, g_pl, D_MODEL)
        dx2_ref[...] = d_x2
        d_x2b = d_x2.astype(MM)
        d_ycat = _nt(d_x2b, wo_ref[...])
        dwo_ref[...] += _tn(ycat, d_x2b)
        d_ya, d_yc = d_ycat[:, :ATTN_WIDTH], d_ycat[:, ATTN_WIDTH:]
        dgoa_ref[...] += jnp.sum(d_ya * gha, axis=0, keepdims=True)
        dgoc_ref[...] += jnp.sum(d_yc * ghc, axis=0, keepdims=True)
        d_ga = _rms_bwd(d_ya, gha, ra, g_oa, ATTN_WIDTH)
        d_gc = _rms_bwd(d_yc, ghc, rc, g_oc, CONV_WIDTH)
        d_o = d_ga * silu_a
        do_ref[...] = d_o.astype(MM)
        dza_ref[...] = (d_ga * o_v * (sa * (1.0 + z_a * (1.0 - sa)))).astype(MM)
        for hd in range(N_HEADS):
            cols = slice(hd * V_DIM, (hd + 1) * V_DIM)
            dl = jnp.sum(d_o[:, cols] * o_v[:, cols], axis=-1, keepdims=True)
            delta_ref[hd * 8:(hd + 1) * 8, :] = jnp.broadcast_to(dl, (T, LANES)).T[0:8, :]
        d_u = d_gc * cb * silu_c
        du_ref[...] = d_u
        dcbzc_ref[:, :CONV_WIDTH] = (d_gc * u * silu_c).astype(MM)
        dcbzc_ref[:, CONV_WIDTH:] = (d_gc * cb * u * (sc * (1.0 + z_c * (1.0 - sc)))).astype(MM)
        dcw_ref[0:1, :] += jnp.sum(d_u * m2, axis=0, keepdims=True)
        dcw_ref[1:2, :] += jnp.sum(d_u * m1, axis=0, keepdims=True)
        dcw_ref[2:3, :] += jnp.sum(d_u * m0, axis=0, keepdims=True)

    row = lambda w, blk=0: pl.BlockSpec((T, w), lambda i: (i, blk))
    acc = lambda shape: pl.BlockSpec(shape, lambda i: (0, 0))
    tb = T // 8
    sds = jax.ShapeDtypeStruct
    return pl.pallas_call(
        body, name="mid", grid=(nt,),
        out_shape=[sds((S, D_MODEL), f32), sds((S, ATTN_WIDTH), MM), sds((N_HEADS * 8, S), f32), sds((S, ATTN_WIDTH), MM),
                   sds((S, 2 * CONV_WIDTH), MM), sds((S, CONV_WIDTH), f32),
                   sds((D_MODEL, D_MODEL), f32), sds((D_MODEL, D_MODEL), f32), sds((PLE_DIM, D_MODEL), f32), sds((8, CONV_WIDTH), f32),
                   sds((1, ATTN_WIDTH), f32), sds((1, CONV_WIDTH), f32), sds((1, D_MODEL), f32), sds((1, LANES), f32)],
        in_specs=[row(ATTN_WIDTH), row(ATTN_WIDTH, 1), row(2 * CONV_WIDTH, 1), row(2 * CONV_WIDTH, 2),
                  pl.BlockSpec((8, 2 * CONV_WIDTH), lambda i: (jnp.maximum(i * tb - 1, 0), 1)),
                  row(D_MODEL), row(PLE_DIM), row(D_MODEL),
                  _const((D_MODEL, D_MODEL)), _const((D_MODEL, D_MODEL)), _const((PLE_DIM, D_MODEL)), _const((8, CONV_WIDTH)),
                  _const((1, ATTN_WIDTH)), _const((1, CONV_WIDTH)), _const((1, D_MODEL))],
        out_specs=[row(D_MODEL), row(ATTN_WIDTH), pl.BlockSpec((N_HEADS * 8, T), lambda i: (0, i)), row(ATTN_WIDTH),
                   row(2 * CONV_WIDTH), row(CONV_WIDTH),
                   acc((D_MODEL, D_MODEL)), acc((D_MODEL, D_MODEL)), acc((PLE_DIM, D_MODEL)), acc((8, CONV_WIDTH)),
                   acc((1, ATTN_WIDTH)), acc((1, CONV_WIDTH)), acc((1, D_MODEL)), acc((1, LANES))],
        compiler_params=_cparams(("arbitrary",)),
    )(o, proj, proj, proj, proj, x, p, target, w_o, w_plg, w_pl, conv_w8, g_oa, g_oc, g_pl)


def _flash_bwd(q, k, v, d_o, lse2, delta, by_target):
    S = q.shape[0]
    T = min(TQ, S)
    nq = S // T

    def body(k_ref, v_ref, q_ref, do_ref, lse_ref, delta_ref, out_ref, dq_ref, dk_ref, dv_ref, in_ref, s_a, dp_a, s_b, dp_b,
             send_sems, recv_sems, local_sem):
        j = pl.program_id(1)
        scatter = _exchange(lambda d: out_ref.at[d], in_ref, send_sems, recv_sems, local_sem)

        @pl.when((pl.program_id(0) == 0) & (j == 0))
        def _():
            _exchange_start(scatter)

        @pl.when(j == 0)
        def _():
            dq_ref[...] = jnp.zeros(dq_ref.shape, f32)

        dk_ref[...] = jnp.zeros(dk_ref.shape, f32)
        dv_ref[...] = jnp.zeros(dv_ref.shape, f32)
        kv, vv = k_ref[...], v_ref[...]

        def rows_of(i):
            return pl.ds(pl.multiple_of(i * T, T), T)

        def scores(i, s_dst, dp_dst):
            rows = rows_of(i)
            s_dst[...] = _nt(kv, q_ref[rows, :])
            dp_dst[...] = _nt(vv, do_ref[rows, :])

        def grads(i, s_src, dp_src, masked):
            rows = rows_of(i)
            s = s_src[...]
            if masked:
                k_i = lax.broadcasted_iota(jnp.int32, (T, T), 0)
                q_i = lax.broadcasted_iota(jnp.int32, (T, T), 1)
                s = jnp.where(k_i <= q_i, s, NEG_INF)
            p = jnp.exp2(s - lse_ref[0:1, rows])
            ds = (p * (dp_src[...] - delta_ref[0:1, rows])).astype(MM)
            dv_ref[...] += _nn(p.astype(MM), do_ref[rows, :])
            dk_ref[...] += _nn(ds, q_ref[rows, :])
            dq_ref[rows, :] += _tn(ds, kv)

        last = nq - 1
        scores(j, s_a, dp_a)
        scores(jnp.minimum(j + 1, last), s_b, dp_b)
        grads(j, s_a, dp_a, True)
        rest = last - j

        def pair(t, carry):
            u = j + 1 + 2 * t
            scores(u + 1, s_a, dp_a)
            grads(u, s_b, dp_b, False)
            scores(jnp.minimum(u + 2, last), s_b, dp_b)
            grads(u + 1, s_a, dp_a, False)
            return carry

        lax.fori_loop(0, rest // 2, pair, 0)

        @pl.when(rest % 2 == 1)
        def _():
            grads(last, s_b, dp_b, False)

        @pl.when((pl.program_id(0) == N_HEADS - 1) & (j == nq - 1))
        def _():
            _exchange_wait(scatter)

    whole = lambda w: pl.BlockSpec((S, w), lambda h, j: (0, h))
    stat = pl.BlockSpec((8, S), lambda h, j: (h, 0))
    return pl.pallas_call(
        body, name="flash_bwd", grid=(N_HEADS, nq),
        out_shape=[jax.ShapeDtypeStruct((S, QK_PAD), f32), jax.ShapeDtypeStruct((S, QK_PAD), f32),
                   jax.ShapeDtypeStruct((S, ATTN_WIDTH), f32), jax.ShapeDtypeStruct(by_target.shape, by_target.dtype)],
        in_specs=[pl.BlockSpec((T, HEAD_PAD), lambda h, j: (j, h)), pl.BlockSpec((T, V_DIM), lambda h, j: (j, h)),
                  whole(HEAD_PAD), whole(V_DIM), stat, stat, pl.BlockSpec(memory_space=pl.ANY)],
        out_specs=[whole(HEAD_PAD), pl.BlockSpec((T, HEAD_PAD), lambda h, j: (j, h)),
                   pl.BlockSpec((T, V_DIM), lambda h, j: (j, h)), pl.BlockSpec(memory_space=pl.ANY)],
        scratch_shapes=[pltpu.VMEM((T, T), f32)] * 4 + EXCHANGE_SEMS,
        compiler_params=_cparams(("arbitrary", "arbitrary")),
    )(k, v, q, d_o, lse2, delta, by_target)


def _bwd_proj(x, proj, d_q, d_k, d_v, d_za, d_cbzc, d_u, d_x2, tabs, g_in, w_in_p, g_cq, w_uq_p, g_ckv, w_ukv_p,
              g_q_p, g_k_p, conv_w8):
    S = x.shape[0]
    T = min(TS_E, S)
    nt = S // T
    PIECE = 512

    def body(x_ref, cqkv_ref, ccx_ref, dq_ref, dk_ref, dv_ref, dza_ref, dcbzc_ref, du_ref, dun_ref, dx2_ref,
             c_ref, s1_ref, s2_ref, g_in_ref, w_in_ref, g_cq_ref, w_uq_ref, g_ckv_ref, w_ukv_ref, g_q_ref, g_k_ref, cw_ref,
             gx_ref, dwin_hbm, dwuq_ref, dwukv_ref, dgin_ref, dgcq_ref, dgckv_ref, dgq_ref, dgk_ref,
             dproj_a, dproj_b, dqraw_sc, dkvraw_sc, dwin_sc, sem):
        i = pl.program_id(0)

        @pl.when(i == 0)
        def _():
            dwin_sc[...] = jnp.zeros(dwin_sc.shape, f32)
            dproj_b[...] = jnp.zeros(dproj_b.shape, MM)
            for r in (dwuq_ref, dwukv_ref, dgin_ref, dgcq_ref, dgckv_ref, dgq_ref, dgk_ref):
                r[...] = jnp.zeros(r.shape, f32)

        def stage1(dproj_sc):
            live = (i < nt).astype(f32)
            c, s1, s2 = c_ref[...], s1_ref[...], s2_ref[...]
            g_q, g_k = g_q_ref[...], g_k_ref[...]
            g_cq, g_ckv = g_cq_ref[...], g_ckv_ref[...]

            c_q = cqkv_ref[:, 0:Q_LORA]
            r_cq = _rstd(c_q, Q_LORA)
            cqh = c_q * r_cq
            n_cq = (cqh * g_cq).astype(MM)
            q_raw = _nn(n_cq, w_uq_ref[...])
            dgq = jnp.zeros((1, HEAD_PAD), f32)
            for hd in range(N_HEADS):
                cols = slice(hd * HEAD_PAD, (hd + 1) * HEAD_PAD)
                qh = q_raw[:, cols]
                rq = _rstd(qh, QK_DIM)
                qhh = qh * rq
                dqh = dq_ref[:, cols] * QK_SCALE
                d_qn = jnp.concatenate([dqh[:, :LANES], _rope_t(dqh[:, LANES:], c, s1, s2)], axis=-1)
                dgq += jnp.sum(d_qn * qhh, axis=0, keepdims=True)
                dqraw_sc[:, cols] = _rms_bwd(d_qn, qhh, rq, g_q, QK_DIM).astype(MM)
                yield
            dgq_ref[...] += dgq * live
            d_qraw = dqraw_sc[...]
            dwuq_ref[...] += _tn((cqh * (g_cq * live)).astype(MM), d_qraw)
            d_ncq = _nt(d_qraw, w_uq_ref[...])
            dgcq_ref[...] += jnp.sum(d_ncq * cqh, axis=0, keepdims=True) * live
            dproj_sc[:, 0:Q_LORA] = _rms_bwd(d_ncq, cqh, r_cq, g_cq, Q_LORA).astype(MM)
            yield

            c_kv = cqkv_ref[:, Q_LORA:Q_LORA + KV_LORA]
            r_ckv = _rstd(c_kv, KV_LORA)
            ckvh = c_kv * r_ckv
            n_ckv = (ckvh * g_ckv).astype(MM)
            k_nope = _nn(n_ckv, w_ukv_ref[:, :N_HEADS * NOPE_DIM])
            kpe = cqkv_ref[:, Q_LORA + KV_LORA:Q_LORA + KV_LORA + LANES]
            ss_pe = jnp.sum(kpe * kpe, axis=-1, keepdims=True)
            g_kn, g_kp = g_k[:, :LANES], g_k[:, LANES:]
            d_kpe = jnp.zeros((T, LANES), f32)
            dgk_n = jnp.zeros((1, LANES), f32)
            dgk_p = jnp.zeros((1, LANES), f32)
            for hd in range(N_HEADS):
                kn = k_nope[:, hd * LANES:(hd + 1) * LANES]
                rk = lax.rsqrt((jnp.sum(kn * kn, axis=-1, keepdims=True) + ss_pe) * (1.0 / QK_DIM) + RMS_EPS)
                knh, kph = kn * rk, kpe * rk
                d_kn_n = dk_ref[:, hd * HEAD_PAD:hd * HEAD_PAD + LANES] * DK_SCALE
                d_kr = _rope_t(dk_ref[:, hd * HEAD_PAD + LANES:(hd + 1) * HEAD_PAD] * DK_SCALE, c, s1, s2)
                dgk_n += jnp.sum(d_kn_n * knh, axis=0, keepdims=True)
                dgk_p += jnp.sum(d_kr * kph, axis=0, keepdims=True)
                u_n, u_p = d_kn_n * g_kn, d_kr * g_kp
                mt = (jnp.sum(u_n * knh, axis=-1, keepdims=True) + jnp.sum(u_p * kph, axis=-1, keepdims=True)) * (1.0 / QK_DIM)
                dkvraw_sc[:, hd * LANES:(hd + 1) * LANES] = (rk * (u_n - knh * mt)).astype(MM)
                d_kpe += rk * (u_p - kph * mt)
                yield
            dgk_ref[:, :LANES] += dgk_n * live
            dgk_ref[:, LANES:] += dgk_p * live
            dkvraw_sc[:, N_HEADS * NOPE_DIM:] = dv_ref[...].astype(MM)
            d_kvraw = dkvraw_sc[...]
            dwukv_ref[...] += _tn((ckvh * (g_ckv * live)).astype(MM), d_kvraw)
            d_nckv = _nt(d_kvraw, w_ukv_ref[...])
            dgckv_ref[...] += jnp.sum(d_nckv * ckvh, axis=0, keepdims=True) * live
            dproj_sc[:, Q_LORA:Q_LORA + KV_LORA] = _rms_bwd(d_nckv, ckvh, r_ckv, g_ckv, KV_LORA).astype(MM)
            dproj_sc[:, Q_LORA + KV_LORA:Q_LORA + KV_LORA + LANES] = d_kpe.astype(MM)
            yield

            dproj_sc[:, 512:1024] = dza_ref[...]
            d_u = du_ref[...]
            nxt = dun_ref[...] * (i < nt - 1).astype(f32)
            d_m = cw_ref[2:3, :] * d_u + cw_ref[1:2, :] * _shift_up(d_u, nxt, 1) + cw_ref[0:1, :] * _shift_up(d_u, nxt, 2)
            dproj_sc[:, 1024:1536] = (d_m * ccx_ref[:, CONV_WIDTH:]).astype(MM)
            dproj_sc[:, 1536:2048] = (d_m * ccx_ref[:, :CONV_WIDTH]).astype(MM)
            dproj_sc[:, 2048:3072] = dcbzc_ref[...]

        def stage2(dproj_sc):
            g_in = g_in_ref[...]
            xv = x_ref[...]
            r_in = _rstd(xv, D_MODEL)
            xh = xv * r_in
            hb = (xh * g_in).astype(MM)
            yield
            d_h = jnp.zeros((T, D_MODEL), f32)
            for cb in range(PROJ_PAD // PIECE):
                cols = slice(cb * PIECE, (cb + 1) * PIECE)
                d_piece = dproj_sc[:, cols]
                dwin_sc[:, cols] += _tn(hb, d_piece)
                d_h = d_h + _nt(d_piece, w_in_ref[:, cols])
                yield
            dgin_ref[...] += jnp.sum(d_h * xh, axis=0, keepdims=True)
            gx_ref[...] = dx2_ref[...] + _rms_bwd(d_h, xh, r_in, g_in, D_MODEL)

        @pl.when(i % 2 == 0)
        def _():
            _alternate(stage2(dproj_b), stage1(dproj_a))

        @pl.when(i % 2 == 1)
        def _():
            _alternate(stage2(dproj_a), stage1(dproj_b))

        @pl.when(i == nt)
        def _():
            cp = pltpu.make_async_copy(dwin_sc, dwin_hbm, sem)
            cp.start()
            cp.wait()

    cur = lambda i: jnp.minimum(i, nt - 1)
    prv = lambda i: jnp.maximum(i - 1, 0)
    row = lambda w, blk=0: pl.BlockSpec((T, w), lambda i: (cur(i), blk))
    row2 = lambda w: pl.BlockSpec((T, w), lambda i: (prv(i), 0))
    acc = lambda shape: pl.BlockSpec(shape, lambda i: (0, 0))
    tb = T // 8
    sds = jax.ShapeDtypeStruct
    return pl.pallas_call(
        body, name="bwd_proj", grid=(nt + 1,),
        out_shape=[sds((S, D_MODEL), f32), sds((D_MODEL, PROJ_PAD), f32), sds((Q_LORA, QK_PAD), f32),
                   sds((KV_LORA, 2 * ATTN_WIDTH), f32), sds((1, D_MODEL), f32), sds((1, Q_LORA), f32), sds((1, KV_LORA), f32),
                   sds((1, HEAD_PAD), f32), sds((1, HEAD_PAD), f32)],
        in_specs=[row2(D_MODEL), row(512, 0), row(2 * CONV_WIDTH, 1), row(QK_PAD), row(QK_PAD), row(ATTN_WIDTH),
                  row(ATTN_WIDTH), row(2 * CONV_WIDTH), row(CONV_WIDTH),
                  pl.BlockSpec((8, CONV_WIDTH), lambda i: (jnp.minimum((cur(i) + 1) * tb, S // 8 - 1), 0)),
                  row2(D_MODEL), row(LANES), row(LANES), row(LANES),
                  _const((1, D_MODEL)), _const((D_MODEL, PROJ_PAD)), _const((1, Q_LORA)), _const((Q_LORA, QK_PAD)),
                  _const((1, KV_LORA)), _const((KV_LORA, 2 * ATTN_WIDTH)), _const((1, HEAD_PAD)), _const((1, HEAD_PAD)),
                  _const((8, CONV_WIDTH))],
        out_specs=[row2(D_MODEL), pl.BlockSpec(memory_space=pl.ANY), acc((Q_LORA, QK_PAD)), acc((KV_LORA, 2 * ATTN_WIDTH)),
                   acc((1, D_MODEL)), acc((1, Q_LORA)), acc((1, KV_LORA)), acc((1, HEAD_PAD)), acc((1, HEAD_PAD))],
        scratch_shapes=[pltpu.VMEM((T, PROJ_PAD), MM), pltpu.VMEM((T, PROJ_PAD), MM), pltpu.VMEM((T, QK_PAD), MM),
                        pltpu.VMEM((T, 2 * ATTN_WIDTH), MM), pltpu.VMEM((D_MODEL, PROJ_PAD), f32), pltpu.SemaphoreType.DMA],
        compiler_params=_cparams(("arbitrary",)),
    )(x, proj, proj, d_q, d_k, d_v, d_za, d_cbzc, d_u, d_u, d_x2, *tabs, g_in, w_in_p, g_cq, w_uq_p, g_ckv, w_ukv_p,
      g_q_p, g_k_p, conv_w8)


def _all_gather(block, name, in_vmem):
    m_per, n = block.shape

    def body(x_ref, out_ref, send_sems, recv_sems, local_sem):
        start, finish = _two_level_gather(x_ref, out_ref, send_sems, recv_sems, local_sem)
        start()
        finish()

    space = pltpu.VMEM if in_vmem else pl.ANY
    return pl.pallas_call(
        body, name=name, out_shape=jax.ShapeDtypeStruct((N_DEV * m_per, n), block.dtype),
        in_specs=[pl.BlockSpec(memory_space=space)], out_specs=pl.BlockSpec(memory_space=space),
        scratch_shapes=list(GATHER_SEMS),
    )(block)


def _sibling_exchange(pack):
    nchip, _, R, n = pack.shape

    def body(p_ref, got_ref, send_sems, recv_sems):
        x, y, c = lax.axis_index("x"), lax.axis_index("y"), lax.axis_index("c")
        copies = [pltpu.make_async_remote_copy(src_ref=p_ref.at[t, 1 - c], dst_ref=got_ref.at[t], send_sem=send_sems.at[t],
                                               recv_sem=recv_sems.at[t], device_id=(x, y, 1 - c), device_id_type=MESH)
                  for t in range(nchip)]
        for cp in copies:
            cp.start()
        for cp in copies:
            cp.wait()

    return pl.pallas_call(
        body, name="rs_sibling", out_shape=jax.ShapeDtypeStruct((nchip, R, n), pack.dtype),
        in_specs=[pl.BlockSpec(memory_space=pl.ANY)], out_specs=pl.BlockSpec(memory_space=pl.ANY),
        scratch_shapes=[pltpu.SemaphoreType.DMA((nchip,)), pltpu.SemaphoreType.DMA((nchip,))],
    )(pack)


def _chip_exchange(part):
    nchip, R, n = part.shape

    def body(p_ref, got_ref, send_sems, recv_sems, local_sem):
        x, y, c = lax.axis_index("x"), lax.axis_index("y"), lax.axis_index("c")
        my_chip = 2 * x + y
        chips = [(1 - x, y), (x, 1 - y), (1 - x, 1 - y)]
        mine = pltpu.make_async_copy(p_ref.at[my_chip], got_ref.at[my_chip], local_sem)
        mine.start()
        sends = []
        for k, (tx, ty) in enumerate(chips):
            cp = pltpu.make_async_remote_copy(src_ref=p_ref.at[2 * tx + ty], dst_ref=got_ref.at[my_chip],
                                              send_sem=send_sems.at[k], recv_sem=recv_sems.at[k],
                                              device_id=(tx, ty, c), device_id_type=MESH)
            cp.start()
            sends.append(cp)
        for k, (sx, sy) in enumerate(chips):
            pltpu.make_async_remote_copy(src_ref=p_ref.at[my_chip], dst_ref=got_ref.at[2 * sx + sy],
                                         send_sem=send_sems.at[k], recv_sem=recv_sems.at[k],
                                         device_id=(sx, sy, c), device_id_type=MESH).wait_recv()
        for cp in sends:
            cp.wait_send()
        mine.wait()

    return pl.pallas_call(
        body, name="rs_chips", out_shape=jax.ShapeDtypeStruct((nchip, R, n), part.dtype),
        in_specs=[pl.BlockSpec(memory_space=pl.ANY)], out_specs=pl.BlockSpec(memory_space=pl.ANY),
        scratch_shapes=[pltpu.SemaphoreType.DMA((3,)), pltpu.SemaphoreType.DMA((3,)), pltpu.SemaphoreType.DMA],
    )(part)


def _add_blocks(pack, from_sib, my_c):
    nb, _, R, n = pack.shape

    def body(c_ref, a_ref, b_ref, o_ref):
        o_ref[...] = (a_ref[0] + b_ref[...]).astype(o_ref.dtype)

    return pl.pallas_call(
        body, name="rs_add", out_shape=jax.ShapeDtypeStruct((nb, R, n), RS_DTYPE),
        grid_spec=pltpu.PrefetchScalarGridSpec(
            num_scalar_prefetch=1, grid=(nb,),
            in_specs=[pl.BlockSpec((1, 1, R, n), lambda i, c: (i, c[0], 0, 0)), pl.BlockSpec((1, R, n), lambda i, c: (i, 0, 0))],
            out_specs=pl.BlockSpec((1, R, n), lambda i, c: (i, 0, 0))),
        compiler_params=_cparams(("parallel",)))(my_c, pack, from_sib)


def _sum_slabs(got, name):
    nb, R, n = got.shape

    def body(g_ref, o_ref):
        tot = g_ref[0].astype(f32)
        for d in range(1, nb):
            tot = tot + g_ref[d].astype(f32)
        o_ref[...] = tot

    return pl.pallas_call(body, name=name, grid=(n // LANES,), out_shape=jax.ShapeDtypeStruct((R, n), f32),
                          in_specs=[pl.BlockSpec((nb, R, LANES), lambda j: (0, 0, j))], out_specs=pl.BlockSpec((R, LANES), lambda j: (0, j)),
                          compiler_params=_cparams(("parallel",)))(got)


def _adamw_all(ws, gs, ms, vs, gain_parts):
    n = len(ws)
    ng = len(GAIN_SLOTS)

    def body(*refs):
        gp_ref = refs[0]
        w_refs = refs[1:1 + n]
        g_refs = refs[1 + n:1 + 2 * n - ng]
        m_refs = refs[1 + 2 * n - ng:1 + 3 * n - ng]
        v_refs = refs[1 + 3 * n - ng:1 + 4 * n - ng]
        outs = refs[1 + 4 * n - ng:]
        gsum_ref, loss_ref = outs[0], outs[1]
        gg_refs = outs[2:2 + ng]
        d_refs, nm_refs, nv_refs = (outs[2 + ng + k * n:2 + ng + (k + 1) * n] for k in range(3))
        tot = gp_ref[0]
        for d in range(1, N_DEV):
            tot = tot + gp_ref[d]
        gsum_ref[...] = tot
        loss_ref[...] = gsum_ref[GAIN_ROWS - 1:GAIN_ROWS, :]
        for k in range(n):
            if k < ng:
                r0, width = GAIN_SLOTS[k]
                for r in range(width // LANES):
                    gg_refs[k][:, r * LANES:(r + 1) * LANES] = gsum_ref[r0 + r:r0 + r + 1, :]
                g = gg_refs[k][...]
            else:
                g = g_refs[k - ng][...]
            w = w_refs[k][...]
            m = ADAM_B1 * m_refs[k][...] + (1.0 - ADAM_B1) * g
            v = ADAM_B2 * v_refs[k][...] + (1.0 - ADAM_B2) * (g * g)
            m_hat = m / (1.0 - ADAM_B1 ** ADAM_STEP)
            v_hat = v / (1.0 - ADAM_B2 ** ADAM_STEP)
            d_refs[k][...] = -ADAM_LR * (m_hat / (jnp.sqrt(v_hat) + ADAM_EPS) + ADAM_WD * w)
            nm_refs[k][...] = m
            nv_refs[k][...] = v

    sds = jax.ShapeDtypeStruct
    like = [sds(w.shape, f32) for w in ws]
    out_shape = [sds((GAIN_ROWS, LANES), f32), sds((1, LANES), f32)] + like[:ng] + like * 3
    full = lambda a: pl.BlockSpec(a.shape, lambda i: (0,) * len(a.shape))
    operands = [gain_parts, *ws, *gs, *ms, *vs]
    outs = pl.pallas_call(
        body, name="adamw", grid=(1,), out_shape=out_shape, in_specs=[full(a) for a in operands],
        out_specs=[full(a) for a in out_shape], compiler_params=_cparams(("arbitrary",)),
    )(*operands)
    loss = outs[1]
    gg = outs[2:2 + ng]
    deltas, new_m, new_v = (outs[2 + ng + k * n:2 + ng + (k + 1) * n] for k in range(3))
    return loss, gg, deltas, new_m, new_v


GAIN_ROWS = 32
GAIN_SLOTS = [(0, 1024), (8, 256), (10, 128), (11, 256), (13, 256), (15, 512), (19, 512), (23, 1024)]

SH_IN = IN_TOTAL // N_DEV
SLAB1_COLS = 384
UQ_ROWS = Q_LORA * (N_HEADS * QK_DIM // N_DEV) // SLAB1_COLS
UKV_ROWS = 48
SLAB1_ROWS = D_MODEL + UQ_ROWS + UKV_ROWS
SH_ROWS = D_MODEL // N_DEV
PL_ROWS = PLE_DIM * (D_MODEL // N_DEV) // D_MODEL
CONV_ROWS = 16
SLAB2_ROWS = 2 * SH_ROWS + PL_ROWS + CONV_ROWS
IN_SEGMENTS = [(0, 448, 0), (448, 960, 512), (1472, 2496, 1024), (960, 1472, 2048), (2496, 3008, 2560)]


def _flat_rows(a, rows, cols):
    lead = a.shape[:-2]
    flat = a.reshape(lead + (-1,))
    pad = [(0, 0)] * len(lead) + [(0, rows * cols - flat.shape[-1])]
    return jnp.pad(flat, pad).reshape(lead + (rows, cols))


def _slab1(in_sh, uq_sh, ukv_sh):
    pad = [(0, 0)] * (in_sh.ndim - 1) + [(0, SLAB1_COLS - in_sh.shape[-1])]
    return jnp.concatenate([jnp.pad(in_sh, pad), _flat_rows(uq_sh, UQ_ROWS, SLAB1_COLS), _flat_rows(ukv_sh, UKV_ROWS, SLAB1_COLS)],
                           axis=-2)


def _slab2(o_sh, plg_sh, pl_sh, conv_sh):
    return jnp.concatenate([o_sh, plg_sh, _flat_rows(pl_sh, PL_ROWS, D_MODEL), _flat_rows(conv_sh, CONV_ROWS, D_MODEL)], axis=-2)


def _in_plan():
    src = [None] * PROJ_PAD
    for a, b, start in IN_SEGMENTS:
        for o in range(a, b):
            src[start + o - a] = (o // SH_IN, o % SH_IN)
    plan = []
    for t in range(PROJ_PAD // LANES):
        runs, j = [], 0
        key_of = lambda jj: None if src[t * LANES + jj] is None else (
            src[t * LANES + jj][0], src[t * LANES + jj][1] // LANES, src[t * LANES + jj][1] % LANES - jj)
        while j < LANES:
            key = key_of(j)
            lo = j
            while j < LANES and key_of(j) == key:
                j += 1
            if key is not None:
                runs.append(key + (lo, j))
        plan.append(runs)
    return plan


def _assemble_w_in(g1):
    plan = _in_plan()
    rows = 256

    def body(g_ref, out_ref):
        src_lane = lax.broadcasted_iota(jnp.int32, (LANES, LANES), 0)
        out_lane = lax.broadcasted_iota(jnp.int32, (LANES, LANES), 1)
        for t, runs in enumerate(plan):
            acc = jnp.zeros((rows, LANES), f32)
            for d, tile, shift, lo, hi in runs:
                pick = ((src_lane == out_lane + shift) & (out_lane >= lo) & (out_lane < hi)).astype(bf16)
                acc = acc + _nn(g_ref[d, :, tile * LANES:(tile + 1) * LANES], pick)
            out_ref[:, t * LANES:(t + 1) * LANES] = acc.astype(out_ref.dtype)

    return pl.pallas_call(
        body, name="w_in_columns", grid=(D_MODEL // rows,), out_shape=jax.ShapeDtypeStruct((D_MODEL, PROJ_PAD), MM),
        in_specs=[pl.BlockSpec((N_DEV, rows, SLAB1_COLS), lambda i: (0, i, 0))],
        out_specs=pl.BlockSpec((rows, PROJ_PAD), lambda i: (i, 0)),
        compiler_params=_cparams(("parallel",)))(g1)


def _by_device(a, width):
    return a.reshape(a.shape[0], N_DEV, width).transpose(1, 0, 2)


def _from_devices(a):
    return a.transpose(1, 0, 2).reshape(a.shape[1], -1)


def kernel(x, p, positions, g_in, w_in, g_cq, w_uq, g_ckv, w_ukv, g_q, g_k, conv_w, g_oa, g_oc, w_o, w_pl, w_plg, g_pl, loss_target, m_g_in, m_w_in, m_g_cq, m_w_uq, m_g_ckv, m_w_ukv, m_g_q, m_g_k, m_conv_w, m_g_oa, m_g_oc, m_w_o, m_w_pl, m_w_plg, m_g_pl, v_g_in, v_w_in, v_g_cq, v_w_uq, v_g_ckv, v_w_ukv, v_g_q, v_g_k, v_conv_w, v_g_oa, v_g_oc, v_w_o, v_w_pl, v_w_plg, v_g_pl):
    S = x.shape[1]
    nd = N_DEV
    xs, ps, tgt = x[0], p[0, 0], loss_target[0]

    slab1 = _slab1(w_in[0], w_uq[0], w_ukv[0]).astype(bf16)
    inv_freq = 1.0 / (ROPE_THETA ** (jnp.arange(0, ROPE_DIM, 2, dtype=f32) / ROPE_DIM))
    inv_row = jnp.concatenate([inv_freq, inv_freq, jnp.zeros((LANES - ROPE_DIM,), f32)]).reshape(1, LANES)
    posf = jnp.broadcast_to(positions[0].astype(f32)[:, None], (S, LANES))
    tabs, g1 = _rope_tables(posf, inv_row, slab1)
    g1 = g1.reshape(nd, SLAB1_ROWS, SLAB1_COLS)
    conv_bits = lax.bitcast_convert_type(conv_w[0], bf16).reshape(CONV_K, -1)
    slab2 = _slab2(w_o[0].astype(bf16), w_plg[0].astype(bf16), w_pl[0].astype(bf16), conv_bits)

    w_in_p = _assemble_w_in(g1)
    wuq = _from_devices(g1[:, D_MODEL:D_MODEL + UQ_ROWS].reshape(nd, Q_LORA, -1)).reshape(Q_LORA, N_HEADS, QK_DIM)
    w_uq_p = jnp.pad(wuq, ((0, 0), (0, 0), (0, HEAD_PAD - QK_DIM))).reshape(Q_LORA, QK_PAD)
    ukv_flat = g1[:, D_MODEL + UQ_ROWS:].reshape(nd, -1)[:, :KV_LORA * V_DIM]
    wukv = _from_devices(ukv_flat.reshape(nd, KV_LORA, V_DIM)).reshape(KV_LORA, N_HEADS, 2, NOPE_DIM)
    w_ukv_p = wukv.transpose(0, 2, 1, 3).reshape(KV_LORA, 2 * ATTN_WIDTH)
    w_in_p, w_uq_p, w_ukv_p = (w.astype(MM) for w in (w_in_p, w_uq_p, w_ukv_p))

    g_q_p = jnp.pad(g_q, ((0, 0), (0, HEAD_PAD - QK_DIM)))
    g_k_p = jnp.pad(g_k, ((0, 0), (0, HEAD_PAD - QK_DIM)))

    proj, q, k, v = _fwd_proj(xs, tabs, g_in, w_in_p, g_cq, w_uq_p, g_ckv, w_ukv_p, g_q_p * SCALE_LOG2E, g_k_p)
    o, lse, g2 = _flash_fwd(q, k, v, slab2)
    w_o_f = g2[:, 0:SH_ROWS].reshape(D_MODEL, D_MODEL).astype(MM)
    w_plg_f = g2[:, SH_ROWS:2 * SH_ROWS].reshape(D_MODEL, D_MODEL).astype(MM)
    w_pl_f = _from_devices(g2[:, 2 * SH_ROWS:2 * SH_ROWS + PL_ROWS].reshape(nd, PLE_DIM, -1)).astype(MM)
    cv_bits = g2[:, 2 * SH_ROWS + PL_ROWS:].reshape(nd, -1)[:, :CONV_K * (CONV_WIDTH // nd) * 2]
    conv_full = _from_devices(lax.bitcast_convert_type(cv_bits.reshape(nd, CONV_K, CONV_WIDTH // nd, 2), f32))
    conv_w8 = jnp.pad(conv_full, ((0, 8 - CONV_K), (0, 0)))
    (d_x2, d_o, delta, d_za, d_cbzc, d_u, dw_o, dw_plg, dw_pl, dcw, dg_oa, dg_oc, dg_pl, loss_part) = _mid(
        o, proj, xs, ps, tgt, w_o_f, w_plg_f, w_pl_f, conv_w8, g_oa, g_oc, g_pl)

    slabs_a = _slab2(dw_o.reshape(nd, SH_ROWS, D_MODEL), dw_plg.reshape(nd, SH_ROWS, D_MODEL), _by_device(dw_pl, D_MODEL // nd),
                     _by_device(dcw[:CONV_K], CONV_WIDTH // nd)).astype(RS_DTYPE)
    d_q, d_k, d_v, got_a = _flash_bwd(q, k, v, d_o, lse, delta, slabs_a)
    shard_a = _sum_slabs(got_a, "rs_sum8")
    (grad_x, dw_in_p, dw_uq_p, dw_ukv_p, dg_in, dg_cq, dg_ckv, dg_q_p, dg_k_p) = _bwd_proj(
        xs, proj, d_q, d_k, d_v, d_za, d_cbzc, d_u, d_x2, tabs, g_in, w_in_p, g_cq, w_uq_p, g_ckv, w_ukv_p, g_q_p, g_k_p, conv_w8)

    def dw_in_cols(a, b):
        return [dw_in_p[:, start + max(a, s) - s:start + min(b, e) - s] for s, e, start in sorted(IN_SEGMENTS) if max(a, s) < min(b, e)]

    no_cols = jnp.zeros((D_MODEL, SLAB1_COLS - SH_IN), f32)
    in_shards = jnp.stack([jnp.concatenate(dw_in_cols(SH_IN * d, SH_IN * (d + 1)) + [no_cols], axis=1) for d in range(nd)])
    dw_uq = dw_uq_p.reshape(Q_LORA, N_HEADS, HEAD_PAD)[:, :, :QK_DIM].reshape(Q_LORA, N_HEADS * QK_DIM)
    dw_ukv = dw_ukv_p.reshape(KV_LORA, 2, N_HEADS, NOPE_DIM).transpose(0, 2, 1, 3).reshape(KV_LORA, 2 * ATTN_WIDTH)
    slabs_b = _slab1(in_shards, _by_device(dw_uq, N_HEADS * QK_DIM // nd), _by_device(dw_ukv, 2 * ATTN_WIDTH // nd))
    slabs_b = slabs_b.reshape(4, 2, SLAB1_ROWS, SLAB1_COLS)
    my_c = lax.axis_index("c").astype(jnp.int32).reshape(1)
    chip_part = _add_blocks(slabs_b, _sibling_exchange(slabs_b), my_c)
    shard_b = _sum_slabs(_chip_exchange(chip_part), "rs_sum")
    grads_w = [shard_b[:D_MODEL, :SH_IN], shard_b[D_MODEL:D_MODEL + UQ_ROWS].reshape(Q_LORA, -1),
               shard_b[D_MODEL + UQ_ROWS:].reshape(-1)[:KV_LORA * V_DIM].reshape(KV_LORA, V_DIM),
               shard_a[2 * SH_ROWS + PL_ROWS:].reshape(-1)[:CONV_K * CONV_WIDTH // nd].reshape(CONV_K, -1),
               shard_a[0:SH_ROWS], shard_a[2 * SH_ROWS:2 * SH_ROWS + PL_ROWS].reshape(PLE_DIM, -1), shard_a[SH_ROWS:2 * SH_ROWS]]

    gains_part = [dg_in, dg_cq, dg_ckv, dg_q_p, dg_k_p, dg_oa, dg_oc, dg_pl]
    gflat = jnp.concatenate([g.reshape(-1) for g in gains_part] + [jnp.zeros(((GAIN_ROWS - 1) * LANES - 3968,), f32),
                                                                    loss_part.reshape(-1)])
    gain_parts = _all_gather(gflat.reshape(GAIN_ROWS, LANES), "ag_gains", in_vmem=True).reshape(nd, GAIN_ROWS, LANES)

    gains = [g_in, g_cq, g_ckv, g_q_p, g_k_p, g_oa, g_oc, g_pl]
    padq = lambda a: jnp.pad(a, ((0, 0), (0, HEAD_PAD - QK_DIM)))
    m_gains = [m_g_in, m_g_cq, m_g_ckv, padq(m_g_q), padq(m_g_k), m_g_oa, m_g_oc, m_g_pl]
    v_gains = [v_g_in, v_g_cq, v_g_ckv, padq(v_g_q), padq(v_g_k), v_g_oa, v_g_oc, v_g_pl]
    tr = lambda a: jnp.swapaxes(a, 1, 2)
    turned = lambda l: [tr(l[0]), tr(l[1])] + list(l[2:])
    ws = gains + turned([w_in, w_uq, w_ukv, conv_w, w_o, w_pl, w_plg])
    ms = m_gains + turned([m_w_in, m_w_uq, m_w_ukv, m_conv_w, m_w_o, m_w_pl, m_w_plg])
    vs = v_gains + turned([v_w_in, v_w_uq, v_w_ukv, v_conv_w, v_w_o, v_w_pl, v_w_plg])
    loss_row, gg, deltas, new_m, new_v = _adamw_all(ws, turned([g[None] for g in grads_w]), ms, vs, gain_parts)
    grads_w = [g[None] for g in grads_w]
    ng = len(gains)
    deltas, new_m, new_v = (list(l[:ng]) + turned(l[ng:]) for l in (deltas, new_m, new_v))
    loss = loss_row[0, 0]

    def ordered(gl, wl):
        g_in_, g_cq_, g_ckv_, g_q_, g_k_, g_oa_, g_oc_, g_pl_ = gl
        g_q_, g_k_ = g_q_[:, :QK_DIM], g_k_[:, :QK_DIM]
        w_in_, w_uq_, w_ukv_, cw_, w_o_, w_pl_, w_plg_ = wl
        return [g_in_, w_in_, g_cq_, w_uq_, g_ckv_, w_ukv_, g_q_, g_k_, cw_, g_oa_, g_oc_, w_o_, w_pl_, w_plg_, g_pl_]

    ng = len(gains)
    outs = [loss, grad_x[None]]
    outs += ordered(gg, grads_w)
    for lst in (deltas, new_m, new_v):
        outs += ordered(lst[:ng], lst[ng:])
    return tuple(outs)
```

```python
import functools
import math

import jax
import jax.numpy as jnp
from jax import lax
from jax.experimental import pallas as pl
from jax.experimental.pallas import tpu as pltpu

f32 = jnp.float32
bf16 = jnp.bfloat16
MM = jnp.bfloat16
RS_DTYPE = jnp.bfloat16

D_MODEL = 1024
PLE_DIM = 256
N_HEADS = 4
NOPE_DIM = 128
ROPE_DIM = 64
V_DIM = 128
QK_DIM = NOPE_DIM + ROPE_DIM
Q_LORA = 256
KV_LORA = 128
ATTN_WIDTH = N_HEADS * V_DIM
CONV_WIDTH = D_MODEL - ATTN_WIDTH
CONV_K = 3
ROPE_THETA = 10000.0
RMS_EPS = 1e-6
NEG_INF = -1e30
IN_TOTAL = Q_LORA + KV_LORA + ROPE_DIM + ATTN_WIDTH + 4 * CONV_WIDTH
ADAM_LR = 0.001
ADAM_B1 = 0.9
ADAM_B2 = 0.999
ADAM_EPS = 1e-08
ADAM_WD = 0.01
ADAM_STEP = 10

N_DEV = 8
LANES = 128
HEAD_PAD = 256
QK_PAD = N_HEADS * HEAD_PAD
PROJ_PAD = 3072
TS = 512
TS_M = 256
TS_E = 256
TQ = 512
VMEM_LIMIT = 58 * 1024 * 1024
MESH = pl.DeviceIdType.MESH
QK_SCALE = 1.0 / math.sqrt(QK_DIM)
SCALE_LOG2E = QK_SCALE * math.log2(math.e)
DK_SCALE = QK_SCALE / SCALE_LOG2E


def _cparams(sem=None):
    return pltpu.CompilerParams(dimension_semantics=sem, vmem_limit_bytes=VMEM_LIMIT)


def _nt(a, b):
    return lax.dot_general(a, b, (((1,), (1,)), ((), ())), preferred_element_type=f32)


def _tn(a, b):
    return lax.dot_general(a, b, (((0,), (0,)), ((), ())), preferred_element_type=f32)


def _nn(a, b):
    return jnp.dot(a, b, preferred_element_type=f32)


def _rstd(x, n):
    return lax.rsqrt(jnp.sum(x * x, axis=-1, keepdims=True) * (1.0 / n) + RMS_EPS)


def _rms_bwd(dy, xhat, r, g, n):
    u = dy * g
    return r * (u - xhat * (jnp.sum(u * xhat, axis=-1, keepdims=True) * (1.0 / n)))


def _rope(t, c, s1, s2):
    return t * c + pltpu.roll(t, 32, 1) * s1 + pltpu.roll(t, 96, 1) * s2


def _rope_t(d, c, s1, s2):
    return d * c - pltpu.roll(d, 96, 1) * s2 - pltpu.roll(d, 32, 1) * s1


def _alternate(*stages):
    stages = list(stages)
    while stages:
        for st in list(stages):
            if next(st, "done") == "done":
                stages.remove(st)


def _const(shape):
    return pl.BlockSpec(shape, lambda *_: (0,) * len(shape), pipeline_mode=pl.Buffered(1))


def _two_level_gather(x_ref, out_ref, send_sems, recv_sems, local_sem):
    m_per = x_ref.shape[0]
    x, y, c = lax.axis_index("x"), lax.axis_index("y"), lax.axis_index("c")
    me, sibling = (x, y, c), (x, y, 1 - c)
    chips = [(1 - x, y), (x, 1 - y), (1 - x, 1 - y)]

    def rows(px, py, pc):
        return out_ref.at[pl.ds((4 * px + 2 * py + pc) * m_per, m_per), :]

    def copy(k, blk, to, src=None):
        return pltpu.make_async_remote_copy(
            src_ref=rows(*blk) if src is None else src, dst_ref=rows(*blk),
            send_sem=send_sems.at[k], recv_sem=recv_sems.at[k], device_id=to, device_id_type=MESH)

    mine = pltpu.make_async_copy(x_ref, rows(*me), local_sem)
    first = [copy(0, me, sibling, src=x_ref)] + [copy(1 + j, me, (*chip, c), src=x_ref) for j, chip in enumerate(chips)]
    passed = [copy(4 + j, (*chip, c), sibling) for j, chip in enumerate(chips)]

    def start():
        mine.start()
        for cp in first:
            cp.start()

    def finish():
        for j, chip in enumerate(chips):
            copy(1 + j, (*chip, c), me).wait_recv()
            passed[j].start()
        copy(0, sibling, me).wait_recv()
        for j, chip in enumerate(chips):
            copy(4 + j, (*chip, 1 - c), me).wait_recv()
        for cp in first + passed:
            cp.wait_send()
        mine.wait()

    return start, finish


GATHER_SEMS = [pltpu.SemaphoreType.DMA((7,)), pltpu.SemaphoreType.DMA((7,)), pltpu.SemaphoreType.DMA]


def _rope_tables(posf, inv_freq, block):
    S = posf.shape[0]
    t = min(S, TS)
    nt = S // t

    def body(pos_ref, f_ref, blk_ref, c_ref, s1_ref, s2_ref, all_ref, send_sems, recv_sems, local_sem):
        i = pl.program_id(0)
        start, finish = _two_level_gather(blk_ref, all_ref, send_sems, recv_sems, local_sem)

        @pl.when(i == 0)
        def _():
            start()

        ang = pos_ref[...] * f_ref[...]
        lane = lax.broadcasted_iota(jnp.int32, ang.shape, 1)
        cs, sn = jnp.cos(ang), jnp.sin(ang)
        c_ref[...] = jnp.where(lane < ROPE_DIM, cs, 0.0)
        s1_ref[...] = jnp.where((lane >= ROPE_DIM // 2) & (lane < ROPE_DIM), sn, 0.0)
        s2_ref[...] = jnp.where(lane < ROPE_DIM // 2, -sn, 0.0)

        @pl.when(i == nt - 1)
        def _():
            finish()

    spec = pl.BlockSpec((t, LANES), lambda i: (i, 0))
    hbm = pl.BlockSpec(memory_space=pl.ANY)
    outs = pl.pallas_call(
        body, name="rope_tables", grid=(nt,),
        out_shape=[jax.ShapeDtypeStruct((S, LANES), f32)] * 3 + [jax.ShapeDtypeStruct((N_DEV * block.shape[0], block.shape[1]), block.dtype)],
        in_specs=[spec, _const((1, LANES)), hbm], out_specs=[spec] * 3 + [hbm],
        scratch_shapes=list(GATHER_SEMS),
        compiler_params=_cparams(("arbitrary",)),
    )(posf, inv_freq, block)
    return outs[:3], outs[3]


def _fwd_proj(x, tabs, g_in, w_in_p, g_cq, w_uq_p, g_ckv, w_ukv_p, g_q_p, g_k_p):
    S = x.shape[0]
    T = min(TS, S)
    nt = S // T
    LAT = Q_LORA + KV_LORA + LANES
    PIECE = 512

    def body(x_ref, c_ref, s1_ref, s2_ref, g_in_ref, w_in_ref, g_cq_ref, w_uq_ref, g_ckv_ref, w_ukv_ref,
             g_q_ref, g_k_ref, proj_ref, q_ref, k_ref, v_ref, lat_a, lat_b):
        i = pl.program_id(0)

        @pl.when(i == 0)
        def _():
            lat_b[...] = jnp.zeros(lat_b.shape, f32)

        def stage1(lat):
            xv = x_ref[...]
            h = (xv * _rstd(xv, D_MODEL) * g_in_ref[...]).astype(MM)
            yield
            for cb in range(PROJ_PAD // PIECE):
                cols = slice(cb * PIECE, (cb + 1) * PIECE)
                proj_ref[:, cols] = _nn(h, w_in_ref[:, cols])
                yield
            lat[...] = proj_ref[:, 0:LAT]

        def stage2(lat):
            c, s1, s2 = c_ref[...], s1_ref[...], s2_ref[...]
            g_q, g_k = g_q_ref[...], g_k_ref[...]
            c_q = lat[:, 0:Q_LORA]
            n_cq = (c_q * _rstd(c_q, Q_LORA) * g_cq_ref[...]).astype(MM)
            q_raw = _nn(n_cq, w_uq_ref[...])
            for hd in range(N_HEADS):
                qh = q_raw[:, hd * HEAD_PAD:(hd + 1) * HEAD_PAD]
                qn = qh * _rstd(qh, QK_DIM) * g_q
                q_ref[:, hd * HEAD_PAD:hd * HEAD_PAD + LANES] = qn[:, :LANES].astype(MM)
                q_ref[:, hd * HEAD_PAD + LANES:(hd + 1) * HEAD_PAD] = _rope(qn[:, LANES:], c, s1, s2).astype(MM)
                yield

            c_kv = lat[:, Q_LORA:Q_LORA + KV_LORA]
            n_ckv = (c_kv * _rstd(c_kv, KV_LORA) * g_ckv_ref[...]).astype(MM)
            kv_raw = _nn(n_ckv, w_ukv_ref[...])
            kpe = lat[:, Q_LORA + KV_LORA:LAT]
            ss_pe = jnp.sum(kpe * kpe, axis=-1, keepdims=True)
            kr = _rope(kpe * g_k[:, LANES:], c, s1, s2)
            for hd in range(N_HEADS):
                kn = kv_raw[:, hd * LANES:(hd + 1) * LANES]
                rk = lax.rsqrt((jnp.sum(kn * kn, axis=-1, keepdims=True) + ss_pe) * (1.0 / QK_DIM) + RMS_EPS)
                k_ref[:, hd * HEAD_PAD:hd * HEAD_PAD + LANES] = (kn * rk * g_k[:, :LANES]).astype(MM)
                k_ref[:, hd * HEAD_PAD + LANES:(hd + 1) * HEAD_PAD] = (kr * rk).astype(MM)
                yield
            v_ref[...] = kv_raw[:, N_HEADS * NOPE_DIM:].astype(MM)

        @pl.when(i % 2 == 0)
        def _():
            _alternate(stage1(lat_a), stage2(lat_b))

        @pl.when(i % 2 == 1)
        def _():
            _alternate(stage1(lat_b), stage2(lat_a))

    cur = lambda w: pl.BlockSpec((T, w), lambda i: (jnp.minimum(i, nt - 1), 0))
    prv = lambda w: pl.BlockSpec((T, w), lambda i: (jnp.maximum(i - 1, 0), 0))
    return pl.pallas_call(
        body, name="fwd_proj", grid=(nt + 1,),
        out_shape=[jax.ShapeDtypeStruct((S, PROJ_PAD), f32), jax.ShapeDtypeStruct((S, QK_PAD), MM),
                   jax.ShapeDtypeStruct((S, QK_PAD), MM), jax.ShapeDtypeStruct((S, ATTN_WIDTH), MM)],
        in_specs=[cur(D_MODEL), prv(LANES), prv(LANES), prv(LANES), _const((1, D_MODEL)), _const((D_MODEL, PROJ_PAD)),
                  _const((1, Q_LORA)), _const((Q_LORA, QK_PAD)), _const((1, KV_LORA)), _const((KV_LORA, 2 * ATTN_WIDTH)),
                  _const((1, HEAD_PAD)), _const((1, HEAD_PAD))],
        out_specs=[cur(PROJ_PAD), prv(QK_PAD), prv(QK_PAD), prv(ATTN_WIDTH)],
        scratch_shapes=[pltpu.VMEM((T, LAT), f32), pltpu.VMEM((T, LAT), f32)],
        compiler_params=_cparams(("arbitrary",)),
    )(x, *tabs, g_in, w_in_p, g_cq, w_uq_p, g_ckv, w_ukv_p, g_q_p, g_k_p)


def _exchange(src_of, dst_ref, send_sems, recv_sems, local_sem):
    x, y, c = lax.axis_index("x"), lax.axis_index("y"), lax.axis_index("c")
    me = 4 * x + 2 * y + c
    remote = []
    for k in range(1, N_DEV):
        px = 1 - x if (k >> 2) & 1 else x
        py = 1 - y if (k >> 1) & 1 else y
        pc = 1 - c if k & 1 else c
        remote.append(pltpu.make_async_remote_copy(
            src_ref=src_of(4 * px + 2 * py + pc), dst_ref=dst_ref.at[me], send_sem=send_sems.at[k - 1],
            recv_sem=recv_sems.at[k - 1], device_id=(px, py, pc), device_id_type=MESH))
    return remote, pltpu.make_async_copy(src_of(me), dst_ref.at[me], local_sem)


def _exchange_start(copies):
    remote, local = copies
    local.start()
    for cp in remote:
        cp.start()


def _exchange_wait(copies):
    remote, local = copies
    for cp in remote:
        cp.wait_recv()
    for cp in remote:
        cp.wait_send()
    local.wait()


EXCHANGE_SEMS = [pltpu.SemaphoreType.DMA((N_DEV - 1,)), pltpu.SemaphoreType.DMA((N_DEV - 1,)), pltpu.SemaphoreType.DMA]


def _flash_fwd(q, k, v, block):
    S = q.shape[0]
    T = min(TQ, S)
    nq = S // T

    def body(q_ref, k_ref, v_ref, blk_ref, o_ref, lse_ref, all_ref, s_a, s_b, m_sc, l_sc, acc_sc, send_sems, recv_sems, local_sem):
        n = pl.program_id(1)
        gather = _exchange(lambda d: blk_ref, all_ref, send_sems, recv_sems, local_sem)

        @pl.when((pl.program_id(0) == 0) & (n == 0))
        def _():
            _exchange_start(gather)

        m_sc[...] = jnp.full(m_sc.shape, NEG_INF, f32)
        l_sc[...] = jnp.zeros(l_sc.shape, f32)
        acc_sc[...] = jnp.zeros(acc_sc.shape, f32)
        qv = q_ref[...]

        def rows(j):
            return pl.ds(pl.multiple_of(j * T, T), T)

        def scores(j, dst):
            dst[...] = _nt(k_ref[rows(j), :], qv)

        def update(src, j, masked):
            s = src[...]
            if masked:
                k_i = lax.broadcasted_iota(jnp.int32, (T, T), 0)
                q_i = lax.broadcasted_iota(jnp.int32, (T, T), 1)
                s = jnp.where(k_i <= q_i, s, NEG_INF)
            m_prev = m_sc[...]
            m_new = jnp.maximum(m_prev, jnp.max(s, axis=0, keepdims=True))
            alpha = jnp.exp2(m_prev - m_new)
            p = jnp.exp2(s - m_new[0:1, :])
            l_sc[...] = alpha * l_sc[...] + jnp.sum(p, axis=0, keepdims=True)
            acc_sc[...] = alpha[0:1, :] * acc_sc[...] + _tn(v_ref[rows(j), :], p.astype(MM))
            m_sc[...] = m_new

        scores(0, s_a)

        def pair(t, carry):
            j = 2 * t
            scores(j + 1, s_b)
            update(s_a, j, False)
            scores(j + 2, s_a)
            update(s_b, j + 1, False)
            return carry

        lax.fori_loop(0, n // 2, pair, 0)

        @pl.when(n % 2 == 0)
        def _():
            update(s_a, n, True)

        @pl.when(n % 2 == 1)
        def _():
            scores(n, s_b)
            update(s_a, n - 1, False)
            update(s_b, n, True)

        o_ref[...] = (acc_sc[...] / l_sc[0:1, :]).T
        lse_ref[...] = m_sc[...] + jnp.log2(l_sc[...])

        @pl.when((pl.program_id(0) == N_HEADS - 1) & (n == nq - 1))
        def _():
            _exchange_wait(gather)

    return pl.pallas_call(
        body, name="flash_fwd", grid=(N_HEADS, nq),
        out_shape=[jax.ShapeDtypeStruct((S, ATTN_WIDTH), f32), jax.ShapeDtypeStruct((N_HEADS * 8, S), f32),
                   jax.ShapeDtypeStruct((N_DEV,) + block.shape, block.dtype)],
        in_specs=[pl.BlockSpec((T, HEAD_PAD), lambda h, i: (i, h)),
                  pl.BlockSpec((S, HEAD_PAD), lambda h, i: (0, h)),
                  pl.BlockSpec((S, V_DIM), lambda h, i: (0, h)),
                  pl.BlockSpec(memory_space=pl.ANY)],
        out_specs=[pl.BlockSpec((T, V_DIM), lambda h, i: (i, h)), pl.BlockSpec((8, T), lambda h, i: (h, i)),
                   pl.BlockSpec(memory_space=pl.ANY)],
        scratch_shapes=[pltpu.VMEM((T, T), f32), pltpu.VMEM((T, T), f32), pltpu.VMEM((8, T), f32), pltpu.VMEM((8, T), f32),
                        pltpu.VMEM((V_DIM, T), f32)] + EXCHANGE_SEMS,
        compiler_params=_cparams(("arbitrary", "arbitrary")),
    )(q, k, v, block)


def _shift_down(m, prev8, n):
    T = m.shape[0]
    rows = lax.broadcasted_iota(jnp.int32, m.shape, 0)
    head = jnp.tile(pltpu.roll(prev8, n, 0), (T // 8, 1))
    return jnp.where(rows >= n, pltpu.roll(m, n, 0), head)


def _shift_up(m, next8, n):
    T = m.shape[0]
    rows = lax.broadcasted_iota(jnp.int32, m.shape, 0)
    tail = jnp.tile(pltpu.roll(next8, 8 - n, 0), (T // 8, 1))
    return jnp.where(rows < T - n, pltpu.roll(m, T - n, 0), tail)


def _mid(o, proj, x, p, target, w_o, w_plg, w_pl, conv_w8, g_oa, g_oc, g_pl):
    S = x.shape[0]
    T = min(TS_M, S)
    nt = S // T

    def body(o_ref, za_ref, ccx_ref, cbzc_ref, prev_ref, x_ref, p_ref, t_ref, wo_ref, wplg_ref, wpl_ref, cw_ref,
             goa_ref, goc_ref, gpl_ref,
             dx2_ref, do_ref, delta_ref, dza_ref, dcbzc_ref, du_ref,
             dwo_ref, dwplg_ref, dwpl_ref, dcw_ref, dgoa_ref, dgoc_ref, dgpl_ref, loss_ref):
        i = pl.program_id(0)

        @pl.when(i == 0)
        def _():
            for r in (dwo_ref, dwplg_ref, dwpl_ref, dcw_ref, dgoa_ref, dgoc_ref, dgpl_ref, loss_ref):
                r[...] = jnp.zeros(r.shape, f32)

        o_v = o_ref[...]
        z_a = za_ref[...]
        cc, cx = ccx_ref[:, :CONV_WIDTH], ccx_ref[:, CONV_WIDTH:]
        cb, z_c = cbzc_ref[:, :CONV_WIDTH], cbzc_ref[:, CONV_WIDTH:]
        g_oa, g_oc, g_pl = goa_ref[...], goc_ref[...], gpl_ref[...]
        w0, w1, w2 = cw_ref[0:1, :], cw_ref[1:2, :], cw_ref[2:3, :]
        p_b = p_ref[...].astype(MM)
        pw = _nn(p_b, wpl_ref[...])

        sa = jax.nn.sigmoid(z_a)
        silu_a = z_a * sa
        ga = o_v * silu_a
        ra = _rstd(ga, ATTN_WIDTH)
        gha = ga * ra
        ya = (gha * g_oa).astype(MM)
        x2 = x_ref[...] + _nn(ya, wo_ref[:ATTN_WIDTH, :])
        m0 = cc * cx
        prev = prev_ref[:, :CONV_WIDTH] * prev_ref[:, CONV_WIDTH:] * (i > 0).astype(f32)
        m1 = _shift_down(m0, prev, 1)
        m2 = _shift_down(m0, prev, 2)
        u = w0 * m2 + w1 * m1 + w2 * m0
        sc = jax.nn.sigmoid(z_c)
        silu_c = z_c * sc
        gc = cb * u * silu_c
        rc = _rstd(gc, CONV_WIDTH)
        ghc = gc * rc
        yc = (ghc * g_oc).astype(MM)
        ycat = jnp.concatenate([ya, yc], axis=-1)
        x2 = x2 + _nn(yc, wo_ref[ATTN_WIDTH:, :])
        r2 = _rstd(x2, D_MODEL)
        xh2 = x2 * r2
        n2 = (xh2 * g_pl).astype(MM)
        gate = jax.nn.sigmoid(_nn(n2, wplg_ref[...]))
        err = x2 + gate * pw - t_ref[...]
        loss_ref[...] += jnp.sum(jnp.sum(err * err, axis=-1, keepdims=True), axis=0, keepdims=True) * (0.5 / D_MODEL)
        d_out = err * (1.0 / D_MODEL)
        d_glog = (d_out * pw * gate * (1.0 - gate)).astype(MM)
        d_n2 = _nt(d_glog, wplg_ref[...])
        dwpl_ref[...] += _tn(p_b, (d_out * gate).astype(MM))
        dwplg_ref[...] += _tn(n2, d_glog)
        dgpl_ref[...] += jnp.sum(d_n2 * xh2, axis=0, keepdims=True)
        d_x2 = d_out + _rms_bwd(d_n2, xh2, r2, g_pl, D_MODEL)
        dx2_ref[...] = d_x2
        d_x2b = d_x2.astype(MM)
        d_ycat = _nt(d_x2b, wo_ref[...])
        dwo_ref[...] += _tn(ycat, d_x2b)
        d_ya, d_yc = d_ycat[:, :ATTN_WIDTH], d_ycat[:, ATTN_WIDTH:]
        dgoa_ref[...] += jnp.sum(d_ya * gha, axis=0, keepdims=True)
        dgoc_ref[...] += jnp.sum(d_yc * ghc, axis=0, keepdims=True)
        d_ga = _rms_bwd(d_ya, gha, ra, g_oa, ATTN_WIDTH)
        d_gc = _rms_bwd(d_yc, ghc, rc, g_oc, CONV_WIDTH)
        d_o = d_ga * silu_a
        do_ref[...] = d_o.astype(MM)
        dza_ref[...] = (d_ga * o_v * (sa * (1.0 + z_a * (1.0 - sa)))).astype(MM)
        for hd in range(N_HEADS):
            cols = slice(hd * V_DIM, (hd + 1) * V_DIM)
            dl = jnp.sum(d_o[:, cols] * o_v[:, cols], axis=-1, keepdims=True)
            delta_ref[hd * 8:(hd + 1) * 8, :] = jnp.broadcast_to(dl, (T, LANES)).T[0:8, :]
        d_u = d_gc * cb * silu_c
        du_ref[...] = d_u
        dcbzc_ref[:, :CONV_WIDTH] = (d_gc * u * silu_c).astype(MM)
        dcbzc_ref[:, CONV_WIDTH:] = (d_gc * cb * u * (sc * (1.0 + z_c * (1.0 - sc)))).astype(MM)
        dcw_ref[0:1, :] += jnp.sum(d_u * m2, axis=0, keepdims=True)
        dcw_ref[1:2, :] += jnp.sum(d_u * m1, axis=0, keepdims=True)
        dcw_ref[2:3, :] += jnp.sum(d_u * m0, axis=0, keepdims=True)

    row = lambda w, blk=0: pl.BlockSpec((T, w), lambda i: (i, blk))
    acc = lambda shape: pl.BlockSpec(shape, lambda i: (0, 0))
    tb = T // 8
    sds = jax.ShapeDtypeStruct
    return pl.pallas_call(
        body, name="mid", grid=(nt,),
        out_shape=[sds((S, D_MODEL), f32), sds((S, ATTN_WIDTH), MM), sds((N_HEADS * 8, S), f32), sds((S, ATTN_WIDTH), MM),
                   sds((S, 2 * CONV_WIDTH), MM), sds((S, CONV_WIDTH), f32),
                   sds((D_MODEL, D_MODEL), f32), sds((D_MODEL, D_MODEL), f32), sds((PLE_DIM, D_MODEL), f32), sds((8, CONV_WIDTH), f32),
                   sds((1, ATTN_WIDTH), f32), sds((1, CONV_WIDTH), f32), sds((1, D_MODEL), f32), sds((1, LANES), f32)],
        in_specs=[row(ATTN_WIDTH), row(ATTN_WIDTH, 1), row(2 * CONV_WIDTH, 1), row(2 * CONV_WIDTH, 2),
                  pl.BlockSpec((8, 2 * CONV_WIDTH), lambda i: (jnp.maximum(i * tb - 1, 0), 1)),
                  row(D_MODEL), row(PLE_DIM), row(D_MODEL),
                  _const((D_MODEL, D_MODEL)), _const((D_MODEL, D_MODEL)), _const((PLE_DIM, D_MODEL)), _const((8, CONV_WIDTH)),
                  _const((1, ATTN_WIDTH)), _const((1, CONV_WIDTH)), _const((1, D_MODEL))],
        out_specs=[row(D_MODEL), row(ATTN_WIDTH), pl.BlockSpec((N_HEADS * 8, T), lambda i: (0, i)), row(ATTN_WIDTH),
                   row(2 * CONV_WIDTH), row(CONV_WIDTH),
                   acc((D_MODEL, D_MODEL)), acc((D_MODEL, D_MODEL)), acc((PLE_DIM, D_MODEL)), acc((8, CONV_WIDTH)),
                   acc((1, ATTN_WIDTH)), acc((1, CONV_WIDTH)), acc((1, D_MODEL)), acc((1, LANES))],
        compiler_params=_cparams(("arbitrary",)),
    )(o, proj, proj, proj, proj, x, p, target, w_o, w_plg, w_pl, conv_w8, g_oa, g_oc, g_pl)


def _flash_bwd(q, k, v, d_o, lse2, delta, by_target):
    S = q.shape[0]
    T = min(TQ, S)
    nq = S // T

    def body(k_ref, v_ref, q_ref, do_ref, lse_ref, delta_ref, out_ref, dq_ref, dk_ref, dv_ref, in_ref, s_a, dp_a, s_b, dp_b,
             send_sems, recv_sems, local_sem):
        step = pl.program_id(1)
        j = nq - 1 - step
        scatter = _exchange(lambda d: out_ref.at[d], in_ref, send_sems, recv_sems, local_sem)

        @pl.when((pl.program_id(0) == 0) & (step == 0))
        def _():
            _exchange_start(scatter)

        @pl.when(step == 0)
        def _():
            dq_ref[...] = jnp.zeros(dq_ref.shape, f32)

        dk_ref[...] = jnp.zeros(dk_ref.shape, f32)
        dv_ref[...] = jnp.zeros(dv_ref.shape, f32)
        kv, vv = k_ref[...], v_ref[...]

        def rows_of(i):
            return pl.ds(pl.multiple_of(i * T, T), T)

        def scores(i, s_dst, dp_dst):
            rows = rows_of(i)
            s_dst[...] = _nt(kv, q_ref[rows, :])
            dp_dst[...] = _nt(vv, do_ref[rows, :])

        def grads(i, s_src, dp_src, masked):
            rows = rows_of(i)
            s = s_src[...]
            if masked:
                k_i = lax.broadcasted_iota(jnp.int32, (T, T), 0)
                q_i = lax.broadcasted_iota(jnp.int32, (T, T), 1)
                s = jnp.where(k_i <= q_i, s, NEG_INF)
            p = jnp.exp2(s - lse_ref[0:1, rows])
            ds = (p * (dp_src[...] - delta_ref[0:1, rows])).astype(MM)
            dv_ref[...] += _nn(p.astype(MM), do_ref[rows, :])
            dk_ref[...] += _nn(ds, q_ref[rows, :])
            dq_ref[rows, :] += _tn(ds, kv)

        last = nq - 1
        scores(j, s_a, dp_a)
        scores(jnp.minimum(j + 1, last), s_b, dp_b)
        grads(j, s_a, dp_a, True)
        rest = last - j

        def pair(t, carry):
            u = j + 1 + 2 * t
            scores(u + 1, s_a, dp_a)
            grads(u, s_b, dp_b, False)
            scores(jnp.minimum(u + 2, last), s_b, dp_b)
            grads(u + 1, s_a, dp_a, False)
            return carry

        lax.fori_loop(0, rest // 2, pair, 0)

        @pl.when(rest % 2 == 1)
        def _():
            grads(last, s_b, dp_b, False)

        @pl.when((pl.program_id(0) == N_HEADS - 1) & (step == nq - 1))
        def _():
            _exchange_wait(scatter)

    whole = lambda w: pl.BlockSpec((S, w), lambda h, j: (0, h))
    stat = pl.BlockSpec((8, S), lambda h, j: (h, 0))
    blk = lambda w: pl.BlockSpec((T, w), lambda h, step: (nq - 1 - step, h))
    return pl.pallas_call(
        body, name="flash_bwd", grid=(N_HEADS, nq),
        out_shape=[jax.ShapeDtypeStruct((S, QK_PAD), f32), jax.ShapeDtypeStruct((S, QK_PAD), f32),
                   jax.ShapeDtypeStruct((S, ATTN_WIDTH), f32), jax.ShapeDtypeStruct(by_target.shape, by_target.dtype)],
        in_specs=[blk(HEAD_PAD), blk(V_DIM), whole(HEAD_PAD), whole(V_DIM), stat, stat, pl.BlockSpec(memory_space=pl.ANY)],
        out_specs=[whole(HEAD_PAD), blk(HEAD_PAD), blk(V_DIM), pl.BlockSpec(memory_space=pl.ANY)],
        scratch_shapes=[pltpu.VMEM((T, T), f32)] * 4 + EXCHANGE_SEMS,
        compiler_params=_cparams(("arbitrary", "arbitrary")),
    )(k, v, q, d_o, lse2, delta, by_target)


def _bwd_proj(x, proj, d_q, d_k, d_v, d_za, d_cbzc, d_u, d_x2, tabs, g_in, w_in_p, g_cq, w_uq_p, g_ckv, w_ukv_p,
              g_q_p, g_k_p, conv_w8):
    S = x.shape[0]
    T = min(TS_E, S)
    nt = S // T
    PIECE = 512

    def body(x_ref, cqkv_ref, ccx_ref, dq_ref, dk_ref, dv_ref, dza_ref, dcbzc_ref, du_ref, dun_ref, dx2_ref,
             c_ref, s1_ref, s2_ref, g_in_ref, w_in_ref, g_cq_ref, w_uq_ref, g_ckv_ref, w_ukv_ref, g_q_ref, g_k_ref, cw_ref,
             gx_ref, dwin_hbm, dwuq_ref, dwukv_ref, dgin_ref, dgcq_ref, dgckv_ref, dgq_ref, dgk_ref,
             dproj_a, dproj_b, dqraw_sc, dkvraw_sc, dwin_sc, sem):
        i = pl.program_id(0)

        @pl.when(i == 0)
        def _():
            dwin_sc[...] = jnp.zeros(dwin_sc.shape, f32)
            dproj_b[...] = jnp.zeros(dproj_b.shape, MM)
            for r in (dwuq_ref, dwukv_ref, dgin_ref, dgcq_ref, dgckv_ref, dgq_ref, dgk_ref):
                r[...] = jnp.zeros(r.shape, f32)

        def stage1(dproj_sc):
            live = (i < nt).astype(f32)
            c, s1, s2 = c_ref[...], s1_ref[...], s2_ref[...]
            g_q, g_k = g_q_ref[...], g_k_ref[...]
            g_cq, g_ckv = g_cq_ref[...], g_ckv_ref[...]

            c_q = cqkv_ref[:, 0:Q_LORA]
            r_cq = _rstd(c_q, Q_LORA)
            cqh = c_q * r_cq
            n_cq = (cqh * g_cq).astype(MM)
            q_raw = _nn(n_cq, w_uq_ref[...])
            dgq = jnp.zeros((1, HEAD_PAD), f32)
            for hd in range(N_HEADS):
                cols = slice(hd * HEAD_PAD, (hd + 1) * HEAD_PAD)
                qh = q_raw[:, cols]
                rq = _rstd(qh, QK_DIM)
                qhh = qh * rq
                dqh = dq_ref[:, cols] * QK_SCALE
                d_qn = jnp.concatenate([dqh[:, :LANES], _rope_t(dqh[:, LANES:], c, s1, s2)], axis=-1)
                dgq += jnp.sum(d_qn * qhh, axis=0, keepdims=True)
                dqraw_sc[:, cols] = _rms_bwd(d_qn, qhh, rq, g_q, QK_DIM).astype(MM)
                yield
            dgq_ref[...] += dgq * live
            d_qraw = dqraw_sc[...]
            dwuq_ref[...] += _tn((cqh * (g_cq * live)).astype(MM), d_qraw)
            d_ncq = _nt(d_qraw, w_uq_ref[...])
            dgcq_ref[...] += jnp.sum(d_ncq * cqh, axis=0, keepdims=True) * live
            dproj_sc[:, 0:Q_LORA] = _rms_bwd(d_ncq, cqh, r_cq, g_cq, Q_LORA).astype(MM)
            yield

            c_kv = cqkv_ref[:, Q_LORA:Q_LORA + KV_LORA]
            r_ckv = _rstd(c_kv, KV_LORA)
            ckvh = c_kv * r_ckv
            n_ckv = (ckvh * g_ckv).astype(MM)
            k_nope = _nn(n_ckv, w_ukv_ref[:, :N_HEADS * NOPE_DIM])
            kpe = cqkv_ref[:, Q_LORA + KV_LORA:Q_LORA + KV_LORA + LANES]
            ss_pe = jnp.sum(kpe * kpe, axis=-1, keepdims=True)
            g_kn, g_kp = g_k[:, :LANES], g_k[:, LANES:]
            d_kpe = jnp.zeros((T, LANES), f32)
            dgk_n = jnp.zeros((1, LANES), f32)
            dgk_p = jnp.zeros((1, LANES), f32)
            for hd in range(N_HEADS):
                kn = k_nope[:, hd * LANES:(hd + 1) * LANES]
                rk = lax.rsqrt((jnp.sum(kn * kn, axis=-1, keepdims=True) + ss_pe) * (1.0 / QK_DIM) + RMS_EPS)
                knh, kph = kn * rk, kpe * rk
                d_kn_n = dk_ref[:, hd * HEAD_PAD:hd * HEAD_PAD + LANES] * DK_SCALE
                d_kr = _rope_t(dk_ref[:, hd * HEAD_PAD + LANES:(hd + 1) * HEAD_PAD] * DK_SCALE, c, s1, s2)
                dgk_n += jnp.sum(d_kn_n * knh, axis=0, keepdims=True)
                dgk_p += jnp.sum(d_kr * kph, axis=0, keepdims=True)
                u_n, u_p = d_kn_n * g_kn, d_kr * g_kp
                mt = (jnp.sum(u_n * knh, axis=-1, keepdims=True) + jnp.sum(u_p * kph, axis=-1, keepdims=True)) * (1.0 / QK_DIM)
                dkvraw_sc[:, hd * LANES:(hd + 1) * LANES] = (rk * (u_n - knh * mt)).astype(MM)
                d_kpe += rk * (u_p - kph * mt)
                yield
            dgk_ref[:, :LANES] += dgk_n * live
            dgk_ref[:, LANES:] += dgk_p * live
            dkvraw_sc[:, N_HEADS * NOPE_DIM:] = dv_ref[...].astype(MM)
            d_kvraw = dkvraw_sc[...]
            dwukv_ref[...] += _tn((ckvh * (g_ckv * live)).astype(MM), d_kvraw)
            d_nckv = _nt(d_kvraw, w_ukv_ref[...])
            dgckv_ref[...] += jnp.sum(d_nckv * ckvh, axis=0, keepdims=True) * live
            dproj_sc[:, Q_LORA:Q_LORA + KV_LORA] = _rms_bwd(d_nckv, ckvh, r_ckv, g_ckv, KV_LORA).astype(MM)
            dproj_sc[:, Q_LORA + KV_LORA:Q_LORA + KV_LORA + LANES] = d_kpe.astype(MM)
            yield

            dproj_sc[:, 512:1024] = dza_ref[...]
            d_u = du_ref[...]
            nxt = dun_ref[...] * (i < nt - 1).astype(f32)
            d_m = cw_ref[2:3, :] * d_u + cw_ref[1:2, :] * _shift_up(d_u, nxt, 1) + cw_ref[0:1, :] * _shift_up(d_u, nxt, 2)
            dproj_sc[:, 1024:1536] = (d_m * ccx_ref[:, CONV_WIDTH:]).astype(MM)
            dproj_sc[:, 1536:2048] = (d_m * ccx_ref[:, :CONV_WIDTH]).astype(MM)
            dproj_sc[:, 2048:3072] = dcbzc_ref[...]

        def stage2(dproj_sc):
            g_in = g_in_ref[...]
            xv = x_ref[...]
            r_in = _rstd(xv, D_MODEL)
            xh = xv * r_in
            hb = (xh * g_in).astype(MM)
            yield
            d_h = jnp.zeros((T, D_MODEL), f32)
            for cb in range(PROJ_PAD // PIECE):
                cols = slice(cb * PIECE, (cb + 1) * PIECE)
                d_piece = dproj_sc[:, cols]
                dwin_sc[:, cols] += _tn(hb, d_piece)
                d_h = d_h + _nt(d_piece, w_in_ref[:, cols])
                yield
            dgin_ref[...] += jnp.sum(d_h * xh, axis=0, keepdims=True)
            gx_ref[...] = dx2_ref[...] + _rms_bwd(d_h, xh, r_in, g_in, D_MODEL)

        @pl.when(i % 2 == 0)
        def _():
            _alternate(stage2(dproj_b), stage1(dproj_a))

        @pl.when(i % 2 == 1)
        def _():
            _alternate(stage2(dproj_a), stage1(dproj_b))

        @pl.when(i == nt)
        def _():
            cp = pltpu.make_async_copy(dwin_sc, dwin_hbm, sem)
            cp.start()
            cp.wait()

    cur = lambda i: jnp.minimum(i, nt - 1)
    prv = lambda i: jnp.maximum(i - 1, 0)
    row = lambda w, blk=0: pl.BlockSpec((T, w), lambda i: (cur(i), blk))
    row2 = lambda w: pl.BlockSpec((T, w), lambda i: (prv(i), 0))
    acc = lambda shape: pl.BlockSpec(shape, lambda i: (0, 0))
    tb = T // 8
    sds = jax.ShapeDtypeStruct
    return pl.pallas_call(
        body, name="bwd_proj", grid=(nt + 1,),
        out_shape=[sds((S, D_MODEL), f32), sds((D_MODEL, PROJ_PAD), f32), sds((Q_LORA, QK_PAD), f32),
                   sds((KV_LORA, 2 * ATTN_WIDTH), f32), sds((1, D_MODEL), f32), sds((1, Q_LORA), f32), sds((1, KV_LORA), f32),
                   sds((1, HEAD_PAD), f32), sds((1, HEAD_PAD), f32)],
        in_specs=[row2(D_MODEL), row(512, 0), row(2 * CONV_WIDTH, 1), row(QK_PAD), row(QK_PAD), row(ATTN_WIDTH),
                  row(ATTN_WIDTH), row(2 * CONV_WIDTH), row(CONV_WIDTH),
                  pl.BlockSpec((8, CONV_WIDTH), lambda i: (jnp.minimum((cur(i) + 1) * tb, S // 8 - 1), 0)),
                  row2(D_MODEL), row(LANES), row(LANES), row(LANES),
                  _const((1, D_MODEL)), _const((D_MODEL, PROJ_PAD)), _const((1, Q_LORA)), _const((Q_LORA, QK_PAD)),
                  _const((1, KV_LORA)), _const((KV_LORA, 2 * ATTN_WIDTH)), _const((1, HEAD_PAD)), _const((1, HEAD_PAD)),
                  _const((8, CONV_WIDTH))],
        out_specs=[row2(D_MODEL), pl.BlockSpec(memory_space=pl.ANY), acc((Q_LORA, QK_PAD)), acc((KV_LORA, 2 * ATTN_WIDTH)),
                   acc((1, D_MODEL)), acc((1, Q_LORA)), acc((1, KV_LORA)), acc((1, HEAD_PAD)), acc((1, HEAD_PAD))],
        scratch_shapes=[pltpu.VMEM((T, PROJ_PAD), MM), pltpu.VMEM((T, PROJ_PAD), MM), pltpu.VMEM((T, QK_PAD), MM),
                        pltpu.VMEM((T, 2 * ATTN_WIDTH), MM), pltpu.VMEM((D_MODEL, PROJ_PAD), f32), pltpu.SemaphoreType.DMA],
        compiler_params=_cparams(("arbitrary",)),
    )(x, proj, proj, d_q, d_k, d_v, d_za, d_cbzc, d_u, d_u, d_x2, *tabs, g_in, w_in_p, g_cq, w_uq_p, g_ckv, w_ukv_p,
      g_q_p, g_k_p, conv_w8)


def _all_gather(block, name, in_vmem):
    m_per, n = block.shape

    def body(x_ref, out_ref, send_sems, recv_sems, local_sem):
        start, finish = _two_level_gather(x_ref, out_ref, send_sems, recv_sems, local_sem)
        start()
        finish()

    space = pltpu.VMEM if in_vmem else pl.ANY
    return pl.pallas_call(
        body, name=name, out_shape=jax.ShapeDtypeStruct((N_DEV * m_per, n), block.dtype),
        in_specs=[pl.BlockSpec(memory_space=space)], out_specs=pl.BlockSpec(memory_space=space),
        scratch_shapes=list(GATHER_SEMS),
    )(block)


def _sibling_exchange(pack):
    nchip, _, R, n = pack.shape

    def body(p_ref, got_ref, send_sems, recv_sems):
        x, y, c = lax.axis_index("x"), lax.axis_index("y"), lax.axis_index("c")
        copies = [pltpu.make_async_remote_copy(src_ref=p_ref.at[t, 1 - c], dst_ref=got_ref.at[t], send_sem=send_sems.at[t],
                                               recv_sem=recv_sems.at[t], device_id=(x, y, 1 - c), device_id_type=MESH)
                  for t in range(nchip)]
        for cp in copies:
            cp.start()
        for cp in copies:
            cp.wait()

    return pl.pallas_call(
        body, name="rs_sibling", out_shape=jax.ShapeDtypeStruct((nchip, R, n), pack.dtype),
        in_specs=[pl.BlockSpec(memory_space=pl.ANY)], out_specs=pl.BlockSpec(memory_space=pl.ANY),
        scratch_shapes=[pltpu.SemaphoreType.DMA((nchip,)), pltpu.SemaphoreType.DMA((nchip,))],
    )(pack)


def _chip_exchange(part):
    nchip, R, n = part.shape

    def body(p_ref, got_ref, send_sems, recv_sems, local_sem):
        x, y, c = lax.axis_index("x"), lax.axis_index("y"), lax.axis_index("c")
        my_chip = 2 * x + y
        chips = [(1 - x, y), (x, 1 - y), (1 - x, 1 - y)]
        mine = pltpu.make_async_copy(p_ref.at[my_chip], got_ref.at[my_chip], local_sem)
        mine.start()
        sends = []
        for k, (tx, ty) in enumerate(chips):
            cp = pltpu.make_async_remote_copy(src_ref=p_ref.at[2 * tx + ty], dst_ref=got_ref.at[my_chip],
                                              send_sem=send_sems.at[k], recv_sem=recv_sems.at[k],
                                              device_id=(tx, ty, c), device_id_type=MESH)
            cp.start()
            sends.append(cp)
        for k, (sx, sy) in enumerate(chips):
            pltpu.make_async_remote_copy(src_ref=p_ref.at[my_chip], dst_ref=got_ref.at[2 * sx + sy],
                                         send_sem=send_sems.at[k], recv_sem=recv_sems.at[k],
                                         device_id=(sx, sy, c), device_id_type=MESH).wait_recv()
        for cp in sends:
            cp.wait_send()
        mine.wait()

    return pl.pallas_call(
        body, name="rs_chips", out_shape=jax.ShapeDtypeStruct((nchip, R, n), part.dtype),
        in_specs=[pl.BlockSpec(memory_space=pl.ANY)], out_specs=pl.BlockSpec(memory_space=pl.ANY),
        scratch_shapes=[pltpu.SemaphoreType.DMA((3,)), pltpu.SemaphoreType.DMA((3,)), pltpu.SemaphoreType.DMA],
    )(part)


def _add_blocks(pack, from_sib, my_c):
    nb, _, R, n = pack.shape

    def body(c_ref, a_ref, b_ref, o_ref):
        o_ref[...] = (a_ref[0] + b_ref[...]).astype(o_ref.dtype)

    return pl.pallas_call(
        body, name="rs_add", out_shape=jax.ShapeDtypeStruct((nb, R, n), RS_DTYPE),
        grid_spec=pltpu.PrefetchScalarGridSpec(
            num_scalar_prefetch=1, grid=(nb,),
            in_specs=[pl.BlockSpec((1, 1, R, n), lambda i, c: (i, c[0], 0, 0)), pl.BlockSpec((1, R, n), lambda i, c: (i, 0, 0))],
            out_specs=pl.BlockSpec((1, R, n), lambda i, c: (i, 0, 0))),
        compiler_params=_cparams(("parallel",)))(my_c, pack, from_sib)


def _sum_slabs(got, name):
    nb, R, n = got.shape

    def body(g_ref, o_ref):
        tot = g_ref[0].astype(f32)
        for d in range(1, nb):
            tot = tot + g_ref[d].astype(f32)
        o_ref[...] = tot

    return pl.pallas_call(body, name=name, grid=(n // LANES,), out_shape=jax.ShapeDtypeStruct((R, n), f32),
                          in_specs=[pl.BlockSpec((nb, R, LANES), lambda j: (0, 0, j))], out_specs=pl.BlockSpec((R, LANES), lambda j: (0, j)),
                          compiler_params=_cparams(("parallel",)))(got)


def _adamw_all(ws, gs, ms, vs, gain_parts):
    n = len(ws)
    ng = len(GAIN_SLOTS)

    def body(*refs):
        gp_ref = refs[0]
        w_refs = refs[1:1 + n]
        g_refs = refs[1 + n:1 + 2 * n - ng]
        m_refs = refs[1 + 2 * n - ng:1 + 3 * n - ng]
        v_refs = refs[1 + 3 * n - ng:1 + 4 * n - ng]
        outs = refs[1 + 4 * n - ng:]
        gsum_ref, loss_ref = outs[0], outs[1]
        gg_refs = outs[2:2 + ng]
        d_refs, nm_refs, nv_refs = (outs[2 + ng + k * n:2 + ng + (k + 1) * n] for k in range(3))
        tot = gp_ref[0]
        for d in range(1, N_DEV):
            tot = tot + gp_ref[d]
        gsum_ref[...] = tot
        loss_ref[...] = gsum_ref[GAIN_ROWS - 1:GAIN_ROWS, :]
        for k in range(n):
            if k < ng:
                r0, width = GAIN_SLOTS[k]
                for r in range(width // LANES):
                    gg_refs[k][:, r * LANES:(r + 1) * LANES] = gsum_ref[r0 + r:r0 + r + 1, :]
                g = gg_refs[k][...]
            else:
                g = g_refs[k - ng][...]
            w = w_refs[k][...]
            m = ADAM_B1 * m_refs[k][...] + (1.0 - ADAM_B1) * g
            v = ADAM_B2 * v_refs[k][...] + (1.0 - ADAM_B2) * (g * g)
            m_hat = m / (1.0 - ADAM_B1 ** ADAM_STEP)
            v_hat = v / (1.0 - ADAM_B2 ** ADAM_STEP)
            d_refs[k][...] = -ADAM_LR * (m_hat / (jnp.sqrt(v_hat) + ADAM_EPS) + ADAM_WD * w)
            nm_refs[k][...] = m
            nv_refs[k][...] = v

    sds = jax.ShapeDtypeStruct
    like = [sds(w.shape, f32) for w in ws]
    out_shape = [sds((GAIN_ROWS, LANES), f32), sds((1, LANES), f32)] + like[:ng] + like * 3
    full = lambda a: pl.BlockSpec(a.shape, lambda i: (0,) * len(a.shape))
    operands = [gain_parts, *ws, *gs, *ms, *vs]
    outs = pl.pallas_call(
        body, name="adamw", grid=(1,), out_shape=out_shape, in_specs=[full(a) for a in operands],
        out_specs=[full(a) for a in out_shape], compiler_params=_cparams(("arbitrary",)),
    )(*operands)
    loss = outs[1]
    gg = outs[2:2 + ng]
    deltas, new_m, new_v = (outs[2 + ng + k * n:2 + ng + (k + 1) * n] for k in range(3))
    return loss, gg, deltas, new_m, new_v


GAIN_ROWS = 32
GAIN_SLOTS = [(0, 1024), (8, 256), (10, 128), (11, 256), (13, 256), (15, 512), (19, 512), (23, 1024)]

SH_IN = IN_TOTAL // N_DEV
SLAB1_COLS = 384
UQ_ROWS = Q_LORA * (N_HEADS * QK_DIM // N_DEV) // SLAB1_COLS
UKV_ROWS = 48
SLAB1_ROWS = D_MODEL + UQ_ROWS + UKV_ROWS
SH_ROWS = D_MODEL // N_DEV
PL_ROWS = PLE_DIM * (D_MODEL // N_DEV) // D_MODEL
CONV_ROWS = 16
SLAB2_ROWS = 2 * SH_ROWS + PL_ROWS + CONV_ROWS
IN_SEGMENTS = [(0, 448, 0), (448, 960, 512), (1472, 2496, 1024), (960, 1472, 2048), (2496, 3008, 2560)]


def _flat_rows(a, rows, cols):
    lead = a.shape[:-2]
    flat = a.reshape(lead + (-1,))
    pad = [(0, 0)] * len(lead) + [(0, rows * cols - flat.shape[-1])]
    return jnp.pad(flat, pad).reshape(lead + (rows, cols))


def _slab1(in_sh, uq_sh, ukv_sh):
    pad = [(0, 0)] * (in_sh.ndim - 1) + [(0, SLAB1_COLS - in_sh.shape[-1])]
    return jnp.concatenate([jnp.pad(in_sh, pad), _flat_rows(uq_sh, UQ_ROWS, SLAB1_COLS), _flat_rows(ukv_sh, UKV_ROWS, SLAB1_COLS)],
                           axis=-2)


def _slab2(o_sh, plg_sh, pl_sh, conv_sh):
    return jnp.concatenate([o_sh, plg_sh, _flat_rows(pl_sh, PL_ROWS, D_MODEL), _flat_rows(conv_sh, CONV_ROWS, D_MODEL)], axis=-2)


def _in_plan():
    src = [None] * PROJ_PAD
    for a, b, start in IN_SEGMENTS:
        for o in range(a, b):
            src[start + o - a] = (o // SH_IN, o % SH_IN)
    plan = []
    for t in range(PROJ_PAD // LANES):
        runs, j = [], 0
        key_of = lambda jj: None if src[t * LANES + jj] is None else (
            src[t * LANES + jj][0], src[t * LANES + jj][1] // LANES, src[t * LANES + jj][1] % LANES - jj)
        while j < LANES:
            key = key_of(j)
            lo = j
            while j < LANES and key_of(j) == key:
                j += 1
            if key is not None:
                runs.append(key + (lo, j))
        plan.append(runs)
    return plan


def _assemble_w_in(g1):
    plan = _in_plan()
    rows = 256

    def body(g_ref, out_ref):
        src_lane = lax.broadcasted_iota(jnp.int32, (LANES, LANES), 0)
        out_lane = lax.broadcasted_iota(jnp.int32, (LANES, LANES), 1)
        for t, runs in enumerate(plan):
            acc = jnp.zeros((rows, LANES), f32)
            for d, tile, shift, lo, hi in runs:
                pick = ((src_lane == out_lane + shift) & (out_lane >= lo) & (out_lane < hi)).astype(bf16)
                acc = acc + _nn(g_ref[d, :, tile * LANES:(tile + 1) * LANES], pick)
            out_ref[:, t * LANES:(t + 1) * LANES] = acc.astype(out_ref.dtype)

    return pl.pallas_call(
        body, name="w_in_columns", grid=(D_MODEL // rows,), out_shape=jax.ShapeDtypeStruct((D_MODEL, PROJ_PAD), MM),
        in_specs=[pl.BlockSpec((N_DEV, rows, SLAB1_COLS), lambda i: (0, i, 0))],
        out_specs=pl.BlockSpec((rows, PROJ_PAD), lambda i: (i, 0)),
        compiler_params=_cparams(("parallel",)))(g1)


def _by_device(a, width):
    return a.reshape(a.shape[0], N_DEV, width).transpose(1, 0, 2)


def _from_devices(a):
    return a.transpose(1, 0, 2).reshape(a.shape[1], -1)


def kernel(x, p, positions, g_in, w_in, g_cq, w_uq, g_ckv, w_ukv, g_q, g_k, conv_w, g_oa, g_oc, w_o, w_pl, w_plg, g_pl, loss_target, m_g_in, m_w_in, m_g_cq, m_w_uq, m_g_ckv, m_w_ukv, m_g_q, m_g_k, m_conv_w, m_g_oa, m_g_oc, m_w_o, m_w_pl, m_w_plg, m_g_pl, v_g_in, v_w_in, v_g_cq, v_w_uq, v_g_ckv, v_w_ukv, v_g_q, v_g_k, v_conv_w, v_g_oa, v_g_oc, v_w_o, v_w_pl, v_w_plg, v_g_pl):
    S = x.shape[1]
    nd = N_DEV
    xs, ps, tgt = x[0], p[0, 0], loss_target[0]

    slab1 = _slab1(w_in[0], w_uq[0], w_ukv[0]).astype(bf16)
    inv_freq = 1.0 / (ROPE_THETA ** (jnp.arange(0, ROPE_DIM, 2, dtype=f32) / ROPE_DIM))
    inv_row = jnp.concatenate([inv_freq, inv_freq, jnp.zeros((LANES - ROPE_DIM,), f32)]).reshape(1, LANES)
    posf = jnp.broadcast_to(positions[0].astype(f32)[:, None], (S, LANES))
    tabs, g1 = _rope_tables(posf, inv_row, slab1)
    g1 = g1.reshape(nd, SLAB1_ROWS, SLAB1_COLS)
    conv_bits = lax.bitcast_convert_type(conv_w[0], bf16).reshape(CONV_K, -1)
    slab2 = _slab2(w_o[0].astype(bf16), w_plg[0].astype(bf16), w_pl[0].astype(bf16), conv_bits)

    w_in_p = _assemble_w_in(g1)
    wuq = _from_devices(g1[:, D_MODEL:D_MODEL + UQ_ROWS].reshape(nd, Q_LORA, -1)).reshape(Q_LORA, N_HEADS, QK_DIM)
    w_uq_p = jnp.pad(wuq, ((0, 0), (0, 0), (0, HEAD_PAD - QK_DIM))).reshape(Q_LORA, QK_PAD)
    ukv_flat = g1[:, D_MODEL + UQ_ROWS:].reshape(nd, -1)[:, :KV_LORA * V_DIM]
    wukv = _from_devices(ukv_flat.reshape(nd, KV_LORA, V_DIM)).reshape(KV_LORA, N_HEADS, 2, NOPE_DIM)
    w_ukv_p = wukv.transpose(0, 2, 1, 3).reshape(KV_LORA, 2 * ATTN_WIDTH)
    w_in_p, w_uq_p, w_ukv_p = (w.astype(MM) for w in (w_in_p, w_uq_p, w_ukv_p))

    g_q_p = jnp.pad(g_q, ((0, 0), (0, HEAD_PAD - QK_DIM)))
    g_k_p = jnp.pad(g_k, ((0, 0), (0, HEAD_PAD - QK_DIM)))

    proj, q, k, v = _fwd_proj(xs, tabs, g_in, w_in_p, g_cq, w_uq_p, g_ckv, w_ukv_p, g_q_p * SCALE_LOG2E, g_k_p)
    o, lse, g2 = _flash_fwd(q, k, v, slab2)
    w_o_f = g2[:, 0:SH_ROWS].reshape(D_MODEL, D_MODEL).astype(MM)
    w_plg_f = g2[:, SH_ROWS:2 * SH_ROWS].reshape(D_MODEL, D_MODEL).astype(MM)
    w_pl_f = _from_devices(g2[:, 2 * SH_ROWS:2 * SH_ROWS + PL_ROWS].reshape(nd, PLE_DIM, -1)).astype(MM)
    cv_bits = g2[:, 2 * SH_ROWS + PL_ROWS:].reshape(nd, -1)[:, :CONV_K * (CONV_WIDTH // nd) * 2]
    conv_full = _from_devices(lax.bitcast_convert_type(cv_bits.reshape(nd, CONV_K, CONV_WIDTH // nd, 2), f32))
    conv_w8 = jnp.pad(conv_full, ((0, 8 - CONV_K), (0, 0)))
    (d_x2, d_o, delta, d_za, d_cbzc, d_u, dw_o, dw_plg, dw_pl, dcw, dg_oa, dg_oc, dg_pl, loss_part) = _mid(
        o, proj, xs, ps, tgt, w_o_f, w_plg_f, w_pl_f, conv_w8, g_oa, g_oc, g_pl)

    slabs_a = _slab2(dw_o.reshape(nd, SH_ROWS, D_MODEL), dw_plg.reshape(nd, SH_ROWS, D_MODEL), _by_device(dw_pl, D_MODEL // nd),
                     _by_device(dcw[:CONV_K], CONV_WIDTH // nd)).astype(RS_DTYPE)
    d_q, d_k, d_v, got_a = _flash_bwd(q, k, v, d_o, lse, delta, slabs_a)
    shard_a = _sum_slabs(got_a, "rs_sum8")
    (grad_x, dw_in_p, dw_uq_p, dw_ukv_p, dg_in, dg_cq, dg_ckv, dg_q_p, dg_k_p) = _bwd_proj(
        xs, proj, d_q, d_k, d_v, d_za, d_cbzc, d_u, d_x2, tabs, g_in, w_in_p, g_cq, w_uq_p, g_ckv, w_ukv_p, g_q_p, g_k_p, conv_w8)

    def dw_in_cols(a, b):
        return [dw_in_p[:, start + max(a, s) - s:start + min(b, e) - s] for s, e, start in sorted(IN_SEGMENTS) if max(a, s) < min(b, e)]

    no_cols = jnp.zeros((D_MODEL, SLAB1_COLS - SH_IN), f32)
    in_shards = jnp.stack([jnp.concatenate(dw_in_cols(SH_IN * d, SH_IN * (d + 1)) + [no_cols], axis=1) for d in range(nd)])
    dw_uq = dw_uq_p.reshape(Q_LORA, N_HEADS, HEAD_PAD)[:, :, :QK_DIM].reshape(Q_LORA, N_HEADS * QK_DIM)
    dw_ukv = dw_ukv_p.reshape(KV_LORA, 2, N_HEADS, NOPE_DIM).transpose(0, 2, 1, 3).reshape(KV_LORA, 2 * ATTN_WIDTH)
    slabs_b = _slab1(in_shards, _by_device(dw_uq, N_HEADS * QK_DIM // nd), _by_device(dw_ukv, 2 * ATTN_WIDTH // nd))
    slabs_b = slabs_b.reshape(4, 2, SLAB1_ROWS, SLAB1_COLS)
    my_c = lax.axis_index("c").astype(jnp.int32).reshape(1)
    chip_part = _add_blocks(slabs_b, _sibling_exchange(slabs_b), my_c)
    shard_b = _sum_slabs(_chip_exchange(chip_part), "rs_sum")
    grads_w = [shard_b[:D_MODEL, :SH_IN], shard_b[D_MODEL:D_MODEL + UQ_ROWS].reshape(Q_LORA, -1),
               shard_b[D_MODEL + UQ_ROWS:].reshape(-1)[:KV_LORA * V_DIM].reshape(KV_LORA, V_DIM),
               shard_a[2 * SH_ROWS + PL_ROWS:].reshape(-1)[:CONV_K * CONV_WIDTH // nd].reshape(CONV_K, -1),
               shard_a[0:SH_ROWS], shard_a[2 * SH_ROWS:2 * SH_ROWS + PL_ROWS].reshape(PLE_DIM, -1), shard_a[SH_ROWS:2 * SH_ROWS]]

    gains_part = [dg_in, dg_cq, dg_ckv, dg_q_p, dg_k_p, dg_oa, dg_oc, dg_pl]
    gflat = jnp.concatenate([g.reshape(-1) for g in gains_part] + [jnp.zeros(((GAIN_ROWS - 1) * LANES - 3968,), f32),
                                                                    loss_part.reshape(-1)])
    gain_parts = _all_gather(gflat.reshape(GAIN_ROWS, LANES), "ag_gains", in_vmem=True).reshape(nd, GAIN_ROWS, LANES)

    gains = [g_in, g_cq, g_ckv, g_q_p, g_k_p, g_oa, g_oc, g_pl]
    padq = lambda a: jnp.pad(a, ((0, 0), (0, HEAD_PAD - QK_DIM)))
    m_gains = [m_g_in, m_g_cq, m_g_ckv, padq(m_g_q), padq(m_g_k), m_g_oa, m_g_oc, m_g_pl]
    v_gains = [v_g_in, v_g_cq, v_g_ckv, padq(v_g_q), padq(v_g_k), v_g_oa, v_g_oc, v_g_pl]
    tr = lambda a: jnp.swapaxes(a, 1, 2)
    turned = lambda l: [tr(l[0]), tr(l[1])] + list(l[2:])
    ws = gains + turned([w_in, w_uq, w_ukv, conv_w, w_o, w_pl, w_plg])
    ms = m_gains + turned([m_w_in, m_w_uq, m_w_ukv, m_conv_w, m_w_o, m_w_pl, m_w_plg])
    vs = v_gains + turned([v_w_in, v_w_uq, v_w_ukv, v_conv_w, v_w_o, v_w_pl, v_w_plg])
    loss_row, gg, deltas, new_m, new_v = _adamw_all(ws, turned([g[None] for g in grads_w]), ms, vs, gain_parts)
    grads_w = [g[None] for g in grads_w]
    ng = len(gains)
    deltas, new_m, new_v = (list(l[:ng]) + turned(l[ng:]) for l in (deltas, new_m, new_v))
    loss = loss_row[0, 0]

    def ordered(gl, wl):
        g_in_, g_cq_, g_ckv_, g_q_, g_k_, g_oa_, g_oc_, g_pl_ = gl
        g_q_, g_k_ = g_q_[:, :QK_DIM], g_k_[:, :QK_DIM]
        w_in_, w_uq_, w_ukv_, cw_, w_o_, w_pl_, w_plg_ = wl
        return [g_in_, w_in_, g_cq_, w_uq_, g_ckv_, w_ukv_, g_q_, g_k_, cw_, g_oa_, g_oc_, w_o_, w_pl_, w_plg_, g_pl_]

    ng = len(gains)
    outs = [loss, grad_x[None]]
    outs += ordered(gg, grads_w)
    for lst in (deltas, new_m, new_v):
        outs += ordered(lst[:ng], lst[ng:])
    return tuple(outs)
```

```python
import functools
import math

import jax
import jax.numpy as jnp
from jax import lax
from jax.experimental import pallas as pl
from jax.experimental.pallas import tpu as pltpu

f32 = jnp.float32
bf16 = jnp.bfloat16
MM = jnp.bfloat16
RS_DTYPE = jnp.bfloat16

D_MODEL = 1024
PLE_DIM = 256
N_HEADS = 4
NOPE_DIM = 128
ROPE_DIM = 64
V_DIM = 128
QK_DIM = NOPE_DIM + ROPE_DIM
Q_LORA = 256
KV_LORA = 128
ATTN_WIDTH = N_HEADS * V_DIM
CONV_WIDTH = D_MODEL - ATTN_WIDTH
CONV_K = 3
ROPE_THETA = 10000.0
RMS_EPS = 1e-6
NEG_INF = -1e30
IN_TOTAL = Q_LORA + KV_LORA + ROPE_DIM + ATTN_WIDTH + 4 * CONV_WIDTH
ADAM_LR = 0.001
ADAM_B1 = 0.9
ADAM_B2 = 0.999
ADAM_EPS = 1e-08
ADAM_WD = 0.01
ADAM_STEP = 10

N_DEV = 8
LANES = 128
HEAD_PAD = 256
QK_PAD = N_HEADS * HEAD_PAD
PROJ_PAD = 3072
TS = 512
TS_M = 256
TS_E = 256
TQ = 512
VMEM_LIMIT = 58 * 1024 * 1024
MESH = pl.DeviceIdType.MESH
QK_SCALE = 1.0 / math.sqrt(QK_DIM)
SCALE_LOG2E = QK_SCALE * math.log2(math.e)
DK_SCALE = QK_SCALE / SCALE_LOG2E


def _cparams(sem=None):
    return pltpu.CompilerParams(dimension_semantics=sem, vmem_limit_bytes=VMEM_LIMIT)


def _nt(a, b):
    return lax.dot_general(a, b, (((1,), (1,)), ((), ())), preferred_element_type=f32)


def _tn(a, b):
    return lax.dot_general(a, b, (((0,), (0,)), ((), ())), preferred_element_type=f32)


def _nn(a, b):
    return jnp.dot(a, b, preferred_element_type=f32)


def _rstd(x, n):
    return lax.rsqrt(jnp.sum(x * x, axis=-1, keepdims=True) * (1.0 / n) + RMS_EPS)


def _rms_bwd(dy, xhat, r, g, n):
    u = dy * g
    return r * (u - xhat * (jnp.sum(u * xhat, axis=-1, keepdims=True) * (1.0 / n)))


def _rope(t, c, s1, s2):
    return t * c + pltpu.roll(t, 32, 1) * s1 + pltpu.roll(t, 96, 1) * s2


def _rope_t(d, c, s1, s2):
    return d * c - pltpu.roll(d, 96, 1) * s2 - pltpu.roll(d, 32, 1) * s1


def _alternate(*stages):
    stages = list(stages)
    while stages:
        for st in list(stages):
            if next(st, "done") == "done":
                stages.remove(st)


def _const(shape):
    return pl.BlockSpec(shape, lambda *_: (0,) * len(shape), pipeline_mode=pl.Buffered(1))


def _two_level_gather(x_ref, out_ref, send_sems, recv_sems, local_sem):
    m_per = x_ref.shape[0]
    x, y, c = lax.axis_index("x"), lax.axis_index("y"), lax.axis_index("c")
    me, sibling = (x, y, c), (x, y, 1 - c)
    chips = [(1 - x, y), (x, 1 - y), (1 - x, 1 - y)]

    def rows(px, py, pc):
        return out_ref.at[pl.ds((4 * px + 2 * py + pc) * m_per, m_per), :]

    def copy(k, blk, to, src=None):
        return pltpu.make_async_remote_copy(
            src_ref=rows(*blk) if src is None else src, dst_ref=rows(*blk),
            send_sem=send_sems.at[k], recv_sem=recv_sems.at[k], device_id=to, device_id_type=MESH)

    mine = pltpu.make_async_copy(x_ref, rows(*me), local_sem)
    first = [copy(0, me, sibling, src=x_ref)] + [copy(1 + j, me, (*chip, c), src=x_ref) for j, chip in enumerate(chips)]
    passed = [copy(4 + j, (*chip, c), sibling) for j, chip in enumerate(chips)]

    def start():
        mine.start()
        for cp in first:
            cp.start()

    def finish():
        for j, chip in enumerate(chips):
            copy(1 + j, (*chip, c), me).wait_recv()
            passed[j].start()
        copy(0, sibling, me).wait_recv()
        for j, chip in enumerate(chips):
            copy(4 + j, (*chip, 1 - c), me).wait_recv()
        for cp in first + passed:
            cp.wait_send()
        mine.wait()

    return start, finish


GATHER_SEMS = [pltpu.SemaphoreType.DMA((7,)), pltpu.SemaphoreType.DMA((7,)), pltpu.SemaphoreType.DMA]


def _rope_tables(posf, inv_freq, block):
    S = posf.shape[0]
    t = min(S, TS)
    nt = S // t

    def body(pos_ref, f_ref, blk_ref, c_ref, s1_ref, s2_ref, all_ref, send_sems, recv_sems, local_sem):
        i = pl.program_id(0)
        start, finish = _two_level_gather(blk_ref, all_ref, send_sems, recv_sems, local_sem)

        @pl.when(i == 0)
        def _():
            start()

        ang = pos_ref[...] * f_ref[...]
        lane = lax.broadcasted_iota(jnp.int32, ang.shape, 1)
        cs, sn = jnp.cos(ang), jnp.sin(ang)
        c_ref[...] = jnp.where(lane < ROPE_DIM, cs, 0.0)
        s1_ref[...] = jnp.where((lane >= ROPE_DIM // 2) & (lane < ROPE_DIM), sn, 0.0)
        s2_ref[...] = jnp.where(lane < ROPE_DIM // 2, -sn, 0.0)

        @pl.when(i == nt - 1)
        def _():
            finish()

    spec = pl.BlockSpec((t, LANES), lambda i: (i, 0))
    hbm = pl.BlockSpec(memory_space=pl.ANY)
    outs = pl.pallas_call(
        body, name="rope_tables", grid=(nt,),
        out_shape=[jax.ShapeDtypeStruct((S, LANES), f32)] * 3 + [jax.ShapeDtypeStruct((N_DEV * block.shape[0], block.shape[1]), block.dtype)],
        in_specs=[spec, _const((1, LANES)), hbm], out_specs=[spec] * 3 + [hbm],
        scratch_shapes=list(GATHER_SEMS),
        compiler_params=_cparams(("arbitrary",)),
    )(posf, inv_freq, block)
    return outs[:3], outs[3]


def _fwd_proj(x, tabs, g_in, w_in_p, g_cq, w_uq_p, g_ckv, w_ukv_p, g_q_p, g_k_p):
    S = x.shape[0]
    T = min(TS, S)
    nt = S // T
    LAT = Q_LORA + KV_LORA + LANES
    PIECE = 512

    def body(x_ref, c_ref, s1_ref, s2_ref, g_in_ref, w_in_ref, g_cq_ref, w_uq_ref, g_ckv_ref, w_ukv_ref,
             g_q_ref, g_k_ref, proj_ref, q_ref, k_ref, v_ref, lat_a, lat_b):
        i = pl.program_id(0)

        @pl.when(i == 0)
        def _():
            lat_b[...] = jnp.zeros(lat_b.shape, f32)

        def stage1(lat):
            xv = x_ref[...]
            h = (xv * _rstd(xv, D_MODEL) * g_in_ref[...]).astype(MM)
            yield
            for cb in range(PROJ_PAD // PIECE):
                cols = slice(cb * PIECE, (cb + 1) * PIECE)
                proj_ref[:, cols] = _nn(h, w_in_ref[:, cols])
                yield
            lat[...] = proj_ref[:, 0:LAT]

        def stage2(lat):
            c, s1, s2 = c_ref[...], s1_ref[...], s2_ref[...]
            g_q, g_k = g_q_ref[...], g_k_ref[...]
            c_q = lat[:, 0:Q_LORA]
            n_cq = (c_q * _rstd(c_q, Q_LORA) * g_cq_ref[...]).astype(MM)
            q_raw = _nn(n_cq, w_uq_ref[...])
            for hd in range(N_HEADS):
                qh = q_raw[:, hd * HEAD_PAD:(hd + 1) * HEAD_PAD]
                qn = qh * _rstd(qh, QK_DIM) * g_q
                q_ref[:, hd * HEAD_PAD:hd * HEAD_PAD + LANES] = qn[:, :LANES].astype(MM)
                q_ref[:, hd * HEAD_PAD + LANES:(hd + 1) * HEAD_PAD] = _rope(qn[:, LANES:], c, s1, s2).astype(MM)
                yield

            c_kv = lat[:, Q_LORA:Q_LORA + KV_LORA]
            n_ckv = (c_kv * _rstd(c_kv, KV_LORA) * g_ckv_ref[...]).astype(MM)
            kv_raw = _nn(n_ckv, w_ukv_ref[...])
            kpe = lat[:, Q_LORA + KV_LORA:LAT]
            ss_pe = jnp.sum(kpe * kpe, axis=-1, keepdims=True)
            kr = _rope(kpe * g_k[:, LANES:], c, s1, s2)
            for hd in range(N_HEADS):
                kn = kv_raw[:, hd * LANES:(hd + 1) * LANES]
                rk = lax.rsqrt((jnp.sum(kn * kn, axis=-1, keepdims=True) + ss_pe) * (1.0 / QK_DIM) + RMS_EPS)
                k_ref[:, hd * HEAD_PAD:hd * HEAD_PAD + LANES] = (kn * rk * g_k[:, :LANES]).astype(MM)
                k_ref[:, hd * HEAD_PAD + LANES:(hd + 1) * HEAD_PAD] = (kr * rk).astype(MM)
                yield
            v_ref[...] = kv_raw[:, N_HEADS * NOPE_DIM:].astype(MM)

        @pl.when(i % 2 == 0)
        def _():
            _alternate(stage1(lat_a), stage2(lat_b))

        @pl.when(i % 2 == 1)
        def _():
            _alternate(stage1(lat_b), stage2(lat_a))

    cur = lambda w: pl.BlockSpec((T, w), lambda i: (jnp.minimum(i, nt - 1), 0))
    prv = lambda w: pl.BlockSpec((T, w), lambda i: (jnp.maximum(i - 1, 0), 0))
    return pl.pallas_call(
        body, name="fwd_proj", grid=(nt + 1,),
        out_shape=[jax.ShapeDtypeStruct((S, PROJ_PAD), f32), jax.ShapeDtypeStruct((S, QK_PAD), MM),
                   jax.ShapeDtypeStruct((S, QK_PAD), MM), jax.ShapeDtypeStruct((S, ATTN_WIDTH), MM)],
        in_specs=[cur(D_MODEL), prv(LANES), prv(LANES), prv(LANES), _const((1, D_MODEL)), _const((D_MODEL, PROJ_PAD)),
                  _const((1, Q_LORA)), _const((Q_LORA, QK_PAD)), _const((1, KV_LORA)), _const((KV_LORA, 2 * ATTN_WIDTH)),
                  _const((1, HEAD_PAD)), _const((1, HEAD_PAD))],
        out_specs=[cur(PROJ_PAD), prv(QK_PAD), prv(QK_PAD), prv(ATTN_WIDTH)],
        scratch_shapes=[pltpu.VMEM((T, LAT), f32), pltpu.VMEM((T, LAT), f32)],
        compiler_params=_cparams(("arbitrary",)),
    )(x, *tabs, g_in, w_in_p, g_cq, w_uq_p, g_ckv, w_ukv_p, g_q_p, g_k_p)


def _exchange(src_of, dst_ref, send_sems, recv_sems, local_sem):
    x, y, c = lax.axis_index("x"), lax.axis_index("y"), lax.axis_index("c")
    me = 4 * x + 2 * y + c
    remote = []
    for k in range(1, N_DEV):
        px = 1 - x if (k >> 2) & 1 else x
        py = 1 - y if (k >> 1) & 1 else y
        pc = 1 - c if k & 1 else c
        remote.append(pltpu.make_async_remote_copy(
            src_ref=src_of(4 * px + 2 * py + pc), dst_ref=dst_ref.at[me], send_sem=send_sems.at[k - 1],
            recv_sem=recv_sems.at[k - 1], device_id=(px, py, pc), device_id_type=MESH))
    return remote, pltpu.make_async_copy(src_of(me), dst_ref.at[me], local_sem)


def _exchange_start(copies):
    remote, local = copies
    local.start()
    for cp in remote:
        cp.start()


def _exchange_wait(copies):
    remote, local = copies
    for cp in remote:
        cp.wait_recv()
    for cp in remote:
        cp.wait_send()
    local.wait()


EXCHANGE_SEMS = [pltpu.SemaphoreType.DMA((N_DEV - 1,)), pltpu.SemaphoreType.DMA((N_DEV - 1,)), pltpu.SemaphoreType.DMA]


def _flash_fwd(q, k, v, block):
    S = q.shape[0]
    T = min(TQ, S)
    nq = S // T

    def body(q_ref, k_ref, v_ref, blk_ref, o_ref, lse_ref, all_ref, s_a, s_b, m_sc, l_sc, acc_sc, send_sems, recv_sems, local_sem):
        n = pl.program_id(1)
        gather = _exchange(lambda d: blk_ref, all_ref, send_sems, recv_sems, local_sem)

        @pl.when((pl.program_id(0) == 0) & (n == 0))
        def _():
            _exchange_start(gather)

        m_sc[...] = jnp.full(m_sc.shape, NEG_INF, f32)
        l_sc[...] = jnp.zeros(l_sc.shape, f32)
        acc_sc[...] = jnp.zeros(acc_sc.shape, f32)
        qv = q_ref[...]

        def rows(j):
            return pl.ds(pl.multiple_of(j * T, T), T)

        def scores(j, dst):
            dst[...] = _nt(k_ref[rows(j), :], qv)

        def update(src, j, masked):
            s = src[...]
            if masked:
                k_i = lax.broadcasted_iota(jnp.int32, (T, T), 0)
                q_i = lax.broadcasted_iota(jnp.int32, (T, T), 1)
                s = jnp.where(k_i <= q_i, s, NEG_INF)
            m_prev = m_sc[...]
            m_new = jnp.maximum(m_prev, jnp.max(s, axis=0, keepdims=True))
            alpha = jnp.exp2(m_prev - m_new)
            p = jnp.exp2(s - m_new[0:1, :])
            l_sc[...] = alpha * l_sc[...] + jnp.sum(p, axis=0, keepdims=True)
            acc_sc[...] = alpha[0:1, :] * acc_sc[...] + _tn(v_ref[rows(j), :], p.astype(MM))
            m_sc[...] = m_new

        scores(0, s_a)

        def pair(t, carry):
            j = 2 * t
            scores(j + 1, s_b)
            update(s_a, j, False)
            scores(j + 2, s_a)
            update(s_b, j + 1, False)
            return carry

        lax.fori_loop(0, n // 2, pair, 0)

        @pl.when(n % 2 == 0)
        def _():
            update(s_a, n, True)

        @pl.when(n % 2 == 1)
        def _():
            scores(n, s_b)
            update(s_a, n - 1, False)
            update(s_b, n, True)

        o_ref[...] = (acc_sc[...] / l_sc[0:1, :]).T
        lse_ref[...] = m_sc[...] + jnp.log2(l_sc[...])

        @pl.when((pl.program_id(0) == N_HEADS - 1) & (n == nq - 1))
        def _():
            _exchange_wait(gather)

    return pl.pallas_call(
        body, name="flash_fwd", grid=(N_HEADS, nq),
        out_shape=[jax.ShapeDtypeStruct((S, ATTN_WIDTH), f32), jax.ShapeDtypeStruct((N_HEADS * 8, S), f32),
                   jax.ShapeDtypeStruct((N_DEV,) + block.shape, block.dtype)],
        in_specs=[pl.BlockSpec((T, HEAD_PAD), lambda h, i: (i, h)),
                  pl.BlockSpec((S, HEAD_PAD), lambda h, i: (0, h)),
                  pl.BlockSpec((S, V_DIM), lambda h, i: (0, h)),
                  pl.BlockSpec(memory_space=pl.ANY)],
        out_specs=[pl.BlockSpec((T, V_DIM), lambda h, i: (i, h)), pl.BlockSpec((8, T), lambda h, i: (h, i)),
                   pl.BlockSpec(memory_space=pl.ANY)],
        scratch_shapes=[pltpu.VMEM((T, T), f32), pltpu.VMEM((T, T), f32), pltpu.VMEM((8, T), f32), pltpu.VMEM((8, T), f32),
                        pltpu.VMEM((V_DIM, T), f32)] + EXCHANGE_SEMS,
        compiler_params=_cparams(("arbitrary", "arbitrary")),
    )(q, k, v, block)


def _shift_down(m, prev8, n):
    T = m.shape[0]
    rows = lax.broadcasted_iota(jnp.int32, m.shape, 0)
    head = jnp.tile(pltpu.roll(prev8, n, 0), (T // 8, 1))
    return jnp.where(rows >= n, pltpu.roll(m, n, 0), head)


def _shift_up(m, next8, n):
    T = m.shape[0]
    rows = lax.broadcasted_iota(jnp.int32, m.shape, 0)
    tail = jnp.tile(pltpu.roll(next8, 8 - n, 0), (T // 8, 1))
    return jnp.where(rows < T - n, pltpu.roll(m, T - n, 0), tail)


def _mid(o, proj, x, p, target, w_o, w_plg, w_pl, conv_w8, g_oa, g_oc, g_pl):
    S = x.shape[0]
    T = min(TS_M, S)
    nt = S // T

    def body(o_ref, za_ref, ccx_ref, cbzc_ref, prev_ref, x_ref, p_ref, t_ref, wo_ref, wplg_ref, wpl_ref, cw_ref,
             goa_ref, goc_ref, gpl_ref,
             dx2_ref, do_ref, delta_ref, dza_ref, dcbzc_ref, du_ref,
             dwo_ref, dwplg_ref, dwpl_ref, dcw_ref, dgoa_ref, dgoc_ref, dgpl_ref, loss_ref):
        i = pl.program_id(0)

        @pl.when(i == 0)
        def _():
            for r in (dwo_ref, dwplg_ref, dwpl_ref, dcw_ref, dgoa_ref, dgoc_ref, dgpl_ref, loss_ref):
                r[...] = jnp.zeros(r.shape, f32)

        o_v = o_ref[...]
        z_a = za_ref[...]
        cc, cx = ccx_ref[:, :CONV_WIDTH], ccx_ref[:, CONV_WIDTH:]
        cb, z_c = cbzc_ref[:, :CONV_WIDTH], cbzc_ref[:, CONV_WIDTH:]
        g_oa, g_oc, g_pl = goa_ref[...], goc_ref[...], gpl_ref[...]
        w0, w1, w2 = cw_ref[0:1, :], cw_ref[1:2, :], cw_ref[2:3, :]
        p_b = p_ref[...].astype(MM)
        pw = _nn(p_b, wpl_ref[...])

        sa = jax.nn.sigmoid(z_a)
        silu_a = z_a * sa
        ga = o_v * silu_a
        ra = _rstd(ga, ATTN_WIDTH)
        gha = ga * ra
        ya = (gha * g_oa).astype(MM)
        x2 = x_ref[...] + _nn(ya, wo_ref[:ATTN_WIDTH, :])
        m0 = cc * cx
        prev = prev_ref[:, :CONV_WIDTH] * prev_ref[:, CONV_WIDTH:] * (i > 0).astype(f32)
        m1 = _shift_down(m0, prev, 1)
        m2 = _shift_down(m0, prev, 2)
        u = w0 * m2 + w1 * m1 + w2 * m0
        sc = jax.nn.sigmoid(z_c)
        silu_c = z_c * sc
        gc = cb * u * silu_c
        rc = _rstd(gc, CONV_WIDTH)
        ghc = gc * rc
        yc = (ghc * g_oc).astype(MM)
        ycat = jnp.concatenate([ya, yc], axis=-1)
        x2 = x2 + _nn(yc, wo_ref[ATTN_WIDTH:, :])
        r2 = _rstd(x2, D_MODEL)
        xh2 = x2 * r2
        n2 = (xh2 * g_pl).astype(MM)
        gate = jax.nn.sigmoid(_nn(n2, wplg_ref[...]))
        err = x2 + gate * pw - t_ref[...]
        loss_ref[...] += jnp.sum(jnp.sum(err * err, axis=-1, keepdims=True), axis=0, keepdims=True) * (0.5 / D_MODEL)
        d_out = err * (1.0 / D_MODEL)
        d_glog = (d_out * pw * gate * (1.0 - gate)).astype(MM)
        d_n2 = _nt(d_glog, wplg_ref[...])
        dwpl_ref[...] += _tn(p_b, (d_out * gate).astype(MM))
        dwplg_ref[...] += _tn(n2, d_glog)
        dgpl_ref[...] += jnp.sum(d_n2 * xh2, axis=0, keepdims=True)
        d_x2 = d_out + _rms_bwd(d_n2, xh2, r2, g_pl, D_MODEL)
        dx2_ref[...] = d_x2
        d_x2b = d_x2.astype(MM)
        d_ycat = _nt(d_x2b, wo_ref[...])
        dwo_ref[...] += _tn(ycat, d_x2b)
        d_ya, d_yc = d_ycat[:, :ATTN_WIDTH], d_ycat[:, ATTN_WIDTH:]
        dgoa_ref[...] += jnp.sum(d_ya * gha, axis=0, keepdims=True)
        dgoc_ref[...] += jnp.sum(d_yc * ghc, axis=0, keepdims=True)
        d_ga = _rms_bwd(d_ya, gha, ra, g_oa, ATTN_WIDTH)
        d_gc = _rms_bwd(d_yc, ghc, rc, g_oc, CONV_WIDTH)
        d_o = d_ga * silu_a
        do_ref[...] = d_o.astype(MM)
        dza_ref[...] = (d_ga * o_v * (sa * (1.0 + z_a * (1.0 - sa)))).astype(MM)
        for hd in range(N_HEADS):
            cols = slice(hd * V_DIM, (hd + 1) * V_DIM)
            dl = jnp.sum(d_o[:, cols] * o_v[:, cols], axis=-1, keepdims=True)
            delta_ref[hd * 8:(hd + 1) * 8, :] = jnp.broadcast_to(dl, (T, LANES)).T[0:8, :]
        d_u = d_gc * cb * silu_c
        du_ref[...] = d_u
        dcbzc_ref[:, :CONV_WIDTH] = (d_gc * u * silu_c).astype(MM)
        dcbzc_ref[:, CONV_WIDTH:] = (d_gc * cb * u * (sc * (1.0 + z_c * (1.0 - sc)))).astype(MM)
        dcw_ref[0:1, :] += jnp.sum(d_u * m2, axis=0, keepdims=True)
        dcw_ref[1:2, :] += jnp.sum(d_u * m1, axis=0, keepdims=True)
        dcw_ref[2:3, :] += jnp.sum(d_u * m0, axis=0, keepdims=True)

    row = lambda w, blk=0: pl.BlockSpec((T, w), lambda i: (i, blk))
    acc = lambda shape: pl.BlockSpec(shape, lambda i: (0, 0))
    tb = T // 8
    sds = jax.ShapeDtypeStruct
    return pl.pallas_call(
        body, name="mid", grid=(nt,),
        out_shape=[sds((S, D_MODEL), f32), sds((S, ATTN_WIDTH), MM), sds((N_HEADS * 8, S), f32), sds((S, ATTN_WIDTH), MM),
                   sds((S, 2 * CONV_WIDTH), MM), sds((S, CONV_WIDTH), f32),
                   sds((D_MODEL, D_MODEL), f32), sds((D_MODEL, D_MODEL), f32), sds((PLE_DIM, D_MODEL), f32), sds((8, CONV_WIDTH), f32),
                   sds((1, ATTN_WIDTH), f32), sds((1, CONV_WIDTH), f32), sds((1, D_MODEL), f32), sds((1, LANES), f32)],
        in_specs=[row(ATTN_WIDTH), row(ATTN_WIDTH, 1), row(2 * CONV_WIDTH, 1), row(2 * CONV_WIDTH, 2),
                  pl.BlockSpec((8, 2 * CONV_WIDTH), lambda i: (jnp.maximum(i * tb - 1, 0), 1)),
                  row(D_MODEL), row(PLE_DIM), row(D_MODEL),
                  _const((D_MODEL, D_MODEL)), _const((D_MODEL, D_MODEL)), _const((PLE_DIM, D_MODEL)), _const((8, CONV_WIDTH)),
                  _const((1, ATTN_WIDTH)), _const((1, CONV_WIDTH)), _const((1, D_MODEL))],
        out_specs=[row(D_MODEL), row(ATTN_WIDTH), pl.BlockSpec((N_HEADS * 8, T), lambda i: (0, i)), row(ATTN_WIDTH),
                   row(2 * CONV_WIDTH), row(CONV_WIDTH),
                   acc((D_MODEL, D_MODEL)), acc((D_MODEL, D_MODEL)), acc((PLE_DIM, D_MODEL)), acc((8, CONV_WIDTH)),
                   acc((1, ATTN_WIDTH)), acc((1, CONV_WIDTH)), acc((1, D_MODEL)), acc((1, LANES))],
        compiler_params=_cparams(("arbitrary",)),
    )(o, proj, proj, proj, proj, x, p, target, w_o, w_plg, w_pl, conv_w8, g_oa, g_oc, g_pl)


def _flash_bwd(q, k, v, d_o, lse2, delta, by_target):
    S = q.shape[0]
    T = min(TQ, S)
    nq = S // T

    def body(k_ref, v_ref, q_ref, do_ref, lse_ref, delta_ref, out_ref, dq_ref, dk_ref, dv_ref, in_ref, s_a, dp_a, s_b, dp_b,
             send_sems, recv_sems, local_sem):
        step = pl.program_id(1)
        j = nq - 1 - step
        scatter = _exchange(lambda d: out_ref.at[d], in_ref, send_sems, recv_sems, local_sem)

        @pl.when((pl.program_id(0) == 0) & (step == 0))
        def _():
            _exchange_start(scatter)

        @pl.when(step == 0)
        def _():
            dq_ref[...] = jnp.zeros(dq_ref.shape, f32)

        dk_ref[...] = jnp.zeros(dk_ref.shape, f32)
        dv_ref[...] = jnp.zeros(dv_ref.shape, f32)
        kv, vv = k_ref[...], v_ref[...]

        def rows_of(i):
            return pl.ds(pl.multiple_of(i * T, T), T)

        def scores(i, s_dst, dp_dst):
            rows = rows_of(i)
            s_dst[...] = _nt(kv, q_ref[rows, :])
            dp_dst[...] = _nt(vv, do_ref[rows, :])

        def grads(i, s_src, dp_src, masked):
            rows = rows_of(i)
            s = s_src[...]
            if masked:
                k_i = lax.broadcasted_iota(jnp.int32, (T, T), 0)
                q_i = lax.broadcasted_iota(jnp.int32, (T, T), 1)
                s = jnp.where(k_i <= q_i, s, NEG_INF)
            p = jnp.exp2(s - lse_ref[0:1, rows])
            ds = (p * (dp_src[...] - delta_ref[0:1, rows])).astype(MM)
            dv_ref[...] += _nn(p.astype(MM), do_ref[rows, :])
            dk_ref[...] += _nn(ds, q_ref[rows, :])
            dq_ref[rows, :] += _tn(ds, kv)

        last = nq - 1
        scores(j, s_a, dp_a)
        scores(jnp.minimum(j + 1, last), s_b, dp_b)
        grads(j, s_a, dp_a, True)
        rest = last - j
        full = jnp.maximum(rest - 1, 0) // 2

        def pair(t, carry):
            u = j + 1 + 2 * t
            scores(u + 1, s_a, dp_a)
            grads(u, s_b, dp_b, False)
            scores(u + 2, s_b, dp_b)
            grads(u + 1, s_a, dp_a, False)
            return carry

        lax.fori_loop(0, full, pair, 0)
        left = rest - 2 * full

        @pl.when(left == 2)
        def _():
            scores(last, s_a, dp_a)
            grads(last - 1, s_b, dp_b, False)
            grads(last, s_a, dp_a, False)

        @pl.when(left == 1)
        def _():
            grads(last, s_b, dp_b, False)

        @pl.when((pl.program_id(0) == N_HEADS - 1) & (step == nq - 1))
        def _():
            _exchange_wait(scatter)

    whole = lambda w: pl.BlockSpec((S, w), lambda h, j: (0, h))
    stat = pl.BlockSpec((8, S), lambda h, j: (h, 0))
    blk = lambda w: pl.BlockSpec((T, w), lambda h, step: (nq - 1 - step, h))
    return pl.pallas_call(
        body, name="flash_bwd", grid=(N_HEADS, nq),
        out_shape=[jax.ShapeDtypeStruct((S, QK_PAD), f32), jax.ShapeDtypeStruct((S, QK_PAD), f32),
                   jax.ShapeDtypeStruct((S, ATTN_WIDTH), f32), jax.ShapeDtypeStruct(by_target.shape, by_target.dtype)],
        in_specs=[blk(HEAD_PAD), blk(V_DIM), whole(HEAD_PAD), whole(V_DIM), stat, stat, pl.BlockSpec(memory_space=pl.ANY)],
        out_specs=[whole(HEAD_PAD), blk(HEAD_PAD), blk(V_DIM), pl.BlockSpec(memory_space=pl.ANY)],
        scratch_shapes=[pltpu.VMEM((T, T), f32)] * 4 + EXCHANGE_SEMS,
        compiler_params=_cparams(("arbitrary", "arbitrary")),
    )(k, v, q, d_o, lse2, delta, by_target)


def _bwd_proj(x, proj, d_q, d_k, d_v, d_za, d_cbzc, d_u, d_x2, tabs, g_in, w_in_p, g_cq, w_uq_p, g_ckv, w_ukv_p,
              g_q_p, g_k_p, conv_w8):
    S = x.shape[0]
    T = min(TS_E, S)
    nt = S // T
    PIECE = 512

    def body(x_ref, cqkv_ref, ccx_ref, dq_ref, dk_ref, dv_ref, dza_ref, dcbzc_ref, du_ref, dun_ref, dx2_ref,
             c_ref, s1_ref, s2_ref, g_in_ref, w_in_ref, g_cq_ref, w_uq_ref, g_ckv_ref, w_ukv_ref, g_q_ref, g_k_ref, cw_ref,
             gx_ref, dwin_hbm, dwuq_ref, dwukv_ref, dgin_ref, dgcq_ref, dgckv_ref, dgq_ref, dgk_ref,
             dproj_a, dproj_b, dqraw_sc, dkvraw_sc, dwin_sc, sem):
        i = pl.program_id(0)

        @pl.when(i == 0)
        def _():
            dwin_sc[...] = jnp.zeros(dwin_sc.shape, f32)
            dproj_b[...] = jnp.zeros(dproj_b.shape, MM)
            for r in (dwuq_ref, dwukv_ref, dgin_ref, dgcq_ref, dgckv_ref, dgq_ref, dgk_ref):
                r[...] = jnp.zeros(r.shape, f32)

        def stage1(dproj_sc):
            live = (i < nt).astype(f32)
            c, s1, s2 = c_ref[...], s1_ref[...], s2_ref[...]
            g_q, g_k = g_q_ref[...], g_k_ref[...]
            g_cq, g_ckv = g_cq_ref[...], g_ckv_ref[...]

            c_q = cqkv_ref[:, 0:Q_LORA]
            r_cq = _rstd(c_q, Q_LORA)
            cqh = c_q * r_cq
            n_cq = (cqh * g_cq).astype(MM)
            q_raw = _nn(n_cq, w_uq_ref[...])
            dgq = jnp.zeros((1, HEAD_PAD), f32)
            for hd in range(N_HEADS):
                cols = slice(hd * HEAD_PAD, (hd + 1) * HEAD_PAD)
                qh = q_raw[:, cols]
                rq = _rstd(qh, QK_DIM)
                qhh = qh * rq
                dqh = dq_ref[:, cols] * QK_SCALE
                d_qn = jnp.concatenate([dqh[:, :LANES], _rope_t(dqh[:, LANES:], c, s1, s2)], axis=-1)
                dgq += jnp.sum(d_qn * qhh, axis=0, keepdims=True)
                dqraw_sc[:, cols] = _rms_bwd(d_qn, qhh, rq, g_q, QK_DIM).astype(MM)
                yield
            dgq_ref[...] += dgq * live
            d_qraw = dqraw_sc[...]
            dwuq_ref[...] += _tn((cqh * (g_cq * live)).astype(MM), d_qraw)
            d_ncq = _nt(d_qraw, w_uq_ref[...])
            dgcq_ref[...] += jnp.sum(d_ncq * cqh, axis=0, keepdims=True) * live
            dproj_sc[:, 0:Q_LORA] = _rms_bwd(d_ncq, cqh, r_cq, g_cq, Q_LORA).astype(MM)
            yield

            c_kv = cqkv_ref[:, Q_LORA:Q_LORA + KV_LORA]
            r_ckv = _rstd(c_kv, KV_LORA)
            ckvh = c_kv * r_ckv
            n_ckv = (ckvh * g_ckv).astype(MM)
            k_nope = _nn(n_ckv, w_ukv_ref[:, :N_HEADS * NOPE_DIM])
            kpe = cqkv_ref[:, Q_LORA + KV_LORA:Q_LORA + KV_LORA + LANES]
            ss_pe = jnp.sum(kpe * kpe, axis=-1, keepdims=True)
            g_kn, g_kp = g_k[:, :LANES], g_k[:, LANES:]
            d_kpe = jnp.zeros((T, LANES), f32)
            dgk_n = jnp.zeros((1, LANES), f32)
            dgk_p = jnp.zeros((1, LANES), f32)
            for hd in range(N_HEADS):
                kn = k_nope[:, hd * LANES:(hd + 1) * LANES]
                rk = lax.rsqrt((jnp.sum(kn * kn, axis=-1, keepdims=True) + ss_pe) * (1.0 / QK_DIM) + RMS_EPS)
                knh, kph = kn * rk, kpe * rk
                d_kn_n = dk_ref[:, hd * HEAD_PAD:hd * HEAD_PAD + LANES] * DK_SCALE
                d_kr = _rope_t(dk_ref[:, hd * HEAD_PAD + LANES:(hd + 1) * HEAD_PAD] * DK_SCALE, c, s1, s2)
                dgk_n += jnp.sum(d_kn_n * knh, axis=0, keepdims=True)
                dgk_p += jnp.sum(d_kr * kph, axis=0, keepdims=True)
                u_n, u_p = d_kn_n * g_kn, d_kr * g_kp
                mt = (jnp.sum(u_n * knh, axis=-1, keepdims=True) + jnp.sum(u_p * kph, axis=-1, keepdims=True)) * (1.0 / QK_DIM)
                dkvraw_sc[:, hd * LANES:(hd + 1) * LANES] = (rk * (u_n - knh * mt)).astype(MM)
                d_kpe += rk * (u_p - kph * mt)
                yield
            dgk_ref[:, :LANES] += dgk_n * live
            dgk_ref[:, LANES:] += dgk_p * live
            dkvraw_sc[:, N_HEADS * NOPE_DIM:] = dv_ref[...].astype(MM)
            d_kvraw = dkvraw_sc[...]
            dwukv_ref[...] += _tn((ckvh * (g_ckv * live)).astype(MM), d_kvraw)
            d_nckv = _nt(d_kvraw, w_ukv_ref[...])
            dgckv_ref[...] += jnp.sum(d_nckv * ckvh, axis=0, keepdims=True) * live
            dproj_sc[:, Q_LORA:Q_LORA + KV_LORA] = _rms_bwd(d_nckv, ckvh, r_ckv, g_ckv, KV_LORA).astype(MM)
            dproj_sc[:, Q_LORA + KV_LORA:Q_LORA + KV_LORA + LANES] = d_kpe.astype(MM)
            yield

            dproj_sc[:, 512:1024] = dza_ref[...]
            d_u = du_ref[...]
            nxt = dun_ref[...] * (i < nt - 1).astype(f32)
            d_m = cw_ref[2:3, :] * d_u + cw_ref[1:2, :] * _shift_up(d_u, nxt, 1) + cw_ref[0:1, :] * _shift_up(d_u, nxt, 2)
            dproj_sc[:, 1024:1536] = (d_m * ccx_ref[:, CONV_WIDTH:]).astype(MM)
            dproj_sc[:, 1536:2048] = (d_m * ccx_ref[:, :CONV_WIDTH]).astype(MM)
            dproj_sc[:, 2048:3072] = dcbzc_ref[...]

        def stage2(dproj_sc):
            g_in = g_in_ref[...]
            xv = x_ref[...]
            r_in = _rstd(xv, D_MODEL)
            xh = xv * r_in
            hb = (xh * g_in).astype(MM)
            yield
            d_h = jnp.zeros((T, D_MODEL), f32)
            for cb in range(PROJ_PAD // PIECE):
                cols = slice(cb * PIECE, (cb + 1) * PIECE)
                d_piece = dproj_sc[:, cols]
                dwin_sc[:, cols] += _tn(hb, d_piece)
                d_h = d_h + _nt(d_piece, w_in_ref[:, cols])
                yield
            dgin_ref[...] += jnp.sum(d_h * xh, axis=0, keepdims=True)
            gx_ref[...] = dx2_ref[...] + _rms_bwd(d_h, xh, r_in, g_in, D_MODEL)

        @pl.when(i % 2 == 0)
        def _():
            _alternate(stage2(dproj_b), stage1(dproj_a))

        @pl.when(i % 2 == 1)
        def _():
            _alternate(stage2(dproj_a), stage1(dproj_b))

        @pl.when(i == nt)
        def _():
            cp = pltpu.make_async_copy(dwin_sc, dwin_hbm, sem)
            cp.start()
            cp.wait()

    cur = lambda i: jnp.minimum(i, nt - 1)
    prv = lambda i: jnp.maximum(i - 1, 0)
    row = lambda w, blk=0: pl.BlockSpec((T, w), lambda i: (cur(i), blk))
    row2 = lambda w: pl.BlockSpec((T, w), lambda i: (prv(i), 0))
    acc = lambda shape: pl.BlockSpec(shape, lambda i: (0, 0))
    tb = T // 8
    sds = jax.ShapeDtypeStruct
    return pl.pallas_call(
        body, name="bwd_proj", grid=(nt + 1,),
        out_shape=[sds((S, D_MODEL), f32), sds((D_MODEL, PROJ_PAD), f32), sds((Q_LORA, QK_PAD), f32),
                   sds((KV_LORA, 2 * ATTN_WIDTH), f32), sds((1, D_MODEL), f32), sds((1, Q_LORA), f32), sds((1, KV_LORA), f32),
                   sds((1, HEAD_PAD), f32), sds((1, HEAD_PAD), f32)],
        in_specs=[row2(D_MODEL), row(512, 0), row(2 * CONV_WIDTH, 1), row(QK_PAD), row(QK_PAD), row(ATTN_WIDTH),
                  row(ATTN_WIDTH), row(2 * CONV_WIDTH), row(CONV_WIDTH),
                  pl.BlockSpec((8, CONV_WIDTH), lambda i: (jnp.minimum((cur(i) + 1) * tb, S // 8 - 1), 0)),
                  row2(D_MODEL), row(LANES), row(LANES), row(LANES),
                  _const((1, D_MODEL)), _const((D_MODEL, PROJ_PAD)), _const((1, Q_LORA)), _const((Q_LORA, QK_PAD)),
                  _const((1, KV_LORA)), _const((KV_LORA, 2 * ATTN_WIDTH)), _const((1, HEAD_PAD)), _const((1, HEAD_PAD)),
                  _const((8, CONV_WIDTH))],
        out_specs=[row2(D_MODEL), pl.BlockSpec(memory_space=pl.ANY), acc((Q_LORA, QK_PAD)), acc((KV_LORA, 2 * ATTN_WIDTH)),
                   acc((1, D_MODEL)), acc((1, Q_LORA)), acc((1, KV_LORA)), acc((1, HEAD_PAD)), acc((1, HEAD_PAD))],
        scratch_shapes=[pltpu.VMEM((T, PROJ_PAD), MM), pltpu.VMEM((T, PROJ_PAD), MM), pltpu.VMEM((T, QK_PAD), MM),
                        pltpu.VMEM((T, 2 * ATTN_WIDTH), MM), pltpu.VMEM((D_MODEL, PROJ_PAD), f32), pltpu.SemaphoreType.DMA],
        compiler_params=_cparams(("arbitrary",)),
    )(x, proj, proj, d_q, d_k, d_v, d_za, d_cbzc, d_u, d_u, d_x2, *tabs, g_in, w_in_p, g_cq, w_uq_p, g_ckv, w_ukv_p,
      g_q_p, g_k_p, conv_w8)


def _sibling_exchange(pack, small):
    nchip, _, R, n = pack.shape

    def body(p_ref, s_ref, got_ref, all_ref, send_sems, recv_sems, g_send, g_recv, g_local):
        x, y, c = lax.axis_index("x"), lax.axis_index("y"), lax.axis_index("c")
        start, finish = _two_level_gather(s_ref, all_ref, g_send, g_recv, g_local)
        start()
        copies = [pltpu.make_async_remote_copy(src_ref=p_ref.at[t, 1 - c], dst_ref=got_ref.at[t], send_sem=send_sems.at[t],
                                               recv_sem=recv_sems.at[t], device_id=(x, y, 1 - c), device_id_type=MESH)
                  for t in range(nchip)]
        for cp in copies:
            cp.start()
        finish()
        for cp in copies:
            cp.wait()

    vmem = pl.BlockSpec(memory_space=pltpu.VMEM)
    return pl.pallas_call(
        body, name="rs_sibling",
        out_shape=[jax.ShapeDtypeStruct((nchip, R, n), pack.dtype), jax.ShapeDtypeStruct((N_DEV * small.shape[0], small.shape[1]), small.dtype)],
        in_specs=[pl.BlockSpec(memory_space=pl.ANY), vmem], out_specs=[pl.BlockSpec(memory_space=pl.ANY), vmem],
        scratch_shapes=[pltpu.SemaphoreType.DMA((nchip,)), pltpu.SemaphoreType.DMA((nchip,))] + list(GATHER_SEMS),
    )(pack, small)


def _chip_exchange(part):
    nchip, R, n = part.shape

    def body(p_ref, got_ref, send_sems, recv_sems, local_sem):
        x, y, c = lax.axis_index("x"), lax.axis_index("y"), lax.axis_index("c")
        my_chip = 2 * x + y
        chips = [(1 - x, y), (x, 1 - y), (1 - x, 1 - y)]
        mine = pltpu.make_async_copy(p_ref.at[my_chip], got_ref.at[my_chip], local_sem)
        mine.start()
        sends = []
        for k, (tx, ty) in enumerate(chips):
            cp = pltpu.make_async_remote_copy(src_ref=p_ref.at[2 * tx + ty], dst_ref=got_ref.at[my_chip],
                                              send_sem=send_sems.at[k], recv_sem=recv_sems.at[k],
                                              device_id=(tx, ty, c), device_id_type=MESH)
            cp.start()
            sends.append(cp)
        for k, (sx, sy) in enumerate(chips):
            pltpu.make_async_remote_copy(src_ref=p_ref.at[my_chip], dst_ref=got_ref.at[2 * sx + sy],
                                         send_sem=send_sems.at[k], recv_sem=recv_sems.at[k],
                                         device_id=(sx, sy, c), device_id_type=MESH).wait_recv()
        for cp in sends:
            cp.wait_send()
        mine.wait()

    return pl.pallas_call(
        body, name="rs_chips", out_shape=jax.ShapeDtypeStruct((nchip, R, n), part.dtype),
        in_specs=[pl.BlockSpec(memory_space=pl.ANY)], out_specs=pl.BlockSpec(memory_space=pl.ANY),
        scratch_shapes=[pltpu.SemaphoreType.DMA((3,)), pltpu.SemaphoreType.DMA((3,)), pltpu.SemaphoreType.DMA],
    )(part)


def _add_blocks(pack, from_sib, my_c):
    nb, _, R, n = pack.shape

    def body(c_ref, a_ref, b_ref, o_ref):
        o_ref[...] = (a_ref[0] + b_ref[...]).astype(o_ref.dtype)

    return pl.pallas_call(
        body, name="rs_add", out_shape=jax.ShapeDtypeStruct((nb, R, n), RS_DTYPE),
        grid_spec=pltpu.PrefetchScalarGridSpec(
            num_scalar_prefetch=1, grid=(nb,),
            in_specs=[pl.BlockSpec((1, 1, R, n), lambda i, c: (i, c[0], 0, 0)), pl.BlockSpec((1, R, n), lambda i, c: (i, 0, 0))],
            out_specs=pl.BlockSpec((1, R, n), lambda i, c: (i, 0, 0))),
        compiler_params=_cparams(("parallel",)))(my_c, pack, from_sib)


def _sum_slabs(got, name):
    nb, R, n = got.shape

    def body(g_ref, o_ref):
        tot = g_ref[0].astype(f32)
        for d in range(1, nb):
            tot = tot + g_ref[d].astype(f32)
        o_ref[...] = tot

    return pl.pallas_call(body, name=name, grid=(n // LANES,), out_shape=jax.ShapeDtypeStruct((R, n), f32),
                          in_specs=[pl.BlockSpec((nb, R, LANES), lambda j: (0, 0, j))], out_specs=pl.BlockSpec((R, LANES), lambda j: (0, j)),
                          compiler_params=_cparams(("parallel",)))(got)


def _adamw_all(ws, gs, ms, vs, gain_parts):
    n = len(ws)
    ng = len(GAIN_SLOTS)

    def body(*refs):
        gp_ref = refs[0]
        w_refs = refs[1:1 + n]
        g_refs = refs[1 + n:1 + 2 * n - ng]
        m_refs = refs[1 + 2 * n - ng:1 + 3 * n - ng]
        v_refs = refs[1 + 3 * n - ng:1 + 4 * n - ng]
        outs = refs[1 + 4 * n - ng:]
        gsum_ref, loss_ref = outs[0], outs[1]
        gg_refs = outs[2:2 + ng]
        d_refs, nm_refs, nv_refs = (outs[2 + ng + k * n:2 + ng + (k + 1) * n] for k in range(3))
        tot = gp_ref[0]
        for d in range(1, N_DEV):
            tot = tot + gp_ref[d]
        gsum_ref[...] = tot
        loss_ref[...] = gsum_ref[GAIN_ROWS - 1:GAIN_ROWS, :]
        for k in range(n):
            if k < ng:
                r0, width = GAIN_SLOTS[k]
                for r in range(width // LANES):
                    gg_refs[k][:, r * LANES:(r + 1) * LANES] = gsum_ref[r0 + r:r0 + r + 1, :]
                g = gg_refs[k][...]
            else:
                g = g_refs[k - ng][...]
            w = w_refs[k][...]
            m = ADAM_B1 * m_refs[k][...] + (1.0 - ADAM_B1) * g
            v = ADAM_B2 * v_refs[k][...] + (1.0 - ADAM_B2) * (g * g)
            m_hat = m / (1.0 - ADAM_B1 ** ADAM_STEP)
            v_hat = v / (1.0 - ADAM_B2 ** ADAM_STEP)
            d_refs[k][...] = -ADAM_LR * (m_hat / (jnp.sqrt(v_hat) + ADAM_EPS) + ADAM_WD * w)
            nm_refs[k][...] = m
            nv_refs[k][...] = v

    sds = jax.ShapeDtypeStruct
    like = [sds(w.shape, f32) for w in ws]
    out_shape = [sds((GAIN_ROWS, LANES), f32), sds((1, LANES), f32)] + like[:ng] + like * 3
    full = lambda a: pl.BlockSpec(a.shape, lambda i: (0,) * len(a.shape))
    operands = [gain_parts, *ws, *gs, *ms, *vs]
    outs = pl.pallas_call(
        body, name="adamw", grid=(1,), out_shape=out_shape, in_specs=[full(a) for a in operands],
        out_specs=[full(a) for a in out_shape], compiler_params=_cparams(("arbitrary",)),
    )(*operands)
    loss = outs[1]
    gg = outs[2:2 + ng]
    deltas, new_m, new_v = (outs[2 + ng + k * n:2 + ng + (k + 1) * n] for k in range(3))
    return loss, gg, deltas, new_m, new_v


GAIN_ROWS = 32
GAIN_SLOTS = [(0, 1024), (8, 256), (10, 128), (11, 256), (13, 256), (15, 512), (19, 512), (23, 1024)]

SH_IN = IN_TOTAL // N_DEV
SLAB1_COLS = 384
UQ_ROWS = Q_LORA * (N_HEADS * QK_DIM // N_DEV) // SLAB1_COLS
UKV_ROWS = 48
SLAB1_ROWS = D_MODEL + UQ_ROWS + UKV_ROWS
SH_ROWS = D_MODEL // N_DEV
PL_ROWS = PLE_DIM * (D_MODEL // N_DEV) // D_MODEL
CONV_ROWS = 16
SLAB2_ROWS = 2 * SH_ROWS + PL_ROWS + CONV_ROWS
IN_SEGMENTS = [(0, 448, 0), (448, 960, 512), (1472, 2496, 1024), (960, 1472, 2048), (2496, 3008, 2560)]


def _flat_rows(a, rows, cols):
    lead = a.shape[:-2]
    flat = a.reshape(lead + (-1,))
    pad = [(0, 0)] * len(lead) + [(0, rows * cols - flat.shape[-1])]
    return jnp.pad(flat, pad).reshape(lead + (rows, cols))


def _slab1(in_sh, uq_sh, ukv_sh):
    pad = [(0, 0)] * (in_sh.ndim - 1) + [(0, SLAB1_COLS - in_sh.shape[-1])]
    return jnp.concatenate([jnp.pad(in_sh, pad), _flat_rows(uq_sh, UQ_ROWS, SLAB1_COLS), _flat_rows(ukv_sh, UKV_ROWS, SLAB1_COLS)],
                           axis=-2)


def _slab2(o_sh, plg_sh, pl_sh, conv_sh):
    return jnp.concatenate([o_sh, plg_sh, _flat_rows(pl_sh, PL_ROWS, D_MODEL), _flat_rows(conv_sh, CONV_ROWS, D_MODEL)], axis=-2)


def _in_plan():
    src = [None] * PROJ_PAD
    for a, b, start in IN_SEGMENTS:
        for o in range(a, b):
            src[start + o - a] = (o // SH_IN, o % SH_IN)
    plan = []
    for t in range(PROJ_PAD // LANES):
        runs, j = [], 0
        key_of = lambda jj: None if src[t * LANES + jj] is None else (
            src[t * LANES + jj][0], src[t * LANES + jj][1] // LANES, src[t * LANES + jj][1] % LANES - jj)
        while j < LANES:
            key = key_of(j)
            lo = j
            while j < LANES and key_of(j) == key:
                j += 1
            if key is not None:
                runs.append(key + (lo, j))
        plan.append(runs)
    return plan


def _assemble_w_in(g1):
    plan = _in_plan()
    rows = 256

    def body(g_ref, out_ref):
        src_lane = lax.broadcasted_iota(jnp.int32, (LANES, LANES), 0)
        out_lane = lax.broadcasted_iota(jnp.int32, (LANES, LANES), 1)
        for t, runs in enumerate(plan):
            acc = jnp.zeros((rows, LANES), f32)
            for d, tile, shift, lo, hi in runs:
                pick = ((src_lane == out_lane + shift) & (out_lane >= lo) & (out_lane < hi)).astype(bf16)
                acc = acc + _nn(g_ref[d, :, tile * LANES:(tile + 1) * LANES], pick)
            out_ref[:, t * LANES:(t + 1) * LANES] = acc.astype(out_ref.dtype)

    return pl.pallas_call(
        body, name="w_in_columns", grid=(D_MODEL // rows,), out_shape=jax.ShapeDtypeStruct((D_MODEL, PROJ_PAD), MM),
        in_specs=[pl.BlockSpec((N_DEV, rows, SLAB1_COLS), lambda i: (0, i, 0))],
        out_specs=pl.BlockSpec((rows, PROJ_PAD), lambda i: (i, 0)),
        compiler_params=_cparams(("parallel",)))(g1)


def _by_device(a, width):
    return a.reshape(a.shape[0], N_DEV, width).transpose(1, 0, 2)


def _from_devices(a):
    return a.transpose(1, 0, 2).reshape(a.shape[1], -1)


def kernel(x, p, positions, g_in, w_in, g_cq, w_uq, g_ckv, w_ukv, g_q, g_k, conv_w, g_oa, g_oc, w_o, w_pl, w_plg, g_pl, loss_target, m_g_in, m_w_in, m_g_cq, m_w_uq, m_g_ckv, m_w_ukv, m_g_q, m_g_k, m_conv_w, m_g_oa, m_g_oc, m_w_o, m_w_pl, m_w_plg, m_g_pl, v_g_in, v_w_in, v_g_cq, v_w_uq, v_g_ckv, v_w_ukv, v_g_q, v_g_k, v_conv_w, v_g_oa, v_g_oc, v_w_o, v_w_pl, v_w_plg, v_g_pl):
    S = x.shape[1]
    nd = N_DEV
    xs, ps, tgt = x[0], p[0, 0], loss_target[0]

    slab1 = _slab1(w_in[0], w_uq[0], w_ukv[0]).astype(bf16)
    inv_freq = 1.0 / (ROPE_THETA ** (jnp.arange(0, ROPE_DIM, 2, dtype=f32) / ROPE_DIM))
    inv_row = jnp.concatenate([inv_freq, inv_freq, jnp.zeros((LANES - ROPE_DIM,), f32)]).reshape(1, LANES)
    posf = jnp.broadcast_to(positions[0].astype(f32)[:, None], (S, LANES))
    tabs, g1 = _rope_tables(posf, inv_row, slab1)
    g1 = g1.reshape(nd, SLAB1_ROWS, SLAB1_COLS)
    conv_bits = lax.bitcast_convert_type(conv_w[0], bf16).reshape(CONV_K, -1)
    slab2 = _slab2(w_o[0].astype(bf16), w_plg[0].astype(bf16), w_pl[0].astype(bf16), conv_bits)

    w_in_p = _assemble_w_in(g1)
    wuq = _from_devices(g1[:, D_MODEL:D_MODEL + UQ_ROWS].reshape(nd, Q_LORA, -1)).reshape(Q_LORA, N_HEADS, QK_DIM)
    w_uq_p = jnp.pad(wuq, ((0, 0), (0, 0), (0, HEAD_PAD - QK_DIM))).reshape(Q_LORA, QK_PAD)
    ukv_flat = g1[:, D_MODEL + UQ_ROWS:].reshape(nd, -1)[:, :KV_LORA * V_DIM]
    wukv = _from_devices(ukv_flat.reshape(nd, KV_LORA, V_DIM)).reshape(KV_LORA, N_HEADS, 2, NOPE_DIM)
    w_ukv_p = wukv.transpose(0, 2, 1, 3).reshape(KV_LORA, 2 * ATTN_WIDTH)
    w_in_p, w_uq_p, w_ukv_p = (w.astype(MM) for w in (w_in_p, w_uq_p, w_ukv_p))

    g_q_p = jnp.pad(g_q, ((0, 0), (0, HEAD_PAD - QK_DIM)))
    g_k_p = jnp.pad(g_k, ((0, 0), (0, HEAD_PAD - QK_DIM)))

    proj, q, k, v = _fwd_proj(xs, tabs, g_in, w_in_p, g_cq, w_uq_p, g_ckv, w_ukv_p, g_q_p * SCALE_LOG2E, g_k_p)
    o, lse, g2 = _flash_fwd(q, k, v, slab2)
    w_o_f = g2[:, 0:SH_ROWS].reshape(D_MODEL, D_MODEL).astype(MM)
    w_plg_f = g2[:, SH_ROWS:2 * SH_ROWS].reshape(D_MODEL, D_MODEL).astype(MM)
    w_pl_f = _from_devices(g2[:, 2 * SH_ROWS:2 * SH_ROWS + PL_ROWS].reshape(nd, PLE_DIM, -1)).astype(MM)
    cv_bits = g2[:, 2 * SH_ROWS + PL_ROWS:].reshape(nd, -1)[:, :CONV_K * (CONV_WIDTH // nd) * 2]
    conv_full = _from_devices(lax.bitcast_convert_type(cv_bits.reshape(nd, CONV_K, CONV_WIDTH // nd, 2), f32))
    conv_w8 = jnp.pad(conv_full, ((0, 8 - CONV_K), (0, 0)))
    (d_x2, d_o, delta, d_za, d_cbzc, d_u, dw_o, dw_plg, dw_pl, dcw, dg_oa, dg_oc, dg_pl, loss_part) = _mid(
        o, proj, xs, ps, tgt, w_o_f, w_plg_f, w_pl_f, conv_w8, g_oa, g_oc, g_pl)

    slabs_a = _slab2(dw_o.reshape(nd, SH_ROWS, D_MODEL), dw_plg.reshape(nd, SH_ROWS, D_MODEL), _by_device(dw_pl, D_MODEL // nd),
                     _by_device(dcw[:CONV_K], CONV_WIDTH // nd)).astype(RS_DTYPE)
    d_q, d_k, d_v, got_a = _flash_bwd(q, k, v, d_o, lse, delta, slabs_a)
    shard_a = _sum_slabs(got_a, "rs_sum8")
    (grad_x, dw_in_p, dw_uq_p, dw_ukv_p, dg_in, dg_cq, dg_ckv, dg_q_p, dg_k_p) = _bwd_proj(
        xs, proj, d_q, d_k, d_v, d_za, d_cbzc, d_u, d_x2, tabs, g_in, w_in_p, g_cq, w_uq_p, g_ckv, w_ukv_p, g_q_p, g_k_p, conv_w8)

    def dw_in_cols(a, b):
        return [dw_in_p[:, start + max(a, s) - s:start + min(b, e) - s] for s, e, start in sorted(IN_SEGMENTS) if max(a, s) < min(b, e)]

    no_cols = jnp.zeros((D_MODEL, SLAB1_COLS - SH_IN), f32)
    in_shards = jnp.stack([jnp.concatenate(dw_in_cols(SH_IN * d, SH_IN * (d + 1)) + [no_cols], axis=1) for d in range(nd)])
    dw_uq = dw_uq_p.reshape(Q_LORA, N_HEADS, HEAD_PAD)[:, :, :QK_DIM].reshape(Q_LORA, N_HEADS * QK_DIM)
    dw_ukv = dw_ukv_p.reshape(KV_LORA, 2, N_HEADS, NOPE_DIM).transpose(0, 2, 1, 3).reshape(KV_LORA, 2 * ATTN_WIDTH)
    slabs_b = _slab1(in_shards, _by_device(dw_uq, N_HEADS * QK_DIM // nd), _by_device(dw_ukv, 2 * ATTN_WIDTH // nd))
    slabs_b = slabs_b.reshape(4, 2, SLAB1_ROWS, SLAB1_COLS)
    my_c = lax.axis_index("c").astype(jnp.int32).reshape(1)
    gains_part = [dg_in, dg_cq, dg_ckv, dg_q_p, dg_k_p, dg_oa, dg_oc, dg_pl]
    gflat = jnp.concatenate([g.reshape(-1) for g in gains_part] + [loss_part.reshape(-1)])
    from_sib, gain_parts = _sibling_exchange(slabs_b, gflat.reshape(GAIN_ROWS, LANES))
    gain_parts = gain_parts.reshape(nd, GAIN_ROWS, LANES)
    chip_part = _add_blocks(slabs_b, from_sib, my_c)
    shard_b = _sum_slabs(_chip_exchange(chip_part), "rs_sum")
    grads_w = [shard_b[:D_MODEL, :SH_IN], shard_b[D_MODEL:D_MODEL + UQ_ROWS].reshape(Q_LORA, -1),
               shard_b[D_MODEL + UQ_ROWS:].reshape(-1)[:KV_LORA * V_DIM].reshape(KV_LORA, V_DIM),
               shard_a[2 * SH_ROWS + PL_ROWS:].reshape(-1)[:CONV_K * CONV_WIDTH // nd].reshape(CONV_K, -1),
               shard_a[0:SH_ROWS], shard_a[2 * SH_ROWS:2 * SH_ROWS + PL_ROWS].reshape(PLE_DIM, -1), shard_a[SH_ROWS:2 * SH_ROWS]]

    gains = [g_in, g_cq, g_ckv, g_q_p, g_k_p, g_oa, g_oc, g_pl]
    padq = lambda a: jnp.pad(a, ((0, 0), (0, HEAD_PAD - QK_DIM)))
    m_gains = [m_g_in, m_g_cq, m_g_ckv, padq(m_g_q), padq(m_g_k), m_g_oa, m_g_oc, m_g_pl]
    v_gains = [v_g_in, v_g_cq, v_g_ckv, padq(v_g_q), padq(v_g_k), v_g_oa, v_g_oc, v_g_pl]
    tr = lambda a: jnp.swapaxes(a, 1, 2)
    turned = lambda l: [tr(l[0]), tr(l[1])] + list(l[2:])
    ws = gains + turned([w_in, w_uq, w_ukv, conv_w, w_o, w_pl, w_plg])
    ms = m_gains + turned([m_w_in, m_w_uq, m_w_ukv, m_conv_w, m_w_o, m_w_pl, m_w_plg])
    vs = v_gains + turned([v_w_in, v_w_uq, v_w_ukv, v_conv_w, v_w_o, v_w_pl, v_w_plg])
    loss_row, gg, deltas, new_m, new_v = _adamw_all(ws, turned([g[None] for g in grads_w]), ms, vs, gain_parts)
    grads_w = [g[None] for g in grads_w]
    ng = len(gains)
    deltas, new_m, new_v = (list(l[:ng]) + turned(l[ng:]) for l in (deltas, new_m, new_v))
    loss = loss_row[0, 0]

    def ordered(gl, wl):
        g_in_, g_cq_, g_ckv_, g_q_, g_k_, g_oa_, g_oc_, g_pl_ = gl
        g_q_, g_k_ = g_q_[:, :QK_DIM], g_k_[:, :QK_DIM]
        w_in_, w_uq_, w_ukv_, cw_, w_o_, w_pl_, w_plg_ = wl
        return [g_in_, w_in_, g_cq_, w_uq_, g_ckv_, w_ukv_, g_q_, g_k_, cw_, g_oa_, g_oc_, w_o_, w_pl_, w_plg_, g_pl_]

    ng = len(gains)
    outs = [loss, grad_x[None]]
    outs += ordered(gg, grads_w)
    for lst in (deltas, new_m, new_v):
        outs += ordered(lst[:ng], lst[ng:])
    return tuple(outs)
```

```python
import math

import jax
import jax.numpy as jnp
from jax import lax
from jax.experimental import pallas as pl
from jax.experimental.pallas import tpu as pltpu

f32 = jnp.float32
bf16 = jnp.bfloat16
MM = jnp.bfloat16
RS_DTYPE = jnp.bfloat16

D_MODEL = 1024
PLE_DIM = 256
N_HEADS = 4
NOPE_DIM = 128
ROPE_DIM = 64
V_DIM = 128
QK_DIM = NOPE_DIM + ROPE_DIM
Q_LORA = 256
KV_LORA = 128
ATTN_WIDTH = N_HEADS * V_DIM
CONV_WIDTH = D_MODEL - ATTN_WIDTH
CONV_K = 3
ROPE_THETA = 10000.0
RMS_EPS = 1e-6
NEG_INF = -1e30
IN_TOTAL = Q_LORA + KV_LORA + ROPE_DIM + ATTN_WIDTH + 4 * CONV_WIDTH
ADAM_LR = 0.001
ADAM_B1 = 0.9
ADAM_B2 = 0.999
ADAM_EPS = 1e-08
ADAM_WD = 0.01
ADAM_STEP = 10

N_DEV = 8
LANES = 128
HEAD_PAD = 256
QK_PAD = N_HEADS * HEAD_PAD
PROJ_PAD = 3072
TS = 512
TS_M = 256
TS_E = 256
TQ = 512
VMEM_LIMIT = 58 * 1024 * 1024
MESH = pl.DeviceIdType.MESH
QK_SCALE = 1.0 / math.sqrt(QK_DIM)
SCALE_LOG2E = QK_SCALE * math.log2(math.e)
DK_SCALE = QK_SCALE / SCALE_LOG2E


def _cparams(sem=None):
    return pltpu.CompilerParams(dimension_semantics=sem, vmem_limit_bytes=VMEM_LIMIT)


def _nt(a, b):
    return lax.dot_general(a, b, (((1,), (1,)), ((), ())), preferred_element_type=f32)


def _tn(a, b):
    return lax.dot_general(a, b, (((0,), (0,)), ((), ())), preferred_element_type=f32)


def _nn(a, b):
    return jnp.dot(a, b, preferred_element_type=f32)


def _rstd(x, n):
    return lax.rsqrt(jnp.sum(x * x, axis=-1, keepdims=True) * (1.0 / n) + RMS_EPS)


def _rms_bwd(dy, xhat, r, g, n):
    u = dy * g
    return r * (u - xhat * (jnp.sum(u * xhat, axis=-1, keepdims=True) * (1.0 / n)))


def _rope(t, c, s1, s2):
    return t * c + pltpu.roll(t, 32, 1) * s1 + pltpu.roll(t, 96, 1) * s2


def _rope_t(d, c, s1, s2):
    return d * c - pltpu.roll(d, 96, 1) * s2 - pltpu.roll(d, 32, 1) * s1


def _alternate(*stages):
    stages = list(stages)
    while stages:
        for st in list(stages):
            if next(st, "done") == "done":
                stages.remove(st)


def _const(shape):
    return pl.BlockSpec(shape, lambda *_: (0,) * len(shape), pipeline_mode=pl.Buffered(1))


def _two_level_gather(x_ref, out_ref, send_sems, recv_sems, local_sem):
    m_per = x_ref.shape[0]
    x, y, c = lax.axis_index("x"), lax.axis_index("y"), lax.axis_index("c")
    me, sibling = (x, y, c), (x, y, 1 - c)
    chips = [(1 - x, y), (x, 1 - y), (1 - x, 1 - y)]

    def rows(px, py, pc):
        return out_ref.at[pl.ds((4 * px + 2 * py + pc) * m_per, m_per), :]

    def copy(k, blk, to, src=None):
        return pltpu.make_async_remote_copy(
            src_ref=rows(*blk) if src is None else src, dst_ref=rows(*blk),
            send_sem=send_sems.at[k], recv_sem=recv_sems.at[k], device_id=to, device_id_type=MESH)

    mine = pltpu.make_async_copy(x_ref, rows(*me), local_sem)
    first = [copy(0, me, sibling, src=x_ref)] + [copy(1 + j, me, (*chip, c), src=x_ref) for j, chip in enumerate(chips)]
    passed = [copy(4 + j, (*chip, c), sibling) for j, chip in enumerate(chips)]

    def start():
        mine.start()
        for cp in first:
            cp.start()

    def finish():
        for j, chip in enumerate(chips):
            copy(1 + j, (*chip, c), me).wait_recv()
            passed[j].start()
        copy(0, sibling, me).wait_recv()
        for j, chip in enumerate(chips):
            copy(4 + j, (*chip, 1 - c), me).wait_recv()
        for cp in first + passed:
            cp.wait_send()
        mine.wait()

    return start, finish


GATHER_SEMS = [pltpu.SemaphoreType.DMA((7,)), pltpu.SemaphoreType.DMA((7,)), pltpu.SemaphoreType.DMA]


def _rope_tables(posf, inv_freq, block):
    S = posf.shape[0]
    t = min(S, TS)
    nt = S // t

    def body(pos_ref, f_ref, blk_ref, c_ref, s1_ref, s2_ref, all_ref, send_sems, recv_sems, local_sem):
        i = pl.program_id(0)
        start, finish = _two_level_gather(blk_ref, all_ref, send_sems, recv_sems, local_sem)

        @pl.when(i == 0)
        def _():
            start()

        ang = pos_ref[...] * f_ref[...]
        lane = lax.broadcasted_iota(jnp.int32, ang.shape, 1)
        cs, sn = jnp.cos(ang), jnp.sin(ang)
        c_ref[...] = jnp.where(lane < ROPE_DIM, cs, 0.0)
        s1_ref[...] = jnp.where((lane >= ROPE_DIM // 2) & (lane < ROPE_DIM), sn, 0.0)
        s2_ref[...] = jnp.where(lane < ROPE_DIM // 2, -sn, 0.0)

        @pl.when(i == nt - 1)
        def _():
            finish()

    spec = pl.BlockSpec((t, LANES), lambda i: (i, 0))
    hbm = pl.BlockSpec(memory_space=pl.ANY)
    outs = pl.pallas_call(
        body, name="rope_tables", grid=(nt,),
        out_shape=[jax.ShapeDtypeStruct((S, LANES), f32)] * 3 + [jax.ShapeDtypeStruct((N_DEV * block.shape[0], block.shape[1]), block.dtype)],
        in_specs=[spec, _const((1, LANES)), hbm], out_specs=[spec] * 3 + [hbm],
        scratch_shapes=list(GATHER_SEMS),
        compiler_params=_cparams(("arbitrary",)),
    )(posf, inv_freq, block)
    return outs[:3], outs[3]


def _fwd_proj(x, tabs, g_in, w_in_p, g_cq, w_uq_p, g_ckv, w_ukv_p, g_q_p, g_k_p):
    S = x.shape[0]
    T = min(TS, S)
    nt = S // T
    LAT = Q_LORA + KV_LORA + LANES
    PIECE = 512

    def body(x_ref, c_ref, s1_ref, s2_ref, g_in_ref, w_in_ref, g_cq_ref, w_uq_ref, g_ckv_ref, w_ukv_ref,
             g_q_ref, g_k_ref, proj_ref, q_ref, k_ref, v_ref, lat_a, lat_b):
        i = pl.program_id(0)

        @pl.when(i == 0)
        def _():
            lat_b[...] = jnp.zeros(lat_b.shape, f32)

        def stage1(lat):
            xv = x_ref[...]
            h = (xv * _rstd(xv, D_MODEL) * g_in_ref[...]).astype(MM)
            yield
            for cb in range(PROJ_PAD // PIECE):
                cols = slice(cb * PIECE, (cb + 1) * PIECE)
                proj_ref[:, cols] = _nn(h, w_in_ref[:, cols])
                yield
            lat[...] = proj_ref[:, 0:LAT]

        def stage2(lat):
            c, s1, s2 = c_ref[...], s1_ref[...], s2_ref[...]
            g_q, g_k = g_q_ref[...], g_k_ref[...]
            c_q = lat[:, 0:Q_LORA]
            n_cq = (c_q * _rstd(c_q, Q_LORA) * g_cq_ref[...]).astype(MM)
            q_raw = _nn(n_cq, w_uq_ref[...])
            for hd in range(N_HEADS):
                qh = q_raw[:, hd * HEAD_PAD:(hd + 1) * HEAD_PAD]
                qn = qh * _rstd(qh, QK_DIM) * g_q
                q_ref[:, hd * HEAD_PAD:hd * HEAD_PAD + LANES] = qn[:, :LANES].astype(MM)
                q_ref[:, hd * HEAD_PAD + LANES:(hd + 1) * HEAD_PAD] = _rope(qn[:, LANES:], c, s1, s2).astype(MM)
                yield

            c_kv = lat[:, Q_LORA:Q_LORA + KV_LORA]
            n_ckv = (c_kv * _rstd(c_kv, KV_LORA) * g_ckv_ref[...]).astype(MM)
            kv_raw = _nn(n_ckv, w_ukv_ref[...])
            kpe = lat[:, Q_LORA + KV_LORA:LAT]
            ss_pe = jnp.sum(kpe * kpe, axis=-1, keepdims=True)
            kr = _rope(kpe * g_k[:, LANES:], c, s1, s2)
            for hd in range(N_HEADS):
                kn = kv_raw[:, hd * LANES:(hd + 1) * LANES]
                rk = lax.rsqrt((jnp.sum(kn * kn, axis=-1, keepdims=True) + ss_pe) * (1.0 / QK_DIM) + RMS_EPS)
                k_ref[:, hd * HEAD_PAD:hd * HEAD_PAD + LANES] = (kn * rk * g_k[:, :LANES]).astype(MM)
                k_ref[:, hd * HEAD_PAD + LANES:(hd + 1) * HEAD_PAD] = (kr * rk).astype(MM)
                yield
            v_ref[...] = kv_raw[:, N_HEADS * NOPE_DIM:].astype(MM)

        @pl.when(i % 2 == 0)
        def _():
            _alternate(stage1(lat_a), stage2(lat_b))

        @pl.when(i % 2 == 1)
        def _():
            _alternate(stage1(lat_b), stage2(lat_a))

    cur = lambda w: pl.BlockSpec((T, w), lambda i: (jnp.minimum(i, nt - 1), 0))
    prv = lambda w: pl.BlockSpec((T, w), lambda i: (jnp.maximum(i - 1, 0), 0))
    return pl.pallas_call(
        body, name="fwd_proj", grid=(nt + 1,),
        out_shape=[jax.ShapeDtypeStruct((S, PROJ_PAD), f32), jax.ShapeDtypeStruct((S, QK_PAD), MM),
                   jax.ShapeDtypeStruct((S, QK_PAD), MM), jax.ShapeDtypeStruct((S, ATTN_WIDTH), MM)],
        in_specs=[cur(D_MODEL), prv(LANES), prv(LANES), prv(LANES), _const((1, D_MODEL)), _const((D_MODEL, PROJ_PAD)),
                  _const((1, Q_LORA)), _const((Q_LORA, QK_PAD)), _const((1, KV_LORA)), _const((KV_LORA, 2 * ATTN_WIDTH)),
                  _const((1, HEAD_PAD)), _const((1, HEAD_PAD))],
        out_specs=[cur(PROJ_PAD), prv(QK_PAD), prv(QK_PAD), prv(ATTN_WIDTH)],
        scratch_shapes=[pltpu.VMEM((T, LAT), f32), pltpu.VMEM((T, LAT), f32)],
        compiler_params=_cparams(("arbitrary",)),
    )(x, *tabs, g_in, w_in_p, g_cq, w_uq_p, g_ckv, w_ukv_p, g_q_p, g_k_p)


def _exchange(src_of, dst_ref, send_sems, recv_sems, local_sem):
    x, y, c = lax.axis_index("x"), lax.axis_index("y"), lax.axis_index("c")
    me = 4 * x + 2 * y + c
    remote = []
    for k in range(1, N_DEV):
        px = 1 - x if (k >> 2) & 1 else x
        py = 1 - y if (k >> 1) & 1 else y
        pc = 1 - c if k & 1 else c
        remote.append(pltpu.make_async_remote_copy(
            src_ref=src_of(4 * px + 2 * py + pc), dst_ref=dst_ref.at[me], send_sem=send_sems.at[k - 1],
            recv_sem=recv_sems.at[k - 1], device_id=(px, py, pc), device_id_type=MESH))
    return remote, pltpu.make_async_copy(src_of(me), dst_ref.at[me], local_sem)


def _exchange_start(copies):
    remote, local = copies
    local.start()
    for cp in remote:
        cp.start()


def _exchange_wait(copies):
    remote, local = copies
    for cp in remote:
        cp.wait_recv()
    for cp in remote:
        cp.wait_send()
    local.wait()


EXCHANGE_SEMS = [pltpu.SemaphoreType.DMA((N_DEV - 1,)), pltpu.SemaphoreType.DMA((N_DEV - 1,)), pltpu.SemaphoreType.DMA]


def _flash_fwd(q, k, v, block):
    S = q.shape[0]
    T = min(TQ, S)
    nq = S // T

    def body(q_ref, k_ref, v_ref, blk_ref, o_ref, lse_ref, all_ref, s_a, s_b, m_sc, l_sc, acc_sc, send_sems, recv_sems, local_sem):
        n = pl.program_id(1)
        gather = _exchange(lambda d: blk_ref, all_ref, send_sems, recv_sems, local_sem)

        m_sc[...] = jnp.full(m_sc.shape, NEG_INF, f32)
        l_sc[...] = jnp.zeros(l_sc.shape, f32)
        acc_sc[...] = jnp.zeros(acc_sc.shape, f32)
        qv = q_ref[...]

        def rows(j):
            return pl.ds(pl.multiple_of(j * T, T), T)

        def scores(j, dst):
            dst[...] = _nt(k_ref[rows(j), :], qv)

        def update(src, j, masked):
            s = src[...]
            if masked:
                k_i = lax.broadcasted_iota(jnp.int32, (T, T), 0)
                q_i = lax.broadcasted_iota(jnp.int32, (T, T), 1)
                s = jnp.where(k_i <= q_i, s, NEG_INF)
            m_prev = m_sc[...]
            m_new = jnp.maximum(m_prev, jnp.max(s, axis=0, keepdims=True))
            alpha = jnp.exp2(m_prev - m_new)
            p = jnp.exp2(s - m_new[0:1, :])
            l_sc[...] = alpha * l_sc[...] + jnp.sum(p, axis=0, keepdims=True)
            acc_sc[...] = alpha[0:1, :] * acc_sc[...] + _tn(v_ref[rows(j), :], p.astype(MM))
            m_sc[...] = m_new

        scores(0, s_a)

        def pair(t, carry):
            j = 2 * t
            scores(j + 1, s_b)
            update(s_a, j, False)
            scores(j + 2, s_a)
            update(s_b, j + 1, False)
            return carry

        lax.fori_loop(0, n // 2, pair, 0)

        @pl.when(n % 2 == 0)
        def _():
            update(s_a, n, True)

        @pl.when(n % 2 == 1)
        def _():
            scores(n, s_b)
            update(s_a, n - 1, False)
            update(s_b, n, True)

        o_ref[...] = (acc_sc[...] / l_sc[0:1, :]).T
        lse_ref[...] = m_sc[...] + jnp.log2(l_sc[...])

        @pl.when((pl.program_id(0) == 0) & (n == 0))
        def _():
            _exchange_start(gather)

        @pl.when((pl.program_id(0) == N_HEADS - 1) & (n == nq - 1))
        def _():
            _exchange_wait(gather)

    return pl.pallas_call(
        body, name="flash_fwd", grid=(N_HEADS, nq),
        out_shape=[jax.ShapeDtypeStruct((S, ATTN_WIDTH), f32), jax.ShapeDtypeStruct((N_HEADS * 8, S), f32),
                   jax.ShapeDtypeStruct((N_DEV,) + block.shape, block.dtype)],
        in_specs=[pl.BlockSpec((T, HEAD_PAD), lambda h, i: (i, h)),
                  pl.BlockSpec((S, HEAD_PAD), lambda h, i: (0, h)),
                  pl.BlockSpec((S, V_DIM), lambda h, i: (0, h)),
                  pl.BlockSpec(memory_space=pl.ANY)],
        out_specs=[pl.BlockSpec((T, V_DIM), lambda h, i: (i, h)), pl.BlockSpec((8, T), lambda h, i: (h, i)),
                   pl.BlockSpec(memory_space=pl.ANY)],
        scratch_shapes=[pltpu.VMEM((T, T), f32), pltpu.VMEM((T, T), f32), pltpu.VMEM((8, T), f32), pltpu.VMEM((8, T), f32),
                        pltpu.VMEM((V_DIM, T), f32)] + EXCHANGE_SEMS,
        compiler_params=_cparams(("arbitrary", "arbitrary")),
    )(q, k, v, block)


def _shift_down(m, prev8, n):
    T = m.shape[0]
    rows = lax.broadcasted_iota(jnp.int32, m.shape, 0)
    head = jnp.tile(pltpu.roll(prev8, n, 0), (T // 8, 1))
    return jnp.where(rows >= n, pltpu.roll(m, n, 0), head)


def _shift_up(m, next8, n):
    T = m.shape[0]
    rows = lax.broadcasted_iota(jnp.int32, m.shape, 0)
    tail = jnp.tile(pltpu.roll(next8, 8 - n, 0), (T // 8, 1))
    return jnp.where(rows < T - n, pltpu.roll(m, T - n, 0), tail)


def _mid(o, proj, x, p, target, w_o, w_plg, w_pl, conv_w8, g_oa, g_oc, g_pl):
    S = x.shape[0]
    T = min(TS_M, S)
    nt = S // T

    def body(o_ref, za_ref, ccx_ref, cbzc_ref, prev_ref, x_ref, p_ref, t_ref, wo_ref, wplg_ref, wpl_ref, cw_ref,
             goa_ref, goc_ref, gpl_ref,
             dx2_ref, do_ref, delta_ref, dza_ref, dcbzc_ref, du_ref,
             dwo_ref, dwplg_ref, dwpl_ref, dcw_ref, dgoa_ref, dgoc_ref, dgpl_ref, loss_ref):
        i = pl.program_id(0)

        @pl.when(i == 0)
        def _():
            for r in (dwo_ref, dwplg_ref, dwpl_ref, dcw_ref, dgoa_ref, dgoc_ref, dgpl_ref, loss_ref):
                r[...] = jnp.zeros(r.shape, f32)

        o_v = o_ref[...]
        z_a = za_ref[...]
        cc, cx = ccx_ref[:, :CONV_WIDTH], ccx_ref[:, CONV_WIDTH:]
        cb, z_c = cbzc_ref[:, :CONV_WIDTH], cbzc_ref[:, CONV_WIDTH:]
        g_oa, g_oc, g_pl = goa_ref[...], goc_ref[...], gpl_ref[...]
        w0, w1, w2 = cw_ref[0:1, :], cw_ref[1:2, :], cw_ref[2:3, :]
        p_b = p_ref[...].astype(MM)
        pw = _nn(p_b, wpl_ref[...])

        sa = jax.nn.sigmoid(z_a)
        silu_a = z_a * sa
        ga = o_v * silu_a
        ra = _rstd(ga, ATTN_WIDTH)
        gha = ga * ra
        ya = (gha * g_oa).astype(MM)
        x2 = x_ref[...] + _nn(ya, wo_ref[:ATTN_WIDTH, :])
        m0 = cc * cx
        prev = prev_ref[:, :CONV_WIDTH] * prev_ref[:, CONV_WIDTH:] * (i > 0).astype(f32)
        m1 = _shift_down(m0, prev, 1)
        m2 = _shift_down(m0, prev, 2)
        u = w0 * m2 + w1 * m1 + w2 * m0
        sc = jax.nn.sigmoid(z_c)
        silu_c = z_c * sc
        gc = cb * u * silu_c
        rc = _rstd(gc, CONV_WIDTH)
        ghc = gc * rc
        yc = (ghc * g_oc).astype(MM)
        ycat = jnp.concatenate([ya, yc], axis=-1)
        x2 = x2 + _nn(yc, wo_ref[ATTN_WIDTH:, :])
        r2 = _rstd(x2, D_MODEL)
        xh2 = x2 * r2
        n2 = (xh2 * g_pl).astype(MM)
        gate = jax.nn.sigmoid(_nn(n2, wplg_ref[...]))
        err = x2 + gate * pw - t_ref[...]
        loss_ref[...] += jnp.sum(jnp.sum(err * err, axis=-1, keepdims=True), axis=0, keepdims=True) * (0.5 / D_MODEL)
        d_out = err * (1.0 / D_MODEL)
        d_glog = (d_out * pw * gate * (1.0 - gate)).astype(MM)
        d_n2 = _nt(d_glog, wplg_ref[...])
        dwpl_ref[...] += _tn(p_b, (d_out * gate).astype(MM))
        dwplg_ref[...] += _tn(n2, d_glog)
        dgpl_ref[...] += jnp.sum(d_n2 * xh2, axis=0, keepdims=True)
        d_x2 = d_out + _rms_bwd(d_n2, xh2, r2, g_pl, D_MODEL)
        dx2_ref[...] = d_x2
        d_x2b = d_x2.astype(MM)
        d_ycat = _nt(d_x2b, wo_ref[...])
        dwo_ref[...] += _tn(ycat, d_x2b)
        d_ya, d_yc = d_ycat[:, :ATTN_WIDTH], d_ycat[:, ATTN_WIDTH:]
        dgoa_ref[...] += jnp.sum(d_ya * gha, axis=0, keepdims=True)
        dgoc_ref[...] += jnp.sum(d_yc * ghc, axis=0, keepdims=True)
        d_ga = _rms_bwd(d_ya, gha, ra, g_oa, ATTN_WIDTH)
        d_gc = _rms_bwd(d_yc, ghc, rc, g_oc, CONV_WIDTH)
        d_o = d_ga * silu_a
        do_ref[...] = d_o.astype(MM)
        dza_ref[...] = (d_ga * o_v * (sa * (1.0 + z_a * (1.0 - sa)))).astype(MM)
        for hd in range(N_HEADS):
            cols = slice(hd * V_DIM, (hd + 1) * V_DIM)
            dl = jnp.sum(d_o[:, cols] * o_v[:, cols], axis=-1, keepdims=True)
            delta_ref[hd * 8:(hd + 1) * 8, :] = jnp.broadcast_to(dl, (T, LANES)).T[0:8, :]
        d_u = d_gc * cb * silu_c
        du_ref[...] = d_u
        dcbzc_ref[:, :CONV_WIDTH] = (d_gc * u * silu_c).astype(MM)
        dcbzc_ref[:, CONV_WIDTH:] = (d_gc * cb * u * (sc * (1.0 + z_c * (1.0 - sc)))).astype(MM)
        dcw_ref[0:1, :] += jnp.sum(d_u * m2, axis=0, keepdims=True)
        dcw_ref[1:2, :] += jnp.sum(d_u * m1, axis=0, keepdims=True)
        dcw_ref[2:3, :] += jnp.sum(d_u * m0, axis=0, keepdims=True)

    row = lambda w, blk=0: pl.BlockSpec((T, w), lambda i: (i, blk))
    acc = lambda shape: pl.BlockSpec(shape, lambda i: (0, 0))
    tb = T // 8
    sds = jax.ShapeDtypeStruct
    return pl.pallas_call(
        body, name="mid", grid=(nt,),
        out_shape=[sds((S, D_MODEL), f32), sds((S, ATTN_WIDTH), MM), sds((N_HEADS * 8, S), f32), sds((S, ATTN_WIDTH), MM),
                   sds((S, 2 * CONV_WIDTH), MM), sds((S, CONV_WIDTH), f32),
                   sds((D_MODEL, D_MODEL), f32), sds((D_MODEL, D_MODEL), f32), sds((PLE_DIM, D_MODEL), f32), sds((8, CONV_WIDTH), f32),
                   sds((1, ATTN_WIDTH), f32), sds((1, CONV_WIDTH), f32), sds((1, D_MODEL), f32), sds((1, LANES), f32)],
        in_specs=[row(ATTN_WIDTH), row(ATTN_WIDTH, 1), row(2 * CONV_WIDTH, 1), row(2 * CONV_WIDTH, 2),
                  pl.BlockSpec((8, 2 * CONV_WIDTH), lambda i: (jnp.maximum(i * tb - 1, 0), 1)),
                  row(D_MODEL), row(PLE_DIM), row(D_MODEL),
                  _const((D_MODEL, D_MODEL)), _const((D_MODEL, D_MODEL)), _const((PLE_DIM, D_MODEL)), _const((8, CONV_WIDTH)),
                  _const((1, ATTN_WIDTH)), _const((1, CONV_WIDTH)), _const((1, D_MODEL))],
        out_specs=[row(D_MODEL), row(ATTN_WIDTH), pl.BlockSpec((N_HEADS * 8, T), lambda i: (0, i)), row(ATTN_WIDTH),
                   row(2 * CONV_WIDTH), row(CONV_WIDTH),
                   acc((D_MODEL, D_MODEL)), acc((D_MODEL, D_MODEL)), acc((PLE_DIM, D_MODEL)), acc((8, CONV_WIDTH)),
                   acc((1, ATTN_WIDTH)), acc((1, CONV_WIDTH)), acc((1, D_MODEL)), acc((1, LANES))],
        compiler_params=_cparams(("arbitrary",)),
    )(o, proj, proj, proj, proj, x, p, target, w_o, w_plg, w_pl, conv_w8, g_oa, g_oc, g_pl)


def _flash_bwd(q, k, v, d_o, lse2, delta, by_target):
    S = q.shape[0]
    T = min(TQ, S)
    nq = S // T

    def body(k_ref, v_ref, q_ref, do_ref, lse_ref, delta_ref, out_ref, dq_ref, dk_ref, dv_ref, in_ref, s_a, dp_a, s_b, dp_b,
             send_sems, recv_sems, local_sem):
        step = pl.program_id(1)
        j = nq - 1 - step
        scatter = _exchange(lambda d: out_ref.at[d], in_ref, send_sems, recv_sems, local_sem)

        @pl.when(step == 0)
        def _():
            dq_ref[...] = jnp.zeros(dq_ref.shape, f32)

        dk_ref[...] = jnp.zeros(dk_ref.shape, f32)
        dv_ref[...] = jnp.zeros(dv_ref.shape, f32)
        kv, vv = k_ref[...], v_ref[...]

        def rows_of(i):
            return pl.ds(pl.multiple_of(i * T, T), T)

        def scores(i, s_dst, dp_dst):
            rows = rows_of(i)
            s_dst[...] = _nt(kv, q_ref[rows, :])
            dp_dst[...] = _nt(vv, do_ref[rows, :])

        def grads(i, s_src, dp_src, masked):
            rows = rows_of(i)
            s = s_src[...]
            if masked:
                k_i = lax.broadcasted_iota(jnp.int32, (T, T), 0)
                q_i = lax.broadcasted_iota(jnp.int32, (T, T), 1)
                s = jnp.where(k_i <= q_i, s, NEG_INF)
            p = jnp.exp2(s - lse_ref[0:1, rows])
            ds = (p * (dp_src[...] - delta_ref[0:1, rows])).astype(MM)
            dv_ref[...] += _nn(p.astype(MM), do_ref[rows, :])
            dk_ref[...] += _nn(ds, q_ref[rows, :])
            dq_ref[rows, :] += _tn(ds, kv)

        last = nq - 1
        scores(j, s_a, dp_a)
        scores(jnp.minimum(j + 1, last), s_b, dp_b)
        grads(j, s_a, dp_a, True)
        rest = last - j
        full = jnp.maximum(rest - 1, 0) // 2

        def pair(t, carry):
            u = j + 1 + 2 * t
            scores(u + 1, s_a, dp_a)
            grads(u, s_b, dp_b, False)
            scores(u + 2, s_b, dp_b)
            grads(u + 1, s_a, dp_a, False)
            return carry

        lax.fori_loop(0, full, pair, 0)
        left = rest - 2 * full

        @pl.when(left == 2)
        def _():
            scores(last, s_a, dp_a)
            grads(last - 1, s_b, dp_b, False)
            grads(last, s_a, dp_a, False)

        @pl.when(left == 1)
        def _():
            grads(last, s_b, dp_b, False)

        @pl.when((pl.program_id(0) == 0) & (step == 0))
        def _():
            _exchange_start(scatter)

        @pl.when((pl.program_id(0) == N_HEADS - 1) & (step == nq - 1))
        def _():
            _exchange_wait(scatter)

    whole = lambda w: pl.BlockSpec((S, w), lambda h, j: (0, h))
    stat = pl.BlockSpec((8, S), lambda h, j: (h, 0))
    blk = lambda w: pl.BlockSpec((T, w), lambda h, step: (nq - 1 - step, h))
    return pl.pallas_call(
        body, name="flash_bwd", grid=(N_HEADS, nq),
        out_shape=[jax.ShapeDtypeStruct((S, QK_PAD), f32), jax.ShapeDtypeStruct((S, QK_PAD), f32),
                   jax.ShapeDtypeStruct((S, ATTN_WIDTH), f32), jax.ShapeDtypeStruct(by_target.shape, by_target.dtype)],
        in_specs=[blk(HEAD_PAD), blk(V_DIM), whole(HEAD_PAD), whole(V_DIM), stat, stat, pl.BlockSpec(memory_space=pl.ANY)],
        out_specs=[whole(HEAD_PAD), blk(HEAD_PAD), blk(V_DIM), pl.BlockSpec(memory_space=pl.ANY)],
        scratch_shapes=[pltpu.VMEM((T, T), f32)] * 4 + EXCHANGE_SEMS,
        compiler_params=_cparams(("arbitrary", "arbitrary")),
    )(k, v, q, d_o, lse2, delta, by_target)


def _bwd_proj(x, proj, d_q, d_k, d_v, d_za, d_cbzc, d_u, d_x2, tabs, g_in, w_in_p, g_cq, w_uq_p, g_ckv, w_ukv_p,
              g_q_p, g_k_p, conv_w8):
    S = x.shape[0]
    T = min(TS_E, S)
    nt = S // T
    PIECE = 512

    def body(x_ref, cqkv_ref, ccx_ref, dq_ref, dk_ref, dv_ref, dza_ref, dcbzc_ref, du_ref, dun_ref, dx2_ref,
             c_ref, s1_ref, s2_ref, g_in_ref, w_in_ref, g_cq_ref, w_uq_ref, g_ckv_ref, w_ukv_ref, g_q_ref, g_k_ref, cw_ref,
             gx_ref, dwin_hbm, dwuq_ref, dwukv_ref, dgin_ref, dgcq_ref, dgckv_ref, dgq_ref, dgk_ref,
             dproj_a, dproj_b, dqraw_sc, dkvraw_sc, dwin_sc, sem):
        i = pl.program_id(0)

        @pl.when(i == 0)
        def _():
            dwin_sc[...] = jnp.zeros(dwin_sc.shape, f32)
            dproj_b[...] = jnp.zeros(dproj_b.shape, MM)
            for r in (dwuq_ref, dwukv_ref, dgin_ref, dgcq_ref, dgckv_ref, dgq_ref, dgk_ref):
                r[...] = jnp.zeros(r.shape, f32)

        def stage1(dproj_sc):
            live = (i < nt).astype(f32)
            c, s1, s2 = c_ref[...], s1_ref[...], s2_ref[...]
            g_q, g_k = g_q_ref[...], g_k_ref[...]
            g_cq, g_ckv = g_cq_ref[...], g_ckv_ref[...]

            c_q = cqkv_ref[:, 0:Q_LORA]
            r_cq = _rstd(c_q, Q_LORA)
            cqh = c_q * r_cq
            n_cq = (cqh * g_cq).astype(MM)
            q_raw = _nn(n_cq, w_uq_ref[...])
            dgq = jnp.zeros((1, HEAD_PAD), f32)
            for hd in range(N_HEADS):
                cols = slice(hd * HEAD_PAD, (hd + 1) * HEAD_PAD)
                qh = q_raw[:, cols]
                rq = _rstd(qh, QK_DIM)
                qhh = qh * rq
                dqh = dq_ref[:, cols] * QK_SCALE
                d_qn = jnp.concatenate([dqh[:, :LANES], _rope_t(dqh[:, LANES:], c, s1, s2)], axis=-1)
                dgq += jnp.sum(d_qn * qhh, axis=0, keepdims=True)
                dqraw_sc[:, cols] = _rms_bwd(d_qn, qhh, rq, g_q, QK_DIM).astype(MM)
                yield
            dgq_ref[...] += dgq * live
            d_qraw = dqraw_sc[...]
            dwuq_ref[...] += _tn((cqh * (g_cq * live)).astype(MM), d_qraw)
            d_ncq = _nt(d_qraw, w_uq_ref[...])
            dgcq_ref[...] += jnp.sum(d_ncq * cqh, axis=0, keepdims=True) * live
            dproj_sc[:, 0:Q_LORA] = _rms_bwd(d_ncq, cqh, r_cq, g_cq, Q_LORA).astype(MM)
            yield

            c_kv = cqkv_ref[:, Q_LORA:Q_LORA + KV_LORA]
            r_ckv = _rstd(c_kv, KV_LORA)
            ckvh = c_kv * r_ckv
            n_ckv = (ckvh * g_ckv).astype(MM)
            k_nope = _nn(n_ckv, w_ukv_ref[:, :N_HEADS * NOPE_DIM])
            kpe = cqkv_ref[:, Q_LORA + KV_LORA:Q_LORA + KV_LORA + LANES]
            ss_pe = jnp.sum(kpe * kpe, axis=-1, keepdims=True)
            g_kn, g_kp = g_k[:, :LANES], g_k[:, LANES:]
            d_kpe = jnp.zeros((T, LANES), f32)
            dgk_n = jnp.zeros((1, LANES), f32)
            dgk_p = jnp.zeros((1, LANES), f32)
            for hd in range(N_HEADS):
                kn = k_nope[:, hd * LANES:(hd + 1) * LANES]
                rk = lax.rsqrt((jnp.sum(kn * kn, axis=-1, keepdims=True) + ss_pe) * (1.0 / QK_DIM) + RMS_EPS)
                knh, kph = kn * rk, kpe * rk
                d_kn_n = dk_ref[:, hd * HEAD_PAD:hd * HEAD_PAD + LANES] * DK_SCALE
                d_kr = _rope_t(dk_ref[:, hd * HEAD_PAD + LANES:(hd + 1) * HEAD_PAD] * DK_SCALE, c, s1, s2)
                dgk_n += jnp.sum(d_kn_n * knh, axis=0, keepdims=True)
                dgk_p += jnp.sum(d_kr * kph, axis=0, keepdims=True)
                u_n, u_p = d_kn_n * g_kn, d_kr * g_kp
                mt = (jnp.sum(u_n * knh, axis=-1, keepdims=True) + jnp.sum(u_p * kph, axis=-1, keepdims=True)) * (1.0 / QK_DIM)
                dkvraw_sc[:, hd * LANES:(hd + 1) * LANES] = (rk * (u_n - knh * mt)).astype(MM)
                d_kpe += rk * (u_p - kph * mt)
                yield
            dgk_ref[:, :LANES] += dgk_n * live
            dgk_ref[:, LANES:] += dgk_p * live
            dkvraw_sc[:, N_HEADS * NOPE_DIM:] = dv_ref[...].astype(MM)
            d_kvraw = dkvraw_sc[...]
            dwukv_ref[...] += _tn((ckvh * (g_ckv * live)).astype(MM), d_kvraw)
            d_nckv = _nt(d_kvraw, w_ukv_ref[...])
            dgckv_ref[...] += jnp.sum(d_nckv * ckvh, axis=0, keepdims=True) * live
            dproj_sc[:, Q_LORA:Q_LORA + KV_LORA] = _rms_bwd(d_nckv, ckvh, r_ckv, g_ckv, KV_LORA).astype(MM)
            dproj_sc[:, Q_LORA + KV_LORA:Q_LORA + KV_LORA + LANES] = d_kpe.astype(MM)
            yield

            dproj_sc[:, 512:1024] = dza_ref[...]
            d_u = du_ref[...]
            nxt = dun_ref[...] * (i < nt - 1).astype(f32)
            d_m = cw_ref[2:3, :] * d_u + cw_ref[1:2, :] * _shift_up(d_u, nxt, 1) + cw_ref[0:1, :] * _shift_up(d_u, nxt, 2)
            dproj_sc[:, 1024:1536] = (d_m * ccx_ref[:, CONV_WIDTH:]).astype(MM)
            dproj_sc[:, 1536:2048] = (d_m * ccx_ref[:, :CONV_WIDTH]).astype(MM)
            dproj_sc[:, 2048:3072] = dcbzc_ref[...]

        def stage2(dproj_sc):
            g_in = g_in_ref[...]
            xv = x_ref[...]
            r_in = _rstd(xv, D_MODEL)
            xh = xv * r_in
            hb = (xh * g_in).astype(MM)
            yield
            d_h = jnp.zeros((T, D_MODEL), f32)
            for cb in range(PROJ_PAD // PIECE):
                cols = slice(cb * PIECE, (cb + 1) * PIECE)
                d_piece = dproj_sc[:, cols]
                dwin_sc[:, cols] += _tn(hb, d_piece)
                d_h = d_h + _nt(d_piece, w_in_ref[:, cols])
                yield
            dgin_ref[...] += jnp.sum(d_h * xh, axis=0, keepdims=True)
            gx_ref[...] = dx2_ref[...] + _rms_bwd(d_h, xh, r_in, g_in, D_MODEL)

        @pl.when(i % 2 == 0)
        def _():
            _alternate(stage2(dproj_b), stage1(dproj_a))

        @pl.when(i % 2 == 1)
        def _():
            _alternate(stage2(dproj_a), stage1(dproj_b))

        @pl.when(i == nt)
        def _():
            cp = pltpu.make_async_copy(dwin_sc, dwin_hbm, sem)
            cp.start()
            cp.wait()

    cur = lambda i: jnp.minimum(i, nt - 1)
    prv = lambda i: jnp.maximum(i - 1, 0)
    row = lambda w, blk=0: pl.BlockSpec((T, w), lambda i: (cur(i), blk))
    row2 = lambda w: pl.BlockSpec((T, w), lambda i: (prv(i), 0))
    acc = lambda shape: pl.BlockSpec(shape, lambda i: (0, 0))
    tb = T // 8
    sds = jax.ShapeDtypeStruct
    return pl.pallas_call(
        body, name="bwd_proj", grid=(nt + 1,),
        out_shape=[sds((S, D_MODEL), f32), sds((D_MODEL, PROJ_PAD), f32), sds((Q_LORA, QK_PAD), f32),
                   sds((KV_LORA, 2 * ATTN_WIDTH), f32), sds((1, D_MODEL), f32), sds((1, Q_LORA), f32), sds((1, KV_LORA), f32),
                   sds((1, HEAD_PAD), f32), sds((1, HEAD_PAD), f32)],
        in_specs=[row2(D_MODEL), row(512, 0), row(2 * CONV_WIDTH, 1), row(QK_PAD), row(QK_PAD), row(ATTN_WIDTH),
                  row(ATTN_WIDTH), row(2 * CONV_WIDTH), row(CONV_WIDTH),
                  pl.BlockSpec((8, CONV_WIDTH), lambda i: (jnp.minimum((cur(i) + 1) * tb, S // 8 - 1), 0)),
                  row2(D_MODEL), row(LANES), row(LANES), row(LANES),
                  _const((1, D_MODEL)), _const((D_MODEL, PROJ_PAD)), _const((1, Q_LORA)), _const((Q_LORA, QK_PAD)),
                  _const((1, KV_LORA)), _const((KV_LORA, 2 * ATTN_WIDTH)), _const((1, HEAD_PAD)), _const((1, HEAD_PAD)),
                  _const((8, CONV_WIDTH))],
        out_specs=[row2(D_MODEL), pl.BlockSpec(memory_space=pl.ANY), acc((Q_LORA, QK_PAD)), acc((KV_LORA, 2 * ATTN_WIDTH)),
                   acc((1, D_MODEL)), acc((1, Q_LORA)), acc((1, KV_LORA)), acc((1, HEAD_PAD)), acc((1, HEAD_PAD))],
        scratch_shapes=[pltpu.VMEM((T, PROJ_PAD), MM), pltpu.VMEM((T, PROJ_PAD), MM), pltpu.VMEM((T, QK_PAD), MM),
                        pltpu.VMEM((T, 2 * ATTN_WIDTH), MM), pltpu.VMEM((D_MODEL, PROJ_PAD), f32), pltpu.SemaphoreType.DMA],
        compiler_params=_cparams(("arbitrary",)),
    )(x, proj, proj, d_q, d_k, d_v, d_za, d_cbzc, d_u, d_u, d_x2, *tabs, g_in, w_in_p, g_cq, w_uq_p, g_ckv, w_ukv_p,
      g_q_p, g_k_p, conv_w8)


def _sibling_exchange(pack, small):
    nchip, _, R, n = pack.shape

    def body(p_ref, s_ref, got_ref, all_ref, send_sems, recv_sems, g_send, g_recv, g_local):
        x, y, c = lax.axis_index("x"), lax.axis_index("y"), lax.axis_index("c")
        start, finish = _two_level_gather(s_ref, all_ref, g_send, g_recv, g_local)
        start()
        copies = [pltpu.make_async_remote_copy(src_ref=p_ref.at[t, 1 - c], dst_ref=got_ref.at[t], send_sem=send_sems.at[t],
                                               recv_sem=recv_sems.at[t], device_id=(x, y, 1 - c), device_id_type=MESH)
                  for t in range(nchip)]
        for cp in copies:
            cp.start()
        finish()
        for cp in copies:
            cp.wait()

    vmem = pl.BlockSpec(memory_space=pltpu.VMEM)
    return pl.pallas_call(
        body, name="rs_sibling",
        out_shape=[jax.ShapeDtypeStruct((nchip, R, n), pack.dtype), jax.ShapeDtypeStruct((N_DEV * small.shape[0], small.shape[1]), small.dtype)],
        in_specs=[pl.BlockSpec(memory_space=pl.ANY), vmem], out_specs=[pl.BlockSpec(memory_space=pl.ANY), vmem],
        scratch_shapes=[pltpu.SemaphoreType.DMA((nchip,)), pltpu.SemaphoreType.DMA((nchip,))] + list(GATHER_SEMS),
    )(pack, small)


def _chip_exchange(part):
    nchip, R, n = part.shape

    def body(p_ref, got_ref, send_sems, recv_sems, local_sem):
        x, y, c = lax.axis_index("x"), lax.axis_index("y"), lax.axis_index("c")
        my_chip = 2 * x + y
        chips = [(1 - x, y), (x, 1 - y), (1 - x, 1 - y)]
        mine = pltpu.make_async_copy(p_ref.at[my_chip], got_ref.at[my_chip], local_sem)
        mine.start()
        sends = []
        for k, (tx, ty) in enumerate(chips):
            cp = pltpu.make_async_remote_copy(src_ref=p_ref.at[2 * tx + ty], dst_ref=got_ref.at[my_chip],
                                              send_sem=send_sems.at[k], recv_sem=recv_sems.at[k],
                                              device_id=(tx, ty, c), device_id_type=MESH)
            cp.start()
            sends.append(cp)
        for k, (sx, sy) in enumerate(chips):
            pltpu.make_async_remote_copy(src_ref=p_ref.at[my_chip], dst_ref=got_ref.at[2 * sx + sy],
                                         send_sem=send_sems.at[k], recv_sem=recv_sems.at[k],
                                         device_id=(sx, sy, c), device_id_type=MESH).wait_recv()
        for cp in sends:
            cp.wait_send()
        mine.wait()

    return pl.pallas_call(
        body, name="rs_chips", out_shape=jax.ShapeDtypeStruct((nchip, R, n), part.dtype),
        in_specs=[pl.BlockSpec(memory_space=pl.ANY)], out_specs=pl.BlockSpec(memory_space=pl.ANY),
        scratch_shapes=[pltpu.SemaphoreType.DMA((3,)), pltpu.SemaphoreType.DMA((3,)), pltpu.SemaphoreType.DMA],
    )(part)


def _add_blocks(pack, from_sib, my_c):
    nb, _, R, n = pack.shape

    def body(c_ref, a_ref, b_ref, o_ref):
        o_ref[...] = (a_ref[0] + b_ref[...]).astype(o_ref.dtype)

    return pl.pallas_call(
        body, name="rs_add", out_shape=jax.ShapeDtypeStruct((nb, R, n), RS_DTYPE),
        grid_spec=pltpu.PrefetchScalarGridSpec(
            num_scalar_prefetch=1, grid=(nb,),
            in_specs=[pl.BlockSpec((1, 1, R, n), lambda i, c: (i, c[0], 0, 0)), pl.BlockSpec((1, R, n), lambda i, c: (i, 0, 0))],
            out_specs=pl.BlockSpec((1, R, n), lambda i, c: (i, 0, 0))),
        compiler_params=_cparams(("parallel",)))(my_c, pack, from_sib)


def _sum_slabs(got, name):
    nb, R, n = got.shape

    def body(g_ref, o_ref):
        tot = g_ref[0].astype(f32)
        for d in range(1, nb):
            tot = tot + g_ref[d].astype(f32)
        o_ref[...] = tot

    return pl.pallas_call(body, name=name, grid=(n // LANES,), out_shape=jax.ShapeDtypeStruct((R, n), f32),
                          in_specs=[pl.BlockSpec((nb, R, LANES), lambda j: (0, 0, j))], out_specs=pl.BlockSpec((R, LANES), lambda j: (0, j)),
                          compiler_params=_cparams(("parallel",)))(got)


def _adamw_all(ws, gs, ms, vs, gain_parts):
    n = len(ws)
    ng = len(GAIN_SLOTS)

    def body(*refs):
        gp_ref = refs[0]
        w_refs = refs[1:1 + n]
        g_refs = refs[1 + n:1 + 2 * n - ng]
        m_refs = refs[1 + 2 * n - ng:1 + 3 * n - ng]
        v_refs = refs[1 + 3 * n - ng:1 + 4 * n - ng]
        outs = refs[1 + 4 * n - ng:]
        gsum_ref, loss_ref = outs[0], outs[1]
        gg_refs = outs[2:2 + ng]
        d_refs, nm_refs, nv_refs = (outs[2 + ng + k * n:2 + ng + (k + 1) * n] for k in range(3))
        tot = gp_ref[0]
        for d in range(1, N_DEV):
            tot = tot + gp_ref[d]
        gsum_ref[...] = tot
        loss_ref[...] = gsum_ref[GAIN_ROWS - 1:GAIN_ROWS, :]
        for k in range(n):
            if k < ng:
                r0, width = GAIN_SLOTS[k]
                for r in range(width // LANES):
                    gg_refs[k][:, r * LANES:(r + 1) * LANES] = gsum_ref[r0 + r:r0 + r + 1, :]
                g = gg_refs[k][...]
            else:
                g = g_refs[k - ng][...]
            w = w_refs[k][...]
            m = ADAM_B1 * m_refs[k][...] + (1.0 - ADAM_B1) * g
            v = ADAM_B2 * v_refs[k][...] + (1.0 - ADAM_B2) * (g * g)
            m_hat = m / (1.0 - ADAM_B1 ** ADAM_STEP)
            v_hat = v / (1.0 - ADAM_B2 ** ADAM_STEP)
            d_refs[k][...] = -ADAM_LR * (m_hat / (jnp.sqrt(v_hat) + ADAM_EPS) + ADAM_WD * w)
            nm_refs[k][...] = m
            nv_refs[k][...] = v

    sds = jax.ShapeDtypeStruct
    like = [sds(w.shape, f32) for w in ws]
    out_shape = [sds((GAIN_ROWS, LANES), f32), sds((1, LANES), f32)] + like[:ng] + like * 3
    full = lambda a: pl.BlockSpec(a.shape, lambda i: (0,) * len(a.shape))
    operands = [gain_parts, *ws, *gs, *ms, *vs]
    outs = pl.pallas_call(
        body, name="adamw", grid=(1,), out_shape=out_shape, in_specs=[full(a) for a in operands],
        out_specs=[full(a) for a in out_shape], compiler_params=_cparams(("arbitrary",)),
    )(*operands)
    loss = outs[1]
    gg = outs[2:2 + ng]
    deltas, new_m, new_v = (outs[2 + ng + k * n:2 + ng + (k + 1) * n] for k in range(3))
    return loss, gg, deltas, new_m, new_v


GAIN_ROWS = 32
GAIN_SLOTS = [(0, 1024), (8, 256), (10, 128), (11, 256), (13, 256), (15, 512), (19, 512), (23, 1024)]

SH_IN = IN_TOTAL // N_DEV
SLAB1_COLS = 384
UQ_ROWS = Q_LORA * (N_HEADS * QK_DIM // N_DEV) // SLAB1_COLS
UKV_ROWS = 48
SLAB1_ROWS = D_MODEL + UQ_ROWS + UKV_ROWS
SH_ROWS = D_MODEL // N_DEV
PL_ROWS = PLE_DIM * (D_MODEL // N_DEV) // D_MODEL
CONV_ROWS = 16
SLAB2_ROWS = 2 * SH_ROWS + PL_ROWS + CONV_ROWS
IN_SEGMENTS = [(0, 448, 0), (448, 960, 512), (1472, 2496, 1024), (960, 1472, 2048), (2496, 3008, 2560)]


def _flat_rows(a, rows, cols):
    lead = a.shape[:-2]
    flat = a.reshape(lead + (-1,))
    pad = [(0, 0)] * len(lead) + [(0, rows * cols - flat.shape[-1])]
    return jnp.pad(flat, pad).reshape(lead + (rows, cols))


def _slab1(in_sh, uq_sh, ukv_sh):
    pad = [(0, 0)] * (in_sh.ndim - 1) + [(0, SLAB1_COLS - in_sh.shape[-1])]
    return jnp.concatenate([jnp.pad(in_sh, pad), _flat_rows(uq_sh, UQ_ROWS, SLAB1_COLS), _flat_rows(ukv_sh, UKV_ROWS, SLAB1_COLS)],
                           axis=-2)


def _slab2(o_sh, plg_sh, pl_sh, conv_sh):
    return jnp.concatenate([o_sh, plg_sh, _flat_rows(pl_sh, PL_ROWS, D_MODEL), _flat_rows(conv_sh, CONV_ROWS, D_MODEL)], axis=-2)


def _in_plan():
    src = [None] * PROJ_PAD
    for a, b, start in IN_SEGMENTS:
        for o in range(a, b):
            src[start + o - a] = (o // SH_IN, o % SH_IN)
    plan = []
    for t in range(PROJ_PAD // LANES):
        runs, j = [], 0
        key_of = lambda jj: None if src[t * LANES + jj] is None else (
            src[t * LANES + jj][0], src[t * LANES + jj][1] // LANES, src[t * LANES + jj][1] % LANES - jj)
        while j < LANES:
            key = key_of(j)
            lo = j
            while j < LANES and key_of(j) == key:
                j += 1
            if key is not None:
                runs.append(key + (lo, j))
        plan.append(runs)
    return plan


def _assemble_w_in(g1):
    plan = _in_plan()
    rows = 256

    def body(g_ref, out_ref):
        src_lane = lax.broadcasted_iota(jnp.int32, (LANES, LANES), 0)
        out_lane = lax.broadcasted_iota(jnp.int32, (LANES, LANES), 1)
        for t, runs in enumerate(plan):
            acc = jnp.zeros((rows, LANES), f32)
            for d, tile, shift, lo, hi in runs:
                pick = ((src_lane == out_lane + shift) & (out_lane >= lo) & (out_lane < hi)).astype(bf16)
                acc = acc + _nn(g_ref[d, :, tile * LANES:(tile + 1) * LANES], pick)
            out_ref[:, t * LANES:(t + 1) * LANES] = acc.astype(out_ref.dtype)

    return pl.pallas_call(
        body, name="w_in_columns", grid=(D_MODEL // rows,), out_shape=jax.ShapeDtypeStruct((D_MODEL, PROJ_PAD), MM),
        in_specs=[pl.BlockSpec((N_DEV, rows, SLAB1_COLS), lambda i: (0, i, 0))],
        out_specs=pl.BlockSpec((rows, PROJ_PAD), lambda i: (i, 0)),
        compiler_params=_cparams(("parallel",)))(g1)


def _by_device(a, width):
    return a.reshape(a.shape[0], N_DEV, width).transpose(1, 0, 2)


def _from_devices(a):
    return a.transpose(1, 0, 2).reshape(a.shape[1], -1)


def kernel(x, p, positions, g_in, w_in, g_cq, w_uq, g_ckv, w_ukv, g_q, g_k, conv_w, g_oa, g_oc, w_o, w_pl, w_plg, g_pl, loss_target, m_g_in, m_w_in, m_g_cq, m_w_uq, m_g_ckv, m_w_ukv, m_g_q, m_g_k, m_conv_w, m_g_oa, m_g_oc, m_w_o, m_w_pl, m_w_plg, m_g_pl, v_g_in, v_w_in, v_g_cq, v_w_uq, v_g_ckv, v_w_ukv, v_g_q, v_g_k, v_conv_w, v_g_oa, v_g_oc, v_w_o, v_w_pl, v_w_plg, v_g_pl):
    S = x.shape[1]
    nd = N_DEV
    xs, ps, tgt = x[0], p[0, 0], loss_target[0]

    slab1 = _slab1(w_in[0], w_uq[0], w_ukv[0]).astype(bf16)
    inv_freq = 1.0 / (ROPE_THETA ** (jnp.arange(0, ROPE_DIM, 2, dtype=f32) / ROPE_DIM))
    inv_row = jnp.concatenate([inv_freq, inv_freq, jnp.zeros((LANES - ROPE_DIM,), f32)]).reshape(1, LANES)
    posf = jnp.broadcast_to(positions[0].astype(f32)[:, None], (S, LANES))
    tabs, g1 = _rope_tables(posf, inv_row, slab1)
    g1 = g1.reshape(nd, SLAB1_ROWS, SLAB1_COLS)
    conv_bits = lax.bitcast_convert_type(conv_w[0], bf16).reshape(CONV_K, -1)
    slab2 = _slab2(w_o[0].astype(bf16), w_plg[0].astype(bf16), w_pl[0].astype(bf16), conv_bits)

    w_in_p = _assemble_w_in(g1)
    wuq = _from_devices(g1[:, D_MODEL:D_MODEL + UQ_ROWS].reshape(nd, Q_LORA, -1)).reshape(Q_LORA, N_HEADS, QK_DIM)
    w_uq_p = jnp.pad(wuq, ((0, 0), (0, 0), (0, HEAD_PAD - QK_DIM))).reshape(Q_LORA, QK_PAD)
    ukv_flat = g1[:, D_MODEL + UQ_ROWS:].reshape(nd, -1)[:, :KV_LORA * V_DIM]
    wukv = _from_devices(ukv_flat.reshape(nd, KV_LORA, V_DIM)).reshape(KV_LORA, N_HEADS, 2, NOPE_DIM)
    w_ukv_p = wukv.transpose(0, 2, 1, 3).reshape(KV_LORA, 2 * ATTN_WIDTH)
    w_in_p, w_uq_p, w_ukv_p = (w.astype(MM) for w in (w_in_p, w_uq_p, w_ukv_p))

    g_q_p = jnp.pad(g_q, ((0, 0), (0, HEAD_PAD - QK_DIM)))
    g_k_p = jnp.pad(g_k, ((0, 0), (0, HEAD_PAD - QK_DIM)))

    proj, q, k, v = _fwd_proj(xs, tabs, g_in, w_in_p, g_cq, w_uq_p, g_ckv, w_ukv_p, g_q_p * SCALE_LOG2E, g_k_p)
    o, lse, g2 = _flash_fwd(q, k, v, slab2)
    w_o_f = g2[:, 0:SH_ROWS].reshape(D_MODEL, D_MODEL).astype(MM)
    w_plg_f = g2[:, SH_ROWS:2 * SH_ROWS].reshape(D_MODEL, D_MODEL).astype(MM)
    w_pl_f = _from_devices(g2[:, 2 * SH_ROWS:2 * SH_ROWS + PL_ROWS].reshape(nd, PLE_DIM, -1)).astype(MM)
    cv_bits = g2[:, 2 * SH_ROWS + PL_ROWS:].reshape(nd, -1)[:, :CONV_K * (CONV_WIDTH // nd) * 2]
    conv_full = _from_devices(lax.bitcast_convert_type(cv_bits.reshape(nd, CONV_K, CONV_WIDTH // nd, 2), f32))
    conv_w8 = jnp.pad(conv_full, ((0, 8 - CONV_K), (0, 0)))
    (d_x2, d_o, delta, d_za, d_cbzc, d_u, dw_o, dw_plg, dw_pl, dcw, dg_oa, dg_oc, dg_pl, loss_part) = _mid(
        o, proj, xs, ps, tgt, w_o_f, w_plg_f, w_pl_f, conv_w8, g_oa, g_oc, g_pl)

    slabs_a = _slab2(dw_o.reshape(nd, SH_ROWS, D_MODEL), dw_plg.reshape(nd, SH_ROWS, D_MODEL), _by_device(dw_pl, D_MODEL // nd),
                     _by_device(dcw[:CONV_K], CONV_WIDTH // nd)).astype(RS_DTYPE)
    d_q, d_k, d_v, got_a = _flash_bwd(q, k, v, d_o, lse, delta, slabs_a)
    shard_a = _sum_slabs(got_a, "rs_sum8")
    (grad_x, dw_in_p, dw_uq_p, dw_ukv_p, dg_in, dg_cq, dg_ckv, dg_q_p, dg_k_p) = _bwd_proj(
        xs, proj, d_q, d_k, d_v, d_za, d_cbzc, d_u, d_x2, tabs, g_in, w_in_p, g_cq, w_uq_p, g_ckv, w_ukv_p, g_q_p, g_k_p, conv_w8)

    def dw_in_cols(a, b):
        return [dw_in_p[:, start + max(a, s) - s:start + min(b, e) - s] for s, e, start in sorted(IN_SEGMENTS) if max(a, s) < min(b, e)]

    no_cols = jnp.zeros((D_MODEL, SLAB1_COLS - SH_IN), f32)
    in_shards = jnp.stack([jnp.concatenate(dw_in_cols(SH_IN * d, SH_IN * (d + 1)) + [no_cols], axis=1) for d in range(nd)])
    dw_uq = dw_uq_p.reshape(Q_LORA, N_HEADS, HEAD_PAD)[:, :, :QK_DIM].reshape(Q_LORA, N_HEADS * QK_DIM)
    dw_ukv = dw_ukv_p.reshape(KV_LORA, 2, N_HEADS, NOPE_DIM).transpose(0, 2, 1, 3).reshape(KV_LORA, 2 * ATTN_WIDTH)
    slabs_b = _slab1(in_shards, _by_device(dw_uq, N_HEADS * QK_DIM // nd), _by_device(dw_ukv, 2 * ATTN_WIDTH // nd))
    slabs_b = slabs_b.reshape(4, 2, SLAB1_ROWS, SLAB1_COLS)
    my_c = lax.axis_index("c").astype(jnp.int32).reshape(1)
    gains_part = [dg_in, dg_cq, dg_ckv, dg_q_p, dg_k_p, dg_oa, dg_oc, dg_pl]
    gflat = jnp.concatenate([g.reshape(-1) for g in gains_part] + [loss_part.reshape(-1)])
    from_sib, gain_parts = _sibling_exchange(slabs_b, gflat.reshape(GAIN_ROWS, LANES))
    gain_parts = gain_parts.reshape(nd, GAIN_ROWS, LANES)
    chip_part = _add_blocks(slabs_b, from_sib, my_c)
    shard_b = _sum_slabs(_chip_exchange(chip_part), "rs_sum")
    grads_w = [shard_b[:D_MODEL, :SH_IN], shard_b[D_MODEL:D_MODEL + UQ_ROWS].reshape(Q_LORA, -1),
               shard_b[D_MODEL + UQ_ROWS:].reshape(-1)[:KV_LORA * V_DIM].reshape(KV_LORA, V_DIM),
               shard_a[2 * SH_ROWS + PL_ROWS:].reshape(-1)[:CONV_K * CONV_WIDTH // nd].reshape(CONV_K, -1),
               shard_a[0:SH_ROWS], shard_a[2 * SH_ROWS:2 * SH_ROWS + PL_ROWS].reshape(PLE_DIM, -1), shard_a[SH_ROWS:2 * SH_ROWS]]

    gains = [g_in, g_cq, g_ckv, g_q_p, g_k_p, g_oa, g_oc, g_pl]
    padq = lambda a: jnp.pad(a, ((0, 0), (0, HEAD_PAD - QK_DIM)))
    m_gains = [m_g_in, m_g_cq, m_g_ckv, padq(m_g_q), padq(m_g_k), m_g_oa, m_g_oc, m_g_pl]
    v_gains = [v_g_in, v_g_cq, v_g_ckv, padq(v_g_q), padq(v_g_k), v_g_oa, v_g_oc, v_g_pl]
    tr = lambda a: jnp.swapaxes(a, 1, 2)
    turned = lambda l: [tr(l[0]), tr(l[1])] + list(l[2:])
    ws = gains + turned([w_in, w_uq, w_ukv, conv_w, w_o, w_pl, w_plg])
    ms = m_gains + turned([m_w_in, m_w_uq, m_w_ukv, m_conv_w, m_w_o, m_w_pl, m_w_plg])
    vs = v_gains + turned([v_w_in, v_w_uq, v_w_ukv, v_conv_w, v_w_o, v_w_pl, v_w_plg])
    loss_row, gg, deltas, new_m, new_v = _adamw_all(ws, turned([g[None] for g in grads_w]), ms, vs, gain_parts)
    grads_w = [g[None] for g in grads_w]
    ng = len(gains)
    deltas, new_m, new_v = (list(l[:ng]) + turned(l[ng:]) for l in (deltas, new_m, new_v))
    loss = loss_row[0, 0]

    def ordered(gl, wl):
        g_in_, g_cq_, g_ckv_, g_q_, g_k_, g_oa_, g_oc_, g_pl_ = gl
        g_q_, g_k_ = g_q_[:, :QK_DIM], g_k_[:, :QK_DIM]
        w_in_, w_uq_, w_ukv_, cw_, w_o_, w_pl_, w_plg_ = wl
        return [g_in_, w_in_, g_cq_, w_uq_, g_ckv_, w_ukv_, g_q_, g_k_, cw_, g_oa_, g_oc_, w_o_, w_pl_, w_plg_, g_pl_]

    ng = len(gains)
    outs = [loss, grad_x[None]]
    outs += ordered(gg, grads_w)
    for lst in (deltas, new_m, new_v):
        outs += ordered(lst[:ng], lst[ng:])
    return tuple(outs)
```
